```python
import jax, jax.numpy as jnp
from jax import lax
import numpy as np

D_MODEL = 2048
BATCH = 8
SEQ = 4096
DEPTH = 1

CONV_WIDTH = D_MODEL // 2
CONV_GROUPS = 8
CONV_GROUP_DIM = CONV_WIDTH // CONV_GROUPS
CONV_KERNEL = 31
HGRN_WIDTH = D_MODEL - CONV_WIDTH
HGRN_HEADS = 8
HGRN_EXPAND = HGRN_WIDTH // HGRN_HEADS
HGRN_HEAD_DIM = HGRN_WIDTH // HGRN_HEADS
HGRN_CHUNK = 64
HGRN_SUB = 8
IN_PROJ_DIM = 2 * CONV_WIDTH + 4 * HGRN_WIDTH
D_FF = 5632
FFN_KERNEL = 3
LN_EPS = 1e-5
RMS_EPS = 1e-6
ALPHA = (2.0 * DEPTH) ** 0.25
BETA = (8.0 * DEPTH) ** -0.25

kernel_name = "hymba_conformer_hgrn2_deepnorm"


def layer_norm(x, g, b):
    xf = x.astype(jnp.float32)
    mu = jnp.mean(xf, axis=-1, keepdims=True)
    var = jnp.mean(jnp.square(xf - mu), axis=-1, keepdims=True)
    y = (xf - mu) * lax.rsqrt(var + LN_EPS)
    return (y * g.astype(jnp.float32) + b.astype(jnp.float32)).astype(x.dtype)


def causal_dwconv(x, w, b):
    k = w.shape[0]
    c = x.shape[-1]
    y = lax.conv_general_dilated(
        x, w[:, None, :].astype(x.dtype), window_strides=(1,), padding=[(k - 1, 0)],
        dimension_numbers=("NWC", "WIO", "NWC"), feature_group_count=c)
    return y + b.astype(x.dtype)


def hgrn2_recurrence(q, k, v, log_f):
    B, T, H, N = q.shape
    Dv = v.shape[-1]
    C, L = HGRN_CHUNK, HGRN_SUB
    nc, ns = T // C, C // L

    def blocks(t):
        return t.astype(jnp.float32).reshape(B, nc, ns, L, H, -1).transpose(0, 4, 1, 2, 3, 5)

    qb, kb, vb = blocks(q), blocks(k), blocks(v)
    gb = blocks(log_f)
    cum = jnp.cumsum(gb.reshape(B, H, nc, C, N), axis=3)
    bb = cum.reshape(B, H, nc, ns, L, N)
    bend = bb[..., -1, :]

    idx = jnp.arange(ns)
    blk_lower = idx[:, None] > idx[None, :]
    e_off = bb[:, :, :, :, None, :, :] - bend[:, :, :, None, :, None, :]
    q_off = qb[:, :, :, :, None] * jnp.exp(jnp.where(blk_lower[:, :, None, None], e_off, -jnp.inf))
    k_off = kb * jnp.exp(bend[..., None, :] - bb)
    a_off = jnp.einsum("bhcijtn,bhcjsn->bhcijts", q_off, k_off)
    o_off = jnp.einsum("bhcijts,bhcjsd->bhcitd", a_off, vb)

    pos = jnp.arange(L)
    causal = pos[:, None] >= pos[None, :]
    e_d = bb[..., :, None, :] - bb[..., None, :, :]
    decay = jnp.exp(jnp.where(causal[:, :, None], e_d, -jnp.inf))
    a_d = jnp.einsum("bhcitn,bhcitsn,bhcisn->bhcits", qb, decay, kb)
    o_d = jnp.einsum("bhcits,bhcisd->bhcitd", a_d, vb)
    o_intra = (o_off + o_d).reshape(B, H, nc, C, Dv)

    qc = qb.reshape(B, H, nc, C, N)
    kc = kb.reshape(B, H, nc, C, N)
    vc = vb.reshape(B, H, nc, C, Dv)
    blast = cum[..., -1, :]
    q_in = qc * jnp.exp(cum)
    k_up = kc * jnp.exp(blast[..., None, :] - cum)

    def step(S, xs):
        q_i, k_i, v_i, bl_i = xs
        o_i = jnp.einsum("bhtn,bhnd->bhtd", q_i, S)
        S = S * jnp.exp(bl_i)[..., None] + jnp.einsum("bhtn,bhtd->bhnd", k_i, v_i)
        return S, o_i

    xs = (jnp.moveaxis(q_in, 2, 0), jnp.moveaxis(k_up, 2, 0),
          jnp.moveaxis(vc, 2, 0), jnp.moveaxis(blast, 2, 0))
    S0 = jnp.zeros((B, H, N, Dv), jnp.float32)
    _, o_inter = lax.scan(step, S0, xs)
    o = o_intra + jnp.moveaxis(o_inter, 0, 2)
    return o.reshape(B, H, T, Dv).transpose(0, 2, 1, 3)


def hybrid_mixer(x, w_in, conv_w, conv_b, conv_norm_g, conv_norm_b, lb_logits, hgrn_norm_g, w_out, layer):
    B, T, _ = x.shape
    h = x @ w_in
    c1, c2 = CONV_WIDTH, 2 * CONV_WIDTH
    a, gate = h[..., :c1], h[..., c1:c2]
    q, f, i, og = (h[..., c2 + n * HGRN_WIDTH: c2 + (n + 1) * HGRN_WIDTH] for n in range(4))

    u = a * jax.nn.sigmoid(gate)
    u = causal_dwconv(u, conv_w, conv_b)
    u = layer_norm(u.reshape(B, T, CONV_GROUPS, CONV_GROUP_DIM),
                   conv_norm_g.reshape(CONV_GROUPS, CONV_GROUP_DIM),
                   conv_norm_b.reshape(CONV_GROUPS, CONV_GROUP_DIM)).reshape(B, T, CONV_WIDTH)
    u = jax.nn.silu(u)

    lb_table = jnp.cumsum(jax.nn.softmax(lb_logits.astype(jnp.float32), axis=0), axis=0)
    lb = lb_table[layer]
    fg = lb + (1.0 - lb) * jax.nn.sigmoid(f.astype(jnp.float32))
    log_f = jnp.log(fg)
    kk = 1.0 - fg
    qh = jax.nn.silu(q.astype(jnp.float32))
    shp = (B, T, HGRN_HEADS, HGRN_EXPAND)
    o = hgrn2_recurrence(qh.reshape(shp), kk.reshape(shp),
                         i.reshape(B, T, HGRN_HEADS, HGRN_HEAD_DIM), log_f.reshape(shp))
    o = o * lax.rsqrt(jnp.mean(jnp.square(o), axis=-1, keepdims=True) + RMS_EPS)
    o = o.reshape(B, T, HGRN_WIDTH) * hgrn_norm_g.astype(jnp.float32)
    o = (o * jax.nn.silu(og.astype(jnp.float32))).astype(x.dtype)

    return jnp.concatenate([u, o], axis=-1) @ w_out


def conv_ffn(x, w_up, conv_w, conv_b, w_down):
    h = x @ w_up
    g, v = h[..., :D_FF], h[..., D_FF:]
    g = causal_dwconv(g, conv_w, conv_b)
    return (jax.nn.silu(g) * v) @ w_down


def _fwd_setup_inputs(seed: int = 0) -> dict:
    key = jax.random.key(seed)
    ks = jax.random.split(key, 20)
    f32 = jnp.float32
    nrm = lambda k, s: jax.random.normal(k, s, f32)
    return {
        "x": nrm(ks[0], (BATCH, SEQ, D_MODEL)),
        "emb_ln_g": 1.0 + 0.02 * nrm(ks[1], (D_MODEL,)),
        "emb_ln_b": 0.02 * nrm(ks[2], (D_MODEL,)),
        "w_in": nrm(ks[3], (DEPTH, D_MODEL, IN_PROJ_DIM)) * D_MODEL ** -0.5,
        "conv_w": nrm(ks[4], (DEPTH, CONV_KERNEL, CONV_WIDTH)) * CONV_KERNEL ** -0.5,
        "conv_b": 0.01 * nrm(ks[5], (DEPTH, CONV_WIDTH)),
        "conv_norm_g": 1.0 + 0.02 * nrm(ks[6], (DEPTH, CONV_WIDTH)),
        "conv_norm_b": 0.02 * nrm(ks[7], (DEPTH, CONV_WIDTH)),
        "lb_logits": 0.5 * nrm(ks[8], (DEPTH + 1, HGRN_WIDTH)),
        "hgrn_norm_g": 1.0 + 0.02 * nrm(ks[9], (DEPTH, HGRN_WIDTH)),
        "w_out": nrm(ks[10], (DEPTH, D_MODEL, D_MODEL)) * (D_MODEL ** -0.5) * BETA,
        "ln1_g": 1.0 + 0.02 * nrm(ks[11], (DEPTH, D_MODEL)),
        "ln1_b": 0.02 * nrm(ks[12], (DEPTH, D_MODEL)),
        "w_ffn_up": nrm(ks[13], (DEPTH, D_MODEL, 2 * D_FF)) * D_MODEL ** -0.5,
        "ffn_conv_w": nrm(ks[14], (DEPTH, FFN_KERNEL, D_FF)) * FFN_KERNEL ** -0.5,
        "ffn_conv_b": 0.01 * nrm(ks[15], (DEPTH, D_FF)),
        "w_ffn_down": nrm(ks[16], (DEPTH, D_FF, D_MODEL)) * (D_FF ** -0.5) * BETA,
        "ln2_g": 1.0 + 0.02 * nrm(ks[17], (DEPTH, D_MODEL)),
        "ln2_b": 0.02 * nrm(ks[18], (DEPTH, D_MODEL)),
    }


def _fwd_reference(x, emb_ln_g, emb_ln_b, w_in, conv_w, conv_b, conv_norm_g, conv_norm_b, lb_logits,
              hgrn_norm_g, w_out, ln1_g, ln1_b, w_ffn_up, ffn_conv_w, ffn_conv_b, w_ffn_down,
              ln2_g, ln2_b):
    h = layer_norm(x, emb_ln_g, emb_ln_b)
    for l in range(DEPTH):
        mix = hybrid_mixer(h, w_in[l], conv_w[l], conv_b[l], conv_norm_g[l], conv_norm_b[l],
                           lb_logits, hgrn_norm_g[l], w_out[l], l)
        h = layer_norm(ALPHA * h + mix, ln1_g[l], ln1_b[l])
        ffn = conv_ffn(h, w_ffn_up[l], ffn_conv_w[l], ffn_conv_b[l], w_ffn_down[l])
        h = layer_norm(ALPHA * h + ffn, ln2_g[l], ln2_b[l])
    return h


import jax as _jax
import jax.numpy as _jnp

TWIN_FORMAT = 'train_step'
FWD_PARAMS = ['x', 'emb_ln_g', 'emb_ln_b', 'w_in', 'conv_w', 'conv_b', 'conv_norm_g', 'conv_norm_b', 'lb_logits', 'hgrn_norm_g', 'w_out', 'ln1_g', 'ln1_b', 'w_ffn_up', 'ffn_conv_w', 'ffn_conv_b', 'w_ffn_down', 'ln2_g', 'ln2_b']
TWIN_WEIGHTS = ['emb_ln_g', 'emb_ln_b', 'w_in', 'conv_w', 'conv_b', 'conv_norm_g', 'conv_norm_b', 'lb_logits', 'hgrn_norm_g', 'w_out', 'ln1_g', 'ln1_b', 'w_ffn_up', 'ffn_conv_w', 'ffn_conv_b', 'w_ffn_down', 'ln2_g', 'ln2_b']
TWIN_DIFF_INPUT = 'x'
TWIN_INPUTS = ['x', 'emb_ln_g', 'emb_ln_b', 'w_in', 'conv_w', 'conv_b', 'conv_norm_g', 'conv_norm_b', 'lb_logits', 'hgrn_norm_g', 'w_out', 'ln1_g', 'ln1_b', 'w_ffn_up', 'ffn_conv_w', 'ffn_conv_b', 'w_ffn_down', 'ln2_g', 'ln2_b', 'loss_target', 'm_emb_ln_g', 'm_emb_ln_b', 'm_w_in', 'm_conv_w', 'm_conv_b', 'm_conv_norm_g', 'm_conv_norm_b', 'm_lb_logits', 'm_hgrn_norm_g', 'm_w_out', 'm_ln1_g', 'm_ln1_b', 'm_w_ffn_up', 'm_ffn_conv_w', 'm_ffn_conv_b', 'm_w_ffn_down', 'm_ln2_g', 'm_ln2_b', 'v_emb_ln_g', 'v_emb_ln_b', 'v_w_in', 'v_conv_w', 'v_conv_b', 'v_conv_norm_g', 'v_conv_norm_b', 'v_lb_logits', 'v_hgrn_norm_g', 'v_w_out', 'v_ln1_g', 'v_ln1_b', 'v_w_ffn_up', 'v_ffn_conv_w', 'v_ffn_conv_b', 'v_w_ffn_down', 'v_ln2_g', 'v_ln2_b']
TWIN_OUTPUTS = ['loss', 'grad_x', 'grad_emb_ln_g', 'grad_emb_ln_b', 'grad_w_in', 'grad_conv_w', 'grad_conv_b', 'grad_conv_norm_g', 'grad_conv_norm_b', 'grad_lb_logits', 'grad_hgrn_norm_g', 'grad_w_out', 'grad_ln1_g', 'grad_ln1_b', 'grad_w_ffn_up', 'grad_ffn_conv_w', 'grad_ffn_conv_b', 'grad_w_ffn_down', 'grad_ln2_g', 'grad_ln2_b', 'delta_emb_ln_g', 'delta_emb_ln_b', 'delta_w_in', 'delta_conv_w', 'delta_conv_b', 'delta_conv_norm_g', 'delta_conv_norm_b', 'delta_lb_logits', 'delta_hgrn_norm_g', 'delta_w_out', 'delta_ln1_g', 'delta_ln1_b', 'delta_w_ffn_up', 'delta_ffn_conv_w', 'delta_ffn_conv_b', 'delta_w_ffn_down', 'delta_ln2_g', 'delta_ln2_b', 'new_m_emb_ln_g', 'new_m_emb_ln_b', 'new_m_w_in', 'new_m_conv_w', 'new_m_conv_b', 'new_m_conv_norm_g', 'new_m_conv_norm_b', 'new_m_lb_logits', 'new_m_hgrn_norm_g', 'new_m_w_out', 'new_m_ln1_g', 'new_m_ln1_b', 'new_m_w_ffn_up', 'new_m_ffn_conv_w', 'new_m_ffn_conv_b', 'new_m_w_ffn_down', 'new_m_ln2_g', 'new_m_ln2_b', 'new_v_emb_ln_g', 'new_v_emb_ln_b', 'new_v_w_in', 'new_v_conv_w', 'new_v_conv_b', 'new_v_conv_norm_g', 'new_v_conv_norm_b', 'new_v_lb_logits', 'new_v_hgrn_norm_g', 'new_v_w_out', 'new_v_ln1_g', 'new_v_ln1_b', 'new_v_w_ffn_up', 'new_v_ffn_conv_w', 'new_v_ffn_conv_b', 'new_v_w_ffn_down', 'new_v_ln2_g', 'new_v_ln2_b']
TWIN_LEAF_KINDS = {'loss': 'loss', 'grad_x': 'grad_x', 'grad_emb_ln_g': 'grad_w', 'grad_emb_ln_b': 'grad_w', 'grad_w_in': 'grad_w', 'grad_conv_w': 'grad_w', 'grad_conv_b': 'grad_w', 'grad_conv_norm_g': 'grad_w', 'grad_conv_norm_b': 'grad_w', 'grad_lb_logits': 'grad_w', 'grad_hgrn_norm_g': 'grad_w', 'grad_w_out': 'grad_w', 'grad_ln1_g': 'grad_w', 'grad_ln1_b': 'grad_w', 'grad_w_ffn_up': 'grad_w', 'grad_ffn_conv_w': 'grad_w', 'grad_ffn_conv_b': 'grad_w', 'grad_w_ffn_down': 'grad_w', 'grad_ln2_g': 'grad_w', 'grad_ln2_b': 'grad_w', 'delta_emb_ln_g': 'delta_w', 'delta_emb_ln_b': 'delta_w', 'delta_w_in': 'delta_w', 'delta_conv_w': 'delta_w', 'delta_conv_b': 'delta_w', 'delta_conv_norm_g': 'delta_w', 'delta_conv_norm_b': 'delta_w', 'delta_lb_logits': 'delta_w', 'delta_hgrn_norm_g': 'delta_w', 'delta_w_out': 'delta_w', 'delta_ln1_g': 'delta_w', 'delta_ln1_b': 'delta_w', 'delta_w_ffn_up': 'delta_w', 'delta_ffn_conv_w': 'delta_w', 'delta_ffn_conv_b': 'delta_w', 'delta_w_ffn_down': 'delta_w', 'delta_ln2_g': 'delta_w', 'delta_ln2_b': 'delta_w', 'new_m_emb_ln_g': 'new_m', 'new_m_emb_ln_b': 'new_m', 'new_m_w_in': 'new_m', 'new_m_conv_w': 'new_m', 'new_m_conv_b': 'new_m', 'new_m_conv_norm_g': 'new_m', 'new_m_conv_norm_b': 'new_m', 'new_m_lb_logits': 'new_m', 'new_m_hgrn_norm_g': 'new_m', 'new_m_w_out': 'new_m', 'new_m_ln1_g': 'new_m', 'new_m_ln1_b': 'new_m', 'new_m_w_ffn_up': 'new_m', 'new_m_ffn_conv_w': 'new_m', 'new_m_ffn_conv_b': 'new_m', 'new_m_w_ffn_down': 'new_m', 'new_m_ln2_g': 'new_m', 'new_m_ln2_b': 'new_m', 'new_v_emb_ln_g': 'new_v', 'new_v_emb_ln_b': 'new_v', 'new_v_w_in': 'new_v', 'new_v_conv_w': 'new_v', 'new_v_conv_b': 'new_v', 'new_v_conv_norm_g': 'new_v', 'new_v_conv_norm_b': 'new_v', 'new_v_lb_logits': 'new_v', 'new_v_hgrn_norm_g': 'new_v', 'new_v_w_out': 'new_v', 'new_v_ln1_g': 'new_v', 'new_v_ln1_b': 'new_v', 'new_v_w_ffn_up': 'new_v', 'new_v_ffn_conv_w': 'new_v', 'new_v_ffn_conv_b': 'new_v', 'new_v_w_ffn_down': 'new_v', 'new_v_ln2_g': 'new_v', 'new_v_ln2_b': 'new_v'}


def _forward(args):
    return _fwd_reference(*[args[k] for k in FWD_PARAMS])


def _output_shape():
    def fwd():
        inp = _fwd_setup_inputs(0)
        return _fwd_reference(*[inp[k] for k in FWD_PARAMS])
    out = _jax.eval_shape(fwd)
    return out.shape, out.dtype

N_MICROBATCH = 1
ADAM_LR = 0.001
ADAM_B1 = 0.9
ADAM_B2 = 0.999
ADAM_EPS = 1e-08
ADAM_WD = 0.01
ADAM_STEP = 10
PER_EXAMPLE_BATCH_AXIS = {'x': 0, 'loss_target': 0}
SHARED_INPUTS = []
_WEIGHT_DTYPES = {'emb_ln_g': _jnp.float32, 'emb_ln_b': _jnp.float32, 'w_in': _jnp.float32, 'conv_w': _jnp.float32, 'conv_b': _jnp.float32, 'conv_norm_g': _jnp.float32, 'conv_norm_b': _jnp.float32, 'lb_logits': _jnp.float32, 'hgrn_norm_g': _jnp.float32, 'w_out': _jnp.float32, 'ln1_g': _jnp.float32, 'ln1_b': _jnp.float32, 'w_ffn_up': _jnp.float32, 'ffn_conv_w': _jnp.float32, 'ffn_conv_b': _jnp.float32, 'w_ffn_down': _jnp.float32, 'ln2_g': _jnp.float32, 'ln2_b': _jnp.float32}
MOMENT_SCALE = {'emb_ln_g': 5.026938e-01, 'emb_ln_b': 2.759378e-01, 'w_in': 1.977406e-02, 'conv_w': 2.821207e-02, 'conv_b': 1.164590e-01, 'conv_norm_g': 5.284407e-02, 'conv_norm_b': 6.930514e-02, 'lb_logits': 2.332952e-03, 'hgrn_norm_g': 2.747335e-02, 'w_out': 5.158434e-02, 'ln1_g': 5.488700e-01, 'ln1_b': 2.843881e-01, 'w_ffn_up': 1.592210e-02, 'ffn_conv_w': 1.627515e-02, 'ffn_conv_b': 1.575284e-02, 'w_ffn_down': 4.380216e-02, 'ln2_g': 1.600782e+01, 'ln2_b': 1.167475e+00}


def _to_microbatches(a, axis):
    t = _jnp.moveaxis(a, axis, 0)
    t = t.reshape((N_MICROBATCH, t.shape[0] // N_MICROBATCH) + t.shape[1:])
    return _jnp.moveaxis(t, 1, axis + 1)


def setup_inputs(seed: int = 0) -> dict:
    inp = _fwd_setup_inputs(seed)
    key = _jax.random.fold_in(_jax.random.key(seed), 7919)
    shape, _ = _output_shape()
    out = dict(inp)
    out["loss_target"] = _jax.random.normal(_jax.random.fold_in(key, 0), shape, _jnp.float32)
    for i, name in enumerate(TWIN_WEIGHTS):
        w = inp[name].astype(_jnp.float32)
        if MOMENT_SCALE is None:
            s = _jnp.sqrt(_jnp.mean(_jnp.square(w)) + 1e-30)
        else:
            s = MOMENT_SCALE[name]
        km, kv = _jax.random.split(_jax.random.fold_in(key, i + 1))
        out[name] = w
        out["m_" + name] = s * _jax.random.normal(km, w.shape, _jnp.float32)
        out["v_" + name] = (s * s) * _jax.random.uniform(kv, w.shape, _jnp.float32, 0.5, 1.5)
    if N_MICROBATCH > 1:
        for name, axis in PER_EXAMPLE_BATCH_AXIS.items():
            out[name] = _to_microbatches(out[name], axis)
    return {'x': out['x'], 'emb_ln_g': out['emb_ln_g'], 'emb_ln_b': out['emb_ln_b'], 'w_in': out['w_in'], 'conv_w': out['conv_w'], 'conv_b': out['conv_b'], 'conv_norm_g': out['conv_norm_g'], 'conv_norm_b': out['conv_norm_b'], 'lb_logits': out['lb_logits'], 'hgrn_norm_g': out['hgrn_norm_g'], 'w_out': out['w_out'], 'ln1_g': out['ln1_g'], 'ln1_b': out['ln1_b'], 'w_ffn_up': out['w_ffn_up'], 'ffn_conv_w': out['ffn_conv_w'], 'ffn_conv_b': out['ffn_conv_b'], 'w_ffn_down': out['w_ffn_down'], 'ln2_g': out['ln2_g'], 'ln2_b': out['ln2_b'], 'loss_target': out['loss_target'], 'm_emb_ln_g': out['m_emb_ln_g'], 'm_emb_ln_b': out['m_emb_ln_b'], 'm_w_in': out['m_w_in'], 'm_conv_w': out['m_conv_w'], 'm_conv_b': out['m_conv_b'], 'm_conv_norm_g': out['m_conv_norm_g'], 'm_conv_norm_b': out['m_conv_norm_b'], 'm_lb_logits': out['m_lb_logits'], 'm_hgrn_norm_g': out['m_hgrn_norm_g'], 'm_w_out': out['m_w_out'], 'm_ln1_g': out['m_ln1_g'], 'm_ln1_b': out['m_ln1_b'], 'm_w_ffn_up': out['m_w_ffn_up'], 'm_ffn_conv_w': out['m_ffn_conv_w'], 'm_ffn_conv_b': out['m_ffn_conv_b'], 'm_w_ffn_down': out['m_w_ffn_down'], 'm_ln2_g': out['m_ln2_g'], 'm_ln2_b': out['m_ln2_b'], 'v_emb_ln_g': out['v_emb_ln_g'], 'v_emb_ln_b': out['v_emb_ln_b'], 'v_w_in': out['v_w_in'], 'v_conv_w': out['v_conv_w'], 'v_conv_b': out['v_conv_b'], 'v_conv_norm_g': out['v_conv_norm_g'], 'v_conv_norm_b': out['v_conv_norm_b'], 'v_lb_logits': out['v_lb_logits'], 'v_hgrn_norm_g': out['v_hgrn_norm_g'], 'v_w_out': out['v_w_out'], 'v_ln1_g': out['v_ln1_g'], 'v_ln1_b': out['v_ln1_b'], 'v_w_ffn_up': out['v_w_ffn_up'], 'v_ffn_conv_w': out['v_ffn_conv_w'], 'v_ffn_conv_b': out['v_ffn_conv_b'], 'v_w_ffn_down': out['v_w_ffn_down'], 'v_ln2_g': out['v_ln2_g'], 'v_ln2_b': out['v_ln2_b']}


def _loss(weights, diff, rest, loss_target):
    with _jax.named_scope("forward"):
        args = {**rest, TWIN_DIFF_INPUT: diff, **{k: w.astype(_WEIGHT_DTYPES[k]) for k, w in weights.items()}}
        y = _forward(args)
    with _jax.named_scope("loss_head"):
        err = _jnp.square(y.astype(_jnp.float32) - loss_target)
        return 0.5 * _jnp.sum(_jnp.mean(err, axis=-1)) if err.ndim else 0.5 * err


def _adamw(w, g, m, v):
    m = ADAM_B1 * m + (1.0 - ADAM_B1) * g
    v = ADAM_B2 * v + (1.0 - ADAM_B2) * _jnp.square(g)
    m_hat = m / (1.0 - ADAM_B1 ** ADAM_STEP)
    v_hat = v / (1.0 - ADAM_B2 ** ADAM_STEP)
    delta = -ADAM_LR * (m_hat / (_jnp.sqrt(v_hat) + ADAM_EPS) + ADAM_WD * w)
    return delta, m, v


def reference(x, emb_ln_g, emb_ln_b, w_in, conv_w, conv_b, conv_norm_g, conv_norm_b, lb_logits, hgrn_norm_g, w_out, ln1_g, ln1_b, w_ffn_up, ffn_conv_w, ffn_conv_b, w_ffn_down, ln2_g, ln2_b, loss_target, m_emb_ln_g, m_emb_ln_b, m_w_in, m_conv_w, m_conv_b, m_conv_norm_g, m_conv_norm_b, m_lb_logits, m_hgrn_norm_g, m_w_out, m_ln1_g, m_ln1_b, m_w_ffn_up, m_ffn_conv_w, m_ffn_conv_b, m_w_ffn_down, m_ln2_g, m_ln2_b, v_emb_ln_g, v_emb_ln_b, v_w_in, v_conv_w, v_conv_b, v_conv_norm_g, v_conv_norm_b, v_lb_logits, v_hgrn_norm_g, v_w_out, v_ln1_g, v_ln1_b, v_w_ffn_up, v_ffn_conv_w, v_ffn_conv_b, v_w_ffn_down, v_ln2_g, v_ln2_b):
    given = dict(x=x, emb_ln_g=emb_ln_g, emb_ln_b=emb_ln_b, w_in=w_in, conv_w=conv_w, conv_b=conv_b, conv_norm_g=conv_norm_g, conv_norm_b=conv_norm_b, lb_logits=lb_logits, hgrn_norm_g=hgrn_norm_g, w_out=w_out, ln1_g=ln1_g, ln1_b=ln1_b, w_ffn_up=w_ffn_up, ffn_conv_w=ffn_conv_w, ffn_conv_b=ffn_conv_b, w_ffn_down=w_ffn_down, ln2_g=ln2_g, ln2_b=ln2_b, loss_target=loss_target, m_emb_ln_g=m_emb_ln_g, m_emb_ln_b=m_emb_ln_b, m_w_in=m_w_in, m_conv_w=m_conv_w, m_conv_b=m_conv_b, m_conv_norm_g=m_conv_norm_g, m_conv_norm_b=m_conv_norm_b, m_lb_logits=m_lb_logits, m_hgrn_norm_g=m_hgrn_norm_g, m_w_out=m_w_out, m_ln1_g=m_ln1_g, m_ln1_b=m_ln1_b, m_w_ffn_up=m_w_ffn_up, m_ffn_conv_w=m_ffn_conv_w, m_ffn_conv_b=m_ffn_conv_b, m_w_ffn_down=m_w_ffn_down, m_ln2_g=m_ln2_g, m_ln2_b=m_ln2_b, v_emb_ln_g=v_emb_ln_g, v_emb_ln_b=v_emb_ln_b, v_w_in=v_w_in, v_conv_w=v_conv_w, v_conv_b=v_conv_b, v_conv_norm_g=v_conv_norm_g, v_conv_norm_b=v_conv_norm_b, v_lb_logits=v_lb_logits, v_hgrn_norm_g=v_hgrn_norm_g, v_w_out=v_w_out, v_ln1_g=v_ln1_g, v_ln1_b=v_ln1_b, v_w_ffn_up=v_w_ffn_up, v_ffn_conv_w=v_ffn_conv_w, v_ffn_conv_b=v_ffn_conv_b, v_w_ffn_down=v_w_ffn_down, v_ln2_g=v_ln2_g, v_ln2_b=v_ln2_b)
    weights = {n: given[n] for n in TWIN_WEIGHTS}
    shared = {n: given[n] for n in SHARED_INPUTS}
    per_example = {n: given[n] for n in ['x']}
    grad_fn = _jax.value_and_grad(_loss, argnums=(0, 1))

    def one_microbatch(ex, loss_target):
        ex = dict(ex)
        diff = ex.pop(TWIN_DIFF_INPUT)
        return grad_fn(weights, diff, {**shared, **ex}, loss_target)

    if N_MICROBATCH == 1:
        loss, (grad_w, grad_x) = one_microbatch(per_example, given["loss_target"])
    else:
        def body(carry, xs):
            loss_sum, grad_sum = carry
            l_k, (gw_k, gx_k) = one_microbatch(xs[0], xs[1])
            with _jax.named_scope("update"):
                return (loss_sum + l_k, _jax.tree.map(_jnp.add, grad_sum, gw_k)), gx_k

        init = (_jnp.zeros((), _jnp.float32), _jax.tree.map(_jnp.zeros_like, weights))
        (loss, grad_w), grad_x = _jax.lax.scan(body, init, (per_example, given["loss_target"]))
    with _jax.named_scope("update"):
        delta_w, new_m, new_v = {}, {}, {}
        for n in TWIN_WEIGHTS:
            delta_w[n], new_m[n], new_v[n] = _adamw(weights[n], grad_w[n], given["m_" + n], given["v_" + n])
    return (loss, grad_x, *[grad_w[n] for n in TWIN_WEIGHTS], *[delta_w[n] for n in TWIN_WEIGHTS],
            *[new_m[n] for n in TWIN_WEIGHTS], *[new_v[n] for n in TWIN_WEIGHTS])
```

```python
import functools

import jax
import jax.numpy as jnp
from jax import lax
from jax.experimental import pallas as pl
from jax.experimental.pallas import tpu as pltpu

F32 = jnp.float32
BF16 = jnp.bfloat16

LN_EPS = 1e-5
RMS_EPS = 1e-6
LANE = 128
SUBLANE = 8
CHUNK = 64
SUB = 8
HALO = 32
FHALO = 8
ROWS = 64
N_CHIPS = 4
VMEM_LIMIT = 56 << 20
NEG_BIG = -1e30

ADAM_LR = 0.001
ADAM_B1 = 0.9
ADAM_B2 = 0.999
ADAM_EPS = 1e-08
ADAM_WD = 0.01
ADAM_STEP = 10

MESH = pl.DeviceIdType.MESH
HBM = pl.BlockSpec(memory_space=pl.ANY)
VMEM_FULL = pl.BlockSpec(memory_space=pltpu.VMEM)


def _params(*sem):
    return pltpu.CompilerParams(dimension_semantics=sem, vmem_limit_bytes=VMEM_LIMIT)


def _div_tile(n, mult, cap):
    best = n
    for t in range(mult, min(n, cap) + 1, mult):
        if n % t == 0:
            best = t
    return best


def _sigmoid(x):
    return 1.0 / (1.0 + jnp.exp(-x))


def _ln_stats(x):
    mu = jnp.mean(x, axis=-1, keepdims=True)
    xc = x - mu
    var = jnp.mean(xc * xc, axis=-1, keepdims=True)
    rstd = lax.rsqrt(var + LN_EPS)
    return xc * rstd, rstd


def _ln_bwd(dy, xhat, rstd, g):
    dyg = dy * g
    m1 = jnp.mean(dyg, axis=-1, keepdims=True)
    m2 = jnp.mean(dyg * xhat, axis=-1, keepdims=True)
    return rstd * (dyg - m1 - xhat * m2)


def _dot_nt(a, b):
    return lax.dot_general(a, b, (((1,), (1,)), ((), ())), preferred_element_type=F32)


def _dot_tn(a, b):
    return lax.dot_general(a, b, (((0,), (0,)), ((), ())), preferred_element_type=F32)


def _dot(a, b):
    return jnp.dot(a, b, preferred_element_type=F32)


def _dot3(m, x):
    mb = m.astype(BF16)
    x1 = x.astype(BF16)
    r1 = x - x1.astype(F32)
    x2 = r1.astype(BF16)
    x3 = (r1 - x2.astype(F32)).astype(BF16)
    return _dot(mb, x1) + _dot(mb, x2) + _dot(mb, x3)


def _cast_bf16(x, name):
    r, c = x.shape
    tr = _div_tile(r, 16, 512)

    def body(x_ref, o_ref):
        o_ref[...] = x_ref[...].astype(BF16)

    return pl.pallas_call(
        body, name=name, grid=(r // tr,),
        in_specs=[pl.BlockSpec((tr, c), lambda i: (i, 0))],
        out_specs=pl.BlockSpec((tr, c), lambda i: (i, 0)),
        out_shape=jax.ShapeDtypeStruct((r, c), BF16),
        compiler_params=_params("parallel"),
    )(x)


def _ln0(x, g, b, tm):
    t, d = x.shape

    def body(x_ref, g_ref, b_ref, o_ref):
        xh, _ = _ln_stats(x_ref[...])
        o_ref[...] = (xh * g_ref[...] + b_ref[...]).astype(BF16)

    row = pl.BlockSpec((1, d), lambda i: (0, 0))
    return pl.pallas_call(
        body, name="ln0", grid=(t // tm,),
        in_specs=[pl.BlockSpec((tm, d), lambda i: (i, 0)), row, row],
        out_specs=pl.BlockSpec((tm, d), lambda i: (i, 0)),
        out_shape=jax.ShapeDtypeStruct((t, d), BF16),
        compiler_params=_params("parallel"),
    )(x, g, b)


def _proj(name, a, w3, n_sec, tm, tn):
    m, k = a.shape
    s, _, ws = w3.shape
    sec_w = s * ws // n_sec
    nj = ws // tn
    per_sec = sec_w // tn

    def body(a_ref, w_ref, o_ref):
        o_ref[...] = _dot(a_ref[...], w_ref[...])

    return pl.pallas_call(
        body, name=name, grid=(s * nj, m // tm),
        in_specs=[pl.BlockSpec((tm, k), lambda j, i: (i, 0)),
                  pl.BlockSpec((None, k, tn), lambda j, i: (j // nj, 0, j % nj))],
        out_specs=pl.BlockSpec((None, tm, tn), lambda j, i: (j // per_sec, i, j % per_sec)),
        out_shape=jax.ShapeDtypeStruct((n_sec, m, sec_w), F32),
        compiler_params=_params("parallel", "parallel"),
    )(a, w3)


def _proj_t(name, a, w, tm, tn):
    m, k = a.shape
    n = w.shape[0]

    def body(a_ref, w_ref, o_ref):
        o_ref[...] = _dot_nt(a_ref[...], w_ref[...])

    return pl.pallas_call(
        body, name=name, grid=(n // tn, m // tm),
        in_specs=[pl.BlockSpec((tm, k), lambda j, i: (i, 0)),
                  pl.BlockSpec((tn, k), lambda j, i: (j, 0))],
        out_specs=pl.BlockSpec((tm, tn), lambda j, i: (i, j)),
        out_shape=jax.ShapeDtypeStruct((m, n), F32),
        compiler_params=_params("parallel", "parallel"),
    )(a, w)


def _wgrad(name, a, b, out_shape, grid, a_spec, b_spec, o_spec):
    nt = len(grid) - 1

    def body(a_ref, b_ref, o_ref):
        t = pl.program_id(nt)
        prod = _dot_tn(a_ref[...], b_ref[...])

        @pl.when(t == 0)
        def _():
            o_ref[...] = prod

        @pl.when(t > 0)
        def _():
            o_ref[...] += prod

    return pl.pallas_call(
        body, name=name, grid=grid, in_specs=[a_spec, b_spec], out_specs=o_spec,
        out_shape=jax.ShapeDtypeStruct(out_shape, F32),
        compiler_params=_params(*(["parallel"] * nt + ["arbitrary"])),
    )(a, b)


def _mix_ln1(cat, w_out, x, g0, b0, g1, b1, alpha, tm):
    t, d = x.shape

    def body(cat_ref, w_ref, x_ref, g0_ref, b0_ref, g1_ref, b1_ref, xh_ref, h1b_ref, rstd_ref):
        mix = _dot(cat_ref[...], w_ref[...])
        xh0, _ = _ln_stats(x_ref[...])
        z1 = alpha * (xh0 * g0_ref[...] + b0_ref[...]) + mix
        xh1, rstd1 = _ln_stats(z1)
        xh_ref[...] = xh1
        h1b_ref[...] = (xh1 * g1_ref[...] + b1_ref[...]).astype(BF16)
        rstd_ref[...] = rstd1

    row = pl.BlockSpec((1, d), lambda i: (0, 0))
    blk = pl.BlockSpec((tm, d), lambda i: (i, 0))
    return pl.pallas_call(
        body, name="mix_ln1", grid=(t // tm,),
        in_specs=[blk, pl.BlockSpec((d, d), lambda i: (0, 0)), blk, row, row, row, row],
        out_specs=[blk, blk, pl.BlockSpec((tm, 1), lambda i: (i, 0))],
        out_shape=[jax.ShapeDtypeStruct((t, d), F32), jax.ShapeDtypeStruct((t, d), BF16),
                   jax.ShapeDtypeStruct((t, 1), F32)],
        compiler_params=_params("parallel"),
    )(cat, w_out, x, g0, b0, g1, b1)


def _down_ln2_loss(act, w_down3, xhat1, tgt, g1, b1, g2, b2, alpha, tm):
    t, d = xhat1.shape
    s, ks, _ = w_down3.shape
    ni = t // tm
    inv_d = 1.0 / d

    def body(act_ref, w_ref, xh1_ref, tgt_ref, g1_ref, b1_ref, g2_ref, b2_ref,
             dz2_ref, dz2b_ref, dg2_ref, db2_ref, loss_ref, acc, lrow):
        i, k = pl.program_id(0), pl.program_id(1)
        prod = _dot(act_ref[...], w_ref[...])

        @pl.when(k == 0)
        def _():
            acc[...] = prod

        @pl.when(k > 0)
        def _():
            acc[...] += prod

        @pl.when(k == s - 1)
        def _():
            h1 = xh1_ref[...] * g1_ref[...] + b1_ref[...]
            xh2, rstd2 = _ln_stats(alpha * h1 + acc[...])
            g2v = g2_ref[...]
            diff = xh2 * g2v + b2_ref[...] - tgt_ref[...]
            dh2 = diff * inv_d
            sq = jnp.sum(diff * diff, axis=0, keepdims=True)
            dg = jnp.sum(dh2 * xh2, axis=0, keepdims=True)
            db = jnp.sum(dh2, axis=0, keepdims=True)

            @pl.when(i == 0)
            def _():
                lrow[...] = sq
                dg2_ref[...] = dg
                db2_ref[...] = db

            @pl.when(i > 0)
            def _():
                lrow[...] += sq
                dg2_ref[...] += dg
                db2_ref[...] += db

            dz2 = _ln_bwd(dh2, xh2, rstd2, g2v)
            dz2_ref[...] = dz2
            dz2b_ref[...] = dz2.astype(BF16)

            @pl.when(i == ni - 1)
            def _():
                tot = jnp.sum(lrow[...], axis=-1, keepdims=True) * (0.5 * inv_d)
                loss_ref[...] = jnp.broadcast_to(tot, (1, LANE))

    row = pl.BlockSpec((1, d), lambda i, k: (0, 0))
    blk = pl.BlockSpec((tm, d), lambda i, k: (i, 0))
    return pl.pallas_call(
        body, name="down_ln2_loss", grid=(ni, s),
        in_specs=[pl.BlockSpec((tm, ks), lambda i, k: (i, k)),
                  pl.BlockSpec((None, ks, d), lambda i, k: (k, 0, 0)),
                  blk, blk, row, row, row, row],
        out_specs=[blk, blk, row, row, pl.BlockSpec((1, LANE), lambda i, k: (0, 0))],
        out_shape=[jax.ShapeDtypeStruct((t, d), F32), jax.ShapeDtypeStruct((t, d), BF16),
                   jax.ShapeDtypeStruct((1, d), F32), jax.ShapeDtypeStruct((1, d), F32),
                   jax.ShapeDtypeStruct((1, LANE), F32)],
        scratch_shapes=[pltpu.VMEM((tm, d), F32), pltpu.VMEM((1, d), F32)],
        compiler_params=_params("arbitrary", "arbitrary"),
    )(act, w_down3, xhat1, tgt, g1, b1, g2, b2)


def _up_t_ln1_bwd(dhh3, w_up3, dz2, xhat1, rstd1, g1, alpha, tm):
    t, d = dz2.shape
    s, _, ws = w_up3.shape
    tk = ws // 2
    nk = 2 * s
    per_sec = dhh3.shape[2] // tk

    def body(a_ref, w_ref, dz2_ref, xh_ref, rstd_ref, g_ref, dz1_ref, dz1b_ref, dg_ref, db_ref, acc):
        i, k = pl.program_id(0), pl.program_id(1)
        prod = _dot_nt(a_ref[...], w_ref[...])

        @pl.when(k == 0)
        def _():
            acc[...] = prod

        @pl.when(k > 0)
        def _():
            acc[...] += prod

        @pl.when(k == nk - 1)
        def _():
            dh1 = alpha * dz2_ref[...] + acc[...]
            xh = xh_ref[...]
            dg = jnp.sum(dh1 * xh, axis=0, keepdims=True)
            db = jnp.sum(dh1, axis=0, keepdims=True)

            @pl.when(i == 0)
            def _():
                dg_ref[...] = dg
                db_ref[...] = db

            @pl.when(i > 0)
            def _():
                dg_ref[...] += dg
                db_ref[...] += db

            dz1 = _ln_bwd(dh1, xh, rstd_ref[...], g_ref[...])
            dz1_ref[...] = dz1
            dz1b_ref[...] = dz1.astype(BF16)

    row = pl.BlockSpec((1, d), lambda i, k: (0, 0))
    blk = pl.BlockSpec((tm, d), lambda i, k: (i, 0))
    return pl.pallas_call(
        body, name="up_t_ln1_bwd", grid=(t // tm, nk),
        in_specs=[pl.BlockSpec((None, tm, tk), lambda i, k: (k // per_sec, i, k % per_sec)),
                  pl.BlockSpec((None, d, tk), lambda i, k: (k // 2, 0, k % 2)),
                  blk, blk, pl.BlockSpec((tm, 1), lambda i, k: (i, 0)), row],
        out_specs=[blk, blk, row, row],
        out_shape=[jax.ShapeDtypeStruct((t, d), F32), jax.ShapeDtypeStruct((t, d), BF16),
                   jax.ShapeDtypeStruct((1, d), F32), jax.ShapeDtypeStruct((1, d), F32)],
        scratch_shapes=[pltpu.VMEM((tm, d), F32)],
        compiler_params=_params("arbitrary", "arbitrary"),
    )(dhh3, w_up3, dz2, xhat1, rstd1, g1)


def _in_t_ln0_bwd(dp3, w_in3, dz1, x, g0, alpha, tm):
    t, d = x.shape
    s, _, ws = w_in3.shape
    n_sec, _, sec_w = dp3.shape
    tk = sec_w // 2
    nk = n_sec * 2
    per_shard = ws // tk

    def body(a_ref, w_ref, dz1_ref, x_ref, g_ref, dx_ref, dg_ref, db_ref, acc):
        i, k = pl.program_id(0), pl.program_id(1)
        prod = _dot_nt(a_ref[...], w_ref[...])

        @pl.when(k == 0)
        def _():
            acc[...] = prod

        @pl.when(k > 0)
        def _():
            acc[...] += prod

        @pl.when(k == nk - 1)
        def _():
            dh0 = alpha * dz1_ref[...] + acc[...]
            xh, rstd = _ln_stats(x_ref[...])
            dg = jnp.sum(dh0 * xh, axis=0, keepdims=True)
            db = jnp.sum(dh0, axis=0, keepdims=True)

            @pl.when(i == 0)
            def _():
                dg_ref[...] = dg
                db_ref[...] = db

            @pl.when(i > 0)
            def _():
                dg_ref[...] += dg
                db_ref[...] += db

            dx_ref[...] = _ln_bwd(dh0, xh, rstd, g_ref[...])

    row = pl.BlockSpec((1, d), lambda i, k: (0, 0))
    blk = pl.BlockSpec((tm, d), lambda i, k: (i, 0))
    return pl.pallas_call(
        body, name="in_t_ln0_bwd", grid=(t // tm, nk),
        in_specs=[pl.BlockSpec((None, tm, tk), lambda i, k: (k // 2, i, k % 2)),
                  pl.BlockSpec((None, d, tk), lambda i, k: (k // per_shard, 0, k % per_shard)),
                  blk, blk, row],
        out_specs=[blk, row, row],
        out_shape=[jax.ShapeDtypeStruct((t, d), F32), jax.ShapeDtypeStruct((1, d), F32),
                   jax.ShapeDtypeStruct((1, d), F32)],
        scratch_shapes=[pltpu.VMEM((tm, d), F32)],
        compiler_params=_params("arbitrary", "arbitrary"),
    )(dp3, w_in3, dz1, x, g0)


def _conv_fwd(p3, conv_w, conv_b, cn_g, cn_b, tc, cb):
    _, t, w = p3.shape
    kk = conv_w.shape[0]
    off = HALO - (kk - 1)
    hb = tc // HALO

    def body(a_ref, g_ref, ap_ref, gp_ref, w_ref, b_ref, ng_ref, nb_ref, cat_ref, u1_ref, ext):
        i = pl.program_id(1)
        ext[pl.ds(HALO, tc), :] = a_ref[...] * _sigmoid(g_ref[...])
        prev = ap_ref[...] * _sigmoid(gp_ref[...])
        ext[pl.ds(0, HALO), :] = jnp.where(i > 0, prev, 0.0)
        for r in range(tc // ROWS):
            acc = jnp.broadcast_to(b_ref[...], (ROWS, cb))
            for k in range(kk):
                acc = acc + w_ref[k:k + 1, :] * ext[pl.ds(r * ROWS + off + k, ROWS), :]
            u1_ref[pl.ds(r * ROWS, ROWS), :] = acc
            for g in range(cb // LANE):
                sl = slice(g * LANE, (g + 1) * LANE)
                xh, _ = _ln_stats(acc[:, sl])
                u2 = xh * ng_ref[:, sl] + nb_ref[:, sl]
                cat_ref[pl.ds(r * ROWS, ROWS), sl] = (u2 * _sigmoid(u2)).astype(BF16)

    cur = lambda sec: pl.BlockSpec((None, tc, cb), lambda j, i: (sec, i, j))
    prev = lambda sec: pl.BlockSpec((None, HALO, cb), lambda j, i: (sec, jnp.maximum(i * hb - 1, 0), j))
    row = pl.BlockSpec((1, cb), lambda j, i: (0, j))
    return pl.pallas_call(
        body, name="conv_fwd", grid=(w // cb, t // tc),
        in_specs=[cur(0), cur(1), prev(0), prev(1), pl.BlockSpec((kk, cb), lambda j, i: (0, j)), row, row, row],
        out_specs=[pl.BlockSpec((tc, cb), lambda j, i: (i, j)), pl.BlockSpec((tc, cb), lambda j, i: (i, j))],
        out_shape=[jax.ShapeDtypeStruct((t, 2 * w), BF16), jax.ShapeDtypeStruct((t, w), F32)],
        scratch_shapes=[pltpu.VMEM((tc + HALO, cb), F32)],
        compiler_params=_params("parallel", "arbitrary"),
    )(p3, p3, p3, p3, conv_w, conv_b, cn_g, cn_b)


def _conv_norm_bwd(dcat, u1, cn_g, cn_b, tc):
    t, w = u1.shape

    def body(du_ref, u1_ref, ng_ref, nb_ref, du1_ref, dg_ref, db_ref):
        i = pl.program_id(0)
        for g in range(w // LANE):
            sl = slice(g * LANE, (g + 1) * LANE)
            ng = ng_ref[:, sl]
            xh, rstd = _ln_stats(u1_ref[:, sl])
            u2 = xh * ng + nb_ref[:, sl]
            sg = _sigmoid(u2)
            du2 = du_ref[:, sl] * (sg * (1.0 + u2 * (1.0 - sg)))
            dg = jnp.sum(du2 * xh, axis=0, keepdims=True)
            db = jnp.sum(du2, axis=0, keepdims=True)

            @pl.when(i == 0)
            def _():
                dg_ref[:, sl] = dg
                db_ref[:, sl] = db

            @pl.when(i > 0)
            def _():
                dg_ref[:, sl] += dg
                db_ref[:, sl] += db

            du1_ref[:, sl] = _ln_bwd(du2, xh, rstd, ng)

    row = pl.BlockSpec((1, w), lambda i: (0, 0))
    blk = pl.BlockSpec((tc, w), lambda i: (i, 0))
    return pl.pallas_call(
        body, name="conv_norm_bwd", grid=(t // tc,),
        in_specs=[blk, blk, row, row], out_specs=[blk, row, row],
        out_shape=[jax.ShapeDtypeStruct((t, w), F32), jax.ShapeDtypeStruct((1, w), F32),
                   jax.ShapeDtypeStruct((1, w), F32)],
        compiler_params=_params("arbitrary"),
    )(dcat, u1, cn_g, cn_b)


def _conv_bwd(du1, p3, conv_w, tc, cb):
    n_sec, t, w = p3.shape
    kk = conv_w.shape[0]
    off = HALO - (kk - 1)
    hb = tc // HALO
    nt = t // tc
    kpad = -(-kk // SUBLANE) * SUBLANE

    def body(d_ref, dn_ref, a_ref, g_ref, ap_ref, gp_ref, w_ref, dp_ref, dw_ref, db_ref, extd, extu, wacc, bacc):
        i = pl.program_id(1)

        @pl.when(i == 0)
        def _():
            wacc[...] = jnp.zeros_like(wacc)
            bacc[...] = jnp.zeros_like(bacc)

        extd[pl.ds(0, tc), :] = d_ref[...]
        extd[pl.ds(tc, HALO), :] = jnp.where(i < nt - 1, dn_ref[...], 0.0)
        extu[pl.ds(HALO, tc), :] = a_ref[...] * _sigmoid(g_ref[...])
        extu[pl.ds(0, HALO), :] = jnp.where(i > 0, ap_ref[...] * _sigmoid(gp_ref[...]), 0.0)
        for r in range(tc // ROWS):
            rows = pl.ds(r * ROWS, ROWS)
            acc = jnp.zeros((ROWS, cb), F32)
            for k in range(kk):
                acc = acc + w_ref[k:k + 1, :] * extd[pl.ds(r * ROWS + (kk - 1) - k, ROWS), :]
            a = a_ref[rows, :]
            sg = _sigmoid(g_ref[rows, :])
            dp_ref[0, rows, :] = (acc * sg).astype(BF16)
            dp_ref[1, rows, :] = (acc * a * sg * (1.0 - sg)).astype(BF16)
            d = d_ref[rows, :]
            bacc[...] += jnp.sum(d.reshape(ROWS // SUBLANE, SUBLANE, cb), axis=0)
            for k in range(kk):
                prod = d * extu[pl.ds(r * ROWS + off + k, ROWS), :]
                wacc[k] += jnp.sum(prod.reshape(ROWS // SUBLANE, SUBLANE, cb), axis=0)

        @pl.when(i == nt - 1)
        def _():
            for k in range(kk):
                dw_ref[k:k + 1, :] = jnp.sum(wacc[k], axis=0, keepdims=True)
            if kpad > kk:
                dw_ref[kk:kpad, :] = jnp.zeros((kpad - kk, cb), F32)
            db_ref[...] = jnp.sum(bacc[...], axis=0, keepdims=True)

    cur = lambda sec: pl.BlockSpec((None, tc, cb), lambda j, i: (sec, i, j))
    prev = lambda sec: pl.BlockSpec((None, HALO, cb), lambda j, i: (sec, jnp.maximum(i * hb - 1, 0), j))
    return pl.pallas_call(
        body, name="conv_bwd", grid=(w // cb, nt),
        in_specs=[pl.BlockSpec((tc, cb), lambda j, i: (i, j)),
                  pl.BlockSpec((HALO, cb), lambda j, i: (jnp.minimum((i + 1) * hb, t // HALO - 1), j)),
                  cur(0), cur(1), prev(0), prev(1), pl.BlockSpec((kk, cb), lambda j, i: (0, j))],
        out_specs=[pl.BlockSpec((2, tc, cb), lambda j, i: (0, i, j)),
                   pl.BlockSpec((kpad, cb), lambda j, i: (0, j)),
                   pl.BlockSpec((1, cb), lambda j, i: (0, j))],
        out_shape=[jax.ShapeDtypeStruct((n_sec, t, w), BF16), jax.ShapeDtypeStruct((kpad, w), F32),
                   jax.ShapeDtypeStruct((1, w), F32)],
        scratch_shapes=[pltpu.VMEM((tc + HALO, cb), F32), pltpu.VMEM((tc + HALO, cb), F32),
                        pltpu.VMEM((kk, SUBLANE, cb), F32), pltpu.VMEM((SUBLANE, cb), F32)],
        compiler_params=_params("parallel", "arbitrary"),
    )(du1, du1, p3, p3, p3, p3, conv_w)


def _ffn_act_fwd(hh3, fw, fb, tc, cb):
    _, t, dff = hh3.shape
    kk = fw.shape[0]
    off = FHALO - (kk - 1)
    hb = tc // FHALO

    def body(g_ref, v_ref, gp_ref, w_ref, b_ref, act_ref, ext):
        i = pl.program_id(1)
        ext[pl.ds(FHALO, tc), :] = g_ref[...]
        ext[pl.ds(0, FHALO), :] = jnp.where(i > 0, gp_ref[...], 0.0)
        gc = jnp.broadcast_to(b_ref[...], (tc, cb))
        for k in range(kk):
            gc = gc + w_ref[k:k + 1, :] * ext[pl.ds(off + k, tc), :]
        act_ref[...] = (gc * _sigmoid(gc) * v_ref[...]).astype(BF16)

    return pl.pallas_call(
        body, name="ffn_act_fwd", grid=(dff // cb, t // tc),
        in_specs=[pl.BlockSpec((None, tc, cb), lambda j, i: (0, i, j)),
                  pl.BlockSpec((None, tc, cb), lambda j, i: (1, i, j)),
                  pl.BlockSpec((None, FHALO, cb), lambda j, i: (0, jnp.maximum(i * hb - 1, 0), j)),
                  pl.BlockSpec((kk, cb), lambda j, i: (0, j)),
                  pl.BlockSpec((1, cb), lambda j, i: (0, j))],
        out_specs=pl.BlockSpec((tc, cb), lambda j, i: (i, j)),
        out_shape=jax.ShapeDtypeStruct((t, dff), BF16),
        scratch_shapes=[pltpu.VMEM((tc + FHALO, cb), F32)],
        compiler_params=_params("parallel", "arbitrary"),
    )(hh3, hh3, hh3, fw, fb)


def _ffn_act_bwd(dact, hh3, fw, fb, tc, cb):
    _, t, dff = hh3.shape
    kk = fw.shape[0]
    off = FHALO - (kk - 1)
    hb = tc // FHALO
    nt = t // tc
    te = tc + FHALO

    def body(da_ref, dan_ref, g_ref, gp_ref, gn_ref, v_ref, vn_ref, w_ref, b_ref,
             dhh_ref, dw_ref, db_ref, gext, dext, wacc, bacc):
        i = pl.program_id(1)

        @pl.when(i == 0)
        def _():
            wacc[...] = jnp.zeros_like(wacc)
            bacc[...] = jnp.zeros_like(bacc)

        gext[pl.ds(0, FHALO), :] = jnp.where(i > 0, gp_ref[...], 0.0)
        gext[pl.ds(FHALO, tc), :] = g_ref[...]
        gext[pl.ds(FHALO + tc, FHALO), :] = gn_ref[...]
        gc = jnp.broadcast_to(b_ref[...], (te, cb))
        for k in range(kk):
            gc = gc + w_ref[k:k + 1, :] * gext[pl.ds(off + k, te), :]
        sg = _sigmoid(gc)
        dsilu = sg * (1.0 + gc * (1.0 - sg))
        live = i < nt - 1
        da_cur = da_ref[...]
        dext[pl.ds(0, tc), :] = da_cur * v_ref[...] * dsilu[0:tc]
        dext[pl.ds(tc, FHALO), :] = jnp.where(live, dan_ref[...] * vn_ref[...] * dsilu[tc:te], 0.0)
        dhh_ref[1] = (da_cur * (gc[0:tc] * sg[0:tc])).astype(BF16)
        dg = jnp.zeros((tc, cb), F32)
        for k in range(kk):
            dg = dg + w_ref[k:k + 1, :] * dext[pl.ds((kk - 1) - k, tc), :]
        dhh_ref[0] = dg.astype(BF16)
        dgc = dext[pl.ds(0, tc), :]
        bacc[...] += jnp.sum(dgc.reshape(tc // SUBLANE, SUBLANE, cb), axis=0)
        for k in range(kk):
            prod = dgc * gext[pl.ds(off + k, tc), :]
            wacc[k] += jnp.sum(prod.reshape(tc // SUBLANE, SUBLANE, cb), axis=0)

        @pl.when(i == nt - 1)
        def _():
            for k in range(kk):
                dw_ref[k:k + 1, :] = jnp.sum(wacc[k], axis=0, keepdims=True)
            dw_ref[kk:SUBLANE, :] = jnp.zeros((SUBLANE - kk, cb), F32)
            db_ref[...] = jnp.sum(bacc[...], axis=0, keepdims=True)

    nxt = lambda i: jnp.minimum((i + 1) * hb, t // FHALO - 1)
    return pl.pallas_call(
        body, name="ffn_act_bwd", grid=(dff // cb, nt),
        in_specs=[pl.BlockSpec((tc, cb), lambda j, i: (i, j)),
                  pl.BlockSpec((FHALO, cb), lambda j, i: (nxt(i), j)),
                  pl.BlockSpec((None, tc, cb), lambda j, i: (0, i, j)),
                  pl.BlockSpec((None, FHALO, cb), lambda j, i: (0, jnp.maximum(i * hb - 1, 0), j)),
                  pl.BlockSpec((None, FHALO, cb), lambda j, i: (0, nxt(i), j)),
                  pl.BlockSpec((None, tc, cb), lambda j, i: (1, i, j)),
                  pl.BlockSpec((None, FHALO, cb), lambda j, i: (1, nxt(i), j)),
                  pl.BlockSpec((kk, cb), lambda j, i: (0, j)),
                  pl.BlockSpec((1, cb), lambda j, i: (0, j))],
        out_specs=[pl.BlockSpec((2, tc, cb), lambda j, i: (0, i, j)),
                   pl.BlockSpec((SUBLANE, cb), lambda j, i: (0, j)),
                   pl.BlockSpec((1, cb), lambda j, i: (0, j))],
        out_shape=[jax.ShapeDtypeStruct((2, t, dff), BF16), jax.ShapeDtypeStruct((SUBLANE, dff), F32),
                   jax.ShapeDtypeStruct((1, dff), F32)],
        scratch_shapes=[pltpu.VMEM((tc + 2 * FHALO, cb), F32), pltpu.VMEM((te, cb), F32),
                        pltpu.VMEM((kk, SUBLANE, cb), F32), pltpu.VMEM((SUBLANE, cb), F32)],
        compiler_params=_params("parallel", "arbitrary"),
    )(dact, dact, hh3, hh3, hh3, hh3, hh3, fw, fb)


def _chunk_consts():
    r = lax.broadcasted_iota(jnp.int32, (CHUNK, CHUNK), 0)
    c = lax.broadcasted_iota(jnp.int32, (CHUNK, CHUNK), 1)
    blk = (r // SUB) * SUB
    tri = (c <= r).astype(F32)
    start = (c < blk).astype(F32)
    end = (c < blk + SUB).astype(F32)
    return jnp.concatenate([tri, start, end, jnp.ones((SUBLANE, CHUNK), F32)], axis=0)


def _gate_terms(q, fpre, lb):
    sf = _sigmoid(fpre)
    fg = lb + (1.0 - lb) * sf
    sq = _sigmoid(q)
    return sf, fg, 1.0 - fg, sq, q * sq


def _decays(g, consts):
    cs = _dot3(consts, g)
    b = cs[0:CHUNK]
    rs = cs[CHUNK:2 * CHUNK]
    re = cs[2 * CHUNK:3 * CHUNK]
    tot = cs[3 * CHUNK:3 * CHUNK + 1]
    return b, rs, re, tot


def _lower_bound(lb_ref):
    l0, l1 = lb_ref[0:1, :], lb_ref[1:2, :]
    mx = jnp.maximum(l0, l1)
    e0, e1 = jnp.exp(l0 - mx), jnp.exp(l1 - mx)
    return e0 / (e0 + e1)


def _scaled_keys(kt, rs, re, rowblk, i):
    scale = jnp.where(rowblk < i, jnp.exp(jnp.minimum(rs[SUB * i:SUB * i + 1, :] - re, 0.0)), 0.0)
    return kt * scale, scale


def _off_diag_scores(qt, kt, rs, re, rowblk):
    a = jnp.zeros((CHUNK, CHUNK), F32)
    for i in range(1, CHUNK // SUB):
        ki, _ = _scaled_keys(kt, rs, re, rowblk, i)
        a = a + _dot_nt(jnp.where(rowblk == i, qt, 0.0).astype(BF16), ki.astype(BF16))
    return a


def _hgrn_fwd(p3, lb_logits, hg, cat, tb):
    _, t, w = p3.shape
    nh = w // LANE
    nc = tb // CHUNK

    def body(q_ref, f_ref, v_ref, og_ref, lb_ref, hg_ref, cat_in, cat_ref, o_ref, st_ref, state):
        del cat_in
        consts = _chunk_consts()
        lb = _lower_bound(lb_ref)
        hgv = hg_ref[...]
        rowblk = lax.broadcasted_iota(jnp.int32, (CHUNK, 1), 0) // SUB
        rowpos = lax.broadcasted_iota(jnp.int32, (CHUNK, 1), 0) % SUB

        @pl.when(pl.program_id(1) == 0)
        def _():
            state[...] = jnp.zeros_like(state)

        def chunk(c, carry):
            rows = pl.ds(pl.multiple_of(c * CHUNK, CHUNK), CHUNK)
            v = v_ref[rows, :]
            og = og_ref[rows, :]
            _, fg, kk, _, qh = _gate_terms(q_ref[rows, :], f_ref[rows, :], lb)
            b, rs, re, tot = _decays(jnp.log(fg), consts)
            qt = qh * jnp.exp(b - rs)
            kt = kk * jnp.exp(re - b)
            vb = v.astype(BF16)
            st = state[...]
            st_ref[c] = st
            o = _dot(_off_diag_scores(qt, kt, rs, re, rowblk).astype(BF16), vb)
            o = o + _dot_nt((qh * jnp.exp(b)).astype(BF16), st.astype(BF16))
            for d in range(SUB):
                ks, vs = (kk, v) if d == 0 else (pltpu.roll(kk, d, 0), pltpu.roll(v, d, 0))
                e = 1.0 if d == 0 else jnp.exp(jnp.where(rowpos >= d, b - pltpu.roll(b, d, 0), NEG_BIG))
                o = o + jnp.sum(qh * ks * e, axis=-1, keepdims=True) * vs
            k_up = kk * jnp.exp(tot - b)
            state[...] = st * jnp.exp(tot) + _dot_tn(vb, k_up.astype(BF16))
            o_ref[rows, :] = o
            r = lax.rsqrt(jnp.mean(o * o, axis=-1, keepdims=True) + RMS_EPS)
            cat_ref[rows, :] = (o * r * hgv * (og * _sigmoid(og))).astype(BF16)
            return carry

        lax.fori_loop(0, nc, chunk, 0)

    sec = lambda s: pl.BlockSpec((None, tb, LANE), lambda h, i: (s, i, h))
    return pl.pallas_call(
        body, name="hgrn_fwd", grid=(nh, t // tb),
        in_specs=[sec(2), sec(3), sec(4), sec(5),
                  pl.BlockSpec((2, LANE), lambda h, i: (0, h)),
                  pl.BlockSpec((1, LANE), lambda h, i: (0, h)), HBM],
        out_specs=[pl.BlockSpec((tb, LANE), lambda h, i: (i, nh + h)),
                   pl.BlockSpec((tb, LANE), lambda h, i: (i, h)),
                   pl.BlockSpec((None, nc, LANE, LANE), lambda h, i: (h, i, 0, 0))],
        out_shape=[jax.ShapeDtypeStruct(cat.shape, BF16), jax.ShapeDtypeStruct((t, w), F32),
                   jax.ShapeDtypeStruct((nh, t // CHUNK, LANE, LANE), F32)],
        scratch_shapes=[pltpu.VMEM((LANE, LANE), F32)],
        input_output_aliases={6: 0},
        compiler_params=_params("parallel", "arbitrary"),
    )(p3, p3, p3, p3, lb_logits, hg, cat)


def _hgrn_bwd(p3, lb_logits, hg, o_pre, states, dcat, dp3, tb):
    n_sec, t, w = p3.shape
    nh = w // LANE
    nc = tb // CHUNK
    nb = t // tb

    def body(q_ref, f_ref, v_ref, og_ref, lb_ref, hg_ref, o_ref, st_ref, dc_ref, dp_in,
             dp_ref, dlb_ref, dhg_ref, dstate, stash, lbacc, hgacc):
        del dp_in
        i, half = pl.program_id(1), pl.program_id(2)

        @pl.when(half == 1)
        def _():
            dp_ref[...] = stash[...]

        @pl.when(half == 0)
        def _():
            consts = _chunk_consts()
            rr = lax.broadcasted_iota(jnp.int32, (CHUNK, CHUNK), 0)
            cc = lax.broadcasted_iota(jnp.int32, (CHUNK, CHUNK), 1)
            upper = (cc >= rr).astype(F32)
            lb = _lower_bound(lb_ref)
            hgv = hg_ref[...]
            rowblk = lax.broadcasted_iota(jnp.int32, (CHUNK, 1), 0) // SUB
            rowpos = lax.broadcasted_iota(jnp.int32, (CHUNK, 1), 0) % SUB

            @pl.when(i == 0)
            def _():
                dstate[...] = jnp.zeros_like(dstate)
                lbacc[...] = jnp.zeros_like(lbacc)
                hgacc[...] = jnp.zeros_like(hgacc)

            def chunk(cr, carry):
                c = nc - 1 - cr
                rows = pl.ds(pl.multiple_of(c * CHUNK, CHUNK), CHUNK)
                q = q_ref[rows, :]
                v = v_ref[rows, :]
                og = og_ref[rows, :]
                o = o_ref[rows, :]
                dcg = dc_ref[rows, :]
                sf, fg, kk, sq, qh = _gate_terms(q, f_ref[rows, :], lb)
                b, rs, re, tot = _decays(jnp.log(fg), consts)
                eq = jnp.exp(b - rs)
                ek = jnp.exp(re - b)
                qt = qh * eq
                kt = kk * ek
                e_in = jnp.exp(b)
                e_up = jnp.exp(tot - b)
                e_tot = jnp.exp(tot)
                q_in = (qh * e_in).astype(BF16)
                k_up = (kk * e_up).astype(BF16)
                vb = v.astype(BF16)
                st = st_ref[c]
                dst = dstate[...]
                dstb = dst.astype(BF16)

                sg = _sigmoid(og)
                r = lax.rsqrt(jnp.mean(o * o, axis=-1, keepdims=True) + RMS_EPS)
                ohat = o * r
                d_og = dcg * ohat * hgv * (sg * (1.0 + og * (1.0 - sg)))
                d_on = dcg * (og * sg)
                hgacc[...] += jnp.sum((d_on * ohat).reshape(CHUNK // SUBLANE, SUBLANE, LANE), axis=0)
                d_oh = d_on * hgv
                do = r * (d_oh - ohat * jnp.mean(d_oh * ohat, axis=-1, keepdims=True))
                dob = do.astype(BF16)

                da = _dot_nt(dob, vb)
                a_off = jnp.zeros((CHUNK, CHUNK), F32)
                dqt = jnp.zeros((CHUNK, LANE), F32)
                dkt = jnp.zeros((CHUNK, LANE), F32)
                for blk in range(1, CHUNK // SUB):
                    ki, scale = _scaled_keys(kt, rs, re, rowblk, blk)
                    kib = ki.astype(BF16)
                    qib = jnp.where(rowblk == blk, qt, 0.0).astype(BF16)
                    dab = jnp.where(rowblk == blk, da, 0.0).astype(BF16)
                    a_off = a_off + _dot_nt(qib, kib)
                    dqt = dqt + _dot(dab, kib)
                    dkt = dkt + _dot_tn(dab, qib) * scale
                dqh = dqt * eq
                dk = dkt * ek
                dv = _dot_tn(a_off.astype(BF16), dob)

                dqh = dqh + _dot(dob, st.astype(BF16)) * e_in
                dk = dk + _dot(vb, dstb) * e_up
                dv = dv + _dot_nt(k_up, dstb)
                st_end = st * e_tot + _dot_tn(vb, k_up)
                carry_g = jnp.sum(st_end * dst, axis=0, keepdims=True)
                dstate[...] = dst * e_tot + _dot_tn(dob, q_in)

                for d in range(SUB):
                    if d == 0:
                        ks, vs, e = kk, v, 1.0
                    else:
                        ks, vs = pltpu.roll(kk, d, 0), pltpu.roll(v, d, 0)
                        e = jnp.exp(jnp.where(rowpos >= d, b - pltpu.roll(b, d, 0), NEG_BIG))
                    a_d = jnp.sum(qh * ks * e, axis=-1, keepdims=True)
                    da_d = jnp.sum(do * vs, axis=-1, keepdims=True) * e
                    dqh = dqh + da_d * ks
                    ck = da_d * qh
                    cv = a_d * do
                    if d == 0:
                        dk = dk + ck
                        dv = dv + cv
                    else:
                        dk = dk + pltpu.roll(ck, CHUNK - d, 0)
                        dv = dv + pltpu.roll(cv, CHUNK - d, 0)

                dg = _dot3(upper, qh * dqh - kk * dk) + carry_g
                dfg = dg / fg - dk
                lbacc[...] += jnp.sum((dfg * (1.0 - sf)).reshape(CHUNK // SUBLANE, SUBLANE, LANE), axis=0)
                dp_ref[0, rows, :] = (dqh * (sq * (1.0 + q * (1.0 - sq)))).astype(BF16)
                dp_ref[1, rows, :] = (dfg * (1.0 - lb) * sf * (1.0 - sf)).astype(BF16)
                stash[0, rows, :] = dv.astype(BF16)
                stash[1, rows, :] = d_og.astype(BF16)
                return carry

            lax.fori_loop(0, nc, chunk, 0)

            @pl.when(i == nb - 1)
            def _():
                dlb_ref[...] = jnp.sum(lbacc[...], axis=0, keepdims=True)
                dhg_ref[...] = jnp.sum(hgacc[...], axis=0, keepdims=True)

    rev = lambda i: nb - 1 - i
    sec = lambda s: pl.BlockSpec((None, tb, LANE), lambda h, i, z: (s, rev(i), h))
    return pl.pallas_call(
        body, name="hgrn_bwd", grid=(nh, nb, 2),
        in_specs=[sec(2), sec(3), sec(4), sec(5),
                  pl.BlockSpec((2, LANE), lambda h, i, z: (0, h)),
                  pl.BlockSpec((1, LANE), lambda h, i, z: (0, h)),
                  pl.BlockSpec((tb, LANE), lambda h, i, z: (rev(i), h)),
                  pl.BlockSpec((None, nc, LANE, LANE), lambda h, i, z: (h, rev(i), 0, 0)),
                  pl.BlockSpec((tb, LANE), lambda h, i, z: (rev(i), nh + h)), HBM],
        out_specs=[pl.BlockSpec((2, tb, LANE), lambda h, i, z: (1 + z, rev(i), h)),
                   pl.BlockSpec((1, LANE), lambda h, i, z: (0, h)),
                   pl.BlockSpec((1, LANE), lambda h, i, z: (0, h))],
        out_shape=[jax.ShapeDtypeStruct((n_sec, t, w), BF16), jax.ShapeDtypeStruct((1, w), F32),
                   jax.ShapeDtypeStruct((1, w), F32)],
        scratch_shapes=[pltpu.VMEM((LANE, LANE), F32), pltpu.VMEM((2, tb, LANE), BF16),
                        pltpu.VMEM((SUBLANE, LANE), F32), pltpu.VMEM((SUBLANE, LANE), F32)],
        input_output_aliases={9: 0},
        compiler_params=_params("parallel", "arbitrary", "arbitrary"),
    )(p3, p3, p3, p3, lb_logits, hg, o_pre, states, dcat, dp3)


def _place():
    x, y, c = lax.axis_index("x"), lax.axis_index("y"), lax.axis_index("c")
    chips = [(1 - x, y), (x, 1 - y), (1 - x, 1 - y)]
    return x, y, c, chips


def _gather_weights(shards):
    n = len(shards)

    def body(*refs):
        ins, outs = refs[:n], refs[n:2 * n]
        send, recv, local = refs[2 * n:]
        x, y, c, chips = _place()
        me, sibling = (x, y, c), (x, y, 1 - c)

        def rows(k, px, py, pc):
            half = ins[k].shape[0] // 2
            return outs[k].at[2 * px + py, pl.ds(pc * half, half)]

        def copy(k, j, blk, to, src=None):
            return pltpu.make_async_remote_copy(
                src_ref=rows(k, *blk) if src is None else src, dst_ref=rows(k, *blk),
                send_sem=send.at[k, j], recv_sem=recv.at[k, j], device_id=to, device_id_type=MESH)

        mine = [pltpu.make_async_copy(ins[k], outs[k].at[2 * x + y], local.at[k]) for k in range(n)]
        for cp in mine:
            cp.start()
        first = []
        for k in range(n):
            half = ins[k].shape[0] // 2
            for j, chip in enumerate(chips):
                first.append(copy(k, j, me, (*chip, c), src=ins[k].at[pl.ds(c * half, half)]))
                first[-1].start()
        passed = []
        for k in range(n):
            for j, chip in enumerate(chips):
                copy(k, j, (*chip, c), me).wait_recv()
                passed.append(copy(k, 3 + j, (*chip, c), sibling))
                passed[-1].start()
        for k in range(n):
            for j, chip in enumerate(chips):
                copy(k, 3 + j, (*chip, 1 - c), me).wait_recv()
        for cp in first + passed:
            cp.wait_send()
        for cp in mine:
            cp.wait()

    return pl.pallas_call(
        body, name="gather_weights",
        in_specs=[HBM] * n, out_specs=[HBM] * n,
        out_shape=[jax.ShapeDtypeStruct((N_CHIPS,) + s.shape, s.dtype) for s in shards],
        scratch_shapes=[pltpu.SemaphoreType.DMA((n, 6)), pltpu.SemaphoreType.DMA((n, 6)),
                        pltpu.SemaphoreType.DMA((n,))],
    )(*shards)


def _swap_halves(grads):
    n = len(grads)

    def body(*refs):
        ins, outs = refs[:n], refs[n:2 * n]
        send, recv = refs[2 * n:]
        x, y, c, _ = _place()
        cps = []
        for k in range(n):
            half = ins[k].shape[1] // 2
            cp = pltpu.make_async_remote_copy(
                src_ref=ins[k].at[:, pl.ds((1 - c) * half, half)], dst_ref=outs[k],
                send_sem=send.at[k], recv_sem=recv.at[k], device_id=(x, y, 1 - c), device_id_type=MESH)
            cp.start()
            cps.append(cp)
        for cp in cps:
            cp.wait()

    return pl.pallas_call(
        body, name="swap_halves", in_specs=[HBM] * n, out_specs=[HBM] * n,
        out_shape=[jax.ShapeDtypeStruct((g.shape[0], g.shape[1] // 2, g.shape[2]), g.dtype) for g in grads],
        scratch_shapes=[pltpu.SemaphoreType.DMA((n,)), pltpu.SemaphoreType.DMA((n,))],
    )(*grads)


def _add_halves(name, g, other, c_idx):
    s, r, cols = g.shape
    half = r // 2
    tr = _div_tile(half, 16, 512)
    nb = half // tr

    def body(c_ref, g_ref, o_ref, q_ref):
        del c_ref
        q_ref[...] = (g_ref[...] + o_ref[...]).astype(BF16)

    return pl.pallas_call(
        body, name=name,
        grid_spec=pltpu.PrefetchScalarGridSpec(
            num_scalar_prefetch=1, grid=(s, nb),
            in_specs=[pl.BlockSpec((None, tr, cols), lambda k, i, c: (k, c[0] * nb + i, 0)),
                      pl.BlockSpec((None, tr, cols), lambda k, i, c: (k, i, 0))],
            out_specs=pl.BlockSpec((None, tr, cols), lambda k, i, c: (k, i, 0))),
        out_shape=jax.ShapeDtypeStruct((s, half, cols), BF16),
        compiler_params=_params("parallel", "parallel"),
    )(c_idx, g, other)


def _send_partials(parts):
    n = len(parts)

    def body(*refs):
        ins, outs = refs[:n], refs[n:2 * n]
        send, recv = refs[2 * n:]
        x, y, c, chips = _place()
        cps = []
        for k in range(n):
            for j, (px, py) in enumerate(chips):
                cp = pltpu.make_async_remote_copy(
                    src_ref=ins[k].at[2 * px + py], dst_ref=outs[k].at[j],
                    send_sem=send.at[k, j], recv_sem=recv.at[k, j], device_id=(px, py, c), device_id_type=MESH)
                cp.start()
                cps.append(cp)
        for cp in cps:
            cp.wait()

    return pl.pallas_call(
        body, name="send_partials", in_specs=[HBM] * n, out_specs=[HBM] * n,
        out_shape=[jax.ShapeDtypeStruct((3,) + p.shape[1:], p.dtype) for p in parts],
        scratch_shapes=[pltpu.SemaphoreType.DMA((n, 3)), pltpu.SemaphoreType.DMA((n, 3))],
    )(*parts)


def _sum_partials(name, part, arrived, chip_idx):
    _, half, cols = part.shape
    tr = _div_tile(half, 16, 512)

    def body(s_ref, p_ref, a_ref, o_ref):
        del s_ref
        o_ref[...] = ((p_ref[...].astype(F32) + a_ref[0].astype(F32)) + a_ref[1].astype(F32)) + a_ref[2].astype(F32)

    return pl.pallas_call(
        body, name=name,
        grid_spec=pltpu.PrefetchScalarGridSpec(
            num_scalar_prefetch=1, grid=(half // tr,),
            in_specs=[pl.BlockSpec((None, tr, cols), lambda i, s: (s[0], i, 0)),
                      pl.BlockSpec((3, tr, cols), lambda i, s: (0, i, 0))],
            out_specs=pl.BlockSpec((tr, cols), lambda i, s: (i, 0))),
        out_shape=jax.ShapeDtypeStruct((half, cols), F32),
        compiler_params=_params("parallel"),
    )(chip_idx, part, arrived)


def _join_halves(halves):
    n = len(halves)

    def body(*refs):
        ins, outs = refs[:n], refs[n:2 * n]
        send, recv, local = refs[2 * n:]
        x, y, c, _ = _place()
        cps = []
        for k in range(n):
            half = ins[k].shape[0]
            mine = pltpu.make_async_copy(ins[k], outs[k].at[pl.ds(c * half, half)], local.at[k])
            mine.start()
            cp = pltpu.make_async_remote_copy(
                src_ref=ins[k], dst_ref=outs[k].at[pl.ds(c * half, half)],
                send_sem=send.at[k], recv_sem=recv.at[k], device_id=(x, y, 1 - c), device_id_type=MESH)
            cp.start()
            cps.extend([mine, cp])
        for cp in cps:
            cp.wait()

    return pl.pallas_call(
        body, name="join_halves", in_specs=[HBM] * n, out_specs=[HBM] * n,
        out_shape=[jax.ShapeDtypeStruct((2 * h.shape[0], h.shape[1]), h.dtype) for h in halves],
        scratch_shapes=[pltpu.SemaphoreType.DMA((n,)), pltpu.SemaphoreType.DMA((n,)),
                        pltpu.SemaphoreType.DMA((n,))],
    )(*halves)


def _small_allreduce(wide_rows, ffn_rows, w, dff, n_wide, n_ffn):
    n_in = len(wide_rows) + len(ffn_rows)

    def body(*refs):
        ins = refs[:n_in]
        s1_ref, s2_ref, r1, r2, p1, p2, send, recv = refs[n_in:]
        x, y, c, _ = _place()
        me = 4 * x + 2 * y + c
        p1[...] = jnp.zeros_like(p1)
        p2[...] = jnp.zeros_like(p2)
        row = 0
        for ref, (_, r, m) in zip(ins, wide_rows):
            if m == 1 and r % SUBLANE == 0 and row % SUBLANE == 0:
                p1[row:row + r, :] = ref[...]
                row += r
                continue
            for rr in range(r):
                for mm in range(m):
                    p1[row:row + 1, :] = ref[rr:rr + 1, mm * w:(mm + 1) * w]
                    row += 1
        row = 0
        for ref, arr in zip(ins[len(wide_rows):], ffn_rows):
            r = arr.shape[0]
            p2[row:row + r, :] = ref[...]
            row += r
        r1[me] = p1[...]
        r2[me] = p2[...]
        cps = []
        for mask in range(1, 8):
            peer = (x ^ (mask >> 2), y ^ ((mask >> 1) & 1), c ^ (mask & 1))
            for a, (src, dst) in enumerate(((p1, r1), (p2, r2))):
                cp = pltpu.make_async_remote_copy(
                    src_ref=src, dst_ref=dst.at[me], send_sem=send.at[a, mask - 1], recv_sem=recv.at[a, mask - 1],
                    device_id=peer, device_id_type=MESH)
                cp.start()
                cps.append(cp)
        for cp in cps:
            cp.wait()
        t1, t2 = r1[0], r2[0]
        for d in range(1, 8):
            t1 = t1 + r1[d]
            t2 = t2 + r2[d]
        s1_ref[...] = t1
        s2_ref[...] = t2

    ins = [a for a, _, _ in wide_rows] + list(ffn_rows)
    return pl.pallas_call(
        body, name="small_allreduce", in_specs=[VMEM_FULL] * n_in, out_specs=[VMEM_FULL, VMEM_FULL],
        out_shape=[jax.ShapeDtypeStruct((n_wide, w), F32), jax.ShapeDtypeStruct((n_ffn, dff), F32)],
        scratch_shapes=[pltpu.VMEM((8, n_wide, w), F32), pltpu.VMEM((8, n_ffn, dff), F32),
                        pltpu.VMEM((n_wide, w), F32), pltpu.VMEM((n_ffn, dff), F32),
                        pltpu.SemaphoreType.DMA((2, 7)), pltpu.SemaphoreType.DMA((2, 7))],
        compiler_params=pltpu.CompilerParams(vmem_limit_bytes=VMEM_LIMIT),
    )(*ins)


def _adamw(w, g, m, v):
    m2 = ADAM_B1 * m + (1.0 - ADAM_B1) * g
    v2 = ADAM_B2 * v + (1.0 - ADAM_B2) * (g * g)
    m_hat = m2 / (1.0 - ADAM_B1 ** ADAM_STEP)
    v_hat = v2 / (1.0 - ADAM_B2 ** ADAM_STEP)
    delta = -ADAM_LR * (m_hat / (jnp.sqrt(v_hat) + ADAM_EPS) + ADAM_WD * w)
    return delta, m2, v2


def _adam_big(name, w, g, m, v):
    r, c = w.shape
    tr = 128 if r % 128 == 0 else r

    def body(w_ref, g_ref, m_ref, v_ref, d_ref, m2_ref, v2_ref):
        d_ref[...], m2_ref[...], v2_ref[...] = _adamw(w_ref[...], g_ref[...], m_ref[...], v_ref[...])

    blk = pl.BlockSpec((tr, c), lambda i: (i, 0))
    return pl.pallas_call(
        body, name=name, grid=(r // tr,), in_specs=[blk] * 4, out_specs=[blk] * 3,
        out_shape=[jax.ShapeDtypeStruct((r, c), F32)] * 3,
        compiler_params=_params("parallel"),
    )(w, g, m, v)


def _adam_small(s1, s2, cw_g, fw_g, lb_logits, triples, layout, w):
    n = len(triples)

    def body(*refs):
        s1_ref, s2_ref, cw_ref, fw_ref, lbl_ref = refs[:5]
        prm = refs[5:5 + 3 * n]
        outs = refs[5 + 3 * n:]
        for p, lay in enumerate(layout):
            w_ref, m_ref, v_ref = prm[3 * p:3 * p + 3]
            g_ref, d_ref, m2_ref, v2_ref = outs[4 * p:4 * p + 4]
            if lay[0] == "wide":
                _, row, r, pieces = lay
                for rr in range(r):
                    for mm in range(pieces):
                        g_ref[rr:rr + 1, mm * w:(mm + 1) * w] = s1_ref[row:row + 1, :]
                        row += 1
            elif lay[0] == "ffn":
                _, row, r = lay
                g_ref[...] = s2_ref[row:row + r, :]
            elif lay[0] == "cw":
                g_ref[...] = cw_ref[0:g_ref.shape[0], :]
            elif lay[0] == "fw":
                g_ref[...] = fw_ref[0:g_ref.shape[0], :]
            else:
                s0 = _lower_bound(lbl_ref)
                d0 = s1_ref[lay[1]:lay[1] + 1, :] * s0 * (1.0 - s0)
                g_ref[0:1, :] = d0
                g_ref[1:2, :] = -d0
            d_ref[...], m2_ref[...], v2_ref[...] = _adamw(w_ref[...], g_ref[...], m_ref[...], v_ref[...])

    flat = [a for tr in triples for a in tr]
    shapes = []
    for tr in triples:
        shapes.extend([jax.ShapeDtypeStruct(tr[0].shape, F32)] * 4)
    return pl.pallas_call(
        body, name="adam_small", in_specs=[VMEM_FULL] * (5 + 3 * n), out_specs=[VMEM_FULL] * (4 * n),
        out_shape=shapes, compiler_params=pltpu.CompilerParams(vmem_limit_bytes=VMEM_LIMIT),
    )(s1, s2, cw_g, fw_g, lb_logits, *flat)


def _row_tile(t):
    return 512 if t % 512 == 0 and t >= 2048 else 128


def kernel(x, emb_ln_g, emb_ln_b, w_in, conv_w, conv_b, conv_norm_g, conv_norm_b, lb_logits, hgrn_norm_g, w_out, ln1_g, ln1_b, w_ffn_up, ffn_conv_w, ffn_conv_b, w_ffn_down, ln2_g, ln2_b, loss_target, m_emb_ln_g, m_emb_ln_b, m_w_in, m_conv_w, m_conv_b, m_conv_norm_g, m_conv_norm_b, m_lb_logits, m_hgrn_norm_g, m_w_out, m_ln1_g, m_ln1_b, m_w_ffn_up, m_ffn_conv_w, m_ffn_conv_b, m_w_ffn_down, m_ln2_g, m_ln2_b, v_emb_ln_g, v_emb_ln_b, v_w_in, v_conv_w, v_conv_b, v_conv_norm_g, v_conv_norm_b, v_lb_logits, v_hgrn_norm_g, v_w_out, v_ln1_g, v_ln1_b, v_w_ffn_up, v_ffn_conv_w, v_ffn_conv_b, v_w_ffn_down, v_ln2_g, v_ln2_b):
    depth = w_in.shape[0]
    assert depth == 1 and x.shape[0] == 1
    alpha = (2.0 * depth) ** 0.25
    t, d = x.shape[1], x.shape[2]
    w = d // 2
    dff = ffn_conv_b.shape[1]
    kc = conv_w.shape[1]
    assert w % (2 * LANE) == 0 and dff % (4 * LANE) == 0 and t % 128 == 0
    tm = _row_tile(t)
    tm2 = tm // 2
    cb = 2 * LANE
    cbf = 4 * LANE
    tb = tm

    xi = lax.axis_index("x")
    yi = lax.axis_index("y")
    ci = lax.axis_index("c")
    chip = 2 * xi + yi
    c_idx = jnp.reshape(ci, (1,)).astype(jnp.int32)
    chip_idx = jnp.reshape(chip, (1,)).astype(jnp.int32)

    x2 = x[0]
    tgt = loss_target[0]
    g0, b0 = emb_ln_g.reshape(1, d), emb_ln_b.reshape(1, d)
    w_in2, w_out2, w_up2, w_dn2 = w_in[0], w_out[0], w_ffn_up[0], w_ffn_down[0]
    cw2, fw2 = conv_w[0], ffn_conv_w[0]

    full = _gather_weights([_cast_bf16(w_in2, "cast_w_in"), _cast_bf16(w_out2, "cast_w_out"),
                            _cast_bf16(w_up2, "cast_w_up"), _cast_bf16(w_dn2, "cast_w_down"),
                            _pad_rows(cw2), _pad_rows(fw2)])
    w_in3, w_out3, w_up3, w_dn3, cw_full3, fw_full3 = full
    w_out_full = w_out3.reshape(d, d)
    cw_full = _unshard_cols(cw_full3)[:kc]
    fw_full = _unshard_cols(fw_full3)[:fw2.shape[0]]

    h0b = _ln0(x2, g0, b0, tm)
    p3 = _proj("in_proj", h0b, w_in3, 6, tm, w // 2)
    cat, u1 = _conv_fwd(p3, cw_full, conv_b, conv_norm_g, conv_norm_b, tm2, cb)
    cat, o_pre, states = _hgrn_fwd(p3, lb_logits, hgrn_norm_g, cat, tb)
    xhat1, h1b, rstd1 = _mix_ln1(cat, w_out_full, x2, g0, b0, ln1_g, ln1_b, alpha, tm2)
    hh3 = _proj("ffn_up", h1b, w_up3, 2, tm, dff // 4)
    act = _ffn_act_fwd(hh3, fw_full, ffn_conv_b, tm, cbf)
    dz2, dz2b, dg2, db2, loss_row = _down_ln2_loss(act, w_dn3, xhat1, tgt, ln1_g, ln1_b, ln2_g, ln2_b, alpha, tm2)

    ks = dff // N_CHIPS
    dact = _proj_t("ffn_down_t", dz2b, w_dn3.reshape(dff, d), tm, ks)
    dhh3, dfw, dfb = _ffn_act_bwd(dact, hh3, fw_full, ffn_conv_b, tm, cbf)
    tt = tm
    d_w_dn = _wgrad("wgrad_down", act, dz2b, (N_CHIPS, ks, d), (N_CHIPS, 2, t // tt),
                    pl.BlockSpec((tt, ks), lambda s, j, k: (k, s)),
                    pl.BlockSpec((tt, d // 2), lambda s, j, k: (k, j)),
                    pl.BlockSpec((None, ks, d // 2), lambda s, j, k: (s, 0, j)))
    dz1, dz1b, dg1, db1 = _up_t_ln1_bwd(dhh3, w_up3, dz2, xhat1, rstd1, ln1_g, alpha, tm2)
    wu = 2 * dff // N_CHIPS
    tnu = wu // 2
    per_sec_u = dff // tnu
    d_w_up = _wgrad("wgrad_up", h1b, dhh3, (N_CHIPS, d, wu), (N_CHIPS, 2, 2, t // tt),
                    pl.BlockSpec((tt, d // 2), lambda s, r, j, k: (k, r)),
                    pl.BlockSpec((None, tt, tnu), lambda s, r, j, k: ((2 * s + j) // per_sec_u, k, (2 * s + j) % per_sec_u)),
                    pl.BlockSpec((None, d // 2, tnu), lambda s, r, j, k: (s, r, j)))
    dcat = _proj_t("out_proj_t", dz1b, w_out_full, tm, d // 2)
    d_w_out = _wgrad("wgrad_out", cat, dz1b, (d, d), (2, 2, t // tt),
                     pl.BlockSpec((tt, d // 2), lambda r, j, k: (k, r)),
                     pl.BlockSpec((tt, d // 2), lambda r, j, k: (k, j)),
                     pl.BlockSpec((d // 2, d // 2), lambda r, j, k: (r, j)))
    du1, dcng, dcnb = _conv_norm_bwd(dcat, u1, conv_norm_g, conv_norm_b, tm)
    dp3, dcw, dcb = _conv_bwd(du1, p3, cw_full, tm2, cb)
    dp3, dlb, dhg = _hgrn_bwd(p3, lb_logits, hgrn_norm_g, o_pre, states, dcat, dp3, tb)
    dx, dg0, db0 = _in_t_ln0_bwd(dp3, w_in3, dz1, x2, g0, alpha, tm2)
    wi = 6 * w // N_CHIPS
    tni = w // 2
    d_w_in = _wgrad("wgrad_in", h0b, dp3, (N_CHIPS, d, wi), (N_CHIPS, 2, wi // tni, t // tt),
                    pl.BlockSpec((tt, d // 2), lambda s, r, j, k: (k, r)),
                    pl.BlockSpec((None, tt, tni), lambda s, r, j, k: (((wi // tni) * s + j) // 2, k, ((wi // tni) * s + j) % 2)),
                    pl.BlockSpec((None, d // 2, tni), lambda s, r, j, k: (s, r, j)))

    big = [d_w_in, d_w_out.reshape(N_CHIPS, d // N_CHIPS, d), d_w_up, d_w_dn]
    names = ["w_in", "w_out", "w_up", "w_down"]
    arrived = _swap_halves(big)
    parts = [_add_halves("add_halves_" + nm, g, a, c_idx) for nm, g, a in zip(names, big, arrived)]
    landed = _send_partials(parts)
    halves = [_sum_partials("sum_partials_" + nm, p, a, chip_idx) for nm, p, a in zip(names, parts, landed)]
    g_w_in, g_w_out, g_w_up, g_w_dn = _join_halves(halves)

    kpad = dcw.shape[0]
    wide = [(dcw, kpad, 1), (dg0, 1, 2), (db0, 1, 2), (dg1, 1, 2), (db1, 1, 2), (dg2, 1, 2), (db2, 1, 2),
            (dcb, 1, 1), (dcng, 1, 1), (dcnb, 1, 1), (dlb, 1, 1), (dhg, 1, 1)]
    n_wide = sum(r * m for _, r, m in wide)
    n_wide_pad = -(-n_wide // SUBLANE) * SUBLANE
    s1, s2 = _small_allreduce(wide, [dfw, dfb], w, dff, n_wide_pad, 2 * SUBLANE)
    cw_g = lax.dynamic_slice_in_dim(s1[0:kpad], chip * (w // N_CHIPS), w // N_CHIPS, axis=1)
    fw_g = lax.dynamic_slice_in_dim(s2[0:SUBLANE], chip * (dff // N_CHIPS), dff // N_CHIPS, axis=1)

    small = [
        (g0, m_emb_ln_g.reshape(1, d), v_emb_ln_g.reshape(1, d)), (b0, m_emb_ln_b.reshape(1, d), v_emb_ln_b.reshape(1, d)),
        (cw2, m_conv_w[0], v_conv_w[0]), (conv_b, m_conv_b, v_conv_b),
        (conv_norm_g, m_conv_norm_g, v_conv_norm_g), (conv_norm_b, m_conv_norm_b, v_conv_norm_b),
        (lb_logits, m_lb_logits, v_lb_logits), (hgrn_norm_g, m_hgrn_norm_g, v_hgrn_norm_g),
        (ln1_g, m_ln1_g, v_ln1_g), (ln1_b, m_ln1_b, v_ln1_b),
        (fw2, m_ffn_conv_w[0], v_ffn_conv_w[0]), (ffn_conv_b, m_ffn_conv_b, v_ffn_conv_b),
        (ln2_g, m_ln2_g, v_ln2_g), (ln2_b, m_ln2_b, v_ln2_b),
    ]
    r0 = kpad
    layout = [("wide", r0, 1, 2), ("wide", r0 + 2, 1, 2), ("cw",), ("wide", r0 + 12, 1, 1), ("wide", r0 + 13, 1, 1),
              ("wide", r0 + 14, 1, 1), ("lb", r0 + 15), ("wide", r0 + 16, 1, 1), ("wide", r0 + 4, 1, 2),
              ("wide", r0 + 6, 1, 2), ("fw",), ("ffn", SUBLANE, 1), ("wide", r0 + 8, 1, 2), ("wide", r0 + 10, 1, 2)]
    so = _adam_small(s1, s2, cw_g, fw_g, lb_logits, small, layout, w)
    sm = {nm: so[4 * i:4 * i + 4] for i, nm in enumerate(
        ["emb_ln_g", "emb_ln_b", "conv_w", "conv_b", "conv_norm_g", "conv_norm_b", "lb_logits", "hgrn_norm_g",
         "ln1_g", "ln1_b", "ffn_conv_w", "ffn_conv_b", "ln2_g", "ln2_b"])}
    bigs = {}
    for nm, wt, g, m, v in (("w_in", w_in2, g_w_in, m_w_in[0], v_w_in[0]), ("w_out", w_out2, g_w_out, m_w_out[0], v_w_out[0]),
                            ("w_ffn_up", w_up2, g_w_up, m_w_ffn_up[0], v_w_ffn_up[0]),
                            ("w_ffn_down", w_dn2, g_w_dn, m_w_ffn_down[0], v_w_ffn_down[0])):
        bigs[nm] = (g,) + tuple(_adam_big("adam_" + nm, wt, g, m, v))

    loss = lax.psum(loss_row[0, 0], ("x", "y", "c"))

    order = ["emb_ln_g", "emb_ln_b", "w_in", "conv_w", "conv_b", "conv_norm_g", "conv_norm_b", "lb_logits",
             "hgrn_norm_g", "w_out", "ln1_g", "ln1_b", "w_ffn_up", "ffn_conv_w", "ffn_conv_b", "w_ffn_down",
             "ln2_g", "ln2_b"]
    shapes = dict(emb_ln_g=emb_ln_g.shape, emb_ln_b=emb_ln_b.shape, w_in=w_in.shape, conv_w=conv_w.shape,
                  w_out=w_out.shape, w_ffn_up=w_ffn_up.shape, ffn_conv_w=ffn_conv_w.shape, w_ffn_down=w_ffn_down.shape)
    outs = [loss, dx.reshape(x.shape)]
    for which in range(4):
        for nm in order:
            a = bigs[nm][which] if nm in bigs else sm[nm][which]
            outs.append(a.reshape(shapes[nm]) if nm in shapes else a)
    return tuple(outs)


def _pad_rows(a):
    k = a.shape[0]
    kp = -(-k // 16) * 16
    return jnp.pad(a, ((0, kp - k), (0, 0)))


def _unshard_cols(a3):
    s, k, c = a3.shape
    return jnp.transpose(a3, (1, 0, 2)).reshape(k, s * c)
```

```python
import functools

import jax
import jax.numpy as jnp
from jax import lax
from jax.experimental import pallas as pl
from jax.experimental.pallas import tpu as pltpu

F32 = jnp.float32
BF16 = jnp.bfloat16

LN_EPS = 1e-5
RMS_EPS = 1e-6
LANE = 128
SUBLANE = 8
CHUNK = 64
SUB = 8
HALO = 32
FHALO = 8
ROWS = 64
N_CHIPS = 4
VMEM_LIMIT = 56 << 20
NEG_BIG = -1e30

ADAM_LR = 0.001
ADAM_B1 = 0.9
ADAM_B2 = 0.999
ADAM_EPS = 1e-08
ADAM_WD = 0.01
ADAM_STEP = 10

MESH = pl.DeviceIdType.MESH
HBM = pl.BlockSpec(memory_space=pl.ANY)
VMEM_FULL = pl.BlockSpec(memory_space=pltpu.VMEM)


def _params(*sem):
    return pltpu.CompilerParams(dimension_semantics=sem, vmem_limit_bytes=VMEM_LIMIT)


def _div_tile(n, mult, cap):
    best = n
    for t in range(mult, min(n, cap) + 1, mult):
        if n % t == 0:
            best = t
    return best


def _sigmoid(x):
    return 1.0 / (1.0 + jnp.exp(-x))


def _ln_stats(x):
    mu = jnp.mean(x, axis=-1, keepdims=True)
    xc = x - mu
    var = jnp.mean(xc * xc, axis=-1, keepdims=True)
    rstd = lax.rsqrt(var + LN_EPS)
    return xc * rstd, rstd


def _ln_bwd(dy, xhat, rstd, g):
    dyg = dy * g
    m1 = jnp.mean(dyg, axis=-1, keepdims=True)
    m2 = jnp.mean(dyg * xhat, axis=-1, keepdims=True)
    return rstd * (dyg - m1 - xhat * m2)


def _dot_nt(a, b):
    return lax.dot_general(a, b, (((1,), (1,)), ((), ())), preferred_element_type=F32)


def _dot_tn(a, b):
    return lax.dot_general(a, b, (((0,), (0,)), ((), ())), preferred_element_type=F32)


def _dot(a, b):
    return jnp.dot(a, b, preferred_element_type=F32)


def _dot3(m, x):
    mb = m.astype(BF16)
    x1 = x.astype(BF16)
    r1 = x - x1.astype(F32)
    x2 = r1.astype(BF16)
    x3 = (r1 - x2.astype(F32)).astype(BF16)
    return _dot(mb, x1) + _dot(mb, x2) + _dot(mb, x3)


def _place_shard(x, name, chip_idx, dtype):
    r, c = x.shape
    tr = _div_tile(r, 16, 512)

    def body(s_ref, x_ref, o_ref):
        del s_ref
        o_ref[...] = x_ref[...].astype(dtype)

    return pl.pallas_call(
        body, name=name,
        grid_spec=pltpu.PrefetchScalarGridSpec(
            num_scalar_prefetch=1, grid=(r // tr,),
            in_specs=[pl.BlockSpec((tr, c), lambda i, s: (i, 0))],
            out_specs=pl.BlockSpec((None, tr, c), lambda i, s: (s[0], i, 0))),
        out_shape=jax.ShapeDtypeStruct((N_CHIPS, r, c), dtype),
        compiler_params=_params("parallel"),
    )(chip_idx, x)


def _ln0(x, g, b, tm):
    t, d = x.shape

    def body(x_ref, g_ref, b_ref, o_ref):
        xh, _ = _ln_stats(x_ref[...])
        o_ref[...] = (xh * g_ref[...] + b_ref[...]).astype(BF16)

    row = pl.BlockSpec((1, d), lambda i: (0, 0))
    return pl.pallas_call(
        body, name="ln0", grid=(t // tm,),
        in_specs=[pl.BlockSpec((tm, d), lambda i: (i, 0)), row, row],
        out_specs=pl.BlockSpec((tm, d), lambda i: (i, 0)),
        out_shape=jax.ShapeDtypeStruct((t, d), BF16),
        compiler_params=_params("parallel"),
    )(x, g, b)


def _proj(name, a, w3, n_sec, tm, tn):
    m, k = a.shape
    s, _, ws = w3.shape
    sec_w = s * ws // n_sec
    nj = ws // tn
    per_sec = sec_w // tn

    def body(a_ref, w_ref, o_ref):
        o_ref[...] = _dot(a_ref[...], w_ref[...])

    return pl.pallas_call(
        body, name=name, grid=(s * nj, m // tm),
        in_specs=[pl.BlockSpec((tm, k), lambda j, i: (i, 0)),
                  pl.BlockSpec((None, k, tn), lambda j, i: (j // nj, 0, j % nj))],
        out_specs=pl.BlockSpec((None, tm, tn), lambda j, i: (j // per_sec, i, j % per_sec)),
        out_shape=jax.ShapeDtypeStruct((n_sec, m, sec_w), F32),
        compiler_params=_params("parallel", "parallel"),
    )(a, w3)


def _proj_t(name, a, w, tm, tn):
    m, k = a.shape
    n = w.shape[0]

    def body(a_ref, w_ref, o_ref):
        o_ref[...] = _dot_nt(a_ref[...], w_ref[...])

    return pl.pallas_call(
        body, name=name, grid=(n // tn, m // tm),
        in_specs=[pl.BlockSpec((tm, k), lambda j, i: (i, 0)),
                  pl.BlockSpec((tn, k), lambda j, i: (j, 0))],
        out_specs=pl.BlockSpec((tm, tn), lambda j, i: (i, j)),
        out_shape=jax.ShapeDtypeStruct((m, n), F32),
        compiler_params=_params("parallel", "parallel"),
    )(a, w)


def _wgrad(name, a, b, out_shape, grid, a_spec, b_spec, o_spec):
    nt = len(grid) - 1

    def body(a_ref, b_ref, o_ref):
        t = pl.program_id(nt)
        prod = _dot_tn(a_ref[...], b_ref[...])

        @pl.when(t == 0)
        def _():
            o_ref[...] = prod

        @pl.when(t > 0)
        def _():
            o_ref[...] += prod

    return pl.pallas_call(
        body, name=name, grid=grid, in_specs=[a_spec, b_spec], out_specs=o_spec,
        out_shape=jax.ShapeDtypeStruct(out_shape, F32),
        compiler_params=_params(*(["parallel"] * nt + ["arbitrary"])),
    )(a, b)


def _mix_ln1(cat, w_out, x, g0, b0, g1, b1, alpha, tm):
    t, d = x.shape

    def body(cat_ref, w_ref, x_ref, g0_ref, b0_ref, g1_ref, b1_ref, xh_ref, h1b_ref, rstd_ref):
        mix = _dot(cat_ref[...], w_ref[...])
        xh0, _ = _ln_stats(x_ref[...])
        z1 = alpha * (xh0 * g0_ref[...] + b0_ref[...]) + mix
        xh1, rstd1 = _ln_stats(z1)
        xh_ref[...] = xh1
        h1b_ref[...] = (xh1 * g1_ref[...] + b1_ref[...]).astype(BF16)
        rstd_ref[...] = rstd1

    row = pl.BlockSpec((1, d), lambda i: (0, 0))
    blk = pl.BlockSpec((tm, d), lambda i: (i, 0))
    return pl.pallas_call(
        body, name="mix_ln1", grid=(t // tm,),
        in_specs=[blk, pl.BlockSpec((d, d), lambda i: (0, 0)), blk, row, row, row, row],
        out_specs=[blk, blk, pl.BlockSpec((tm, 1), lambda i: (i, 0))],
        out_shape=[jax.ShapeDtypeStruct((t, d), F32), jax.ShapeDtypeStruct((t, d), BF16),
                   jax.ShapeDtypeStruct((t, 1), F32)],
        compiler_params=_params("parallel"),
    )(cat, w_out, x, g0, b0, g1, b1)


def _down_ln2_loss(act, w_down3, xhat1, tgt, g1, b1, g2, b2, alpha, tm):
    t, d = xhat1.shape
    s, ks, _ = w_down3.shape
    ni = t // tm
    inv_d = 1.0 / d

    def body(act_ref, w_ref, xh1_ref, tgt_ref, g1_ref, b1_ref, g2_ref, b2_ref,
             dz2_ref, dz2b_ref, dg2_ref, db2_ref, loss_ref, acc, lrow):
        i, k = pl.program_id(0), pl.program_id(1)
        prod = _dot(act_ref[...], w_ref[...])

        @pl.when(k == 0)
        def _():
            acc[...] = prod

        @pl.when(k > 0)
        def _():
            acc[...] += prod

        @pl.when(k == s - 1)
        def _():
            h1 = xh1_ref[...] * g1_ref[...] + b1_ref[...]
            xh2, rstd2 = _ln_stats(alpha * h1 + acc[...])
            g2v = g2_ref[...]
            diff = xh2 * g2v + b2_ref[...] - tgt_ref[...]
            dh2 = diff * inv_d
            sq = jnp.sum(diff * diff, axis=0, keepdims=True)
            dg = jnp.sum(dh2 * xh2, axis=0, keepdims=True)
            db = jnp.sum(dh2, axis=0, keepdims=True)

            @pl.when(i == 0)
            def _():
                lrow[...] = sq
                dg2_ref[...] = dg
                db2_ref[...] = db

            @pl.when(i > 0)
            def _():
                lrow[...] += sq
                dg2_ref[...] += dg
                db2_ref[...] += db

            dz2 = _ln_bwd(dh2, xh2, rstd2, g2v)
            dz2_ref[...] = dz2
            dz2b_ref[...] = dz2.astype(BF16)

            @pl.when(i == ni - 1)
            def _():
                tot = jnp.sum(lrow[...], axis=-1, keepdims=True) * (0.5 * inv_d)
                loss_ref[...] = jnp.broadcast_to(tot, (1, LANE))

    row = pl.BlockSpec((1, d), lambda i, k: (0, 0))
    blk = pl.BlockSpec((tm, d), lambda i, k: (i, 0))
    return pl.pallas_call(
        body, name="down_ln2_loss", grid=(ni, s),
        in_specs=[pl.BlockSpec((tm, ks), lambda i, k: (i, k)),
                  pl.BlockSpec((None, ks, d), lambda i, k: (k, 0, 0)),
                  blk, blk, row, row, row, row],
        out_specs=[blk, blk, row, row, pl.BlockSpec((1, LANE), lambda i, k: (0, 0))],
        out_shape=[jax.ShapeDtypeStruct((t, d), F32), jax.ShapeDtypeStruct((t, d), BF16),
                   jax.ShapeDtypeStruct((1, d), F32), jax.ShapeDtypeStruct((1, d), F32),
                   jax.ShapeDtypeStruct((1, LANE), F32)],
        scratch_shapes=[pltpu.VMEM((tm, d), F32), pltpu.VMEM((1, d), F32)],
        compiler_params=_params("arbitrary", "arbitrary"),
    )(act, w_down3, xhat1, tgt, g1, b1, g2, b2)


def _up_t_ln1_bwd(dhh3, w_up3, dz2, xhat1, rstd1, g1, alpha, tm):
    t, d = dz2.shape
    s, _, ws = w_up3.shape
    tk = ws // 2
    nk = 2 * s
    per_sec = dhh3.shape[2] // tk

    def body(a_ref, w_ref, dz2_ref, xh_ref, rstd_ref, g_ref, dz1_ref, dz1b_ref, dg_ref, db_ref, acc):
        i, k = pl.program_id(0), pl.program_id(1)
        prod = _dot_nt(a_ref[...], w_ref[...])

        @pl.when(k == 0)
        def _():
            acc[...] = prod

        @pl.when(k > 0)
        def _():
            acc[...] += prod

        @pl.when(k == nk - 1)
        def _():
            dh1 = alpha * dz2_ref[...] + acc[...]
            xh = xh_ref[...]
            dg = jnp.sum(dh1 * xh, axis=0, keepdims=True)
            db = jnp.sum(dh1, axis=0, keepdims=True)

            @pl.when(i == 0)
            def _():
                dg_ref[...] = dg
                db_ref[...] = db

            @pl.when(i > 0)
            def _():
                dg_ref[...] += dg
                db_ref[...] += db

            dz1 = _ln_bwd(dh1, xh, rstd_ref[...], g_ref[...])
            dz1_ref[...] = dz1
            dz1b_ref[...] = dz1.astype(BF16)

    row = pl.BlockSpec((1, d), lambda i, k: (0, 0))
    blk = pl.BlockSpec((tm, d), lambda i, k: (i, 0))
    return pl.pallas_call(
        body, name="up_t_ln1_bwd", grid=(t // tm, nk),
        in_specs=[pl.BlockSpec((None, tm, tk), lambda i, k: (k // per_sec, i, k % per_sec)),
                  pl.BlockSpec((None, d, tk), lambda i, k: (k // 2, 0, k % 2)),
                  blk, blk, pl.BlockSpec((tm, 1), lambda i, k: (i, 0)), row],
        out_specs=[blk, blk, row, row],
        out_shape=[jax.ShapeDtypeStruct((t, d), F32), jax.ShapeDtypeStruct((t, d), BF16),
                   jax.ShapeDtypeStruct((1, d), F32), jax.ShapeDtypeStruct((1, d), F32)],
        scratch_shapes=[pltpu.VMEM((tm, d), F32)],
        compiler_params=_params("arbitrary", "arbitrary"),
    )(dhh3, w_up3, dz2, xhat1, rstd1, g1)


def _in_t_ln0_bwd(dp3, w_in3, dz1, x, g0, alpha, tm):
    t, d = x.shape
    s, _, ws = w_in3.shape
    n_sec, _, sec_w = dp3.shape
    tk = sec_w // 2
    nk = n_sec * 2
    per_shard = ws // tk

    def body(a_ref, w_ref, dz1_ref, x_ref, g_ref, dx_ref, dg_ref, db_ref, acc):
        i, k = pl.program_id(0), pl.program_id(1)
        prod = _dot_nt(a_ref[...], w_ref[...])

        @pl.when(k == 0)
        def _():
            acc[...] = prod

        @pl.when(k > 0)
        def _():
            acc[...] += prod

        @pl.when(k == nk - 1)
        def _():
            dh0 = alpha * dz1_ref[...] + acc[...]
            xh, rstd = _ln_stats(x_ref[...])
            dg = jnp.sum(dh0 * xh, axis=0, keepdims=True)
            db = jnp.sum(dh0, axis=0, keepdims=True)

            @pl.when(i == 0)
            def _():
                dg_ref[...] = dg
                db_ref[...] = db

            @pl.when(i > 0)
            def _():
                dg_ref[...] += dg
                db_ref[...] += db

            dx_ref[...] = _ln_bwd(dh0, xh, rstd, g_ref[...])

    row = pl.BlockSpec((1, d), lambda i, k: (0, 0))
    blk = pl.BlockSpec((tm, d), lambda i, k: (i, 0))
    return pl.pallas_call(
        body, name="in_t_ln0_bwd", grid=(t // tm, nk),
        in_specs=[pl.BlockSpec((None, tm, tk), lambda i, k: (k // 2, i, k % 2)),
                  pl.BlockSpec((None, d, tk), lambda i, k: (k // per_shard, 0, k % per_shard)),
                  blk, blk, row],
        out_specs=[blk, row, row],
        out_shape=[jax.ShapeDtypeStruct((t, d), F32), jax.ShapeDtypeStruct((1, d), F32),
                   jax.ShapeDtypeStruct((1, d), F32)],
        scratch_shapes=[pltpu.VMEM((tm, d), F32)],
        compiler_params=_params("arbitrary", "arbitrary"),
    )(dp3, w_in3, dz1, x, g0)


def _conv_fwd(p3, conv_w, conv_b, cn_g, cn_b, tc, cb):
    _, t, w = p3.shape
    kk = conv_w.shape[0]
    off = HALO - (kk - 1)
    hb = tc // HALO

    def body(a_ref, g_ref, ap_ref, gp_ref, w_ref, b_ref, ng_ref, nb_ref, cat_ref, u1_ref, ext):
        i = pl.program_id(1)
        ext[pl.ds(HALO, tc), :] = a_ref[...] * _sigmoid(g_ref[...])
        prev = ap_ref[...] * _sigmoid(gp_ref[...])
        ext[pl.ds(0, HALO), :] = jnp.where(i > 0, prev, 0.0)
        for r in range(tc // ROWS):
            acc = jnp.broadcast_to(b_ref[...], (ROWS, cb))
            for k in range(kk):
                acc = acc + w_ref[k:k + 1, :] * ext[pl.ds(r * ROWS + off + k, ROWS), :]
            u1_ref[pl.ds(r * ROWS, ROWS), :] = acc
            for g in range(cb // LANE):
                sl = slice(g * LANE, (g + 1) * LANE)
                xh, _ = _ln_stats(acc[:, sl])
                u2 = xh * ng_ref[:, sl] + nb_ref[:, sl]
                cat_ref[pl.ds(r * ROWS, ROWS), sl] = (u2 * _sigmoid(u2)).astype(BF16)

    cur = lambda sec: pl.BlockSpec((None, tc, cb), lambda j, i: (sec, i, j))
    prev = lambda sec: pl.BlockSpec((None, HALO, cb), lambda j, i: (sec, jnp.maximum(i * hb - 1, 0), j))
    row = pl.BlockSpec((1, cb), lambda j, i: (0, j))
    return pl.pallas_call(
        body, name="conv_fwd", grid=(w // cb, t // tc),
        in_specs=[cur(0), cur(1), prev(0), prev(1), pl.BlockSpec((kk, cb), lambda j, i: (0, j)), row, row, row],
        out_specs=[pl.BlockSpec((tc, cb), lambda j, i: (i, j)), pl.BlockSpec((tc, cb), lambda j, i: (i, j))],
        out_shape=[jax.ShapeDtypeStruct((t, 2 * w), BF16), jax.ShapeDtypeStruct((t, w), F32)],
        scratch_shapes=[pltpu.VMEM((tc + HALO, cb), F32)],
        compiler_params=_params("parallel", "arbitrary"),
    )(p3, p3, p3, p3, conv_w, conv_b, cn_g, cn_b)


def _conv_norm_bwd(dcat, u1, cn_g, cn_b, tc):
    t, w = u1.shape

    def body(du_ref, u1_ref, ng_ref, nb_ref, du1_ref, dg_ref, db_ref):
        i = pl.program_id(0)
        for g in range(w // LANE):
            sl = slice(g * LANE, (g + 1) * LANE)
            ng = ng_ref[:, sl]
            xh, rstd = _ln_stats(u1_ref[:, sl])
            u2 = xh * ng + nb_ref[:, sl]
            sg = _sigmoid(u2)
            du2 = du_ref[:, sl] * (sg * (1.0 + u2 * (1.0 - sg)))
            dg = jnp.sum(du2 * xh, axis=0, keepdims=True)
            db = jnp.sum(du2, axis=0, keepdims=True)

            @pl.when(i == 0)
            def _():
                dg_ref[:, sl] = dg
                db_ref[:, sl] = db

            @pl.when(i > 0)
            def _():
                dg_ref[:, sl] += dg
                db_ref[:, sl] += db

            du1_ref[:, sl] = _ln_bwd(du2, xh, rstd, ng)

    row = pl.BlockSpec((1, w), lambda i: (0, 0))
    blk = pl.BlockSpec((tc, w), lambda i: (i, 0))
    return pl.pallas_call(
        body, name="conv_norm_bwd", grid=(t // tc,),
        in_specs=[blk, blk, row, row], out_specs=[blk, row, row],
        out_shape=[jax.ShapeDtypeStruct((t, w), F32), jax.ShapeDtypeStruct((1, w), F32),
                   jax.ShapeDtypeStruct((1, w), F32)],
        compiler_params=_params("arbitrary"),
    )(dcat, u1, cn_g, cn_b)


def _conv_bwd(du1, p3, conv_w, tc, cb):
    n_sec, t, w = p3.shape
    kk = conv_w.shape[0]
    off = HALO - (kk - 1)
    hb = tc // HALO
    nt = t // tc
    kpad = -(-kk // SUBLANE) * SUBLANE

    def body(d_ref, dn_ref, a_ref, g_ref, ap_ref, gp_ref, w_ref, dp_ref, dw_ref, db_ref, extd, extu, wacc, bacc):
        i = pl.program_id(1)

        @pl.when(i == 0)
        def _():
            wacc[...] = jnp.zeros_like(wacc)
            bacc[...] = jnp.zeros_like(bacc)

        extd[pl.ds(0, tc), :] = d_ref[...]
        extd[pl.ds(tc, HALO), :] = jnp.where(i < nt - 1, dn_ref[...], 0.0)
        extu[pl.ds(HALO, tc), :] = a_ref[...] * _sigmoid(g_ref[...])
        extu[pl.ds(0, HALO), :] = jnp.where(i > 0, ap_ref[...] * _sigmoid(gp_ref[...]), 0.0)
        for r in range(tc // ROWS):
            rows = pl.ds(r * ROWS, ROWS)
            acc = jnp.zeros((ROWS, cb), F32)
            for k in range(kk):
                acc = acc + w_ref[k:k + 1, :] * extd[pl.ds(r * ROWS + (kk - 1) - k, ROWS), :]
            a = a_ref[rows, :]
            sg = _sigmoid(g_ref[rows, :])
            dp_ref[0, rows, :] = (acc * sg).astype(BF16)
            dp_ref[1, rows, :] = (acc * a * sg * (1.0 - sg)).astype(BF16)
            d = d_ref[rows, :]
            bacc[...] += jnp.sum(d.reshape(ROWS // SUBLANE, SUBLANE, cb), axis=0)
            for k in range(kk):
                prod = d * extu[pl.ds(r * ROWS + off + k, ROWS), :]
                wacc[k] += jnp.sum(prod.reshape(ROWS // SUBLANE, SUBLANE, cb), axis=0)

        @pl.when(i == nt - 1)
        def _():
            for k in range(kk):
                dw_ref[k:k + 1, :] = jnp.sum(wacc[k], axis=0, keepdims=True)
            if kpad > kk:
                dw_ref[kk:kpad, :] = jnp.zeros((kpad - kk, cb), F32)
            db_ref[...] = jnp.sum(bacc[...], axis=0, keepdims=True)

    cur = lambda sec: pl.BlockSpec((None, tc, cb), lambda j, i: (sec, i, j))
    prev = lambda sec: pl.BlockSpec((None, HALO, cb), lambda j, i: (sec, jnp.maximum(i * hb - 1, 0), j))
    return pl.pallas_call(
        body, name="conv_bwd", grid=(w // cb, nt),
        in_specs=[pl.BlockSpec((tc, cb), lambda j, i: (i, j)),
                  pl.BlockSpec((HALO, cb), lambda j, i: (jnp.minimum((i + 1) * hb, t // HALO - 1), j)),
                  cur(0), cur(1), prev(0), prev(1), pl.BlockSpec((kk, cb), lambda j, i: (0, j))],
        out_specs=[pl.BlockSpec((2, tc, cb), lambda j, i: (0, i, j)),
                   pl.BlockSpec((kpad, cb), lambda j, i: (0, j)),
                   pl.BlockSpec((1, cb), lambda j, i: (0, j))],
        out_shape=[jax.ShapeDtypeStruct((n_sec, t, w), BF16), jax.ShapeDtypeStruct((kpad, w), F32),
                   jax.ShapeDtypeStruct((1, w), F32)],
        scratch_shapes=[pltpu.VMEM((tc + HALO, cb), F32), pltpu.VMEM((tc + HALO, cb), F32),
                        pltpu.VMEM((kk, SUBLANE, cb), F32), pltpu.VMEM((SUBLANE, cb), F32)],
        compiler_params=_params("parallel", "arbitrary"),
    )(du1, du1, p3, p3, p3, p3, conv_w)


def _ffn_act_fwd(hh3, fw, fb, tc, cb):
    _, t, dff = hh3.shape
    kk = fw.shape[0]
    off = FHALO - (kk - 1)
    hb = tc // FHALO

    def body(g_ref, v_ref, gp_ref, w_ref, b_ref, act_ref, ext):
        i = pl.program_id(1)
        ext[pl.ds(FHALO, tc), :] = g_ref[...]
        ext[pl.ds(0, FHALO), :] = jnp.where(i > 0, gp_ref[...], 0.0)
        gc = jnp.broadcast_to(b_ref[...], (tc, cb))
        for k in range(kk):
            gc = gc + w_ref[k:k + 1, :] * ext[pl.ds(off + k, tc), :]
        act_ref[...] = (gc * _sigmoid(gc) * v_ref[...]).astype(BF16)

    return pl.pallas_call(
        body, name="ffn_act_fwd", grid=(dff // cb, t // tc),
        in_specs=[pl.BlockSpec((None, tc, cb), lambda j, i: (0, i, j)),
                  pl.BlockSpec((None, tc, cb), lambda j, i: (1, i, j)),
                  pl.BlockSpec((None, FHALO, cb), lambda j, i: (0, jnp.maximum(i * hb - 1, 0), j)),
                  pl.BlockSpec((kk, cb), lambda j, i: (0, j)),
                  pl.BlockSpec((1, cb), lambda j, i: (0, j))],
        out_specs=pl.BlockSpec((tc, cb), lambda j, i: (i, j)),
        out_shape=jax.ShapeDtypeStruct((t, dff), BF16),
        scratch_shapes=[pltpu.VMEM((tc + FHALO, cb), F32)],
        compiler_params=_params("parallel", "arbitrary"),
    )(hh3, hh3, hh3, fw, fb)


def _ffn_act_bwd(dact, hh3, fw, fb, tc, cb):
    _, t, dff = hh3.shape
    kk = fw.shape[0]
    off = FHALO - (kk - 1)
    hb = tc // FHALO
    nt = t // tc
    te = tc + FHALO

    def body(da_ref, dan_ref, g_ref, gp_ref, gn_ref, v_ref, vn_ref, w_ref, b_ref,
             dhh_ref, dw_ref, db_ref, gext, dext, wacc, bacc):
        i = pl.program_id(1)

        @pl.when(i == 0)
        def _():
            wacc[...] = jnp.zeros_like(wacc)
            bacc[...] = jnp.zeros_like(bacc)

        gext[pl.ds(0, FHALO), :] = jnp.where(i > 0, gp_ref[...], 0.0)
        gext[pl.ds(FHALO, tc), :] = g_ref[...]
        gext[pl.ds(FHALO + tc, FHALO), :] = gn_ref[...]
        gc = jnp.broadcast_to(b_ref[...], (te, cb))
        for k in range(kk):
            gc = gc + w_ref[k:k + 1, :] * gext[pl.ds(off + k, te), :]
        sg = _sigmoid(gc)
        dsilu = sg * (1.0 + gc * (1.0 - sg))
        live = i < nt - 1
        da_cur = da_ref[...]
        dext[pl.ds(0, tc), :] = da_cur * v_ref[...] * dsilu[0:tc]
        dext[pl.ds(tc, FHALO), :] = jnp.where(live, dan_ref[...] * vn_ref[...] * dsilu[tc:te], 0.0)
        dhh_ref[1] = (da_cur * (gc[0:tc] * sg[0:tc])).astype(BF16)
        dg = jnp.zeros((tc, cb), F32)
        for k in range(kk):
            dg = dg + w_ref[k:k + 1, :] * dext[pl.ds((kk - 1) - k, tc), :]
        dhh_ref[0] = dg.astype(BF16)
        dgc = dext[pl.ds(0, tc), :]
        bacc[...] += jnp.sum(dgc.reshape(tc // SUBLANE, SUBLANE, cb), axis=0)
        for k in range(kk):
            prod = dgc * gext[pl.ds(off + k, tc), :]
            wacc[k] += jnp.sum(prod.reshape(tc // SUBLANE, SUBLANE, cb), axis=0)

        @pl.when(i == nt - 1)
        def _():
            for k in range(kk):
                dw_ref[k:k + 1, :] = jnp.sum(wacc[k], axis=0, keepdims=True)
            dw_ref[kk:SUBLANE, :] = jnp.zeros((SUBLANE - kk, cb), F32)
            db_ref[...] = jnp.sum(bacc[...], axis=0, keepdims=True)

    nxt = lambda i: jnp.minimum((i + 1) * hb, t // FHALO - 1)
    return pl.pallas_call(
        body, name="ffn_act_bwd", grid=(dff // cb, nt),
        in_specs=[pl.BlockSpec((tc, cb), lambda j, i: (i, j)),
                  pl.BlockSpec((FHALO, cb), lambda j, i: (nxt(i), j)),
                  pl.BlockSpec((None, tc, cb), lambda j, i: (0, i, j)),
                  pl.BlockSpec((None, FHALO, cb), lambda j, i: (0, jnp.maximum(i * hb - 1, 0), j)),
                  pl.BlockSpec((None, FHALO, cb), lambda j, i: (0, nxt(i), j)),
                  pl.BlockSpec((None, tc, cb), lambda j, i: (1, i, j)),
                  pl.BlockSpec((None, FHALO, cb), lambda j, i: (1, nxt(i), j)),
                  pl.BlockSpec((kk, cb), lambda j, i: (0, j)),
                  pl.BlockSpec((1, cb), lambda j, i: (0, j))],
        out_specs=[pl.BlockSpec((2, tc, cb), lambda j, i: (0, i, j)),
                   pl.BlockSpec((SUBLANE, cb), lambda j, i: (0, j)),
                   pl.BlockSpec((1, cb), lambda j, i: (0, j))],
        out_shape=[jax.ShapeDtypeStruct((2, t, dff), BF16), jax.ShapeDtypeStruct((SUBLANE, dff), F32),
                   jax.ShapeDtypeStruct((1, dff), F32)],
        scratch_shapes=[pltpu.VMEM((tc + 2 * FHALO, cb), F32), pltpu.VMEM((te, cb), F32),
                        pltpu.VMEM((kk, SUBLANE, cb), F32), pltpu.VMEM((SUBLANE, cb), F32)],
        compiler_params=_params("parallel", "arbitrary"),
    )(dact, dact, hh3, hh3, hh3, hh3, hh3, fw, fb)


def _chunk_consts():
    r = lax.broadcasted_iota(jnp.int32, (CHUNK, CHUNK), 0)
    c = lax.broadcasted_iota(jnp.int32, (CHUNK, CHUNK), 1)
    blk = (r // SUB) * SUB
    tri = (c <= r).astype(F32)
    start = (c < blk).astype(F32)
    end = (c < blk + SUB).astype(F32)
    return jnp.concatenate([tri, start, end, jnp.ones((SUBLANE, CHUNK), F32)], axis=0)


def _gate_terms(q, fpre, lb):
    sf = _sigmoid(fpre)
    fg = lb + (1.0 - lb) * sf
    sq = _sigmoid(q)
    return sf, fg, 1.0 - fg, sq, q * sq


def _decays(g, consts):
    cs = _dot3(consts, g)
    b = cs[0:CHUNK]
    rs = cs[CHUNK:2 * CHUNK]
    re = cs[2 * CHUNK:3 * CHUNK]
    tot = cs[3 * CHUNK:3 * CHUNK + 1]
    return b, rs, re, tot


def _lower_bound(lb_ref):
    l0, l1 = lb_ref[0:1, :], lb_ref[1:2, :]
    mx = jnp.maximum(l0, l1)
    e0, e1 = jnp.exp(l0 - mx), jnp.exp(l1 - mx)
    return e0 / (e0 + e1)


def _scaled_keys(kt, rs, re, rowblk, i):
    scale = jnp.where(rowblk < i, jnp.exp(jnp.minimum(rs[SUB * i:SUB * i + 1, :] - re, 0.0)), 0.0)
    return kt * scale, scale


def _off_diag_scores(qt, kt, rs, re, rowblk):
    a = jnp.zeros((CHUNK, CHUNK), F32)
    for i in range(1, CHUNK // SUB):
        ki, _ = _scaled_keys(kt, rs, re, rowblk, i)
        a = a + _dot_nt(jnp.where(rowblk == i, qt, 0.0).astype(BF16), ki.astype(BF16))
    return a


def _hgrn_fwd(p3, lb_logits, hg, cat, tb):
    _, t, w = p3.shape
    nh = w // LANE
    nc = tb // CHUNK

    def body(q_ref, f_ref, v_ref, og_ref, lb_ref, hg_ref, cat_in, cat_ref, o_ref, st_ref, state):
        del cat_in
        consts = _chunk_consts()
        lb = _lower_bound(lb_ref)
        hgv = hg_ref[...]
        rowblk = lax.broadcasted_iota(jnp.int32, (CHUNK, 1), 0) // SUB
        rowpos = lax.broadcasted_iota(jnp.int32, (CHUNK, 1), 0) % SUB

        @pl.when(pl.program_id(1) == 0)
        def _():
            state[...] = jnp.zeros_like(state)

        def chunk(c, carry):
            rows = pl.ds(pl.multiple_of(c * CHUNK, CHUNK), CHUNK)
            v = v_ref[rows, :]
            og = og_ref[rows, :]
            _, fg, kk, _, qh = _gate_terms(q_ref[rows, :], f_ref[rows, :], lb)
            b, rs, re, tot = _decays(jnp.log(fg), consts)
            qt = qh * jnp.exp(b - rs)
            kt = kk * jnp.exp(re - b)
            vb = v.astype(BF16)
            st = state[...]
            st_ref[c] = st
            o = _dot(_off_diag_scores(qt, kt, rs, re, rowblk).astype(BF16), vb)
            o = o + _dot_nt((qh * jnp.exp(b)).astype(BF16), st.astype(BF16))
            for d in range(SUB):
                ks, vs = (kk, v) if d == 0 else (pltpu.roll(kk, d, 0), pltpu.roll(v, d, 0))
                e = 1.0 if d == 0 else jnp.exp(jnp.where(rowpos >= d, b - pltpu.roll(b, d, 0), NEG_BIG))
                o = o + jnp.sum(qh * ks * e, axis=-1, keepdims=True) * vs
            k_up = kk * jnp.exp(tot - b)
            state[...] = st * jnp.exp(tot) + _dot_tn(vb, k_up.astype(BF16))
            o_ref[rows, :] = o
            r = lax.rsqrt(jnp.mean(o * o, axis=-1, keepdims=True) + RMS_EPS)
            cat_ref[rows, :] = (o * r * hgv * (og * _sigmoid(og))).astype(BF16)
            return carry

        lax.fori_loop(0, nc, chunk, 0)

    sec = lambda s: pl.BlockSpec((None, tb, LANE), lambda h, i: (s, i, h))
    return pl.pallas_call(
        body, name="hgrn_fwd", grid=(nh, t // tb),
        in_specs=[sec(2), sec(3), sec(4), sec(5),
                  pl.BlockSpec((2, LANE), lambda h, i: (0, h)),
                  pl.BlockSpec((1, LANE), lambda h, i: (0, h)), HBM],
        out_specs=[pl.BlockSpec((tb, LANE), lambda h, i: (i, nh + h)),
                   pl.BlockSpec((tb, LANE), lambda h, i: (i, h)),
                   pl.BlockSpec((None, nc, LANE, LANE), lambda h, i: (h, i, 0, 0))],
        out_shape=[jax.ShapeDtypeStruct(cat.shape, BF16), jax.ShapeDtypeStruct((t, w), F32),
                   jax.ShapeDtypeStruct((nh, t // CHUNK, LANE, LANE), F32)],
        scratch_shapes=[pltpu.VMEM((LANE, LANE), F32)],
        input_output_aliases={6: 0},
        compiler_params=_params("parallel", "arbitrary"),
    )(p3, p3, p3, p3, lb_logits, hg, cat)


def _hgrn_bwd(p3, lb_logits, hg, o_pre, states, dcat, dp3, tb):
    n_sec, t, w = p3.shape
    nh = w // LANE
    nc = tb // CHUNK
    nb = t // tb

    def body(q_ref, f_ref, v_ref, og_ref, lb_ref, hg_ref, o_ref, st_ref, dc_ref, dp_in,
             dp_ref, dlb_ref, dhg_ref, dstate, stash, lbacc, hgacc):
        del dp_in
        i, half = pl.program_id(1), pl.program_id(2)

        @pl.when(half == 1)
        def _():
            dp_ref[...] = stash[...]

        @pl.when(half == 0)
        def _():
            consts = _chunk_consts()
            rr = lax.broadcasted_iota(jnp.int32, (CHUNK, CHUNK), 0)
            cc = lax.broadcasted_iota(jnp.int32, (CHUNK, CHUNK), 1)
            upper = (cc >= rr).astype(F32)
            lb = _lower_bound(lb_ref)
            hgv = hg_ref[...]
            rowblk = lax.broadcasted_iota(jnp.int32, (CHUNK, 1), 0) // SUB
            rowpos = lax.broadcasted_iota(jnp.int32, (CHUNK, 1), 0) % SUB

            @pl.when(i == 0)
            def _():
                dstate[...] = jnp.zeros_like(dstate)
                lbacc[...] = jnp.zeros_like(lbacc)
                hgacc[...] = jnp.zeros_like(hgacc)

            def chunk(cr, carry):
                c = nc - 1 - cr
                rows = pl.ds(pl.multiple_of(c * CHUNK, CHUNK), CHUNK)
                q = q_ref[rows, :]
                v = v_ref[rows, :]
                og = og_ref[rows, :]
                o = o_ref[rows, :]
                dcg = dc_ref[rows, :]
                sf, fg, kk, sq, qh = _gate_terms(q, f_ref[rows, :], lb)
                b, rs, re, tot = _decays(jnp.log(fg), consts)
                eq = jnp.exp(b - rs)
                ek = jnp.exp(re - b)
                qt = qh * eq
                kt = kk * ek
                e_in = jnp.exp(b)
                e_up = jnp.exp(tot - b)
                e_tot = jnp.exp(tot)
                q_in = (qh * e_in).astype(BF16)
                k_up = (kk * e_up).astype(BF16)
                vb = v.astype(BF16)
                st = st_ref[c]
                dst = dstate[...]
                dstb = dst.astype(BF16)

                sg = _sigmoid(og)
                r = lax.rsqrt(jnp.mean(o * o, axis=-1, keepdims=True) + RMS_EPS)
                ohat = o * r
                d_og = dcg * ohat * hgv * (sg * (1.0 + og * (1.0 - sg)))
                d_on = dcg * (og * sg)
                hgacc[...] += jnp.sum((d_on * ohat).reshape(CHUNK // SUBLANE, SUBLANE, LANE), axis=0)
                d_oh = d_on * hgv
                do = r * (d_oh - ohat * jnp.mean(d_oh * ohat, axis=-1, keepdims=True))
                dob = do.astype(BF16)

                da = _dot_nt(dob, vb)
                a_off = jnp.zeros((CHUNK, CHUNK), F32)
                dqt = jnp.zeros((CHUNK, LANE), F32)
                dkt = jnp.zeros((CHUNK, LANE), F32)
                for blk in range(1, CHUNK // SUB):
                    ki, scale = _scaled_keys(kt, rs, re, rowblk, blk)
                    kib = ki.astype(BF16)
                    qib = jnp.where(rowblk == blk, qt, 0.0).astype(BF16)
                    dab = jnp.where(rowblk == blk, da, 0.0).astype(BF16)
                    a_off = a_off + _dot_nt(qib, kib)
                    dqt = dqt + _dot(dab, kib)
                    dkt = dkt + _dot_tn(dab, qib) * scale
                dqh = dqt * eq
                dk = dkt * ek
                dv = _dot_tn(a_off.astype(BF16), dob)

                dqh = dqh + _dot(dob, st.astype(BF16)) * e_in
                dk = dk + _dot(vb, dstb) * e_up
                dv = dv + _dot_nt(k_up, dstb)
                st_end = st * e_tot + _dot_tn(vb, k_up)
                carry_g = jnp.sum(st_end * dst, axis=0, keepdims=True)
                dstate[...] = dst * e_tot + _dot_tn(dob, q_in)

                for d in range(SUB):
                    if d == 0:
                        ks, vs, e = kk, v, 1.0
                    else:
                        ks, vs = pltpu.roll(kk, d, 0), pltpu.roll(v, d, 0)
                        e = jnp.exp(jnp.where(rowpos >= d, b - pltpu.roll(b, d, 0), NEG_BIG))
                    a_d = jnp.sum(qh * ks * e, axis=-1, keepdims=True)
                    da_d = jnp.sum(do * vs, axis=-1, keepdims=True) * e
                    dqh = dqh + da_d * ks
                    ck = da_d * qh
                    cv = a_d * do
                    if d == 0:
                        dk = dk + ck
                        dv = dv + cv
                    else:
                        dk = dk + pltpu.roll(ck, CHUNK - d, 0)
                        dv = dv + pltpu.roll(cv, CHUNK - d, 0)

                dg = _dot3(upper, qh * dqh - kk * dk) + carry_g
                dfg = dg / fg - dk
                lbacc[...] += jnp.sum((dfg * (1.0 - sf)).reshape(CHUNK // SUBLANE, SUBLANE, LANE), axis=0)
                dp_ref[0, rows, :] = (dqh * (sq * (1.0 + q * (1.0 - sq)))).astype(BF16)
                dp_ref[1, rows, :] = (dfg * (1.0 - lb) * sf * (1.0 - sf)).astype(BF16)
                stash[0, rows, :] = dv.astype(BF16)
                stash[1, rows, :] = d_og.astype(BF16)
                return carry

            lax.fori_loop(0, nc, chunk, 0)

            @pl.when(i == nb - 1)
            def _():
                dlb_ref[...] = jnp.sum(lbacc[...], axis=0, keepdims=True)
                dhg_ref[...] = jnp.sum(hgacc[...], axis=0, keepdims=True)

    rev = lambda i: nb - 1 - i
    sec = lambda s: pl.BlockSpec((None, tb, LANE), lambda h, i, z: (s, rev(i), h))
    return pl.pallas_call(
        body, name="hgrn_bwd", grid=(nh, nb, 2),
        in_specs=[sec(2), sec(3), sec(4), sec(5),
                  pl.BlockSpec((2, LANE), lambda h, i, z: (0, h)),
                  pl.BlockSpec((1, LANE), lambda h, i, z: (0, h)),
                  pl.BlockSpec((tb, LANE), lambda h, i, z: (rev(i), h)),
                  pl.BlockSpec((None, nc, LANE, LANE), lambda h, i, z: (h, rev(i), 0, 0)),
                  pl.BlockSpec((tb, LANE), lambda h, i, z: (rev(i), nh + h)), HBM],
        out_specs=[pl.BlockSpec((2, tb, LANE), lambda h, i, z: (1 + z, rev(i), h)),
                   pl.BlockSpec((1, LANE), lambda h, i, z: (0, h)),
                   pl.BlockSpec((1, LANE), lambda h, i, z: (0, h))],
        out_shape=[jax.ShapeDtypeStruct((n_sec, t, w), BF16), jax.ShapeDtypeStruct((1, w), F32),
                   jax.ShapeDtypeStruct((1, w), F32)],
        scratch_shapes=[pltpu.VMEM((LANE, LANE), F32), pltpu.VMEM((2, tb, LANE), BF16),
                        pltpu.VMEM((SUBLANE, LANE), F32), pltpu.VMEM((SUBLANE, LANE), F32)],
        input_output_aliases={9: 0},
        compiler_params=_params("parallel", "arbitrary", "arbitrary"),
    )(p3, p3, p3, p3, lb_logits, hg, o_pre, states, dcat, dp3)


def _place():
    x, y, c = lax.axis_index("x"), lax.axis_index("y"), lax.axis_index("c")
    chips = [(1 - x, y), (x, 1 - y), (1 - x, 1 - y)]
    return x, y, c, chips


def _gather_weights(bufs):
    n = len(bufs)

    def body(*refs):
        outs = refs[n:2 * n]
        send, recv = refs[2 * n:]
        x, y, c, chips = _place()
        me, sibling = (x, y, c), (x, y, 1 - c)

        def rows(k, px, py, pc):
            half = outs[k].shape[1] // 2
            return outs[k].at[2 * px + py, pl.ds(pc * half, half)]

        def copy(k, j, blk, to):
            return pltpu.make_async_remote_copy(
                src_ref=rows(k, *blk), dst_ref=rows(k, *blk),
                send_sem=send.at[k, j], recv_sem=recv.at[k, j], device_id=to, device_id_type=MESH)

        first = []
        for k in range(n):
            for j, chip in enumerate(chips):
                first.append(copy(k, j, me, (*chip, c)))
                first[-1].start()
        passed = []
        for k in range(n):
            for j, chip in enumerate(chips):
                copy(k, j, (*chip, c), me).wait_recv()
                passed.append(copy(k, 3 + j, (*chip, c), sibling))
                passed[-1].start()
        for k in range(n):
            for j, chip in enumerate(chips):
                copy(k, 3 + j, (*chip, 1 - c), me).wait_recv()
        for cp in first + passed:
            cp.wait_send()

    return pl.pallas_call(
        body, name="gather_weights",
        in_specs=[HBM] * n, out_specs=[HBM] * n,
        out_shape=[jax.ShapeDtypeStruct(b.shape, b.dtype) for b in bufs],
        scratch_shapes=[pltpu.SemaphoreType.DMA((n, 6)), pltpu.SemaphoreType.DMA((n, 6))],
        input_output_aliases={k: k for k in range(n)},
    )(*bufs)


def _swap_halves(grads):
    n = len(grads)

    def body(*refs):
        ins, outs = refs[:n], refs[n:2 * n]
        send, recv = refs[2 * n:]
        x, y, c, _ = _place()
        cps = []
        for k in range(n):
            half = ins[k].shape[1] // 2
            cp = pltpu.make_async_remote_copy(
                src_ref=ins[k].at[:, pl.ds((1 - c) * half, half)], dst_ref=outs[k],
                send_sem=send.at[k], recv_sem=recv.at[k], device_id=(x, y, 1 - c), device_id_type=MESH)
            cp.start()
            cps.append(cp)
        for cp in cps:
            cp.wait()

    return pl.pallas_call(
        body, name="swap_halves", in_specs=[HBM] * n, out_specs=[HBM] * n,
        out_shape=[jax.ShapeDtypeStruct((g.shape[0], g.shape[1] // 2, g.shape[2]), g.dtype) for g in grads],
        scratch_shapes=[pltpu.SemaphoreType.DMA((n,)), pltpu.SemaphoreType.DMA((n,))],
    )(*grads)


def _add_halves(name, g, other, c_idx):
    s, r, cols = g.shape
    half = r // 2
    tr = _div_tile(half, 16, 512)
    nb = half // tr

    def body(c_ref, g_ref, o_ref, q_ref):
        del c_ref
        q_ref[...] = (g_ref[...] + o_ref[...]).astype(BF16)

    return pl.pallas_call(
        body, name=name,
        grid_spec=pltpu.PrefetchScalarGridSpec(
            num_scalar_prefetch=1, grid=(s, nb),
            in_specs=[pl.BlockSpec((None, tr, cols), lambda k, i, c: (k, c[0] * nb + i, 0)),
                      pl.BlockSpec((None, tr, cols), lambda k, i, c: (k, i, 0))],
            out_specs=pl.BlockSpec((None, tr, cols), lambda k, i, c: (k, i, 0))),
        out_shape=jax.ShapeDtypeStruct((s, half, cols), BF16),
        compiler_params=_params("parallel", "parallel"),
    )(c_idx, g, other)


def _send_partials(parts):
    n = len(parts)

    def body(*refs):
        ins, outs = refs[:n], refs[n:2 * n]
        send, recv = refs[2 * n:]
        x, y, c, chips = _place()
        cps = []
        for k in range(n):
            for j, (px, py) in enumerate(chips):
                cp = pltpu.make_async_remote_copy(
                    src_ref=ins[k].at[2 * px + py], dst_ref=outs[k].at[j],
                    send_sem=send.at[k, j], recv_sem=recv.at[k, j], device_id=(px, py, c), device_id_type=MESH)
                cp.start()
                cps.append(cp)
        for cp in cps:
            cp.wait()

    return pl.pallas_call(
        body, name="send_partials", in_specs=[HBM] * n, out_specs=[HBM] * n,
        out_shape=[jax.ShapeDtypeStruct((3,) + p.shape[1:], p.dtype) for p in parts],
        scratch_shapes=[pltpu.SemaphoreType.DMA((n, 3)), pltpu.SemaphoreType.DMA((n, 3))],
    )(*parts)


def _sum_partials(name, part, arrived, place_idx):
    _, half, cols = part.shape
    tr = _div_tile(half, 16, 512)
    nb = half // tr

    def body(s_ref, p_ref, a_ref, o_ref):
        del s_ref
        o_ref[...] = ((p_ref[...].astype(F32) + a_ref[0].astype(F32)) + a_ref[1].astype(F32)) + a_ref[2].astype(F32)

    return pl.pallas_call(
        body, name=name,
        grid_spec=pltpu.PrefetchScalarGridSpec(
            num_scalar_prefetch=1, grid=(nb,),
            in_specs=[pl.BlockSpec((None, tr, cols), lambda i, s: (s[0], i, 0)),
                      pl.BlockSpec((3, tr, cols), lambda i, s: (0, i, 0))],
            out_specs=pl.BlockSpec((tr, cols), lambda i, s: (s[1] * nb + i, 0))),
        out_shape=jax.ShapeDtypeStruct((2 * half, cols), F32),
        compiler_params=_params("parallel"),
    )(place_idx, part, arrived)


def _join_halves(bufs):
    n = len(bufs)

    def body(*refs):
        outs = refs[n:2 * n]
        send, recv = refs[2 * n:]
        x, y, c, _ = _place()
        cps = []
        for k in range(n):
            half = outs[k].shape[0] // 2
            mine = outs[k].at[pl.ds(c * half, half)]
            cp = pltpu.make_async_remote_copy(
                src_ref=mine, dst_ref=mine, send_sem=send.at[k], recv_sem=recv.at[k],
                device_id=(x, y, 1 - c), device_id_type=MESH)
            cp.start()
            cps.append(cp)
        for k, cp in enumerate(cps):
            cp.wait_send()
            half = outs[k].shape[0] // 2
            theirs = outs[k].at[pl.ds((1 - c) * half, half)]
            pltpu.make_async_remote_copy(
                src_ref=theirs, dst_ref=theirs, send_sem=send.at[k], recv_sem=recv.at[k],
                device_id=(x, y, 1 - c), device_id_type=MESH).wait_recv()

    return pl.pallas_call(
        body, name="join_halves", in_specs=[HBM] * n, out_specs=[HBM] * n,
        out_shape=[jax.ShapeDtypeStruct(b.shape, b.dtype) for b in bufs],
        scratch_shapes=[pltpu.SemaphoreType.DMA((n,)), pltpu.SemaphoreType.DMA((n,))],
        input_output_aliases={k: k for k in range(n)},
    )(*bufs)


def _small_allreduce(wide_rows, ffn_rows, w, dff, n_wide, n_ffn):
    n_in = len(wide_rows) + len(ffn_rows)

    def body(*refs):
        ins = refs[:n_in]
        s1_ref, s2_ref, r1, r2, p1, p2, send, recv = refs[n_in:]
        x, y, c, _ = _place()
        me = 4 * x + 2 * y + c
        p1[...] = jnp.zeros_like(p1)
        p2[...] = jnp.zeros_like(p2)
        row = 0
        for ref, (_, r, m) in zip(ins, wide_rows):
            if m == 1 and r % SUBLANE == 0 and row % SUBLANE == 0:
                p1[row:row + r, :] = ref[...]
                row += r
                continue
            for rr in range(r):
                for mm in range(m):
                    p1[row:row + 1, :] = ref[rr:rr + 1, mm * w:(mm + 1) * w]
                    row += 1
        row = 0
        for ref, arr in zip(ins[len(wide_rows):], ffn_rows):
            r = arr.shape[0]
            p2[row:row + r, :] = ref[...]
            row += r
        r1[me] = p1[...]
        r2[me] = p2[...]
        cps = []
        for mask in range(1, 8):
            peer = (x ^ (mask >> 2), y ^ ((mask >> 1) & 1), c ^ (mask & 1))
            for a, (src, dst) in enumerate(((p1, r1), (p2, r2))):
                cp = pltpu.make_async_remote_copy(
                    src_ref=src, dst_ref=dst.at[me], send_sem=send.at[a, mask - 1], recv_sem=recv.at[a, mask - 1],
                    device_id=peer, device_id_type=MESH)
                cp.start()
                cps.append(cp)
        for cp in cps:
            cp.wait()
        t1, t2 = r1[0], r2[0]
        for d in range(1, 8):
            t1 = t1 + r1[d]
            t2 = t2 + r2[d]
        s1_ref[...] = t1
        s2_ref[...] = t2

    ins = [a for a, _, _ in wide_rows] + list(ffn_rows)
    return pl.pallas_call(
        body, name="small_allreduce", in_specs=[VMEM_FULL] * n_in, out_specs=[VMEM_FULL, VMEM_FULL],
        out_shape=[jax.ShapeDtypeStruct((n_wide, w), F32), jax.ShapeDtypeStruct((n_ffn, dff), F32)],
        scratch_shapes=[pltpu.VMEM((8, n_wide, w), F32), pltpu.VMEM((8, n_ffn, dff), F32),
                        pltpu.VMEM((n_wide, w), F32), pltpu.VMEM((n_ffn, dff), F32),
                        pltpu.SemaphoreType.DMA((2, 7)), pltpu.SemaphoreType.DMA((2, 7))],
        compiler_params=pltpu.CompilerParams(vmem_limit_bytes=VMEM_LIMIT),
    )(*ins)


def _adamw(w, g, m, v):
    m2 = ADAM_B1 * m + (1.0 - ADAM_B1) * g
    v2 = ADAM_B2 * v + (1.0 - ADAM_B2) * (g * g)
    m_hat = m2 / (1.0 - ADAM_B1 ** ADAM_STEP)
    v_hat = v2 / (1.0 - ADAM_B2 ** ADAM_STEP)
    delta = -ADAM_LR * (m_hat / (jnp.sqrt(v_hat) + ADAM_EPS) + ADAM_WD * w)
    return delta, m2, v2


def _adam_big(name, w, g, m, v):
    r, c = w.shape
    tr = 128 if r % 128 == 0 else r

    def body(w_ref, g_ref, m_ref, v_ref, d_ref, m2_ref, v2_ref):
        d_ref[...], m2_ref[...], v2_ref[...] = _adamw(w_ref[...], g_ref[...], m_ref[...], v_ref[...])

    blk = pl.BlockSpec((tr, c), lambda i: (i, 0))
    return pl.pallas_call(
        body, name=name, grid=(r // tr,), in_specs=[blk] * 4, out_specs=[blk] * 3,
        out_shape=[jax.ShapeDtypeStruct((r, c), F32)] * 3,
        compiler_params=_params("parallel"),
    )(w, g, m, v)


def _adam_small(s1, s2, cw_g, fw_g, lb_logits, triples, layout, w):
    n = len(triples)

    def body(*refs):
        s1_ref, s2_ref, cw_ref, fw_ref, lbl_ref = refs[:5]
        prm = refs[5:5 + 3 * n]
        outs = refs[5 + 3 * n:]
        for p, lay in enumerate(layout):
            w_ref, m_ref, v_ref = prm[3 * p:3 * p + 3]
            g_ref, d_ref, m2_ref, v2_ref = outs[4 * p:4 * p + 4]
            if lay[0] == "wide":
                _, row, r, pieces = lay
                for rr in range(r):
                    for mm in range(pieces):
                        g_ref[rr:rr + 1, mm * w:(mm + 1) * w] = s1_ref[row:row + 1, :]
                        row += 1
            elif lay[0] == "ffn":
                _, row, r = lay
                g_ref[...] = s2_ref[row:row + r, :]
            elif lay[0] == "cw":
                g_ref[...] = cw_ref[0:g_ref.shape[0], :]
            elif lay[0] == "fw":
                g_ref[...] = fw_ref[0:g_ref.shape[0], :]
            else:
                s0 = _lower_bound(lbl_ref)
                d0 = s1_ref[lay[1]:lay[1] + 1, :] * s0 * (1.0 - s0)
                g_ref[0:1, :] = d0
                g_ref[1:2, :] = -d0
            d_ref[...], m2_ref[...], v2_ref[...] = _adamw(w_ref[...], g_ref[...], m_ref[...], v_ref[...])

    flat = [a for tr in triples for a in tr]
    shapes = []
    for tr in triples:
        shapes.extend([jax.ShapeDtypeStruct(tr[0].shape, F32)] * 4)
    return pl.pallas_call(
        body, name="adam_small", in_specs=[VMEM_FULL] * (5 + 3 * n), out_specs=[VMEM_FULL] * (4 * n),
        out_shape=shapes, compiler_params=pltpu.CompilerParams(vmem_limit_bytes=VMEM_LIMIT),
    )(s1, s2, cw_g, fw_g, lb_logits, *flat)


def _row_tile(t):
    return 512 if t % 512 == 0 and t >= 2048 else 128


def kernel(x, emb_ln_g, emb_ln_b, w_in, conv_w, conv_b, conv_norm_g, conv_norm_b, lb_logits, hgrn_norm_g, w_out, ln1_g, ln1_b, w_ffn_up, ffn_conv_w, ffn_conv_b, w_ffn_down, ln2_g, ln2_b, loss_target, m_emb_ln_g, m_emb_ln_b, m_w_in, m_conv_w, m_conv_b, m_conv_norm_g, m_conv_norm_b, m_lb_logits, m_hgrn_norm_g, m_w_out, m_ln1_g, m_ln1_b, m_w_ffn_up, m_ffn_conv_w, m_ffn_conv_b, m_w_ffn_down, m_ln2_g, m_ln2_b, v_emb_ln_g, v_emb_ln_b, v_w_in, v_conv_w, v_conv_b, v_conv_norm_g, v_conv_norm_b, v_lb_logits, v_hgrn_norm_g, v_w_out, v_ln1_g, v_ln1_b, v_w_ffn_up, v_ffn_conv_w, v_ffn_conv_b, v_w_ffn_down, v_ln2_g, v_ln2_b):
    depth = w_in.shape[0]
    assert depth == 1 and x.shape[0] == 1
    alpha = (2.0 * depth) ** 0.25
    t, d = x.shape[1], x.shape[2]
    w = d // 2
    dff = ffn_conv_b.shape[1]
    kc = conv_w.shape[1]
    assert w % (2 * LANE) == 0 and dff % (4 * LANE) == 0 and t % 128 == 0
    tm = _row_tile(t)
    tm2 = tm // 2
    cb = 2 * LANE
    cbf = 4 * LANE
    tb = tm

    xi = lax.axis_index("x")
    yi = lax.axis_index("y")
    ci = lax.axis_index("c")
    chip = 2 * xi + yi
    c_idx = jnp.reshape(ci, (1,)).astype(jnp.int32)
    chip_idx = jnp.reshape(chip, (1,)).astype(jnp.int32)
    place_idx = jnp.stack([chip, ci]).astype(jnp.int32)

    x2 = x[0]
    tgt = loss_target[0]
    g0, b0 = emb_ln_g.reshape(1, d), emb_ln_b.reshape(1, d)
    w_in2, w_out2, w_up2, w_dn2 = w_in[0], w_out[0], w_ffn_up[0], w_ffn_down[0]
    cw2, fw2 = conv_w[0], ffn_conv_w[0]

    full = _gather_weights([_place_shard(w_in2, "place_w_in", chip_idx, BF16),
                            _place_shard(w_out2, "place_w_out", chip_idx, BF16),
                            _place_shard(w_up2, "place_w_up", chip_idx, BF16),
                            _place_shard(w_dn2, "place_w_down", chip_idx, BF16),
                            _place_shard(_pad_rows(cw2), "place_conv_w", chip_idx, F32),
                            _place_shard(_pad_rows(fw2), "place_ffn_conv_w", chip_idx, F32)])
    w_in3, w_out3, w_up3, w_dn3, cw_full3, fw_full3 = full
    w_out_full = w_out3.reshape(d, d)
    cw_full = _unshard_cols(cw_full3)[:kc]
    fw_full = _unshard_cols(fw_full3)[:fw2.shape[0]]

    h0b = _ln0(x2, g0, b0, tm)
    p3 = _proj("in_proj", h0b, w_in3, 6, tm, w // 2)
    cat, u1 = _conv_fwd(p3, cw_full, conv_b, conv_norm_g, conv_norm_b, tm2, cb)
    cat, o_pre, states = _hgrn_fwd(p3, lb_logits, hgrn_norm_g, cat, tb)
    xhat1, h1b, rstd1 = _mix_ln1(cat, w_out_full, x2, g0, b0, ln1_g, ln1_b, alpha, tm2)
    hh3 = _proj("ffn_up", h1b, w_up3, 2, tm, dff // 4)
    act = _ffn_act_fwd(hh3, fw_full, ffn_conv_b, tm, cbf)
    dz2, dz2b, dg2, db2, loss_row = _down_ln2_loss(act, w_dn3, xhat1, tgt, ln1_g, ln1_b, ln2_g, ln2_b, alpha, tm2)

    ks = dff // N_CHIPS
    dact = _proj_t("ffn_down_t", dz2b, w_dn3.reshape(dff, d), tm, ks)
    dhh3, dfw, dfb = _ffn_act_bwd(dact, hh3, fw_full, ffn_conv_b, tm, cbf)
    tt = tm
    d_w_dn = _wgrad("wgrad_down", act, dz2b, (N_CHIPS, ks, d), (N_CHIPS, 2, t // tt),
                    pl.BlockSpec((tt, ks), lambda s, j, k: (k, s)),
                    pl.BlockSpec((tt, d // 2), lambda s, j, k: (k, j)),
                    pl.BlockSpec((None, ks, d // 2), lambda s, j, k: (s, 0, j)))
    dz1, dz1b, dg1, db1 = _up_t_ln1_bwd(dhh3, w_up3, dz2, xhat1, rstd1, ln1_g, alpha, tm2)
    wu = 2 * dff // N_CHIPS
    tnu = wu // 2
    per_sec_u = dff // tnu
    d_w_up = _wgrad("wgrad_up", h1b, dhh3, (N_CHIPS, d, wu), (N_CHIPS, 2, 2, t // tt),
                    pl.BlockSpec((tt, d // 2), lambda s, r, j, k: (k, r)),
                    pl.BlockSpec((None, tt, tnu), lambda s, r, j, k: ((2 * s + j) // per_sec_u, k, (2 * s + j) % per_sec_u)),
                    pl.BlockSpec((None, d // 2, tnu), lambda s, r, j, k: (s, r, j)))
    dcat = _proj_t("out_proj_t", dz1b, w_out_full, tm, d // 2)
    d_w_out = _wgrad("wgrad_out", cat, dz1b, (d, d), (2, 2, t // tt),
                     pl.BlockSpec((tt, d // 2), lambda r, j, k: (k, r)),
                     pl.BlockSpec((tt, d // 2), lambda r, j, k: (k, j)),
                     pl.BlockSpec((d // 2, d // 2), lambda r, j, k: (r, j)))
    du1, dcng, dcnb = _conv_norm_bwd(dcat, u1, conv_norm_g, conv_norm_b, tm)
    dp3, dcw, dcb = _conv_bwd(du1, p3, cw_full, tm2, cb)
    dp3, dlb, dhg = _hgrn_bwd(p3, lb_logits, hgrn_norm_g, o_pre, states, dcat, dp3, tb)
    dx, dg0, db0 = _in_t_ln0_bwd(dp3, w_in3, dz1, x2, g0, alpha, tm2)
    wi = 6 * w // N_CHIPS
    tni = w // 2
    d_w_in = _wgrad("wgrad_in", h0b, dp3, (N_CHIPS, d, wi), (N_CHIPS, 2, wi // tni, t // tt),
                    pl.BlockSpec((tt, d // 2), lambda s, r, j, k: (k, r)),
                    pl.BlockSpec((None, tt, tni), lambda s, r, j, k: (((wi // tni) * s + j) // 2, k, ((wi // tni) * s + j) % 2)),
                    pl.BlockSpec((None, d // 2, tni), lambda s, r, j, k: (s, r, j)))

    big = [d_w_in, d_w_out.reshape(N_CHIPS, d // N_CHIPS, d), d_w_up, d_w_dn]
    names = ["w_in", "w_out", "w_up", "w_down"]
    arrived = _swap_halves(big)
    parts = [_add_halves("add_halves_" + nm, g, a, c_idx) for nm, g, a in zip(names, big, arrived)]
    landed = _send_partials(parts)
    halves = [_sum_partials("sum_partials_" + nm, p, a, place_idx) for nm, p, a in zip(names, parts, landed)]
    g_w_in, g_w_out, g_w_up, g_w_dn = _join_halves(halves)

    kpad = dcw.shape[0]
    wide = [(dcw, kpad, 1), (dg0, 1, 2), (db0, 1, 2), (dg1, 1, 2), (db1, 1, 2), (dg2, 1, 2), (db2, 1, 2),
            (dcb, 1, 1), (dcng, 1, 1), (dcnb, 1, 1), (dlb, 1, 1), (dhg, 1, 1)]
    n_wide = sum(r * m for _, r, m in wide)
    n_wide_pad = -(-n_wide // SUBLANE) * SUBLANE
    s1, s2 = _small_allreduce(wide, [dfw, dfb], w, dff, n_wide_pad, 2 * SUBLANE)
    cw_g = lax.dynamic_slice_in_dim(s1[0:kpad], chip * (w // N_CHIPS), w // N_CHIPS, axis=1)
    fw_g = lax.dynamic_slice_in_dim(s2[0:SUBLANE], chip * (dff // N_CHIPS), dff // N_CHIPS, axis=1)

    small = [
        (g0, m_emb_ln_g.reshape(1, d), v_emb_ln_g.reshape(1, d)), (b0, m_emb_ln_b.reshape(1, d), v_emb_ln_b.reshape(1, d)),
        (cw2, m_conv_w[0], v_conv_w[0]), (conv_b, m_conv_b, v_conv_b),
        (conv_norm_g, m_conv_norm_g, v_conv_norm_g), (conv_norm_b, m_conv_norm_b, v_conv_norm_b),
        (lb_logits, m_lb_logits, v_lb_logits), (hgrn_norm_g, m_hgrn_norm_g, v_hgrn_norm_g),
        (ln1_g, m_ln1_g, v_ln1_g), (ln1_b, m_ln1_b, v_ln1_b),
        (fw2, m_ffn_conv_w[0], v_ffn_conv_w[0]), (ffn_conv_b, m_ffn_conv_b, v_ffn_conv_b),
        (ln2_g, m_ln2_g, v_ln2_g), (ln2_b, m_ln2_b, v_ln2_b),
    ]
    r0 = kpad
    layout = [("wide", r0, 1, 2), ("wide", r0 + 2, 1, 2), ("cw",), ("wide", r0 + 12, 1, 1), ("wide", r0 + 13, 1, 1),
              ("wide", r0 + 14, 1, 1), ("lb", r0 + 15), ("wide", r0 + 16, 1, 1), ("wide", r0 + 4, 1, 2),
              ("wide", r0 + 6, 1, 2), ("fw",), ("ffn", SUBLANE, 1), ("wide", r0 + 8, 1, 2), ("wide", r0 + 10, 1, 2)]
    so = _adam_small(s1, s2, cw_g, fw_g, lb_logits, small, layout, w)
    sm = {nm: so[4 * i:4 * i + 4] for i, nm in enumerate(
        ["emb_ln_g", "emb_ln_b", "conv_w", "conv_b", "conv_norm_g", "conv_norm_b", "lb_logits", "hgrn_norm_g",
         "ln1_g", "ln1_b", "ffn_conv_w", "ffn_conv_b", "ln2_g", "ln2_b"])}
    bigs = {}
    for nm, wt, g, m, v in (("w_in", w_in2, g_w_in, m_w_in[0], v_w_in[0]), ("w_out", w_out2, g_w_out, m_w_out[0], v_w_out[0]),
                            ("w_ffn_up", w_up2, g_w_up, m_w_ffn_up[0], v_w_ffn_up[0]),
                            ("w_ffn_down", w_dn2, g_w_dn, m_w_ffn_down[0], v_w_ffn_down[0])):
        bigs[nm] = (g,) + tuple(_adam_big("adam_" + nm, wt, g, m, v))

    loss = lax.psum(loss_row[0, 0], ("x", "y", "c"))

    order = ["emb_ln_g", "emb_ln_b", "w_in", "conv_w", "conv_b", "conv_norm_g", "conv_norm_b", "lb_logits",
             "hgrn_norm_g", "w_out", "ln1_g", "ln1_b", "w_ffn_up", "ffn_conv_w", "ffn_conv_b", "w_ffn_down",
             "ln2_g", "ln2_b"]
    shapes = dict(emb_ln_g=emb_ln_g.shape, emb_ln_b=emb_ln_b.shape, w_in=w_in.shape, conv_w=conv_w.shape,
                  w_out=w_out.shape, w_ffn_up=w_ffn_up.shape, ffn_conv_w=ffn_conv_w.shape, w_ffn_down=w_ffn_down.shape)
    outs = [loss, dx.reshape(x.shape)]
    for which in range(4):
        for nm in order:
            a = bigs[nm][which] if nm in bigs else sm[nm][which]
            outs.append(a.reshape(shapes[nm]) if nm in shapes else a)
    return tuple(outs)


def _pad_rows(a):
    k = a.shape[0]
    kp = -(-k // 16) * 16
    return jnp.pad(a, ((0, kp - k), (0, 0)))


def _unshard_cols(a3):
    s, k, c = a3.shape
    return jnp.transpose(a3, (1, 0, 2)).reshape(k, s * c)
```

```python
import functools

import jax
import jax.numpy as jnp
from jax import lax
from jax.experimental import pallas as pl
from jax.experimental.pallas import tpu as pltpu

F32 = jnp.float32
BF16 = jnp.bfloat16

LN_EPS = 1e-5
RMS_EPS = 1e-6
LANE = 128
SUBLANE = 8
CHUNK = 64
SUB = 8
HALO = 32
FHALO = 8
ROWS = 64
N_CHIPS = 4
VMEM_LIMIT = 56 << 20
NEG_BIG = -1e30

ADAM_LR = 0.001
ADAM_B1 = 0.9
ADAM_B2 = 0.999
ADAM_EPS = 1e-08
ADAM_WD = 0.01
ADAM_STEP = 10

MESH = pl.DeviceIdType.MESH
HBM = pl.BlockSpec(memory_space=pl.ANY)
VMEM_FULL = pl.BlockSpec(memory_space=pltpu.VMEM)


def _params(*sem):
    return pltpu.CompilerParams(dimension_semantics=sem, vmem_limit_bytes=VMEM_LIMIT)


class _Rider:
    def __init__(self, ins, outs, aliases, n_sems, start, finish):
        self.ins, self.outs, self.aliases = list(ins), list(outs), dict(aliases)
        self.n_sems, self.start, self.finish = n_sems, start, finish


def _call(body, args, *, name, grid, in_specs, out_specs, out_shape, scratch_shapes=(), aliases=None, rider=None):
    many = isinstance(out_shape, (list, tuple))
    shapes = list(out_shape) if many else [out_shape]
    ospecs = list(out_specs) if many else [out_specs]
    aliases = dict(aliases or {})
    sem = ("arbitrary",) * len(grid)
    if rider is None:
        res = pl.pallas_call(
            body, name=name, grid=grid, in_specs=list(in_specs), out_specs=ospecs, out_shape=shapes,
            scratch_shapes=list(scratch_shapes), input_output_aliases=aliases, compiler_params=_params(*sem))(*args)
        return res if many else res[0]
    n_in, n_out, n_scr = len(args), len(shapes), len(scratch_shapes)
    nri, nro = len(rider.ins), len(rider.outs)

    def wrapped(*refs):
        ins, rin = refs[:n_in], refs[n_in:n_in + nri]
        o0 = n_in + nri
        outs, rout = refs[o0:o0 + n_out], refs[o0 + n_out:o0 + n_out + nro]
        s0 = o0 + n_out + nro
        scr, (send, recv) = refs[s0:s0 + n_scr], refs[s0 + n_scr:]
        ids = [pl.program_id(a) for a in range(len(grid))]
        first = functools.reduce(jnp.logical_and, [i == 0 for i in ids])
        last = functools.reduce(jnp.logical_and, [i == g - 1 for i, g in zip(ids, grid)])

        @pl.when(first)
        def _():
            rider.start(rin, rout, send, recv)

        body(*ins, *outs, *scr)

        @pl.when(last)
        def _():
            rider.finish(rin, rout, send, recv)

    for ri, ro in rider.aliases.items():
        aliases[n_in + ri] = n_out + ro
    res = pl.pallas_call(
        wrapped, name=name, grid=grid, in_specs=list(in_specs) + [HBM] * nri, out_specs=ospecs + [HBM] * nro,
        out_shape=shapes + rider.outs,
        scratch_shapes=list(scratch_shapes) + [pltpu.SemaphoreType.DMA((rider.n_sems,)),
                                               pltpu.SemaphoreType.DMA((rider.n_sems,))],
        input_output_aliases=aliases, compiler_params=_params(*sem))(*args, *rider.ins)
    main, extra = res[:n_out], list(res[n_out:])
    return (list(main) if many else main[0]), extra


def _div_tile(n, mult, cap):
    best = n
    for t in range(mult, min(n, cap) + 1, mult):
        if n % t == 0:
            best = t
    return best


def _sigmoid(x):
    return 1.0 / (1.0 + jnp.exp(-x))


def _ln_stats(x):
    mu = jnp.mean(x, axis=-1, keepdims=True)
    xc = x - mu
    var = jnp.mean(xc * xc, axis=-1, keepdims=True)
    rstd = lax.rsqrt(var + LN_EPS)
    return xc * rstd, rstd


def _ln_bwd(dy, xhat, rstd, g):
    dyg = dy * g
    m1 = jnp.mean(dyg, axis=-1, keepdims=True)
    m2 = jnp.mean(dyg * xhat, axis=-1, keepdims=True)
    return rstd * (dyg - m1 - xhat * m2)


def _dot_nt(a, b):
    return lax.dot_general(a, b, (((1,), (1,)), ((), ())), preferred_element_type=F32)


def _dot_tn(a, b):
    return lax.dot_general(a, b, (((0,), (0,)), ((), ())), preferred_element_type=F32)


def _dot(a, b):
    return jnp.dot(a, b, preferred_element_type=F32)


def _dot3(m, x):
    mb = m.astype(BF16)
    x1 = x.astype(BF16)
    r1 = x - x1.astype(F32)
    x2 = r1.astype(BF16)
    x3 = (r1 - x2.astype(F32)).astype(BF16)
    return _dot(mb, x1) + _dot(mb, x2) + _dot(mb, x3)


def _place_shard(x, name, chip_idx, dtype):
    r, c = x.shape
    tr = _div_tile(r, 16, 512)

    def body(s_ref, x_ref, o_ref):
        del s_ref
        o_ref[...] = x_ref[...].astype(dtype)

    return pl.pallas_call(
        body, name=name,
        grid_spec=pltpu.PrefetchScalarGridSpec(
            num_scalar_prefetch=1, grid=(r // tr,),
            in_specs=[pl.BlockSpec((tr, c), lambda i, s: (i, 0))],
            out_specs=pl.BlockSpec((None, tr, c), lambda i, s: (s[0], i, 0))),
        out_shape=jax.ShapeDtypeStruct((N_CHIPS, r, c), dtype),
        compiler_params=_params("parallel"),
    )(chip_idx, x)


def _ln0(x, g, b, tm):
    t, d = x.shape

    def body(x_ref, g_ref, b_ref, o_ref):
        xh, _ = _ln_stats(x_ref[...])
        o_ref[...] = (xh * g_ref[...] + b_ref[...]).astype(BF16)

    row = pl.BlockSpec((1, d), lambda i: (0, 0))
    return pl.pallas_call(
        body, name="ln0", grid=(t // tm,),
        in_specs=[pl.BlockSpec((tm, d), lambda i: (i, 0)), row, row],
        out_specs=pl.BlockSpec((tm, d), lambda i: (i, 0)),
        out_shape=jax.ShapeDtypeStruct((t, d), BF16),
        compiler_params=_params("parallel"),
    )(x, g, b)


def _proj(name, a, w3, n_sec, tm, tn, rider=None):
    m, k = a.shape
    s, _, ws = w3.shape
    sec_w = s * ws // n_sec
    nj = ws // tn
    per_sec = sec_w // tn

    def body(a_ref, w_ref, o_ref):
        o_ref[...] = _dot(a_ref[...], w_ref[...])

    return _call(
        body, (a, w3), name=name, grid=(s * nj, m // tm),
        in_specs=[pl.BlockSpec((tm, k), lambda j, i: (i, 0)),
                  pl.BlockSpec((None, k, tn), lambda j, i: (j // nj, 0, j % nj))],
        out_specs=pl.BlockSpec((None, tm, tn), lambda j, i: (j // per_sec, i, j % per_sec)),
        out_shape=jax.ShapeDtypeStruct((n_sec, m, sec_w), F32), rider=rider)


def _proj_t(name, a, w, tm, tn, rider=None):
    m, k = a.shape
    n = w.shape[0]

    def body(a_ref, w_ref, o_ref):
        o_ref[...] = _dot_nt(a_ref[...], w_ref[...])

    return _call(
        body, (a, w), name=name, grid=(n // tn, m // tm),
        in_specs=[pl.BlockSpec((tm, k), lambda j, i: (i, 0)),
                  pl.BlockSpec((tn, k), lambda j, i: (j, 0))],
        out_specs=pl.BlockSpec((tm, tn), lambda j, i: (i, j)),
        out_shape=jax.ShapeDtypeStruct((m, n), F32), rider=rider)


def _wgrad(name, a, b, out_shape, grid, a_spec, b_spec, o_spec, rider=None):
    nt = len(grid) - 1

    def body(a_ref, b_ref, o_ref):
        t = pl.program_id(nt)
        prod = _dot_tn(a_ref[...], b_ref[...])

        @pl.when(t == 0)
        def _():
            o_ref[...] = prod

        @pl.when(t > 0)
        def _():
            o_ref[...] += prod

    return _call(
        body, (a, b), name=name, grid=grid, in_specs=[a_spec, b_spec], out_specs=o_spec,
        out_shape=jax.ShapeDtypeStruct(out_shape, F32), rider=rider)


def _mix_ln1(cat, w_out, x, g0, b0, g1, b1, alpha, tm, rider=None):
    t, d = x.shape

    def body(cat_ref, w_ref, x_ref, g0_ref, b0_ref, g1_ref, b1_ref, xh_ref, h1b_ref, rstd_ref):
        mix = _dot(cat_ref[...], w_ref[...])
        xh0, _ = _ln_stats(x_ref[...])
        z1 = alpha * (xh0 * g0_ref[...] + b0_ref[...]) + mix
        xh1, rstd1 = _ln_stats(z1)
        xh_ref[...] = xh1
        h1b_ref[...] = (xh1 * g1_ref[...] + b1_ref[...]).astype(BF16)
        rstd_ref[...] = rstd1

    row = pl.BlockSpec((1, d), lambda i: (0, 0))
    blk = pl.BlockSpec((tm, d), lambda i: (i, 0))
    return _call(
        body, (cat, w_out, x, g0, b0, g1, b1), name="mix_ln1", grid=(t // tm,),
        in_specs=[blk, pl.BlockSpec((d, d), lambda i: (0, 0)), blk, row, row, row, row],
        out_specs=[blk, blk, pl.BlockSpec((tm, 1), lambda i: (i, 0))],
        out_shape=[jax.ShapeDtypeStruct((t, d), F32), jax.ShapeDtypeStruct((t, d), BF16),
                   jax.ShapeDtypeStruct((t, 1), F32)], rider=rider)


def _down_ln2_loss(act, w_down3, xhat1, tgt, g1, b1, g2, b2, alpha, tm):
    t, d = xhat1.shape
    s, ks, _ = w_down3.shape
    ni = t // tm
    inv_d = 1.0 / d

    def body(act_ref, w_ref, xh1_ref, tgt_ref, g1_ref, b1_ref, g2_ref, b2_ref,
             dz2_ref, dz2b_ref, dg2_ref, db2_ref, loss_ref, acc, lrow):
        i, k = pl.program_id(0), pl.program_id(1)
        prod = _dot(act_ref[...], w_ref[...])

        @pl.when(k == 0)
        def _():
            acc[...] = prod

        @pl.when(k > 0)
        def _():
            acc[...] += prod

        @pl.when(k == s - 1)
        def _():
            h1 = xh1_ref[...] * g1_ref[...] + b1_ref[...]
            xh2, rstd2 = _ln_stats(alpha * h1 + acc[...])
            g2v = g2_ref[...]
            diff = xh2 * g2v + b2_ref[...] - tgt_ref[...]
            dh2 = diff * inv_d
            sq = jnp.sum(diff * diff, axis=0, keepdims=True)
            dg = jnp.sum(dh2 * xh2, axis=0, keepdims=True)
            db = jnp.sum(dh2, axis=0, keepdims=True)

            @pl.when(i == 0)
            def _():
                lrow[...] = sq
                dg2_ref[...] = dg
                db2_ref[...] = db

            @pl.when(i > 0)
            def _():
                lrow[...] += sq
                dg2_ref[...] += dg
                db2_ref[...] += db

            dz2 = _ln_bwd(dh2, xh2, rstd2, g2v)
            dz2_ref[...] = dz2
            dz2b_ref[...] = dz2.astype(BF16)

            @pl.when(i == ni - 1)
            def _():
                tot = jnp.sum(lrow[...], axis=-1, keepdims=True) * (0.5 * inv_d)
                loss_ref[...] = jnp.broadcast_to(tot, (1, LANE))

    row = pl.BlockSpec((1, d), lambda i, k: (0, 0))
    blk = pl.BlockSpec((tm, d), lambda i, k: (i, 0))
    return pl.pallas_call(
        body, name="down_ln2_loss", grid=(ni, s),
        in_specs=[pl.BlockSpec((tm, ks), lambda i, k: (i, k)),
                  pl.BlockSpec((None, ks, d), lambda i, k: (k, 0, 0)),
                  blk, blk, row, row, row, row],
        out_specs=[blk, blk, row, row, pl.BlockSpec((1, LANE), lambda i, k: (0, 0))],
        out_shape=[jax.ShapeDtypeStruct((t, d), F32), jax.ShapeDtypeStruct((t, d), BF16),
                   jax.ShapeDtypeStruct((1, d), F32), jax.ShapeDtypeStruct((1, d), F32),
                   jax.ShapeDtypeStruct((1, LANE), F32)],
        scratch_shapes=[pltpu.VMEM((tm, d), F32), pltpu.VMEM((1, d), F32)],
        compiler_params=_params("arbitrary", "arbitrary"),
    )(act, w_down3, xhat1, tgt, g1, b1, g2, b2)


def _up_t_ln1_bwd(dhh3, w_up3, dz2, xhat1, rstd1, g1, alpha, tm, rider=None):
    t, d = dz2.shape
    s, _, ws = w_up3.shape
    tk = ws // 2
    nk = 2 * s
    per_sec = dhh3.shape[2] // tk

    def body(a_ref, w_ref, dz2_ref, xh_ref, rstd_ref, g_ref, dz1_ref, dz1b_ref, dg_ref, db_ref, acc):
        i, k = pl.program_id(0), pl.program_id(1)
        prod = _dot_nt(a_ref[...], w_ref[...])

        @pl.when(k == 0)
        def _():
            acc[...] = prod

        @pl.when(k > 0)
        def _():
            acc[...] += prod

        @pl.when(k == nk - 1)
        def _():
            dh1 = alpha * dz2_ref[...] + acc[...]
            xh = xh_ref[...]
            dg = jnp.sum(dh1 * xh, axis=0, keepdims=True)
            db = jnp.sum(dh1, axis=0, keepdims=True)

            @pl.when(i == 0)
            def _():
                dg_ref[...] = dg
                db_ref[...] = db

            @pl.when(i > 0)
            def _():
                dg_ref[...] += dg
                db_ref[...] += db

            dz1 = _ln_bwd(dh1, xh, rstd_ref[...], g_ref[...])
            dz1_ref[...] = dz1
            dz1b_ref[...] = dz1.astype(BF16)

    row = pl.BlockSpec((1, d), lambda i, k: (0, 0))
    blk = pl.BlockSpec((tm, d), lambda i, k: (i, 0))
    return _call(
        body, (dhh3, w_up3, dz2, xhat1, rstd1, g1), name="up_t_ln1_bwd", grid=(t // tm, nk),
        in_specs=[pl.BlockSpec((None, tm, tk), lambda i, k: (k // per_sec, i, k % per_sec)),
                  pl.BlockSpec((None, d, tk), lambda i, k: (k // 2, 0, k % 2)),
                  blk, blk, pl.BlockSpec((tm, 1), lambda i, k: (i, 0)), row],
        out_specs=[blk, blk, row, row],
        out_shape=[jax.ShapeDtypeStruct((t, d), F32), jax.ShapeDtypeStruct((t, d), BF16),
                   jax.ShapeDtypeStruct((1, d), F32), jax.ShapeDtypeStruct((1, d), F32)],
        scratch_shapes=[pltpu.VMEM((tm, d), F32)], rider=rider)


def _in_t_ln0_bwd(dp3, w_in3, dz1, x, g0, alpha, tm, rider=None):
    t, d = x.shape
    s, _, ws = w_in3.shape
    n_sec, _, sec_w = dp3.shape
    tk = sec_w // 2
    nk = n_sec * 2
    per_shard = ws // tk

    def body(a_ref, w_ref, dz1_ref, x_ref, g_ref, dx_ref, dg_ref, db_ref, acc):
        i, k = pl.program_id(0), pl.program_id(1)
        prod = _dot_nt(a_ref[...], w_ref[...])

        @pl.when(k == 0)
        def _():
            acc[...] = prod

        @pl.when(k > 0)
        def _():
            acc[...] += prod

        @pl.when(k == nk - 1)
        def _():
            dh0 = alpha * dz1_ref[...] + acc[...]
            xh, rstd = _ln_stats(x_ref[...])
            dg = jnp.sum(dh0 * xh, axis=0, keepdims=True)
            db = jnp.sum(dh0, axis=0, keepdims=True)

            @pl.when(i == 0)
            def _():
                dg_ref[...] = dg
                db_ref[...] = db

            @pl.when(i > 0)
            def _():
                dg_ref[...] += dg
                db_ref[...] += db

            dx_ref[...] = _ln_bwd(dh0, xh, rstd, g_ref[...])

    row = pl.BlockSpec((1, d), lambda i, k: (0, 0))
    blk = pl.BlockSpec((tm, d), lambda i, k: (i, 0))
    return _call(
        body, (dp3, w_in3, dz1, x, g0), name="in_t_ln0_bwd", grid=(t // tm, nk),
        in_specs=[pl.BlockSpec((None, tm, tk), lambda i, k: (k // 2, i, k % 2)),
                  pl.BlockSpec((None, d, tk), lambda i, k: (k // per_shard, 0, k % per_shard)),
                  blk, blk, row],
        out_specs=[blk, row, row],
        out_shape=[jax.ShapeDtypeStruct((t, d), F32), jax.ShapeDtypeStruct((1, d), F32),
                   jax.ShapeDtypeStruct((1, d), F32)],
        scratch_shapes=[pltpu.VMEM((tm, d), F32)], rider=rider)


def _conv_fwd(p3, conv_w, conv_b, cn_g, cn_b, tc, cb, rider=None):
    _, t, w = p3.shape
    kk = conv_w.shape[0]
    off = HALO - (kk - 1)
    hb = tc // HALO

    def body(a_ref, g_ref, ap_ref, gp_ref, w_ref, b_ref, ng_ref, nb_ref, cat_ref, u1_ref, ext):
        i = pl.program_id(1)
        ext[pl.ds(HALO, tc), :] = a_ref[...] * _sigmoid(g_ref[...])
        prev = ap_ref[...] * _sigmoid(gp_ref[...])
        ext[pl.ds(0, HALO), :] = jnp.where(i > 0, prev, 0.0)
        for r in range(tc // ROWS):
            acc = jnp.broadcast_to(b_ref[...], (ROWS, cb))
            for k in range(kk):
                acc = acc + w_ref[k:k + 1, :] * ext[pl.ds(r * ROWS + off + k, ROWS), :]
            u1_ref[pl.ds(r * ROWS, ROWS), :] = acc
            for g in range(cb // LANE):
                sl = slice(g * LANE, (g + 1) * LANE)
                xh, _ = _ln_stats(acc[:, sl])
                u2 = xh * ng_ref[:, sl] + nb_ref[:, sl]
                cat_ref[pl.ds(r * ROWS, ROWS), sl] = (u2 * _sigmoid(u2)).astype(BF16)

    cur = lambda sec: pl.BlockSpec((None, tc, cb), lambda j, i: (sec, i, j))
    prev = lambda sec: pl.BlockSpec((None, HALO, cb), lambda j, i: (sec, jnp.maximum(i * hb - 1, 0), j))
    row = pl.BlockSpec((1, cb), lambda j, i: (0, j))
    return _call(
        body, (p3, p3, p3, p3, conv_w, conv_b, cn_g, cn_b), name="conv_fwd", grid=(w // cb, t // tc),
        in_specs=[cur(0), cur(1), prev(0), prev(1), pl.BlockSpec((kk, cb), lambda j, i: (0, j)), row, row, row],
        out_specs=[pl.BlockSpec((tc, cb), lambda j, i: (i, j)), pl.BlockSpec((tc, cb), lambda j, i: (i, j))],
        out_shape=[jax.ShapeDtypeStruct((t, 2 * w), BF16), jax.ShapeDtypeStruct((t, w), F32)],
        scratch_shapes=[pltpu.VMEM((tc + HALO, cb), F32)], rider=rider)


def _conv_norm_bwd(dcat, u1, cn_g, cn_b, tc):
    t, w = u1.shape

    def body(du_ref, u1_ref, ng_ref, nb_ref, du1_ref, dg_ref, db_ref):
        i = pl.program_id(0)
        for g in range(w // LANE):
            sl = slice(g * LANE, (g + 1) * LANE)
            ng = ng_ref[:, sl]
            xh, rstd = _ln_stats(u1_ref[:, sl])
            u2 = xh * ng + nb_ref[:, sl]
            sg = _sigmoid(u2)
            du2 = du_ref[:, sl] * (sg * (1.0 + u2 * (1.0 - sg)))
            dg = jnp.sum(du2 * xh, axis=0, keepdims=True)
            db = jnp.sum(du2, axis=0, keepdims=True)

            @pl.when(i == 0)
            def _():
                dg_ref[:, sl] = dg
                db_ref[:, sl] = db

            @pl.when(i > 0)
            def _():
                dg_ref[:, sl] += dg
                db_ref[:, sl] += db

            du1_ref[:, sl] = _ln_bwd(du2, xh, rstd, ng)

    row = pl.BlockSpec((1, w), lambda i: (0, 0))
    blk = pl.BlockSpec((tc, w), lambda i: (i, 0))
    return pl.pallas_call(
        body, name="conv_norm_bwd", grid=(t // tc,),
        in_specs=[blk, blk, row, row], out_specs=[blk, row, row],
        out_shape=[jax.ShapeDtypeStruct((t, w), F32), jax.ShapeDtypeStruct((1, w), F32),
                   jax.ShapeDtypeStruct((1, w), F32)],
        compiler_params=_params("arbitrary"),
    )(dcat, u1, cn_g, cn_b)


def _conv_bwd(du1, p3, conv_w, tc, cb, rider=None):
    n_sec, t, w = p3.shape
    kk = conv_w.shape[0]
    off = HALO - (kk - 1)
    hb = tc // HALO
    nt = t // tc
    kpad = -(-kk // SUBLANE) * SUBLANE

    def body(d_ref, dn_ref, a_ref, g_ref, ap_ref, gp_ref, w_ref, dp_ref, dw_ref, db_ref, extd, extu, wacc, bacc):
        i = pl.program_id(1)

        @pl.when(i == 0)
        def _():
            wacc[...] = jnp.zeros_like(wacc)
            bacc[...] = jnp.zeros_like(bacc)

        extd[pl.ds(0, tc), :] = d_ref[...]
        extd[pl.ds(tc, HALO), :] = jnp.where(i < nt - 1, dn_ref[...], 0.0)
        extu[pl.ds(HALO, tc), :] = a_ref[...] * _sigmoid(g_ref[...])
        extu[pl.ds(0, HALO), :] = jnp.where(i > 0, ap_ref[...] * _sigmoid(gp_ref[...]), 0.0)
        for r in range(tc // ROWS):
            rows = pl.ds(r * ROWS, ROWS)
            acc = jnp.zeros((ROWS, cb), F32)
            for k in range(kk):
                acc = acc + w_ref[k:k + 1, :] * extd[pl.ds(r * ROWS + (kk - 1) - k, ROWS), :]
            a = a_ref[rows, :]
            sg = _sigmoid(g_ref[rows, :])
            dp_ref[0, rows, :] = (acc * sg).astype(BF16)
            dp_ref[1, rows, :] = (acc * a * sg * (1.0 - sg)).astype(BF16)
            d = d_ref[rows, :]
            bacc[...] += jnp.sum(d.reshape(ROWS // SUBLANE, SUBLANE, cb), axis=0)
            for k in range(kk):
                prod = d * extu[pl.ds(r * ROWS + off + k, ROWS), :]
                wacc[k] += jnp.sum(prod.reshape(ROWS // SUBLANE, SUBLANE, cb), axis=0)

        @pl.when(i == nt - 1)
        def _():
            for k in range(kk):
                dw_ref[k:k + 1, :] = jnp.sum(wacc[k], axis=0, keepdims=True)
            if kpad > kk:
                dw_ref[kk:kpad, :] = jnp.zeros((kpad - kk, cb), F32)
            db_ref[...] = jnp.sum(bacc[...], axis=0, keepdims=True)

    cur = lambda sec: pl.BlockSpec((None, tc, cb), lambda j, i: (sec, i, j))
    prev = lambda sec: pl.BlockSpec((None, HALO, cb), lambda j, i: (sec, jnp.maximum(i * hb - 1, 0), j))
    return _call(
        body, (du1, du1, p3, p3, p3, p3, conv_w), name="conv_bwd", grid=(w // cb, nt),
        in_specs=[pl.BlockSpec((tc, cb), lambda j, i: (i, j)),
                  pl.BlockSpec((HALO, cb), lambda j, i: (jnp.minimum((i + 1) * hb, t // HALO - 1), j)),
                  cur(0), cur(1), prev(0), prev(1), pl.BlockSpec((kk, cb), lambda j, i: (0, j))],
        out_specs=[pl.BlockSpec((2, tc, cb), lambda j, i: (0, i, j)),
                   pl.BlockSpec((kpad, cb), lambda j, i: (0, j)),
                   pl.BlockSpec((1, cb), lambda j, i: (0, j))],
        out_shape=[jax.ShapeDtypeStruct((n_sec, t, w), BF16), jax.ShapeDtypeStruct((kpad, w), F32),
                   jax.ShapeDtypeStruct((1, w), F32)],
        scratch_shapes=[pltpu.VMEM((tc + HALO, cb), F32), pltpu.VMEM((tc + HALO, cb), F32),
                        pltpu.VMEM((kk, SUBLANE, cb), F32), pltpu.VMEM((SUBLANE, cb), F32)], rider=rider)


def _ffn_act_fwd(hh3, fw, fb, tc, cb):
    _, t, dff = hh3.shape
    kk = fw.shape[0]
    off = FHALO - (kk - 1)
    hb = tc // FHALO

    def body(g_ref, v_ref, gp_ref, w_ref, b_ref, act_ref, ext):
        i = pl.program_id(1)
        ext[pl.ds(FHALO, tc), :] = g_ref[...]
        ext[pl.ds(0, FHALO), :] = jnp.where(i > 0, gp_ref[...], 0.0)
        gc = jnp.broadcast_to(b_ref[...], (tc, cb))
        for k in range(kk):
            gc = gc + w_ref[k:k + 1, :] * ext[pl.ds(off + k, tc), :]
        act_ref[...] = (gc * _sigmoid(gc) * v_ref[...]).astype(BF16)

    return pl.pallas_call(
        body, name="ffn_act_fwd", grid=(dff // cb, t // tc),
        in_specs=[pl.BlockSpec((None, tc, cb), lambda j, i: (0, i, j)),
                  pl.BlockSpec((None, tc, cb), lambda j, i: (1, i, j)),
                  pl.BlockSpec((None, FHALO, cb), lambda j, i: (0, jnp.maximum(i * hb - 1, 0), j)),
                  pl.BlockSpec((kk, cb), lambda j, i: (0, j)),
                  pl.BlockSpec((1, cb), lambda j, i: (0, j))],
        out_specs=pl.BlockSpec((tc, cb), lambda j, i: (i, j)),
        out_shape=jax.ShapeDtypeStruct((t, dff), BF16),
        scratch_shapes=[pltpu.VMEM((tc + FHALO, cb), F32)],
        compiler_params=_params("parallel", "arbitrary"),
    )(hh3, hh3, hh3, fw, fb)


def _ffn_act_bwd(dact, hh3, fw, fb, tc, cb):
    _, t, dff = hh3.shape
    kk = fw.shape[0]
    off = FHALO - (kk - 1)
    hb = tc // FHALO
    nt = t // tc
    te = tc + FHALO

    def body(da_ref, dan_ref, g_ref, gp_ref, gn_ref, v_ref, vn_ref, w_ref, b_ref,
             dhh_ref, dw_ref, db_ref, gext, dext, wacc, bacc):
        i = pl.program_id(1)

        @pl.when(i == 0)
        def _():
            wacc[...] = jnp.zeros_like(wacc)
            bacc[...] = jnp.zeros_like(bacc)

        gext[pl.ds(0, FHALO), :] = jnp.where(i > 0, gp_ref[...], 0.0)
        gext[pl.ds(FHALO, tc), :] = g_ref[...]
        gext[pl.ds(FHALO + tc, FHALO), :] = gn_ref[...]
        gc = jnp.broadcast_to(b_ref[...], (te, cb))
        for k in range(kk):
            gc = gc + w_ref[k:k + 1, :] * gext[pl.ds(off + k, te), :]
        sg = _sigmoid(gc)
        dsilu = sg * (1.0 + gc * (1.0 - sg))
        live = i < nt - 1
        da_cur = da_ref[...]
        dext[pl.ds(0, tc), :] = da_cur * v_ref[...] * dsilu[0:tc]
        dext[pl.ds(tc, FHALO), :] = jnp.where(live, dan_ref[...] * vn_ref[...] * dsilu[tc:te], 0.0)
        dhh_ref[1] = (da_cur * (gc[0:tc] * sg[0:tc])).astype(BF16)
        dg = jnp.zeros((tc, cb), F32)
        for k in range(kk):
            dg = dg + w_ref[k:k + 1, :] * dext[pl.ds((kk - 1) - k, tc), :]
        dhh_ref[0] = dg.astype(BF16)
        dgc = dext[pl.ds(0, tc), :]
        bacc[...] += jnp.sum(dgc.reshape(tc // SUBLANE, SUBLANE, cb), axis=0)
        for k in range(kk):
            prod = dgc * gext[pl.ds(off + k, tc), :]
            wacc[k] += jnp.sum(prod.reshape(tc // SUBLANE, SUBLANE, cb), axis=0)

        @pl.when(i == nt - 1)
        def _():
            for k in range(kk):
                dw_ref[k:k + 1, :] = jnp.sum(wacc[k], axis=0, keepdims=True)
            dw_ref[kk:SUBLANE, :] = jnp.zeros((SUBLANE - kk, cb), F32)
            db_ref[...] = jnp.sum(bacc[...], axis=0, keepdims=True)

    nxt = lambda i: jnp.minimum((i + 1) * hb, t // FHALO - 1)
    return pl.pallas_call(
        body, name="ffn_act_bwd", grid=(dff // cb, nt),
        in_specs=[pl.BlockSpec((tc, cb), lambda j, i: (i, j)),
                  pl.BlockSpec((FHALO, cb), lambda j, i: (nxt(i), j)),
                  pl.BlockSpec((None, tc, cb), lambda j, i: (0, i, j)),
                  pl.BlockSpec((None, FHALO, cb), lambda j, i: (0, jnp.maximum(i * hb - 1, 0), j)),
                  pl.BlockSpec((None, FHALO, cb), lambda j, i: (0, nxt(i), j)),
                  pl.BlockSpec((None, tc, cb), lambda j, i: (1, i, j)),
                  pl.BlockSpec((None, FHALO, cb), lambda j, i: (1, nxt(i), j)),
                  pl.BlockSpec((kk, cb), lambda j, i: (0, j)),
                  pl.BlockSpec((1, cb), lambda j, i: (0, j))],
        out_specs=[pl.BlockSpec((2, tc, cb), lambda j, i: (0, i, j)),
                   pl.BlockSpec((SUBLANE, cb), lambda j, i: (0, j)),
                   pl.BlockSpec((1, cb), lambda j, i: (0, j))],
        out_shape=[jax.ShapeDtypeStruct((2, t, dff), BF16), jax.ShapeDtypeStruct((SUBLANE, dff), F32),
                   jax.ShapeDtypeStruct((1, dff), F32)],
        scratch_shapes=[pltpu.VMEM((tc + 2 * FHALO, cb), F32), pltpu.VMEM((te, cb), F32),
                        pltpu.VMEM((kk, SUBLANE, cb), F32), pltpu.VMEM((SUBLANE, cb), F32)],
        compiler_params=_params("parallel", "arbitrary"),
    )(dact, dact, hh3, hh3, hh3, hh3, hh3, fw, fb)


def _chunk_consts():
    r = lax.broadcasted_iota(jnp.int32, (CHUNK, CHUNK), 0)
    c = lax.broadcasted_iota(jnp.int32, (CHUNK, CHUNK), 1)
    blk = (r // SUB) * SUB
    tri = (c <= r).astype(F32)
    start = (c < blk).astype(F32)
    end = (c < blk + SUB).astype(F32)
    return jnp.concatenate([tri, start, end, jnp.ones((SUBLANE, CHUNK), F32)], axis=0)


def _gate_terms(q, fpre, lb):
    sf = _sigmoid(fpre)
    fg = lb + (1.0 - lb) * sf
    sq = _sigmoid(q)
    return sf, fg, 1.0 - fg, sq, q * sq


def _decays(g, consts):
    cs = _dot3(consts, g)
    b = cs[0:CHUNK]
    rs = cs[CHUNK:2 * CHUNK]
    re = cs[2 * CHUNK:3 * CHUNK]
    tot = cs[3 * CHUNK:3 * CHUNK + 1]
    return b, rs, re, tot


def _lower_bound(lb_ref):
    l0, l1 = lb_ref[0:1, :], lb_ref[1:2, :]
    mx = jnp.maximum(l0, l1)
    e0, e1 = jnp.exp(l0 - mx), jnp.exp(l1 - mx)
    return e0 / (e0 + e1)


def _scaled_keys(kt, rs, re, rowblk, i):
    scale = jnp.where(rowblk < i, jnp.exp(jnp.minimum(rs[SUB * i:SUB * i + 1, :] - re, 0.0)), 0.0)
    return kt * scale, scale


def _off_diag_scores(qt, kt, rs, re, rowblk):
    a = jnp.zeros((CHUNK, CHUNK), F32)
    for i in range(1, CHUNK // SUB):
        ki, _ = _scaled_keys(kt, rs, re, rowblk, i)
        a = a + _dot_nt(jnp.where(rowblk == i, qt, 0.0).astype(BF16), ki.astype(BF16))
    return a


def _hgrn_fwd(p3, lb_logits, hg, cat, tb, rider=None):
    _, t, w = p3.shape
    nh = w // LANE
    nc = tb // CHUNK

    def body(q_ref, f_ref, v_ref, og_ref, lb_ref, hg_ref, cat_in, cat_ref, o_ref, st_ref, state):
        del cat_in
        consts = _chunk_consts()
        lb = _lower_bound(lb_ref)
        hgv = hg_ref[...]
        rowblk = lax.broadcasted_iota(jnp.int32, (CHUNK, 1), 0) // SUB
        rowpos = lax.broadcasted_iota(jnp.int32, (CHUNK, 1), 0) % SUB

        @pl.when(pl.program_id(1) == 0)
        def _():
            state[...] = jnp.zeros_like(state)

        def chunk(c, carry):
            rows = pl.ds(pl.multiple_of(c * CHUNK, CHUNK), CHUNK)
            v = v_ref[rows, :]
            og = og_ref[rows, :]
            _, fg, kk, _, qh = _gate_terms(q_ref[rows, :], f_ref[rows, :], lb)
            b, rs, re, tot = _decays(jnp.log(fg), consts)
            qt = qh * jnp.exp(b - rs)
            kt = kk * jnp.exp(re - b)
            vb = v.astype(BF16)
            st = state[...]
            st_ref[c] = st
            o = _dot(_off_diag_scores(qt, kt, rs, re, rowblk).astype(BF16), vb)
            o = o + _dot_nt((qh * jnp.exp(b)).astype(BF16), st.astype(BF16))
            for d in range(SUB):
                ks, vs = (kk, v) if d == 0 else (pltpu.roll(kk, d, 0), pltpu.roll(v, d, 0))
                e = 1.0 if d == 0 else jnp.exp(jnp.where(rowpos >= d, b - pltpu.roll(b, d, 0), NEG_BIG))
                o = o + jnp.sum(qh * ks * e, axis=-1, keepdims=True) * vs
            k_up = kk * jnp.exp(tot - b)
            state[...] = st * jnp.exp(tot) + _dot_tn(vb, k_up.astype(BF16))
            o_ref[rows, :] = o
            r = lax.rsqrt(jnp.mean(o * o, axis=-1, keepdims=True) + RMS_EPS)
            cat_ref[rows, :] = (o * r * hgv * (og * _sigmoid(og))).astype(BF16)
            return carry

        lax.fori_loop(0, nc, chunk, 0)

    sec = lambda s: pl.BlockSpec((None, tb, LANE), lambda h, i: (s, i, h))
    return _call(
        body, (p3, p3, p3, p3, lb_logits, hg, cat), name="hgrn_fwd", grid=(nh, t // tb),
        in_specs=[sec(2), sec(3), sec(4), sec(5),
                  pl.BlockSpec((2, LANE), lambda h, i: (0, h)),
                  pl.BlockSpec((1, LANE), lambda h, i: (0, h)), HBM],
        out_specs=[pl.BlockSpec((tb, LANE), lambda h, i: (i, nh + h)),
                   pl.BlockSpec((tb, LANE), lambda h, i: (i, h)),
                   pl.BlockSpec((None, nc, LANE, LANE), lambda h, i: (h, i, 0, 0))],
        out_shape=[jax.ShapeDtypeStruct(cat.shape, BF16), jax.ShapeDtypeStruct((t, w), F32),
                   jax.ShapeDtypeStruct((nh, t // CHUNK, LANE, LANE), F32)],
        scratch_shapes=[pltpu.VMEM((LANE, LANE), F32)], aliases={6: 0}, rider=rider)


def _hgrn_bwd(p3, lb_logits, hg, o_pre, states, dcat, dp3, tb, rider=None):
    n_sec, t, w = p3.shape
    nh = w // LANE
    nc = tb // CHUNK
    nb = t // tb

    def body(q_ref, f_ref, v_ref, og_ref, lb_ref, hg_ref, o_ref, st_ref, dc_ref, dp_in,
             dp_ref, dlb_ref, dhg_ref, dstate, stash, lbacc, hgacc):
        del dp_in
        i, half = pl.program_id(1), pl.program_id(2)

        @pl.when(half == 1)
        def _():
            dp_ref[...] = stash[...]

        @pl.when(half == 0)
        def _():
            consts = _chunk_consts()
            rr = lax.broadcasted_iota(jnp.int32, (CHUNK, CHUNK), 0)
            cc = lax.broadcasted_iota(jnp.int32, (CHUNK, CHUNK), 1)
            upper = (cc >= rr).astype(F32)
            lb = _lower_bound(lb_ref)
            hgv = hg_ref[...]
            rowblk = lax.broadcasted_iota(jnp.int32, (CHUNK, 1), 0) // SUB
            rowpos = lax.broadcasted_iota(jnp.int32, (CHUNK, 1), 0) % SUB

            @pl.when(i == 0)
            def _():
                dstate[...] = jnp.zeros_like(dstate)
                lbacc[...] = jnp.zeros_like(lbacc)
                hgacc[...] = jnp.zeros_like(hgacc)

            def chunk(cr, carry):
                c = nc - 1 - cr
                rows = pl.ds(pl.multiple_of(c * CHUNK, CHUNK), CHUNK)
                q = q_ref[rows, :]
                v = v_ref[rows, :]
                og = og_ref[rows, :]
                o = o_ref[rows, :]
                dcg = dc_ref[rows, :]
                sf, fg, kk, sq, qh = _gate_terms(q, f_ref[rows, :], lb)
                b, rs, re, tot = _decays(jnp.log(fg), consts)
                eq = jnp.exp(b - rs)
                ek = jnp.exp(re - b)
                qt = qh * eq
                kt = kk * ek
                e_in = jnp.exp(b)
                e_up = jnp.exp(tot - b)
                e_tot = jnp.exp(tot)
                q_in = (qh * e_in).astype(BF16)
                k_up = (kk * e_up).astype(BF16)
                vb = v.astype(BF16)
                st = st_ref[c]
                dst = dstate[...]
                dstb = dst.astype(BF16)

                sg = _sigmoid(og)
                r = lax.rsqrt(jnp.mean(o * o, axis=-1, keepdims=True) + RMS_EPS)
                ohat = o * r
                d_og = dcg * ohat * hgv * (sg * (1.0 + og * (1.0 - sg)))
                d_on = dcg * (og * sg)
                hgacc[...] += jnp.sum((d_on * ohat).reshape(CHUNK // SUBLANE, SUBLANE, LANE), axis=0)
                d_oh = d_on * hgv
                do = r * (d_oh - ohat * jnp.mean(d_oh * ohat, axis=-1, keepdims=True))
                dob = do.astype(BF16)

                da = _dot_nt(dob, vb)
                a_off = jnp.zeros((CHUNK, CHUNK), F32)
                dqt = jnp.zeros((CHUNK, LANE), F32)
                dkt = jnp.zeros((CHUNK, LANE), F32)
                for blk in range(1, CHUNK // SUB):
                    ki, scale = _scaled_keys(kt, rs, re, rowblk, blk)
                    kib = ki.astype(BF16)
                    qib = jnp.where(rowblk == blk, qt, 0.0).astype(BF16)
                    dab = jnp.where(rowblk == blk, da, 0.0).astype(BF16)
                    a_off = a_off + _dot_nt(qib, kib)
                    dqt = dqt + _dot(dab, kib)
                    dkt = dkt + _dot_tn(dab, qib) * scale
                dqh = dqt * eq
                dk = dkt * ek
                dv = _dot_tn(a_off.astype(BF16), dob)

                dqh = dqh + _dot(dob, st.astype(BF16)) * e_in
                dk = dk + _dot(vb, dstb) * e_up
                dv = dv + _dot_nt(k_up, dstb)
                st_end = st * e_tot + _dot_tn(vb, k_up)
                carry_g = jnp.sum(st_end * dst, axis=0, keepdims=True)
                dstate[...] = dst * e_tot + _dot_tn(dob, q_in)

                for d in range(SUB):
                    if d == 0:
                        ks, vs, e = kk, v, 1.0
                    else:
                        ks, vs = pltpu.roll(kk, d, 0), pltpu.roll(v, d, 0)
                        e = jnp.exp(jnp.where(rowpos >= d, b - pltpu.roll(b, d, 0), NEG_BIG))
                    a_d = jnp.sum(qh * ks * e, axis=-1, keepdims=True)
                    da_d = jnp.sum(do * vs, axis=-1, keepdims=True) * e
                    dqh = dqh + da_d * ks
                    ck = da_d * qh
                    cv = a_d * do
                    if d == 0:
                        dk = dk + ck
                        dv = dv + cv
                    else:
                        dk = dk + pltpu.roll(ck, CHUNK - d, 0)
                        dv = dv + pltpu.roll(cv, CHUNK - d, 0)

                dg = _dot3(upper, qh * dqh - kk * dk) + carry_g
                dfg = dg / fg - dk
                lbacc[...] += jnp.sum((dfg * (1.0 - sf)).reshape(CHUNK // SUBLANE, SUBLANE, LANE), axis=0)
                dp_ref[0, rows, :] = (dqh * (sq * (1.0 + q * (1.0 - sq)))).astype(BF16)
                dp_ref[1, rows, :] = (dfg * (1.0 - lb) * sf * (1.0 - sf)).astype(BF16)
                stash[0, rows, :] = dv.astype(BF16)
                stash[1, rows, :] = d_og.astype(BF16)
                return carry

            lax.fori_loop(0, nc, chunk, 0)

            @pl.when(i == nb - 1)
            def _():
                dlb_ref[...] = jnp.sum(lbacc[...], axis=0, keepdims=True)
                dhg_ref[...] = jnp.sum(hgacc[...], axis=0, keepdims=True)

    rev = lambda i: nb - 1 - i
    sec = lambda s: pl.BlockSpec((None, tb, LANE), lambda h, i, z: (s, rev(i), h))
    return _call(
        body, (p3, p3, p3, p3, lb_logits, hg, o_pre, states, dcat, dp3), name="hgrn_bwd", grid=(nh, nb, 2),
        in_specs=[sec(2), sec(3), sec(4), sec(5),
                  pl.BlockSpec((2, LANE), lambda h, i, z: (0, h)),
                  pl.BlockSpec((1, LANE), lambda h, i, z: (0, h)),
                  pl.BlockSpec((tb, LANE), lambda h, i, z: (rev(i), h)),
                  pl.BlockSpec((None, nc, LANE, LANE), lambda h, i, z: (h, rev(i), 0, 0)),
                  pl.BlockSpec((tb, LANE), lambda h, i, z: (rev(i), nh + h)), HBM],
        out_specs=[pl.BlockSpec((2, tb, LANE), lambda h, i, z: (1 + z, rev(i), h)),
                   pl.BlockSpec((1, LANE), lambda h, i, z: (0, h)),
                   pl.BlockSpec((1, LANE), lambda h, i, z: (0, h))],
        out_shape=[jax.ShapeDtypeStruct((n_sec, t, w), BF16), jax.ShapeDtypeStruct((1, w), F32),
                   jax.ShapeDtypeStruct((1, w), F32)],
        scratch_shapes=[pltpu.VMEM((LANE, LANE), F32), pltpu.VMEM((2, tb, LANE), BF16),
                        pltpu.VMEM((SUBLANE, LANE), F32), pltpu.VMEM((SUBLANE, LANE), F32)],
        aliases={9: 0}, rider=rider)


def _place():
    x, y, c = lax.axis_index("x"), lax.axis_index("y"), lax.axis_index("c")
    chips = [(1 - x, y), (x, 1 - y), (1 - x, 1 - y)]
    return x, y, c, chips


def _rows(buf, px, py, pc):
    half = buf.shape[1] // 2
    return buf.at[2 * px + py, pl.ds(pc * half, half)]


def _rcopy(src, dst, send, recv, idx, to):
    return pltpu.make_async_remote_copy(src_ref=src, dst_ref=dst, send_sem=send.at[idx], recv_sem=recv.at[idx],
                                        device_id=to, device_id_type=MESH)


def _same(bufs):
    return [jax.ShapeDtypeStruct(b.shape, b.dtype) for b in bufs]


def _ride_gather_ici(bufs):
    n = len(bufs)

    def start(rin, rout, send, recv):
        x, y, c, chips = _place()
        for k in range(n):
            mine = _rows(rout[k], x, y, c)
            for j, chip in enumerate(chips):
                _rcopy(mine, mine, send, recv, 3 * k + j, (*chip, c)).start()

    def finish(rin, rout, send, recv):
        x, y, c, chips = _place()
        for k in range(n):
            for j, chip in enumerate(chips):
                theirs = _rows(rout[k], *chip, c)
                _rcopy(theirs, theirs, send, recv, 3 * k + j, (x, y, c)).wait_recv()
        for k in range(n):
            mine = _rows(rout[k], x, y, c)
            for j in range(3):
                _rcopy(mine, mine, send, recv, 3 * k + j, (x, y, c)).wait_send()

    return _Rider(bufs, _same(bufs), {k: k for k in range(n)}, 3 * n, start, finish)


def _ride_gather_d2d(bufs):
    n = len(bufs)

    def start(rin, rout, send, recv):
        x, y, c, chips = _place()
        for k in range(n):
            for j, chip in enumerate(chips):
                got = _rows(rout[k], *chip, c)
                _rcopy(got, got, send, recv, 3 * k + j, (x, y, 1 - c)).start()

    def finish(rin, rout, send, recv):
        x, y, c, chips = _place()
        for k in range(n):
            for j, chip in enumerate(chips):
                theirs = _rows(rout[k], *chip, 1 - c)
                _rcopy(theirs, theirs, send, recv, 3 * k + j, (x, y, c)).wait_recv()
        for k in range(n):
            for j, chip in enumerate(chips):
                got = _rows(rout[k], *chip, c)
                _rcopy(got, got, send, recv, 3 * k + j, (x, y, c)).wait_send()

    return _Rider(bufs, _same(bufs), {k: k for k in range(n)}, 3 * n, start, finish)


def _ride_swap(grads):
    n = len(grads)

    def copy(k, rin, rout, send, recv):
        x, y, c, _ = _place()
        half = rin[k].shape[1] // 2
        return _rcopy(rin[k].at[:, pl.ds((1 - c) * half, half)], rout[k], send, recv, k, (x, y, 1 - c))

    def start(rin, rout, send, recv):
        for k in range(n):
            copy(k, rin, rout, send, recv).start()

    def finish(rin, rout, send, recv):
        for k in range(n):
            copy(k, rin, rout, send, recv).wait()

    outs = [jax.ShapeDtypeStruct((g.shape[0], g.shape[1] // 2, g.shape[2]), g.dtype) for g in grads]
    return _Rider(grads, outs, {}, n, start, finish)


def _ride_send_partials(parts):
    n = len(parts)

    def copies(rin, rout, send, recv):
        x, y, c, chips = _place()
        return [_rcopy(rin[k].at[2 * px + py], rout[k].at[j], send, recv, 3 * k + j, (px, py, c))
                for k in range(n) for j, (px, py) in enumerate(chips)]

    def start(rin, rout, send, recv):
        for cp in copies(rin, rout, send, recv):
            cp.start()

    def finish(rin, rout, send, recv):
        for cp in copies(rin, rout, send, recv):
            cp.wait()

    outs = [jax.ShapeDtypeStruct((3,) + p.shape[1:], p.dtype) for p in parts]
    return _Rider(parts, outs, {}, 3 * n, start, finish)


def _ride_join(bufs):
    n = len(bufs)

    def half_of(buf, pc):
        half = buf.shape[0] // 2
        return buf.at[pl.ds(pc * half, half)]

    def start(rin, rout, send, recv):
        x, y, c, _ = _place()
        for k in range(n):
            mine = half_of(rout[k], c)
            _rcopy(mine, mine, send, recv, k, (x, y, 1 - c)).start()

    def finish(rin, rout, send, recv):
        x, y, c, _ = _place()
        for k in range(n):
            mine, theirs = half_of(rout[k], c), half_of(rout[k], 1 - c)
            _rcopy(mine, mine, send, recv, k, (x, y, c)).wait_send()
            _rcopy(theirs, theirs, send, recv, k, (x, y, c)).wait_recv()

    return _Rider(bufs, _same(bufs), {k: k for k in range(n)}, n, start, finish)


def _run(name, rider):
    def body(*refs):
        nri, nro = len(rider.ins), len(rider.outs)
        rin, rout = refs[:nri], refs[nri:nri + nro]
        send, recv = refs[nri + nro:]
        rider.start(rin, rout, send, recv)
        rider.finish(rin, rout, send, recv)

    return pl.pallas_call(
        body, name=name, in_specs=[HBM] * len(rider.ins), out_specs=[HBM] * len(rider.outs), out_shape=rider.outs,
        scratch_shapes=[pltpu.SemaphoreType.DMA((rider.n_sems,)), pltpu.SemaphoreType.DMA((rider.n_sems,))],
        input_output_aliases=rider.aliases,
    )(*rider.ins)


def _add_halves(name, g, other, c_idx):
    s, r, cols = g.shape
    half = r // 2
    tr = _div_tile(half, 16, 512)
    nb = half // tr

    def body(c_ref, g_ref, o_ref, q_ref):
        del c_ref
        q_ref[...] = (g_ref[...] + o_ref[...]).astype(BF16)

    return pl.pallas_call(
        body, name=name,
        grid_spec=pltpu.PrefetchScalarGridSpec(
            num_scalar_prefetch=1, grid=(s, nb),
            in_specs=[pl.BlockSpec((None, tr, cols), lambda k, i, c: (k, c[0] * nb + i, 0)),
                      pl.BlockSpec((None, tr, cols), lambda k, i, c: (k, i, 0))],
            out_specs=pl.BlockSpec((None, tr, cols), lambda k, i, c: (k, i, 0))),
        out_shape=jax.ShapeDtypeStruct((s, half, cols), BF16),
        compiler_params=_params("parallel", "parallel"),
    )(c_idx, g, other)


def _sum_partials(name, part, arrived, place_idx):
    _, half, cols = part.shape
    tr = _div_tile(half, 16, 512)
    nb = half // tr

    def body(s_ref, p_ref, a_ref, o_ref):
        del s_ref
        o_ref[...] = ((p_ref[...].astype(F32) + a_ref[0].astype(F32)) + a_ref[1].astype(F32)) + a_ref[2].astype(F32)

    return pl.pallas_call(
        body, name=name,
        grid_spec=pltpu.PrefetchScalarGridSpec(
            num_scalar_prefetch=1, grid=(nb,),
            in_specs=[pl.BlockSpec((None, tr, cols), lambda i, s: (s[0], i, 0)),
                      pl.BlockSpec((3, tr, cols), lambda i, s: (0, i, 0))],
            out_specs=pl.BlockSpec((tr, cols), lambda i, s: (s[1] * nb + i, 0))),
        out_shape=jax.ShapeDtypeStruct((2 * half, cols), F32),
        compiler_params=_params("parallel"),
    )(place_idx, part, arrived)


def _small_allreduce(wide_rows, ffn_rows, w, dff, n_wide, n_ffn):
    n_in = len(wide_rows) + len(ffn_rows)

    def body(*refs):
        ins = refs[:n_in]
        s1_ref, s2_ref, r1, r2, p1, p2, send, recv = refs[n_in:]
        x, y, c, _ = _place()
        me = 4 * x + 2 * y + c
        p1[...] = jnp.zeros_like(p1)
        p2[...] = jnp.zeros_like(p2)
        row = 0
        for ref, (_, r, m) in zip(ins, wide_rows):
            if m == 1 and r % SUBLANE == 0 and row % SUBLANE == 0:
                p1[row:row + r, :] = ref[...]
                row += r
                continue
            for rr in range(r):
                for mm in range(m):
                    p1[row:row + 1, :] = ref[rr:rr + 1, mm * w:(mm + 1) * w]
                    row += 1
        row = 0
        for ref, arr in zip(ins[len(wide_rows):], ffn_rows):
            r = arr.shape[0]
            p2[row:row + r, :] = ref[...]
            row += r
        r1[me] = p1[...]
        r2[me] = p2[...]
        cps = []
        for mask in range(1, 8):
            peer = (x ^ (mask >> 2), y ^ ((mask >> 1) & 1), c ^ (mask & 1))
            for a, (src, dst) in enumerate(((p1, r1), (p2, r2))):
                cp = pltpu.make_async_remote_copy(
                    src_ref=src, dst_ref=dst.at[me], send_sem=send.at[a, mask - 1], recv_sem=recv.at[a, mask - 1],
                    device_id=peer, device_id_type=MESH)
                cp.start()
                cps.append(cp)
        for cp in cps:
            cp.wait()
        t1, t2 = r1[0], r2[0]
        for d in range(1, 8):
            t1 = t1 + r1[d]
            t2 = t2 + r2[d]
        s1_ref[...] = t1
        s2_ref[...] = t2

    ins = [a for a, _, _ in wide_rows] + list(ffn_rows)
    return pl.pallas_call(
        body, name="small_allreduce", in_specs=[VMEM_FULL] * n_in, out_specs=[VMEM_FULL, VMEM_FULL],
        out_shape=[jax.ShapeDtypeStruct((n_wide, w), F32), jax.ShapeDtypeStruct((n_ffn, dff), F32)],
        scratch_shapes=[pltpu.VMEM((8, n_wide, w), F32), pltpu.VMEM((8, n_ffn, dff), F32),
                        pltpu.VMEM((n_wide, w), F32), pltpu.VMEM((n_ffn, dff), F32),
                        pltpu.SemaphoreType.DMA((2, 7)), pltpu.SemaphoreType.DMA((2, 7))],
        compiler_params=pltpu.CompilerParams(vmem_limit_bytes=VMEM_LIMIT),
    )(*ins)


def _adamw(w, g, m, v):
    m2 = ADAM_B1 * m + (1.0 - ADAM_B1) * g
    v2 = ADAM_B2 * v + (1.0 - ADAM_B2) * (g * g)
    m_hat = m2 / (1.0 - ADAM_B1 ** ADAM_STEP)
    v_hat = v2 / (1.0 - ADAM_B2 ** ADAM_STEP)
    delta = -ADAM_LR * (m_hat / (jnp.sqrt(v_hat) + ADAM_EPS) + ADAM_WD * w)
    return delta, m2, v2


def _adam_big(name, w, g, m, v):
    r, c = w.shape
    tr = 128 if r % 128 == 0 else r

    def body(w_ref, g_ref, m_ref, v_ref, d_ref, m2_ref, v2_ref):
        d_ref[...], m2_ref[...], v2_ref[...] = _adamw(w_ref[...], g_ref[...], m_ref[...], v_ref[...])

    blk = pl.BlockSpec((tr, c), lambda i: (i, 0))
    return pl.pallas_call(
        body, name=name, grid=(r // tr,), in_specs=[blk] * 4, out_specs=[blk] * 3,
        out_shape=[jax.ShapeDtypeStruct((r, c), F32)] * 3,
        compiler_params=_params("parallel"),
    )(w, g, m, v)


def _adam_small(s1, s2, cw_g, fw_g, lb_logits, triples, layout, w):
    n = len(triples)

    def body(*refs):
        s1_ref, s2_ref, cw_ref, fw_ref, lbl_ref = refs[:5]
        prm = refs[5:5 + 3 * n]
        outs = refs[5 + 3 * n:]
        for p, lay in enumerate(layout):
            w_ref, m_ref, v_ref = prm[3 * p:3 * p + 3]
            g_ref, d_ref, m2_ref, v2_ref = outs[4 * p:4 * p + 4]
            if lay[0] == "wide":
                _, row, r, pieces = lay
                for rr in range(r):
                    for mm in range(pieces):
                        g_ref[rr:rr + 1, mm * w:(mm + 1) * w] = s1_ref[row:row + 1, :]
                        row += 1
            elif lay[0] == "ffn":
                _, row, r = lay
                g_ref[...] = s2_ref[row:row + r, :]
            elif lay[0] == "cw":
                g_ref[...] = cw_ref[0:g_ref.shape[0], :]
            elif lay[0] == "fw":
                g_ref[...] = fw_ref[0:g_ref.shape[0], :]
            else:
                s0 = _lower_bound(lbl_ref)
                d0 = s1_ref[lay[1]:lay[1] + 1, :] * s0 * (1.0 - s0)
                g_ref[0:1, :] = d0
                g_ref[1:2, :] = -d0
            d_ref[...], m2_ref[...], v2_ref[...] = _adamw(w_ref[...], g_ref[...], m_ref[...], v_ref[...])

    flat = [a for tr in triples for a in tr]
    shapes = []
    for tr in triples:
        shapes.extend([jax.ShapeDtypeStruct(tr[0].shape, F32)] * 4)
    return pl.pallas_call(
        body, name="adam_small", in_specs=[VMEM_FULL] * (5 + 3 * n), out_specs=[VMEM_FULL] * (4 * n),
        out_shape=shapes, compiler_params=pltpu.CompilerParams(vmem_limit_bytes=VMEM_LIMIT),
    )(s1, s2, cw_g, fw_g, lb_logits, *flat)


def _row_tile(t):
    return 512 if t % 512 == 0 and t >= 2048 else 128


def kernel(x, emb_ln_g, emb_ln_b, w_in, conv_w, conv_b, conv_norm_g, conv_norm_b, lb_logits, hgrn_norm_g, w_out, ln1_g, ln1_b, w_ffn_up, ffn_conv_w, ffn_conv_b, w_ffn_down, ln2_g, ln2_b, loss_target, m_emb_ln_g, m_emb_ln_b, m_w_in, m_conv_w, m_conv_b, m_conv_norm_g, m_conv_norm_b, m_lb_logits, m_hgrn_norm_g, m_w_out, m_ln1_g, m_ln1_b, m_w_ffn_up, m_ffn_conv_w, m_ffn_conv_b, m_w_ffn_down, m_ln2_g, m_ln2_b, v_emb_ln_g, v_emb_ln_b, v_w_in, v_conv_w, v_conv_b, v_conv_norm_g, v_conv_norm_b, v_lb_logits, v_hgrn_norm_g, v_w_out, v_ln1_g, v_ln1_b, v_w_ffn_up, v_ffn_conv_w, v_ffn_conv_b, v_w_ffn_down, v_ln2_g, v_ln2_b):
    depth = w_in.shape[0]
    assert depth == 1 and x.shape[0] == 1
    alpha = (2.0 * depth) ** 0.25
    t, d = x.shape[1], x.shape[2]
    w = d // 2
    dff = ffn_conv_b.shape[1]
    kc = conv_w.shape[1]
    assert w % (2 * LANE) == 0 and dff % (4 * LANE) == 0 and t % 128 == 0
    tm = _row_tile(t)
    tm2 = tm // 2
    cb = 2 * LANE
    cbf = 4 * LANE
    tb = tm

    xi = lax.axis_index("x")
    yi = lax.axis_index("y")
    ci = lax.axis_index("c")
    chip = 2 * xi + yi
    c_idx = jnp.reshape(ci, (1,)).astype(jnp.int32)
    chip_idx = jnp.reshape(chip, (1,)).astype(jnp.int32)
    place_idx = jnp.stack([chip, ci]).astype(jnp.int32)

    x2 = x[0]
    tgt = loss_target[0]
    g0, b0 = emb_ln_g.reshape(1, d), emb_ln_b.reshape(1, d)
    w_in2, w_out2, w_up2, w_dn2 = w_in[0], w_out[0], w_ffn_up[0], w_ffn_down[0]
    cw2, fw2 = conv_w[0], ffn_conv_w[0]

    b_in = _place_shard(w_in2, "place_w_in", chip_idx, BF16)
    b_out = _place_shard(w_out2, "place_w_out", chip_idx, BF16)
    b_up = _place_shard(w_up2, "place_w_up", chip_idx, BF16)
    b_dn = _place_shard(w_dn2, "place_w_down", chip_idx, BF16)
    b_cw = _place_shard(_pad_rows(cw2), "place_conv_w", chip_idx, F32)
    b_fw = _place_shard(_pad_rows(fw2), "place_ffn_conv_w", chip_idx, F32)
    first = _run("gather_first_ici", _ride_gather_ici([b_in, b_cw, b_fw]))
    w_in3, cw_full3, fw_full3 = _run("gather_first_d2d", _ride_gather_d2d(first))
    cw_full = _unshard_cols(cw_full3)[:kc]
    fw_full = _unshard_cols(fw_full3)[:fw2.shape[0]]

    h0b = _ln0(x2, g0, b0, tm)
    p3, got = _proj("in_proj", h0b, w_in3, 6, tm, w // 2, rider=_ride_gather_ici([b_out, b_dn]))
    (cat, u1), (w_out3, w_dn3) = _conv_fwd(p3, cw_full, conv_b, conv_norm_g, conv_norm_b, tm2, cb,
                                            rider=_ride_gather_d2d(got))
    w_out_full = w_out3.reshape(d, d)
    (cat, o_pre, states), got = _hgrn_fwd(p3, lb_logits, hgrn_norm_g, cat, tb, rider=_ride_gather_ici([b_up]))
    (xhat1, h1b, rstd1), (w_up3,) = _mix_ln1(cat, w_out_full, x2, g0, b0, ln1_g, ln1_b, alpha, tm2,
                                             rider=_ride_gather_d2d(got))
    hh3 = _proj("ffn_up", h1b, w_up3, 2, tm, dff // 4)
    act = _ffn_act_fwd(hh3, fw_full, ffn_conv_b, tm, cbf)
    dz2, dz2b, dg2, db2, loss_row = _down_ln2_loss(act, w_dn3, xhat1, tgt, ln1_g, ln1_b, ln2_g, ln2_b, alpha, tm2)

    ks = dff // N_CHIPS
    dact = _proj_t("ffn_down_t", dz2b, w_dn3.reshape(dff, d), tm, ks)
    dhh3, dfw, dfb = _ffn_act_bwd(dact, hh3, fw_full, ffn_conv_b, tm, cbf)
    tt = tm
    d_w_dn = _wgrad("wgrad_down", act, dz2b, (N_CHIPS, ks, d), (N_CHIPS, 2, t // tt),
                    pl.BlockSpec((tt, ks), lambda s, j, k: (k, s)),
                    pl.BlockSpec((tt, d // 2), lambda s, j, k: (k, j)),
                    pl.BlockSpec((None, ks, d // 2), lambda s, j, k: (s, 0, j)))
    (dz1, dz1b, dg1, db1), (arr_dn,) = _up_t_ln1_bwd(dhh3, w_up3, dz2, xhat1, rstd1, ln1_g, alpha, tm2,
                                                   rider=_ride_swap([d_w_dn]))
    part_dn = _add_halves("add_halves_w_down", d_w_dn, arr_dn, c_idx)
    wu = 2 * dff // N_CHIPS
    tnu = wu // 2
    per_sec_u = dff // tnu
    d_w_up, (land_dn,) = _wgrad(
        "wgrad_up", h1b, dhh3, (N_CHIPS, d, wu), (N_CHIPS, 2, 2, t // tt),
        pl.BlockSpec((tt, d // 2), lambda s, r, j, k: (k, r)),
        pl.BlockSpec((None, tt, tnu), lambda s, r, j, k: ((2 * s + j) // per_sec_u, k, (2 * s + j) % per_sec_u)),
        pl.BlockSpec((None, d // 2, tnu), lambda s, r, j, k: (s, r, j)), rider=_ride_send_partials([part_dn]))
    dcat, (arr_up,) = _proj_t("out_proj_t", dz1b, w_out_full, tm, d // 2, rider=_ride_swap([d_w_up]))
    part_up = _add_halves("add_halves_w_up", d_w_up, arr_up, c_idx)
    d_w_out = _wgrad("wgrad_out", cat, dz1b, (d, d), (2, 2, t // tt),
                     pl.BlockSpec((tt, d // 2), lambda r, j, k: (k, r)),
                     pl.BlockSpec((tt, d // 2), lambda r, j, k: (k, j)),
                     pl.BlockSpec((d // 2, d // 2), lambda r, j, k: (r, j))).reshape(N_CHIPS, d // N_CHIPS, d)
    du1, dcng, dcnb = _conv_norm_bwd(dcat, u1, conv_norm_g, conv_norm_b, tm)
    (dp3, dcw, dcb), (arr_out,) = _conv_bwd(du1, p3, cw_full, tm2, cb, rider=_ride_swap([d_w_out]))
    part_out = _add_halves("add_halves_w_out", d_w_out, arr_out, c_idx)
    (dp3, dlb, dhg), (land_up, land_out) = _hgrn_bwd(p3, lb_logits, hgrn_norm_g, o_pre, states, dcat, dp3, tb,
                                                    rider=_ride_send_partials([part_up, part_out]))
    wi = 6 * w // N_CHIPS
    tni = w // 2
    d_w_in = _wgrad("wgrad_in", h0b, dp3, (N_CHIPS, d, wi), (N_CHIPS, 2, wi // tni, t // tt),
                    pl.BlockSpec((tt, d // 2), lambda s, r, j, k: (k, r)),
                    pl.BlockSpec((None, tt, tni), lambda s, r, j, k: (((wi // tni) * s + j) // 2, k, ((wi // tni) * s + j) % 2)),
                    pl.BlockSpec((None, d // 2, tni), lambda s, r, j, k: (s, r, j)))
    (arr_in,) = _run("swap_w_in", _ride_swap([d_w_in]))
    part_in = _add_halves("add_halves_w_in", d_w_in, arr_in, c_idx)
    (dx, dg0, db0), (land_in,) = _in_t_ln0_bwd(dp3, w_in3, dz1, x2, g0, alpha, tm2,
                                               rider=_ride_send_partials([part_in]))
    halves = [_sum_partials("sum_partials_" + nm, p, a, place_idx)
              for nm, p, a in (("w_in", part_in, land_in), ("w_out", part_out, land_out),
                               ("w_up", part_up, land_up), ("w_down", part_dn, land_dn))]
    g_w_in, g_w_out, g_w_up, g_w_dn = _run("join_halves", _ride_join(halves))

    kpad = dcw.shape[0]
    wide = [(dcw, kpad, 1), (dg0, 1, 2), (db0, 1, 2), (dg1, 1, 2), (db1, 1, 2), (dg2, 1, 2), (db2, 1, 2),
            (dcb, 1, 1), (dcng, 1, 1), (dcnb, 1, 1), (dlb, 1, 1), (dhg, 1, 1)]
    n_wide = sum(r * m for _, r, m in wide)
    n_wide_pad = -(-n_wide // SUBLANE) * SUBLANE
    s1, s2 = _small_allreduce(wide, [dfw, dfb], w, dff, n_wide_pad, 2 * SUBLANE)
    cw_g = lax.dynamic_slice_in_dim(s1[0:kpad], chip * (w // N_CHIPS), w // N_CHIPS, axis=1)
    fw_g = lax.dynamic_slice_in_dim(s2[0:SUBLANE], chip * (dff // N_CHIPS), dff // N_CHIPS, axis=1)

    small = [
        (g0, m_emb_ln_g.reshape(1, d), v_emb_ln_g.reshape(1, d)), (b0, m_emb_ln_b.reshape(1, d), v_emb_ln_b.reshape(1, d)),
        (cw2, m_conv_w[0], v_conv_w[0]), (conv_b, m_conv_b, v_conv_b),
        (conv_norm_g, m_conv_norm_g, v_conv_norm_g), (conv_norm_b, m_conv_norm_b, v_conv_norm_b),
        (lb_logits, m_lb_logits, v_lb_logits), (hgrn_norm_g, m_hgrn_norm_g, v_hgrn_norm_g),
        (ln1_g, m_ln1_g, v_ln1_g), (ln1_b, m_ln1_b, v_ln1_b),
        (fw2, m_ffn_conv_w[0], v_ffn_conv_w[0]), (ffn_conv_b, m_ffn_conv_b, v_ffn_conv_b),
        (ln2_g, m_ln2_g, v_ln2_g), (ln2_b, m_ln2_b, v_ln2_b),
    ]
    r0 = kpad
    layout = [("wide", r0, 1, 2), ("wide", r0 + 2, 1, 2), ("cw",), ("wide", r0 + 12, 1, 1), ("wide", r0 + 13, 1, 1),
              ("wide", r0 + 14, 1, 1), ("lb", r0 + 15), ("wide", r0 + 16, 1, 1), ("wide", r0 + 4, 1, 2),
              ("wide", r0 + 6, 1, 2), ("fw",), ("ffn", SUBLANE, 1), ("wide", r0 + 8, 1, 2), ("wide", r0 + 10, 1, 2)]
    so = _adam_small(s1, s2, cw_g, fw_g, lb_logits, small, layout, w)
    sm = {nm: so[4 * i:4 * i + 4] for i, nm in enumerate(
        ["emb_ln_g", "emb_ln_b", "conv_w", "conv_b", "conv_norm_g", "conv_norm_b", "lb_logits", "hgrn_norm_g",
         "ln1_g", "ln1_b", "ffn_conv_w", "ffn_conv_b", "ln2_g", "ln2_b"])}
    bigs = {}
    for nm, wt, g, m, v in (("w_in", w_in2, g_w_in, m_w_in[0], v_w_in[0]), ("w_out", w_out2, g_w_out, m_w_out[0], v_w_out[0]),
                            ("w_ffn_up", w_up2, g_w_up, m_w_ffn_up[0], v_w_ffn_up[0]),
                            ("w_ffn_down", w_dn2, g_w_dn, m_w_ffn_down[0], v_w_ffn_down[0])):
        bigs[nm] = (g,) + tuple(_adam_big("adam_" + nm, wt, g, m, v))

    loss = lax.psum(loss_row[0, 0], ("x", "y", "c"))

    order = ["emb_ln_g", "emb_ln_b", "w_in", "conv_w", "conv_b", "conv_norm_g", "conv_norm_b", "lb_logits",
             "hgrn_norm_g", "w_out", "ln1_g", "ln1_b", "w_ffn_up", "ffn_conv_w", "ffn_conv_b", "w_ffn_down",
             "ln2_g", "ln2_b"]
    shapes = dict(emb_ln_g=emb_ln_g.shape, emb_ln_b=emb_ln_b.shape, w_in=w_in.shape, conv_w=conv_w.shape,
                  w_out=w_out.shape, w_ffn_up=w_ffn_up.shape, ffn_conv_w=ffn_conv_w.shape, w_ffn_down=w_ffn_down.shape)
    outs = [loss, dx.reshape(x.shape)]
    for which in range(4):
        for nm in order:
            a = bigs[nm][which] if nm in bigs else sm[nm][which]
            outs.append(a.reshape(shapes[nm]) if nm in shapes else a)
    return tuple(outs)


def _pad_rows(a):
    k = a.shape[0]
    kp = -(-k // 16) * 16
    return jnp.pad(a, ((0, kp - k), (0, 0)))


def _unshard_cols(a3):
    s, k, c = a3.shape
    return jnp.transpose(a3, (1, 0, 2)).reshape(k, s * c)
```

```python
import functools

import jax
import jax.numpy as jnp
from jax import lax
from jax.experimental import pallas as pl
from jax.experimental.pallas import tpu as pltpu

F32 = jnp.float32
BF16 = jnp.bfloat16

LN_EPS = 1e-5
RMS_EPS = 1e-6
LANE = 128
SUBLANE = 8
CHUNK = 64
SUB = 8
HALO = 32
FHALO = 8
ROWS = 64
N_CHIPS = 4
VMEM_LIMIT = 56 << 20
NEG_BIG = -1e30

ADAM_LR = 0.001
ADAM_B1 = 0.9
ADAM_B2 = 0.999
ADAM_EPS = 1e-08
ADAM_WD = 0.01
ADAM_STEP = 10

MESH = pl.DeviceIdType.MESH
HBM = pl.BlockSpec(memory_space=pl.ANY)
VMEM_FULL = pl.BlockSpec(memory_space=pltpu.VMEM)


def _params(*sem):
    return pltpu.CompilerParams(dimension_semantics=sem, vmem_limit_bytes=VMEM_LIMIT)


class _Rider:
    def __init__(self, ins, outs, aliases, n_sems, start, finish):
        self.ins, self.outs, self.aliases = list(ins), list(outs), dict(aliases)
        self.n_sems, self.start, self.finish = n_sems, start, finish


def _call(body, args, *, name, grid, in_specs, out_specs, out_shape, scratch_shapes=(), aliases=None, rider=None):
    many = isinstance(out_shape, (list, tuple))
    shapes = list(out_shape) if many else [out_shape]
    ospecs = list(out_specs) if many else [out_specs]
    aliases = dict(aliases or {})
    sem = ("arbitrary",) * len(grid)
    if rider is None:
        res = pl.pallas_call(
            body, name=name, grid=grid, in_specs=list(in_specs), out_specs=ospecs, out_shape=shapes,
            scratch_shapes=list(scratch_shapes), input_output_aliases=aliases, compiler_params=_params(*sem))(*args)
        return res if many else res[0]
    n_in, n_out, n_scr = len(args), len(shapes), len(scratch_shapes)
    nri, nro = len(rider.ins), len(rider.outs)

    def wrapped(*refs):
        ins, rin = refs[:n_in], refs[n_in:n_in + nri]
        o0 = n_in + nri
        outs, rout = refs[o0:o0 + n_out], refs[o0 + n_out:o0 + n_out + nro]
        s0 = o0 + n_out + nro
        scr, (send, recv) = refs[s0:s0 + n_scr], refs[s0 + n_scr:]
        ids = [pl.program_id(a) for a in range(len(grid))]
        first = functools.reduce(jnp.logical_and, [i == 0 for i in ids])
        last = functools.reduce(jnp.logical_and, [i == g - 1 for i, g in zip(ids, grid)])

        @pl.when(first)
        def _():
            rider.start(rin, rout, send, recv)

        body(*ins, *outs, *scr)

        @pl.when(last)
        def _():
            rider.finish(rin, rout, send, recv)

    for ri, ro in rider.aliases.items():
        aliases[n_in + ri] = n_out + ro
    res = pl.pallas_call(
        wrapped, name=name, grid=grid, in_specs=list(in_specs) + [HBM] * nri, out_specs=ospecs + [HBM] * nro,
        out_shape=shapes + rider.outs,
        scratch_shapes=list(scratch_shapes) + [pltpu.SemaphoreType.DMA((rider.n_sems,)),
                                               pltpu.SemaphoreType.DMA((rider.n_sems,))],
        input_output_aliases=aliases, compiler_params=_params(*sem))(*args, *rider.ins)
    main, extra = res[:n_out], list(res[n_out:])
    return (list(main) if many else main[0]), extra


def _div_tile(n, mult, cap):
    best = n
    for t in range(mult, min(n, cap) + 1, mult):
        if n % t == 0:
            best = t
    return best


def _sigmoid(x):
    return 1.0 / (1.0 + jnp.exp(-x))


def _ln_stats(x):
    mu = jnp.mean(x, axis=-1, keepdims=True)
    xc = x - mu
    var = jnp.mean(xc * xc, axis=-1, keepdims=True)
    rstd = lax.rsqrt(var + LN_EPS)
    return xc * rstd, rstd


def _ln_bwd(dy, xhat, rstd, g):
    dyg = dy * g
    m1 = jnp.mean(dyg, axis=-1, keepdims=True)
    m2 = jnp.mean(dyg * xhat, axis=-1, keepdims=True)
    return rstd * (dyg - m1 - xhat * m2)


def _dot_nt(a, b):
    return lax.dot_general(a, b, (((1,), (1,)), ((), ())), preferred_element_type=F32)


def _dot_tn(a, b):
    return lax.dot_general(a, b, (((0,), (0,)), ((), ())), preferred_element_type=F32)


def _dot(a, b):
    return jnp.dot(a, b, preferred_element_type=F32)


def _dot3(m, x):
    mb = m.astype(BF16)
    x1 = x.astype(BF16)
    r1 = x - x1.astype(F32)
    x2 = r1.astype(BF16)
    x3 = (r1 - x2.astype(F32)).astype(BF16)
    return _dot(mb, x1) + _dot(mb, x2) + _dot(mb, x3)


def _place_shard(x, name, chip_idx, dtype):
    r, c = x.shape
    tr = _div_tile(r, 16, 512)

    def body(s_ref, x_ref, o_ref):
        del s_ref
        o_ref[...] = x_ref[...].astype(dtype)

    return pl.pallas_call(
        body, name=name,
        grid_spec=pltpu.PrefetchScalarGridSpec(
            num_scalar_prefetch=1, grid=(r // tr,),
            in_specs=[pl.BlockSpec((tr, c), lambda i, s: (i, 0))],
            out_specs=pl.BlockSpec((None, tr, c), lambda i, s: (s[0], i, 0))),
        out_shape=jax.ShapeDtypeStruct((N_CHIPS, r, c), dtype),
        compiler_params=_params("parallel"),
    )(chip_idx, x)


def _ln0(x, g, b, tm):
    t, d = x.shape

    def body(x_ref, g_ref, b_ref, o_ref):
        xh, _ = _ln_stats(x_ref[...])
        o_ref[...] = (xh * g_ref[...] + b_ref[...]).astype(BF16)

    row = pl.BlockSpec((1, d), lambda i: (0, 0))
    return pl.pallas_call(
        body, name="ln0", grid=(t // tm,),
        in_specs=[pl.BlockSpec((tm, d), lambda i: (i, 0)), row, row],
        out_specs=pl.BlockSpec((tm, d), lambda i: (i, 0)),
        out_shape=jax.ShapeDtypeStruct((t, d), BF16),
        compiler_params=_params("parallel"),
    )(x, g, b)


def _proj(name, a, w3, n_sec, tm, tn, rider=None):
    m, k = a.shape
    s, _, ws = w3.shape
    sec_w = s * ws // n_sec
    nj = ws // tn
    per_sec = sec_w // tn

    def body(a_ref, w_ref, o_ref):
        o_ref[...] = _dot(a_ref[...], w_ref[...])

    return _call(
        body, (a, w3), name=name, grid=(s * nj, m // tm),
        in_specs=[pl.BlockSpec((tm, k), lambda j, i: (i, 0)),
                  pl.BlockSpec((None, k, tn), lambda j, i: (j // nj, 0, j % nj))],
        out_specs=pl.BlockSpec((None, tm, tn), lambda j, i: (j // per_sec, i, j % per_sec)),
        out_shape=jax.ShapeDtypeStruct((n_sec, m, sec_w), F32), rider=rider)


def _proj_t(name, a, w, tm, tn, rider=None):
    m, k = a.shape
    n = w.shape[0]

    def body(a_ref, w_ref, o_ref):
        o_ref[...] = _dot_nt(a_ref[...], w_ref[...])

    return _call(
        body, (a, w), name=name, grid=(n // tn, m // tm),
        in_specs=[pl.BlockSpec((tm, k), lambda j, i: (i, 0)),
                  pl.BlockSpec((tn, k), lambda j, i: (j, 0))],
        out_specs=pl.BlockSpec((tm, tn), lambda j, i: (i, j)),
        out_shape=jax.ShapeDtypeStruct((m, n), F32), rider=rider)


def _wgrad(name, a, b, out_shape, grid, a_spec, b_spec, o_spec, rider=None, dot=_dot_tn):
    nt = len(grid) - 1

    def body(a_ref, b_ref, o_ref):
        t = pl.program_id(nt)
        prod = dot(a_ref[...], b_ref[...])

        @pl.when(t == 0)
        def _():
            o_ref[...] = prod

        @pl.when(t > 0)
        def _():
            o_ref[...] += prod

    return _call(
        body, (a, b), name=name, grid=grid, in_specs=[a_spec, b_spec], out_specs=o_spec,
        out_shape=jax.ShapeDtypeStruct(out_shape, F32), rider=rider)


def _mix_ln1(cat, w_out, x, g0, b0, g1, b1, alpha, tm, rider=None):
    t, d = x.shape

    def body(cat_ref, w_ref, x_ref, g0_ref, b0_ref, g1_ref, b1_ref, xh_ref, h1b_ref, rstd_ref):
        mix = _dot(cat_ref[...], w_ref[...])
        xh0, _ = _ln_stats(x_ref[...])
        z1 = alpha * (xh0 * g0_ref[...] + b0_ref[...]) + mix
        xh1, rstd1 = _ln_stats(z1)
        xh_ref[...] = xh1
        h1b_ref[...] = (xh1 * g1_ref[...] + b1_ref[...]).astype(BF16)
        rstd_ref[...] = rstd1

    row = pl.BlockSpec((1, d), lambda i: (0, 0))
    blk = pl.BlockSpec((tm, d), lambda i: (i, 0))
    return _call(
        body, (cat, w_out, x, g0, b0, g1, b1), name="mix_ln1", grid=(t // tm,),
        in_specs=[blk, pl.BlockSpec((d, d), lambda i: (0, 0)), blk, row, row, row, row],
        out_specs=[blk, blk, pl.BlockSpec((tm, 1), lambda i: (i, 0))],
        out_shape=[jax.ShapeDtypeStruct((t, d), F32), jax.ShapeDtypeStruct((t, d), BF16),
                   jax.ShapeDtypeStruct((t, 1), F32)], rider=rider)


def _ln2_loss(ffn, xhat1, tgt, g1, b1, g2, b2, alpha, tm):
    t, d = xhat1.shape
    ni = t // tm
    inv_d = 1.0 / d

    def body(ffn_ref, xh1_ref, tgt_ref, g1_ref, b1_ref, g2_ref, b2_ref,
             dz2_ref, dz2b_ref, dg2_ref, db2_ref, loss_ref, lrow):
        i = pl.program_id(0)
        h1 = xh1_ref[...] * g1_ref[...] + b1_ref[...]
        xh2, rstd2 = _ln_stats(alpha * h1 + ffn_ref[...])
        g2v = g2_ref[...]
        diff = xh2 * g2v + b2_ref[...] - tgt_ref[...]
        dh2 = diff * inv_d
        sq = jnp.sum(diff * diff, axis=0, keepdims=True)
        dg = jnp.sum(dh2 * xh2, axis=0, keepdims=True)
        db = jnp.sum(dh2, axis=0, keepdims=True)

        @pl.when(i == 0)
        def _():
            lrow[...] = sq
            dg2_ref[...] = dg
            db2_ref[...] = db

        @pl.when(i > 0)
        def _():
            lrow[...] += sq
            dg2_ref[...] += dg
            db2_ref[...] += db

        dz2 = _ln_bwd(dh2, xh2, rstd2, g2v)
        dz2_ref[...] = dz2
        dz2b_ref[...] = dz2.astype(BF16)

        @pl.when(i == ni - 1)
        def _():
            tot = jnp.sum(lrow[...], axis=-1, keepdims=True) * (0.5 * inv_d)
            loss_ref[...] = jnp.broadcast_to(tot, (1, LANE))

    row = pl.BlockSpec((1, d), lambda i: (0, 0))
    blk = pl.BlockSpec((tm, d), lambda i: (i, 0))
    return _call(
        body, (ffn, xhat1, tgt, g1, b1, g2, b2), name="ln2_loss", grid=(ni,),
        in_specs=[blk, blk, blk, row, row, row, row],
        out_specs=[blk, blk, row, row, pl.BlockSpec((1, LANE), lambda i: (0, 0))],
        out_shape=[jax.ShapeDtypeStruct((t, d), F32), jax.ShapeDtypeStruct((t, d), BF16),
                   jax.ShapeDtypeStruct((1, d), F32), jax.ShapeDtypeStruct((1, d), F32),
                   jax.ShapeDtypeStruct((1, LANE), F32)],
        scratch_shapes=[pltpu.VMEM((1, d), F32)])


def _ln1_bwd(pre, dz2, xhat1, rstd1, g1, alpha, tm):
    t, d = dz2.shape

    def body(pre_ref, dz2_ref, xh_ref, rstd_ref, g_ref, dz1_ref, dz1b_ref, dg_ref, db_ref):
        i = pl.program_id(0)
        dh1 = alpha * dz2_ref[...] + pre_ref[...]
        xh = xh_ref[...]
        dg = jnp.sum(dh1 * xh, axis=0, keepdims=True)
        db = jnp.sum(dh1, axis=0, keepdims=True)

        @pl.when(i == 0)
        def _():
            dg_ref[...] = dg
            db_ref[...] = db

        @pl.when(i > 0)
        def _():
            dg_ref[...] += dg
            db_ref[...] += db

        dz1 = _ln_bwd(dh1, xh, rstd_ref[...], g_ref[...])
        dz1_ref[...] = dz1
        dz1b_ref[...] = dz1.astype(BF16)

    row = pl.BlockSpec((1, d), lambda i: (0, 0))
    blk = pl.BlockSpec((tm, d), lambda i: (i, 0))
    return _call(
        body, (pre, dz2, xhat1, rstd1, g1), name="ln1_bwd", grid=(t // tm,),
        in_specs=[blk, blk, blk, pl.BlockSpec((tm, 1), lambda i: (i, 0)), row],
        out_specs=[blk, blk, row, row],
        out_shape=[jax.ShapeDtypeStruct((t, d), F32), jax.ShapeDtypeStruct((t, d), BF16),
                   jax.ShapeDtypeStruct((1, d), F32), jax.ShapeDtypeStruct((1, d), F32)])


def _ln0_bwd(pre, dz1, x, g0, alpha, tm):
    t, d = x.shape

    def body(pre_ref, dz1_ref, x_ref, g_ref, dx_ref, dg_ref, db_ref):
        i = pl.program_id(0)
        dh0 = alpha * dz1_ref[...] + pre_ref[...]
        xh, rstd = _ln_stats(x_ref[...])
        dg = jnp.sum(dh0 * xh, axis=0, keepdims=True)
        db = jnp.sum(dh0, axis=0, keepdims=True)

        @pl.when(i == 0)
        def _():
            dg_ref[...] = dg
            db_ref[...] = db

        @pl.when(i > 0)
        def _():
            dg_ref[...] += dg
            db_ref[...] += db

        dx_ref[...] = _ln_bwd(dh0, xh, rstd, g_ref[...])

    row = pl.BlockSpec((1, d), lambda i: (0, 0))
    blk = pl.BlockSpec((tm, d), lambda i: (i, 0))
    return _call(
        body, (pre, dz1, x, g0), name="ln0_bwd", grid=(t // tm,),
        in_specs=[blk, blk, blk, row], out_specs=[blk, row, row],
        out_shape=[jax.ShapeDtypeStruct((t, d), F32), jax.ShapeDtypeStruct((1, d), F32),
                   jax.ShapeDtypeStruct((1, d), F32)])


def _conv_fwd(p3, conv_w, conv_b, cn_g, cn_b, tc, cb, rider=None):
    _, t, w = p3.shape
    kk = conv_w.shape[0]
    off = HALO - (kk - 1)
    hb = tc // HALO

    def body(a_ref, g_ref, ap_ref, gp_ref, w_ref, b_ref, ng_ref, nb_ref, cat_ref, u1_ref, ext):
        i = pl.program_id(1)
        ext[pl.ds(HALO, tc), :] = a_ref[...] * _sigmoid(g_ref[...])
        prev = ap_ref[...] * _sigmoid(gp_ref[...])
        ext[pl.ds(0, HALO), :] = jnp.where(i > 0, prev, 0.0)
        for r in range(tc // ROWS):
            acc = jnp.broadcast_to(b_ref[...], (ROWS, cb))
            for k in range(kk):
                acc = acc + w_ref[k:k + 1, :] * ext[pl.ds(r * ROWS + off + k, ROWS), :]
            u1_ref[pl.ds(r * ROWS, ROWS), :] = acc
            for g in range(cb // LANE):
                sl = slice(g * LANE, (g + 1) * LANE)
                xh, _ = _ln_stats(acc[:, sl])
                u2 = xh * ng_ref[:, sl] + nb_ref[:, sl]
                cat_ref[pl.ds(r * ROWS, ROWS), sl] = (u2 * _sigmoid(u2)).astype(BF16)

    cur = lambda sec: pl.BlockSpec((None, tc, cb), lambda j, i: (sec, i, j))
    prev = lambda sec: pl.BlockSpec((None, HALO, cb), lambda j, i: (sec, jnp.maximum(i * hb - 1, 0), j))
    row = pl.BlockSpec((1, cb), lambda j, i: (0, j))
    return _call(
        body, (p3, p3, p3, p3, conv_w, conv_b, cn_g, cn_b), name="conv_fwd", grid=(w // cb, t // tc),
        in_specs=[cur(0), cur(1), prev(0), prev(1), pl.BlockSpec((kk, cb), lambda j, i: (0, j)), row, row, row],
        out_specs=[pl.BlockSpec((tc, cb), lambda j, i: (i, j)), pl.BlockSpec((tc, cb), lambda j, i: (i, j))],
        out_shape=[jax.ShapeDtypeStruct((t, 2 * w), BF16), jax.ShapeDtypeStruct((t, w), F32)],
        scratch_shapes=[pltpu.VMEM((tc + HALO, cb), F32)], rider=rider)


def _conv_norm_bwd(dcat, u1, cn_g, cn_b, tc):
    t, w = u1.shape

    def body(du_ref, u1_ref, ng_ref, nb_ref, du1_ref, dg_ref, db_ref):
        i = pl.program_id(0)
        for g in range(w // LANE):
            sl = slice(g * LANE, (g + 1) * LANE)
            ng = ng_ref[:, sl]
            xh, rstd = _ln_stats(u1_ref[:, sl])
            u2 = xh * ng + nb_ref[:, sl]
            sg = _sigmoid(u2)
            du2 = du_ref[:, sl] * (sg * (1.0 + u2 * (1.0 - sg)))
            dg = jnp.sum(du2 * xh, axis=0, keepdims=True)
            db = jnp.sum(du2, axis=0, keepdims=True)

            @pl.when(i == 0)
            def _():
                dg_ref[:, sl] = dg
                db_ref[:, sl] = db

            @pl.when(i > 0)
            def _():
                dg_ref[:, sl] += dg
                db_ref[:, sl] += db

            du1_ref[:, sl] = _ln_bwd(du2, xh, rstd, ng)

    row = pl.BlockSpec((1, w), lambda i: (0, 0))
    blk = pl.BlockSpec((tc, w), lambda i: (i, 0))
    return pl.pallas_call(
        body, name="conv_norm_bwd", grid=(t // tc,),
        in_specs=[blk, blk, row, row], out_specs=[blk, row, row],
        out_shape=[jax.ShapeDtypeStruct((t, w), F32), jax.ShapeDtypeStruct((1, w), F32),
                   jax.ShapeDtypeStruct((1, w), F32)],
        compiler_params=_params("arbitrary"),
    )(dcat, u1, cn_g, cn_b)


def _conv_bwd(du1, p3, conv_w, tc, cb, rider=None):
    n_sec, t, w = p3.shape
    kk = conv_w.shape[0]
    off = HALO - (kk - 1)
    hb = tc // HALO
    nt = t // tc
    kpad = -(-kk // SUBLANE) * SUBLANE

    def body(d_ref, dn_ref, a_ref, g_ref, ap_ref, gp_ref, w_ref, dp_ref, dw_ref, db_ref, extd, extu, wacc, bacc):
        i = pl.program_id(1)

        @pl.when(i == 0)
        def _():
            wacc[...] = jnp.zeros_like(wacc)
            bacc[...] = jnp.zeros_like(bacc)

        extd[pl.ds(0, tc), :] = d_ref[...]
        extd[pl.ds(tc, HALO), :] = jnp.where(i < nt - 1, dn_ref[...], 0.0)
        extu[pl.ds(HALO, tc), :] = a_ref[...] * _sigmoid(g_ref[...])
        extu[pl.ds(0, HALO), :] = jnp.where(i > 0, ap_ref[...] * _sigmoid(gp_ref[...]), 0.0)
        for r in range(tc // ROWS):
            rows = pl.ds(r * ROWS, ROWS)
            acc = jnp.zeros((ROWS, cb), F32)
            for k in range(kk):
                acc = acc + w_ref[k:k + 1, :] * extd[pl.ds(r * ROWS + (kk - 1) - k, ROWS), :]
            a = a_ref[rows, :]
            sg = _sigmoid(g_ref[rows, :])
            dp_ref[0, rows, :] = (acc * sg).astype(BF16)
            dp_ref[1, rows, :] = (acc * a * sg * (1.0 - sg)).astype(BF16)
            d = d_ref[rows, :]
            bacc[...] += jnp.sum(d.reshape(ROWS // SUBLANE, SUBLANE, cb), axis=0)
            for k in range(kk):
                prod = d * extu[pl.ds(r * ROWS + off + k, ROWS), :]
                wacc[k] += jnp.sum(prod.reshape(ROWS // SUBLANE, SUBLANE, cb), axis=0)

        @pl.when(i == nt - 1)
        def _():
            for k in range(kk):
                dw_ref[k:k + 1, :] = jnp.sum(wacc[k], axis=0, keepdims=True)
            if kpad > kk:
                dw_ref[kk:kpad, :] = jnp.zeros((kpad - kk, cb), F32)
            db_ref[...] = jnp.sum(bacc[...], axis=0, keepdims=True)

    cur = lambda sec: pl.BlockSpec((None, tc, cb), lambda j, i: (sec, i, j))
    prev = lambda sec: pl.BlockSpec((None, HALO, cb), lambda j, i: (sec, jnp.maximum(i * hb - 1, 0), j))
    return _call(
        body, (du1, du1, p3, p3, p3, p3, conv_w), name="conv_bwd", grid=(w // cb, nt),
        in_specs=[pl.BlockSpec((tc, cb), lambda j, i: (i, j)),
                  pl.BlockSpec((HALO, cb), lambda j, i: (jnp.minimum((i + 1) * hb, t // HALO - 1), j)),
                  cur(0), cur(1), prev(0), prev(1), pl.BlockSpec((kk, cb), lambda j, i: (0, j))],
        out_specs=[pl.BlockSpec((2, tc, cb), lambda j, i: (0, i, j)),
                   pl.BlockSpec((kpad, cb), lambda j, i: (0, j)),
                   pl.BlockSpec((1, cb), lambda j, i: (0, j))],
        out_shape=[jax.ShapeDtypeStruct((n_sec, t, w), BF16), jax.ShapeDtypeStruct((kpad, w), F32),
                   jax.ShapeDtypeStruct((1, w), F32)],
        scratch_shapes=[pltpu.VMEM((tc + HALO, cb), F32), pltpu.VMEM((tc + HALO, cb), F32),
                        pltpu.VMEM((kk, SUBLANE, cb), F32), pltpu.VMEM((SUBLANE, cb), F32)], rider=rider)


def _ffn_act_fwd(hh3, fw, fb, tc, cb):
    _, t, dff = hh3.shape
    kk = fw.shape[0]
    off = FHALO - (kk - 1)
    hb = tc // FHALO

    def body(g_ref, v_ref, gp_ref, w_ref, b_ref, act_ref, ext):
        i = pl.program_id(1)
        ext[pl.ds(FHALO, tc), :] = g_ref[...]
        ext[pl.ds(0, FHALO), :] = jnp.where(i > 0, gp_ref[...], 0.0)
        gc = jnp.broadcast_to(b_ref[...], (tc, cb))
        for k in range(kk):
            gc = gc + w_ref[k:k + 1, :] * ext[pl.ds(off + k, tc), :]
        act_ref[...] = (gc * _sigmoid(gc) * v_ref[...]).astype(BF16)

    return pl.pallas_call(
        body, name="ffn_act_fwd", grid=(dff // cb, t // tc),
        in_specs=[pl.BlockSpec((None, tc, cb), lambda j, i: (0, i, j)),
                  pl.BlockSpec((None, tc, cb), lambda j, i: (1, i, j)),
                  pl.BlockSpec((None, FHALO, cb), lambda j, i: (0, jnp.maximum(i * hb - 1, 0), j)),
                  pl.BlockSpec((kk, cb), lambda j, i: (0, j)),
                  pl.BlockSpec((1, cb), lambda j, i: (0, j))],
        out_specs=pl.BlockSpec((tc, cb), lambda j, i: (i, j)),
        out_shape=jax.ShapeDtypeStruct((t, dff), BF16),
        scratch_shapes=[pltpu.VMEM((tc + FHALO, cb), F32)],
        compiler_params=_params("parallel", "arbitrary"),
    )(hh3, hh3, hh3, fw, fb)


def _ffn_act_bwd(dact, hh3, fw, fb, tc, cb):
    _, t, dff = hh3.shape
    kk = fw.shape[0]
    off = FHALO - (kk - 1)
    hb = tc // FHALO
    nt = t // tc
    te = tc + FHALO

    def body(da_ref, dan_ref, g_ref, gp_ref, gn_ref, v_ref, vn_ref, w_ref, b_ref,
             dhh_ref, dw_ref, db_ref, gext, dext, wacc, bacc):
        i = pl.program_id(1)

        @pl.when(i == 0)
        def _():
            wacc[...] = jnp.zeros_like(wacc)
            bacc[...] = jnp.zeros_like(bacc)

        gext[pl.ds(0, FHALO), :] = jnp.where(i > 0, gp_ref[...], 0.0)
        gext[pl.ds(FHALO, tc), :] = g_ref[...]
        gext[pl.ds(FHALO + tc, FHALO), :] = gn_ref[...]
        gc = jnp.broadcast_to(b_ref[...], (te, cb))
        for k in range(kk):
            gc = gc + w_ref[k:k + 1, :] * gext[pl.ds(off + k, te), :]
        sg = _sigmoid(gc)
        dsilu = sg * (1.0 + gc * (1.0 - sg))
        live = i < nt - 1
        da_cur = da_ref[...]
        dext[pl.ds(0, tc), :] = da_cur * v_ref[...] * dsilu[0:tc]
        dext[pl.ds(tc, FHALO), :] = jnp.where(live, dan_ref[...] * vn_ref[...] * dsilu[tc:te], 0.0)
        dhh_ref[1] = (da_cur * (gc[0:tc] * sg[0:tc])).astype(BF16)
        dg = jnp.zeros((tc, cb), F32)
        for k in range(kk):
            dg = dg + w_ref[k:k + 1, :] * dext[pl.ds((kk - 1) - k, tc), :]
        dhh_ref[0] = dg.astype(BF16)
        dgc = dext[pl.ds(0, tc), :]
        bacc[...] += jnp.sum(dgc.reshape(tc // SUBLANE, SUBLANE, cb), axis=0)
        for k in range(kk):
            prod = dgc * gext[pl.ds(off + k, tc), :]
            wacc[k] += jnp.sum(prod.reshape(tc // SUBLANE, SUBLANE, cb), axis=0)

        @pl.when(i == nt - 1)
        def _():
            for k in range(kk):
                dw_ref[k:k + 1, :] = jnp.sum(wacc[k], axis=0, keepdims=True)
            dw_ref[kk:SUBLANE, :] = jnp.zeros((SUBLANE - kk, cb), F32)
            db_ref[...] = jnp.sum(bacc[...], axis=0, keepdims=True)

    nxt = lambda i: jnp.minimum((i + 1) * hb, t // FHALO - 1)
    return pl.pallas_call(
        body, name="ffn_act_bwd", grid=(dff // cb, nt),
        in_specs=[pl.BlockSpec((tc, cb), lambda j, i: (i, j)),
                  pl.BlockSpec((FHALO, cb), lambda j, i: (nxt(i), j)),
                  pl.BlockSpec((None, tc, cb), lambda j, i: (0, i, j)),
                  pl.BlockSpec((None, FHALO, cb), lambda j, i: (0, jnp.maximum(i * hb - 1, 0), j)),
                  pl.BlockSpec((None, FHALO, cb), lambda j, i: (0, nxt(i), j)),
                  pl.BlockSpec((None, tc, cb), lambda j, i: (1, i, j)),
                  pl.BlockSpec((None, FHALO, cb), lambda j, i: (1, nxt(i), j)),
                  pl.BlockSpec((kk, cb), lambda j, i: (0, j)),
                  pl.BlockSpec((1, cb), lambda j, i: (0, j))],
        out_specs=[pl.BlockSpec((2, tc, cb), lambda j, i: (0, i, j)),
                   pl.BlockSpec((SUBLANE, cb), lambda j, i: (0, j)),
                   pl.BlockSpec((1, cb), lambda j, i: (0, j))],
        out_shape=[jax.ShapeDtypeStruct((2, t, dff), BF16), jax.ShapeDtypeStruct((SUBLANE, dff), F32),
                   jax.ShapeDtypeStruct((1, dff), F32)],
        scratch_shapes=[pltpu.VMEM((tc + 2 * FHALO, cb), F32), pltpu.VMEM((te, cb), F32),
                        pltpu.VMEM((kk, SUBLANE, cb), F32), pltpu.VMEM((SUBLANE, cb), F32)],
        compiler_params=_params("parallel", "arbitrary"),
    )(dact, dact, hh3, hh3, hh3, hh3, hh3, fw, fb)


def _chunk_consts():
    r = lax.broadcasted_iota(jnp.int32, (CHUNK, CHUNK), 0)
    c = lax.broadcasted_iota(jnp.int32, (CHUNK, CHUNK), 1)
    blk = (r // SUB) * SUB
    tri = (c <= r).astype(F32)
    start = (c < blk).astype(F32)
    end = (c < blk + SUB).astype(F32)
    return jnp.concatenate([tri, start, end, jnp.ones((SUBLANE, CHUNK), F32)], axis=0)


def _gate_terms(q, fpre, lb):
    sf = _sigmoid(fpre)
    fg = lb + (1.0 - lb) * sf
    sq = _sigmoid(q)
    return sf, fg, 1.0 - fg, sq, q * sq


def _decays(g, consts):
    cs = _dot3(consts, g)
    b = cs[0:CHUNK]
    rs = cs[CHUNK:2 * CHUNK]
    re = cs[2 * CHUNK:3 * CHUNK]
    tot = cs[3 * CHUNK:3 * CHUNK + 1]
    return b, rs, re, tot


def _lower_bound(lb_ref):
    l0, l1 = lb_ref[0:1, :], lb_ref[1:2, :]
    mx = jnp.maximum(l0, l1)
    e0, e1 = jnp.exp(l0 - mx), jnp.exp(l1 - mx)
    return e0 / (e0 + e1)


def _scaled_keys(kt, rs, re, rowblk, i):
    scale = jnp.where(rowblk < i, jnp.exp(jnp.minimum(rs[SUB * i:SUB * i + 1, :] - re, 0.0)), 0.0)
    return kt * scale, scale


def _off_diag_scores(qt, kt, rs, re, rowblk):
    a = jnp.zeros((CHUNK, CHUNK), F32)
    for i in range(1, CHUNK // SUB):
        ki, _ = _scaled_keys(kt, rs, re, rowblk, i)
        a = a + _dot_nt(jnp.where(rowblk == i, qt, 0.0).astype(BF16), ki.astype(BF16))
    return a


def _hgrn_fwd(p3, lb_logits, hg, cat, tb, hpb, rider=None):
    _, t, w = p3.shape
    nh = w // LANE
    nc = tb // CHUNK
    assert nh % hpb == 0

    def body(q_ref, f_ref, v_ref, og_ref, lb_ref, hg_ref, cat_in, cat_ref, o_ref, st_ref, state):
        del cat_in
        consts = _chunk_consts()
        lb_all = _lower_bound(lb_ref)
        rowblk = lax.broadcasted_iota(jnp.int32, (CHUNK, 1), 0) // SUB
        rowpos = lax.broadcasted_iota(jnp.int32, (CHUNK, 1), 0) % SUB

        @pl.when(pl.program_id(1) == 0)
        def _():
            state[...] = jnp.zeros_like(state)

        def head(j, c, rows):
            sl = slice(j * LANE, (j + 1) * LANE)
            v = v_ref[rows, sl]
            og = og_ref[rows, sl]
            _, fg, kk, _, qh = _gate_terms(q_ref[rows, sl], f_ref[rows, sl], lb_all[:, sl])
            b, rs, re, tot = _decays(jnp.log(fg), consts)
            qt = qh * jnp.exp(b - rs)
            kt = kk * jnp.exp(re - b)
            vb = v.astype(BF16)
            st = state[j]
            st_ref[j, c] = st
            o = _dot(_off_diag_scores(qt, kt, rs, re, rowblk).astype(BF16), vb)
            o = o + _dot_nt((qh * jnp.exp(b)).astype(BF16), st.astype(BF16))
            for d in range(SUB):
                ks, vs = (kk, v) if d == 0 else (pltpu.roll(kk, d, 0), pltpu.roll(v, d, 0))
                e = 1.0 if d == 0 else jnp.exp(jnp.where(rowpos >= d, b - pltpu.roll(b, d, 0), NEG_BIG))
                o = o + jnp.sum(qh * ks * e, axis=-1, keepdims=True) * vs
            k_up = kk * jnp.exp(tot - b)
            state[j] = st * jnp.exp(tot) + _dot_tn(vb, k_up.astype(BF16))
            o_ref[rows, sl] = o
            r = lax.rsqrt(jnp.mean(o * o, axis=-1, keepdims=True) + RMS_EPS)
            cat_ref[rows, sl] = (o * r * hg_ref[:, sl] * (og * _sigmoid(og))).astype(BF16)

        def chunk(c, carry):
            rows = pl.ds(pl.multiple_of(c * CHUNK, CHUNK), CHUNK)
            for j in range(hpb):
                head(j, c, rows)
            return carry

        lax.fori_loop(0, nc, chunk, 0)

    bw = hpb * LANE
    sec = lambda s: pl.BlockSpec((None, tb, bw), lambda h, i: (s, i, h))
    return _call(
        body, (p3, p3, p3, p3, lb_logits, hg, cat), name="hgrn_fwd", grid=(nh // hpb, t // tb),
        in_specs=[sec(2), sec(3), sec(4), sec(5),
                  pl.BlockSpec((2, bw), lambda h, i: (0, h)),
                  pl.BlockSpec((1, bw), lambda h, i: (0, h)), HBM],
        out_specs=[pl.BlockSpec((tb, bw), lambda h, i: (i, nh // hpb + h)),
                   pl.BlockSpec((tb, bw), lambda h, i: (i, h)),
                   pl.BlockSpec((hpb, nc, LANE, LANE), lambda h, i: (h, i, 0, 0))],
        out_shape=[jax.ShapeDtypeStruct(cat.shape, BF16), jax.ShapeDtypeStruct((t, w), F32),
                   jax.ShapeDtypeStruct((nh, t // CHUNK, LANE, LANE), F32)],
        scratch_shapes=[pltpu.VMEM((hpb, LANE, LANE), F32)], aliases={6: 0}, rider=rider)


def _hgrn_bwd(p3, lb_logits, hg, o_pre, states, dcat, dp3, tb, hpb, rider=None):
    n_sec, t, w = p3.shape
    nh = w // LANE
    assert nh % hpb == 0
    nc = tb // CHUNK
    nb = t // tb

    def body(q_ref, f_ref, v_ref, og_ref, lb_ref, hg_ref, o_ref, st_ref, dc_ref, dp_in,
             dp_ref, dlb_ref, dhg_ref, dstate, stash, lbacc, hgacc):
        del dp_in
        i, half = pl.program_id(1), pl.program_id(2)

        @pl.when(half == 1)
        def _():
            dp_ref[...] = stash[...]

        @pl.when(half == 0)
        def _():
            consts = _chunk_consts()
            rr = lax.broadcasted_iota(jnp.int32, (CHUNK, CHUNK), 0)
            cc = lax.broadcasted_iota(jnp.int32, (CHUNK, CHUNK), 1)
            upper = (cc >= rr).astype(F32)
            lb_all = _lower_bound(lb_ref)
            rowblk = lax.broadcasted_iota(jnp.int32, (CHUNK, 1), 0) // SUB
            rowpos = lax.broadcasted_iota(jnp.int32, (CHUNK, 1), 0) % SUB

            @pl.when(i == 0)
            def _():
                dstate[...] = jnp.zeros_like(dstate)
                lbacc[...] = jnp.zeros_like(lbacc)
                hgacc[...] = jnp.zeros_like(hgacc)

            def head(j, c, rows):
                sl = slice(j * LANE, (j + 1) * LANE)
                lb = lb_all[:, sl]
                hgv = hg_ref[:, sl]
                q = q_ref[rows, sl]
                v = v_ref[rows, sl]
                og = og_ref[rows, sl]
                o = o_ref[rows, sl]
                dcg = dc_ref[rows, sl]
                sf, fg, kk, sq, qh = _gate_terms(q, f_ref[rows, sl], lb)
                b, rs, re, tot = _decays(jnp.log(fg), consts)
                eq = jnp.exp(b - rs)
                ek = jnp.exp(re - b)
                qt = qh * eq
                kt = kk * ek
                e_in = jnp.exp(b)
                e_up = jnp.exp(tot - b)
                e_tot = jnp.exp(tot)
                q_in = (qh * e_in).astype(BF16)
                k_up = (kk * e_up).astype(BF16)
                vb = v.astype(BF16)
                st = st_ref[j, c]
                dst = dstate[j]
                dstb = dst.astype(BF16)

                sg = _sigmoid(og)
                r = lax.rsqrt(jnp.mean(o * o, axis=-1, keepdims=True) + RMS_EPS)
                ohat = o * r
                d_og = dcg * ohat * hgv * (sg * (1.0 + og * (1.0 - sg)))
                d_on = dcg * (og * sg)
                hgacc[:, sl] += jnp.sum((d_on * ohat).reshape(CHUNK // SUBLANE, SUBLANE, LANE), axis=0)
                d_oh = d_on * hgv
                do = r * (d_oh - ohat * jnp.mean(d_oh * ohat, axis=-1, keepdims=True))
                dob = do.astype(BF16)

                da = _dot_nt(dob, vb)
                a_off = jnp.zeros((CHUNK, CHUNK), F32)
                dqt = jnp.zeros((CHUNK, LANE), F32)
                dkt = jnp.zeros((CHUNK, LANE), F32)
                for blk in range(1, CHUNK // SUB):
                    ki, scale = _scaled_keys(kt, rs, re, rowblk, blk)
                    kib = ki.astype(BF16)
                    qib = jnp.where(rowblk == blk, qt, 0.0).astype(BF16)
                    dab = jnp.where(rowblk == blk, da, 0.0).astype(BF16)
                    a_off = a_off + _dot_nt(qib, kib)
                    dqt = dqt + _dot(dab, kib)
                    dkt = dkt + _dot_tn(dab, qib) * scale
                dqh = dqt * eq
                dk = dkt * ek
                dv = _dot_tn(a_off.astype(BF16), dob)

                dqh = dqh + _dot(dob, st.astype(BF16)) * e_in
                dk = dk + _dot(vb, dstb) * e_up
                dv = dv + _dot_nt(k_up, dstb)
                st_end = st * e_tot + _dot_tn(vb, k_up)
                carry_g = jnp.sum(st_end * dst, axis=0, keepdims=True)
                dstate[j] = dst * e_tot + _dot_tn(dob, q_in)

                for d in range(SUB):
                    if d == 0:
                        ks, vs, e = kk, v, 1.0
                    else:
                        ks, vs = pltpu.roll(kk, d, 0), pltpu.roll(v, d, 0)
                        e = jnp.exp(jnp.where(rowpos >= d, b - pltpu.roll(b, d, 0), NEG_BIG))
                    a_d = jnp.sum(qh * ks * e, axis=-1, keepdims=True)
                    da_d = jnp.sum(do * vs, axis=-1, keepdims=True) * e
                    dqh = dqh + da_d * ks
                    ck = da_d * qh
                    cv = a_d * do
                    if d == 0:
                        dk = dk + ck
                        dv = dv + cv
                    else:
                        dk = dk + pltpu.roll(ck, CHUNK - d, 0)
                        dv = dv + pltpu.roll(cv, CHUNK - d, 0)

                dg = _dot3(upper, qh * dqh - kk * dk) + carry_g
                dfg = dg / fg - dk
                lbacc[:, sl] += jnp.sum((dfg * (1.0 - sf)).reshape(CHUNK // SUBLANE, SUBLANE, LANE), axis=0)
                dp_ref[0, rows, sl] = (dqh * (sq * (1.0 + q * (1.0 - sq)))).astype(BF16)
                dp_ref[1, rows, sl] = (dfg * (1.0 - lb) * sf * (1.0 - sf)).astype(BF16)
                stash[0, rows, sl] = dv.astype(BF16)
                stash[1, rows, sl] = d_og.astype(BF16)

            def chunk(cr, carry):
                c = nc - 1 - cr
                rows = pl.ds(pl.multiple_of(c * CHUNK, CHUNK), CHUNK)
                for j in range(hpb):
                    head(j, c, rows)
                return carry

            lax.fori_loop(0, nc, chunk, 0)

            @pl.when(i == nb - 1)
            def _():
                dlb_ref[...] = jnp.sum(lbacc[...], axis=0, keepdims=True)
                dhg_ref[...] = jnp.sum(hgacc[...], axis=0, keepdims=True)

    rev = lambda i: nb - 1 - i
    bw = hpb * LANE
    sec = lambda s: pl.BlockSpec((None, tb, bw), lambda h, i, z: (s, rev(i), h))
    return _call(
        body, (p3, p3, p3, p3, lb_logits, hg, o_pre, states, dcat, dp3), name="hgrn_bwd", grid=(nh // hpb, nb, 2),
        in_specs=[sec(2), sec(3), sec(4), sec(5),
                  pl.BlockSpec((2, bw), lambda h, i, z: (0, h)),
                  pl.BlockSpec((1, bw), lambda h, i, z: (0, h)),
                  pl.BlockSpec((tb, bw), lambda h, i, z: (rev(i), h)),
                  pl.BlockSpec((hpb, nc, LANE, LANE), lambda h, i, z: (h, rev(i), 0, 0)),
                  pl.BlockSpec((tb, bw), lambda h, i, z: (rev(i), nh // hpb + h)), HBM],
        out_specs=[pl.BlockSpec((2, tb, bw), lambda h, i, z: (1 + z, rev(i), h)),
                   pl.BlockSpec((1, bw), lambda h, i, z: (0, h)),
                   pl.BlockSpec((1, bw), lambda h, i, z: (0, h))],
        out_shape=[jax.ShapeDtypeStruct((n_sec, t, w), BF16), jax.ShapeDtypeStruct((1, w), F32),
                   jax.ShapeDtypeStruct((1, w), F32)],
        scratch_shapes=[pltpu.VMEM((hpb, LANE, LANE), F32), pltpu.VMEM((2, tb, bw), BF16),
                        pltpu.VMEM((SUBLANE, bw), F32), pltpu.VMEM((SUBLANE, bw), F32)],
        aliases={9: 0}, rider=rider)


def _place():
    x, y, c = lax.axis_index("x"), lax.axis_index("y"), lax.axis_index("c")
    chips = [(1 - x, y), (x, 1 - y), (1 - x, 1 - y)]
    return x, y, c, chips


def _rows(buf, px, py, pc):
    half = buf.shape[1] // 2
    return buf.at[2 * px + py, pl.ds(pc * half, half)]


def _rcopy(src, dst, send, recv, idx, to):
    return pltpu.make_async_remote_copy(src_ref=src, dst_ref=dst, send_sem=send.at[idx], recv_sem=recv.at[idx],
                                        device_id=to, device_id_type=MESH)


def _same(bufs):
    return [jax.ShapeDtypeStruct(b.shape, b.dtype) for b in bufs]


def _ride_gather_ici(bufs):
    n = len(bufs)

    def start(rin, rout, send, recv):
        x, y, c, chips = _place()
        for k in range(n):
            mine = _rows(rout[k], x, y, c)
            for j, chip in enumerate(chips):
                _rcopy(mine, mine, send, recv, 3 * k + j, (*chip, c)).start()

    def finish(rin, rout, send, recv):
        x, y, c, chips = _place()
        for k in range(n):
            for j, chip in enumerate(chips):
                theirs = _rows(rout[k], *chip, c)
                _rcopy(theirs, theirs, send, recv, 3 * k + j, (x, y, c)).wait_recv()
        for k in range(n):
            mine = _rows(rout[k], x, y, c)
            for j in range(3):
                _rcopy(mine, mine, send, recv, 3 * k + j, (x, y, c)).wait_send()

    return _Rider(bufs, _same(bufs), {k: k for k in range(n)}, 3 * n, start, finish)


def _ride_gather_d2d(bufs):
    n = len(bufs)

    def start(rin, rout, send, recv):
        x, y, c, chips = _place()
        for k in range(n):
            for j, chip in enumerate(chips):
                got = _rows(rout[k], *chip, c)
                _rcopy(got, got, send, recv, 3 * k + j, (x, y, 1 - c)).start()

    def finish(rin, rout, send, recv):
        x, y, c, chips = _place()
        for k in range(n):
            for j, chip in enumerate(chips):
                theirs = _rows(rout[k], *chip, 1 - c)
                _rcopy(theirs, theirs, send, recv, 3 * k + j, (x, y, c)).wait_recv()
        for k in range(n):
            for j, chip in enumerate(chips):
                got = _rows(rout[k], *chip, c)
                _rcopy(got, got, send, recv, 3 * k + j, (x, y, c)).wait_send()

    return _Rider(bufs, _same(bufs), {k: k for k in range(n)}, 3 * n, start, finish)


def _ride_swap(grads):
    n = len(grads)

    def copy(k, rin, rout, send, recv):
        x, y, c, _ = _place()
        half = rin[k].shape[1] // 2
        return _rcopy(rin[k].at[:, pl.ds((1 - c) * half, half)], rout[k], send, recv, k, (x, y, 1 - c))

    def start(rin, rout, send, recv):
        for k in range(n):
            copy(k, rin, rout, send, recv).start()

    def finish(rin, rout, send, recv):
        for k in range(n):
            copy(k, rin, rout, send, recv).wait()

    outs = [jax.ShapeDtypeStruct((g.shape[0], g.shape[1] // 2, g.shape[2]), g.dtype) for g in grads]
    return _Rider(grads, outs, {}, n, start, finish)


def _ride_send_partials(parts):
    n = len(parts)

    def copies(rin, rout, send, recv):
        x, y, c, chips = _place()
        return [_rcopy(rin[k].at[2 * px + py], rout[k].at[j], send, recv, 3 * k + j, (px, py, c))
                for k in range(n) for j, (px, py) in enumerate(chips)]

    def start(rin, rout, send, recv):
        for cp in copies(rin, rout, send, recv):
            cp.start()

    def finish(rin, rout, send, recv):
        for cp in copies(rin, rout, send, recv):
            cp.wait()

    outs = [jax.ShapeDtypeStruct((3,) + p.shape[1:], p.dtype) for p in parts]
    return _Rider(parts, outs, {}, 3 * n, start, finish)


def _ride_join(bufs):
    n = len(bufs)

    def half_of(buf, pc):
        half = buf.shape[0] // 2
        return buf.at[pl.ds(pc * half, half)]

    def start(rin, rout, send, recv):
        x, y, c, _ = _place()
        for k in range(n):
            mine = half_of(rout[k], c)
            _rcopy(mine, mine, send, recv, k, (x, y, 1 - c)).start()

    def finish(rin, rout, send, recv):
        x, y, c, _ = _place()
        for k in range(n):
            mine, theirs = half_of(rout[k], c), half_of(rout[k], 1 - c)
            _rcopy(mine, mine, send, recv, k, (x, y, c)).wait_send()
            _rcopy(theirs, theirs, send, recv, k, (x, y, c)).wait_recv()

    return _Rider(bufs, _same(bufs), {k: k for k in range(n)}, n, start, finish)


def _run(name, rider):
    def body(*refs):
        nri, nro = len(rider.ins), len(rider.outs)
        rin, rout = refs[:nri], refs[nri:nri + nro]
        send, recv = refs[nri + nro:]
        rider.start(rin, rout, send, recv)
        rider.finish(rin, rout, send, recv)

    return pl.pallas_call(
        body, name=name, in_specs=[HBM] * len(rider.ins), out_specs=[HBM] * len(rider.outs), out_shape=rider.outs,
        scratch_shapes=[pltpu.SemaphoreType.DMA((rider.n_sems,)), pltpu.SemaphoreType.DMA((rider.n_sems,))],
        input_output_aliases=rider.aliases,
    )(*rider.ins)


def _add_halves(name, g, other, c_idx):
    s, r, cols = g.shape
    half = r // 2
    tr = _div_tile(half, 16, 512)
    nb = half // tr

    def body(c_ref, g_ref, o_ref, q_ref):
        del c_ref
        q_ref[...] = (g_ref[...] + o_ref[...]).astype(BF16)

    return pl.pallas_call(
        body, name=name,
        grid_spec=pltpu.PrefetchScalarGridSpec(
            num_scalar_prefetch=1, grid=(s, nb),
            in_specs=[pl.BlockSpec((None, tr, cols), lambda k, i, c: (k, c[0] * nb + i, 0)),
                      pl.BlockSpec((None, tr, cols), lambda k, i, c: (k, i, 0))],
            out_specs=pl.BlockSpec((None, tr, cols), lambda k, i, c: (k, i, 0))),
        out_shape=jax.ShapeDtypeStruct((s, half, cols), BF16),
        compiler_params=_params("parallel", "parallel"),
    )(c_idx, g, other)


def _sum_partials(name, part, arrived, place_idx):
    _, half, cols = part.shape
    tr = _div_tile(half, 16, 512)
    nb = half // tr

    def body(s_ref, p_ref, a_ref, o_ref):
        del s_ref
        o_ref[...] = ((p_ref[...].astype(F32) + a_ref[0].astype(F32)) + a_ref[1].astype(F32)) + a_ref[2].astype(F32)

    return pl.pallas_call(
        body, name=name,
        grid_spec=pltpu.PrefetchScalarGridSpec(
            num_scalar_prefetch=1, grid=(nb,),
            in_specs=[pl.BlockSpec((None, tr, cols), lambda i, s: (s[0], i, 0)),
                      pl.BlockSpec((3, tr, cols), lambda i, s: (0, i, 0))],
            out_specs=pl.BlockSpec((tr, cols), lambda i, s: (s[1] * nb + i, 0))),
        out_shape=jax.ShapeDtypeStruct((2 * half, cols), F32),
        compiler_params=_params("parallel"),
    )(place_idx, part, arrived)


def _small_allreduce(wide_rows, ffn_rows, w, dff, n_wide, n_ffn):
    n_in = len(wide_rows) + len(ffn_rows)

    def body(*refs):
        ins = refs[:n_in]
        s1_ref, s2_ref, r1, r2, p1, p2, send, recv = refs[n_in:]
        x, y, c, _ = _place()
        me = 4 * x + 2 * y + c
        p1[...] = jnp.zeros_like(p1)
        p2[...] = jnp.zeros_like(p2)
        row = 0
        for ref, (_, r, m) in zip(ins, wide_rows):
            if m == 1 and r % SUBLANE == 0 and row % SUBLANE == 0:
                p1[row:row + r, :] = ref[...]
                row += r
                continue
            for rr in range(r):
                for mm in range(m):
                    p1[row:row + 1, :] = ref[rr:rr + 1, mm * w:(mm + 1) * w]
                    row += 1
        row = 0
        for ref, arr in zip(ins[len(wide_rows):], ffn_rows):
            r = arr.shape[0]
            p2[row:row + r, :] = ref[...]
            row += r
        r1[me] = p1[...]
        r2[me] = p2[...]
        cps = []
        for mask in range(1, 8):
            peer = (x ^ (mask >> 2), y ^ ((mask >> 1) & 1), c ^ (mask & 1))
            for a, (src, dst) in enumerate(((p1, r1), (p2, r2))):
                cp = pltpu.make_async_remote_copy(
                    src_ref=src, dst_ref=dst.at[me], send_sem=send.at[a, mask - 1], recv_sem=recv.at[a, mask - 1],
                    device_id=peer, device_id_type=MESH)
                cp.start()
                cps.append(cp)
        for cp in cps:
            cp.wait()
        t1, t2 = r1[0], r2[0]
        for d in range(1, 8):
            t1 = t1 + r1[d]
            t2 = t2 + r2[d]
        s1_ref[...] = t1
        s2_ref[...] = t2

    ins = [a for a, _, _ in wide_rows] + list(ffn_rows)
    return pl.pallas_call(
        body, name="small_allreduce", in_specs=[VMEM_FULL] * n_in, out_specs=[VMEM_FULL, VMEM_FULL],
        out_shape=[jax.ShapeDtypeStruct((n_wide, w), F32), jax.ShapeDtypeStruct((n_ffn, dff), F32)],
        scratch_shapes=[pltpu.VMEM((8, n_wide, w), F32), pltpu.VMEM((8, n_ffn, dff), F32),
                        pltpu.VMEM((n_wide, w), F32), pltpu.VMEM((n_ffn, dff), F32),
                        pltpu.SemaphoreType.DMA((2, 7)), pltpu.SemaphoreType.DMA((2, 7))],
        compiler_params=pltpu.CompilerParams(vmem_limit_bytes=VMEM_LIMIT),
    )(*ins)


def _adamw(w, g, m, v):
    m2 = ADAM_B1 * m + (1.0 - ADAM_B1) * g
    v2 = ADAM_B2 * v + (1.0 - ADAM_B2) * (g * g)
    m_hat = m2 / (1.0 - ADAM_B1 ** ADAM_STEP)
    v_hat = v2 / (1.0 - ADAM_B2 ** ADAM_STEP)
    delta = -ADAM_LR * (m_hat / (jnp.sqrt(v_hat) + ADAM_EPS) + ADAM_WD * w)
    return delta, m2, v2


def _adam_big(name, w, g, m, v):
    r, c = w.shape
    tr = 128 if r % 128 == 0 else r

    def body(w_ref, g_ref, m_ref, v_ref, d_ref, m2_ref, v2_ref):
        d_ref[...], m2_ref[...], v2_ref[...] = _adamw(w_ref[...], g_ref[...], m_ref[...], v_ref[...])

    blk = pl.BlockSpec((tr, c), lambda i: (i, 0))
    return pl.pallas_call(
        body, name=name, grid=(r // tr,), in_specs=[blk] * 4, out_specs=[blk] * 3,
        out_shape=[jax.ShapeDtypeStruct((r, c), F32)] * 3,
        compiler_params=_params("parallel"),
    )(w, g, m, v)


def _adam_small(s1, s2, cw_g, fw_g, lb_logits, triples, layout, w):
    n = len(triples)

    def body(*refs):
        s1_ref, s2_ref, cw_ref, fw_ref, lbl_ref = refs[:5]
        prm = refs[5:5 + 3 * n]
        outs = refs[5 + 3 * n:]
        for p, lay in enumerate(layout):
            w_ref, m_ref, v_ref = prm[3 * p:3 * p + 3]
            g_ref, d_ref, m2_ref, v2_ref = outs[4 * p:4 * p + 4]
            if lay[0] == "wide":
                _, row, r, pieces = lay
                for rr in range(r):
                    for mm in range(pieces):
                        g_ref[rr:rr + 1, mm * w:(mm + 1) * w] = s1_ref[row:row + 1, :]
                        row += 1
            elif lay[0] == "ffn":
                _, row, r = lay
                g_ref[...] = s2_ref[row:row + r, :]
            elif lay[0] == "cw":
                g_ref[...] = cw_ref[0:g_ref.shape[0], :]
            elif lay[0] == "fw":
                g_ref[...] = fw_ref[0:g_ref.shape[0], :]
            else:
                s0 = _lower_bound(lbl_ref)
                d0 = s1_ref[lay[1]:lay[1] + 1, :] * s0 * (1.0 - s0)
                g_ref[0:1, :] = d0
                g_ref[1:2, :] = -d0
            d_ref[...], m2_ref[...], v2_ref[...] = _adamw(w_ref[...], g_ref[...], m_ref[...], v_ref[...])

    flat = [a for tr in triples for a in tr]
    shapes = []
    for tr in triples:
        shapes.extend([jax.ShapeDtypeStruct(tr[0].shape, F32)] * 4)
    return pl.pallas_call(
        body, name="adam_small", in_specs=[VMEM_FULL] * (5 + 3 * n), out_specs=[VMEM_FULL] * (4 * n),
        out_shape=shapes, compiler_params=pltpu.CompilerParams(vmem_limit_bytes=VMEM_LIMIT),
    )(s1, s2, cw_g, fw_g, lb_logits, *flat)


def _row_tile(t):
    return 512 if t % 512 == 0 and t >= 2048 else 128


def kernel(x, emb_ln_g, emb_ln_b, w_in, conv_w, conv_b, conv_norm_g, conv_norm_b, lb_logits, hgrn_norm_g, w_out, ln1_g, ln1_b, w_ffn_up, ffn_conv_w, ffn_conv_b, w_ffn_down, ln2_g, ln2_b, loss_target, m_emb_ln_g, m_emb_ln_b, m_w_in, m_conv_w, m_conv_b, m_conv_norm_g, m_conv_norm_b, m_lb_logits, m_hgrn_norm_g, m_w_out, m_ln1_g, m_ln1_b, m_w_ffn_up, m_ffn_conv_w, m_ffn_conv_b, m_w_ffn_down, m_ln2_g, m_ln2_b, v_emb_ln_g, v_emb_ln_b, v_w_in, v_conv_w, v_conv_b, v_conv_norm_g, v_conv_norm_b, v_lb_logits, v_hgrn_norm_g, v_w_out, v_ln1_g, v_ln1_b, v_w_ffn_up, v_ffn_conv_w, v_ffn_conv_b, v_w_ffn_down, v_ln2_g, v_ln2_b):
    depth = w_in.shape[0]
    assert depth == 1 and x.shape[0] == 1
    alpha = (2.0 * depth) ** 0.25
    t, d = x.shape[1], x.shape[2]
    w = d // 2
    dff = ffn_conv_b.shape[1]
    kc = conv_w.shape[1]
    assert w % (2 * LANE) == 0 and dff % (4 * LANE) == 0 and t % 128 == 0
    tm = _row_tile(t)
    tm2 = tm // 2
    tmm = 1024 if t % 1024 == 0 and t >= 2048 else tm
    cb = 2 * LANE
    cbf = 4 * LANE
    tb = tm
    nh = w // LANE
    hpb_f = 4 if nh % 4 == 0 else 2
    hpb_b = 2

    xi = lax.axis_index("x")
    yi = lax.axis_index("y")
    ci = lax.axis_index("c")
    chip = 2 * xi + yi
    c_idx = jnp.reshape(ci, (1,)).astype(jnp.int32)
    chip_idx = jnp.reshape(chip, (1,)).astype(jnp.int32)
    place_idx = jnp.stack([chip, ci]).astype(jnp.int32)

    x2 = x[0]
    tgt = loss_target[0]
    g0, b0 = emb_ln_g.reshape(1, d), emb_ln_b.reshape(1, d)
    w_in2, w_out2, w_up2, w_dn2 = w_in[0], w_out[0], w_ffn_up[0], w_ffn_down[0]
    cw2, fw2 = conv_w[0], ffn_conv_w[0]

    b_in = _place_shard(w_in2, "place_w_in", chip_idx, BF16)
    b_out = _place_shard(w_out2, "place_w_out", chip_idx, BF16)
    b_up = _place_shard(w_up2, "place_w_up", chip_idx, BF16)
    b_dn = _place_shard(w_dn2, "place_w_down", chip_idx, BF16)
    b_cw = _place_shard(_pad_rows(cw2), "place_conv_w", chip_idx, F32)
    b_fw = _place_shard(_pad_rows(fw2), "place_ffn_conv_w", chip_idx, F32)
    first = _run("gather_first_ici", _ride_gather_ici([b_in, b_cw, b_fw]))
    w_in3, cw_full3, fw_full3 = _run("gather_first_d2d", _ride_gather_d2d(first))
    cw_full = _unshard_cols(cw_full3)[:kc]
    fw_full = _unshard_cols(fw_full3)[:fw2.shape[0]]

    h0b = _ln0(x2, g0, b0, tm)
    p3, got = _proj("in_proj", h0b, w_in3, 6, tmm, w // 2, rider=_ride_gather_ici([b_out, b_dn]))
    (cat, u1), (w_out3, w_dn3) = _conv_fwd(p3, cw_full, conv_b, conv_norm_g, conv_norm_b, tm2, cb,
                                            rider=_ride_gather_d2d(got))
    w_out_full = w_out3.reshape(d, d)
    (cat, o_pre, states), got = _hgrn_fwd(p3, lb_logits, hgrn_norm_g, cat, tb, hpb_f,
                                          rider=_ride_gather_ici([b_up]))
    (xhat1, h1b, rstd1), (w_up3,) = _mix_ln1(cat, w_out_full, x2, g0, b0, ln1_g, ln1_b, alpha, tm2,
                                             rider=_ride_gather_d2d(got))
    hh3 = _proj("ffn_up", h1b, w_up3, 2, tmm, dff // 4)
    act = _ffn_act_fwd(hh3, fw_full, ffn_conv_b, tm, cbf)
    ks = dff // N_CHIPS
    ffn = _wgrad("ffn_down", act, w_dn3, (t, d), (t // tmm, 2, N_CHIPS),
                 pl.BlockSpec((tmm, ks), lambda i, j, k: (i, k)),
                 pl.BlockSpec((None, ks, d // 2), lambda i, j, k: (k, 0, j)),
                 pl.BlockSpec((tmm, d // 2), lambda i, j, k: (i, j)), dot=_dot)
    dz2, dz2b, dg2, db2, loss_row = _ln2_loss(ffn, xhat1, tgt, ln1_g, ln1_b, ln2_g, ln2_b, alpha, tm2)

    dact = _proj_t("ffn_down_t", dz2b, w_dn3.reshape(dff, d), tmm, ks)
    dhh3, dfw, dfb = _ffn_act_bwd(dact, hh3, fw_full, ffn_conv_b, tm, cbf)
    tt = tmm
    d_w_dn = _wgrad("wgrad_down", act, dz2b, (N_CHIPS, ks, d), (N_CHIPS, 2, t // tt),
                    pl.BlockSpec((tt, ks), lambda s, j, k: (k, s)),
                    pl.BlockSpec((tt, d // 2), lambda s, j, k: (k, j)),
                    pl.BlockSpec((None, ks, d // 2), lambda s, j, k: (s, 0, j)))
    wu = 2 * dff // N_CHIPS
    tnu = wu // 2
    per_sec_u = dff // tnu
    pre1, (arr_dn,) = _wgrad(
        "up_t", dhh3, w_up3, (t, d), (t // tmm, 2, 2 * N_CHIPS),
        pl.BlockSpec((None, tmm, tnu), lambda i, j, k: (k // per_sec_u, i, k % per_sec_u)),
        pl.BlockSpec((None, d // 2, tnu), lambda i, j, k: (k // 2, j, k % 2)),
        pl.BlockSpec((tmm, d // 2), lambda i, j, k: (i, j)), dot=_dot_nt, rider=_ride_swap([d_w_dn]))
    dz1, dz1b, dg1, db1 = _ln1_bwd(pre1, dz2, xhat1, rstd1, ln1_g, alpha, tm2)
    part_dn = _add_halves("add_halves_w_down", d_w_dn, arr_dn, c_idx)
    d_w_up, (land_dn,) = _wgrad(
        "wgrad_up", h1b, dhh3, (N_CHIPS, d, wu), (N_CHIPS, 2, 2, t // tt),
        pl.BlockSpec((tt, d // 2), lambda s, r, j, k: (k, r)),
        pl.BlockSpec((None, tt, tnu), lambda s, r, j, k: ((2 * s + j) // per_sec_u, k, (2 * s + j) % per_sec_u)),
        pl.BlockSpec((None, d // 2, tnu), lambda s, r, j, k: (s, r, j)), rider=_ride_send_partials([part_dn]))
    dcat, (arr_up,) = _proj_t("out_proj_t", dz1b, w_out_full, tmm, d // 2, rider=_ride_swap([d_w_up]))
    part_up = _add_halves("add_halves_w_up", d_w_up, arr_up, c_idx)
    d_w_out = _wgrad("wgrad_out", cat, dz1b, (d, d), (2, 2, t // tt),
                     pl.BlockSpec((tt, d // 2), lambda r, j, k: (k, r)),
                     pl.BlockSpec((tt, d // 2), lambda r, j, k: (k, j)),
                     pl.BlockSpec((d // 2, d // 2), lambda r, j, k: (r, j))).reshape(N_CHIPS, d // N_CHIPS, d)
    du1, dcng, dcnb = _conv_norm_bwd(dcat, u1, conv_norm_g, conv_norm_b, tm)
    (dp3, dcw, dcb), (arr_out,) = _conv_bwd(du1, p3, cw_full, tm2, cb, rider=_ride_swap([d_w_out]))
    part_out = _add_halves("add_halves_w_out", d_w_out, arr_out, c_idx)
    (dp3, dlb, dhg), (land_up, land_out) = _hgrn_bwd(p3, lb_logits, hgrn_norm_g, o_pre, states, dcat, dp3, tb, hpb_b,
                                                    rider=_ride_send_partials([part_up, part_out]))
    wi = 6 * w // N_CHIPS
    tni = w // 2
    d_w_in = _wgrad("wgrad_in", h0b, dp3, (N_CHIPS, d, wi), (N_CHIPS, 2, wi // tni, t // tt),
                    pl.BlockSpec((tt, d // 2), lambda s, r, j, k: (k, r)),
                    pl.BlockSpec((None, tt, tni), lambda s, r, j, k: (((wi // tni) * s + j) // 2, k, ((wi // tni) * s + j) % 2)),
                    pl.BlockSpec((None, d // 2, tni), lambda s, r, j, k: (s, r, j)))
    (arr_in,) = _run("swap_w_in", _ride_swap([d_w_in]))
    part_in = _add_halves("add_halves_w_in", d_w_in, arr_in, c_idx)
    per_shard_i = wi // tni
    pre0, (land_in,) = _wgrad(
        "in_t", dp3, w_in3, (t, d), (t // tmm, 2, 2 * dp3.shape[0]),
        pl.BlockSpec((None, tmm, tni), lambda i, j, k: (k // 2, i, k % 2)),
        pl.BlockSpec((None, d // 2, tni), lambda i, j, k: (k // per_shard_i, j, k % per_shard_i)),
        pl.BlockSpec((tmm, d // 2), lambda i, j, k: (i, j)), dot=_dot_nt, rider=_ride_send_partials([part_in]))
    dx, dg0, db0 = _ln0_bwd(pre0, dz1, x2, g0, alpha, tm2)
    halves = [_sum_partials("sum_partials_" + nm, p, a, place_idx)
              for nm, p, a in (("w_in", part_in, land_in), ("w_out", part_out, land_out),
                               ("w_up", part_up, land_up), ("w_down", part_dn, land_dn))]
    g_w_in, g_w_out, g_w_up, g_w_dn = _run("join_halves", _ride_join(halves))

    kpad = dcw.shape[0]
    wide = [(dcw, kpad, 1), (dg0, 1, 2), (db0, 1, 2), (dg1, 1, 2), (db1, 1, 2), (dg2, 1, 2), (db2, 1, 2),
            (dcb, 1, 1), (dcng, 1, 1), (dcnb, 1, 1), (dlb, 1, 1), (dhg, 1, 1)]
    n_wide = sum(r * m for _, r, m in wide)
    n_wide_pad = -(-n_wide // SUBLANE) * SUBLANE
    s1, s2 = _small_allreduce(wide, [dfw, dfb], w, dff, n_wide_pad, 2 * SUBLANE)
    cw_g = lax.dynamic_slice_in_dim(s1[0:kpad], chip * (w // N_CHIPS), w // N_CHIPS, axis=1)
    fw_g = lax.dynamic_slice_in_dim(s2[0:SUBLANE], chip * (dff // N_CHIPS), dff // N_CHIPS, axis=1)

    small = [
        (g0, m_emb_ln_g.reshape(1, d), v_emb_ln_g.reshape(1, d)), (b0, m_emb_ln_b.reshape(1, d), v_emb_ln_b.reshape(1, d)),
        (cw2, m_conv_w[0], v_conv_w[0]), (conv_b, m_conv_b, v_conv_b),
        (conv_norm_g, m_conv_norm_g, v_conv_norm_g), (conv_norm_b, m_conv_norm_b, v_conv_norm_b),
        (lb_logits, m_lb_logits, v_lb_logits), (hgrn_norm_g, m_hgrn_norm_g, v_hgrn_norm_g),
        (ln1_g, m_ln1_g, v_ln1_g), (ln1_b, m_ln1_b, v_ln1_b),
        (fw2, m_ffn_conv_w[0], v_ffn_conv_w[0]), (ffn_conv_b, m_ffn_conv_b, v_ffn_conv_b),
        (ln2_g, m_ln2_g, v_ln2_g), (ln2_b, m_ln2_b, v_ln2_b),
    ]
    r0 = kpad
    layout = [("wide", r0, 1, 2), ("wide", r0 + 2, 1, 2), ("cw",), ("wide", r0 + 12, 1, 1), ("wide", r0 + 13, 1, 1),
              ("wide", r0 + 14, 1, 1), ("lb", r0 + 15), ("wide", r0 + 16, 1, 1), ("wide", r0 + 4, 1, 2),
              ("wide", r0 + 6, 1, 2), ("fw",), ("ffn", SUBLANE, 1), ("wide", r0 + 8, 1, 2), ("wide", r0 + 10, 1, 2)]
    so = _adam_small(s1, s2, cw_g, fw_g, lb_logits, small, layout, w)
    sm = {nm: so[4 * i:4 * i + 4] for i, nm in enumerate(
        ["emb_ln_g", "emb_ln_b", "conv_w", "conv_b", "conv_norm_g", "conv_norm_b", "lb_logits", "hgrn_norm_g",
         "ln1_g", "ln1_b", "ffn_conv_w", "ffn_conv_b", "ln2_g", "ln2_b"])}
    bigs = {}
    for nm, wt, g, m, v in (("w_in", w_in2, g_w_in, m_w_in[0], v_w_in[0]), ("w_out", w_out2, g_w_out, m_w_out[0], v_w_out[0]),
                            ("w_ffn_up", w_up2, g_w_up, m_w_ffn_up[0], v_w_ffn_up[0]),
                            ("w_ffn_down", w_dn2, g_w_dn, m_w_ffn_down[0], v_w_ffn_down[0])):
        bigs[nm] = (g,) + tuple(_adam_big("adam_" + nm, wt, g, m, v))

    loss = lax.psum(loss_row[0, 0], ("x", "y", "c"))

    order = ["emb_ln_g", "emb_ln_b", "w_in", "conv_w", "conv_b", "conv_norm_g", "conv_norm_b", "lb_logits",
             "hgrn_norm_g", "w_out", "ln1_g", "ln1_b", "w_ffn_up", "ffn_conv_w", "ffn_conv_b", "w_ffn_down",
             "ln2_g", "ln2_b"]
    shapes = dict(emb_ln_g=emb_ln_g.shape, emb_ln_b=emb_ln_b.shape, w_in=w_in.shape, conv_w=conv_w.shape,
                  w_out=w_out.shape, w_ffn_up=w_ffn_up.shape, ffn_conv_w=ffn_conv_w.shape, w_ffn_down=w_ffn_down.shape)
    outs = [loss, dx.reshape(x.shape)]
    for which in range(4):
        for nm in order:
            a = bigs[nm][which] if nm in bigs else sm[nm][which]
            outs.append(a.reshape(shapes[nm]) if nm in shapes else a)
    return tuple(outs)


def _pad_rows(a):
    k = a.shape[0]
    kp = -(-k // 16) * 16
    return jnp.pad(a, ((0, kp - k), (0, 0)))


def _unshard_cols(a3):
    s, k, c = a3.shape
    return jnp.transpose(a3, (1, 0, 2)).reshape(k, s * c)
```

```python
import functools

import jax
import jax.numpy as jnp
from jax import lax
from jax.experimental import pallas as pl
from jax.experimental.pallas import tpu as pltpu

F32 = jnp.float32
BF16 = jnp.bfloat16

LN_EPS = 1e-5
RMS_EPS = 1e-6
LANE = 128
SUBLANE = 8
CHUNK = 64
SUB = 8
HALO = 32
FHALO = 8
ROWS = 64
N_CHIPS = 4
VMEM_LIMIT = 56 << 20
NEG_BIG = -1e30

ADAM_LR = 0.001
ADAM_B1 = 0.9
ADAM_B2 = 0.999
ADAM_EPS = 1e-08
ADAM_WD = 0.01
ADAM_STEP = 10

MESH = pl.DeviceIdType.MESH
HBM = pl.BlockSpec(memory_space=pl.ANY)
VMEM_FULL = pl.BlockSpec(memory_space=pltpu.VMEM)


def _params(*sem):
    return pltpu.CompilerParams(dimension_semantics=sem, vmem_limit_bytes=VMEM_LIMIT)


class _Rider:
    def __init__(self, ins, outs, aliases, n_sems, start, finish):
        self.ins, self.outs, self.aliases = list(ins), list(outs), dict(aliases)
        self.n_sems, self.start, self.finish = n_sems, start, finish


def _call(body, args, *, name, grid, in_specs, out_specs, out_shape, scratch_shapes=(), aliases=None, rider=None):
    many = isinstance(out_shape, (list, tuple))
    shapes = list(out_shape) if many else [out_shape]
    ospecs = list(out_specs) if many else [out_specs]
    aliases = dict(aliases or {})
    sem = ("arbitrary",) * len(grid)
    if rider is None:
        res = pl.pallas_call(
            body, name=name, grid=grid, in_specs=list(in_specs), out_specs=ospecs, out_shape=shapes,
            scratch_shapes=list(scratch_shapes), input_output_aliases=aliases, compiler_params=_params(*sem))(*args)
        return res if many else res[0]
    n_in, n_out, n_scr = len(args), len(shapes), len(scratch_shapes)
    nri, nro = len(rider.ins), len(rider.outs)

    def wrapped(*refs):
        ins, rin = refs[:n_in], refs[n_in:n_in + nri]
        o0 = n_in + nri
        outs, rout = refs[o0:o0 + n_out], refs[o0 + n_out:o0 + n_out + nro]
        s0 = o0 + n_out + nro
        scr, (send, recv) = refs[s0:s0 + n_scr], refs[s0 + n_scr:]
        ids = [pl.program_id(a) for a in range(len(grid))]
        first = functools.reduce(jnp.logical_and, [i == 0 for i in ids])
        last = functools.reduce(jnp.logical_and, [i == g - 1 for i, g in zip(ids, grid)])

        @pl.when(first)
        def _():
            rider.start(rin, rout, send, recv)

        body(*ins, *outs, *scr)

        @pl.when(last)
        def _():
            rider.finish(rin, rout, send, recv)

    for ri, ro in rider.aliases.items():
        aliases[n_in + ri] = n_out + ro
    res = pl.pallas_call(
        wrapped, name=name, grid=grid, in_specs=list(in_specs) + [HBM] * nri, out_specs=ospecs + [HBM] * nro,
        out_shape=shapes + rider.outs,
        scratch_shapes=list(scratch_shapes) + [pltpu.SemaphoreType.DMA((rider.n_sems,)),
                                               pltpu.SemaphoreType.DMA((rider.n_sems,))],
        input_output_aliases=aliases, compiler_params=_params(*sem))(*args, *rider.ins)
    main, extra = res[:n_out], list(res[n_out:])
    return (list(main) if many else main[0]), extra


def _div_tile(n, mult, cap):
    best = n
    for t in range(mult, min(n, cap) + 1, mult):
        if n % t == 0:
            best = t
    return best


def _sigmoid(x):
    return 1.0 / (1.0 + jnp.exp(-x))


def _ln_stats(x):
    mu = jnp.mean(x, axis=-1, keepdims=True)
    xc = x - mu
    var = jnp.mean(xc * xc, axis=-1, keepdims=True)
    rstd = lax.rsqrt(var + LN_EPS)
    return xc * rstd, rstd


def _ln_bwd(dy, xhat, rstd, g):
    dyg = dy * g
    m1 = jnp.mean(dyg, axis=-1, keepdims=True)
    m2 = jnp.mean(dyg * xhat, axis=-1, keepdims=True)
    return rstd * (dyg - m1 - xhat * m2)


def _dot_nt(a, b):
    return lax.dot_general(a, b, (((1,), (1,)), ((), ())), preferred_element_type=F32)


def _dot_tn(a, b):
    return lax.dot_general(a, b, (((0,), (0,)), ((), ())), preferred_element_type=F32)


def _dot(a, b):
    return jnp.dot(a, b, preferred_element_type=F32)


def _dot3(m, x):
    mb = m.astype(BF16)
    x1 = x.astype(BF16)
    r1 = x - x1.astype(F32)
    x2 = r1.astype(BF16)
    x3 = (r1 - x2.astype(F32)).astype(BF16)
    return _dot(mb, x1) + _dot(mb, x2) + _dot(mb, x3)


def _place_shard(x, name, chip_idx, dtype):
    r, c = x.shape
    tr = _div_tile(r, 16, 512)

    def body(s_ref, x_ref, o_ref):
        del s_ref
        o_ref[...] = x_ref[...].astype(dtype)

    return pl.pallas_call(
        body, name=name,
        grid_spec=pltpu.PrefetchScalarGridSpec(
            num_scalar_prefetch=1, grid=(r // tr,),
            in_specs=[pl.BlockSpec((tr, c), lambda i, s: (i, 0))],
            out_specs=pl.BlockSpec((None, tr, c), lambda i, s: (s[0], i, 0))),
        out_shape=jax.ShapeDtypeStruct((N_CHIPS, r, c), dtype),
        compiler_params=_params("parallel"),
    )(chip_idx, x)


def _ln0(x, g, b, tm):
    t, d = x.shape

    def body(x_ref, g_ref, b_ref, o_ref):
        xh, _ = _ln_stats(x_ref[...])
        o_ref[...] = (xh * g_ref[...] + b_ref[...]).astype(BF16)

    row = pl.BlockSpec((1, d), lambda i: (0, 0))
    return pl.pallas_call(
        body, name="ln0", grid=(t // tm,),
        in_specs=[pl.BlockSpec((tm, d), lambda i: (i, 0)), row, row],
        out_specs=pl.BlockSpec((tm, d), lambda i: (i, 0)),
        out_shape=jax.ShapeDtypeStruct((t, d), BF16),
        compiler_params=_params("parallel"),
    )(x, g, b)


def _proj(name, a, w3, n_sec, tm, tn, rider=None):
    m, k = a.shape
    s, _, ws = w3.shape
    sec_w = s * ws // n_sec
    nj = ws // tn
    per_sec = sec_w // tn

    def body(a_ref, w_ref, o_ref):
        o_ref[...] = _dot(a_ref[...], w_ref[...])

    return _call(
        body, (a, w3), name=name, grid=(s * nj, m // tm),
        in_specs=[pl.BlockSpec((tm, k), lambda j, i: (i, 0)),
                  pl.BlockSpec((None, k, tn), lambda j, i: (j // nj, 0, j % nj))],
        out_specs=pl.BlockSpec((None, tm, tn), lambda j, i: (j // per_sec, i, j % per_sec)),
        out_shape=jax.ShapeDtypeStruct((n_sec, m, sec_w), F32), rider=rider)


def _proj_t(name, a, w, tm, tn, rider=None):
    m, k = a.shape
    n = w.shape[0]

    def body(a_ref, w_ref, o_ref):
        o_ref[...] = _dot_nt(a_ref[...], w_ref[...])

    return _call(
        body, (a, w), name=name, grid=(n // tn, m // tm),
        in_specs=[pl.BlockSpec((tm, k), lambda j, i: (i, 0)),
                  pl.BlockSpec((tn, k), lambda j, i: (j, 0))],
        out_specs=pl.BlockSpec((tm, tn), lambda j, i: (i, j)),
        out_shape=jax.ShapeDtypeStruct((m, n), F32), rider=rider)


def _wgrad(name, a, b, out_shape, grid, a_spec, b_spec, o_spec, rider=None, dot=_dot_tn):
    nt = len(grid) - 1

    def body(a_ref, b_ref, o_ref):
        t = pl.program_id(nt)
        prod = dot(a_ref[...], b_ref[...])

        @pl.when(t == 0)
        def _():
            o_ref[...] = prod

        @pl.when(t > 0)
        def _():
            o_ref[...] += prod

    return _call(
        body, (a, b), name=name, grid=grid, in_specs=[a_spec, b_spec], out_specs=o_spec,
        out_shape=jax.ShapeDtypeStruct(out_shape, F32), rider=rider)


def _mix_ln1(cat, w_out, x, g0, b0, g1, b1, alpha, tm, rider=None):
    t, d = x.shape

    def body(cat_ref, w_ref, x_ref, g0_ref, b0_ref, g1_ref, b1_ref, xh_ref, h1b_ref, rstd_ref):
        mix = _dot(cat_ref[...], w_ref[...])
        xh0, _ = _ln_stats(x_ref[...])
        z1 = alpha * (xh0 * g0_ref[...] + b0_ref[...]) + mix
        xh1, rstd1 = _ln_stats(z1)
        xh_ref[...] = xh1
        h1b_ref[...] = (xh1 * g1_ref[...] + b1_ref[...]).astype(BF16)
        rstd_ref[...] = rstd1

    row = pl.BlockSpec((1, d), lambda i: (0, 0))
    blk = pl.BlockSpec((tm, d), lambda i: (i, 0))
    return _call(
        body, (cat, w_out, x, g0, b0, g1, b1), name="mix_ln1", grid=(t // tm,),
        in_specs=[blk, pl.BlockSpec((d, d), lambda i: (0, 0)), blk, row, row, row, row],
        out_specs=[blk, blk, pl.BlockSpec((tm, 1), lambda i: (i, 0))],
        out_shape=[jax.ShapeDtypeStruct((t, d), F32), jax.ShapeDtypeStruct((t, d), BF16),
                   jax.ShapeDtypeStruct((t, 1), F32)], rider=rider)


def _ln2_loss(ffn, xhat1, tgt, g1, b1, g2, b2, alpha, tm):
    t, d = xhat1.shape
    ni = t // tm
    inv_d = 1.0 / d

    def body(ffn_ref, xh1_ref, tgt_ref, g1_ref, b1_ref, g2_ref, b2_ref,
             dz2_ref, dz2b_ref, dg2_ref, db2_ref, loss_ref, lrow):
        i = pl.program_id(0)
        h1 = xh1_ref[...] * g1_ref[...] + b1_ref[...]
        xh2, rstd2 = _ln_stats(alpha * h1 + ffn_ref[...])
        g2v = g2_ref[...]
        diff = xh2 * g2v + b2_ref[...] - tgt_ref[...]
        dh2 = diff * inv_d
        sq = jnp.sum(diff * diff, axis=0, keepdims=True)
        dg = jnp.sum(dh2 * xh2, axis=0, keepdims=True)
        db = jnp.sum(dh2, axis=0, keepdims=True)

        @pl.when(i == 0)
        def _():
            lrow[...] = sq
            dg2_ref[...] = dg
            db2_ref[...] = db

        @pl.when(i > 0)
        def _():
            lrow[...] += sq
            dg2_ref[...] += dg
            db2_ref[...] += db

        dz2 = _ln_bwd(dh2, xh2, rstd2, g2v)
        dz2_ref[...] = dz2
        dz2b_ref[...] = dz2.astype(BF16)

        @pl.when(i == ni - 1)
        def _():
            tot = jnp.sum(lrow[...], axis=-1, keepdims=True) * (0.5 * inv_d)
            loss_ref[...] = jnp.broadcast_to(tot, (1, LANE))

    row = pl.BlockSpec((1, d), lambda i: (0, 0))
    blk = pl.BlockSpec((tm, d), lambda i: (i, 0))
    return _call(
        body, (ffn, xhat1, tgt, g1, b1, g2, b2), name="ln2_loss", grid=(ni,),
        in_specs=[blk, blk, blk, row, row, row, row],
        out_specs=[blk, blk, row, row, pl.BlockSpec((1, LANE), lambda i: (0, 0))],
        out_shape=[jax.ShapeDtypeStruct((t, d), F32), jax.ShapeDtypeStruct((t, d), BF16),
                   jax.ShapeDtypeStruct((1, d), F32), jax.ShapeDtypeStruct((1, d), F32),
                   jax.ShapeDtypeStruct((1, LANE), F32)],
        scratch_shapes=[pltpu.VMEM((1, d), F32)])


def _ln1_bwd(pre, dz2, xhat1, rstd1, g1, alpha, tm):
    t, d = dz2.shape

    def body(pre_ref, dz2_ref, xh_ref, rstd_ref, g_ref, dz1_ref, dz1b_ref, dg_ref, db_ref):
        i = pl.program_id(0)
        dh1 = alpha * dz2_ref[...] + pre_ref[...]
        xh = xh_ref[...]
        dg = jnp.sum(dh1 * xh, axis=0, keepdims=True)
        db = jnp.sum(dh1, axis=0, keepdims=True)

        @pl.when(i == 0)
        def _():
            dg_ref[...] = dg
            db_ref[...] = db

        @pl.when(i > 0)
        def _():
            dg_ref[...] += dg
            db_ref[...] += db

        dz1 = _ln_bwd(dh1, xh, rstd_ref[...], g_ref[...])
        dz1_ref[...] = dz1
        dz1b_ref[...] = dz1.astype(BF16)

    row = pl.BlockSpec((1, d), lambda i: (0, 0))
    blk = pl.BlockSpec((tm, d), lambda i: (i, 0))
    return _call(
        body, (pre, dz2, xhat1, rstd1, g1), name="ln1_bwd", grid=(t // tm,),
        in_specs=[blk, blk, blk, pl.BlockSpec((tm, 1), lambda i: (i, 0)), row],
        out_specs=[blk, blk, row, row],
        out_shape=[jax.ShapeDtypeStruct((t, d), F32), jax.ShapeDtypeStruct((t, d), BF16),
                   jax.ShapeDtypeStruct((1, d), F32), jax.ShapeDtypeStruct((1, d), F32)])


def _ln0_bwd(pre, dz1, x, g0, alpha, tm):
    t, d = x.shape

    def body(pre_ref, dz1_ref, x_ref, g_ref, dx_ref, dg_ref, db_ref):
        i = pl.program_id(0)
        dh0 = alpha * dz1_ref[...] + pre_ref[...]
        xh, rstd = _ln_stats(x_ref[...])
        dg = jnp.sum(dh0 * xh, axis=0, keepdims=True)
        db = jnp.sum(dh0, axis=0, keepdims=True)

        @pl.when(i == 0)
        def _():
            dg_ref[...] = dg
            db_ref[...] = db

        @pl.when(i > 0)
        def _():
            dg_ref[...] += dg
            db_ref[...] += db

        dx_ref[...] = _ln_bwd(dh0, xh, rstd, g_ref[...])

    row = pl.BlockSpec((1, d), lambda i: (0, 0))
    blk = pl.BlockSpec((tm, d), lambda i: (i, 0))
    return _call(
        body, (pre, dz1, x, g0), name="ln0_bwd", grid=(t // tm,),
        in_specs=[blk, blk, blk, row], out_specs=[blk, row, row],
        out_shape=[jax.ShapeDtypeStruct((t, d), F32), jax.ShapeDtypeStruct((1, d), F32),
                   jax.ShapeDtypeStruct((1, d), F32)])


def _conv_fwd(p3, conv_w, conv_b, cn_g, cn_b, tc, cb, rider=None):
    _, t, w = p3.shape
    kk = conv_w.shape[0]
    off = HALO - (kk - 1)
    hb = tc // HALO

    def body(a_ref, g_ref, ap_ref, gp_ref, w_ref, b_ref, ng_ref, nb_ref, cat_ref, u1_ref, ext):
        i = pl.program_id(1)
        ext[pl.ds(HALO, tc), :] = a_ref[...] * _sigmoid(g_ref[...])
        prev = ap_ref[...] * _sigmoid(gp_ref[...])
        ext[pl.ds(0, HALO), :] = jnp.where(i > 0, prev, 0.0)
        for r in range(tc // ROWS):
            acc = jnp.broadcast_to(b_ref[...], (ROWS, cb))
            for k in range(kk):
                acc = acc + w_ref[k:k + 1, :] * ext[pl.ds(r * ROWS + off + k, ROWS), :]
            u1_ref[pl.ds(r * ROWS, ROWS), :] = acc
            for g in range(cb // LANE):
                sl = slice(g * LANE, (g + 1) * LANE)
                xh, _ = _ln_stats(acc[:, sl])
                u2 = xh * ng_ref[:, sl] + nb_ref[:, sl]
                cat_ref[pl.ds(r * ROWS, ROWS), sl] = (u2 * _sigmoid(u2)).astype(BF16)

    cur = lambda sec: pl.BlockSpec((None, tc, cb), lambda j, i: (sec, i, j))
    prev = lambda sec: pl.BlockSpec((None, HALO, cb), lambda j, i: (sec, jnp.maximum(i * hb - 1, 0), j))
    row = pl.BlockSpec((1, cb), lambda j, i: (0, j))
    return _call(
        body, (p3, p3, p3, p3, conv_w, conv_b, cn_g, cn_b), name="conv_fwd", grid=(w // cb, t // tc),
        in_specs=[cur(0), cur(1), prev(0), prev(1), pl.BlockSpec((kk, cb), lambda j, i: (0, j)), row, row, row],
        out_specs=[pl.BlockSpec((tc, cb), lambda j, i: (i, j)), pl.BlockSpec((tc, cb), lambda j, i: (i, j))],
        out_shape=[jax.ShapeDtypeStruct((t, 2 * w), BF16), jax.ShapeDtypeStruct((t, w), F32)],
        scratch_shapes=[pltpu.VMEM((tc + HALO, cb), F32)], rider=rider)


def _conv_norm_bwd(dcat, u1, cn_g, cn_b, tc):
    t, w = u1.shape

    def body(du_ref, u1_ref, ng_ref, nb_ref, du1_ref, dg_ref, db_ref):
        i = pl.program_id(0)
        for g in range(w // LANE):
            sl = slice(g * LANE, (g + 1) * LANE)
            ng = ng_ref[:, sl]
            xh, rstd = _ln_stats(u1_ref[:, sl])
            u2 = xh * ng + nb_ref[:, sl]
            sg = _sigmoid(u2)
            du2 = du_ref[:, sl] * (sg * (1.0 + u2 * (1.0 - sg)))
            dg = jnp.sum(du2 * xh, axis=0, keepdims=True)
            db = jnp.sum(du2, axis=0, keepdims=True)

            @pl.when(i == 0)
            def _():
                dg_ref[:, sl] = dg
                db_ref[:, sl] = db

            @pl.when(i > 0)
            def _():
                dg_ref[:, sl] += dg
                db_ref[:, sl] += db

            du1_ref[:, sl] = _ln_bwd(du2, xh, rstd, ng)

    row = pl.BlockSpec((1, w), lambda i: (0, 0))
    blk = pl.BlockSpec((tc, w), lambda i: (i, 0))
    return pl.pallas_call(
        body, name="conv_norm_bwd", grid=(t // tc,),
        in_specs=[blk, blk, row, row], out_specs=[blk, row, row],
        out_shape=[jax.ShapeDtypeStruct((t, w), F32), jax.ShapeDtypeStruct((1, w), F32),
                   jax.ShapeDtypeStruct((1, w), F32)],
        compiler_params=_params("arbitrary"),
    )(dcat, u1, cn_g, cn_b)


def _conv_bwd(du1, p3, conv_w, tc, cb, rider=None):
    n_sec, t, w = p3.shape
    kk = conv_w.shape[0]
    off = HALO - (kk - 1)
    hb = tc // HALO
    nt = t // tc
    kpad = -(-kk // SUBLANE) * SUBLANE

    def body(d_ref, dn_ref, a_ref, g_ref, ap_ref, gp_ref, w_ref, dp_ref, dw_ref, db_ref, extd, extu, wacc, bacc):
        i = pl.program_id(1)

        @pl.when(i == 0)
        def _():
            wacc[...] = jnp.zeros_like(wacc)
            bacc[...] = jnp.zeros_like(bacc)

        extd[pl.ds(0, tc), :] = d_ref[...]
        extd[pl.ds(tc, HALO), :] = jnp.where(i < nt - 1, dn_ref[...], 0.0)
        extu[pl.ds(HALO, tc), :] = a_ref[...] * _sigmoid(g_ref[...])
        extu[pl.ds(0, HALO), :] = jnp.where(i > 0, ap_ref[...] * _sigmoid(gp_ref[...]), 0.0)
        for r in range(tc // ROWS):
            rows = pl.ds(r * ROWS, ROWS)
            acc = jnp.zeros((ROWS, cb), F32)
            for k in range(kk):
                acc = acc + w_ref[k:k + 1, :] * extd[pl.ds(r * ROWS + (kk - 1) - k, ROWS), :]
            a = a_ref[rows, :]
            sg = _sigmoid(g_ref[rows, :])
            dp_ref[0, rows, :] = (acc * sg).astype(BF16)
            dp_ref[1, rows, :] = (acc * a * sg * (1.0 - sg)).astype(BF16)
            d = d_ref[rows, :]
            bacc[...] += jnp.sum(d.reshape(ROWS // SUBLANE, SUBLANE, cb), axis=0)
            for k in range(kk):
                prod = d * extu[pl.ds(r * ROWS + off + k, ROWS), :]
                wacc[k] += jnp.sum(prod.reshape(ROWS // SUBLANE, SUBLANE, cb), axis=0)

        @pl.when(i == nt - 1)
        def _():
            for k in range(kk):
                dw_ref[k:k + 1, :] = jnp.sum(wacc[k], axis=0, keepdims=True)
            if kpad > kk:
                dw_ref[kk:kpad, :] = jnp.zeros((kpad - kk, cb), F32)
            db_ref[...] = jnp.sum(bacc[...], axis=0, keepdims=True)

    cur = lambda sec: pl.BlockSpec((None, tc, cb), lambda j, i: (sec, i, j))
    prev = lambda sec: pl.BlockSpec((None, HALO, cb), lambda j, i: (sec, jnp.maximum(i * hb - 1, 0), j))
    return _call(
        body, (du1, du1, p3, p3, p3, p3, conv_w), name="conv_bwd", grid=(w // cb, nt),
        in_specs=[pl.BlockSpec((tc, cb), lambda j, i: (i, j)),
                  pl.BlockSpec((HALO, cb), lambda j, i: (jnp.minimum((i + 1) * hb, t // HALO - 1), j)),
                  cur(0), cur(1), prev(0), prev(1), pl.BlockSpec((kk, cb), lambda j, i: (0, j))],
        out_specs=[pl.BlockSpec((2, tc, cb), lambda j, i: (0, i, j)),
                   pl.BlockSpec((kpad, cb), lambda j, i: (0, j)),
                   pl.BlockSpec((1, cb), lambda j, i: (0, j))],
        out_shape=[jax.ShapeDtypeStruct((n_sec, t, w), BF16), jax.ShapeDtypeStruct((kpad, w), F32),
                   jax.ShapeDtypeStruct((1, w), F32)],
        scratch_shapes=[pltpu.VMEM((tc + HALO, cb), F32), pltpu.VMEM((tc + HALO, cb), F32),
                        pltpu.VMEM((kk, SUBLANE, cb), F32), pltpu.VMEM((SUBLANE, cb), F32)], rider=rider)


def _ffn_act_fwd(hh3, fw, fb, tc, cb, rider=None):
    _, t, dff = hh3.shape
    kk = fw.shape[0]
    off = FHALO - (kk - 1)
    hb = tc // FHALO

    def body(g_ref, v_ref, gp_ref, w_ref, b_ref, act_ref, ext):
        i = pl.program_id(1)
        ext[pl.ds(FHALO, tc), :] = g_ref[...]
        ext[pl.ds(0, FHALO), :] = jnp.where(i > 0, gp_ref[...], 0.0)
        gc = jnp.broadcast_to(b_ref[...], (tc, cb))
        for k in range(kk):
            gc = gc + w_ref[k:k + 1, :] * ext[pl.ds(off + k, tc), :]
        act_ref[...] = (gc * _sigmoid(gc) * v_ref[...]).astype(BF16)

    return _call(
        body, (hh3, hh3, hh3, fw, fb), name="ffn_act_fwd", grid=(dff // cb, t // tc),
        in_specs=[pl.BlockSpec((None, tc, cb), lambda j, i: (0, i, j)),
                  pl.BlockSpec((None, tc, cb), lambda j, i: (1, i, j)),
                  pl.BlockSpec((None, FHALO, cb), lambda j, i: (0, jnp.maximum(i * hb - 1, 0), j)),
                  pl.BlockSpec((kk, cb), lambda j, i: (0, j)),
                  pl.BlockSpec((1, cb), lambda j, i: (0, j))],
        out_specs=pl.BlockSpec((tc, cb), lambda j, i: (i, j)),
        out_shape=jax.ShapeDtypeStruct((t, dff), BF16),
        scratch_shapes=[pltpu.VMEM((tc + FHALO, cb), F32)], rider=rider)


def _ffn_act_bwd(dact, hh3, fw, fb, tc, cb):
    _, t, dff = hh3.shape
    kk = fw.shape[0]
    off = FHALO - (kk - 1)
    hb = tc // FHALO
    nt = t // tc
    te = tc + FHALO

    def body(da_ref, dan_ref, g_ref, gp_ref, gn_ref, v_ref, vn_ref, w_ref, b_ref,
             dhh_ref, dw_ref, db_ref, gext, dext, wacc, bacc):
        i = pl.program_id(1)

        @pl.when(i == 0)
        def _():
            wacc[...] = jnp.zeros_like(wacc)
            bacc[...] = jnp.zeros_like(bacc)

        gext[pl.ds(0, FHALO), :] = jnp.where(i > 0, gp_ref[...], 0.0)
        gext[pl.ds(FHALO, tc), :] = g_ref[...]
        gext[pl.ds(FHALO + tc, FHALO), :] = gn_ref[...]
        gc = jnp.broadcast_to(b_ref[...], (te, cb))
        for k in range(kk):
            gc = gc + w_ref[k:k + 1, :] * gext[pl.ds(off + k, te), :]
        sg = _sigmoid(gc)
        dsilu = sg * (1.0 + gc * (1.0 - sg))
        live = i < nt - 1
        da_cur = da_ref[...]
        dext[pl.ds(0, tc), :] = da_cur * v_ref[...] * dsilu[0:tc]
        dext[pl.ds(tc, FHALO), :] = jnp.where(live, dan_ref[...] * vn_ref[...] * dsilu[tc:te], 0.0)
        dhh_ref[1] = (da_cur * (gc[0:tc] * sg[0:tc])).astype(BF16)
        dg = jnp.zeros((tc, cb), F32)
        for k in range(kk):
            dg = dg + w_ref[k:k + 1, :] * dext[pl.ds((kk - 1) - k, tc), :]
        dhh_ref[0] = dg.astype(BF16)
        dgc = dext[pl.ds(0, tc), :]
        bacc[...] += jnp.sum(dgc.reshape(tc // SUBLANE, SUBLANE, cb), axis=0)
        for k in range(kk):
            prod = dgc * gext[pl.ds(off + k, tc), :]
            wacc[k] += jnp.sum(prod.reshape(tc // SUBLANE, SUBLANE, cb), axis=0)

        @pl.when(i == nt - 1)
        def _():
            for k in range(kk):
                dw_ref[k:k + 1, :] = jnp.sum(wacc[k], axis=0, keepdims=True)
            dw_ref[kk:SUBLANE, :] = jnp.zeros((SUBLANE - kk, cb), F32)
            db_ref[...] = jnp.sum(bacc[...], axis=0, keepdims=True)

    nxt = lambda i: jnp.minimum((i + 1) * hb, t // FHALO - 1)
    return pl.pallas_call(
        body, name="ffn_act_bwd", grid=(dff // cb, nt),
        in_specs=[pl.BlockSpec((tc, cb), lambda j, i: (i, j)),
                  pl.BlockSpec((FHALO, cb), lambda j, i: (nxt(i), j)),
                  pl.BlockSpec((None, tc, cb), lambda j, i: (0, i, j)),
                  pl.BlockSpec((None, FHALO, cb), lambda j, i: (0, jnp.maximum(i * hb - 1, 0), j)),
                  pl.BlockSpec((None, FHALO, cb), lambda j, i: (0, nxt(i), j)),
                  pl.BlockSpec((None, tc, cb), lambda j, i: (1, i, j)),
                  pl.BlockSpec((None, FHALO, cb), lambda j, i: (1, nxt(i), j)),
                  pl.BlockSpec((kk, cb), lambda j, i: (0, j)),
                  pl.BlockSpec((1, cb), lambda j, i: (0, j))],
        out_specs=[pl.BlockSpec((2, tc, cb), lambda j, i: (0, i, j)),
                   pl.BlockSpec((SUBLANE, cb), lambda j, i: (0, j)),
                   pl.BlockSpec((1, cb), lambda j, i: (0, j))],
        out_shape=[jax.ShapeDtypeStruct((2, t, dff), BF16), jax.ShapeDtypeStruct((SUBLANE, dff), F32),
                   jax.ShapeDtypeStruct((1, dff), F32)],
        scratch_shapes=[pltpu.VMEM((tc + 2 * FHALO, cb), F32), pltpu.VMEM((te, cb), F32),
                        pltpu.VMEM((kk, SUBLANE, cb), F32), pltpu.VMEM((SUBLANE, cb), F32)],
        compiler_params=_params("parallel", "arbitrary"),
    )(dact, dact, hh3, hh3, hh3, hh3, hh3, fw, fb)


def _chunk_consts():
    r = lax.broadcasted_iota(jnp.int32, (CHUNK, CHUNK), 0)
    c = lax.broadcasted_iota(jnp.int32, (CHUNK, CHUNK), 1)
    blk = (r // SUB) * SUB
    tri = (c <= r).astype(F32)
    start = (c < blk).astype(F32)
    end = (c < blk + SUB).astype(F32)
    return jnp.concatenate([tri, start, end, jnp.ones((SUBLANE, CHUNK), F32)], axis=0)


def _gate_terms(q, fpre, lb):
    sf = _sigmoid(fpre)
    fg = lb + (1.0 - lb) * sf
    sq = _sigmoid(q)
    return sf, fg, 1.0 - fg, sq, q * sq


def _decays(g, consts):
    cs = _dot3(consts, g)
    b = cs[0:CHUNK]
    rs = cs[CHUNK:2 * CHUNK]
    re = cs[2 * CHUNK:3 * CHUNK]
    tot = cs[3 * CHUNK:3 * CHUNK + 1]
    return b, rs, re, tot


def _lower_bound(lb_ref):
    l0, l1 = lb_ref[0:1, :], lb_ref[1:2, :]
    mx = jnp.maximum(l0, l1)
    e0, e1 = jnp.exp(l0 - mx), jnp.exp(l1 - mx)
    return e0 / (e0 + e1)


def _scaled_keys(kt, rs, re, rowblk, i):
    scale = jnp.where(rowblk < i, jnp.exp(jnp.minimum(rs[SUB * i:SUB * i + 1, :] - re, 0.0)), 0.0)
    return kt * scale, scale


def _hgrn_fwd(p3, lb_logits, hg, cat, tb, hpb, rider=None):
    _, t, w = p3.shape
    nh = w // LANE
    nc = tb // CHUNK
    assert nh % hpb == 0

    def body(q_ref, f_ref, v_ref, og_ref, lb_ref, hg_ref, cat_in, cat_ref, o_ref, st_ref, state):
        del cat_in
        consts = _chunk_consts()
        lb_all = _lower_bound(lb_ref)
        rowblk = lax.broadcasted_iota(jnp.int32, (CHUNK, 1), 0) // SUB
        rowpos = lax.broadcasted_iota(jnp.int32, (CHUNK, 1), 0) % SUB

        @pl.when(pl.program_id(1) == 0)
        def _():
            state[...] = jnp.zeros_like(state)

        def chunk(c, carry):
            rows = pl.ds(pl.multiple_of(c * CHUNK, CHUNK), CHUNK)
            heads = range(hpb)
            sls = [slice(j * LANE, (j + 1) * LANE) for j in heads]
            v = [v_ref[rows, s] for s in sls]
            vb = [x.astype(BF16) for x in v]
            gates = [_gate_terms(q_ref[rows, s], f_ref[rows, s], lb_all[:, s]) for s in sls]
            fg = [g[1] for g in gates]
            kk = [g[2] for g in gates]
            qh = [g[4] for g in gates]
            dec = [_decays(jnp.log(x), consts) for x in fg]
            b = [x[0] for x in dec]
            rs = [x[1] for x in dec]
            re = [x[2] for x in dec]
            tot = [x[3] for x in dec]
            qt = [qh[j] * jnp.exp(b[j] - rs[j]) for j in heads]
            kt = [kk[j] * jnp.exp(re[j] - b[j]) for j in heads]
            st = [state[j] for j in heads]
            for j in heads:
                st_ref[j, c] = st[j]
            a = [jnp.zeros((CHUNK, CHUNK), F32) for _ in heads]
            for i in range(1, CHUNK // SUB):
                for j in heads:
                    ki, _ = _scaled_keys(kt[j], rs[j], re[j], rowblk, i)
                    a[j] = a[j] + _dot_nt(jnp.where(rowblk == i, qt[j], 0.0).astype(BF16), ki.astype(BF16))
            o = [_dot(a[j].astype(BF16), vb[j]) for j in heads]
            o = [o[j] + _dot_nt((qh[j] * jnp.exp(b[j])).astype(BF16), st[j].astype(BF16)) for j in heads]
            for j in heads:
                k_up = kk[j] * jnp.exp(tot[j] - b[j])
                state[j] = st[j] * jnp.exp(tot[j]) + _dot_tn(vb[j], k_up.astype(BF16))
            for j in heads:
                e = None
                for d in range(SUB):
                    if d == 0:
                        ks, vs, term = kk[j], v[j], qh[j] * kk[j]
                    else:
                        rf = pltpu.roll(fg[j], d - 1, 0) if d > 1 else fg[j]
                        e = rf if e is None else e * rf
                        ks, vs = pltpu.roll(kk[j], d, 0), pltpu.roll(v[j], d, 0)
                        term = jnp.where(rowpos >= d, qh[j] * ks * e, 0.0)
                    o[j] = o[j] + jnp.sum(term, axis=-1, keepdims=True) * vs
            for j in heads:
                og = og_ref[rows, sls[j]]
                o_ref[rows, sls[j]] = o[j]
                r = lax.rsqrt(jnp.mean(o[j] * o[j], axis=-1, keepdims=True) + RMS_EPS)
                cat_ref[rows, sls[j]] = (o[j] * r * hg_ref[:, sls[j]] * (og * _sigmoid(og))).astype(BF16)
            return carry

        lax.fori_loop(0, nc, chunk, 0)

    bw = hpb * LANE
    sec = lambda s: pl.BlockSpec((None, tb, bw), lambda h, i: (s, i, h))
    return _call(
        body, (p3, p3, p3, p3, lb_logits, hg, cat), name="hgrn_fwd", grid=(nh // hpb, t // tb),
        in_specs=[sec(2), sec(3), sec(4), sec(5),
                  pl.BlockSpec((2, bw), lambda h, i: (0, h)),
                  pl.BlockSpec((1, bw), lambda h, i: (0, h)), HBM],
        out_specs=[pl.BlockSpec((tb, bw), lambda h, i: (i, nh // hpb + h)),
                   pl.BlockSpec((tb, bw), lambda h, i: (i, h)),
                   pl.BlockSpec((hpb, nc, LANE, LANE), lambda h, i: (h, i, 0, 0))],
        out_shape=[jax.ShapeDtypeStruct(cat.shape, BF16), jax.ShapeDtypeStruct((t, w), F32),
                   jax.ShapeDtypeStruct((nh, t // CHUNK, LANE, LANE), F32)],
        scratch_shapes=[pltpu.VMEM((hpb, LANE, LANE), F32)], aliases={6: 0}, rider=rider)


def _hgrn_bwd(p3, lb_logits, hg, o_pre, states, dcat, dp3, tb, hpb, rider=None):
    n_sec, t, w = p3.shape
    nh = w // LANE
    assert nh % hpb == 0
    nc = tb // CHUNK
    nb = t // tb

    def body(q_ref, f_ref, v_ref, og_ref, lb_ref, hg_ref, o_ref, st_ref, dc_ref, dp_in,
             dp_ref, dlb_ref, dhg_ref, dstate, stash, lbacc, hgacc):
        del dp_in
        i, half = pl.program_id(1), pl.program_id(2)

        @pl.when(half == 1)
        def _():
            dp_ref[...] = stash[...]

        @pl.when(half == 0)
        def _():
            consts = _chunk_consts()
            rr = lax.broadcasted_iota(jnp.int32, (CHUNK, CHUNK), 0)
            cc = lax.broadcasted_iota(jnp.int32, (CHUNK, CHUNK), 1)
            upper = (cc >= rr).astype(F32)
            lb_all = _lower_bound(lb_ref)
            rowblk = lax.broadcasted_iota(jnp.int32, (CHUNK, 1), 0) // SUB
            rowpos = lax.broadcasted_iota(jnp.int32, (CHUNK, 1), 0) % SUB

            @pl.when(i == 0)
            def _():
                dstate[...] = jnp.zeros_like(dstate)
                lbacc[...] = jnp.zeros_like(lbacc)
                hgacc[...] = jnp.zeros_like(hgacc)

            def head(j, c, rows):
                sl = slice(j * LANE, (j + 1) * LANE)
                lb = lb_all[:, sl]
                hgv = hg_ref[:, sl]
                q = q_ref[rows, sl]
                v = v_ref[rows, sl]
                og = og_ref[rows, sl]
                o = o_ref[rows, sl]
                dcg = dc_ref[rows, sl]
                sf, fg, kk, sq, qh = _gate_terms(q, f_ref[rows, sl], lb)
                b, rs, re, tot = _decays(jnp.log(fg), consts)
                eq = jnp.exp(b - rs)
                ek = jnp.exp(re - b)
                qt = qh * eq
                kt = kk * ek
                e_in = jnp.exp(b)
                e_up = jnp.exp(tot - b)
                e_tot = jnp.exp(tot)
                q_in = (qh * e_in).astype(BF16)
                k_up = (kk * e_up).astype(BF16)
                vb = v.astype(BF16)
                st = st_ref[j, c]
                dst = dstate[j]
                dstb = dst.astype(BF16)
                yield

                sg = _sigmoid(og)
                r = lax.rsqrt(jnp.mean(o * o, axis=-1, keepdims=True) + RMS_EPS)
                ohat = o * r
                d_og = dcg * ohat * hgv * (sg * (1.0 + og * (1.0 - sg)))
                d_on = dcg * (og * sg)
                hgacc[:, sl] += jnp.sum((d_on * ohat).reshape(CHUNK // SUBLANE, SUBLANE, LANE), axis=0)
                d_oh = d_on * hgv
                do = r * (d_oh - ohat * jnp.mean(d_oh * ohat, axis=-1, keepdims=True))
                dob = do.astype(BF16)

                da = _dot_nt(dob, vb)
                yield
                a_off = jnp.zeros((CHUNK, CHUNK), F32)
                dqt = jnp.zeros((CHUNK, LANE), F32)
                dkt = jnp.zeros((CHUNK, LANE), F32)
                for blk in range(1, CHUNK // SUB):
                    ki, scale = _scaled_keys(kt, rs, re, rowblk, blk)
                    kib = ki.astype(BF16)
                    qib = jnp.where(rowblk == blk, qt, 0.0).astype(BF16)
                    dab = jnp.where(rowblk == blk, da, 0.0).astype(BF16)
                    a_off = a_off + _dot_nt(qib, kib)
                    dqt = dqt + _dot(dab, kib)
                    dkt = dkt + _dot_tn(dab, qib) * scale
                    yield
                dqh = dqt * eq
                dk = dkt * ek
                dv = _dot_tn(a_off.astype(BF16), dob)

                dqh = dqh + _dot(dob, st.astype(BF16)) * e_in
                dk = dk + _dot(vb, dstb) * e_up
                dv = dv + _dot_nt(k_up, dstb)
                st_end = st * e_tot + _dot_tn(vb, k_up)
                carry_g = jnp.sum(st_end * dst, axis=0, keepdims=True)
                dstate[j] = dst * e_tot + _dot_tn(dob, q_in)
                yield

                e = None
                for d in range(SUB):
                    if d == 0:
                        a_d = jnp.sum(qh * kk, axis=-1, keepdims=True)
                        da_d = jnp.sum(do * v, axis=-1, keepdims=True)
                        dqh = dqh + da_d * kk
                        dk = dk + da_d * qh
                        dv = dv + a_d * do
                        continue
                    rf = pltpu.roll(fg, d - 1, 0) if d > 1 else fg
                    e = rf if e is None else e * rf
                    em = jnp.where(rowpos >= d, e, 0.0)
                    ks, vs = pltpu.roll(kk, d, 0), pltpu.roll(v, d, 0)
                    a_d = jnp.sum(qh * ks * em, axis=-1, keepdims=True)
                    da_d = jnp.sum(do * vs, axis=-1, keepdims=True) * em
                    dqh = dqh + da_d * ks
                    dk = dk + pltpu.roll(da_d * qh, CHUNK - d, 0)
                    dv = dv + pltpu.roll(a_d * do, CHUNK - d, 0)
                yield

                dg = _dot3(upper, qh * dqh - kk * dk) + carry_g
                dfg = dg / fg - dk
                lbacc[:, sl] += jnp.sum((dfg * (1.0 - sf)).reshape(CHUNK // SUBLANE, SUBLANE, LANE), axis=0)
                dp_ref[0, rows, sl] = (dqh * (sq * (1.0 + q * (1.0 - sq)))).astype(BF16)
                dp_ref[1, rows, sl] = (dfg * (1.0 - lb) * sf * (1.0 - sf)).astype(BF16)
                stash[0, rows, sl] = dv.astype(BF16)
                stash[1, rows, sl] = d_og.astype(BF16)

            def chunk(cr, carry):
                c = nc - 1 - cr
                rows = pl.ds(pl.multiple_of(c * CHUNK, CHUNK), CHUNK)
                running = [head(j, c, rows) for j in range(hpb)]
                while running:
                    running = [g for g in running if next(g, StopIteration) is not StopIteration]
                return carry

            lax.fori_loop(0, nc, chunk, 0)

            @pl.when(i == nb - 1)
            def _():
                dlb_ref[...] = jnp.sum(lbacc[...], axis=0, keepdims=True)
                dhg_ref[...] = jnp.sum(hgacc[...], axis=0, keepdims=True)

    rev = lambda i: nb - 1 - i
    bw = hpb * LANE
    sec = lambda s: pl.BlockSpec((None, tb, bw), lambda h, i, z: (s, rev(i), h))
    return _call(
        body, (p3, p3, p3, p3, lb_logits, hg, o_pre, states, dcat, dp3), name="hgrn_bwd", grid=(nh // hpb, nb, 2),
        in_specs=[sec(2), sec(3), sec(4), sec(5),
                  pl.BlockSpec((2, bw), lambda h, i, z: (0, h)),
                  pl.BlockSpec((1, bw), lambda h, i, z: (0, h)),
                  pl.BlockSpec((tb, bw), lambda h, i, z: (rev(i), h)),
                  pl.BlockSpec((hpb, nc, LANE, LANE), lambda h, i, z: (h, rev(i), 0, 0)),
                  pl.BlockSpec((tb, bw), lambda h, i, z: (rev(i), nh // hpb + h)), HBM],
        out_specs=[pl.BlockSpec((2, tb, bw), lambda h, i, z: (1 + z, rev(i), h)),
                   pl.BlockSpec((1, bw), lambda h, i, z: (0, h)),
                   pl.BlockSpec((1, bw), lambda h, i, z: (0, h))],
        out_shape=[jax.ShapeDtypeStruct((n_sec, t, w), BF16), jax.ShapeDtypeStruct((1, w), F32),
                   jax.ShapeDtypeStruct((1, w), F32)],
        scratch_shapes=[pltpu.VMEM((hpb, LANE, LANE), F32), pltpu.VMEM((2, tb, bw), BF16),
                        pltpu.VMEM((SUBLANE, bw), F32), pltpu.VMEM((SUBLANE, bw), F32)],
        aliases={9: 0}, rider=rider)


def _place():
    x, y, c = lax.axis_index("x"), lax.axis_index("y"), lax.axis_index("c")
    chips = [(1 - x, y), (x, 1 - y), (1 - x, 1 - y)]
    return x, y, c, chips


def _rows(buf, px, py, pc):
    half = buf.shape[1] // 2
    return buf.at[2 * px + py, pl.ds(pc * half, half)]


def _rcopy(src, dst, send, recv, idx, to):
    return pltpu.make_async_remote_copy(src_ref=src, dst_ref=dst, send_sem=send.at[idx], recv_sem=recv.at[idx],
                                        device_id=to, device_id_type=MESH)


def _same(bufs):
    return [jax.ShapeDtypeStruct(b.shape, b.dtype) for b in bufs]


def _ride_gather_ici(bufs):
    n = len(bufs)

    def start(rin, rout, send, recv):
        x, y, c, chips = _place()
        for k in range(n):
            mine = _rows(rout[k], x, y, c)
            for j, chip in enumerate(chips):
                _rcopy(mine, mine, send, recv, 3 * k + j, (*chip, c)).start()

    def finish(rin, rout, send, recv):
        x, y, c, chips = _place()
        for k in range(n):
            for j, chip in enumerate(chips):
                theirs = _rows(rout[k], *chip, c)
                _rcopy(theirs, theirs, send, recv, 3 * k + j, (x, y, c)).wait_recv()
        for k in range(n):
            mine = _rows(rout[k], x, y, c)
            for j in range(3):
                _rcopy(mine, mine, send, recv, 3 * k + j, (x, y, c)).wait_send()

    return _Rider(bufs, _same(bufs), {k: k for k in range(n)}, 3 * n, start, finish)


def _ride_gather_d2d(bufs):
    n = len(bufs)

    def start(rin, rout, send, recv):
        x, y, c, chips = _place()
        for k in range(n):
            for j, chip in enumerate(chips):
                got = _rows(rout[k], *chip, c)
                _rcopy(got, got, send, recv, 3 * k + j, (x, y, 1 - c)).start()

    def finish(rin, rout, send, recv):
        x, y, c, chips = _place()
        for k in range(n):
            for j, chip in enumerate(chips):
                theirs = _rows(rout[k], *chip, 1 - c)
                _rcopy(theirs, theirs, send, recv, 3 * k + j, (x, y, c)).wait_recv()
        for k in range(n):
            for j, chip in enumerate(chips):
                got = _rows(rout[k], *chip, c)
                _rcopy(got, got, send, recv, 3 * k + j, (x, y, c)).wait_send()

    return _Rider(bufs, _same(bufs), {k: k for k in range(n)}, 3 * n, start, finish)


def _ride_swap(grads):
    n = len(grads)

    def copy(k, rin, rout, send, recv):
        x, y, c, _ = _place()
        half = rin[k].shape[1] // 2
        return _rcopy(rin[k].at[:, pl.ds((1 - c) * half, half)], rout[k], send, recv, k, (x, y, 1 - c))

    def start(rin, rout, send, recv):
        for k in range(n):
            copy(k, rin, rout, send, recv).start()

    def finish(rin, rout, send, recv):
        for k in range(n):
            copy(k, rin, rout, send, recv).wait()

    outs = [jax.ShapeDtypeStruct((g.shape[0], g.shape[1] // 2, g.shape[2]), g.dtype) for g in grads]
    return _Rider(grads, outs, {}, n, start, finish)


def _ride_send_partials(parts):
    n = len(parts)

    def copies(rin, rout, send, recv):
        x, y, c, chips = _place()
        return [_rcopy(rin[k].at[2 * px + py], rout[k].at[j], send, recv, 3 * k + j, (px, py, c))
                for k in range(n) for j, (px, py) in enumerate(chips)]

    def start(rin, rout, send, recv):
        for cp in copies(rin, rout, send, recv):
            cp.start()

    def finish(rin, rout, send, recv):
        for cp in copies(rin, rout, send, recv):
            cp.wait()

    outs = [jax.ShapeDtypeStruct((3,) + p.shape[1:], p.dtype) for p in parts]
    return _Rider(parts, outs, {}, 3 * n, start, finish)


def _ride_join(bufs):
    n = len(bufs)

    def half_of(buf, pc):
        half = buf.shape[0] // 2
        return buf.at[pl.ds(pc * half, half)]

    def start(rin, rout, send, recv):
        x, y, c, _ = _place()
        for k in range(n):
            mine = half_of(rout[k], c)
            _rcopy(mine, mine, send, recv, k, (x, y, 1 - c)).start()

    def finish(rin, rout, send, recv):
        x, y, c, _ = _place()
        for k in range(n):
            mine, theirs = half_of(rout[k], c), half_of(rout[k], 1 - c)
            _rcopy(mine, mine, send, recv, k, (x, y, c)).wait_send()
            _rcopy(theirs, theirs, send, recv, k, (x, y, c)).wait_recv()

    return _Rider(bufs, _same(bufs), {k: k for k in range(n)}, n, start, finish)


def _run(name, rider):
    def body(*refs):
        nri, nro = len(rider.ins), len(rider.outs)
        rin, rout = refs[:nri], refs[nri:nri + nro]
        send, recv = refs[nri + nro:]
        rider.start(rin, rout, send, recv)
        rider.finish(rin, rout, send, recv)

    return pl.pallas_call(
        body, name=name, in_specs=[HBM] * len(rider.ins), out_specs=[HBM] * len(rider.outs), out_shape=rider.outs,
        scratch_shapes=[pltpu.SemaphoreType.DMA((rider.n_sems,)), pltpu.SemaphoreType.DMA((rider.n_sems,))],
        input_output_aliases=rider.aliases,
    )(*rider.ins)


def _add_halves(name, g, other, c_idx):
    s, r, cols = g.shape
    half = r // 2
    tr = _div_tile(half, 16, 512)
    nb = half // tr

    def body(c_ref, g_ref, o_ref, q_ref):
        del c_ref
        q_ref[...] = (g_ref[...] + o_ref[...]).astype(BF16)

    return pl.pallas_call(
        body, name=name,
        grid_spec=pltpu.PrefetchScalarGridSpec(
            num_scalar_prefetch=1, grid=(s, nb),
            in_specs=[pl.BlockSpec((None, tr, cols), lambda k, i, c: (k, c[0] * nb + i, 0)),
                      pl.BlockSpec((None, tr, cols), lambda k, i, c: (k, i, 0))],
            out_specs=pl.BlockSpec((None, tr, cols), lambda k, i, c: (k, i, 0))),
        out_shape=jax.ShapeDtypeStruct((s, half, cols), BF16),
        compiler_params=_params("parallel", "parallel"),
    )(c_idx, g, other)


def _sum_partials(name, part, arrived, place_idx):
    _, half, cols = part.shape
    tr = _div_tile(half, 16, 512)
    nb = half // tr

    def body(s_ref, p_ref, a_ref, o_ref):
        del s_ref
        o_ref[...] = ((p_ref[...].astype(F32) + a_ref[0].astype(F32)) + a_ref[1].astype(F32)) + a_ref[2].astype(F32)

    return pl.pallas_call(
        body, name=name,
        grid_spec=pltpu.PrefetchScalarGridSpec(
            num_scalar_prefetch=1, grid=(nb,),
            in_specs=[pl.BlockSpec((None, tr, cols), lambda i, s: (s[0], i, 0)),
                      pl.BlockSpec((3, tr, cols), lambda i, s: (0, i, 0))],
            out_specs=pl.BlockSpec((tr, cols), lambda i, s: (s[1] * nb + i, 0))),
        out_shape=jax.ShapeDtypeStruct((2 * half, cols), F32),
        compiler_params=_params("parallel"),
    )(place_idx, part, arrived)


def _small_allreduce(wide_rows, ffn_rows, w, dff, n_wide, n_ffn):
    n_in = len(wide_rows) + len(ffn_rows)

    def body(*refs):
        ins = refs[:n_in]
        s1_ref, s2_ref, r1, r2, p1, p2, send, recv = refs[n_in:]
        x, y, c, _ = _place()
        me = 4 * x + 2 * y + c
        p1[...] = jnp.zeros_like(p1)
        p2[...] = jnp.zeros_like(p2)
        row = 0
        for ref, (_, r, m) in zip(ins, wide_rows):
            if m == 1 and r % SUBLANE == 0 and row % SUBLANE == 0:
                p1[row:row + r, :] = ref[...]
                row += r
                continue
            for rr in range(r):
                for mm in range(m):
                    p1[row:row + 1, :] = ref[rr:rr + 1, mm * w:(mm + 1) * w]
                    row += 1
        row = 0
        for ref, arr in zip(ins[len(wide_rows):], ffn_rows):
            r = arr.shape[0]
            p2[row:row + r, :] = ref[...]
            row += r
        r1[me] = p1[...]
        r2[me] = p2[...]
        cps = []
        for mask in range(1, 8):
            peer = (x ^ (mask >> 2), y ^ ((mask >> 1) & 1), c ^ (mask & 1))
            for a, (src, dst) in enumerate(((p1, r1), (p2, r2))):
                cp = pltpu.make_async_remote_copy(
                    src_ref=src, dst_ref=dst.at[me], send_sem=send.at[a, mask - 1], recv_sem=recv.at[a, mask - 1],
                    device_id=peer, device_id_type=MESH)
                cp.start()
                cps.append(cp)
        for cp in cps:
            cp.wait()
        t1, t2 = r1[0], r2[0]
        for d in range(1, 8):
            t1 = t1 + r1[d]
            t2 = t2 + r2[d]
        s1_ref[...] = t1
        s2_ref[...] = t2

    ins = [a for a, _, _ in wide_rows] + list(ffn_rows)
    return pl.pallas_call(
        body, name="small_allreduce", in_specs=[VMEM_FULL] * n_in, out_specs=[VMEM_FULL, VMEM_FULL],
        out_shape=[jax.ShapeDtypeStruct((n_wide, w), F32), jax.ShapeDtypeStruct((n_ffn, dff), F32)],
        scratch_shapes=[pltpu.VMEM((8, n_wide, w), F32), pltpu.VMEM((8, n_ffn, dff), F32),
                        pltpu.VMEM((n_wide, w), F32), pltpu.VMEM((n_ffn, dff), F32),
                        pltpu.SemaphoreType.DMA((2, 7)), pltpu.SemaphoreType.DMA((2, 7))],
        compiler_params=pltpu.CompilerParams(vmem_limit_bytes=VMEM_LIMIT),
    )(*ins)


def _adamw(w, g, m, v):
    m2 = ADAM_B1 * m + (1.0 - ADAM_B1) * g
    v2 = ADAM_B2 * v + (1.0 - ADAM_B2) * (g * g)
    m_hat = m2 / (1.0 - ADAM_B1 ** ADAM_STEP)
    v_hat = v2 / (1.0 - ADAM_B2 ** ADAM_STEP)
    delta = -ADAM_LR * (m_hat / (jnp.sqrt(v_hat) + ADAM_EPS) + ADAM_WD * w)
    return delta, m2, v2


def _adam_big(name, w, g, m, v):
    r, c = w.shape
    tr = 128 if r % 128 == 0 else r

    def body(w_ref, g_ref, m_ref, v_ref, d_ref, m2_ref, v2_ref):
        d_ref[...], m2_ref[...], v2_ref[...] = _adamw(w_ref[...], g_ref[...], m_ref[...], v_ref[...])

    blk = pl.BlockSpec((tr, c), lambda i: (i, 0))
    return pl.pallas_call(
        body, name=name, grid=(r // tr,), in_specs=[blk] * 4, out_specs=[blk] * 3,
        out_shape=[jax.ShapeDtypeStruct((r, c), F32)] * 3,
        compiler_params=_params("parallel"),
    )(w, g, m, v)


def _adam_small(s1, s2, cw_g, fw_g, lb_logits, triples, layout, w):
    n = len(triples)

    def body(*refs):
        s1_ref, s2_ref, cw_ref, fw_ref, lbl_ref = refs[:5]
        prm = refs[5:5 + 3 * n]
        outs = refs[5 + 3 * n:]
        for p, lay in enumerate(layout):
            w_ref, m_ref, v_ref = prm[3 * p:3 * p + 3]
            g_ref, d_ref, m2_ref, v2_ref = outs[4 * p:4 * p + 4]
            if lay[0] == "wide":
                _, row, r, pieces = lay
                for rr in range(r):
                    for mm in range(pieces):
                        g_ref[rr:rr + 1, mm * w:(mm + 1) * w] = s1_ref[row:row + 1, :]
                        row += 1
            elif lay[0] == "ffn":
                _, row, r = lay
                g_ref[...] = s2_ref[row:row + r, :]
            elif lay[0] == "cw":
                g_ref[...] = cw_ref[0:g_ref.shape[0], :]
            elif lay[0] == "fw":
                g_ref[...] = fw_ref[0:g_ref.shape[0], :]
            else:
                s0 = _lower_bound(lbl_ref)
                d0 = s1_ref[lay[1]:lay[1] + 1, :] * s0 * (1.0 - s0)
                g_ref[0:1, :] = d0
                g_ref[1:2, :] = -d0
            d_ref[...], m2_ref[...], v2_ref[...] = _adamw(w_ref[...], g_ref[...], m_ref[...], v_ref[...])

    flat = [a for tr in triples for a in tr]
    shapes = []
    for tr in triples:
        shapes.extend([jax.ShapeDtypeStruct(tr[0].shape, F32)] * 4)
    return pl.pallas_call(
        body, name="adam_small", in_specs=[VMEM_FULL] * (5 + 3 * n), out_specs=[VMEM_FULL] * (4 * n),
        out_shape=shapes, compiler_params=pltpu.CompilerParams(vmem_limit_bytes=VMEM_LIMIT),
    )(s1, s2, cw_g, fw_g, lb_logits, *flat)


def _row_tile(t):
    return 512 if t % 512 == 0 and t >= 2048 else 128


def kernel(x, emb_ln_g, emb_ln_b, w_in, conv_w, conv_b, conv_norm_g, conv_norm_b, lb_logits, hgrn_norm_g, w_out, ln1_g, ln1_b, w_ffn_up, ffn_conv_w, ffn_conv_b, w_ffn_down, ln2_g, ln2_b, loss_target, m_emb_ln_g, m_emb_ln_b, m_w_in, m_conv_w, m_conv_b, m_conv_norm_g, m_conv_norm_b, m_lb_logits, m_hgrn_norm_g, m_w_out, m_ln1_g, m_ln1_b, m_w_ffn_up, m_ffn_conv_w, m_ffn_conv_b, m_w_ffn_down, m_ln2_g, m_ln2_b, v_emb_ln_g, v_emb_ln_b, v_w_in, v_conv_w, v_conv_b, v_conv_norm_g, v_conv_norm_b, v_lb_logits, v_hgrn_norm_g, v_w_out, v_ln1_g, v_ln1_b, v_w_ffn_up, v_ffn_conv_w, v_ffn_conv_b, v_w_ffn_down, v_ln2_g, v_ln2_b):
    depth = w_in.shape[0]
    assert depth == 1 and x.shape[0] == 1
    alpha = (2.0 * depth) ** 0.25
    t, d = x.shape[1], x.shape[2]
    w = d // 2
    dff = ffn_conv_b.shape[1]
    kc = conv_w.shape[1]
    assert w % (2 * LANE) == 0 and dff % (4 * LANE) == 0 and t % 128 == 0
    tm = _row_tile(t)
    tm2 = tm // 2
    tmm = 1024 if t % 1024 == 0 and t >= 2048 else tm
    cb = 2 * LANE
    cbf = 4 * LANE
    tb = tm
    nh = w // LANE
    hpb = 4 if nh % 4 == 0 else 2

    xi = lax.axis_index("x")
    yi = lax.axis_index("y")
    ci = lax.axis_index("c")
    chip = 2 * xi + yi
    c_idx = jnp.reshape(ci, (1,)).astype(jnp.int32)
    chip_idx = jnp.reshape(chip, (1,)).astype(jnp.int32)
    place_idx = jnp.stack([chip, ci]).astype(jnp.int32)

    x2 = x[0]
    tgt = loss_target[0]
    g0, b0 = emb_ln_g.reshape(1, d), emb_ln_b.reshape(1, d)
    w_in2, w_out2, w_up2, w_dn2 = w_in[0], w_out[0], w_ffn_up[0], w_ffn_down[0]
    cw2, fw2 = conv_w[0], ffn_conv_w[0]

    b_in = _place_shard(w_in2, "place_w_in", chip_idx, BF16)
    b_out = _place_shard(w_out2, "place_w_out", chip_idx, BF16)
    b_up = _place_shard(w_up2, "place_w_up", chip_idx, BF16)
    b_dn = _place_shard(w_dn2, "place_w_down", chip_idx, BF16)
    b_cw = _place_shard(_pad_rows(cw2), "place_conv_w", chip_idx, F32)
    b_fw = _place_shard(_pad_rows(fw2), "place_ffn_conv_w", chip_idx, F32)
    first = _run("gather_first_ici", _ride_gather_ici([b_in, b_cw, b_fw]))
    w_in3, cw_full3, fw_full3 = _run("gather_first_d2d", _ride_gather_d2d(first))
    cw_full = _unshard_cols(cw_full3)[:kc]
    fw_full = _unshard_cols(fw_full3)[:fw2.shape[0]]

    h0b = _ln0(x2, g0, b0, tm)
    p3, got = _proj("in_proj", h0b, w_in3, 6, tmm, w // 2, rider=_ride_gather_ici([b_out]))
    (cat, u1), (w_out3,) = _conv_fwd(p3, cw_full, conv_b, conv_norm_g, conv_norm_b, tm2, cb,
                                     rider=_ride_gather_d2d(got))
    w_out_full = w_out3.reshape(d, d)
    (cat, o_pre, states), got = _hgrn_fwd(p3, lb_logits, hgrn_norm_g, cat, tb, hpb,
                                          rider=_ride_gather_ici([b_up]))
    (xhat1, h1b, rstd1), (w_up3,) = _mix_ln1(cat, w_out_full, x2, g0, b0, ln1_g, ln1_b, alpha, tm2,
                                             rider=_ride_gather_d2d(got))
    hh3, got = _proj("ffn_up", h1b, w_up3, 2, tmm, dff // 4, rider=_ride_gather_ici([b_dn]))
    act, (w_dn3,) = _ffn_act_fwd(hh3, fw_full, ffn_conv_b, tm, cbf, rider=_ride_gather_d2d(got))
    ks = dff // N_CHIPS
    ffn = _wgrad("ffn_down", act, w_dn3, (t, d), (t // tmm, 1, N_CHIPS),
                 pl.BlockSpec((tmm, ks), lambda i, j, k: (i, k)),
                 pl.BlockSpec((None, ks, d), lambda i, j, k: (k, 0, 0)),
                 pl.BlockSpec((tmm, d), lambda i, j, k: (i, 0)), dot=_dot)
    dz2, dz2b, dg2, db2, loss_row = _ln2_loss(ffn, xhat1, tgt, ln1_g, ln1_b, ln2_g, ln2_b, alpha, tm2)

    dact = _proj_t("ffn_down_t", dz2b, w_dn3.reshape(dff, d), tmm, ks)
    dhh3, dfw, dfb = _ffn_act_bwd(dact, hh3, fw_full, ffn_conv_b, tm, cbf)
    tt = tmm
    d_w_dn = _wgrad("wgrad_down", act, dz2b, (N_CHIPS, ks, d), (N_CHIPS, 2, t // tt),
                    pl.BlockSpec((tt, ks), lambda s, j, k: (k, s)),
                    pl.BlockSpec((tt, d // 2), lambda s, j, k: (k, j)),
                    pl.BlockSpec((None, ks, d // 2), lambda s, j, k: (s, 0, j)))
    wu = 2 * dff // N_CHIPS
    tnu = wu // 2
    per_sec_u = dff // tnu
    pre1, (arr_dn,) = _wgrad(
        "up_t", dhh3, w_up3, (t, d), (t // tmm, 1, 2 * N_CHIPS),
        pl.BlockSpec((None, tmm, tnu), lambda i, j, k: (k // per_sec_u, i, k % per_sec_u)),
        pl.BlockSpec((None, d, tnu), lambda i, j, k: (k // 2, 0, k % 2)),
        pl.BlockSpec((tmm, d), lambda i, j, k: (i, 0)), dot=_dot_nt, rider=_ride_swap([d_w_dn]))
    dz1, dz1b, dg1, db1 = _ln1_bwd(pre1, dz2, xhat1, rstd1, ln1_g, alpha, tm2)
    part_dn = _add_halves("add_halves_w_down", d_w_dn, arr_dn, c_idx)
    d_w_up, (land_dn,) = _wgrad(
        "wgrad_up", h1b, dhh3, (N_CHIPS, d, wu), (N_CHIPS, 2, 2, t // tt),
        pl.BlockSpec((tt, d // 2), lambda s, r, j, k: (k, r)),
        pl.BlockSpec((None, tt, tnu), lambda s, r, j, k: ((2 * s + j) // per_sec_u, k, (2 * s + j) % per_sec_u)),
        pl.BlockSpec((None, d // 2, tnu), lambda s, r, j, k: (s, r, j)), rider=_ride_send_partials([part_dn]))
    dcat, (arr_up,) = _proj_t("out_proj_t", dz1b, w_out_full, tmm, d // 2, rider=_ride_swap([d_w_up]))
    part_up = _add_halves("add_halves_w_up", d_w_up, arr_up, c_idx)
    d_w_out = _wgrad("wgrad_out", cat, dz1b, (d, d), (2, 2, t // tt),
                     pl.BlockSpec((tt, d // 2), lambda r, j, k: (k, r)),
                     pl.BlockSpec((tt, d // 2), lambda r, j, k: (k, j)),
                     pl.BlockSpec((d // 2, d // 2), lambda r, j, k: (r, j))).reshape(N_CHIPS, d // N_CHIPS, d)
    du1, dcng, dcnb = _conv_norm_bwd(dcat, u1, conv_norm_g, conv_norm_b, tm)
    (dp3, dcw, dcb), (arr_out,) = _conv_bwd(du1, p3, cw_full, tm2, cb, rider=_ride_swap([d_w_out]))
    part_out = _add_halves("add_halves_w_out", d_w_out, arr_out, c_idx)
    (dp3, dlb, dhg), (land_up, land_out) = _hgrn_bwd(p3, lb_logits, hgrn_norm_g, o_pre, states, dcat, dp3, tb, hpb,
                                                    rider=_ride_send_partials([part_up, part_out]))
    wi = 6 * w // N_CHIPS
    tni = w // 2
    d_w_in = _wgrad("wgrad_in", h0b, dp3, (N_CHIPS, d, wi), (N_CHIPS, 2, wi // tni, t // tt),
                    pl.BlockSpec((tt, d // 2), lambda s, r, j, k: (k, r)),
                    pl.BlockSpec((None, tt, tni), lambda s, r, j, k: (((wi // tni) * s + j) // 2, k, ((wi // tni) * s + j) % 2)),
                    pl.BlockSpec((None, d // 2, tni), lambda s, r, j, k: (s, r, j)))
    (arr_in,) = _run("swap_w_in", _ride_swap([d_w_in]))
    part_in = _add_halves("add_halves_w_in", d_w_in, arr_in, c_idx)
    per_shard_i = wi // tni
    pre0, (land_in,) = _wgrad(
        "in_t", dp3, w_in3, (t, d), (t // tmm, 1, 2 * dp3.shape[0]),
        pl.BlockSpec((None, tmm, tni), lambda i, j, k: (k // 2, i, k % 2)),
        pl.BlockSpec((None, d, tni), lambda i, j, k: (k // per_shard_i, 0, k % per_shard_i)),
        pl.BlockSpec((tmm, d), lambda i, j, k: (i, 0)), dot=_dot_nt, rider=_ride_send_partials([part_in]))
    dx, dg0, db0 = _ln0_bwd(pre0, dz1, x2, g0, alpha, tm2)
    halves = [_sum_partials("sum_partials_" + nm, p, a, place_idx)
              for nm, p, a in (("w_in", part_in, land_in), ("w_out", part_out, land_out),
                               ("w_up", part_up, land_up), ("w_down", part_dn, land_dn))]
    g_w_in, g_w_out, g_w_up, g_w_dn = _run("join_halves", _ride_join(halves))

    kpad = dcw.shape[0]
    wide = [(dcw, kpad, 1), (dg0, 1, 2), (db0, 1, 2), (dg1, 1, 2), (db1, 1, 2), (dg2, 1, 2), (db2, 1, 2),
            (dcb, 1, 1), (dcng, 1, 1), (dcnb, 1, 1), (dlb, 1, 1), (dhg, 1, 1)]
    n_wide = sum(r * m for _, r, m in wide)
    n_wide_pad = -(-n_wide // SUBLANE) * SUBLANE
    s1, s2 = _small_allreduce(wide, [dfw, dfb], w, dff, n_wide_pad, 2 * SUBLANE)
    cw_g = lax.dynamic_slice_in_dim(s1[0:kpad], chip * (w // N_CHIPS), w // N_CHIPS, axis=1)
    fw_g = lax.dynamic_slice_in_dim(s2[0:SUBLANE], chip * (dff // N_CHIPS), dff // N_CHIPS, axis=1)

    small = [
        (g0, m_emb_ln_g.reshape(1, d), v_emb_ln_g.reshape(1, d)), (b0, m_emb_ln_b.reshape(1, d), v_emb_ln_b.reshape(1, d)),
        (cw2, m_conv_w[0], v_conv_w[0]), (conv_b, m_conv_b, v_conv_b),
        (conv_norm_g, m_conv_norm_g, v_conv_norm_g), (conv_norm_b, m_conv_norm_b, v_conv_norm_b),
        (lb_logits, m_lb_logits, v_lb_logits), (hgrn_norm_g, m_hgrn_norm_g, v_hgrn_norm_g),
        (ln1_g, m_ln1_g, v_ln1_g), (ln1_b, m_ln1_b, v_ln1_b),
        (fw2, m_ffn_conv_w[0], v_ffn_conv_w[0]), (ffn_conv_b, m_ffn_conv_b, v_ffn_conv_b),
        (ln2_g, m_ln2_g, v_ln2_g), (ln2_b, m_ln2_b, v_ln2_b),
    ]
    r0 = kpad
    layout = [("wide", r0, 1, 2), ("wide", r0 + 2, 1, 2), ("cw",), ("wide", r0 + 12, 1, 1), ("wide", r0 + 13, 1, 1),
              ("wide", r0 + 14, 1, 1), ("lb", r0 + 15), ("wide", r0 + 16, 1, 1), ("wide", r0 + 4, 1, 2),
              ("wide", r0 + 6, 1, 2), ("fw",), ("ffn", SUBLANE, 1), ("wide", r0 + 8, 1, 2), ("wide", r0 + 10, 1, 2)]
    so = _adam_small(s1, s2, cw_g, fw_g, lb_logits, small, layout, w)
    sm = {nm: so[4 * i:4 * i + 4] for i, nm in enumerate(
        ["emb_ln_g", "emb_ln_b", "conv_w", "conv_b", "conv_norm_g", "conv_norm_b", "lb_logits", "hgrn_norm_g",
         "ln1_g", "ln1_b", "ffn_conv_w", "ffn_conv_b", "ln2_g", "ln2_b"])}
    bigs = {}
    for nm, wt, g, m, v in (("w_in", w_in2, g_w_in, m_w_in[0], v_w_in[0]), ("w_out", w_out2, g_w_out, m_w_out[0], v_w_out[0]),
                            ("w_ffn_up", w_up2, g_w_up, m_w_ffn_up[0], v_w_ffn_up[0]),
                            ("w_ffn_down", w_dn2, g_w_dn, m_w_ffn_down[0], v_w_ffn_down[0])):
        bigs[nm] = (g,) + tuple(_adam_big("adam_" + nm, wt, g, m, v))

    loss = lax.psum(loss_row[0, 0], ("x", "y", "c"))

    order = ["emb_ln_g", "emb_ln_b", "w_in", "conv_w", "conv_b", "conv_norm_g", "conv_norm_b", "lb_logits",
             "hgrn_norm_g", "w_out", "ln1_g", "ln1_b", "w_ffn_up", "ffn_conv_w", "ffn_conv_b", "w_ffn_down",
             "ln2_g", "ln2_b"]
    shapes = dict(emb_ln_g=emb_ln_g.shape, emb_ln_b=emb_ln_b.shape, w_in=w_in.shape, conv_w=conv_w.shape,
                  w_out=w_out.shape, w_ffn_up=w_ffn_up.shape, ffn_conv_w=ffn_conv_w.shape, w_ffn_down=w_ffn_down.shape)
    outs = [loss, dx.reshape(x.shape)]
    for which in range(4):
        for nm in order:
            a = bigs[nm][which] if nm in bigs else sm[nm][which]
            outs.append(a.reshape(shapes[nm]) if nm in shapes else a)
    return tuple(outs)


def _pad_rows(a):
    k = a.shape[0]
    kp = -(-k // 16) * 16
    return jnp.pad(a, ((0, kp - k), (0, 0)))


def _unshard_cols(a3):
    s, k, c = a3.shape
    return jnp.transpose(a3, (1, 0, 2)).reshape(k, s * c)
```

```python
import functools

import jax
import jax.numpy as jnp
from jax import lax
from jax.experimental import pallas as pl
from jax.experimental.pallas import tpu as pltpu

F32 = jnp.float32
BF16 = jnp.bfloat16

LN_EPS = 1e-5
RMS_EPS = 1e-6
LANE = 128
SUBLANE = 8
CHUNK = 64
SUB = 8
HALO = 32
FHALO = 8
ROWS = 64
N_CHIPS = 4
VMEM_LIMIT = 56 << 20
NEG_BIG = -1e30

ADAM_LR = 0.001
ADAM_B1 = 0.9
ADAM_B2 = 0.999
ADAM_EPS = 1e-08
ADAM_WD = 0.01
ADAM_STEP = 10

MESH = pl.DeviceIdType.MESH
HBM = pl.BlockSpec(memory_space=pl.ANY)
VMEM_FULL = pl.BlockSpec(memory_space=pltpu.VMEM)


def _params(*sem):
    return pltpu.CompilerParams(dimension_semantics=sem, vmem_limit_bytes=VMEM_LIMIT)


class _Rider:
    def __init__(self, ins, outs, aliases, n_sems, start, finish):
        self.ins, self.outs, self.aliases = list(ins), list(outs), dict(aliases)
        self.n_sems, self.start, self.finish = n_sems, start, finish


def _call(body, args, *, name, grid, in_specs, out_specs, out_shape, scratch_shapes=(), aliases=None, rider=None):
    many = isinstance(out_shape, (list, tuple))
    shapes = list(out_shape) if many else [out_shape]
    ospecs = list(out_specs) if many else [out_specs]
    aliases = dict(aliases or {})
    sem = ("arbitrary",) * len(grid)
    if rider is None:
        res = pl.pallas_call(
            body, name=name, grid=grid, in_specs=list(in_specs), out_specs=ospecs, out_shape=shapes,
            scratch_shapes=list(scratch_shapes), input_output_aliases=aliases, compiler_params=_params(*sem))(*args)
        return res if many else res[0]
    n_in, n_out, n_scr = len(args), len(shapes), len(scratch_shapes)
    nri, nro = len(rider.ins), len(rider.outs)

    def wrapped(*refs):
        ins, rin = refs[:n_in], refs[n_in:n_in + nri]
        o0 = n_in + nri
        outs, rout = refs[o0:o0 + n_out], refs[o0 + n_out:o0 + n_out + nro]
        s0 = o0 + n_out + nro
        scr, (send, recv) = refs[s0:s0 + n_scr], refs[s0 + n_scr:]
        ids = [pl.program_id(a) for a in range(len(grid))]
        first = functools.reduce(jnp.logical_and, [i == 0 for i in ids])
        last = functools.reduce(jnp.logical_and, [i == g - 1 for i, g in zip(ids, grid)])

        @pl.when(first)
        def _():
            rider.start(rin, rout, send, recv)

        body(*ins, *outs, *scr)

        @pl.when(last)
        def _():
            rider.finish(rin, rout, send, recv)

    for ri, ro in rider.aliases.items():
        aliases[n_in + ri] = n_out + ro
    res = pl.pallas_call(
        wrapped, name=name, grid=grid, in_specs=list(in_specs) + [HBM] * nri, out_specs=ospecs + [HBM] * nro,
        out_shape=shapes + rider.outs,
        scratch_shapes=list(scratch_shapes) + [pltpu.SemaphoreType.DMA((rider.n_sems,)),
                                               pltpu.SemaphoreType.DMA((rider.n_sems,))],
        input_output_aliases=aliases, compiler_params=_params(*sem))(*args, *rider.ins)
    main, extra = res[:n_out], list(res[n_out:])
    return (list(main) if many else main[0]), extra


def _div_tile(n, mult, cap):
    best = n
    for t in range(mult, min(n, cap) + 1, mult):
        if n % t == 0:
            best = t
    return best


def _sigmoid(x):
    return 1.0 / (1.0 + jnp.exp(-x))


def _ln_stats(x):
    mu = jnp.mean(x, axis=-1, keepdims=True)
    xc = x - mu
    var = jnp.mean(xc * xc, axis=-1, keepdims=True)
    rstd = lax.rsqrt(var + LN_EPS)
    return xc * rstd, rstd


def _ln_bwd(dy, xhat, rstd, g):
    dyg = dy * g
    m1 = jnp.mean(dyg, axis=-1, keepdims=True)
    m2 = jnp.mean(dyg * xhat, axis=-1, keepdims=True)
    return rstd * (dyg - m1 - xhat * m2)


def _dot_nt(a, b):
    return lax.dot_general(a, b, (((1,), (1,)), ((), ())), preferred_element_type=F32)


def _dot_tn(a, b):
    return lax.dot_general(a, b, (((0,), (0,)), ((), ())), preferred_element_type=F32)


def _dot(a, b):
    return jnp.dot(a, b, preferred_element_type=F32)


def _dot3(m, x):
    mb = m.astype(BF16)
    x1 = x.astype(BF16)
    r1 = x - x1.astype(F32)
    x2 = r1.astype(BF16)
    x3 = (r1 - x2.astype(F32)).astype(BF16)
    return _dot(mb, x1) + _dot(mb, x2) + _dot(mb, x3)


def _place_shard(x, name, chip_idx, dtype):
    r, c = x.shape
    tr = _div_tile(r, 16, 512)

    def body(s_ref, x_ref, o_ref):
        del s_ref
        o_ref[...] = x_ref[...].astype(dtype)

    return pl.pallas_call(
        body, name=name,
        grid_spec=pltpu.PrefetchScalarGridSpec(
            num_scalar_prefetch=1, grid=(r // tr,),
            in_specs=[pl.BlockSpec((tr, c), lambda i, s: (i, 0))],
            out_specs=pl.BlockSpec((None, tr, c), lambda i, s: (s[0], i, 0))),
        out_shape=jax.ShapeDtypeStruct((N_CHIPS, r, c), dtype),
        compiler_params=_params("parallel"),
    )(chip_idx, x)


def _ln0(x, g, b, tm):
    t, d = x.shape

    def body(x_ref, g_ref, b_ref, o_ref):
        xh, _ = _ln_stats(x_ref[...])
        o_ref[...] = (xh * g_ref[...] + b_ref[...]).astype(BF16)

    row = pl.BlockSpec((1, d), lambda i: (0, 0))
    return pl.pallas_call(
        body, name="ln0", grid=(t // tm,),
        in_specs=[pl.BlockSpec((tm, d), lambda i: (i, 0)), row, row],
        out_specs=pl.BlockSpec((tm, d), lambda i: (i, 0)),
        out_shape=jax.ShapeDtypeStruct((t, d), BF16),
        compiler_params=_params("parallel"),
    )(x, g, b)


def _proj(name, a, w3, n_sec, tm, tn, rider=None):
    m, k = a.shape
    s, _, ws = w3.shape
    sec_w = s * ws // n_sec
    nj = ws // tn
    per_sec = sec_w // tn

    def body(a_ref, w_ref, o_ref):
        o_ref[...] = _dot(a_ref[...], w_ref[...])

    return _call(
        body, (a, w3), name=name, grid=(s * nj, m // tm),
        in_specs=[pl.BlockSpec((tm, k), lambda j, i: (i, 0)),
                  pl.BlockSpec((None, k, tn), lambda j, i: (j // nj, 0, j % nj))],
        out_specs=pl.BlockSpec((None, tm, tn), lambda j, i: (j // per_sec, i, j % per_sec)),
        out_shape=jax.ShapeDtypeStruct((n_sec, m, sec_w), F32), rider=rider)


def _proj_t(name, a, w, tm, tn, rider=None):
    m, k = a.shape
    n = w.shape[0]

    def body(a_ref, w_ref, o_ref):
        o_ref[...] = _dot_nt(a_ref[...], w_ref[...])

    return _call(
        body, (a, w), name=name, grid=(n // tn, m // tm),
        in_specs=[pl.BlockSpec((tm, k), lambda j, i: (i, 0)),
                  pl.BlockSpec((tn, k), lambda j, i: (j, 0))],
        out_specs=pl.BlockSpec((tm, tn), lambda j, i: (i, j)),
        out_shape=jax.ShapeDtypeStruct((m, n), F32), rider=rider)


def _wgrad(name, a, b, out_shape, grid, a_spec, b_spec, o_spec, rider=None, dot=_dot_tn):
    nt = len(grid) - 1

    def body(a_ref, b_ref, o_ref):
        t = pl.program_id(nt)
        prod = dot(a_ref[...], b_ref[...])

        @pl.when(t == 0)
        def _():
            o_ref[...] = prod

        @pl.when(t > 0)
        def _():
            o_ref[...] += prod

    return _call(
        body, (a, b), name=name, grid=grid, in_specs=[a_spec, b_spec], out_specs=o_spec,
        out_shape=jax.ShapeDtypeStruct(out_shape, F32), rider=rider)


def _mix_ln1(cat, w_out, x, g0, b0, g1, b1, alpha, tm, rider=None):
    t, d = x.shape

    def body(cat_ref, w_ref, x_ref, g0_ref, b0_ref, g1_ref, b1_ref, xh_ref, h1b_ref, rstd_ref):
        mix = _dot(cat_ref[...], w_ref[...])
        xh0, _ = _ln_stats(x_ref[...])
        z1 = alpha * (xh0 * g0_ref[...] + b0_ref[...]) + mix
        xh1, rstd1 = _ln_stats(z1)
        xh_ref[...] = xh1
        h1b_ref[...] = (xh1 * g1_ref[...] + b1_ref[...]).astype(BF16)
        rstd_ref[...] = rstd1

    row = pl.BlockSpec((1, d), lambda i: (0, 0))
    blk = pl.BlockSpec((tm, d), lambda i: (i, 0))
    return _call(
        body, (cat, w_out, x, g0, b0, g1, b1), name="mix_ln1", grid=(t // tm,),
        in_specs=[blk, pl.BlockSpec((d, d), lambda i: (0, 0)), blk, row, row, row, row],
        out_specs=[blk, blk, pl.BlockSpec((tm, 1), lambda i: (i, 0))],
        out_shape=[jax.ShapeDtypeStruct((t, d), F32), jax.ShapeDtypeStruct((t, d), BF16),
                   jax.ShapeDtypeStruct((t, 1), F32)], rider=rider)


def _ln2_loss(ffn, xhat1, tgt, g1, b1, g2, b2, alpha, tm):
    t, d = xhat1.shape
    ni = t // tm
    inv_d = 1.0 / d

    def body(ffn_ref, xh1_ref, tgt_ref, g1_ref, b1_ref, g2_ref, b2_ref,
             dz2_ref, dz2b_ref, dg2_ref, db2_ref, loss_ref, lrow):
        i = pl.program_id(0)
        h1 = xh1_ref[...] * g1_ref[...] + b1_ref[...]
        xh2, rstd2 = _ln_stats(alpha * h1 + ffn_ref[...])
        g2v = g2_ref[...]
        diff = xh2 * g2v + b2_ref[...] - tgt_ref[...]
        dh2 = diff * inv_d
        sq = jnp.sum(diff * diff, axis=0, keepdims=True)
        dg = jnp.sum(dh2 * xh2, axis=0, keepdims=True)
        db = jnp.sum(dh2, axis=0, keepdims=True)

        @pl.when(i == 0)
        def _():
            lrow[...] = sq
            dg2_ref[...] = dg
            db2_ref[...] = db

        @pl.when(i > 0)
        def _():
            lrow[...] += sq
            dg2_ref[...] += dg
            db2_ref[...] += db

        dz2 = _ln_bwd(dh2, xh2, rstd2, g2v)
        dz2_ref[...] = dz2
        dz2b_ref[...] = dz2.astype(BF16)

        @pl.when(i == ni - 1)
        def _():
            tot = jnp.sum(lrow[...], axis=-1, keepdims=True) * (0.5 * inv_d)
            loss_ref[...] = jnp.broadcast_to(tot, (1, LANE))

    row = pl.BlockSpec((1, d), lambda i: (0, 0))
    blk = pl.BlockSpec((tm, d), lambda i: (i, 0))
    return _call(
        body, (ffn, xhat1, tgt, g1, b1, g2, b2), name="ln2_loss", grid=(ni,),
        in_specs=[blk, blk, blk, row, row, row, row],
        out_specs=[blk, blk, row, row, pl.BlockSpec((1, LANE), lambda i: (0, 0))],
        out_shape=[jax.ShapeDtypeStruct((t, d), F32), jax.ShapeDtypeStruct((t, d), BF16),
                   jax.ShapeDtypeStruct((1, d), F32), jax.ShapeDtypeStruct((1, d), F32),
                   jax.ShapeDtypeStruct((1, LANE), F32)],
        scratch_shapes=[pltpu.VMEM((1, d), F32)])


def _ln1_bwd(pre, dz2, xhat1, rstd1, g1, alpha, tm):
    t, d = dz2.shape

    def body(pre_ref, dz2_ref, xh_ref, rstd_ref, g_ref, dz1_ref, dz1b_ref, dg_ref, db_ref):
        i = pl.program_id(0)
        dh1 = alpha * dz2_ref[...] + pre_ref[...]
        xh = xh_ref[...]
        dg = jnp.sum(dh1 * xh, axis=0, keepdims=True)
        db = jnp.sum(dh1, axis=0, keepdims=True)

        @pl.when(i == 0)
        def _():
            dg_ref[...] = dg
            db_ref[...] = db

        @pl.when(i > 0)
        def _():
            dg_ref[...] += dg
            db_ref[...] += db

        dz1 = _ln_bwd(dh1, xh, rstd_ref[...], g_ref[...])
        dz1_ref[...] = dz1
        dz1b_ref[...] = dz1.astype(BF16)

    row = pl.BlockSpec((1, d), lambda i: (0, 0))
    blk = pl.BlockSpec((tm, d), lambda i: (i, 0))
    return _call(
        body, (pre, dz2, xhat1, rstd1, g1), name="ln1_bwd", grid=(t // tm,),
        in_specs=[blk, blk, blk, pl.BlockSpec((tm, 1), lambda i: (i, 0)), row],
        out_specs=[blk, blk, row, row],
        out_shape=[jax.ShapeDtypeStruct((t, d), F32), jax.ShapeDtypeStruct((t, d), BF16),
                   jax.ShapeDtypeStruct((1, d), F32), jax.ShapeDtypeStruct((1, d), F32)])


def _ln0_bwd(pre, dz1, x, g0, alpha, tm):
    t, d = x.shape

    def body(pre_ref, dz1_ref, x_ref, g_ref, dx_ref, dg_ref, db_ref):
        i = pl.program_id(0)
        dh0 = alpha * dz1_ref[...] + pre_ref[...]
        xh, rstd = _ln_stats(x_ref[...])
        dg = jnp.sum(dh0 * xh, axis=0, keepdims=True)
        db = jnp.sum(dh0, axis=0, keepdims=True)

        @pl.when(i == 0)
        def _():
            dg_ref[...] = dg
            db_ref[...] = db

        @pl.when(i > 0)
        def _():
            dg_ref[...] += dg
            db_ref[...] += db

        dx_ref[...] = _ln_bwd(dh0, xh, rstd, g_ref[...])

    row = pl.BlockSpec((1, d), lambda i: (0, 0))
    blk = pl.BlockSpec((tm, d), lambda i: (i, 0))
    return _call(
        body, (pre, dz1, x, g0), name="ln0_bwd", grid=(t // tm,),
        in_specs=[blk, blk, blk, row], out_specs=[blk, row, row],
        out_shape=[jax.ShapeDtypeStruct((t, d), F32), jax.ShapeDtypeStruct((1, d), F32),
                   jax.ShapeDtypeStruct((1, d), F32)])


def _conv_fwd(p3, conv_w, conv_b, cn_g, cn_b, tc, cb, rider=None):
    _, t, w = p3.shape
    kk = conv_w.shape[0]
    off = HALO - (kk - 1)
    hb = tc // HALO

    def body(a_ref, g_ref, ap_ref, gp_ref, w_ref, b_ref, ng_ref, nb_ref, cat_ref, u1_ref, ext):
        i = pl.program_id(1)
        ext[pl.ds(HALO, tc), :] = a_ref[...] * _sigmoid(g_ref[...])
        prev = ap_ref[...] * _sigmoid(gp_ref[...])
        ext[pl.ds(0, HALO), :] = jnp.where(i > 0, prev, 0.0)
        for r in range(tc // ROWS):
            acc = jnp.broadcast_to(b_ref[...], (ROWS, cb))
            for k in range(kk):
                acc = acc + w_ref[k:k + 1, :] * ext[pl.ds(r * ROWS + off + k, ROWS), :]
            u1_ref[pl.ds(r * ROWS, ROWS), :] = acc
            for g in range(cb // LANE):
                sl = slice(g * LANE, (g + 1) * LANE)
                xh, _ = _ln_stats(acc[:, sl])
                u2 = xh * ng_ref[:, sl] + nb_ref[:, sl]
                cat_ref[pl.ds(r * ROWS, ROWS), sl] = (u2 * _sigmoid(u2)).astype(BF16)

    cur = lambda sec: pl.BlockSpec((None, tc, cb), lambda j, i: (sec, i, j))
    prev = lambda sec: pl.BlockSpec((None, HALO, cb), lambda j, i: (sec, jnp.maximum(i * hb - 1, 0), j))
    row = pl.BlockSpec((1, cb), lambda j, i: (0, j))
    return _call(
        body, (p3, p3, p3, p3, conv_w, conv_b, cn_g, cn_b), name="conv_fwd", grid=(w // cb, t // tc),
        in_specs=[cur(0), cur(1), prev(0), prev(1), pl.BlockSpec((kk, cb), lambda j, i: (0, j)), row, row, row],
        out_specs=[pl.BlockSpec((tc, cb), lambda j, i: (i, j)), pl.BlockSpec((tc, cb), lambda j, i: (i, j))],
        out_shape=[jax.ShapeDtypeStruct((t, 2 * w), BF16), jax.ShapeDtypeStruct((t, w), F32)],
        scratch_shapes=[pltpu.VMEM((tc + HALO, cb), F32)], rider=rider)


def _conv_norm_bwd(dcat, u1, cn_g, cn_b, tc):
    t, w = u1.shape

    def body(du_ref, u1_ref, ng_ref, nb_ref, du1_ref, dg_ref, db_ref):
        i = pl.program_id(0)
        for g in range(w // LANE):
            sl = slice(g * LANE, (g + 1) * LANE)
            ng = ng_ref[:, sl]
            xh, rstd = _ln_stats(u1_ref[:, sl])
            u2 = xh * ng + nb_ref[:, sl]
            sg = _sigmoid(u2)
            du2 = du_ref[:, sl] * (sg * (1.0 + u2 * (1.0 - sg)))
            dg = jnp.sum(du2 * xh, axis=0, keepdims=True)
            db = jnp.sum(du2, axis=0, keepdims=True)

            @pl.when(i == 0)
            def _():
                dg_ref[:, sl] = dg
                db_ref[:, sl] = db

            @pl.when(i > 0)
            def _():
                dg_ref[:, sl] += dg
                db_ref[:, sl] += db

            du1_ref[:, sl] = _ln_bwd(du2, xh, rstd, ng)

    row = pl.BlockSpec((1, w), lambda i: (0, 0))
    blk = pl.BlockSpec((tc, w), lambda i: (i, 0))
    return pl.pallas_call(
        body, name="conv_norm_bwd", grid=(t // tc,),
        in_specs=[blk, blk, row, row], out_specs=[blk, row, row],
        out_shape=[jax.ShapeDtypeStruct((t, w), F32), jax.ShapeDtypeStruct((1, w), F32),
                   jax.ShapeDtypeStruct((1, w), F32)],
        compiler_params=_params("arbitrary"),
    )(dcat, u1, cn_g, cn_b)


def _conv_bwd(du1, p3, conv_w, tc, cb, rider=None):
    n_sec, t, w = p3.shape
    kk = conv_w.shape[0]
    off = HALO - (kk - 1)
    hb = tc // HALO
    nt = t // tc
    kpad = -(-kk // SUBLANE) * SUBLANE

    def body(d_ref, dn_ref, a_ref, g_ref, ap_ref, gp_ref, w_ref, dp_ref, dw_ref, db_ref, extd, extu, wacc, bacc):
        i = pl.program_id(1)

        @pl.when(i == 0)
        def _():
            wacc[...] = jnp.zeros_like(wacc)
            bacc[...] = jnp.zeros_like(bacc)

        extd[pl.ds(0, tc), :] = d_ref[...]
        extd[pl.ds(tc, HALO), :] = jnp.where(i < nt - 1, dn_ref[...], 0.0)
        extu[pl.ds(HALO, tc), :] = a_ref[...] * _sigmoid(g_ref[...])
        extu[pl.ds(0, HALO), :] = jnp.where(i > 0, ap_ref[...] * _sigmoid(gp_ref[...]), 0.0)
        for r in range(tc // ROWS):
            rows = pl.ds(r * ROWS, ROWS)
            acc = jnp.zeros((ROWS, cb), F32)
            for k in range(kk):
                acc = acc + w_ref[k:k + 1, :] * extd[pl.ds(r * ROWS + (kk - 1) - k, ROWS), :]
            a = a_ref[rows, :]
            sg = _sigmoid(g_ref[rows, :])
            dp_ref[0, rows, :] = (acc * sg).astype(BF16)
            dp_ref[1, rows, :] = (acc * a * sg * (1.0 - sg)).astype(BF16)
            d = d_ref[rows, :]
            bacc[...] += jnp.sum(d.reshape(ROWS // SUBLANE, SUBLANE, cb), axis=0)
            for k in range(kk):
                prod = d * extu[pl.ds(r * ROWS + off + k, ROWS), :]
                wacc[k] += jnp.sum(prod.reshape(ROWS // SUBLANE, SUBLANE, cb), axis=0)

        @pl.when(i == nt - 1)
        def _():
            for k in range(kk):
                dw_ref[k:k + 1, :] = jnp.sum(wacc[k], axis=0, keepdims=True)
            if kpad > kk:
                dw_ref[kk:kpad, :] = jnp.zeros((kpad - kk, cb), F32)
            db_ref[...] = jnp.sum(bacc[...], axis=0, keepdims=True)

    cur = lambda sec: pl.BlockSpec((None, tc, cb), lambda j, i: (sec, i, j))
    prev = lambda sec: pl.BlockSpec((None, HALO, cb), lambda j, i: (sec, jnp.maximum(i * hb - 1, 0), j))
    return _call(
        body, (du1, du1, p3, p3, p3, p3, conv_w), name="conv_bwd", grid=(w // cb, nt),
        in_specs=[pl.BlockSpec((tc, cb), lambda j, i: (i, j)),
                  pl.BlockSpec((HALO, cb), lambda j, i: (jnp.minimum((i + 1) * hb, t // HALO - 1), j)),
                  cur(0), cur(1), prev(0), prev(1), pl.BlockSpec((kk, cb), lambda j, i: (0, j))],
        out_specs=[pl.BlockSpec((2, tc, cb), lambda j, i: (0, i, j)),
                   pl.BlockSpec((kpad, cb), lambda j, i: (0, j)),
                   pl.BlockSpec((1, cb), lambda j, i: (0, j))],
        out_shape=[jax.ShapeDtypeStruct((n_sec, t, w), BF16), jax.ShapeDtypeStruct((kpad, w), F32),
                   jax.ShapeDtypeStruct((1, w), F32)],
        scratch_shapes=[pltpu.VMEM((tc + HALO, cb), F32), pltpu.VMEM((tc + HALO, cb), F32),
                        pltpu.VMEM((kk, SUBLANE, cb), F32), pltpu.VMEM((SUBLANE, cb), F32)], rider=rider)


def _ffn_act_fwd(hh3, fw, fb, tc, cb, rider=None):
    _, t, dff = hh3.shape
    kk = fw.shape[0]
    off = FHALO - (kk - 1)
    hb = tc // FHALO

    def body(g_ref, v_ref, gp_ref, w_ref, b_ref, act_ref, ext):
        i = pl.program_id(1)
        ext[pl.ds(FHALO, tc), :] = g_ref[...]
        ext[pl.ds(0, FHALO), :] = jnp.where(i > 0, gp_ref[...], 0.0)
        for r in range(tc // ROWS):
            rows = pl.ds(r * ROWS, ROWS)
            gc = jnp.broadcast_to(b_ref[...], (ROWS, cb))
            for k in range(kk):
                gc = gc + w_ref[k:k + 1, :] * ext[pl.ds(r * ROWS + off + k, ROWS), :]
            act_ref[rows, :] = (gc * _sigmoid(gc) * v_ref[rows, :]).astype(BF16)

    return _call(
        body, (hh3, hh3, hh3, fw, fb), name="ffn_act_fwd", grid=(dff // cb, t // tc),
        in_specs=[pl.BlockSpec((None, tc, cb), lambda j, i: (0, i, j)),
                  pl.BlockSpec((None, tc, cb), lambda j, i: (1, i, j)),
                  pl.BlockSpec((None, FHALO, cb), lambda j, i: (0, jnp.maximum(i * hb - 1, 0), j)),
                  pl.BlockSpec((kk, cb), lambda j, i: (0, j)),
                  pl.BlockSpec((1, cb), lambda j, i: (0, j))],
        out_specs=pl.BlockSpec((tc, cb), lambda j, i: (i, j)),
        out_shape=jax.ShapeDtypeStruct((t, dff), BF16),
        scratch_shapes=[pltpu.VMEM((tc + FHALO, cb), F32)], rider=rider)


def _ffn_act_bwd(dact, hh3, fw, fb, tc, cb):
    _, t, dff = hh3.shape
    kk = fw.shape[0]
    off = FHALO - (kk - 1)
    hb = tc // FHALO
    nt = t // tc
    te = tc + FHALO

    def body(da_ref, dan_ref, g_ref, gp_ref, gn_ref, v_ref, vn_ref, w_ref, b_ref,
             dhh_ref, dw_ref, db_ref, gext, dext, wacc, bacc):
        i = pl.program_id(1)

        @pl.when(i == 0)
        def _():
            wacc[...] = jnp.zeros_like(wacc)
            bacc[...] = jnp.zeros_like(bacc)

        gext[pl.ds(0, FHALO), :] = jnp.where(i > 0, gp_ref[...], 0.0)
        gext[pl.ds(FHALO, tc), :] = g_ref[...]
        gext[pl.ds(FHALO + tc, FHALO), :] = gn_ref[...]

        def gate_grad(r0, n, da, v):
            gc = jnp.broadcast_to(b_ref[...], (n, cb))
            for k in range(kk):
                gc = gc + w_ref[k:k + 1, :] * gext[pl.ds(r0 + off + k, n), :]
            sg = _sigmoid(gc)
            return gc * sg, da * v * (sg * (1.0 + gc * (1.0 - sg)))

        for r in range(tc // ROWS):
            rows = pl.ds(r * ROWS, ROWS)
            da = da_ref[rows, :]
            silu, dgc = gate_grad(r * ROWS, ROWS, da, v_ref[rows, :])
            dext[rows, :] = dgc
            dhh_ref[1, rows, :] = (da * silu).astype(BF16)
        _, dgc_next = gate_grad(tc, FHALO, dan_ref[...], vn_ref[...])
        dext[pl.ds(tc, FHALO), :] = jnp.where(i < nt - 1, dgc_next, 0.0)
        for r in range(tc // ROWS):
            rows = pl.ds(r * ROWS, ROWS)
            dg = jnp.zeros((ROWS, cb), F32)
            for k in range(kk):
                dg = dg + w_ref[k:k + 1, :] * dext[pl.ds(r * ROWS + (kk - 1) - k, ROWS), :]
            dhh_ref[0, rows, :] = dg.astype(BF16)
            dgc = dext[rows, :]
            bacc[...] += jnp.sum(dgc.reshape(ROWS // SUBLANE, SUBLANE, cb), axis=0)
            for k in range(kk):
                prod = dgc * gext[pl.ds(r * ROWS + off + k, ROWS), :]
                wacc[k] += jnp.sum(prod.reshape(ROWS // SUBLANE, SUBLANE, cb), axis=0)

        @pl.when(i == nt - 1)
        def _():
            for k in range(kk):
                dw_ref[k:k + 1, :] = jnp.sum(wacc[k], axis=0, keepdims=True)
            dw_ref[kk:SUBLANE, :] = jnp.zeros((SUBLANE - kk, cb), F32)
            db_ref[...] = jnp.sum(bacc[...], axis=0, keepdims=True)

    nxt = lambda i: jnp.minimum((i + 1) * hb, t // FHALO - 1)
    return pl.pallas_call(
        body, name="ffn_act_bwd", grid=(dff // cb, nt),
        in_specs=[pl.BlockSpec((tc, cb), lambda j, i: (i, j)),
                  pl.BlockSpec((FHALO, cb), lambda j, i: (nxt(i), j)),
                  pl.BlockSpec((None, tc, cb), lambda j, i: (0, i, j)),
                  pl.BlockSpec((None, FHALO, cb), lambda j, i: (0, jnp.maximum(i * hb - 1, 0), j)),
                  pl.BlockSpec((None, FHALO, cb), lambda j, i: (0, nxt(i), j)),
                  pl.BlockSpec((None, tc, cb), lambda j, i: (1, i, j)),
                  pl.BlockSpec((None, FHALO, cb), lambda j, i: (1, nxt(i), j)),
                  pl.BlockSpec((kk, cb), lambda j, i: (0, j)),
                  pl.BlockSpec((1, cb), lambda j, i: (0, j))],
        out_specs=[pl.BlockSpec((2, tc, cb), lambda j, i: (0, i, j)),
                   pl.BlockSpec((SUBLANE, cb), lambda j, i: (0, j)),
                   pl.BlockSpec((1, cb), lambda j, i: (0, j))],
        out_shape=[jax.ShapeDtypeStruct((2, t, dff), BF16), jax.ShapeDtypeStruct((SUBLANE, dff), F32),
                   jax.ShapeDtypeStruct((1, dff), F32)],
        scratch_shapes=[pltpu.VMEM((tc + 2 * FHALO, cb), F32), pltpu.VMEM((te, cb), F32),
                        pltpu.VMEM((kk, SUBLANE, cb), F32), pltpu.VMEM((SUBLANE, cb), F32)],
        compiler_params=_params("parallel", "arbitrary"),
    )(dact, dact, hh3, hh3, hh3, hh3, hh3, fw, fb)


def _chunk_consts():
    r = lax.broadcasted_iota(jnp.int32, (CHUNK, CHUNK), 0)
    c = lax.broadcasted_iota(jnp.int32, (CHUNK, CHUNK), 1)
    blk = (r // SUB) * SUB
    tri = (c <= r).astype(F32)
    start = (c < blk).astype(F32)
    end = (c < blk + SUB).astype(F32)
    return jnp.concatenate([tri, start, end, jnp.ones((SUBLANE, CHUNK), F32)], axis=0)


def _gate_terms(q, fpre, lb):
    sf = _sigmoid(fpre)
    fg = lb + (1.0 - lb) * sf
    sq = _sigmoid(q)
    return sf, fg, 1.0 - fg, sq, q * sq


def _decays(g, consts):
    cs = _dot3(consts, g)
    b = cs[0:CHUNK]
    rs = cs[CHUNK:2 * CHUNK]
    re = cs[2 * CHUNK:3 * CHUNK]
    tot = cs[3 * CHUNK:3 * CHUNK + 1]
    return b, rs, re, tot


def _lower_bound(lb_ref):
    l0, l1 = lb_ref[0:1, :], lb_ref[1:2, :]
    mx = jnp.maximum(l0, l1)
    e0, e1 = jnp.exp(l0 - mx), jnp.exp(l1 - mx)
    return e0 / (e0 + e1)


def _scaled_keys(kt, rs, re, rowblk, i):
    scale = jnp.where(rowblk < i, jnp.exp(jnp.minimum(rs[SUB * i:SUB * i + 1, :] - re, 0.0)), 0.0)
    return kt * scale, scale


def _hgrn_fwd(p3, lb_logits, hg, cat, tb, hpb, rider=None):
    _, t, w = p3.shape
    nh = w // LANE
    nc = tb // CHUNK
    assert nh % hpb == 0

    def body(q_ref, f_ref, v_ref, og_ref, lb_ref, hg_ref, cat_in, cat_ref, o_ref, st_ref, state):
        del cat_in
        consts = _chunk_consts()
        lb_all = _lower_bound(lb_ref)
        rowblk = lax.broadcasted_iota(jnp.int32, (CHUNK, 1), 0) // SUB
        rowpos = lax.broadcasted_iota(jnp.int32, (CHUNK, 1), 0) % SUB

        @pl.when(pl.program_id(1) == 0)
        def _():
            state[...] = jnp.zeros_like(state)

        def chunk(c, carry):
            rows = pl.ds(pl.multiple_of(c * CHUNK, CHUNK), CHUNK)
            heads = range(hpb)
            sls = [slice(j * LANE, (j + 1) * LANE) for j in heads]
            v = [v_ref[rows, s] for s in sls]
            vb = [x.astype(BF16) for x in v]
            gates = [_gate_terms(q_ref[rows, s], f_ref[rows, s], lb_all[:, s]) for s in sls]
            fg = [g[1] for g in gates]
            kk = [g[2] for g in gates]
            qh = [g[4] for g in gates]
            dec = [_decays(jnp.log(x), consts) for x in fg]
            b = [x[0] for x in dec]
            rs = [x[1] for x in dec]
            re = [x[2] for x in dec]
            tot = [x[3] for x in dec]
            qt = [qh[j] * jnp.exp(b[j] - rs[j]) for j in heads]
            kt = [kk[j] * jnp.exp(re[j] - b[j]) for j in heads]
            st = [state[j] for j in heads]
            for j in heads:
                st_ref[j, c] = st[j]
            a = [jnp.zeros((CHUNK, CHUNK), F32) for _ in heads]
            for i in range(1, CHUNK // SUB):
                for j in heads:
                    ki, _ = _scaled_keys(kt[j], rs[j], re[j], rowblk, i)
                    a[j] = a[j] + _dot_nt(jnp.where(rowblk == i, qt[j], 0.0).astype(BF16), ki.astype(BF16))
            o = [_dot(a[j].astype(BF16), vb[j]) for j in heads]
            o = [o[j] + _dot_nt((qh[j] * jnp.exp(b[j])).astype(BF16), st[j].astype(BF16)) for j in heads]
            for j in heads:
                k_up = kk[j] * jnp.exp(tot[j] - b[j])
                state[j] = st[j] * jnp.exp(tot[j]) + _dot_tn(vb[j], k_up.astype(BF16))
            for j in heads:
                e = None
                for d in range(SUB):
                    if d == 0:
                        ks, vs, term = kk[j], v[j], qh[j] * kk[j]
                    else:
                        rf = pltpu.roll(fg[j], d - 1, 0) if d > 1 else fg[j]
                        e = rf if e is None else e * rf
                        ks, vs = pltpu.roll(kk[j], d, 0), pltpu.roll(v[j], d, 0)
                        term = jnp.where(rowpos >= d, qh[j] * ks * e, 0.0)
                    o[j] = o[j] + jnp.sum(term, axis=-1, keepdims=True) * vs
            for j in heads:
                og = og_ref[rows, sls[j]]
                o_ref[rows, sls[j]] = o[j]
                r = lax.rsqrt(jnp.mean(o[j] * o[j], axis=-1, keepdims=True) + RMS_EPS)
                cat_ref[rows, sls[j]] = (o[j] * r * hg_ref[:, sls[j]] * (og * _sigmoid(og))).astype(BF16)
            return carry

        lax.fori_loop(0, nc, chunk, 0)

    bw = hpb * LANE
    sec = lambda s: pl.BlockSpec((None, tb, bw), lambda h, i: (s, i, h))
    return _call(
        body, (p3, p3, p3, p3, lb_logits, hg, cat), name="hgrn_fwd", grid=(nh // hpb, t // tb),
        in_specs=[sec(2), sec(3), sec(4), sec(5),
                  pl.BlockSpec((2, bw), lambda h, i: (0, h)),
                  pl.BlockSpec((1, bw), lambda h, i: (0, h)), HBM],
        out_specs=[pl.BlockSpec((tb, bw), lambda h, i: (i, nh // hpb + h)),
                   pl.BlockSpec((tb, bw), lambda h, i: (i, h)),
                   pl.BlockSpec((hpb, nc, LANE, LANE), lambda h, i: (h, i, 0, 0))],
        out_shape=[jax.ShapeDtypeStruct(cat.shape, BF16), jax.ShapeDtypeStruct((t, w), F32),
                   jax.ShapeDtypeStruct((nh, t // CHUNK, LANE, LANE), F32)],
        scratch_shapes=[pltpu.VMEM((hpb, LANE, LANE), F32)], aliases={6: 0}, rider=rider)


def _hgrn_bwd(p3, lb_logits, hg, o_pre, states, dcat, dp3, tb, hpb, rider=None):
    n_sec, t, w = p3.shape
    nh = w // LANE
    assert nh % hpb == 0
    nc = tb // CHUNK
    nb = t // tb

    def body(q_ref, f_ref, v_ref, og_ref, lb_ref, hg_ref, o_ref, st_ref, dc_ref, dp_in,
             dp_ref, dlb_ref, dhg_ref, dstate, stash, lbacc, hgacc):
        del dp_in
        i, half = pl.program_id(1), pl.program_id(2)

        @pl.when(half == 1)
        def _():
            dp_ref[...] = stash[...]

        @pl.when(half == 0)
        def _():
            consts = _chunk_consts()
            rr = lax.broadcasted_iota(jnp.int32, (CHUNK, CHUNK), 0)
            cc = lax.broadcasted_iota(jnp.int32, (CHUNK, CHUNK), 1)
            upper = (cc >= rr).astype(F32)
            lb_all = _lower_bound(lb_ref)
            rowblk = lax.broadcasted_iota(jnp.int32, (CHUNK, 1), 0) // SUB
            rowpos = lax.broadcasted_iota(jnp.int32, (CHUNK, 1), 0) % SUB

            @pl.when(i == 0)
            def _():
                dstate[...] = jnp.zeros_like(dstate)
                lbacc[...] = jnp.zeros_like(lbacc)
                hgacc[...] = jnp.zeros_like(hgacc)

            def head(j, c, rows):
                sl = slice(j * LANE, (j + 1) * LANE)
                lb = lb_all[:, sl]
                hgv = hg_ref[:, sl]
                q = q_ref[rows, sl]
                v = v_ref[rows, sl]
                og = og_ref[rows, sl]
                o = o_ref[rows, sl]
                dcg = dc_ref[rows, sl]
                sf, fg, kk, sq, qh = _gate_terms(q, f_ref[rows, sl], lb)
                b, rs, re, tot = _decays(jnp.log(fg), consts)
                eq = jnp.exp(b - rs)
                ek = jnp.exp(re - b)
                qt = qh * eq
                kt = kk * ek
                e_in = jnp.exp(b)
                e_up = jnp.exp(tot - b)
                e_tot = jnp.exp(tot)
                q_in = (qh * e_in).astype(BF16)
                k_up = (kk * e_up).astype(BF16)
                vb = v.astype(BF16)
                st = st_ref[j, c]
                dst = dstate[j]
                dstb = dst.astype(BF16)
                yield

                sg = _sigmoid(og)
                r = lax.rsqrt(jnp.mean(o * o, axis=-1, keepdims=True) + RMS_EPS)
                ohat = o * r
                d_og = dcg * ohat * hgv * (sg * (1.0 + og * (1.0 - sg)))
                d_on = dcg * (og * sg)
                hgacc[:, sl] += jnp.sum((d_on * ohat).reshape(CHUNK // SUBLANE, SUBLANE, LANE), axis=0)
                d_oh = d_on * hgv
                do = r * (d_oh - ohat * jnp.mean(d_oh * ohat, axis=-1, keepdims=True))
                dob = do.astype(BF16)

                da = _dot_nt(dob, vb)
                yield
                a_off = jnp.zeros((CHUNK, CHUNK), F32)
                dqt = jnp.zeros((CHUNK, LANE), F32)
                dkt = jnp.zeros((CHUNK, LANE), F32)
                for blk in range(1, CHUNK // SUB):
                    ki, scale = _scaled_keys(kt, rs, re, rowblk, blk)
                    kib = ki.astype(BF16)
                    qib = jnp.where(rowblk == blk, qt, 0.0).astype(BF16)
                    dab = jnp.where(rowblk == blk, da, 0.0).astype(BF16)
                    a_off = a_off + _dot_nt(qib, kib)
                    dqt = dqt + _dot(dab, kib)
                    dkt = dkt + _dot_tn(dab, qib) * scale
                    yield
                dqh = dqt * eq
                dk = dkt * ek
                dv = _dot_tn(a_off.astype(BF16), dob)

                dqh = dqh + _dot(dob, st.astype(BF16)) * e_in
                dk = dk + _dot(vb, dstb) * e_up
                dv = dv + _dot_nt(k_up, dstb)
                st_end = st * e_tot + _dot_tn(vb, k_up)
                carry_g = jnp.sum(st_end * dst, axis=0, keepdims=True)
                dstate[j] = dst * e_tot + _dot_tn(dob, q_in)
                yield

                e = None
                for d in range(SUB):
                    if d == 0:
                        a_d = jnp.sum(qh * kk, axis=-1, keepdims=True)
                        da_d = jnp.sum(do * v, axis=-1, keepdims=True)
                        dqh = dqh + da_d * kk
                        dk = dk + da_d * qh
                        dv = dv + a_d * do
                        continue
                    rf = pltpu.roll(fg, d - 1, 0) if d > 1 else fg
                    e = rf if e is None else e * rf
                    em = jnp.where(rowpos >= d, e, 0.0)
                    ks, vs = pltpu.roll(kk, d, 0), pltpu.roll(v, d, 0)
                    a_d = jnp.sum(qh * ks * em, axis=-1, keepdims=True)
                    da_d = jnp.sum(do * vs, axis=-1, keepdims=True) * em
                    dqh = dqh + da_d * ks
                    dk = dk + pltpu.roll(da_d * qh, CHUNK - d, 0)
                    dv = dv + pltpu.roll(a_d * do, CHUNK - d, 0)
                yield

                dg = _dot3(upper, qh * dqh - kk * dk) + carry_g
                dfg = dg / fg - dk
                lbacc[:, sl] += jnp.sum((dfg * (1.0 - sf)).reshape(CHUNK // SUBLANE, SUBLANE, LANE), axis=0)
                dp_ref[0, rows, sl] = (dqh * (sq * (1.0 + q * (1.0 - sq)))).astype(BF16)
                dp_ref[1, rows, sl] = (dfg * (1.0 - lb) * sf * (1.0 - sf)).astype(BF16)
                stash[0, rows, sl] = dv.astype(BF16)
                stash[1, rows, sl] = d_og.astype(BF16)

            def chunk(cr, carry):
                c = nc - 1 - cr
                rows = pl.ds(pl.multiple_of(c * CHUNK, CHUNK), CHUNK)
                running = [head(j, c, rows) for j in range(hpb)]
                while running:
                    running = [g for g in running if next(g, StopIteration) is not StopIteration]
                return carry

            lax.fori_loop(0, nc, chunk, 0)

            @pl.when(i == nb - 1)
            def _():
                dlb_ref[...] = jnp.sum(lbacc[...], axis=0, keepdims=True)
                dhg_ref[...] = jnp.sum(hgacc[...], axis=0, keepdims=True)

    rev = lambda i: nb - 1 - i
    bw = hpb * LANE
    sec = lambda s: pl.BlockSpec((None, tb, bw), lambda h, i, z: (s, rev(i), h))
    return _call(
        body, (p3, p3, p3, p3, lb_logits, hg, o_pre, states, dcat, dp3), name="hgrn_bwd", grid=(nh // hpb, nb, 2),
        in_specs=[sec(2), sec(3), sec(4), sec(5),
                  pl.BlockSpec((2, bw), lambda h, i, z: (0, h)),
                  pl.BlockSpec((1, bw), lambda h, i, z: (0, h)),
                  pl.BlockSpec((tb, bw), lambda h, i, z: (rev(i), h)),
                  pl.BlockSpec((hpb, nc, LANE, LANE), lambda h, i, z: (h, rev(i), 0, 0)),
                  pl.BlockSpec((tb, bw), lambda h, i, z: (rev(i), nh // hpb + h)), HBM],
        out_specs=[pl.BlockSpec((2, tb, bw), lambda h, i, z: (1 + z, rev(i), h)),
                   pl.BlockSpec((1, bw), lambda h, i, z: (0, h)),
                   pl.BlockSpec((1, bw), lambda h, i, z: (0, h))],
        out_shape=[jax.ShapeDtypeStruct((n_sec, t, w), BF16), jax.ShapeDtypeStruct((1, w), F32),
                   jax.ShapeDtypeStruct((1, w), F32)],
        scratch_shapes=[pltpu.VMEM((hpb, LANE, LANE), F32), pltpu.VMEM((2, tb, bw), BF16),
                        pltpu.VMEM((SUBLANE, bw), F32), pltpu.VMEM((SUBLANE, bw), F32)],
        aliases={9: 0}, rider=rider)


def _place():
    x, y, c = lax.axis_index("x"), lax.axis_index("y"), lax.axis_index("c")
    chips = [(1 - x, y), (x, 1 - y), (1 - x, 1 - y)]
    return x, y, c, chips


def _rows(buf, px, py, pc, part=None):
    half = buf.shape[1] // 2
    if part is None:
        return buf.at[2 * px + py, pl.ds(pc * half, half)]
    lo, hi, n = part
    piece = half // n
    return buf.at[2 * px + py, pl.ds(pc * half + lo * piece, (hi - lo) * piece)]


def _rcopy(src, dst, send, recv, idx, to):
    return pltpu.make_async_remote_copy(src_ref=src, dst_ref=dst, send_sem=send.at[idx], recv_sem=recv.at[idx],
                                        device_id=to, device_id_type=MESH)


def _same(bufs):
    return [jax.ShapeDtypeStruct(b.shape, b.dtype) for b in bufs]


def _ride_gather_ici(bufs, parts=None):
    n = len(bufs)
    parts = parts or [None] * n

    def start(rin, rout, send, recv):
        x, y, c, chips = _place()
        for k in range(n):
            mine = _rows(rout[k], x, y, c, parts[k])
            for j, chip in enumerate(chips):
                _rcopy(mine, mine, send, recv, 3 * k + j, (*chip, c)).start()

    def finish(rin, rout, send, recv):
        x, y, c, chips = _place()
        for k in range(n):
            for j, chip in enumerate(chips):
                theirs = _rows(rout[k], *chip, c, parts[k])
                _rcopy(theirs, theirs, send, recv, 3 * k + j, (x, y, c)).wait_recv()
        for k in range(n):
            mine = _rows(rout[k], x, y, c, parts[k])
            for j in range(3):
                _rcopy(mine, mine, send, recv, 3 * k + j, (x, y, c)).wait_send()

    return _Rider(bufs, _same(bufs), {k: k for k in range(n)}, 3 * n, start, finish)


class _SemView:
    def __init__(self, ref, base):
        self.ref, self.base = ref, base

    @property
    def at(self):
        return self

    def __getitem__(self, idx):
        return self.ref.at[idx + self.base]


def _ride_both(a, b):
    nai, nao = len(a.ins), len(a.outs)

    def start(rin, rout, send, recv):
        a.start(rin[:nai], rout[:nao], send, recv)
        b.start(rin[nai:], rout[nao:], _SemView(send, a.n_sems), _SemView(recv, a.n_sems))

    def finish(rin, rout, send, recv):
        a.finish(rin[:nai], rout[:nao], send, recv)
        b.finish(rin[nai:], rout[nao:], _SemView(send, a.n_sems), _SemView(recv, a.n_sems))

    aliases = dict(a.aliases)
    aliases.update({nai + ri: nao + ro for ri, ro in b.aliases.items()})
    return _Rider(a.ins + b.ins, a.outs + b.outs, aliases, a.n_sems + b.n_sems, start, finish)


def _ride_gather_d2d(bufs):
    n = len(bufs)

    def start(rin, rout, send, recv):
        x, y, c, chips = _place()
        for k in range(n):
            for j, chip in enumerate(chips):
                got = _rows(rout[k], *chip, c)
                _rcopy(got, got, send, recv, 3 * k + j, (x, y, 1 - c)).start()

    def finish(rin, rout, send, recv):
        x, y, c, chips = _place()
        for k in range(n):
            for j, chip in enumerate(chips):
                theirs = _rows(rout[k], *chip, 1 - c)
                _rcopy(theirs, theirs, send, recv, 3 * k + j, (x, y, c)).wait_recv()
        for k in range(n):
            for j, chip in enumerate(chips):
                got = _rows(rout[k], *chip, c)
                _rcopy(got, got, send, recv, 3 * k + j, (x, y, c)).wait_send()

    return _Rider(bufs, _same(bufs), {k: k for k in range(n)}, 3 * n, start, finish)


def _ride_swap(grads):
    n = len(grads)

    def copy(k, rin, rout, send, recv):
        x, y, c, _ = _place()
        half = rin[k].shape[1] // 2
        return _rcopy(rin[k].at[:, pl.ds((1 - c) * half, half)], rout[k], send, recv, k, (x, y, 1 - c))

    def start(rin, rout, send, recv):
        for k in range(n):
            copy(k, rin, rout, send, recv).start()

    def finish(rin, rout, send, recv):
        for k in range(n):
            copy(k, rin, rout, send, recv).wait()

    outs = [jax.ShapeDtypeStruct((g.shape[0], g.shape[1] // 2, g.shape[2]), g.dtype) for g in grads]
    return _Rider(grads, outs, {}, n, start, finish)


def _ride_send_partials(parts):
    n = len(parts)

    def copies(rin, rout, send, recv):
        x, y, c, chips = _place()
        return [_rcopy(rin[k].at[2 * px + py], rout[k].at[j], send, recv, 3 * k + j, (px, py, c))
                for k in range(n) for j, (px, py) in enumerate(chips)]

    def start(rin, rout, send, recv):
        for cp in copies(rin, rout, send, recv):
            cp.start()

    def finish(rin, rout, send, recv):
        for cp in copies(rin, rout, send, recv):
            cp.wait()

    outs = [jax.ShapeDtypeStruct((3,) + p.shape[1:], p.dtype) for p in parts]
    return _Rider(parts, outs, {}, 3 * n, start, finish)


def _ride_join(bufs):
    n = len(bufs)

    def half_of(buf, pc):
        half = buf.shape[0] // 2
        return buf.at[pl.ds(pc * half, half)]

    def start(rin, rout, send, recv):
        x, y, c, _ = _place()
        for k in range(n):
            mine = half_of(rout[k], c)
            _rcopy(mine, mine, send, recv, k, (x, y, 1 - c)).start()

    def finish(rin, rout, send, recv):
        x, y, c, _ = _place()
        for k in range(n):
            mine, theirs = half_of(rout[k], c), half_of(rout[k], 1 - c)
            _rcopy(mine, mine, send, recv, k, (x, y, c)).wait_send()
            _rcopy(theirs, theirs, send, recv, k, (x, y, c)).wait_recv()

    return _Rider(bufs, _same(bufs), {k: k for k in range(n)}, n, start, finish)


def _run(name, rider):
    def body(*refs):
        nri, nro = len(rider.ins), len(rider.outs)
        rin, rout = refs[:nri], refs[nri:nri + nro]
        send, recv = refs[nri + nro:]
        rider.start(rin, rout, send, recv)
        rider.finish(rin, rout, send, recv)

    return pl.pallas_call(
        body, name=name, in_specs=[HBM] * len(rider.ins), out_specs=[HBM] * len(rider.outs), out_shape=rider.outs,
        scratch_shapes=[pltpu.SemaphoreType.DMA((rider.n_sems,)), pltpu.SemaphoreType.DMA((rider.n_sems,))],
        input_output_aliases=rider.aliases,
    )(*rider.ins)


def _add_halves(name, g, other, c_idx):
    s, r, cols = g.shape
    half = r // 2
    tr = _div_tile(half, 16, 512)
    nb = half // tr

    def body(c_ref, g_ref, o_ref, q_ref):
        del c_ref
        q_ref[...] = (g_ref[...] + o_ref[...]).astype(BF16)

    return pl.pallas_call(
        body, name=name,
        grid_spec=pltpu.PrefetchScalarGridSpec(
            num_scalar_prefetch=1, grid=(s, nb),
            in_specs=[pl.BlockSpec((None, tr, cols), lambda k, i, c: (k, c[0] * nb + i, 0)),
                      pl.BlockSpec((None, tr, cols), lambda k, i, c: (k, i, 0))],
            out_specs=pl.BlockSpec((None, tr, cols), lambda k, i, c: (k, i, 0))),
        out_shape=jax.ShapeDtypeStruct((s, half, cols), BF16),
        compiler_params=_params("parallel", "parallel"),
    )(c_idx, g, other)


def _sum_partials(name, part, arrived, place_idx):
    _, half, cols = part.shape
    tr = _div_tile(half, 16, 512)
    nb = half // tr

    def body(s_ref, p_ref, a_ref, o_ref):
        del s_ref
        o_ref[...] = ((p_ref[...].astype(F32) + a_ref[0].astype(F32)) + a_ref[1].astype(F32)) + a_ref[2].astype(F32)

    return pl.pallas_call(
        body, name=name,
        grid_spec=pltpu.PrefetchScalarGridSpec(
            num_scalar_prefetch=1, grid=(nb,),
            in_specs=[pl.BlockSpec((None, tr, cols), lambda i, s: (s[0], i, 0)),
                      pl.BlockSpec((3, tr, cols), lambda i, s: (0, i, 0))],
            out_specs=pl.BlockSpec((tr, cols), lambda i, s: (s[1] * nb + i, 0))),
        out_shape=jax.ShapeDtypeStruct((2 * half, cols), F32),
        compiler_params=_params("parallel"),
    )(place_idx, part, arrived)


def _small_allreduce(wide_rows, ffn_rows, w, dff, n_wide, n_ffn):
    n_in = len(wide_rows) + len(ffn_rows)

    def body(*refs):
        ins = refs[:n_in]
        s1_ref, s2_ref, r1, r2, p1, p2, send, recv = refs[n_in:]
        x, y, c, _ = _place()
        me = 4 * x + 2 * y + c
        p1[...] = jnp.zeros_like(p1)
        p2[...] = jnp.zeros_like(p2)
        row = 0
        for ref, (_, r, m) in zip(ins, wide_rows):
            if m == 1 and r % SUBLANE == 0 and row % SUBLANE == 0:
                p1[row:row + r, :] = ref[...]
                row += r
                continue
            for rr in range(r):
                for mm in range(m):
                    p1[row:row + 1, :] = ref[rr:rr + 1, mm * w:(mm + 1) * w]
                    row += 1
        row = 0
        for ref, arr in zip(ins[len(wide_rows):], ffn_rows):
            r = arr.shape[0]
            p2[row:row + r, :] = ref[...]
            row += r
        r1[me] = p1[...]
        r2[me] = p2[...]
        cps = []
        for mask in range(1, 8):
            peer = (x ^ (mask >> 2), y ^ ((mask >> 1) & 1), c ^ (mask & 1))
            for a, (src, dst) in enumerate(((p1, r1), (p2, r2))):
                cp = pltpu.make_async_remote_copy(
                    src_ref=src, dst_ref=dst.at[me], send_sem=send.at[a, mask - 1], recv_sem=recv.at[a, mask - 1],
                    device_id=peer, device_id_type=MESH)
                cp.start()
                cps.append(cp)
        for cp in cps:
            cp.wait()
        t1, t2 = r1[0], r2[0]
        for d in range(1, 8):
            t1 = t1 + r1[d]
            t2 = t2 + r2[d]
        s1_ref[...] = t1
        s2_ref[...] = t2

    ins = [a for a, _, _ in wide_rows] + list(ffn_rows)
    return pl.pallas_call(
        body, name="small_allreduce", in_specs=[VMEM_FULL] * n_in, out_specs=[VMEM_FULL, VMEM_FULL],
        out_shape=[jax.ShapeDtypeStruct((n_wide, w), F32), jax.ShapeDtypeStruct((n_ffn, dff), F32)],
        scratch_shapes=[pltpu.VMEM((8, n_wide, w), F32), pltpu.VMEM((8, n_ffn, dff), F32),
                        pltpu.VMEM((n_wide, w), F32), pltpu.VMEM((n_ffn, dff), F32),
                        pltpu.SemaphoreType.DMA((2, 7)), pltpu.SemaphoreType.DMA((2, 7))],
        compiler_params=pltpu.CompilerParams(vmem_limit_bytes=VMEM_LIMIT),
    )(*ins)


def _adamw(w, g, m, v):
    m2 = ADAM_B1 * m + (1.0 - ADAM_B1) * g
    v2 = ADAM_B2 * v + (1.0 - ADAM_B2) * (g * g)
    m_hat = m2 / (1.0 - ADAM_B1 ** ADAM_STEP)
    v_hat = v2 / (1.0 - ADAM_B2 ** ADAM_STEP)
    delta = -ADAM_LR * (m_hat / (jnp.sqrt(v_hat) + ADAM_EPS) + ADAM_WD * w)
    return delta, m2, v2


def _adam_big(name, w, g, m, v):
    r, c = w.shape
    tr = 128 if r % 128 == 0 else r

    def body(w_ref, g_ref, m_ref, v_ref, d_ref, m2_ref, v2_ref):
        d_ref[...], m2_ref[...], v2_ref[...] = _adamw(w_ref[...], g_ref[...], m_ref[...], v_ref[...])

    blk = pl.BlockSpec((tr, c), lambda i: (i, 0))
    return pl.pallas_call(
        body, name=name, grid=(r // tr,), in_specs=[blk] * 4, out_specs=[blk] * 3,
        out_shape=[jax.ShapeDtypeStruct((r, c), F32)] * 3,
        compiler_params=_params("parallel"),
    )(w, g, m, v)


def _adam_small(s1, s2, cw_g, fw_g, lb_logits, triples, layout, w):
    n = len(triples)

    def body(*refs):
        s1_ref, s2_ref, cw_ref, fw_ref, lbl_ref = refs[:5]
        prm = refs[5:5 + 3 * n]
        outs = refs[5 + 3 * n:]
        for p, lay in enumerate(layout):
            w_ref, m_ref, v_ref = prm[3 * p:3 * p + 3]
            g_ref, d_ref, m2_ref, v2_ref = outs[4 * p:4 * p + 4]
            if lay[0] == "wide":
                _, row, r, pieces = lay
                for rr in range(r):
                    for mm in range(pieces):
                        g_ref[rr:rr + 1, mm * w:(mm + 1) * w] = s1_ref[row:row + 1, :]
                        row += 1
            elif lay[0] == "ffn":
                _, row, r = lay
                g_ref[...] = s2_ref[row:row + r, :]
            elif lay[0] == "cw":
                g_ref[...] = cw_ref[0:g_ref.shape[0], :]
            elif lay[0] == "fw":
                g_ref[...] = fw_ref[0:g_ref.shape[0], :]
            else:
                s0 = _lower_bound(lbl_ref)
                d0 = s1_ref[lay[1]:lay[1] + 1, :] * s0 * (1.0 - s0)
                g_ref[0:1, :] = d0
                g_ref[1:2, :] = -d0
            d_ref[...], m2_ref[...], v2_ref[...] = _adamw(w_ref[...], g_ref[...], m_ref[...], v_ref[...])

    flat = [a for tr in triples for a in tr]
    shapes = []
    for tr in triples:
        shapes.extend([jax.ShapeDtypeStruct(tr[0].shape, F32)] * 4)
    return pl.pallas_call(
        body, name="adam_small", in_specs=[VMEM_FULL] * (5 + 3 * n), out_specs=[VMEM_FULL] * (4 * n),
        out_shape=shapes, compiler_params=pltpu.CompilerParams(vmem_limit_bytes=VMEM_LIMIT),
    )(s1, s2, cw_g, fw_g, lb_logits, *flat)


def _row_tile(t):
    return 512 if t % 512 == 0 and t >= 2048 else 128


def kernel(x, emb_ln_g, emb_ln_b, w_in, conv_w, conv_b, conv_norm_g, conv_norm_b, lb_logits, hgrn_norm_g, w_out, ln1_g, ln1_b, w_ffn_up, ffn_conv_w, ffn_conv_b, w_ffn_down, ln2_g, ln2_b, loss_target, m_emb_ln_g, m_emb_ln_b, m_w_in, m_conv_w, m_conv_b, m_conv_norm_g, m_conv_norm_b, m_lb_logits, m_hgrn_norm_g, m_w_out, m_ln1_g, m_ln1_b, m_w_ffn_up, m_ffn_conv_w, m_ffn_conv_b, m_w_ffn_down, m_ln2_g, m_ln2_b, v_emb_ln_g, v_emb_ln_b, v_w_in, v_conv_w, v_conv_b, v_conv_norm_g, v_conv_norm_b, v_lb_logits, v_hgrn_norm_g, v_w_out, v_ln1_g, v_ln1_b, v_w_ffn_up, v_ffn_conv_w, v_ffn_conv_b, v_w_ffn_down, v_ln2_g, v_ln2_b):
    depth = w_in.shape[0]
    assert depth == 1 and x.shape[0] == 1
    alpha = (2.0 * depth) ** 0.25
    t, d = x.shape[1], x.shape[2]
    w = d // 2
    dff = ffn_conv_b.shape[1]
    kc = conv_w.shape[1]
    assert w % (2 * LANE) == 0 and dff % (4 * LANE) == 0 and t % 128 == 0
    tm = _row_tile(t)
    tm2 = tm // 2
    tmm = 1024 if t % 1024 == 0 and t >= 2048 else tm
    cb = 2 * LANE
    cbf = 4 * LANE
    tb = tm
    nh = w // LANE
    hpb = 4 if nh % 4 == 0 else 2

    xi = lax.axis_index("x")
    yi = lax.axis_index("y")
    ci = lax.axis_index("c")
    chip = 2 * xi + yi
    c_idx = jnp.reshape(ci, (1,)).astype(jnp.int32)
    chip_idx = jnp.reshape(chip, (1,)).astype(jnp.int32)
    place_idx = jnp.stack([chip, ci]).astype(jnp.int32)

    x2 = x[0]
    tgt = loss_target[0]
    g0, b0 = emb_ln_g.reshape(1, d), emb_ln_b.reshape(1, d)
    w_in2, w_out2, w_up2, w_dn2 = w_in[0], w_out[0], w_ffn_up[0], w_ffn_down[0]
    cw2, fw2 = conv_w[0], ffn_conv_w[0]

    b_in = _place_shard(w_in2, "place_w_in", chip_idx, BF16)
    b_out = _place_shard(w_out2, "place_w_out", chip_idx, BF16)
    b_up = _place_shard(w_up2, "place_w_up", chip_idx, BF16)
    b_dn = _place_shard(w_dn2, "place_w_down", chip_idx, BF16)
    b_cw = _place_shard(_pad_rows(cw2), "place_conv_w", chip_idx, F32)
    b_fw = _place_shard(_pad_rows(fw2), "place_ffn_conv_w", chip_idx, F32)
    first = _run("gather_first_ici", _ride_gather_ici([b_in, b_cw, b_fw]))
    w_in3, cw_full3, fw_full3 = _run("gather_first_d2d", _ride_gather_d2d(first))
    cw_full = _unshard_cols(cw_full3)[:kc]
    fw_full = _unshard_cols(fw_full3)[:fw2.shape[0]]

    h0b = _ln0(x2, g0, b0, tm)
    p3, (b_out, b_up) = _proj("in_proj", h0b, w_in3, 6, tmm, w // 2,
                              rider=_ride_gather_ici([b_out, b_up], [None, (0, 1, 4)]))
    (cat, u1), (w_out3, b_up) = _conv_fwd(
        p3, cw_full, conv_b, conv_norm_g, conv_norm_b, tm2, cb,
        rider=_ride_both(_ride_gather_d2d([b_out]), _ride_gather_ici([b_up], [(1, 2, 4)])))
    w_out_full = w_out3.reshape(d, d)
    (cat, o_pre, states), got = _hgrn_fwd(p3, lb_logits, hgrn_norm_g, cat, tb, hpb,
                                          rider=_ride_gather_ici([b_up], [(2, 4, 4)]))
    (xhat1, h1b, rstd1), (w_up3,) = _mix_ln1(cat, w_out_full, x2, g0, b0, ln1_g, ln1_b, alpha, tm2,
                                             rider=_ride_gather_d2d(got))
    hh3, got = _proj("ffn_up", h1b, w_up3, 2, tmm, dff // 4, rider=_ride_gather_ici([b_dn]))
    act, (w_dn3,) = _ffn_act_fwd(hh3, fw_full, ffn_conv_b, tm, cbf, rider=_ride_gather_d2d(got))
    ks = dff // N_CHIPS
    ffn = _wgrad("ffn_down", act, w_dn3, (t, d), (t // tmm, 1, N_CHIPS),
                 pl.BlockSpec((tmm, ks), lambda i, j, k: (i, k)),
                 pl.BlockSpec((None, ks, d), lambda i, j, k: (k, 0, 0)),
                 pl.BlockSpec((tmm, d), lambda i, j, k: (i, 0)), dot=_dot)
    dz2, dz2b, dg2, db2, loss_row = _ln2_loss(ffn, xhat1, tgt, ln1_g, ln1_b, ln2_g, ln2_b, alpha, tm2)

    dact = _proj_t("ffn_down_t", dz2b, w_dn3.reshape(dff, d), tmm, ks)
    dhh3, dfw, dfb = _ffn_act_bwd(dact, hh3, fw_full, ffn_conv_b, tm, cbf)
    tt = tmm
    d_w_dn = _wgrad("wgrad_down", act, dz2b, (N_CHIPS, ks, d), (N_CHIPS, 2, t // tt),
                    pl.BlockSpec((tt, ks), lambda s, j, k: (k, s)),
                    pl.BlockSpec((tt, d // 2), lambda s, j, k: (k, j)),
                    pl.BlockSpec((None, ks, d // 2), lambda s, j, k: (s, 0, j)))
    wu = 2 * dff // N_CHIPS
    tnu = wu // 2
    per_sec_u = dff // tnu
    pre1, (arr_dn,) = _wgrad(
        "up_t", dhh3, w_up3, (t, d), (t // tmm, 1, 2 * N_CHIPS),
        pl.BlockSpec((None, tmm, tnu), lambda i, j, k: (k // per_sec_u, i, k % per_sec_u)),
        pl.BlockSpec((None, d, tnu), lambda i, j, k: (k // 2, 0, k % 2)),
        pl.BlockSpec((tmm, d), lambda i, j, k: (i, 0)), dot=_dot_nt, rider=_ride_swap([d_w_dn]))
    dz1, dz1b, dg1, db1 = _ln1_bwd(pre1, dz2, xhat1, rstd1, ln1_g, alpha, tm2)
    part_dn = _add_halves("add_halves_w_down", d_w_dn, arr_dn, c_idx)
    d_w_up, (land_dn,) = _wgrad(
        "wgrad_up", h1b, dhh3, (N_CHIPS, d, wu), (N_CHIPS, 2, 2, t // tt),
        pl.BlockSpec((tt, d // 2), lambda s, r, j, k: (k, r)),
        pl.BlockSpec((None, tt, tnu), lambda s, r, j, k: ((2 * s + j) // per_sec_u, k, (2 * s + j) % per_sec_u)),
        pl.BlockSpec((None, d // 2, tnu), lambda s, r, j, k: (s, r, j)), rider=_ride_send_partials([part_dn]))
    dcat, (arr_up,) = _proj_t("out_proj_t", dz1b, w_out_full, tmm, d // 2, rider=_ride_swap([d_w_up]))
    part_up = _add_halves("add_halves_w_up", d_w_up, arr_up, c_idx)
    d_w_out = _wgrad("wgrad_out", cat, dz1b, (d, d), (2, 2, t // tt),
                     pl.BlockSpec((tt, d // 2), lambda r, j, k: (k, r)),
                     pl.BlockSpec((tt, d // 2), lambda r, j, k: (k, j)),
                     pl.BlockSpec((d // 2, d // 2), lambda r, j, k: (r, j))).reshape(N_CHIPS, d // N_CHIPS, d)
    du1, dcng, dcnb = _conv_norm_bwd(dcat, u1, conv_norm_g, conv_norm_b, tm)
    (dp3, dcw, dcb), (arr_out,) = _conv_bwd(du1, p3, cw_full, tm2, cb, rider=_ride_swap([d_w_out]))
    part_out = _add_halves("add_halves_w_out", d_w_out, arr_out, c_idx)
    (dp3, dlb, dhg), (land_up, land_out) = _hgrn_bwd(p3, lb_logits, hgrn_norm_g, o_pre, states, dcat, dp3, tb, hpb,
                                                    rider=_ride_send_partials([part_up, part_out]))
    wi = 6 * w // N_CHIPS
    tni = w // 2
    d_w_in = _wgrad("wgrad_in", h0b, dp3, (N_CHIPS, d, wi), (N_CHIPS, 2, wi // tni, t // tt),
                    pl.BlockSpec((tt, d // 2), lambda s, r, j, k: (k, r)),
                    pl.BlockSpec((None, tt, tni), lambda s, r, j, k: (((wi // tni) * s + j) // 2, k, ((wi // tni) * s + j) % 2)),
                    pl.BlockSpec((None, d // 2, tni), lambda s, r, j, k: (s, r, j)))
    (arr_in,) = _run("swap_w_in", _ride_swap([d_w_in]))
    part_in = _add_halves("add_halves_w_in", d_w_in, arr_in, c_idx)
    per_shard_i = wi // tni
    pre0, (land_in,) = _wgrad(
        "in_t", dp3, w_in3, (t, d), (t // tmm, 1, 2 * dp3.shape[0]),
        pl.BlockSpec((None, tmm, tni), lambda i, j, k: (k // 2, i, k % 2)),
        pl.BlockSpec((None, d, tni), lambda i, j, k: (k // per_shard_i, 0, k % per_shard_i)),
        pl.BlockSpec((tmm, d), lambda i, j, k: (i, 0)), dot=_dot_nt, rider=_ride_send_partials([part_in]))
    dx, dg0, db0 = _ln0_bwd(pre0, dz1, x2, g0, alpha, tm2)
    halves = [_sum_partials("sum_partials_" + nm, p, a, place_idx)
              for nm, p, a in (("w_in", part_in, land_in), ("w_out", part_out, land_out),
                               ("w_up", part_up, land_up), ("w_down", part_dn, land_dn))]
    g_w_in, g_w_out, g_w_up, g_w_dn = _run("join_halves", _ride_join(halves))

    kpad = dcw.shape[0]
    wide = [(dcw, kpad, 1), (dg0, 1, 2), (db0, 1, 2), (dg1, 1, 2), (db1, 1, 2), (dg2, 1, 2), (db2, 1, 2),
            (dcb, 1, 1), (dcng, 1, 1), (dcnb, 1, 1), (dlb, 1, 1), (dhg, 1, 1)]
    n_wide = sum(r * m for _, r, m in wide)
    n_wide_pad = -(-n_wide // SUBLANE) * SUBLANE
    s1, s2 = _small_allreduce(wide, [dfw, dfb], w, dff, n_wide_pad, 2 * SUBLANE)
    cw_g = lax.dynamic_slice_in_dim(s1[0:kpad], chip * (w // N_CHIPS), w // N_CHIPS, axis=1)
    fw_g = lax.dynamic_slice_in_dim(s2[0:SUBLANE], chip * (dff // N_CHIPS), dff // N_CHIPS, axis=1)

    small = [
        (g0, m_emb_ln_g.reshape(1, d), v_emb_ln_g.reshape(1, d)), (b0, m_emb_ln_b.reshape(1, d), v_emb_ln_b.reshape(1, d)),
        (cw2, m_conv_w[0], v_conv_w[0]), (conv_b, m_conv_b, v_conv_b),
        (conv_norm_g, m_conv_norm_g, v_conv_norm_g), (conv_norm_b, m_conv_norm_b, v_conv_norm_b),
        (lb_logits, m_lb_logits, v_lb_logits), (hgrn_norm_g, m_hgrn_norm_g, v_hgrn_norm_g),
        (ln1_g, m_ln1_g, v_ln1_g), (ln1_b, m_ln1_b, v_ln1_b),
        (fw2, m_ffn_conv_w[0], v_ffn_conv_w[0]), (ffn_conv_b, m_ffn_conv_b, v_ffn_conv_b),
        (ln2_g, m_ln2_g, v_ln2_g), (ln2_b, m_ln2_b, v_ln2_b),
    ]
    r0 = kpad
    layout = [("wide", r0, 1, 2), ("wide", r0 + 2, 1, 2), ("cw",), ("wide", r0 + 12, 1, 1), ("wide", r0 + 13, 1, 1),
              ("wide", r0 + 14, 1, 1), ("lb", r0 + 15), ("wide", r0 + 16, 1, 1), ("wide", r0 + 4, 1, 2),
              ("wide", r0 + 6, 1, 2), ("fw",), ("ffn", SUBLANE, 1), ("wide", r0 + 8, 1, 2), ("wide", r0 + 10, 1, 2)]
    so = _adam_small(s1, s2, cw_g, fw_g, lb_logits, small, layout, w)
    sm = {nm: so[4 * i:4 * i + 4] for i, nm in enumerate(
        ["emb_ln_g", "emb_ln_b", "conv_w", "conv_b", "conv_norm_g", "conv_norm_b", "lb_logits", "hgrn_norm_g",
         "ln1_g", "ln1_b", "ffn_conv_w", "ffn_conv_b", "ln2_g", "ln2_b"])}
    bigs = {}
    for nm, wt, g, m, v in (("w_in", w_in2, g_w_in, m_w_in[0], v_w_in[0]), ("w_out", w_out2, g_w_out, m_w_out[0], v_w_out[0]),
                            ("w_ffn_up", w_up2, g_w_up, m_w_ffn_up[0], v_w_ffn_up[0]),
                            ("w_ffn_down", w_dn2, g_w_dn, m_w_ffn_down[0], v_w_ffn_down[0])):
        bigs[nm] = (g,) + tuple(_adam_big("adam_" + nm, wt, g, m, v))

    loss = lax.psum(loss_row[0, 0], ("x", "y", "c"))

    order = ["emb_ln_g", "emb_ln_b", "w_in", "conv_w", "conv_b", "conv_norm_g", "conv_norm_b", "lb_logits",
             "hgrn_norm_g", "w_out", "ln1_g", "ln1_b", "w_ffn_up", "ffn_conv_w", "ffn_conv_b", "w_ffn_down",
             "ln2_g", "ln2_b"]
    shapes = dict(emb_ln_g=emb_ln_g.shape, emb_ln_b=emb_ln_b.shape, w_in=w_in.shape, conv_w=conv_w.shape,
                  w_out=w_out.shape, w_ffn_up=w_ffn_up.shape, ffn_conv_w=ffn_conv_w.shape, w_ffn_down=w_ffn_down.shape)
    outs = [loss, dx.reshape(x.shape)]
    for which in range(4):
        for nm in order:
            a = bigs[nm][which] if nm in bigs else sm[nm][which]
            outs.append(a.reshape(shapes[nm]) if nm in shapes else a)
    return tuple(outs)


def _pad_rows(a):
    k = a.shape[0]
    kp = -(-k // 16) * 16
    return jnp.pad(a, ((0, kp - k), (0, 0)))


def _unshard_cols(a3):
    s, k, c = a3.shape
    return jnp.transpose(a3, (1, 0, 2)).reshape(k, s * c)
```

```python
import functools

import jax
import jax.numpy as jnp
from jax import lax
from jax.experimental import pallas as pl
from jax.experimental.pallas import tpu as pltpu

F32 = jnp.float32
BF16 = jnp.bfloat16

LN_EPS = 1e-5
RMS_EPS = 1e-6
LANE = 128
SUBLANE = 8
CHUNK = 64
SUB = 8
HALO = 32
FHALO = 8
ROWS = 64
N_CHIPS = 4
VMEM_LIMIT = 56 << 20
NEG_BIG = -1e30

ADAM_LR = 0.001
ADAM_B1 = 0.9
ADAM_B2 = 0.999
ADAM_EPS = 1e-08
ADAM_WD = 0.01
ADAM_STEP = 10

MESH = pl.DeviceIdType.MESH
HBM = pl.BlockSpec(memory_space=pl.ANY)
VMEM_FULL = pl.BlockSpec(memory_space=pltpu.VMEM)


def _params(*sem):
    return pltpu.CompilerParams(dimension_semantics=sem, vmem_limit_bytes=VMEM_LIMIT)


class _Rider:
    def __init__(self, ins, outs, aliases, n_sems, start, finish):
        self.ins, self.outs, self.aliases = list(ins), list(outs), dict(aliases)
        self.n_sems, self.start, self.finish = n_sems, start, finish


def _call(body, args, *, name, grid, in_specs, out_specs, out_shape, scratch_shapes=(), aliases=None, rider=None):
    many = isinstance(out_shape, (list, tuple))
    shapes = list(out_shape) if many else [out_shape]
    ospecs = list(out_specs) if many else [out_specs]
    aliases = dict(aliases or {})
    sem = ("arbitrary",) * len(grid)
    if rider is None:
        res = pl.pallas_call(
            body, name=name, grid=grid, in_specs=list(in_specs), out_specs=ospecs, out_shape=shapes,
            scratch_shapes=list(scratch_shapes), input_output_aliases=aliases, compiler_params=_params(*sem))(*args)
        return res if many else res[0]
    n_in, n_out, n_scr = len(args), len(shapes), len(scratch_shapes)
    nri, nro = len(rider.ins), len(rider.outs)

    def wrapped(*refs):
        ins, rin = refs[:n_in], refs[n_in:n_in + nri]
        o0 = n_in + nri
        outs, rout = refs[o0:o0 + n_out], refs[o0 + n_out:o0 + n_out + nro]
        s0 = o0 + n_out + nro
        scr, (send, recv) = refs[s0:s0 + n_scr], refs[s0 + n_scr:]
        ids = [pl.program_id(a) for a in range(len(grid))]
        first = functools.reduce(jnp.logical_and, [i == 0 for i in ids])
        last = functools.reduce(jnp.logical_and, [i == g - 1 for i, g in zip(ids, grid)])

        @pl.when(first)
        def _():
            rider.start(rin, rout, send, recv)

        body(*ins, *outs, *scr)

        @pl.when(last)
        def _():
            rider.finish(rin, rout, send, recv)

    for ri, ro in rider.aliases.items():
        aliases[n_in + ri] = n_out + ro
    res = pl.pallas_call(
        wrapped, name=name, grid=grid, in_specs=list(in_specs) + [HBM] * nri, out_specs=ospecs + [HBM] * nro,
        out_shape=shapes + rider.outs,
        scratch_shapes=list(scratch_shapes) + [pltpu.SemaphoreType.DMA((rider.n_sems,)),
                                               pltpu.SemaphoreType.DMA((rider.n_sems,))],
        input_output_aliases=aliases, compiler_params=_params(*sem))(*args, *rider.ins)
    main, extra = res[:n_out], list(res[n_out:])
    return (list(main) if many else main[0]), extra


def _div_tile(n, mult, cap):
    best = n
    for t in range(mult, min(n, cap) + 1, mult):
        if n % t == 0:
            best = t
    return best


def _sigmoid(x):
    return 1.0 / (1.0 + jnp.exp(-x))


def _ln_stats(x):
    mu = jnp.mean(x, axis=-1, keepdims=True)
    xc = x - mu
    var = jnp.mean(xc * xc, axis=-1, keepdims=True)
    rstd = lax.rsqrt(var + LN_EPS)
    return xc * rstd, rstd


def _ln_bwd(dy, xhat, rstd, g):
    dyg = dy * g
    m1 = jnp.mean(dyg, axis=-1, keepdims=True)
    m2 = jnp.mean(dyg * xhat, axis=-1, keepdims=True)
    return rstd * (dyg - m1 - xhat * m2)


def _dot_nt(a, b):
    return lax.dot_general(a, b, (((1,), (1,)), ((), ())), preferred_element_type=F32)


def _dot_tn(a, b):
    return lax.dot_general(a, b, (((0,), (0,)), ((), ())), preferred_element_type=F32)


def _dot(a, b):
    return jnp.dot(a, b, preferred_element_type=F32)


def _dot3(m, x):
    mb = m.astype(BF16)
    x1 = x.astype(BF16)
    r1 = x - x1.astype(F32)
    x2 = r1.astype(BF16)
    x3 = (r1 - x2.astype(F32)).astype(BF16)
    return _dot(mb, x1) + _dot(mb, x2) + _dot(mb, x3)


def _place_shard(x, name, chip_idx, dtype):
    r, c = x.shape
    tr = _div_tile(r, 16, 512)

    def body(s_ref, x_ref, o_ref):
        del s_ref
        o_ref[...] = x_ref[...].astype(dtype)

    return pl.pallas_call(
        body, name=name,
        grid_spec=pltpu.PrefetchScalarGridSpec(
            num_scalar_prefetch=1, grid=(r // tr,),
            in_specs=[pl.BlockSpec((tr, c), lambda i, s: (i, 0))],
            out_specs=pl.BlockSpec((None, tr, c), lambda i, s: (s[0], i, 0))),
        out_shape=jax.ShapeDtypeStruct((N_CHIPS, r, c), dtype),
        compiler_params=_params("parallel"),
    )(chip_idx, x)


def _ln0(x, g, b, tm):
    t, d = x.shape

    def body(x_ref, g_ref, b_ref, o_ref):
        xh, _ = _ln_stats(x_ref[...])
        o_ref[...] = (xh * g_ref[...] + b_ref[...]).astype(BF16)

    row = pl.BlockSpec((1, d), lambda i: (0, 0))
    return pl.pallas_call(
        body, name="ln0", grid=(t // tm,),
        in_specs=[pl.BlockSpec((tm, d), lambda i: (i, 0)), row, row],
        out_specs=pl.BlockSpec((tm, d), lambda i: (i, 0)),
        out_shape=jax.ShapeDtypeStruct((t, d), BF16),
        compiler_params=_params("parallel"),
    )(x, g, b)


def _proj(name, a, w3, n_sec, tm, tn, rider=None):
    m, k = a.shape
    s, _, ws = w3.shape
    sec_w = s * ws // n_sec
    nj = ws // tn
    per_sec = sec_w // tn

    def body(a_ref, w_ref, o_ref):
        o_ref[...] = _dot(a_ref[...], w_ref[...])

    return _call(
        body, (a, w3), name=name, grid=(s * nj, m // tm),
        in_specs=[pl.BlockSpec((tm, k), lambda j, i: (i, 0)),
                  pl.BlockSpec((None, k, tn), lambda j, i: (j // nj, 0, j % nj))],
        out_specs=pl.BlockSpec((None, tm, tn), lambda j, i: (j // per_sec, i, j % per_sec)),
        out_shape=jax.ShapeDtypeStruct((n_sec, m, sec_w), F32), rider=rider)


def _proj_t(name, a, w, tm, tn, rider=None):
    m, k = a.shape
    n = w.shape[0]

    def body(a_ref, w_ref, o_ref):
        o_ref[...] = _dot_nt(a_ref[...], w_ref[...])

    return _call(
        body, (a, w), name=name, grid=(n // tn, m // tm),
        in_specs=[pl.BlockSpec((tm, k), lambda j, i: (i, 0)),
                  pl.BlockSpec((tn, k), lambda j, i: (j, 0))],
        out_specs=pl.BlockSpec((tm, tn), lambda j, i: (i, j)),
        out_shape=jax.ShapeDtypeStruct((m, n), F32), rider=rider)


def _wgrad(name, a, b, out_shape, grid, a_spec, b_spec, o_spec, rider=None, dot=_dot_tn):
    nt = len(grid) - 1

    def body(a_ref, b_ref, o_ref):
        t = pl.program_id(nt)
        prod = dot(a_ref[...], b_ref[...])

        @pl.when(t == 0)
        def _():
            o_ref[...] = prod

        @pl.when(t > 0)
        def _():
            o_ref[...] += prod

    return _call(
        body, (a, b), name=name, grid=grid, in_specs=[a_spec, b_spec], out_specs=o_spec,
        out_shape=jax.ShapeDtypeStruct(out_shape, F32), rider=rider)


def _mix_ln1(cat, w_out, x, g0, b0, g1, b1, alpha, tm, rider=None):
    t, d = x.shape

    def body(cat_ref, w_ref, x_ref, g0_ref, b0_ref, g1_ref, b1_ref, xh_ref, h1b_ref, rstd_ref):
        mix = _dot(cat_ref[...], w_ref[...])
        xh0, _ = _ln_stats(x_ref[...])
        z1 = alpha * (xh0 * g0_ref[...] + b0_ref[...]) + mix
        xh1, rstd1 = _ln_stats(z1)
        xh_ref[...] = xh1
        h1b_ref[...] = (xh1 * g1_ref[...] + b1_ref[...]).astype(BF16)
        rstd_ref[...] = rstd1

    row = pl.BlockSpec((1, d), lambda i: (0, 0))
    blk = pl.BlockSpec((tm, d), lambda i: (i, 0))
    return _call(
        body, (cat, w_out, x, g0, b0, g1, b1), name="mix_ln1", grid=(t // tm,),
        in_specs=[blk, pl.BlockSpec((d, d), lambda i: (0, 0)), blk, row, row, row, row],
        out_specs=[blk, blk, pl.BlockSpec((tm, 1), lambda i: (i, 0))],
        out_shape=[jax.ShapeDtypeStruct((t, d), F32), jax.ShapeDtypeStruct((t, d), BF16),
                   jax.ShapeDtypeStruct((t, 1), F32)], rider=rider)


def _ln2_loss(ffn, xhat1, tgt, g1, b1, g2, b2, alpha, tm):
    t, d = xhat1.shape
    ni = t // tm
    inv_d = 1.0 / d

    def body(ffn_ref, xh1_ref, tgt_ref, g1_ref, b1_ref, g2_ref, b2_ref,
             dz2_ref, dz2b_ref, dg2_ref, db2_ref, loss_ref, lrow):
        i = pl.program_id(0)
        h1 = xh1_ref[...] * g1_ref[...] + b1_ref[...]
        xh2, rstd2 = _ln_stats(alpha * h1 + ffn_ref[...])
        g2v = g2_ref[...]
        diff = xh2 * g2v + b2_ref[...] - tgt_ref[...]
        dh2 = diff * inv_d
        sq = jnp.sum(diff * diff, axis=0, keepdims=True)
        dg = jnp.sum(dh2 * xh2, axis=0, keepdims=True)
        db = jnp.sum(dh2, axis=0, keepdims=True)

        @pl.when(i == 0)
        def _():
            lrow[...] = sq
            dg2_ref[...] = dg
            db2_ref[...] = db

        @pl.when(i > 0)
        def _():
            lrow[...] += sq
            dg2_ref[...] += dg
            db2_ref[...] += db

        dz2 = _ln_bwd(dh2, xh2, rstd2, g2v)
        dz2_ref[...] = dz2
        dz2b_ref[...] = dz2.astype(BF16)

        @pl.when(i == ni - 1)
        def _():
            tot = jnp.sum(lrow[...], axis=-1, keepdims=True) * (0.5 * inv_d)
            loss_ref[...] = jnp.broadcast_to(tot, (1, LANE))

    row = pl.BlockSpec((1, d), lambda i: (0, 0))
    blk = pl.BlockSpec((tm, d), lambda i: (i, 0))
    return _call(
        body, (ffn, xhat1, tgt, g1, b1, g2, b2), name="ln2_loss", grid=(ni,),
        in_specs=[blk, blk, blk, row, row, row, row],
        out_specs=[blk, blk, row, row, pl.BlockSpec((1, LANE), lambda i: (0, 0))],
        out_shape=[jax.ShapeDtypeStruct((t, d), F32), jax.ShapeDtypeStruct((t, d), BF16),
                   jax.ShapeDtypeStruct((1, d), F32), jax.ShapeDtypeStruct((1, d), F32),
                   jax.ShapeDtypeStruct((1, LANE), F32)],
        scratch_shapes=[pltpu.VMEM((1, d), F32)])


def _ln1_bwd(pre, dz2, xhat1, rstd1, g1, alpha, tm):
    t, d = dz2.shape

    def body(pre_ref, dz2_ref, xh_ref, rstd_ref, g_ref, dz1_ref, dz1b_ref, dg_ref, db_ref):
        i = pl.program_id(0)
        dh1 = alpha * dz2_ref[...] + pre_ref[...]
        xh = xh_ref[...]
        dg = jnp.sum(dh1 * xh, axis=0, keepdims=True)
        db = jnp.sum(dh1, axis=0, keepdims=True)

        @pl.when(i == 0)
        def _():
            dg_ref[...] = dg
            db_ref[...] = db

        @pl.when(i > 0)
        def _():
            dg_ref[...] += dg
            db_ref[...] += db

        dz1 = _ln_bwd(dh1, xh, rstd_ref[...], g_ref[...])
        dz1_ref[...] = dz1
        dz1b_ref[...] = dz1.astype(BF16)

    row = pl.BlockSpec((1, d), lambda i: (0, 0))
    blk = pl.BlockSpec((tm, d), lambda i: (i, 0))
    return _call(
        body, (pre, dz2, xhat1, rstd1, g1), name="ln1_bwd", grid=(t // tm,),
        in_specs=[blk, blk, blk, pl.BlockSpec((tm, 1), lambda i: (i, 0)), row],
        out_specs=[blk, blk, row, row],
        out_shape=[jax.ShapeDtypeStruct((t, d), F32), jax.ShapeDtypeStruct((t, d), BF16),
                   jax.ShapeDtypeStruct((1, d), F32), jax.ShapeDtypeStruct((1, d), F32)])


def _ln0_bwd(pre, dz1, x, g0, alpha, tm):
    t, d = x.shape

    def body(pre_ref, dz1_ref, x_ref, g_ref, dx_ref, dg_ref, db_ref):
        i = pl.program_id(0)
        dh0 = alpha * dz1_ref[...] + pre_ref[...]
        xh, rstd = _ln_stats(x_ref[...])
        dg = jnp.sum(dh0 * xh, axis=0, keepdims=True)
        db = jnp.sum(dh0, axis=0, keepdims=True)

        @pl.when(i == 0)
        def _():
            dg_ref[...] = dg
            db_ref[...] = db

        @pl.when(i > 0)
        def _():
            dg_ref[...] += dg
            db_ref[...] += db

        dx_ref[...] = _ln_bwd(dh0, xh, rstd, g_ref[...])

    row = pl.BlockSpec((1, d), lambda i: (0, 0))
    blk = pl.BlockSpec((tm, d), lambda i: (i, 0))
    return _call(
        body, (pre, dz1, x, g0), name="ln0_bwd", grid=(t // tm,),
        in_specs=[blk, blk, blk, row], out_specs=[blk, row, row],
        out_shape=[jax.ShapeDtypeStruct((t, d), F32), jax.ShapeDtypeStruct((1, d), F32),
                   jax.ShapeDtypeStruct((1, d), F32)])


def _shift_copies(ext, shifted):
    n = shifted.shape[1]
    for p in range(1, SUBLANE):
        shifted[p - 1] = ext[pl.ds(p, n), :]


def _window(ext, shifted, start, rows):
    p = start % SUBLANE
    if p == 0:
        return ext[pl.ds(start, rows), :]
    return shifted[p - 1, pl.ds(start - p, rows), :]


def _conv_fwd(p3, conv_w, conv_b, cn_g, cn_b, tc, cb, rider=None):
    _, t, w = p3.shape
    kk = conv_w.shape[0]
    off = HALO - (kk - 1)
    hb = tc // HALO

    def body(a_ref, g_ref, ap_ref, gp_ref, w_ref, b_ref, ng_ref, nb_ref, cat_ref, u1_ref, ext, sh):
        i = pl.program_id(1)
        ext[pl.ds(HALO, tc), :] = a_ref[...] * _sigmoid(g_ref[...])
        prev = ap_ref[...] * _sigmoid(gp_ref[...])
        ext[pl.ds(0, HALO), :] = jnp.where(i > 0, prev, 0.0)
        _shift_copies(ext, sh)
        for r in range(tc // ROWS):
            acc = jnp.broadcast_to(b_ref[...], (ROWS, cb))
            for k in range(kk):
                acc = acc + w_ref[k:k + 1, :] * _window(ext, sh, r * ROWS + off + k, ROWS)
            u1_ref[pl.ds(r * ROWS, ROWS), :] = acc
            for g in range(cb // LANE):
                sl = slice(g * LANE, (g + 1) * LANE)
                xh, _ = _ln_stats(acc[:, sl])
                u2 = xh * ng_ref[:, sl] + nb_ref[:, sl]
                cat_ref[pl.ds(r * ROWS, ROWS), sl] = (u2 * _sigmoid(u2)).astype(BF16)

    cur = lambda sec: pl.BlockSpec((None, tc, cb), lambda j, i: (sec, i, j))
    prev = lambda sec: pl.BlockSpec((None, HALO, cb), lambda j, i: (sec, jnp.maximum(i * hb - 1, 0), j))
    row = pl.BlockSpec((1, cb), lambda j, i: (0, j))
    return _call(
        body, (p3, p3, p3, p3, conv_w, conv_b, cn_g, cn_b), name="conv_fwd", grid=(w // cb, t // tc),
        in_specs=[cur(0), cur(1), prev(0), prev(1), pl.BlockSpec((kk, cb), lambda j, i: (0, j)), row, row, row],
        out_specs=[pl.BlockSpec((tc, cb), lambda j, i: (i, j)), pl.BlockSpec((tc, cb), lambda j, i: (i, j))],
        out_shape=[jax.ShapeDtypeStruct((t, 2 * w), BF16), jax.ShapeDtypeStruct((t, w), F32)],
        scratch_shapes=[pltpu.VMEM((tc + HALO, cb), F32),
                        pltpu.VMEM((SUBLANE - 1, tc + HALO - SUBLANE, cb), F32)], rider=rider)


def _conv_norm_bwd(dcat, u1, cn_g, cn_b, tc):
    t, w = u1.shape

    def body(du_ref, u1_ref, ng_ref, nb_ref, du1_ref, dg_ref, db_ref):
        i = pl.program_id(0)
        for g in range(w // LANE):
            sl = slice(g * LANE, (g + 1) * LANE)
            ng = ng_ref[:, sl]
            xh, rstd = _ln_stats(u1_ref[:, sl])
            u2 = xh * ng + nb_ref[:, sl]
            sg = _sigmoid(u2)
            du2 = du_ref[:, sl] * (sg * (1.0 + u2 * (1.0 - sg)))
            dg = jnp.sum(du2 * xh, axis=0, keepdims=True)
            db = jnp.sum(du2, axis=0, keepdims=True)

            @pl.when(i == 0)
            def _():
                dg_ref[:, sl] = dg
                db_ref[:, sl] = db

            @pl.when(i > 0)
            def _():
                dg_ref[:, sl] += dg
                db_ref[:, sl] += db

            du1_ref[:, sl] = _ln_bwd(du2, xh, rstd, ng)

    row = pl.BlockSpec((1, w), lambda i: (0, 0))
    blk = pl.BlockSpec((tc, w), lambda i: (i, 0))
    return pl.pallas_call(
        body, name="conv_norm_bwd", grid=(t // tc,),
        in_specs=[blk, blk, row, row], out_specs=[blk, row, row],
        out_shape=[jax.ShapeDtypeStruct((t, w), F32), jax.ShapeDtypeStruct((1, w), F32),
                   jax.ShapeDtypeStruct((1, w), F32)],
        compiler_params=_params("arbitrary"),
    )(dcat, u1, cn_g, cn_b)


def _conv_bwd(du1, p3, conv_w, tc, cb, rider=None):
    n_sec, t, w = p3.shape
    kk = conv_w.shape[0]
    off = HALO - (kk - 1)
    hb = tc // HALO
    nt = t // tc
    kpad = -(-kk // SUBLANE) * SUBLANE

    def body(d_ref, dn_ref, a_ref, g_ref, ap_ref, gp_ref, w_ref, dp_ref, dw_ref, db_ref,
             extd, extu, shd, shu, wacc, bacc):
        i = pl.program_id(1)

        @pl.when(i == 0)
        def _():
            wacc[...] = jnp.zeros_like(wacc)
            bacc[...] = jnp.zeros_like(bacc)

        extd[pl.ds(0, tc), :] = d_ref[...]
        extd[pl.ds(tc, HALO), :] = jnp.where(i < nt - 1, dn_ref[...], 0.0)
        extu[pl.ds(HALO, tc), :] = a_ref[...] * _sigmoid(g_ref[...])
        extu[pl.ds(0, HALO), :] = jnp.where(i > 0, ap_ref[...] * _sigmoid(gp_ref[...]), 0.0)
        _shift_copies(extd, shd)
        _shift_copies(extu, shu)
        for r in range(tc // ROWS):
            rows = pl.ds(r * ROWS, ROWS)
            acc = jnp.zeros((ROWS, cb), F32)
            for k in range(kk):
                acc = acc + w_ref[k:k + 1, :] * _window(extd, shd, r * ROWS + (kk - 1) - k, ROWS)
            a = a_ref[rows, :]
            sg = _sigmoid(g_ref[rows, :])
            dp_ref[0, rows, :] = (acc * sg).astype(BF16)
            dp_ref[1, rows, :] = (acc * a * sg * (1.0 - sg)).astype(BF16)
            d = d_ref[rows, :]
            bacc[...] += jnp.sum(d.reshape(ROWS // SUBLANE, SUBLANE, cb), axis=0)
            for k in range(kk):
                prod = d * _window(extu, shu, r * ROWS + off + k, ROWS)
                wacc[k] += jnp.sum(prod.reshape(ROWS // SUBLANE, SUBLANE, cb), axis=0)

        @pl.when(i == nt - 1)
        def _():
            for k in range(kk):
                dw_ref[k:k + 1, :] = jnp.sum(wacc[k], axis=0, keepdims=True)
            if kpad > kk:
                dw_ref[kk:kpad, :] = jnp.zeros((kpad - kk, cb), F32)
            db_ref[...] = jnp.sum(bacc[...], axis=0, keepdims=True)

    cur = lambda sec: pl.BlockSpec((None, tc, cb), lambda j, i: (sec, i, j))
    prev = lambda sec: pl.BlockSpec((None, HALO, cb), lambda j, i: (sec, jnp.maximum(i * hb - 1, 0), j))
    return _call(
        body, (du1, du1, p3, p3, p3, p3, conv_w), name="conv_bwd", grid=(w // cb, nt),
        in_specs=[pl.BlockSpec((tc, cb), lambda j, i: (i, j)),
                  pl.BlockSpec((HALO, cb), lambda j, i: (jnp.minimum((i + 1) * hb, t // HALO - 1), j)),
                  cur(0), cur(1), prev(0), prev(1), pl.BlockSpec((kk, cb), lambda j, i: (0, j))],
        out_specs=[pl.BlockSpec((2, tc, cb), lambda j, i: (0, i, j)),
                   pl.BlockSpec((kpad, cb), lambda j, i: (0, j)),
                   pl.BlockSpec((1, cb), lambda j, i: (0, j))],
        out_shape=[jax.ShapeDtypeStruct((n_sec, t, w), BF16), jax.ShapeDtypeStruct((kpad, w), F32),
                   jax.ShapeDtypeStruct((1, w), F32)],
        scratch_shapes=[pltpu.VMEM((tc + HALO, cb), F32), pltpu.VMEM((tc + HALO, cb), F32),
                        pltpu.VMEM((SUBLANE - 1, tc + HALO - SUBLANE, cb), F32),
                        pltpu.VMEM((SUBLANE - 1, tc + HALO - SUBLANE, cb), F32),
                        pltpu.VMEM((kk, SUBLANE, cb), F32), pltpu.VMEM((SUBLANE, cb), F32)], rider=rider)


def _ffn_act_fwd(hh3, fw, fb, tc, cb, rider=None):
    _, t, dff = hh3.shape
    kk = fw.shape[0]
    off = FHALO - (kk - 1)
    hb = tc // FHALO

    def body(g_ref, v_ref, gp_ref, w_ref, b_ref, act_ref, ext):
        i = pl.program_id(1)
        ext[pl.ds(FHALO, tc), :] = g_ref[...]
        ext[pl.ds(0, FHALO), :] = jnp.where(i > 0, gp_ref[...], 0.0)
        for r in range(tc // ROWS):
            rows = pl.ds(r * ROWS, ROWS)
            gc = jnp.broadcast_to(b_ref[...], (ROWS, cb))
            for k in range(kk):
                gc = gc + w_ref[k:k + 1, :] * ext[pl.ds(r * ROWS + off + k, ROWS), :]
            act_ref[rows, :] = (gc * _sigmoid(gc) * v_ref[rows, :]).astype(BF16)

    return _call(
        body, (hh3, hh3, hh3, fw, fb), name="ffn_act_fwd", grid=(dff // cb, t // tc),
        in_specs=[pl.BlockSpec((None, tc, cb), lambda j, i: (0, i, j)),
                  pl.BlockSpec((None, tc, cb), lambda j, i: (1, i, j)),
                  pl.BlockSpec((None, FHALO, cb), lambda j, i: (0, jnp.maximum(i * hb - 1, 0), j)),
                  pl.BlockSpec((kk, cb), lambda j, i: (0, j)),
                  pl.BlockSpec((1, cb), lambda j, i: (0, j))],
        out_specs=pl.BlockSpec((tc, cb), lambda j, i: (i, j)),
        out_shape=jax.ShapeDtypeStruct((t, dff), BF16),
        scratch_shapes=[pltpu.VMEM((tc + FHALO, cb), F32)], rider=rider)


def _ffn_act_bwd(dact, hh3, fw, fb, tc, cb):
    _, t, dff = hh3.shape
    kk = fw.shape[0]
    off = FHALO - (kk - 1)
    hb = tc // FHALO
    nt = t // tc
    te = tc + FHALO

    def body(da_ref, dan_ref, g_ref, gp_ref, gn_ref, v_ref, vn_ref, w_ref, b_ref,
             dhh_ref, dw_ref, db_ref, gext, dext, wacc, bacc):
        i = pl.program_id(1)

        @pl.when(i == 0)
        def _():
            wacc[...] = jnp.zeros_like(wacc)
            bacc[...] = jnp.zeros_like(bacc)

        gext[pl.ds(0, FHALO), :] = jnp.where(i > 0, gp_ref[...], 0.0)
        gext[pl.ds(FHALO, tc), :] = g_ref[...]
        gext[pl.ds(FHALO + tc, FHALO), :] = gn_ref[...]

        def gate_grad(r0, n, da, v):
            gc = jnp.broadcast_to(b_ref[...], (n, cb))
            for k in range(kk):
                gc = gc + w_ref[k:k + 1, :] * gext[pl.ds(r0 + off + k, n), :]
            sg = _sigmoid(gc)
            return gc * sg, da * v * (sg * (1.0 + gc * (1.0 - sg)))

        for r in range(tc // ROWS):
            rows = pl.ds(r * ROWS, ROWS)
            da = da_ref[rows, :]
            silu, dgc = gate_grad(r * ROWS, ROWS, da, v_ref[rows, :])
            dext[rows, :] = dgc
            dhh_ref[1, rows, :] = (da * silu).astype(BF16)
        _, dgc_next = gate_grad(tc, FHALO, dan_ref[...], vn_ref[...])
        dext[pl.ds(tc, FHALO), :] = jnp.where(i < nt - 1, dgc_next, 0.0)
        for r in range(tc // ROWS):
            rows = pl.ds(r * ROWS, ROWS)
            dg = jnp.zeros((ROWS, cb), F32)
            for k in range(kk):
                dg = dg + w_ref[k:k + 1, :] * dext[pl.ds(r * ROWS + (kk - 1) - k, ROWS), :]
            dhh_ref[0, rows, :] = dg.astype(BF16)
            dgc = dext[rows, :]
            bacc[...] += jnp.sum(dgc.reshape(ROWS // SUBLANE, SUBLANE, cb), axis=0)
            for k in range(kk):
                prod = dgc * gext[pl.ds(r * ROWS + off + k, ROWS), :]
                wacc[k] += jnp.sum(prod.reshape(ROWS // SUBLANE, SUBLANE, cb), axis=0)

        @pl.when(i == nt - 1)
        def _():
            for k in range(kk):
                dw_ref[k:k + 1, :] = jnp.sum(wacc[k], axis=0, keepdims=True)
            dw_ref[kk:SUBLANE, :] = jnp.zeros((SUBLANE - kk, cb), F32)
            db_ref[...] = jnp.sum(bacc[...], axis=0, keepdims=True)

    nxt = lambda i: jnp.minimum((i + 1) * hb, t // FHALO - 1)
    return pl.pallas_call(
        body, name="ffn_act_bwd", grid=(dff // cb, nt),
        in_specs=[pl.BlockSpec((tc, cb), lambda j, i: (i, j)),
                  pl.BlockSpec((FHALO, cb), lambda j, i: (nxt(i), j)),
                  pl.BlockSpec((None, tc, cb), lambda j, i: (0, i, j)),
                  pl.BlockSpec((None, FHALO, cb), lambda j, i: (0, jnp.maximum(i * hb - 1, 0), j)),
                  pl.BlockSpec((None, FHALO, cb), lambda j, i: (0, nxt(i), j)),
                  pl.BlockSpec((None, tc, cb), lambda j, i: (1, i, j)),
                  pl.BlockSpec((None, FHALO, cb), lambda j, i: (1, nxt(i), j)),
                  pl.BlockSpec((kk, cb), lambda j, i: (0, j)),
                  pl.BlockSpec((1, cb), lambda j, i: (0, j))],
        out_specs=[pl.BlockSpec((2, tc, cb), lambda j, i: (0, i, j)),
                   pl.BlockSpec((SUBLANE, cb), lambda j, i: (0, j)),
                   pl.BlockSpec((1, cb), lambda j, i: (0, j))],
        out_shape=[jax.ShapeDtypeStruct((2, t, dff), BF16), jax.ShapeDtypeStruct((SUBLANE, dff), F32),
                   jax.ShapeDtypeStruct((1, dff), F32)],
        scratch_shapes=[pltpu.VMEM((tc + 2 * FHALO, cb), F32), pltpu.VMEM((te, cb), F32),
                        pltpu.VMEM((kk, SUBLANE, cb), F32), pltpu.VMEM((SUBLANE, cb), F32)],
        compiler_params=_params("parallel", "arbitrary"),
    )(dact, dact, hh3, hh3, hh3, hh3, hh3, fw, fb)


def _chunk_consts():
    r = lax.broadcasted_iota(jnp.int32, (CHUNK, CHUNK), 0)
    c = lax.broadcasted_iota(jnp.int32, (CHUNK, CHUNK), 1)
    blk = (r // SUB) * SUB
    tri = (c <= r).astype(F32)
    start = (c < blk).astype(F32)
    end = (c < blk + SUB).astype(F32)
    return jnp.concatenate([tri, start, end, jnp.ones((SUBLANE, CHUNK), F32)], axis=0)


def _gate_terms(q, fpre, lb):
    sf = _sigmoid(fpre)
    fg = lb + (1.0 - lb) * sf
    sq = _sigmoid(q)
    return sf, fg, 1.0 - fg, sq, q * sq


def _decays(g, consts):
    cs = _dot3(consts, g)
    b = cs[0:CHUNK]
    rs = cs[CHUNK:2 * CHUNK]
    re = cs[2 * CHUNK:3 * CHUNK]
    tot = cs[3 * CHUNK:3 * CHUNK + 1]
    return b, rs, re, tot


def _lower_bound(lb_ref):
    l0, l1 = lb_ref[0:1, :], lb_ref[1:2, :]
    mx = jnp.maximum(l0, l1)
    e0, e1 = jnp.exp(l0 - mx), jnp.exp(l1 - mx)
    return e0 / (e0 + e1)


def _scaled_keys(kt, rs, re, rowblk, i):
    scale = jnp.where(rowblk < i, jnp.exp(jnp.minimum(rs[SUB * i:SUB * i + 1, :] - re, 0.0)), 0.0)
    return kt * scale, scale


def _hgrn_fwd(p3, lb_logits, hg, cat, tb, hpb, rider=None):
    _, t, w = p3.shape
    nh = w // LANE
    nc = tb // CHUNK
    assert nh % hpb == 0

    def body(q_ref, f_ref, v_ref, og_ref, lb_ref, hg_ref, cat_in, cat_ref, o_ref, st_ref, state):
        del cat_in
        consts = _chunk_consts()
        lb_all = _lower_bound(lb_ref)
        rowblk = lax.broadcasted_iota(jnp.int32, (CHUNK, 1), 0) // SUB
        rowpos = lax.broadcasted_iota(jnp.int32, (CHUNK, 1), 0) % SUB

        @pl.when(pl.program_id(1) == 0)
        def _():
            state[...] = jnp.zeros_like(state)

        def chunk(c, carry):
            rows = pl.ds(pl.multiple_of(c * CHUNK, CHUNK), CHUNK)
            heads = range(hpb)
            sls = [slice(j * LANE, (j + 1) * LANE) for j in heads]
            v = [v_ref[rows, s] for s in sls]
            vb = [x.astype(BF16) for x in v]
            gates = [_gate_terms(q_ref[rows, s], f_ref[rows, s], lb_all[:, s]) for s in sls]
            fg = [g[1] for g in gates]
            kk = [g[2] for g in gates]
            qh = [g[4] for g in gates]
            dec = [_decays(jnp.log(x), consts) for x in fg]
            b = [x[0] for x in dec]
            rs = [x[1] for x in dec]
            re = [x[2] for x in dec]
            tot = [x[3] for x in dec]
            qt = [qh[j] * jnp.exp(b[j] - rs[j]) for j in heads]
            kt = [kk[j] * jnp.exp(re[j] - b[j]) for j in heads]
            st = [state[j] for j in heads]
            for j in heads:
                st_ref[j, c] = st[j]
            a = [jnp.zeros((CHUNK, CHUNK), F32) for _ in heads]
            for i in range(1, CHUNK // SUB):
                for j in heads:
                    ki, _ = _scaled_keys(kt[j], rs[j], re[j], rowblk, i)
                    a[j] = a[j] + _dot_nt(jnp.where(rowblk == i, qt[j], 0.0).astype(BF16), ki.astype(BF16))
            o = [_dot(a[j].astype(BF16), vb[j]) for j in heads]
            o = [o[j] + _dot_nt((qh[j] * jnp.exp(b[j])).astype(BF16), st[j].astype(BF16)) for j in heads]
            for j in heads:
                k_up = kk[j] * jnp.exp(tot[j] - b[j])
                state[j] = st[j] * jnp.exp(tot[j]) + _dot_tn(vb[j], k_up.astype(BF16))
            for j in heads:
                e = None
                for d in range(SUB):
                    if d == 0:
                        ks, vs, term = kk[j], v[j], qh[j] * kk[j]
                    else:
                        rf = pltpu.roll(fg[j], d - 1, 0) if d > 1 else fg[j]
                        e = rf if e is None else e * rf
                        ks, vs = pltpu.roll(kk[j], d, 0), pltpu.roll(v[j], d, 0)
                        term = jnp.where(rowpos >= d, qh[j] * ks * e, 0.0)
                    o[j] = o[j] + jnp.sum(term, axis=-1, keepdims=True) * vs
            for j in heads:
                og = og_ref[rows, sls[j]]
                o_ref[rows, sls[j]] = o[j]
                r = lax.rsqrt(jnp.mean(o[j] * o[j], axis=-1, keepdims=True) + RMS_EPS)
                cat_ref[rows, sls[j]] = (o[j] * r * hg_ref[:, sls[j]] * (og * _sigmoid(og))).astype(BF16)
            return carry

        lax.fori_loop(0, nc, chunk, 0)

    bw = hpb * LANE
    sec = lambda s: pl.BlockSpec((None, tb, bw), lambda h, i: (s, i, h))
    return _call(
        body, (p3, p3, p3, p3, lb_logits, hg, cat), name="hgrn_fwd", grid=(nh // hpb, t // tb),
        in_specs=[sec(2), sec(3), sec(4), sec(5),
                  pl.BlockSpec((2, bw), lambda h, i: (0, h)),
                  pl.BlockSpec((1, bw), lambda h, i: (0, h)), HBM],
        out_specs=[pl.BlockSpec((tb, bw), lambda h, i: (i, nh // hpb + h)),
                   pl.BlockSpec((tb, bw), lambda h, i: (i, h)),
                   pl.BlockSpec((hpb, nc, LANE, LANE), lambda h, i: (h, i, 0, 0))],
        out_shape=[jax.ShapeDtypeStruct(cat.shape, BF16), jax.ShapeDtypeStruct((t, w), F32),
                   jax.ShapeDtypeStruct((nh, t // CHUNK, LANE, LANE), F32)],
        scratch_shapes=[pltpu.VMEM((hpb, LANE, LANE), F32)], aliases={6: 0}, rider=rider)


def _hgrn_bwd(p3, lb_logits, hg, o_pre, states, dcat, dp3, tb, hpb, rider=None):
    n_sec, t, w = p3.shape
    nh = w // LANE
    assert nh % hpb == 0
    nc = tb // CHUNK
    nb = t // tb

    def body(q_ref, f_ref, v_ref, og_ref, lb_ref, hg_ref, o_ref, st_ref, dc_ref, dp_in,
             dp_ref, dlb_ref, dhg_ref, dstate, stash, lbacc, hgacc):
        del dp_in
        i, half = pl.program_id(1), pl.program_id(2)

        @pl.when(half == 1)
        def _():
            dp_ref[...] = stash[...]

        @pl.when(half == 0)
        def _():
            consts = _chunk_consts()
            rr = lax.broadcasted_iota(jnp.int32, (CHUNK, CHUNK), 0)
            cc = lax.broadcasted_iota(jnp.int32, (CHUNK, CHUNK), 1)
            upper = (cc >= rr).astype(F32)
            lb_all = _lower_bound(lb_ref)
            rowblk = lax.broadcasted_iota(jnp.int32, (CHUNK, 1), 0) // SUB
            rowpos = lax.broadcasted_iota(jnp.int32, (CHUNK, 1), 0) % SUB

            @pl.when(i == 0)
            def _():
                dstate[...] = jnp.zeros_like(dstate)
                lbacc[...] = jnp.zeros_like(lbacc)
                hgacc[...] = jnp.zeros_like(hgacc)

            def head(j, c, rows):
                sl = slice(j * LANE, (j + 1) * LANE)
                lb = lb_all[:, sl]
                hgv = hg_ref[:, sl]
                q = q_ref[rows, sl]
                v = v_ref[rows, sl]
                og = og_ref[rows, sl]
                o = o_ref[rows, sl]
                dcg = dc_ref[rows, sl]
                sf, fg, kk, sq, qh = _gate_terms(q, f_ref[rows, sl], lb)
                b, rs, re, tot = _decays(jnp.log(fg), consts)
                eq = jnp.exp(b - rs)
                ek = jnp.exp(re - b)
                qt = qh * eq
                kt = kk * ek
                e_in = jnp.exp(b)
                e_up = jnp.exp(tot - b)
                e_tot = jnp.exp(tot)
                q_in = (qh * e_in).astype(BF16)
                k_up = (kk * e_up).astype(BF16)
                vb = v.astype(BF16)
                st = st_ref[j, c]
                dst = dstate[j]
                dstb = dst.astype(BF16)
                yield

                sg = _sigmoid(og)
                r = lax.rsqrt(jnp.mean(o * o, axis=-1, keepdims=True) + RMS_EPS)
                ohat = o * r
                d_og = dcg * ohat * hgv * (sg * (1.0 + og * (1.0 - sg)))
                d_on = dcg * (og * sg)
                hgacc[:, sl] += jnp.sum((d_on * ohat).reshape(CHUNK // SUBLANE, SUBLANE, LANE), axis=0)
                d_oh = d_on * hgv
                do = r * (d_oh - ohat * jnp.mean(d_oh * ohat, axis=-1, keepdims=True))
                dob = do.astype(BF16)

                da = _dot_nt(dob, vb)
                yield
                a_off = jnp.zeros((CHUNK, CHUNK), F32)
                dqt = jnp.zeros((CHUNK, LANE), F32)
                dkt = jnp.zeros((CHUNK, LANE), F32)
                for blk in range(1, CHUNK // SUB):
                    ki, scale = _scaled_keys(kt, rs, re, rowblk, blk)
                    kib = ki.astype(BF16)
                    qib = jnp.where(rowblk == blk, qt, 0.0).astype(BF16)
                    dab = jnp.where(rowblk == blk, da, 0.0).astype(BF16)
                    a_off = a_off + _dot_nt(qib, kib)
                    dqt = dqt + _dot(dab, kib)
                    dkt = dkt + _dot_tn(dab, qib) * scale
                    yield
                dqh = dqt * eq
                dk = dkt * ek
                dv = _dot_tn(a_off.astype(BF16), dob)

                dqh = dqh + _dot(dob, st.astype(BF16)) * e_in
                dk = dk + _dot(vb, dstb) * e_up
                dv = dv + _dot_nt(k_up, dstb)
                st_end = st * e_tot + _dot_tn(vb, k_up)
                carry_g = jnp.sum(st_end * dst, axis=0, keepdims=True)
                dstate[j] = dst * e_tot + _dot_tn(dob, q_in)
                yield

                e = None
                for d in range(SUB):
                    if d == 0:
                        a_d = jnp.sum(qh * kk, axis=-1, keepdims=True)
                        da_d = jnp.sum(do * v, axis=-1, keepdims=True)
                        dqh = dqh + da_d * kk
                        dk = dk + da_d * qh
                        dv = dv + a_d * do
                        continue
                    rf = pltpu.roll(fg, d - 1, 0) if d > 1 else fg
                    e = rf if e is None else e * rf
                    em = jnp.where(rowpos >= d, e, 0.0)
                    ks, vs = pltpu.roll(kk, d, 0), pltpu.roll(v, d, 0)
                    a_d = jnp.sum(qh * ks * em, axis=-1, keepdims=True)
                    da_d = jnp.sum(do * vs, axis=-1, keepdims=True) * em
                    dqh = dqh + da_d * ks
                    dk = dk + pltpu.roll(da_d * qh, CHUNK - d, 0)
                    dv = dv + pltpu.roll(a_d * do, CHUNK - d, 0)
                yield

                dg = _dot3(upper, qh * dqh - kk * dk) + carry_g
                dfg = dg / fg - dk
                lbacc[:, sl] += jnp.sum((dfg * (1.0 - sf)).reshape(CHUNK // SUBLANE, SUBLANE, LANE), axis=0)
                dp_ref[0, rows, sl] = (dqh * (sq * (1.0 + q * (1.0 - sq)))).astype(BF16)
                dp_ref[1, rows, sl] = (dfg * (1.0 - lb) * sf * (1.0 - sf)).astype(BF16)
                stash[0, rows, sl] = dv.astype(BF16)
                stash[1, rows, sl] = d_og.astype(BF16)

            def chunk(cr, carry):
                c = nc - 1 - cr
                rows = pl.ds(pl.multiple_of(c * CHUNK, CHUNK), CHUNK)
                running = [head(j, c, rows) for j in range(hpb)]
                while running:
                    running = [g for g in running if next(g, StopIteration) is not StopIteration]
                return carry

            lax.fori_loop(0, nc, chunk, 0)

            @pl.when(i == nb - 1)
            def _():
                dlb_ref[...] = jnp.sum(lbacc[...], axis=0, keepdims=True)
                dhg_ref[...] = jnp.sum(hgacc[...], axis=0, keepdims=True)

    rev = lambda i: nb - 1 - i
    bw = hpb * LANE
    sec = lambda s: pl.BlockSpec((None, tb, bw), lambda h, i, z: (s, rev(i), h))
    return _call(
        body, (p3, p3, p3, p3, lb_logits, hg, o_pre, states, dcat, dp3), name="hgrn_bwd", grid=(nh // hpb, nb, 2),
        in_specs=[sec(2), sec(3), sec(4), sec(5),
                  pl.BlockSpec((2, bw), lambda h, i, z: (0, h)),
                  pl.BlockSpec((1, bw), lambda h, i, z: (0, h)),
                  pl.BlockSpec((tb, bw), lambda h, i, z: (rev(i), h)),
                  pl.BlockSpec((hpb, nc, LANE, LANE), lambda h, i, z: (h, rev(i), 0, 0)),
                  pl.BlockSpec((tb, bw), lambda h, i, z: (rev(i), nh // hpb + h)), HBM],
        out_specs=[pl.BlockSpec((2, tb, bw), lambda h, i, z: (1 + z, rev(i), h)),
                   pl.BlockSpec((1, bw), lambda h, i, z: (0, h)),
                   pl.BlockSpec((1, bw), lambda h, i, z: (0, h))],
        out_shape=[jax.ShapeDtypeStruct((n_sec, t, w), BF16), jax.ShapeDtypeStruct((1, w), F32),
                   jax.ShapeDtypeStruct((1, w), F32)],
        scratch_shapes=[pltpu.VMEM((hpb, LANE, LANE), F32), pltpu.VMEM((2, tb, bw), BF16),
                        pltpu.VMEM((SUBLANE, bw), F32), pltpu.VMEM((SUBLANE, bw), F32)],
        aliases={9: 0}, rider=rider)


def _place():
    x, y, c = lax.axis_index("x"), lax.axis_index("y"), lax.axis_index("c")
    chips = [(1 - x, y), (x, 1 - y), (1 - x, 1 - y)]
    return x, y, c, chips


def _rows(buf, px, py, pc, part=None):
    half = buf.shape[1] // 2
    if part is None:
        return buf.at[2 * px + py, pl.ds(pc * half, half)]
    lo, hi, n = part
    piece = half // n
    return buf.at[2 * px + py, pl.ds(pc * half + lo * piece, (hi - lo) * piece)]


def _rcopy(src, dst, send, recv, idx, to):
    return pltpu.make_async_remote_copy(src_ref=src, dst_ref=dst, send_sem=send.at[idx], recv_sem=recv.at[idx],
                                        device_id=to, device_id_type=MESH)


def _same(bufs):
    return [jax.ShapeDtypeStruct(b.shape, b.dtype) for b in bufs]


def _ride_gather_ici(bufs, parts=None):
    n = len(bufs)
    parts = parts or [None] * n

    def start(rin, rout, send, recv):
        x, y, c, chips = _place()
        for k in range(n):
            mine = _rows(rout[k], x, y, c, parts[k])
            for j, chip in enumerate(chips):
                _rcopy(mine, mine, send, recv, 3 * k + j, (*chip, c)).start()

    def finish(rin, rout, send, recv):
        x, y, c, chips = _place()
        for k in range(n):
            for j, chip in enumerate(chips):
                theirs = _rows(rout[k], *chip, c, parts[k])
                _rcopy(theirs, theirs, send, recv, 3 * k + j, (x, y, c)).wait_recv()
        for k in range(n):
            mine = _rows(rout[k], x, y, c, parts[k])
            for j in range(3):
                _rcopy(mine, mine, send, recv, 3 * k + j, (x, y, c)).wait_send()

    return _Rider(bufs, _same(bufs), {k: k for k in range(n)}, 3 * n, start, finish)


class _SemView:
    def __init__(self, ref, base):
        self.ref, self.base = ref, base

    @property
    def at(self):
        return self

    def __getitem__(self, idx):
        return self.ref.at[idx + self.base]


def _ride_both(a, b):
    nai, nao = len(a.ins), len(a.outs)

    def start(rin, rout, send, recv):
        a.start(rin[:nai], rout[:nao], send, recv)
        b.start(rin[nai:], rout[nao:], _SemView(send, a.n_sems), _SemView(recv, a.n_sems))

    def finish(rin, rout, send, recv):
        a.finish(rin[:nai], rout[:nao], send, recv)
        b.finish(rin[nai:], rout[nao:], _SemView(send, a.n_sems), _SemView(recv, a.n_sems))

    aliases = dict(a.aliases)
    aliases.update({nai + ri: nao + ro for ri, ro in b.aliases.items()})
    return _Rider(a.ins + b.ins, a.outs + b.outs, aliases, a.n_sems + b.n_sems, start, finish)


def _ride_gather_d2d(bufs):
    n = len(bufs)

    def start(rin, rout, send, recv):
        x, y, c, chips = _place()
        for k in range(n):
            for j, chip in enumerate(chips):
                got = _rows(rout[k], *chip, c)
                _rcopy(got, got, send, recv, 3 * k + j, (x, y, 1 - c)).start()

    def finish(rin, rout, send, recv):
        x, y, c, chips = _place()
        for k in range(n):
            for j, chip in enumerate(chips):
                theirs = _rows(rout[k], *chip, 1 - c)
                _rcopy(theirs, theirs, send, recv, 3 * k + j, (x, y, c)).wait_recv()
        for k in range(n):
            for j, chip in enumerate(chips):
                got = _rows(rout[k], *chip, c)
                _rcopy(got, got, send, recv, 3 * k + j, (x, y, c)).wait_send()

    return _Rider(bufs, _same(bufs), {k: k for k in range(n)}, 3 * n, start, finish)


def _ride_swap(grads):
    n = len(grads)

    def copy(k, rin, rout, send, recv):
        x, y, c, _ = _place()
        half = rin[k].shape[1] // 2
        return _rcopy(rin[k].at[:, pl.ds((1 - c) * half, half)], rout[k], send, recv, k, (x, y, 1 - c))

    def start(rin, rout, send, recv):
        for k in range(n):
            copy(k, rin, rout, send, recv).start()

    def finish(rin, rout, send, recv):
        for k in range(n):
            copy(k, rin, rout, send, recv).wait()

    outs = [jax.ShapeDtypeStruct((g.shape[0], g.shape[1] // 2, g.shape[2]), g.dtype) for g in grads]
    return _Rider(grads, outs, {}, n, start, finish)


def _ride_send_partials(parts):
    n = len(parts)

    def copies(rin, rout, send, recv):
        x, y, c, chips = _place()
        return [_rcopy(rin[k].at[2 * px + py], rout[k].at[j], send, recv, 3 * k + j, (px, py, c))
                for k in range(n) for j, (px, py) in enumerate(chips)]

    def start(rin, rout, send, recv):
        for cp in copies(rin, rout, send, recv):
            cp.start()

    def finish(rin, rout, send, recv):
        for cp in copies(rin, rout, send, recv):
            cp.wait()

    outs = [jax.ShapeDtypeStruct((3,) + p.shape[1:], p.dtype) for p in parts]
    return _Rider(parts, outs, {}, 3 * n, start, finish)


def _ride_join(bufs):
    n = len(bufs)

    def half_of(buf, pc):
        half = buf.shape[0] // 2
        return buf.at[pl.ds(pc * half, half)]

    def start(rin, rout, send, recv):
        x, y, c, _ = _place()
        for k in range(n):
            mine = half_of(rout[k], c)
            _rcopy(mine, mine, send, recv, k, (x, y, 1 - c)).start()

    def finish(rin, rout, send, recv):
        x, y, c, _ = _place()
        for k in range(n):
            mine, theirs = half_of(rout[k], c), half_of(rout[k], 1 - c)
            _rcopy(mine, mine, send, recv, k, (x, y, c)).wait_send()
            _rcopy(theirs, theirs, send, recv, k, (x, y, c)).wait_recv()

    return _Rider(bufs, _same(bufs), {k: k for k in range(n)}, n, start, finish)


def _run(name, rider):
    def body(*refs):
        nri, nro = len(rider.ins), len(rider.outs)
        rin, rout = refs[:nri], refs[nri:nri + nro]
        send, recv = refs[nri + nro:]
        rider.start(rin, rout, send, recv)
        rider.finish(rin, rout, send, recv)

    return pl.pallas_call(
        body, name=name, in_specs=[HBM] * len(rider.ins), out_specs=[HBM] * len(rider.outs), out_shape=rider.outs,
        scratch_shapes=[pltpu.SemaphoreType.DMA((rider.n_sems,)), pltpu.SemaphoreType.DMA((rider.n_sems,))],
        input_output_aliases=rider.aliases,
    )(*rider.ins)


def _add_halves(name, g, other, c_idx):
    s, r, cols = g.shape
    half = r // 2
    tr = _div_tile(half, 16, 512)
    nb = half // tr

    def body(c_ref, g_ref, o_ref, q_ref):
        del c_ref
        q_ref[...] = (g_ref[...] + o_ref[...]).astype(BF16)

    return pl.pallas_call(
        body, name=name,
        grid_spec=pltpu.PrefetchScalarGridSpec(
            num_scalar_prefetch=1, grid=(s, nb),
            in_specs=[pl.BlockSpec((None, tr, cols), lambda k, i, c: (k, c[0] * nb + i, 0)),
                      pl.BlockSpec((None, tr, cols), lambda k, i, c: (k, i, 0))],
            out_specs=pl.BlockSpec((None, tr, cols), lambda k, i, c: (k, i, 0))),
        out_shape=jax.ShapeDtypeStruct((s, half, cols), BF16),
        compiler_params=_params("parallel", "parallel"),
    )(c_idx, g, other)


def _sum_partials(name, part, arrived, place_idx):
    _, half, cols = part.shape
    tr = _div_tile(half, 16, 512)
    nb = half // tr

    def body(s_ref, p_ref, a_ref, o_ref):
        del s_ref
        o_ref[...] = ((p_ref[...].astype(F32) + a_ref[0].astype(F32)) + a_ref[1].astype(F32)) + a_ref[2].astype(F32)

    return pl.pallas_call(
        body, name=name,
        grid_spec=pltpu.PrefetchScalarGridSpec(
            num_scalar_prefetch=1, grid=(nb,),
            in_specs=[pl.BlockSpec((None, tr, cols), lambda i, s: (s[0], i, 0)),
                      pl.BlockSpec((3, tr, cols), lambda i, s: (0, i, 0))],
            out_specs=pl.BlockSpec((tr, cols), lambda i, s: (s[1] * nb + i, 0))),
        out_shape=jax.ShapeDtypeStruct((2 * half, cols), F32),
        compiler_params=_params("parallel"),
    )(place_idx, part, arrived)


def _small_allreduce(wide_rows, ffn_rows, w, dff, n_wide, n_ffn):
    n_in = len(wide_rows) + len(ffn_rows)

    def body(*refs):
        ins = refs[:n_in]
        s1_ref, s2_ref, r1, r2, p1, p2, send, recv = refs[n_in:]
        x, y, c, _ = _place()
        me = 4 * x + 2 * y + c
        p1[...] = jnp.zeros_like(p1)
        p2[...] = jnp.zeros_like(p2)
        row = 0
        for ref, (_, r, m) in zip(ins, wide_rows):
            if m == 1 and r % SUBLANE == 0 and row % SUBLANE == 0:
                p1[row:row + r, :] = ref[...]
                row += r
                continue
            for rr in range(r):
                for mm in range(m):
                    p1[row:row + 1, :] = ref[rr:rr + 1, mm * w:(mm + 1) * w]
                    row += 1
        row = 0
        for ref, arr in zip(ins[len(wide_rows):], ffn_rows):
            r = arr.shape[0]
            p2[row:row + r, :] = ref[...]
            row += r
        r1[me] = p1[...]
        r2[me] = p2[...]
        cps = []
        for mask in range(1, 8):
            peer = (x ^ (mask >> 2), y ^ ((mask >> 1) & 1), c ^ (mask & 1))
            for a, (src, dst) in enumerate(((p1, r1), (p2, r2))):
                cp = pltpu.make_async_remote_copy(
                    src_ref=src, dst_ref=dst.at[me], send_sem=send.at[a, mask - 1], recv_sem=recv.at[a, mask - 1],
                    device_id=peer, device_id_type=MESH)
                cp.start()
                cps.append(cp)
        for cp in cps:
            cp.wait()
        t1, t2 = r1[0], r2[0]
        for d in range(1, 8):
            t1 = t1 + r1[d]
            t2 = t2 + r2[d]
        s1_ref[...] = t1
        s2_ref[...] = t2

    ins = [a for a, _, _ in wide_rows] + list(ffn_rows)
    return pl.pallas_call(
        body, name="small_allreduce", in_specs=[VMEM_FULL] * n_in, out_specs=[VMEM_FULL, VMEM_FULL],
        out_shape=[jax.ShapeDtypeStruct((n_wide, w), F32), jax.ShapeDtypeStruct((n_ffn, dff), F32)],
        scratch_shapes=[pltpu.VMEM((8, n_wide, w), F32), pltpu.VMEM((8, n_ffn, dff), F32),
                        pltpu.VMEM((n_wide, w), F32), pltpu.VMEM((n_ffn, dff), F32),
                        pltpu.SemaphoreType.DMA((2, 7)), pltpu.SemaphoreType.DMA((2, 7))],
        compiler_params=pltpu.CompilerParams(vmem_limit_bytes=VMEM_LIMIT),
    )(*ins)


def _adamw(w, g, m, v):
    m2 = ADAM_B1 * m + (1.0 - ADAM_B1) * g
    v2 = ADAM_B2 * v + (1.0 - ADAM_B2) * (g * g)
    m_hat = m2 / (1.0 - ADAM_B1 ** ADAM_STEP)
    v_hat = v2 / (1.0 - ADAM_B2 ** ADAM_STEP)
    delta = -ADAM_LR * (m_hat / (jnp.sqrt(v_hat) + ADAM_EPS) + ADAM_WD * w)
    return delta, m2, v2


def _adam_big(name, w, g, m, v):
    r, c = w.shape
    tr = 128 if r % 128 == 0 else r

    def body(w_ref, g_ref, m_ref, v_ref, d_ref, m2_ref, v2_ref):
        d_ref[...], m2_ref[...], v2_ref[...] = _adamw(w_ref[...], g_ref[...], m_ref[...], v_ref[...])

    blk = pl.BlockSpec((tr, c), lambda i: (i, 0))
    return pl.pallas_call(
        body, name=name, grid=(r // tr,), in_specs=[blk] * 4, out_specs=[blk] * 3,
        out_shape=[jax.ShapeDtypeStruct((r, c), F32)] * 3,
        compiler_params=_params("parallel"),
    )(w, g, m, v)


def _adam_small(s1, s2, cw_g, fw_g, lb_logits, triples, layout, w):
    n = len(triples)

    def body(*refs):
        s1_ref, s2_ref, cw_ref, fw_ref, lbl_ref = refs[:5]
        prm = refs[5:5 + 3 * n]
        outs = refs[5 + 3 * n:]
        for p, lay in enumerate(layout):
            w_ref, m_ref, v_ref = prm[3 * p:3 * p + 3]
            g_ref, d_ref, m2_ref, v2_ref = outs[4 * p:4 * p + 4]
            if lay[0] == "wide":
                _, row, r, pieces = lay
                for rr in range(r):
                    for mm in range(pieces):
                        g_ref[rr:rr + 1, mm * w:(mm + 1) * w] = s1_ref[row:row + 1, :]
                        row += 1
            elif lay[0] == "ffn":
                _, row, r = lay
                g_ref[...] = s2_ref[row:row + r, :]
            elif lay[0] == "cw":
                g_ref[...] = cw_ref[0:g_ref.shape[0], :]
            elif lay[0] == "fw":
                g_ref[...] = fw_ref[0:g_ref.shape[0], :]
            else:
                s0 = _lower_bound(lbl_ref)
                d0 = s1_ref[lay[1]:lay[1] + 1, :] * s0 * (1.0 - s0)
                g_ref[0:1, :] = d0
                g_ref[1:2, :] = -d0
            d_ref[...], m2_ref[...], v2_ref[...] = _adamw(w_ref[...], g_ref[...], m_ref[...], v_ref[...])

    flat = [a for tr in triples for a in tr]
    shapes = []
    for tr in triples:
        shapes.extend([jax.ShapeDtypeStruct(tr[0].shape, F32)] * 4)
    return pl.pallas_call(
        body, name="adam_small", in_specs=[VMEM_FULL] * (5 + 3 * n), out_specs=[VMEM_FULL] * (4 * n),
        out_shape=shapes, compiler_params=pltpu.CompilerParams(vmem_limit_bytes=VMEM_LIMIT),
    )(s1, s2, cw_g, fw_g, lb_logits, *flat)


def _row_tile(t):
    return 512 if t % 512 == 0 and t >= 2048 else 128


def kernel(x, emb_ln_g, emb_ln_b, w_in, conv_w, conv_b, conv_norm_g, conv_norm_b, lb_logits, hgrn_norm_g, w_out, ln1_g, ln1_b, w_ffn_up, ffn_conv_w, ffn_conv_b, w_ffn_down, ln2_g, ln2_b, loss_target, m_emb_ln_g, m_emb_ln_b, m_w_in, m_conv_w, m_conv_b, m_conv_norm_g, m_conv_norm_b, m_lb_logits, m_hgrn_norm_g, m_w_out, m_ln1_g, m_ln1_b, m_w_ffn_up, m_ffn_conv_w, m_ffn_conv_b, m_w_ffn_down, m_ln2_g, m_ln2_b, v_emb_ln_g, v_emb_ln_b, v_w_in, v_conv_w, v_conv_b, v_conv_norm_g, v_conv_norm_b, v_lb_logits, v_hgrn_norm_g, v_w_out, v_ln1_g, v_ln1_b, v_w_ffn_up, v_ffn_conv_w, v_ffn_conv_b, v_w_ffn_down, v_ln2_g, v_ln2_b):
    depth = w_in.shape[0]
    assert depth == 1 and x.shape[0] == 1
    alpha = (2.0 * depth) ** 0.25
    t, d = x.shape[1], x.shape[2]
    w = d // 2
    dff = ffn_conv_b.shape[1]
    kc = conv_w.shape[1]
    assert w % (2 * LANE) == 0 and dff % (4 * LANE) == 0 and t % 128 == 0
    tm = _row_tile(t)
    tm2 = tm // 2
    tmm = 1024 if t % 1024 == 0 and t >= 2048 else tm
    cb = 2 * LANE
    cbf = 4 * LANE
    tb = tm
    nh = w // LANE
    hpb = 4 if nh % 4 == 0 else 2

    xi = lax.axis_index("x")
    yi = lax.axis_index("y")
    ci = lax.axis_index("c")
    chip = 2 * xi + yi
    c_idx = jnp.reshape(ci, (1,)).astype(jnp.int32)
    chip_idx = jnp.reshape(chip, (1,)).astype(jnp.int32)
    place_idx = jnp.stack([chip, ci]).astype(jnp.int32)

    x2 = x[0]
    tgt = loss_target[0]
    g0, b0 = emb_ln_g.reshape(1, d), emb_ln_b.reshape(1, d)
    w_in2, w_out2, w_up2, w_dn2 = w_in[0], w_out[0], w_ffn_up[0], w_ffn_down[0]
    cw2, fw2 = conv_w[0], ffn_conv_w[0]

    b_in = _place_shard(w_in2, "place_w_in", chip_idx, BF16)
    b_out = _place_shard(w_out2, "place_w_out", chip_idx, BF16)
    b_up = _place_shard(w_up2, "place_w_up", chip_idx, BF16)
    b_dn = _place_shard(w_dn2, "place_w_down", chip_idx, BF16)
    b_cw = _place_shard(_pad_rows(cw2), "place_conv_w", chip_idx, F32)
    b_fw = _place_shard(_pad_rows(fw2), "place_ffn_conv_w", chip_idx, F32)
    first = _run("gather_first_ici", _ride_gather_ici([b_in, b_cw, b_fw]))
    w_in3, cw_full3, fw_full3 = _run("gather_first_d2d", _ride_gather_d2d(first))
    cw_full = _unshard_cols(cw_full3)[:kc]
    fw_full = _unshard_cols(fw_full3)[:fw2.shape[0]]

    h0b = _ln0(x2, g0, b0, tm)
    p3, (b_out, b_up) = _proj("in_proj", h0b, w_in3, 6, 2 * tmm if t % (2 * tmm) == 0 else tmm, w // 2,
                              rider=_ride_gather_ici([b_out, b_up], [None, (0, 1, 4)]))
    (cat, u1), (w_out3, b_up) = _conv_fwd(
        p3, cw_full, conv_b, conv_norm_g, conv_norm_b, tm2, cb,
        rider=_ride_both(_ride_gather_d2d([b_out]), _ride_gather_ici([b_up], [(1, 2, 4)])))
    w_out_full = w_out3.reshape(d, d)
    (cat, o_pre, states), got = _hgrn_fwd(p3, lb_logits, hgrn_norm_g, cat, tb, hpb,
                                          rider=_ride_gather_ici([b_up], [(2, 4, 4)]))
    (xhat1, h1b, rstd1), (w_up3,) = _mix_ln1(cat, w_out_full, x2, g0, b0, ln1_g, ln1_b, alpha, tm2,
                                             rider=_ride_gather_d2d(got))
    hh3, got = _proj("ffn_up", h1b, w_up3, 2, tmm, dff // 4, rider=_ride_gather_ici([b_dn]))
    act, (w_dn3,) = _ffn_act_fwd(hh3, fw_full, ffn_conv_b, tm, cbf, rider=_ride_gather_d2d(got))
    ks = dff // N_CHIPS
    ffn = _wgrad("ffn_down", act, w_dn3, (t, d), (t // tmm, 1, N_CHIPS),
                 pl.BlockSpec((tmm, ks), lambda i, j, k: (i, k)),
                 pl.BlockSpec((None, ks, d), lambda i, j, k: (k, 0, 0)),
                 pl.BlockSpec((tmm, d), lambda i, j, k: (i, 0)), dot=_dot)
    dz2, dz2b, dg2, db2, loss_row = _ln2_loss(ffn, xhat1, tgt, ln1_g, ln1_b, ln2_g, ln2_b, alpha, tm2)

    dact = _proj_t("ffn_down_t", dz2b, w_dn3.reshape(dff, d), tmm, ks)
    dhh3, dfw, dfb = _ffn_act_bwd(dact, hh3, fw_full, ffn_conv_b, tm, cbf)
    tt = tmm
    d_w_dn = _wgrad("wgrad_down", act, dz2b, (N_CHIPS, ks, d), (N_CHIPS, 2, t // tt),
                    pl.BlockSpec((tt, ks), lambda s, j, k: (k, s)),
                    pl.BlockSpec((tt, d // 2), lambda s, j, k: (k, j)),
                    pl.BlockSpec((None, ks, d // 2), lambda s, j, k: (s, 0, j)))
    wu = 2 * dff // N_CHIPS
    tnu = wu // 2
    per_sec_u = dff // tnu
    pre1, (arr_dn,) = _wgrad(
        "up_t", dhh3, w_up3, (t, d), (t // tmm, 1, 2 * N_CHIPS),
        pl.BlockSpec((None, tmm, tnu), lambda i, j, k: (k // per_sec_u, i, k % per_sec_u)),
        pl.BlockSpec((None, d, tnu), lambda i, j, k: (k // 2, 0, k % 2)),
        pl.BlockSpec((tmm, d), lambda i, j, k: (i, 0)), dot=_dot_nt, rider=_ride_swap([d_w_dn]))
    dz1, dz1b, dg1, db1 = _ln1_bwd(pre1, dz2, xhat1, rstd1, ln1_g, alpha, tm2)
    part_dn = _add_halves("add_halves_w_down", d_w_dn, arr_dn, c_idx)
    d_w_up, (land_dn,) = _wgrad(
        "wgrad_up", h1b, dhh3, (N_CHIPS, d, wu), (N_CHIPS, 2, 2, t // tt),
        pl.BlockSpec((tt, d // 2), lambda s, r, j, k: (k, r)),
        pl.BlockSpec((None, tt, tnu), lambda s, r, j, k: ((2 * s + j) // per_sec_u, k, (2 * s + j) % per_sec_u)),
        pl.BlockSpec((None, d // 2, tnu), lambda s, r, j, k: (s, r, j)), rider=_ride_send_partials([part_dn]))
    dcat = _proj_t("out_proj_t", dz1b, w_out_full, tmm, d // 2)
    d_w_out = _wgrad("wgrad_out", cat, dz1b, (d, d), (2, 2, t // tt),
                     pl.BlockSpec((tt, d // 2), lambda r, j, k: (k, r)),
                     pl.BlockSpec((tt, d // 2), lambda r, j, k: (k, j)),
                     pl.BlockSpec((d // 2, d // 2), lambda r, j, k: (r, j))).reshape(N_CHIPS, d // N_CHIPS, d)
    du1, dcng, dcnb = _conv_norm_bwd(dcat, u1, conv_norm_g, conv_norm_b, tm)
    (dp3, dcw, dcb), (arr_up, arr_out) = _conv_bwd(du1, p3, cw_full, tm2, cb, rider=_ride_swap([d_w_up, d_w_out]))
    part_up = _add_halves("add_halves_w_up", d_w_up, arr_up, c_idx)
    part_out = _add_halves("add_halves_w_out", d_w_out, arr_out, c_idx)
    (dp3, dlb, dhg), (land_up, land_out) = _hgrn_bwd(p3, lb_logits, hgrn_norm_g, o_pre, states, dcat, dp3, tb, hpb,
                                                    rider=_ride_send_partials([part_up, part_out]))
    wi = 6 * w // N_CHIPS
    tni = w // 2
    d_w_in = _wgrad("wgrad_in", h0b, dp3, (N_CHIPS, d, wi), (N_CHIPS, 2, wi // tni, t // tt),
                    pl.BlockSpec((tt, d // 2), lambda s, r, j, k: (k, r)),
                    pl.BlockSpec((None, tt, tni), lambda s, r, j, k: (((wi // tni) * s + j) // 2, k, ((wi // tni) * s + j) % 2)),
                    pl.BlockSpec((None, d // 2, tni), lambda s, r, j, k: (s, r, j)))
    (arr_in,) = _run("swap_w_in", _ride_swap([d_w_in]))
    part_in = _add_halves("add_halves_w_in", d_w_in, arr_in, c_idx)
    per_shard_i = wi // tni
    pre0, (land_in,) = _wgrad(
        "in_t", dp3, w_in3, (t, d), (t // tmm, 1, 2 * dp3.shape[0]),
        pl.BlockSpec((None, tmm, tni), lambda i, j, k: (k // 2, i, k % 2)),
        pl.BlockSpec((None, d, tni), lambda i, j, k: (k // per_shard_i, 0, k % per_shard_i)),
        pl.BlockSpec((tmm, d), lambda i, j, k: (i, 0)), dot=_dot_nt, rider=_ride_send_partials([part_in]))
    dx, dg0, db0 = _ln0_bwd(pre0, dz1, x2, g0, alpha, tm2)
    halves = [_sum_partials("sum_partials_" + nm, p, a, place_idx)
              for nm, p, a in (("w_in", part_in, land_in), ("w_out", part_out, land_out),
                               ("w_up", part_up, land_up), ("w_down", part_dn, land_dn))]
    g_w_in, g_w_out, g_w_up, g_w_dn = _run("join_halves", _ride_join(halves))

    kpad = dcw.shape[0]
    wide = [(dcw, kpad, 1), (dg0, 1, 2), (db0, 1, 2), (dg1, 1, 2), (db1, 1, 2), (dg2, 1, 2), (db2, 1, 2),
            (dcb, 1, 1), (dcng, 1, 1), (dcnb, 1, 1), (dlb, 1, 1), (dhg, 1, 1)]
    n_wide = sum(r * m for _, r, m in wide)
    n_wide_pad = -(-n_wide // SUBLANE) * SUBLANE
    s1, s2 = _small_allreduce(wide, [dfw, dfb], w, dff, n_wide_pad, 2 * SUBLANE)
    cw_g = lax.dynamic_slice_in_dim(s1[0:kpad], chip * (w // N_CHIPS), w // N_CHIPS, axis=1)
    fw_g = lax.dynamic_slice_in_dim(s2[0:SUBLANE], chip * (dff // N_CHIPS), dff // N_CHIPS, axis=1)

    small = [
        (g0, m_emb_ln_g.reshape(1, d), v_emb_ln_g.reshape(1, d)), (b0, m_emb_ln_b.reshape(1, d), v_emb_ln_b.reshape(1, d)),
        (cw2, m_conv_w[0], v_conv_w[0]), (conv_b, m_conv_b, v_conv_b),
        (conv_norm_g, m_conv_norm_g, v_conv_norm_g), (conv_norm_b, m_conv_norm_b, v_conv_norm_b),
        (lb_logits, m_lb_logits, v_lb_logits), (hgrn_norm_g, m_hgrn_norm_g, v_hgrn_norm_g),
        (ln1_g, m_ln1_g, v_ln1_g), (ln1_b, m_ln1_b, v_ln1_b),
        (fw2, m_ffn_conv_w[0], v_ffn_conv_w[0]), (ffn_conv_b, m_ffn_conv_b, v_ffn_conv_b),
        (ln2_g, m_ln2_g, v_ln2_g), (ln2_b, m_ln2_b, v_ln2_b),
    ]
    r0 = kpad
    layout = [("wide", r0, 1, 2), ("wide", r0 + 2, 1, 2), ("cw",), ("wide", r0 + 12, 1, 1), ("wide", r0 + 13, 1, 1),
              ("wide", r0 + 14, 1, 1), ("lb", r0 + 15), ("wide", r0 + 16, 1, 1), ("wide", r0 + 4, 1, 2),
              ("wide", r0 + 6, 1, 2), ("fw",), ("ffn", SUBLANE, 1), ("wide", r0 + 8, 1, 2), ("wide", r0 + 10, 1, 2)]
    so = _adam_small(s1, s2, cw_g, fw_g, lb_logits, small, layout, w)
    sm = {nm: so[4 * i:4 * i + 4] for i, nm in enumerate(
        ["emb_ln_g", "emb_ln_b", "conv_w", "conv_b", "conv_norm_g", "conv_norm_b", "lb_logits", "hgrn_norm_g",
         "ln1_g", "ln1_b", "ffn_conv_w", "ffn_conv_b", "ln2_g", "ln2_b"])}
    bigs = {}
    for nm, wt, g, m, v in (("w_in", w_in2, g_w_in, m_w_in[0], v_w_in[0]), ("w_out", w_out2, g_w_out, m_w_out[0], v_w_out[0]),
                            ("w_ffn_up", w_up2, g_w_up, m_w_ffn_up[0], v_w_ffn_up[0]),
                            ("w_ffn_down", w_dn2, g_w_dn, m_w_ffn_down[0], v_w_ffn_down[0])):
        bigs[nm] = (g,) + tuple(_adam_big("adam_" + nm, wt, g, m, v))

    loss = lax.psum(loss_row[0, 0], ("x", "y", "c"))

    order = ["emb_ln_g", "emb_ln_b", "w_in", "conv_w", "conv_b", "conv_norm_g", "conv_norm_b", "lb_logits",
             "hgrn_norm_g", "w_out", "ln1_g", "ln1_b", "w_ffn_up", "ffn_conv_w", "ffn_conv_b", "w_ffn_down",
             "ln2_g", "ln2_b"]
    shapes = dict(emb_ln_g=emb_ln_g.shape, emb_ln_b=emb_ln_b.shape, w_in=w_in.shape, conv_w=conv_w.shape,
                  w_out=w_out.shape, w_ffn_up=w_ffn_up.shape, ffn_conv_w=ffn_conv_w.shape, w_ffn_down=w_ffn_down.shape)
    outs = [loss, dx.reshape(x.shape)]
    for which in range(4):
        for nm in order:
            a = bigs[nm][which] if nm in bigs else sm[nm][which]
            outs.append(a.reshape(shapes[nm]) if nm in shapes else a)
    return tuple(outs)


def _pad_rows(a):
    k = a.shape[0]
    kp = -(-k // 16) * 16
    return jnp.pad(a, ((0, kp - k), (0, 0)))


def _unshard_cols(a3):
    s, k, c = a3.shape
    return jnp.transpose(a3, (1, 0, 2)).reshape(k, s * c)
```

```python
import functools

import jax
import jax.numpy as jnp
from jax import lax
from jax.experimental import pallas as pl
from jax.experimental.pallas import tpu as pltpu

F32 = jnp.float32
BF16 = jnp.bfloat16

LN_EPS = 1e-5
RMS_EPS = 1e-6
LANE = 128
SUBLANE = 8
CHUNK = 64
SUB = 8
HALO = 32
FHALO = 8
ROWS = 64
N_CHIPS = 4
VMEM_LIMIT = 56 << 20
NEG_BIG = -1e30

ADAM_LR = 0.001
ADAM_B1 = 0.9
ADAM_B2 = 0.999
ADAM_EPS = 1e-08
ADAM_WD = 0.01
ADAM_STEP = 10

MESH = pl.DeviceIdType.MESH
HBM = pl.BlockSpec(memory_space=pl.ANY)
VMEM_FULL = pl.BlockSpec(memory_space=pltpu.VMEM)


def _params(*sem):
    return pltpu.CompilerParams(dimension_semantics=sem, vmem_limit_bytes=VMEM_LIMIT)


class _Rider:
    def __init__(self, ins, outs, aliases, n_sems, start, finish):
        self.ins, self.outs, self.aliases = list(ins), list(outs), dict(aliases)
        self.n_sems, self.start, self.finish = n_sems, start, finish


def _call(body, args, *, name, grid, in_specs, out_specs, out_shape, scratch_shapes=(), aliases=None, rider=None):
    many = isinstance(out_shape, (list, tuple))
    shapes = list(out_shape) if many else [out_shape]
    ospecs = list(out_specs) if many else [out_specs]
    aliases = dict(aliases or {})
    sem = ("arbitrary",) * len(grid)
    if rider is None:
        res = pl.pallas_call(
            body, name=name, grid=grid, in_specs=list(in_specs), out_specs=ospecs, out_shape=shapes,
            scratch_shapes=list(scratch_shapes), input_output_aliases=aliases, compiler_params=_params(*sem))(*args)
        return res if many else res[0]
    n_in, n_out, n_scr = len(args), len(shapes), len(scratch_shapes)
    nri, nro = len(rider.ins), len(rider.outs)

    def wrapped(*refs):
        ins, rin = refs[:n_in], refs[n_in:n_in + nri]
        o0 = n_in + nri
        outs, rout = refs[o0:o0 + n_out], refs[o0 + n_out:o0 + n_out + nro]
        s0 = o0 + n_out + nro
        scr, (send, recv) = refs[s0:s0 + n_scr], refs[s0 + n_scr:]
        ids = [pl.program_id(a) for a in range(len(grid))]
        first = functools.reduce(jnp.logical_and, [i == 0 for i in ids])
        last = functools.reduce(jnp.logical_and, [i == g - 1 for i, g in zip(ids, grid)])

        @pl.when(first)
        def _():
            rider.start(rin, rout, send, recv)

        body(*ins, *outs, *scr)

        @pl.when(last)
        def _():
            rider.finish(rin, rout, send, recv)

    for ri, ro in rider.aliases.items():
        aliases[n_in + ri] = n_out + ro
    res = pl.pallas_call(
        wrapped, name=name, grid=grid, in_specs=list(in_specs) + [HBM] * nri, out_specs=ospecs + [HBM] * nro,
        out_shape=shapes + rider.outs,
        scratch_shapes=list(scratch_shapes) + [pltpu.SemaphoreType.DMA((rider.n_sems,)),
                                               pltpu.SemaphoreType.DMA((rider.n_sems,))],
        input_output_aliases=aliases, compiler_params=_params(*sem))(*args, *rider.ins)
    main, extra = res[:n_out], list(res[n_out:])
    return (list(main) if many else main[0]), extra


def _div_tile(n, mult, cap):
    best = n
    for t in range(mult, min(n, cap) + 1, mult):
        if n % t == 0:
            best = t
    return best


def _sigmoid(x):
    return 1.0 / (1.0 + jnp.exp(-x))


def _ln_stats(x):
    mu = jnp.mean(x, axis=-1, keepdims=True)
    xc = x - mu
    var = jnp.mean(xc * xc, axis=-1, keepdims=True)
    rstd = lax.rsqrt(var + LN_EPS)
    return xc * rstd, rstd


def _ln_bwd(dy, xhat, rstd, g):
    dyg = dy * g
    m1 = jnp.mean(dyg, axis=-1, keepdims=True)
    m2 = jnp.mean(dyg * xhat, axis=-1, keepdims=True)
    return rstd * (dyg - m1 - xhat * m2)


def _dot_nt(a, b):
    return lax.dot_general(a, b, (((1,), (1,)), ((), ())), preferred_element_type=F32)


def _dot_tn(a, b):
    return lax.dot_general(a, b, (((0,), (0,)), ((), ())), preferred_element_type=F32)


def _dot(a, b):
    return jnp.dot(a, b, preferred_element_type=F32)


def _dot3(m, x):
    mb = m.astype(BF16)
    x1 = x.astype(BF16)
    r1 = x - x1.astype(F32)
    x2 = r1.astype(BF16)
    x3 = (r1 - x2.astype(F32)).astype(BF16)
    return _dot(mb, x1) + _dot(mb, x2) + _dot(mb, x3)


def _place_shard(x, name, chip_idx, dtype):
    r, c = x.shape
    tr = _div_tile(r, 16, 512)

    def body(s_ref, x_ref, o_ref):
        del s_ref
        o_ref[...] = x_ref[...].astype(dtype)

    return pl.pallas_call(
        body, name=name,
        grid_spec=pltpu.PrefetchScalarGridSpec(
            num_scalar_prefetch=1, grid=(r // tr,),
            in_specs=[pl.BlockSpec((tr, c), lambda i, s: (i, 0))],
            out_specs=pl.BlockSpec((None, tr, c), lambda i, s: (s[0], i, 0))),
        out_shape=jax.ShapeDtypeStruct((N_CHIPS, r, c), dtype),
        compiler_params=_params("parallel"),
    )(chip_idx, x)


def _ln0(x, g, b, tm):
    t, d = x.shape

    def body(x_ref, g_ref, b_ref, o_ref):
        xh, _ = _ln_stats(x_ref[...])
        o_ref[...] = (xh * g_ref[...] + b_ref[...]).astype(BF16)

    row = pl.BlockSpec((1, d), lambda i: (0, 0))
    return pl.pallas_call(
        body, name="ln0", grid=(t // tm,),
        in_specs=[pl.BlockSpec((tm, d), lambda i: (i, 0)), row, row],
        out_specs=pl.BlockSpec((tm, d), lambda i: (i, 0)),
        out_shape=jax.ShapeDtypeStruct((t, d), BF16),
        compiler_params=_params("parallel"),
    )(x, g, b)


def _proj(name, a, w3, n_sec, tm, tn, rider=None):
    m, k = a.shape
    s, _, ws = w3.shape
    sec_w = s * ws // n_sec
    nj = ws // tn
    per_sec = sec_w // tn

    def body(a_ref, w_ref, o_ref):
        o_ref[...] = _dot(a_ref[...], w_ref[...])

    return _call(
        body, (a, w3), name=name, grid=(s * nj, m // tm),
        in_specs=[pl.BlockSpec((tm, k), lambda j, i: (i, 0)),
                  pl.BlockSpec((None, k, tn), lambda j, i: (j // nj, 0, j % nj))],
        out_specs=pl.BlockSpec((None, tm, tn), lambda j, i: (j // per_sec, i, j % per_sec)),
        out_shape=jax.ShapeDtypeStruct((n_sec, m, sec_w), F32), rider=rider)


def _proj_t(name, a, w, tm, tn, rider=None):
    m, k = a.shape
    n = w.shape[0]

    def body(a_ref, w_ref, o_ref):
        o_ref[...] = _dot_nt(a_ref[...], w_ref[...])

    return _call(
        body, (a, w), name=name, grid=(n // tn, m // tm),
        in_specs=[pl.BlockSpec((tm, k), lambda j, i: (i, 0)),
                  pl.BlockSpec((tn, k), lambda j, i: (j, 0))],
        out_specs=pl.BlockSpec((tm, tn), lambda j, i: (i, j)),
        out_shape=jax.ShapeDtypeStruct((m, n), F32), rider=rider)


def _wgrad(name, a, b, out_shape, grid, a_spec, b_spec, o_spec, rider=None, dot=_dot_tn):
    nt = len(grid) - 1

    def body(a_ref, b_ref, o_ref):
        t = pl.program_id(nt)
        prod = dot(a_ref[...], b_ref[...])

        @pl.when(t == 0)
        def _():
            o_ref[...] = prod

        @pl.when(t > 0)
        def _():
            o_ref[...] += prod

    return _call(
        body, (a, b), name=name, grid=grid, in_specs=[a_spec, b_spec], out_specs=o_spec,
        out_shape=jax.ShapeDtypeStruct(out_shape, F32), rider=rider)


def _mix_ln1(cat, w_out, x, g0, b0, g1, b1, alpha, tm, rider=None):
    t, d = x.shape

    def body(cat_ref, w_ref, x_ref, g0_ref, b0_ref, g1_ref, b1_ref, xh_ref, h1b_ref, rstd_ref):
        mix = _dot(cat_ref[...], w_ref[...])
        xh0, _ = _ln_stats(x_ref[...])
        z1 = alpha * (xh0 * g0_ref[...] + b0_ref[...]) + mix
        xh1, rstd1 = _ln_stats(z1)
        xh_ref[...] = xh1
        h1b_ref[...] = (xh1 * g1_ref[...] + b1_ref[...]).astype(BF16)
        rstd_ref[...] = rstd1

    row = pl.BlockSpec((1, d), lambda i: (0, 0))
    blk = pl.BlockSpec((tm, d), lambda i: (i, 0))
    return _call(
        body, (cat, w_out, x, g0, b0, g1, b1), name="mix_ln1", grid=(t // tm,),
        in_specs=[blk, pl.BlockSpec((d, d), lambda i: (0, 0)), blk, row, row, row, row],
        out_specs=[blk, blk, pl.BlockSpec((tm, 1), lambda i: (i, 0))],
        out_shape=[jax.ShapeDtypeStruct((t, d), F32), jax.ShapeDtypeStruct((t, d), BF16),
                   jax.ShapeDtypeStruct((t, 1), F32)], rider=rider)


def _ln2_loss(ffn, xhat1, tgt, g1, b1, g2, b2, alpha, tm):
    t, d = xhat1.shape
    ni = t // tm
    inv_d = 1.0 / d

    def body(ffn_ref, xh1_ref, tgt_ref, g1_ref, b1_ref, g2_ref, b2_ref,
             dz2_ref, dz2b_ref, dg2_ref, db2_ref, loss_ref, lrow):
        i = pl.program_id(0)
        h1 = xh1_ref[...] * g1_ref[...] + b1_ref[...]
        xh2, rstd2 = _ln_stats(alpha * h1 + ffn_ref[...])
        g2v = g2_ref[...]
        diff = xh2 * g2v + b2_ref[...] - tgt_ref[...]
        dh2 = diff * inv_d
        sq = jnp.sum(diff * diff, axis=0, keepdims=True)
        dg = jnp.sum(dh2 * xh2, axis=0, keepdims=True)
        db = jnp.sum(dh2, axis=0, keepdims=True)

        @pl.when(i == 0)
        def _():
            lrow[...] = sq
            dg2_ref[...] = dg
            db2_ref[...] = db

        @pl.when(i > 0)
        def _():
            lrow[...] += sq
            dg2_ref[...] += dg
            db2_ref[...] += db

        dz2 = _ln_bwd(dh2, xh2, rstd2, g2v)
        dz2_ref[...] = dz2
        dz2b_ref[...] = dz2.astype(BF16)

        @pl.when(i == ni - 1)
        def _():
            tot = jnp.sum(lrow[...], axis=-1, keepdims=True) * (0.5 * inv_d)
            loss_ref[...] = jnp.broadcast_to(tot, (1, LANE))

    row = pl.BlockSpec((1, d), lambda i: (0, 0))
    blk = pl.BlockSpec((tm, d), lambda i: (i, 0))
    return _call(
        body, (ffn, xhat1, tgt, g1, b1, g2, b2), name="ln2_loss", grid=(ni,),
        in_specs=[blk, blk, blk, row, row, row, row],
        out_specs=[blk, blk, row, row, pl.BlockSpec((1, LANE), lambda i: (0, 0))],
        out_shape=[jax.ShapeDtypeStruct((t, d), F32), jax.ShapeDtypeStruct((t, d), BF16),
                   jax.ShapeDtypeStruct((1, d), F32), jax.ShapeDtypeStruct((1, d), F32),
                   jax.ShapeDtypeStruct((1, LANE), F32)],
        scratch_shapes=[pltpu.VMEM((1, d), F32)])


def _ln1_bwd(pre, dz2, xhat1, rstd1, g1, alpha, tm):
    t, d = dz2.shape

    def body(pre_ref, dz2_ref, xh_ref, rstd_ref, g_ref, dz1_ref, dz1b_ref, dg_ref, db_ref):
        i = pl.program_id(0)
        dh1 = alpha * dz2_ref[...] + pre_ref[...]
        xh = xh_ref[...]
        dg = jnp.sum(dh1 * xh, axis=0, keepdims=True)
        db = jnp.sum(dh1, axis=0, keepdims=True)

        @pl.when(i == 0)
        def _():
            dg_ref[...] = dg
            db_ref[...] = db

        @pl.when(i > 0)
        def _():
            dg_ref[...] += dg
            db_ref[...] += db

        dz1 = _ln_bwd(dh1, xh, rstd_ref[...], g_ref[...])
        dz1_ref[...] = dz1
        dz1b_ref[...] = dz1.astype(BF16)

    row = pl.BlockSpec((1, d), lambda i: (0, 0))
    blk = pl.BlockSpec((tm, d), lambda i: (i, 0))
    return _call(
        body, (pre, dz2, xhat1, rstd1, g1), name="ln1_bwd", grid=(t // tm,),
        in_specs=[blk, blk, blk, pl.BlockSpec((tm, 1), lambda i: (i, 0)), row],
        out_specs=[blk, blk, row, row],
        out_shape=[jax.ShapeDtypeStruct((t, d), F32), jax.ShapeDtypeStruct((t, d), BF16),
                   jax.ShapeDtypeStruct((1, d), F32), jax.ShapeDtypeStruct((1, d), F32)])


def _ln0_bwd(pre, dz1, x, g0, alpha, tm):
    t, d = x.shape

    def body(pre_ref, dz1_ref, x_ref, g_ref, dx_ref, dg_ref, db_ref):
        i = pl.program_id(0)
        dh0 = alpha * dz1_ref[...] + pre_ref[...]
        xh, rstd = _ln_stats(x_ref[...])
        dg = jnp.sum(dh0 * xh, axis=0, keepdims=True)
        db = jnp.sum(dh0, axis=0, keepdims=True)

        @pl.when(i == 0)
        def _():
            dg_ref[...] = dg
            db_ref[...] = db

        @pl.when(i > 0)
        def _():
            dg_ref[...] += dg
            db_ref[...] += db

        dx_ref[...] = _ln_bwd(dh0, xh, rstd, g_ref[...])

    row = pl.BlockSpec((1, d), lambda i: (0, 0))
    blk = pl.BlockSpec((tm, d), lambda i: (i, 0))
    return _call(
        body, (pre, dz1, x, g0), name="ln0_bwd", grid=(t // tm,),
        in_specs=[blk, blk, blk, row], out_specs=[blk, row, row],
        out_shape=[jax.ShapeDtypeStruct((t, d), F32), jax.ShapeDtypeStruct((1, d), F32),
                   jax.ShapeDtypeStruct((1, d), F32)])


def _shift_copies(ext, shifted):
    n = shifted.shape[1]
    for p in range(1, SUBLANE):
        shifted[p - 1] = ext[pl.ds(p, n), :]


def _window(ext, shifted, start, rows):
    p = start % SUBLANE
    if p == 0:
        return ext[pl.ds(start, rows), :]
    return shifted[p - 1, pl.ds(start - p, rows), :]


def _conv_fwd(p3, conv_w, conv_b, cn_g, cn_b, tc, cb, rider=None):
    _, t, w = p3.shape
    kk = conv_w.shape[0]
    off = HALO - (kk - 1)
    hb = tc // HALO

    def body(a_ref, g_ref, ap_ref, gp_ref, w_ref, b_ref, ng_ref, nb_ref, cat_ref, u1_ref, ext, sh):
        i = pl.program_id(1)
        ext[pl.ds(HALO, tc), :] = a_ref[...] * _sigmoid(g_ref[...])
        prev = ap_ref[...] * _sigmoid(gp_ref[...])
        ext[pl.ds(0, HALO), :] = jnp.where(i > 0, prev, 0.0)
        _shift_copies(ext, sh)
        for r in range(tc // ROWS):
            acc = jnp.broadcast_to(b_ref[...], (ROWS, cb))
            for k in range(kk):
                acc = acc + w_ref[k:k + 1, :] * _window(ext, sh, r * ROWS + off + k, ROWS)
            u1_ref[pl.ds(r * ROWS, ROWS), :] = acc
            for g in range(cb // LANE):
                sl = slice(g * LANE, (g + 1) * LANE)
                xh, _ = _ln_stats(acc[:, sl])
                u2 = xh * ng_ref[:, sl] + nb_ref[:, sl]
                cat_ref[pl.ds(r * ROWS, ROWS), sl] = (u2 * _sigmoid(u2)).astype(BF16)

    cur = lambda sec: pl.BlockSpec((None, tc, cb), lambda j, i: (sec, i, j))
    prev = lambda sec: pl.BlockSpec((None, HALO, cb), lambda j, i: (sec, jnp.maximum(i * hb - 1, 0), j))
    row = pl.BlockSpec((1, cb), lambda j, i: (0, j))
    return _call(
        body, (p3, p3, p3, p3, conv_w, conv_b, cn_g, cn_b), name="conv_fwd", grid=(w // cb, t // tc),
        in_specs=[cur(0), cur(1), prev(0), prev(1), pl.BlockSpec((kk, cb), lambda j, i: (0, j)), row, row, row],
        out_specs=[pl.BlockSpec((tc, cb), lambda j, i: (i, j)), pl.BlockSpec((tc, cb), lambda j, i: (i, j))],
        out_shape=[jax.ShapeDtypeStruct((t, 2 * w), BF16), jax.ShapeDtypeStruct((t, w), F32)],
        scratch_shapes=[pltpu.VMEM((tc + HALO, cb), F32),
                        pltpu.VMEM((SUBLANE - 1, tc + HALO - SUBLANE, cb), F32)], rider=rider)


def _conv_norm_bwd(dcat, u1, cn_g, cn_b, tc):
    t, w = u1.shape

    def body(du_ref, u1_ref, ng_ref, nb_ref, du1_ref, dg_ref, db_ref):
        i = pl.program_id(0)
        for g in range(w // LANE):
            sl = slice(g * LANE, (g + 1) * LANE)
            ng = ng_ref[:, sl]
            xh, rstd = _ln_stats(u1_ref[:, sl])
            u2 = xh * ng + nb_ref[:, sl]
            sg = _sigmoid(u2)
            du2 = du_ref[:, sl] * (sg * (1.0 + u2 * (1.0 - sg)))
            dg = jnp.sum(du2 * xh, axis=0, keepdims=True)
            db = jnp.sum(du2, axis=0, keepdims=True)

            @pl.when(i == 0)
            def _():
                dg_ref[:, sl] = dg
                db_ref[:, sl] = db

            @pl.when(i > 0)
            def _():
                dg_ref[:, sl] += dg
                db_ref[:, sl] += db

            du1_ref[:, sl] = _ln_bwd(du2, xh, rstd, ng)

    row = pl.BlockSpec((1, w), lambda i: (0, 0))
    blk = pl.BlockSpec((tc, w), lambda i: (i, 0))
    return pl.pallas_call(
        body, name="conv_norm_bwd", grid=(t // tc,),
        in_specs=[blk, blk, row, row], out_specs=[blk, row, row],
        out_shape=[jax.ShapeDtypeStruct((t, w), F32), jax.ShapeDtypeStruct((1, w), F32),
                   jax.ShapeDtypeStruct((1, w), F32)],
        compiler_params=_params("arbitrary"),
    )(dcat, u1, cn_g, cn_b)


def _conv_bwd(du1, p3, conv_w, tc, cb, rider=None):
    n_sec, t, w = p3.shape
    kk = conv_w.shape[0]
    off = HALO - (kk - 1)
    hb = tc // HALO
    nt = t // tc
    kpad = -(-kk // SUBLANE) * SUBLANE

    def body(d_ref, dn_ref, a_ref, g_ref, ap_ref, gp_ref, w_ref, dp_ref, dw_ref, db_ref,
             extd, extu, shd, shu, wacc, bacc):
        i = pl.program_id(1)

        @pl.when(i == 0)
        def _():
            wacc[...] = jnp.zeros_like(wacc)
            bacc[...] = jnp.zeros_like(bacc)

        extd[pl.ds(0, tc), :] = d_ref[...]
        extd[pl.ds(tc, HALO), :] = jnp.where(i < nt - 1, dn_ref[...], 0.0)
        extu[pl.ds(HALO, tc), :] = a_ref[...] * _sigmoid(g_ref[...])
        extu[pl.ds(0, HALO), :] = jnp.where(i > 0, ap_ref[...] * _sigmoid(gp_ref[...]), 0.0)
        _shift_copies(extd, shd)
        _shift_copies(extu, shu)
        for r in range(tc // ROWS):
            rows = pl.ds(r * ROWS, ROWS)
            acc = jnp.zeros((ROWS, cb), F32)
            for k in range(kk):
                acc = acc + w_ref[k:k + 1, :] * _window(extd, shd, r * ROWS + (kk - 1) - k, ROWS)
            a = a_ref[rows, :]
            sg = _sigmoid(g_ref[rows, :])
            dp_ref[0, rows, :] = (acc * sg).astype(BF16)
            dp_ref[1, rows, :] = (acc * a * sg * (1.0 - sg)).astype(BF16)
            d = d_ref[rows, :]
            bacc[...] += jnp.sum(d.reshape(ROWS // SUBLANE, SUBLANE, cb), axis=0)
            for k in range(kk):
                prod = d * _window(extu, shu, r * ROWS + off + k, ROWS)
                wacc[k] += jnp.sum(prod.reshape(ROWS // SUBLANE, SUBLANE, cb), axis=0)

        @pl.when(i == nt - 1)
        def _():
            for k in range(kk):
                dw_ref[k:k + 1, :] = jnp.sum(wacc[k], axis=0, keepdims=True)
            if kpad > kk:
                dw_ref[kk:kpad, :] = jnp.zeros((kpad - kk, cb), F32)
            db_ref[...] = jnp.sum(bacc[...], axis=0, keepdims=True)

    cur = lambda sec: pl.BlockSpec((None, tc, cb), lambda j, i: (sec, i, j))
    prev = lambda sec: pl.BlockSpec((None, HALO, cb), lambda j, i: (sec, jnp.maximum(i * hb - 1, 0), j))
    return _call(
        body, (du1, du1, p3, p3, p3, p3, conv_w), name="conv_bwd", grid=(w // cb, nt),
        in_specs=[pl.BlockSpec((tc, cb), lambda j, i: (i, j)),
                  pl.BlockSpec((HALO, cb), lambda j, i: (jnp.minimum((i + 1) * hb, t // HALO - 1), j)),
                  cur(0), cur(1), prev(0), prev(1), pl.BlockSpec((kk, cb), lambda j, i: (0, j))],
        out_specs=[pl.BlockSpec((2, tc, cb), lambda j, i: (0, i, j)),
                   pl.BlockSpec((kpad, cb), lambda j, i: (0, j)),
                   pl.BlockSpec((1, cb), lambda j, i: (0, j))],
        out_shape=[jax.ShapeDtypeStruct((n_sec, t, w), BF16), jax.ShapeDtypeStruct((kpad, w), F32),
                   jax.ShapeDtypeStruct((1, w), F32)],
        scratch_shapes=[pltpu.VMEM((tc + HALO, cb), F32), pltpu.VMEM((tc + HALO, cb), F32),
                        pltpu.VMEM((SUBLANE - 1, tc + HALO - SUBLANE, cb), F32),
                        pltpu.VMEM((SUBLANE - 1, tc + HALO - SUBLANE, cb), F32),
                        pltpu.VMEM((kk, SUBLANE, cb), F32), pltpu.VMEM((SUBLANE, cb), F32)], rider=rider)


def _ffn_act_fwd(hh3, fw, fb, tc, cb, rider=None):
    _, t, dff = hh3.shape
    kk = fw.shape[0]
    off = FHALO - (kk - 1)
    hb = tc // FHALO

    def body(g_ref, v_ref, gp_ref, w_ref, b_ref, act_ref, ext):
        i = pl.program_id(1)
        ext[pl.ds(FHALO, tc), :] = g_ref[...]
        ext[pl.ds(0, FHALO), :] = jnp.where(i > 0, gp_ref[...], 0.0)
        for r in range(tc // ROWS):
            rows = pl.ds(r * ROWS, ROWS)
            gc = jnp.broadcast_to(b_ref[...], (ROWS, cb))
            for k in range(kk):
                gc = gc + w_ref[k:k + 1, :] * ext[pl.ds(r * ROWS + off + k, ROWS), :]
            act_ref[rows, :] = (gc * _sigmoid(gc) * v_ref[rows, :]).astype(BF16)

    return _call(
        body, (hh3, hh3, hh3, fw, fb), name="ffn_act_fwd", grid=(dff // cb, t // tc),
        in_specs=[pl.BlockSpec((None, tc, cb), lambda j, i: (0, i, j)),
                  pl.BlockSpec((None, tc, cb), lambda j, i: (1, i, j)),
                  pl.BlockSpec((None, FHALO, cb), lambda j, i: (0, jnp.maximum(i * hb - 1, 0), j)),
                  pl.BlockSpec((kk, cb), lambda j, i: (0, j)),
                  pl.BlockSpec((1, cb), lambda j, i: (0, j))],
        out_specs=pl.BlockSpec((tc, cb), lambda j, i: (i, j)),
        out_shape=jax.ShapeDtypeStruct((t, dff), BF16),
        scratch_shapes=[pltpu.VMEM((tc + FHALO, cb), F32)], rider=rider)


def _ffn_act_bwd(dact, hh3, fw, fb, tc, cb):
    _, t, dff = hh3.shape
    kk = fw.shape[0]
    off = FHALO - (kk - 1)
    hb = tc // FHALO
    nt = t // tc
    te = tc + FHALO

    def body(da_ref, dan_ref, g_ref, gp_ref, gn_ref, v_ref, vn_ref, w_ref, b_ref,
             dhh_ref, dw_ref, db_ref, gext, dext, wacc, bacc):
        i = pl.program_id(1)

        @pl.when(i == 0)
        def _():
            wacc[...] = jnp.zeros_like(wacc)
            bacc[...] = jnp.zeros_like(bacc)

        gext[pl.ds(0, FHALO), :] = jnp.where(i > 0, gp_ref[...], 0.0)
        gext[pl.ds(FHALO, tc), :] = g_ref[...]
        gext[pl.ds(FHALO + tc, FHALO), :] = gn_ref[...]

        def gate_grad(r0, n, da, v):
            gc = jnp.broadcast_to(b_ref[...], (n, cb))
            for k in range(kk):
                gc = gc + w_ref[k:k + 1, :] * gext[pl.ds(r0 + off + k, n), :]
            sg = _sigmoid(gc)
            return gc * sg, da * v * (sg * (1.0 + gc * (1.0 - sg)))

        for r in range(tc // ROWS):
            rows = pl.ds(r * ROWS, ROWS)
            da = da_ref[rows, :]
            silu, dgc = gate_grad(r * ROWS, ROWS, da, v_ref[rows, :])
            dext[rows, :] = dgc
            dhh_ref[1, rows, :] = (da * silu).astype(BF16)
        _, dgc_next = gate_grad(tc, FHALO, dan_ref[...], vn_ref[...])
        dext[pl.ds(tc, FHALO), :] = jnp.where(i < nt - 1, dgc_next, 0.0)
        for r in range(tc // ROWS):
            rows = pl.ds(r * ROWS, ROWS)
            dg = jnp.zeros((ROWS, cb), F32)
            for k in range(kk):
                dg = dg + w_ref[k:k + 1, :] * dext[pl.ds(r * ROWS + (kk - 1) - k, ROWS), :]
            dhh_ref[0, rows, :] = dg.astype(BF16)
            dgc = dext[rows, :]
            bacc[...] += jnp.sum(dgc.reshape(ROWS // SUBLANE, SUBLANE, cb), axis=0)
            for k in range(kk):
                prod = dgc * gext[pl.ds(r * ROWS + off + k, ROWS), :]
                wacc[k] += jnp.sum(prod.reshape(ROWS // SUBLANE, SUBLANE, cb), axis=0)

        @pl.when(i == nt - 1)
        def _():
            for k in range(kk):
                dw_ref[k:k + 1, :] = jnp.sum(wacc[k], axis=0, keepdims=True)
            dw_ref[kk:SUBLANE, :] = jnp.zeros((SUBLANE - kk, cb), F32)
            db_ref[...] = jnp.sum(bacc[...], axis=0, keepdims=True)

    nxt = lambda i: jnp.minimum((i + 1) * hb, t // FHALO - 1)
    return pl.pallas_call(
        body, name="ffn_act_bwd", grid=(dff // cb, nt),
        in_specs=[pl.BlockSpec((tc, cb), lambda j, i: (i, j)),
                  pl.BlockSpec((FHALO, cb), lambda j, i: (nxt(i), j)),
                  pl.BlockSpec((None, tc, cb), lambda j, i: (0, i, j)),
                  pl.BlockSpec((None, FHALO, cb), lambda j, i: (0, jnp.maximum(i * hb - 1, 0), j)),
                  pl.BlockSpec((None, FHALO, cb), lambda j, i: (0, nxt(i), j)),
                  pl.BlockSpec((None, tc, cb), lambda j, i: (1, i, j)),
                  pl.BlockSpec((None, FHALO, cb), lambda j, i: (1, nxt(i), j)),
                  pl.BlockSpec((kk, cb), lambda j, i: (0, j)),
                  pl.BlockSpec((1, cb), lambda j, i: (0, j))],
        out_specs=[pl.BlockSpec((2, tc, cb), lambda j, i: (0, i, j)),
                   pl.BlockSpec((SUBLANE, cb), lambda j, i: (0, j)),
                   pl.BlockSpec((1, cb), lambda j, i: (0, j))],
        out_shape=[jax.ShapeDtypeStruct((2, t, dff), BF16), jax.ShapeDtypeStruct((SUBLANE, dff), F32),
                   jax.ShapeDtypeStruct((1, dff), F32)],
        scratch_shapes=[pltpu.VMEM((tc + 2 * FHALO, cb), F32), pltpu.VMEM((te, cb), F32),
                        pltpu.VMEM((kk, SUBLANE, cb), F32), pltpu.VMEM((SUBLANE, cb), F32)],
        compiler_params=_params("parallel", "arbitrary"),
    )(dact, dact, hh3, hh3, hh3, hh3, hh3, fw, fb)


def _chunk_consts():
    r = lax.broadcasted_iota(jnp.int32, (CHUNK, CHUNK), 0)
    c = lax.broadcasted_iota(jnp.int32, (CHUNK, CHUNK), 1)
    blk = (r // SUB) * SUB
    tri = (c <= r).astype(F32)
    start = (c < blk).astype(F32)
    end = (c < blk + SUB).astype(F32)
    return jnp.concatenate([tri, start, end, jnp.ones((SUBLANE, CHUNK), F32)], axis=0)


def _roll8(x, d):
    return pltpu.roll(x.reshape(CHUNK // SUB, SUB, LANE), d % SUB, 1).reshape(CHUNK, LANE)


def _gate_terms(q, fpre, lb):
    sf = _sigmoid(fpre)
    fg = lb + (1.0 - lb) * sf
    sq = _sigmoid(q)
    return sf, fg, 1.0 - fg, sq, q * sq


def _decays(g, consts):
    cs = _dot3(consts, g)
    b = cs[0:CHUNK]
    rs = cs[CHUNK:2 * CHUNK]
    re = cs[2 * CHUNK:3 * CHUNK]
    tot = cs[3 * CHUNK:3 * CHUNK + 1]
    return b, rs, re, tot


def _lower_bound(lb_ref):
    l0, l1 = lb_ref[0:1, :], lb_ref[1:2, :]
    mx = jnp.maximum(l0, l1)
    e0, e1 = jnp.exp(l0 - mx), jnp.exp(l1 - mx)
    return e0 / (e0 + e1)


def _scaled_keys(kt, rs, re, rowblk, i):
    scale = jnp.where(rowblk < i, jnp.exp(jnp.minimum(rs[SUB * i:SUB * i + 1, :] - re, 0.0)), 0.0)
    return kt * scale, scale


def _hgrn_fwd(p3, lb_logits, hg, cat, tb, hpb, rider=None):
    _, t, w = p3.shape
    nh = w // LANE
    nc = tb // CHUNK
    assert nh % hpb == 0

    def body(q_ref, f_ref, v_ref, og_ref, lb_ref, hg_ref, cat_in, cat_ref, o_ref, st_ref, state):
        del cat_in
        consts = _chunk_consts()
        lb_all = _lower_bound(lb_ref)
        rowblk = lax.broadcasted_iota(jnp.int32, (CHUNK, 1), 0) // SUB
        rowpos = lax.broadcasted_iota(jnp.int32, (CHUNK, 1), 0) % SUB

        @pl.when(pl.program_id(1) == 0)
        def _():
            state[...] = jnp.zeros_like(state)

        def chunk(c, carry):
            rows = pl.ds(pl.multiple_of(c * CHUNK, CHUNK), CHUNK)
            heads = range(hpb)
            sls = [slice(j * LANE, (j + 1) * LANE) for j in heads]
            v = [v_ref[rows, s] for s in sls]
            vb = [x.astype(BF16) for x in v]
            gates = [_gate_terms(q_ref[rows, s], f_ref[rows, s], lb_all[:, s]) for s in sls]
            fg = [g[1] for g in gates]
            kk = [g[2] for g in gates]
            qh = [g[4] for g in gates]
            dec = [_decays(jnp.log(x), consts) for x in fg]
            b = [x[0] for x in dec]
            rs = [x[1] for x in dec]
            re = [x[2] for x in dec]
            tot = [x[3] for x in dec]
            qt = [qh[j] * jnp.exp(b[j] - rs[j]) for j in heads]
            kt = [kk[j] * jnp.exp(re[j] - b[j]) for j in heads]
            st = [state[j] for j in heads]
            for j in heads:
                st_ref[j, c] = st[j]
            a = [jnp.zeros((CHUNK, CHUNK), F32) for _ in heads]
            for i in range(1, CHUNK // SUB):
                for j in heads:
                    ki, _ = _scaled_keys(kt[j], rs[j], re[j], rowblk, i)
                    a[j] = a[j] + _dot_nt(jnp.where(rowblk == i, qt[j], 0.0).astype(BF16), ki.astype(BF16))
            o = [_dot(a[j].astype(BF16), vb[j]) for j in heads]
            o = [o[j] + _dot_nt((qh[j] * jnp.exp(b[j])).astype(BF16), st[j].astype(BF16)) for j in heads]
            for j in heads:
                k_up = kk[j] * jnp.exp(tot[j] - b[j])
                state[j] = st[j] * jnp.exp(tot[j]) + _dot_tn(vb[j], k_up.astype(BF16))
            for j in heads:
                e, rf = None, fg[j]
                for d in range(SUB):
                    if d == 0:
                        vs, term = v[j], qh[j] * kk[j]
                    else:
                        e = rf if e is None else e * rf
                        rf = _roll8(fg[j], d)
                        vs = _roll8(v[j], d)
                        term = jnp.where(rowpos >= d, qh[j] * (1.0 - rf) * e, 0.0)
                    o[j] = o[j] + jnp.sum(term, axis=-1, keepdims=True) * vs
            for j in heads:
                og = og_ref[rows, sls[j]]
                o_ref[rows, sls[j]] = o[j]
                r = lax.rsqrt(jnp.mean(o[j] * o[j], axis=-1, keepdims=True) + RMS_EPS)
                cat_ref[rows, sls[j]] = (o[j] * r * hg_ref[:, sls[j]] * (og * _sigmoid(og))).astype(BF16)
            return carry

        lax.fori_loop(0, nc, chunk, 0)

    bw = hpb * LANE
    sec = lambda s: pl.BlockSpec((None, tb, bw), lambda h, i: (s, i, h))
    return _call(
        body, (p3, p3, p3, p3, lb_logits, hg, cat), name="hgrn_fwd", grid=(nh // hpb, t // tb),
        in_specs=[sec(2), sec(3), sec(4), sec(5),
                  pl.BlockSpec((2, bw), lambda h, i: (0, h)),
                  pl.BlockSpec((1, bw), lambda h, i: (0, h)), HBM],
        out_specs=[pl.BlockSpec((tb, bw), lambda h, i: (i, nh // hpb + h)),
                   pl.BlockSpec((tb, bw), lambda h, i: (i, h)),
                   pl.BlockSpec((hpb, nc, LANE, LANE), lambda h, i: (h, i, 0, 0))],
        out_shape=[jax.ShapeDtypeStruct(cat.shape, BF16), jax.ShapeDtypeStruct((t, w), F32),
                   jax.ShapeDtypeStruct((nh, t // CHUNK, LANE, LANE), F32)],
        scratch_shapes=[pltpu.VMEM((hpb, LANE, LANE), F32)], aliases={6: 0}, rider=rider)


def _hgrn_bwd(p3, lb_logits, hg, o_pre, states, dcat, dp3, tb, hpb, rider=None):
    n_sec, t, w = p3.shape
    nh = w // LANE
    assert nh % hpb == 0
    nc = tb // CHUNK
    nb = t // tb

    def body(q_ref, f_ref, v_ref, og_ref, lb_ref, hg_ref, o_ref, st_ref, dc_ref, dp_in,
             dp_ref, dlb_ref, dhg_ref, dstate, stash, lbacc, hgacc):
        del dp_in
        i, half = pl.program_id(1), pl.program_id(2)

        @pl.when(half == 1)
        def _():
            dp_ref[...] = stash[...]

        @pl.when(half == 0)
        def _():
            consts = _chunk_consts()
            rr = lax.broadcasted_iota(jnp.int32, (CHUNK, CHUNK), 0)
            cc = lax.broadcasted_iota(jnp.int32, (CHUNK, CHUNK), 1)
            upper = (cc >= rr).astype(F32)
            lb_all = _lower_bound(lb_ref)
            rowblk = lax.broadcasted_iota(jnp.int32, (CHUNK, 1), 0) // SUB
            rowpos = lax.broadcasted_iota(jnp.int32, (CHUNK, 1), 0) % SUB

            @pl.when(i == 0)
            def _():
                dstate[...] = jnp.zeros_like(dstate)
                lbacc[...] = jnp.zeros_like(lbacc)
                hgacc[...] = jnp.zeros_like(hgacc)

            def head(j, c, rows):
                sl = slice(j * LANE, (j + 1) * LANE)
                lb = lb_all[:, sl]
                hgv = hg_ref[:, sl]
                q = q_ref[rows, sl]
                v = v_ref[rows, sl]
                og = og_ref[rows, sl]
                o = o_ref[rows, sl]
                dcg = dc_ref[rows, sl]
                sf, fg, kk, sq, qh = _gate_terms(q, f_ref[rows, sl], lb)
                b, rs, re, tot = _decays(jnp.log(fg), consts)
                eq = jnp.exp(b - rs)
                ek = jnp.exp(re - b)
                qt = qh * eq
                kt = kk * ek
                e_in = jnp.exp(b)
                e_up = jnp.exp(tot - b)
                e_tot = jnp.exp(tot)
                q_in = (qh * e_in).astype(BF16)
                k_up = (kk * e_up).astype(BF16)
                vb = v.astype(BF16)
                st = st_ref[j, c]
                dst = dstate[j]
                dstb = dst.astype(BF16)
                yield

                sg = _sigmoid(og)
                r = lax.rsqrt(jnp.mean(o * o, axis=-1, keepdims=True) + RMS_EPS)
                ohat = o * r
                d_og = dcg * ohat * hgv * (sg * (1.0 + og * (1.0 - sg)))
                d_on = dcg * (og * sg)
                hgacc[:, sl] += jnp.sum((d_on * ohat).reshape(CHUNK // SUBLANE, SUBLANE, LANE), axis=0)
                d_oh = d_on * hgv
                do = r * (d_oh - ohat * jnp.mean(d_oh * ohat, axis=-1, keepdims=True))
                dob = do.astype(BF16)

                da = _dot_nt(dob, vb)
                yield
                a_off = jnp.zeros((CHUNK, CHUNK), F32)
                dqt = jnp.zeros((CHUNK, LANE), F32)
                dkt = jnp.zeros((CHUNK, LANE), F32)
                for blk in range(1, CHUNK // SUB):
                    ki, scale = _scaled_keys(kt, rs, re, rowblk, blk)
                    kib = ki.astype(BF16)
                    qib = jnp.where(rowblk == blk, qt, 0.0).astype(BF16)
                    dab = jnp.where(rowblk == blk, da, 0.0).astype(BF16)
                    a_off = a_off + _dot_nt(qib, kib)
                    dqt = dqt + _dot(dab, kib)
                    dkt = dkt + _dot_tn(dab, qib) * scale
                    yield
                dqh = dqt * eq
                dk = dkt * ek
                dv = _dot_tn(a_off.astype(BF16), dob)

                dqh = dqh + _dot(dob, st.astype(BF16)) * e_in
                dk = dk + _dot(vb, dstb) * e_up
                dv = dv + _dot_nt(k_up, dstb)
                st_end = st * e_tot + _dot_tn(vb, k_up)
                carry_g = jnp.sum(st_end * dst, axis=0, keepdims=True)
                dstate[j] = dst * e_tot + _dot_tn(dob, q_in)
                yield

                e, rf = None, fg
                for d in range(SUB):
                    if d == 0:
                        a_d = jnp.sum(qh * kk, axis=-1, keepdims=True)
                        da_d = jnp.sum(do * v, axis=-1, keepdims=True)
                        dqh = dqh + da_d * kk
                        dk = dk + da_d * qh
                        dv = dv + a_d * do
                        continue
                    e = rf if e is None else e * rf
                    rf = _roll8(fg, d)
                    em = jnp.where(rowpos >= d, e, 0.0)
                    ks, vs = 1.0 - rf, _roll8(v, d)
                    a_d = jnp.sum(qh * ks * em, axis=-1, keepdims=True)
                    da_d = jnp.sum(do * vs, axis=-1, keepdims=True) * em
                    dqh = dqh + da_d * ks
                    dk = dk + _roll8(da_d * qh, -d)
                    dv = dv + _roll8(a_d * do, -d)
                yield

                dg = _dot3(upper, qh * dqh - kk * dk) + carry_g
                dfg = dg / fg - dk
                lbacc[:, sl] += jnp.sum((dfg * (1.0 - sf)).reshape(CHUNK // SUBLANE, SUBLANE, LANE), axis=0)
                dp_ref[0, rows, sl] = (dqh * (sq * (1.0 + q * (1.0 - sq)))).astype(BF16)
                dp_ref[1, rows, sl] = (dfg * (1.0 - lb) * sf * (1.0 - sf)).astype(BF16)
                stash[0, rows, sl] = dv.astype(BF16)
                stash[1, rows, sl] = d_og.astype(BF16)

            def chunk(cr, carry):
                c = nc - 1 - cr
                rows = pl.ds(pl.multiple_of(c * CHUNK, CHUNK), CHUNK)
                running = [head(j, c, rows) for j in range(hpb)]
                while running:
                    running = [g for g in running if next(g, StopIteration) is not StopIteration]
                return carry

            lax.fori_loop(0, nc, chunk, 0)

            @pl.when(i == nb - 1)
            def _():
                dlb_ref[...] = jnp.sum(lbacc[...], axis=0, keepdims=True)
                dhg_ref[...] = jnp.sum(hgacc[...], axis=0, keepdims=True)

    rev = lambda i: nb - 1 - i
    bw = hpb * LANE
    sec = lambda s: pl.BlockSpec((None, tb, bw), lambda h, i, z: (s, rev(i), h))
    return _call(
        body, (p3, p3, p3, p3, lb_logits, hg, o_pre, states, dcat, dp3), name="hgrn_bwd", grid=(nh // hpb, nb, 2),
        in_specs=[sec(2), sec(3), sec(4), sec(5),
                  pl.BlockSpec((2, bw), lambda h, i, z: (0, h)),
                  pl.BlockSpec((1, bw), lambda h, i, z: (0, h)),
                  pl.BlockSpec((tb, bw), lambda h, i, z: (rev(i), h)),
                  pl.BlockSpec((hpb, nc, LANE, LANE), lambda h, i, z: (h, rev(i), 0, 0)),
                  pl.BlockSpec((tb, bw), lambda h, i, z: (rev(i), nh // hpb + h)), HBM],
        out_specs=[pl.BlockSpec((2, tb, bw), lambda h, i, z: (1 + z, rev(i), h)),
                   pl.BlockSpec((1, bw), lambda h, i, z: (0, h)),
                   pl.BlockSpec((1, bw), lambda h, i, z: (0, h))],
        out_shape=[jax.ShapeDtypeStruct((n_sec, t, w), BF16), jax.ShapeDtypeStruct((1, w), F32),
                   jax.ShapeDtypeStruct((1, w), F32)],
        scratch_shapes=[pltpu.VMEM((hpb, LANE, LANE), F32), pltpu.VMEM((2, tb, bw), BF16),
                        pltpu.VMEM((SUBLANE, bw), F32), pltpu.VMEM((SUBLANE, bw), F32)],
        aliases={9: 0}, rider=rider)


def _place():
    x, y, c = lax.axis_index("x"), lax.axis_index("y"), lax.axis_index("c")
    chips = [(1 - x, y), (x, 1 - y), (1 - x, 1 - y)]
    return x, y, c, chips


def _rows(buf, px, py, pc, part=None):
    half = buf.shape[1] // 2
    if part is None:
        return buf.at[2 * px + py, pl.ds(pc * half, half)]
    lo, hi, n = part
    piece = half // n
    return buf.at[2 * px + py, pl.ds(pc * half + lo * piece, (hi - lo) * piece)]


def _rcopy(src, dst, send, recv, idx, to):
    return pltpu.make_async_remote_copy(src_ref=src, dst_ref=dst, send_sem=send.at[idx], recv_sem=recv.at[idx],
                                        device_id=to, device_id_type=MESH)


def _same(bufs):
    return [jax.ShapeDtypeStruct(b.shape, b.dtype) for b in bufs]


def _ride_gather_ici(bufs, parts=None):
    n = len(bufs)
    parts = parts or [None] * n

    def start(rin, rout, send, recv):
        x, y, c, chips = _place()
        for k in range(n):
            mine = _rows(rout[k], x, y, c, parts[k])
            for j, chip in enumerate(chips):
                _rcopy(mine, mine, send, recv, 3 * k + j, (*chip, c)).start()

    def finish(rin, rout, send, recv):
        x, y, c, chips = _place()
        for k in range(n):
            for j, chip in enumerate(chips):
                theirs = _rows(rout[k], *chip, c, parts[k])
                _rcopy(theirs, theirs, send, recv, 3 * k + j, (x, y, c)).wait_recv()
        for k in range(n):
            mine = _rows(rout[k], x, y, c, parts[k])
            for j in range(3):
                _rcopy(mine, mine, send, recv, 3 * k + j, (x, y, c)).wait_send()

    return _Rider(bufs, _same(bufs), {k: k for k in range(n)}, 3 * n, start, finish)


class _SemView:
    def __init__(self, ref, base):
        self.ref, self.base = ref, base

    @property
    def at(self):
        return self

    def __getitem__(self, idx):
        return self.ref.at[idx + self.base]


def _ride_both(a, b):
    nai, nao = len(a.ins), len(a.outs)

    def start(rin, rout, send, recv):
        a.start(rin[:nai], rout[:nao], send, recv)
        b.start(rin[nai:], rout[nao:], _SemView(send, a.n_sems), _SemView(recv, a.n_sems))

    def finish(rin, rout, send, recv):
        a.finish(rin[:nai], rout[:nao], send, recv)
        b.finish(rin[nai:], rout[nao:], _SemView(send, a.n_sems), _SemView(recv, a.n_sems))

    aliases = dict(a.aliases)
    aliases.update({nai + ri: nao + ro for ri, ro in b.aliases.items()})
    return _Rider(a.ins + b.ins, a.outs + b.outs, aliases, a.n_sems + b.n_sems, start, finish)


def _ride_gather_d2d(bufs):
    n = len(bufs)

    def start(rin, rout, send, recv):
        x, y, c, chips = _place()
        for k in range(n):
            for j, chip in enumerate(chips):
                got = _rows(rout[k], *chip, c)
                _rcopy(got, got, send, recv, 3 * k + j, (x, y, 1 - c)).start()

    def finish(rin, rout, send, recv):
        x, y, c, chips = _place()
        for k in range(n):
            for j, chip in enumerate(chips):
                theirs = _rows(rout[k], *chip, 1 - c)
                _rcopy(theirs, theirs, send, recv, 3 * k + j, (x, y, c)).wait_recv()
        for k in range(n):
            for j, chip in enumerate(chips):
                got = _rows(rout[k], *chip, c)
                _rcopy(got, got, send, recv, 3 * k + j, (x, y, c)).wait_send()

    return _Rider(bufs, _same(bufs), {k: k for k in range(n)}, 3 * n, start, finish)


def _ride_swap(grads):
    n = len(grads)

    def copy(k, rin, rout, send, recv):
        x, y, c, _ = _place()
        half = rin[k].shape[1] // 2
        return _rcopy(rin[k].at[:, pl.ds((1 - c) * half, half)], rout[k], send, recv, k, (x, y, 1 - c))

    def start(rin, rout, send, recv):
        for k in range(n):
            copy(k, rin, rout, send, recv).start()

    def finish(rin, rout, send, recv):
        for k in range(n):
            copy(k, rin, rout, send, recv).wait()

    outs = [jax.ShapeDtypeStruct((g.shape[0], g.shape[1] // 2, g.shape[2]), g.dtype) for g in grads]
    return _Rider(grads, outs, {}, n, start, finish)


def _ride_send_partials(parts, pieces=None, into=None):
    n = len(parts)
    pieces = pieces or [None] * n

    def cut(ref, k):
        if pieces[k] is None:
            return ref
        lo, hi, m = pieces[k]
        q = ref.shape[0] // m
        return ref.at[pl.ds(lo * q, (hi - lo) * q)]

    def copies(rin, rout, send, recv):
        x, y, c, chips = _place()
        return [_rcopy(cut(rin[k].at[2 * px + py], k), cut(rout[k].at[j], k), send, recv, 3 * k + j, (px, py, c))
                for k in range(n) for j, (px, py) in enumerate(chips)]

    def start(rin, rout, send, recv):
        for cp in copies(rin, rout, send, recv):
            cp.start()

    def finish(rin, rout, send, recv):
        for cp in copies(rin, rout, send, recv):
            cp.wait()

    if into is None:
        outs = [jax.ShapeDtypeStruct((3,) + p.shape[1:], p.dtype) for p in parts]
        return _Rider(parts, outs, {}, 3 * n, start, finish)
    return _Rider(list(parts) + list(into), _same(into), {n + k: k for k in range(n)}, 3 * n, start, finish)


def _ride_join(bufs):
    n = len(bufs)

    def half_of(buf, pc):
        half = buf.shape[0] // 2
        return buf.at[pl.ds(pc * half, half)]

    def start(rin, rout, send, recv):
        x, y, c, _ = _place()
        for k in range(n):
            mine = half_of(rout[k], c)
            _rcopy(mine, mine, send, recv, k, (x, y, 1 - c)).start()

    def finish(rin, rout, send, recv):
        x, y, c, _ = _place()
        for k in range(n):
            mine, theirs = half_of(rout[k], c), half_of(rout[k], 1 - c)
            _rcopy(mine, mine, send, recv, k, (x, y, c)).wait_send()
            _rcopy(theirs, theirs, send, recv, k, (x, y, c)).wait_recv()

    return _Rider(bufs, _same(bufs), {k: k for k in range(n)}, n, start, finish)


def _run(name, rider):
    def body(*refs):
        nri, nro = len(rider.ins), len(rider.outs)
        rin, rout = refs[:nri], refs[nri:nri + nro]
        send, recv = refs[nri + nro:]
        rider.start(rin, rout, send, recv)
        rider.finish(rin, rout, send, recv)

    return pl.pallas_call(
        body, name=name, in_specs=[HBM] * len(rider.ins), out_specs=[HBM] * len(rider.outs), out_shape=rider.outs,
        scratch_shapes=[pltpu.SemaphoreType.DMA((rider.n_sems,)), pltpu.SemaphoreType.DMA((rider.n_sems,))],
        input_output_aliases=rider.aliases,
    )(*rider.ins)


def _add_halves(name, g, other, c_idx):
    s, r, cols = g.shape
    half = r // 2
    tr = _div_tile(half, 16, 512)
    nb = half // tr

    def body(c_ref, g_ref, o_ref, q_ref):
        del c_ref
        q_ref[...] = (g_ref[...] + o_ref[...]).astype(BF16)

    return pl.pallas_call(
        body, name=name,
        grid_spec=pltpu.PrefetchScalarGridSpec(
            num_scalar_prefetch=1, grid=(s, nb),
            in_specs=[pl.BlockSpec((None, tr, cols), lambda k, i, c: (k, c[0] * nb + i, 0)),
                      pl.BlockSpec((None, tr, cols), lambda k, i, c: (k, i, 0))],
            out_specs=pl.BlockSpec((None, tr, cols), lambda k, i, c: (k, i, 0))),
        out_shape=jax.ShapeDtypeStruct((s, half, cols), BF16),
        compiler_params=_params("parallel", "parallel"),
    )(c_idx, g, other)


def _sum_partials(name, part, arrived, place_idx):
    _, half, cols = part.shape
    tr = _div_tile(half, 16, 512)
    nb = half // tr

    def body(s_ref, p_ref, a_ref, o_ref):
        del s_ref
        o_ref[...] = ((p_ref[...].astype(F32) + a_ref[0].astype(F32)) + a_ref[1].astype(F32)) + a_ref[2].astype(F32)

    return pl.pallas_call(
        body, name=name,
        grid_spec=pltpu.PrefetchScalarGridSpec(
            num_scalar_prefetch=1, grid=(nb,),
            in_specs=[pl.BlockSpec((None, tr, cols), lambda i, s: (s[0], i, 0)),
                      pl.BlockSpec((3, tr, cols), lambda i, s: (0, i, 0))],
            out_specs=pl.BlockSpec((tr, cols), lambda i, s: (s[1] * nb + i, 0))),
        out_shape=jax.ShapeDtypeStruct((2 * half, cols), F32),
        compiler_params=_params("parallel"),
    )(place_idx, part, arrived)


def _small_allreduce(wide_rows, ffn_rows, w, dff, n_wide, n_ffn):
    n_in = len(wide_rows) + len(ffn_rows)

    def body(*refs):
        ins = refs[:n_in]
        s1_ref, s2_ref, r1, r2, p1, p2, send, recv = refs[n_in:]
        x, y, c, _ = _place()
        me = 4 * x + 2 * y + c
        p1[...] = jnp.zeros_like(p1)
        p2[...] = jnp.zeros_like(p2)
        row = 0
        for ref, (_, r, m) in zip(ins, wide_rows):
            if m == 1 and r % SUBLANE == 0 and row % SUBLANE == 0:
                p1[row:row + r, :] = ref[...]
                row += r
                continue
            for rr in range(r):
                for mm in range(m):
                    p1[row:row + 1, :] = ref[rr:rr + 1, mm * w:(mm + 1) * w]
                    row += 1
        row = 0
        for ref, arr in zip(ins[len(wide_rows):], ffn_rows):
            r = arr.shape[0]
            p2[row:row + r, :] = ref[...]
            row += r
        r1[me] = p1[...]
        r2[me] = p2[...]
        cps = []
        for mask in range(1, 8):
            peer = (x ^ (mask >> 2), y ^ ((mask >> 1) & 1), c ^ (mask & 1))
            for a, (src, dst) in enumerate(((p1, r1), (p2, r2))):
                cp = pltpu.make_async_remote_copy(
                    src_ref=src, dst_ref=dst.at[me], send_sem=send.at[a, mask - 1], recv_sem=recv.at[a, mask - 1],
                    device_id=peer, device_id_type=MESH)
                cp.start()
                cps.append(cp)
        for cp in cps:
            cp.wait()
        t1, t2 = r1[0], r2[0]
        for d in range(1, 8):
            t1 = t1 + r1[d]
            t2 = t2 + r2[d]
        s1_ref[...] = t1
        s2_ref[...] = t2

    ins = [a for a, _, _ in wide_rows] + list(ffn_rows)
    return pl.pallas_call(
        body, name="small_allreduce", in_specs=[VMEM_FULL] * n_in, out_specs=[VMEM_FULL, VMEM_FULL],
        out_shape=[jax.ShapeDtypeStruct((n_wide, w), F32), jax.ShapeDtypeStruct((n_ffn, dff), F32)],
        scratch_shapes=[pltpu.VMEM((8, n_wide, w), F32), pltpu.VMEM((8, n_ffn, dff), F32),
                        pltpu.VMEM((n_wide, w), F32), pltpu.VMEM((n_ffn, dff), F32),
                        pltpu.SemaphoreType.DMA((2, 7)), pltpu.SemaphoreType.DMA((2, 7))],
        compiler_params=pltpu.CompilerParams(vmem_limit_bytes=VMEM_LIMIT),
    )(*ins)


def _adamw(w, g, m, v):
    m2 = ADAM_B1 * m + (1.0 - ADAM_B1) * g
    v2 = ADAM_B2 * v + (1.0 - ADAM_B2) * (g * g)
    m_hat = m2 / (1.0 - ADAM_B1 ** ADAM_STEP)
    v_hat = v2 / (1.0 - ADAM_B2 ** ADAM_STEP)
    delta = -ADAM_LR * (m_hat / (jnp.sqrt(v_hat) + ADAM_EPS) + ADAM_WD * w)
    return delta, m2, v2


def _adam_big(name, w, g, m, v):
    r, c = w.shape
    tr = 128 if r % 128 == 0 else r

    def body(w_ref, g_ref, m_ref, v_ref, d_ref, m2_ref, v2_ref):
        d_ref[...], m2_ref[...], v2_ref[...] = _adamw(w_ref[...], g_ref[...], m_ref[...], v_ref[...])

    blk = pl.BlockSpec((tr, c), lambda i: (i, 0))
    return pl.pallas_call(
        body, name=name, grid=(r // tr,), in_specs=[blk] * 4, out_specs=[blk] * 3,
        out_shape=[jax.ShapeDtypeStruct((r, c), F32)] * 3,
        compiler_params=_params("parallel"),
    )(w, g, m, v)


def _adam_small(s1, s2, cw_g, fw_g, lb_logits, triples, layout, w):
    n = len(triples)

    def body(*refs):
        s1_ref, s2_ref, cw_ref, fw_ref, lbl_ref = refs[:5]
        prm = refs[5:5 + 3 * n]
        outs = refs[5 + 3 * n:]
        for p, lay in enumerate(layout):
            w_ref, m_ref, v_ref = prm[3 * p:3 * p + 3]
            g_ref, d_ref, m2_ref, v2_ref = outs[4 * p:4 * p + 4]
            if lay[0] == "wide":
                _, row, r, pieces = lay
                for rr in range(r):
                    for mm in range(pieces):
                        g_ref[rr:rr + 1, mm * w:(mm + 1) * w] = s1_ref[row:row + 1, :]
                        row += 1
            elif lay[0] == "ffn":
                _, row, r = lay
                g_ref[...] = s2_ref[row:row + r, :]
            elif lay[0] == "cw":
                g_ref[...] = cw_ref[0:g_ref.shape[0], :]
            elif lay[0] == "fw":
                g_ref[...] = fw_ref[0:g_ref.shape[0], :]
            else:
                s0 = _lower_bound(lbl_ref)
                d0 = s1_ref[lay[1]:lay[1] + 1, :] * s0 * (1.0 - s0)
                g_ref[0:1, :] = d0
                g_ref[1:2, :] = -d0
            d_ref[...], m2_ref[...], v2_ref[...] = _adamw(w_ref[...], g_ref[...], m_ref[...], v_ref[...])

    flat = [a for tr in triples for a in tr]
    shapes = []
    for tr in triples:
        shapes.extend([jax.ShapeDtypeStruct(tr[0].shape, F32)] * 4)
    return pl.pallas_call(
        body, name="adam_small", in_specs=[VMEM_FULL] * (5 + 3 * n), out_specs=[VMEM_FULL] * (4 * n),
        out_shape=shapes, compiler_params=pltpu.CompilerParams(vmem_limit_bytes=VMEM_LIMIT),
    )(s1, s2, cw_g, fw_g, lb_logits, *flat)


def _row_tile(t):
    return 512 if t % 512 == 0 and t >= 2048 else 128


def kernel(x, emb_ln_g, emb_ln_b, w_in, conv_w, conv_b, conv_norm_g, conv_norm_b, lb_logits, hgrn_norm_g, w_out, ln1_g, ln1_b, w_ffn_up, ffn_conv_w, ffn_conv_b, w_ffn_down, ln2_g, ln2_b, loss_target, m_emb_ln_g, m_emb_ln_b, m_w_in, m_conv_w, m_conv_b, m_conv_norm_g, m_conv_norm_b, m_lb_logits, m_hgrn_norm_g, m_w_out, m_ln1_g, m_ln1_b, m_w_ffn_up, m_ffn_conv_w, m_ffn_conv_b, m_w_ffn_down, m_ln2_g, m_ln2_b, v_emb_ln_g, v_emb_ln_b, v_w_in, v_conv_w, v_conv_b, v_conv_norm_g, v_conv_norm_b, v_lb_logits, v_hgrn_norm_g, v_w_out, v_ln1_g, v_ln1_b, v_w_ffn_up, v_ffn_conv_w, v_ffn_conv_b, v_w_ffn_down, v_ln2_g, v_ln2_b):
    depth = w_in.shape[0]
    assert depth == 1 and x.shape[0] == 1
    alpha = (2.0 * depth) ** 0.25
    t, d = x.shape[1], x.shape[2]
    w = d // 2
    dff = ffn_conv_b.shape[1]
    kc = conv_w.shape[1]
    assert w % (2 * LANE) == 0 and dff % (4 * LANE) == 0 and t % 128 == 0
    tm = _row_tile(t)
    tm2 = tm // 2
    tmm = 1024 if t % 1024 == 0 and t >= 2048 else tm
    cb = 2 * LANE
    cbf = 4 * LANE
    tb = tm
    nh = w // LANE
    hpb = 4 if nh % 4 == 0 else 2

    xi = lax.axis_index("x")
    yi = lax.axis_index("y")
    ci = lax.axis_index("c")
    chip = 2 * xi + yi
    c_idx = jnp.reshape(ci, (1,)).astype(jnp.int32)
    chip_idx = jnp.reshape(chip, (1,)).astype(jnp.int32)
    place_idx = jnp.stack([chip, ci]).astype(jnp.int32)

    x2 = x[0]
    tgt = loss_target[0]
    g0, b0 = emb_ln_g.reshape(1, d), emb_ln_b.reshape(1, d)
    w_in2, w_out2, w_up2, w_dn2 = w_in[0], w_out[0], w_ffn_up[0], w_ffn_down[0]
    cw2, fw2 = conv_w[0], ffn_conv_w[0]

    b_in = _place_shard(w_in2, "place_w_in", chip_idx, BF16)
    b_out = _place_shard(w_out2, "place_w_out", chip_idx, BF16)
    b_up = _place_shard(w_up2, "place_w_up", chip_idx, BF16)
    b_dn = _place_shard(w_dn2, "place_w_down", chip_idx, BF16)
    b_cw = _place_shard(_pad_rows(cw2), "place_conv_w", chip_idx, F32)
    b_fw = _place_shard(_pad_rows(fw2), "place_ffn_conv_w", chip_idx, F32)
    first = _run("gather_first_ici", _ride_gather_ici([b_in, b_cw, b_fw]))
    w_in3, cw_full3, fw_full3 = _run("gather_first_d2d", _ride_gather_d2d(first))
    cw_full = _unshard_cols(cw_full3)[:kc]
    fw_full = _unshard_cols(fw_full3)[:fw2.shape[0]]

    h0b = _ln0(x2, g0, b0, tm)
    p3, (b_out, b_up) = _proj("in_proj", h0b, w_in3, 6, 2 * tmm if t % (2 * tmm) == 0 else tmm, w // 2,
                              rider=_ride_gather_ici([b_out, b_up], [None, (0, 1, 4)]))
    (cat, u1), (w_out3, b_up) = _conv_fwd(
        p3, cw_full, conv_b, conv_norm_g, conv_norm_b, tm2, cb,
        rider=_ride_both(_ride_gather_d2d([b_out]), _ride_gather_ici([b_up], [(1, 2, 4)])))
    w_out_full = w_out3.reshape(d, d)
    (cat, o_pre, states), got = _hgrn_fwd(p3, lb_logits, hgrn_norm_g, cat, tb, hpb,
                                          rider=_ride_gather_ici([b_up], [(2, 4, 4)]))
    (xhat1, h1b, rstd1), (w_up3,) = _mix_ln1(cat, w_out_full, x2, g0, b0, ln1_g, ln1_b, alpha, tm2,
                                             rider=_ride_gather_d2d(got))
    hh3, got = _proj("ffn_up", h1b, w_up3, 2, tmm, dff // 4, rider=_ride_gather_ici([b_dn]))
    act, (w_dn3,) = _ffn_act_fwd(hh3, fw_full, ffn_conv_b, tm, cbf, rider=_ride_gather_d2d(got))
    ks = dff // N_CHIPS
    ffn = _wgrad("ffn_down", act, w_dn3, (t, d), (t // tmm, 1, N_CHIPS),
                 pl.BlockSpec((tmm, ks), lambda i, j, k: (i, k)),
                 pl.BlockSpec((None, ks, d), lambda i, j, k: (k, 0, 0)),
                 pl.BlockSpec((tmm, d), lambda i, j, k: (i, 0)), dot=_dot)
    dz2, dz2b, dg2, db2, loss_row = _ln2_loss(ffn, xhat1, tgt, ln1_g, ln1_b, ln2_g, ln2_b, alpha, tm2)

    dact = _proj_t("ffn_down_t", dz2b, w_dn3.reshape(dff, d), tmm, ks)
    dhh3, dfw, dfb = _ffn_act_bwd(dact, hh3, fw_full, ffn_conv_b, tm, cbf)
    tt = 2 * tmm if t % (2 * tmm) == 0 else tmm
    d_w_dn = _wgrad("wgrad_down", act, dz2b, (N_CHIPS, ks, d), (N_CHIPS, 2, t // tt),
                    pl.BlockSpec((tt, ks), lambda s, j, k: (k, s)),
                    pl.BlockSpec((tt, d // 2), lambda s, j, k: (k, j)),
                    pl.BlockSpec((None, ks, d // 2), lambda s, j, k: (s, 0, j)))
    wu = 2 * dff // N_CHIPS
    tnu = wu // 2
    per_sec_u = dff // tnu
    pre1, (arr_dn,) = _wgrad(
        "up_t", dhh3, w_up3, (t, d), (t // tmm, 1, 2 * N_CHIPS),
        pl.BlockSpec((None, tmm, tnu), lambda i, j, k: (k // per_sec_u, i, k % per_sec_u)),
        pl.BlockSpec((None, d, tnu), lambda i, j, k: (k // 2, 0, k % 2)),
        pl.BlockSpec((tmm, d), lambda i, j, k: (i, 0)), dot=_dot_nt, rider=_ride_swap([d_w_dn]))
    dz1, dz1b, dg1, db1 = _ln1_bwd(pre1, dz2, xhat1, rstd1, ln1_g, alpha, tm2)
    part_dn = _add_halves("add_halves_w_down", d_w_dn, arr_dn, c_idx)
    d_w_up, (land_dn,) = _wgrad(
        "wgrad_up", h1b, dhh3, (N_CHIPS, d, wu), (N_CHIPS, 2, 2, t // tt),
        pl.BlockSpec((tt, d // 2), lambda s, r, j, k: (k, r)),
        pl.BlockSpec((None, tt, tnu), lambda s, r, j, k: ((2 * s + j) // per_sec_u, k, (2 * s + j) % per_sec_u)),
        pl.BlockSpec((None, d // 2, tnu), lambda s, r, j, k: (s, r, j)), rider=_ride_send_partials([part_dn]))
    dcat = _proj_t("out_proj_t", dz1b, w_out_full, tmm, d // 2)
    d_w_out = _wgrad("wgrad_out", cat, dz1b, (d, d), (2, 2, t // tt),
                     pl.BlockSpec((tt, d // 2), lambda r, j, k: (k, r)),
                     pl.BlockSpec((tt, d // 2), lambda r, j, k: (k, j)),
                     pl.BlockSpec((d // 2, d // 2), lambda r, j, k: (r, j))).reshape(N_CHIPS, d // N_CHIPS, d)
    du1, dcng, dcnb = _conv_norm_bwd(dcat, u1, conv_norm_g, conv_norm_b, tm)
    (dp3, dcw, dcb), (arr_up, arr_out) = _conv_bwd(du1, p3, cw_full, tm2, cb, rider=_ride_swap([d_w_up, d_w_out]))
    part_up = _add_halves("add_halves_w_up", d_w_up, arr_up, c_idx)
    part_out = _add_halves("add_halves_w_out", d_w_out, arr_out, c_idx)
    (dp3, dlb, dhg), (land_up, land_out) = _hgrn_bwd(
        p3, lb_logits, hgrn_norm_g, o_pre, states, dcat, dp3, tb, hpb,
        rider=_ride_send_partials([part_up, part_out], [(0, 3, 4), None]))
    wi = 6 * w // N_CHIPS
    tni = w // 2
    d_w_in, (land_up,) = _wgrad(
        "wgrad_in", h0b, dp3, (N_CHIPS, d, wi), (N_CHIPS, 2, wi // tni, t // tt),
        pl.BlockSpec((tt, d // 2), lambda s, r, j, k: (k, r)),
        pl.BlockSpec((None, tt, tni), lambda s, r, j, k: (((wi // tni) * s + j) // 2, k, ((wi // tni) * s + j) % 2)),
        pl.BlockSpec((None, d // 2, tni), lambda s, r, j, k: (s, r, j)),
        rider=_ride_send_partials([part_up], [(3, 4, 4)], into=[land_up]))
    (arr_in,) = _run("swap_w_in", _ride_swap([d_w_in]))
    part_in = _add_halves("add_halves_w_in", d_w_in, arr_in, c_idx)
    per_shard_i = wi // tni
    pre0, (land_in,) = _wgrad(
        "in_t", dp3, w_in3, (t, d), (t // tmm, 1, 2 * dp3.shape[0]),
        pl.BlockSpec((None, tmm, tni), lambda i, j, k: (k // 2, i, k % 2)),
        pl.BlockSpec((None, d, tni), lambda i, j, k: (k // per_shard_i, 0, k % per_shard_i)),
        pl.BlockSpec((tmm, d), lambda i, j, k: (i, 0)), dot=_dot_nt, rider=_ride_send_partials([part_in]))
    dx, dg0, db0 = _ln0_bwd(pre0, dz1, x2, g0, alpha, tm2)
    halves = [_sum_partials("sum_partials_" + nm, p, a, place_idx)
              for nm, p, a in (("w_in", part_in, land_in), ("w_out", part_out, land_out),
                               ("w_up", part_up, land_up), ("w_down", part_dn, land_dn))]
    g_w_in, g_w_out, g_w_up, g_w_dn = _run("join_halves", _ride_join(halves))

    kpad = dcw.shape[0]
    wide = [(dcw, kpad, 1), (dg0, 1, 2), (db0, 1, 2), (dg1, 1, 2), (db1, 1, 2), (dg2, 1, 2), (db2, 1, 2),
            (dcb, 1, 1), (dcng, 1, 1), (dcnb, 1, 1), (dlb, 1, 1), (dhg, 1, 1)]
    n_wide = sum(r * m for _, r, m in wide)
    n_wide_pad = -(-n_wide // SUBLANE) * SUBLANE
    s1, s2 = _small_allreduce(wide, [dfw, dfb], w, dff, n_wide_pad, 2 * SUBLANE)
    cw_g = lax.dynamic_slice_in_dim(s1[0:kpad], chip * (w // N_CHIPS), w // N_CHIPS, axis=1)
    fw_g = lax.dynamic_slice_in_dim(s2[0:SUBLANE], chip * (dff // N_CHIPS), dff // N_CHIPS, axis=1)

    small = [
        (g0, m_emb_ln_g.reshape(1, d), v_emb_ln_g.reshape(1, d)), (b0, m_emb_ln_b.reshape(1, d), v_emb_ln_b.reshape(1, d)),
        (cw2, m_conv_w[0], v_conv_w[0]), (conv_b, m_conv_b, v_conv_b),
        (conv_norm_g, m_conv_norm_g, v_conv_norm_g), (conv_norm_b, m_conv_norm_b, v_conv_norm_b),
        (lb_logits, m_lb_logits, v_lb_logits), (hgrn_norm_g, m_hgrn_norm_g, v_hgrn_norm_g),
        (ln1_g, m_ln1_g, v_ln1_g), (ln1_b, m_ln1_b, v_ln1_b),
        (fw2, m_ffn_conv_w[0], v_ffn_conv_w[0]), (ffn_conv_b, m_ffn_conv_b, v_ffn_conv_b),
        (ln2_g, m_ln2_g, v_ln2_g), (ln2_b, m_ln2_b, v_ln2_b),
    ]
    r0 = kpad
    layout = [("wide", r0, 1, 2), ("wide", r0 + 2, 1, 2), ("cw",), ("wide", r0 + 12, 1, 1), ("wide", r0 + 13, 1, 1),
              ("wide", r0 + 14, 1, 1), ("lb", r0 + 15), ("wide", r0 + 16, 1, 1), ("wide", r0 + 4, 1, 2),
              ("wide", r0 + 6, 1, 2), ("fw",), ("ffn", SUBLANE, 1), ("wide", r0 + 8, 1, 2), ("wide", r0 + 10, 1, 2)]
    so = _adam_small(s1, s2, cw_g, fw_g, lb_logits, small, layout, w)
    sm = {nm: so[4 * i:4 * i + 4] for i, nm in enumerate(
        ["emb_ln_g", "emb_ln_b", "conv_w", "conv_b", "conv_norm_g", "conv_norm_b", "lb_logits", "hgrn_norm_g",
         "ln1_g", "ln1_b", "ffn_conv_w", "ffn_conv_b", "ln2_g", "ln2_b"])}
    bigs = {}
    for nm, wt, g, m, v in (("w_in", w_in2, g_w_in, m_w_in[0], v_w_in[0]), ("w_out", w_out2, g_w_out, m_w_out[0], v_w_out[0]),
                            ("w_ffn_up", w_up2, g_w_up, m_w_ffn_up[0], v_w_ffn_up[0]),
                            ("w_ffn_down", w_dn2, g_w_dn, m_w_ffn_down[0], v_w_ffn_down[0])):
        bigs[nm] = (g,) + tuple(_adam_big("adam_" + nm, wt, g, m, v))

    loss = lax.psum(loss_row[0, 0], ("x", "y", "c"))

    order = ["emb_ln_g", "emb_ln_b", "w_in", "conv_w", "conv_b", "conv_norm_g", "conv_norm_b", "lb_logits",
             "hgrn_norm_g", "w_out", "ln1_g", "ln1_b", "w_ffn_up", "ffn_conv_w", "ffn_conv_b", "w_ffn_down",
             "ln2_g", "ln2_b"]
    shapes = dict(emb_ln_g=emb_ln_g.shape, emb_ln_b=emb_ln_b.shape, w_in=w_in.shape, conv_w=conv_w.shape,
                  w_out=w_out.shape, w_ffn_up=w_ffn_up.shape, ffn_conv_w=ffn_conv_w.shape, w_ffn_down=w_ffn_down.shape)
    outs = [loss, dx.reshape(x.shape)]
    for which in range(4):
        for nm in order:
            a = bigs[nm][which] if nm in bigs else sm[nm][which]
            outs.append(a.reshape(shapes[nm]) if nm in shapes else a)
    return tuple(outs)


def _pad_rows(a):
    k = a.shape[0]
    kp = -(-k // 16) * 16
    return jnp.pad(a, ((0, kp - k), (0, 0)))


def _unshard_cols(a3):
    s, k, c = a3.shape
    return jnp.transpose(a3, (1, 0, 2)).reshape(k, s * c)
```

```python
import functools

import jax
import jax.numpy as jnp
from jax import lax
from jax.experimental import pallas as pl
from jax.experimental.pallas import tpu as pltpu

F32 = jnp.float32
BF16 = jnp.bfloat16

LN_EPS = 1e-5
RMS_EPS = 1e-6
LANE = 128
SUBLANE = 8
CHUNK = 64
SUB = 8
HALO = 32
FHALO = 8
ROWS = 64
N_CHIPS = 4
VMEM_LIMIT = 56 << 20
NEG_BIG = -1e30

ADAM_LR = 0.001
ADAM_B1 = 0.9
ADAM_B2 = 0.999
ADAM_EPS = 1e-08
ADAM_WD = 0.01
ADAM_STEP = 10

MESH = pl.DeviceIdType.MESH
HBM = pl.BlockSpec(memory_space=pl.ANY)
VMEM_FULL = pl.BlockSpec(memory_space=pltpu.VMEM)


def _params(*sem):
    return pltpu.CompilerParams(dimension_semantics=sem, vmem_limit_bytes=VMEM_LIMIT)


class _Rider:
    def __init__(self, ins, outs, aliases, n_sems, start, finish):
        self.ins, self.outs, self.aliases = list(ins), list(outs), dict(aliases)
        self.n_sems, self.start, self.finish = n_sems, start, finish


def _call(body, args, *, name, grid, in_specs, out_specs, out_shape, scratch_shapes=(), aliases=None, rider=None):
    many = isinstance(out_shape, (list, tuple))
    shapes = list(out_shape) if many else [out_shape]
    ospecs = list(out_specs) if many else [out_specs]
    aliases = dict(aliases or {})
    sem = ("arbitrary",) * len(grid)
    if rider is None:
        res = pl.pallas_call(
            body, name=name, grid=grid, in_specs=list(in_specs), out_specs=ospecs, out_shape=shapes,
            scratch_shapes=list(scratch_shapes), input_output_aliases=aliases, compiler_params=_params(*sem))(*args)
        return res if many else res[0]
    n_in, n_out, n_scr = len(args), len(shapes), len(scratch_shapes)
    nri, nro = len(rider.ins), len(rider.outs)

    def wrapped(*refs):
        ins, rin = refs[:n_in], refs[n_in:n_in + nri]
        o0 = n_in + nri
        outs, rout = refs[o0:o0 + n_out], refs[o0 + n_out:o0 + n_out + nro]
        s0 = o0 + n_out + nro
        scr, (send, recv) = refs[s0:s0 + n_scr], refs[s0 + n_scr:]
        ids = [pl.program_id(a) for a in range(len(grid))]
        first = functools.reduce(jnp.logical_and, [i == 0 for i in ids])
        last = functools.reduce(jnp.logical_and, [i == g - 1 for i, g in zip(ids, grid)])

        @pl.when(first)
        def _():
            rider.start(rin, rout, send, recv)

        body(*ins, *outs, *scr)

        @pl.when(last)
        def _():
            rider.finish(rin, rout, send, recv)

    for ri, ro in rider.aliases.items():
        aliases[n_in + ri] = n_out + ro
    res = pl.pallas_call(
        wrapped, name=name, grid=grid, in_specs=list(in_specs) + [HBM] * nri, out_specs=ospecs + [HBM] * nro,
        out_shape=shapes + rider.outs,
        scratch_shapes=list(scratch_shapes) + [pltpu.SemaphoreType.DMA((rider.n_sems,)),
                                               pltpu.SemaphoreType.DMA((rider.n_sems,))],
        input_output_aliases=aliases, compiler_params=_params(*sem))(*args, *rider.ins)
    main, extra = res[:n_out], list(res[n_out:])
    return (list(main) if many else main[0]), extra


def _div_tile(n, mult, cap):
    best = n
    for t in range(mult, min(n, cap) + 1, mult):
        if n % t == 0:
            best = t
    return best


def _sigmoid(x):
    return 1.0 / (1.0 + jnp.exp(-x))


def _ln_stats(x):
    mu = jnp.mean(x, axis=-1, keepdims=True)
    xc = x - mu
    var = jnp.mean(xc * xc, axis=-1, keepdims=True)
    rstd = lax.rsqrt(var + LN_EPS)
    return xc * rstd, rstd


def _ln_bwd(dy, xhat, rstd, g):
    dyg = dy * g
    m1 = jnp.mean(dyg, axis=-1, keepdims=True)
    m2 = jnp.mean(dyg * xhat, axis=-1, keepdims=True)
    return rstd * (dyg - m1 - xhat * m2)


def _dot_nt(a, b):
    return lax.dot_general(a, b, (((1,), (1,)), ((), ())), preferred_element_type=F32)


def _dot_tn(a, b):
    return lax.dot_general(a, b, (((0,), (0,)), ((), ())), preferred_element_type=F32)


def _dot(a, b):
    return jnp.dot(a, b, preferred_element_type=F32)


def _dot3(m, x):
    mb = m.astype(BF16)
    x1 = x.astype(BF16)
    r1 = x - x1.astype(F32)
    x2 = r1.astype(BF16)
    x3 = (r1 - x2.astype(F32)).astype(BF16)
    return _dot(mb, x1) + _dot(mb, x2) + _dot(mb, x3)


def _place_shard(x, name, chip_idx, dtype):
    r, c = x.shape
    tr = _div_tile(r, 16, 512)

    def body(s_ref, x_ref, o_ref):
        del s_ref
        o_ref[...] = x_ref[...].astype(dtype)

    return pl.pallas_call(
        body, name=name,
        grid_spec=pltpu.PrefetchScalarGridSpec(
            num_scalar_prefetch=1, grid=(r // tr,),
            in_specs=[pl.BlockSpec((tr, c), lambda i, s: (i, 0))],
            out_specs=pl.BlockSpec((None, tr, c), lambda i, s: (s[0], i, 0))),
        out_shape=jax.ShapeDtypeStruct((N_CHIPS, r, c), dtype),
        compiler_params=_params("parallel"),
    )(chip_idx, x)


def _ln0(x, g, b, tm):
    t, d = x.shape

    def body(x_ref, g_ref, b_ref, o_ref):
        xh, _ = _ln_stats(x_ref[...])
        o_ref[...] = (xh * g_ref[...] + b_ref[...]).astype(BF16)

    row = pl.BlockSpec((1, d), lambda i: (0, 0))
    return pl.pallas_call(
        body, name="ln0", grid=(t // tm,),
        in_specs=[pl.BlockSpec((tm, d), lambda i: (i, 0)), row, row],
        out_specs=pl.BlockSpec((tm, d), lambda i: (i, 0)),
        out_shape=jax.ShapeDtypeStruct((t, d), BF16),
        compiler_params=_params("parallel"),
    )(x, g, b)


def _proj(name, a, w3, n_sec, tm, tn, rider=None):
    m, k = a.shape
    s, _, ws = w3.shape
    sec_w = s * ws // n_sec
    nj = ws // tn
    per_sec = sec_w // tn

    def body(a_ref, w_ref, o_ref):
        o_ref[...] = _dot(a_ref[...], w_ref[...])

    return _call(
        body, (a, w3), name=name, grid=(s * nj, m // tm),
        in_specs=[pl.BlockSpec((tm, k), lambda j, i: (i, 0)),
                  pl.BlockSpec((None, k, tn), lambda j, i: (j // nj, 0, j % nj))],
        out_specs=pl.BlockSpec((None, tm, tn), lambda j, i: (j // per_sec, i, j % per_sec)),
        out_shape=jax.ShapeDtypeStruct((n_sec, m, sec_w), F32), rider=rider)


def _proj_t(name, a, w, tm, tn, rider=None):
    m, k = a.shape
    n = w.shape[0]

    def body(a_ref, w_ref, o_ref):
        o_ref[...] = _dot_nt(a_ref[...], w_ref[...])

    return _call(
        body, (a, w), name=name, grid=(n // tn, m // tm),
        in_specs=[pl.BlockSpec((tm, k), lambda j, i: (i, 0)),
                  pl.BlockSpec((tn, k), lambda j, i: (j, 0))],
        out_specs=pl.BlockSpec((tm, tn), lambda j, i: (i, j)),
        out_shape=jax.ShapeDtypeStruct((m, n), F32), rider=rider)


def _wgrad(name, a, b, out_shape, grid, a_spec, b_spec, o_spec, rider=None, dot=_dot_tn):
    nt = len(grid) - 1

    def body(a_ref, b_ref, o_ref):
        t = pl.program_id(nt)
        prod = dot(a_ref[...], b_ref[...])

        @pl.when(t == 0)
        def _():
            o_ref[...] = prod

        @pl.when(t > 0)
        def _():
            o_ref[...] += prod

    return _call(
        body, (a, b), name=name, grid=grid, in_specs=[a_spec, b_spec], out_specs=o_spec,
        out_shape=jax.ShapeDtypeStruct(out_shape, F32), rider=rider)


def _in_t(dp3, w_in3, tm, rider=None):
    _, t, sec_w = dp3.shape
    s, d, ws = w_in3.shape
    tk = sec_w // 2
    nq = ws // tk

    def body(*refs):
        a_refs, w_ref, o_ref = refs[:nq], refs[nq], refs[nq + 1]
        k = pl.program_id(1)
        prod = _dot_nt(a_refs[0][...], w_ref[:, 0:tk])
        for q in range(1, nq):
            prod = prod + _dot_nt(a_refs[q][...], w_ref[:, q * tk:(q + 1) * tk])

        @pl.when(k == 0)
        def _():
            o_ref[...] = prod

        @pl.when(k > 0)
        def _():
            o_ref[...] += prod

    def a_spec(q):
        return pl.BlockSpec((None, tm, tk), lambda i, k: ((nq * k + q) // 2, i, (nq * k + q) % 2))

    return _call(
        body, (dp3,) * nq + (w_in3,), name="in_t", grid=(t // tm, s),
        in_specs=[a_spec(q) for q in range(nq)] + [pl.BlockSpec((None, d, ws), lambda i, k: (k, 0, 0))],
        out_specs=pl.BlockSpec((tm, d), lambda i, k: (i, 0)),
        out_shape=jax.ShapeDtypeStruct((t, d), F32), rider=rider)


def _mix_ln1(cat, w_out, x, g0, b0, g1, b1, alpha, tm, rider=None):
    t, d = x.shape

    def body(cat_ref, w_ref, x_ref, g0_ref, b0_ref, g1_ref, b1_ref, xh_ref, h1b_ref, rstd_ref):
        mix = _dot(cat_ref[...], w_ref[...])
        xh0, _ = _ln_stats(x_ref[...])
        z1 = alpha * (xh0 * g0_ref[...] + b0_ref[...]) + mix
        xh1, rstd1 = _ln_stats(z1)
        xh_ref[...] = xh1
        h1b_ref[...] = (xh1 * g1_ref[...] + b1_ref[...]).astype(BF16)
        rstd_ref[...] = rstd1

    row = pl.BlockSpec((1, d), lambda i: (0, 0))
    blk = pl.BlockSpec((tm, d), lambda i: (i, 0))
    return _call(
        body, (cat, w_out, x, g0, b0, g1, b1), name="mix_ln1", grid=(t // tm,),
        in_specs=[blk, pl.BlockSpec((d, d), lambda i: (0, 0)), blk, row, row, row, row],
        out_specs=[blk, blk, pl.BlockSpec((tm, 1), lambda i: (i, 0))],
        out_shape=[jax.ShapeDtypeStruct((t, d), F32), jax.ShapeDtypeStruct((t, d), BF16),
                   jax.ShapeDtypeStruct((t, 1), F32)], rider=rider)


def _ln2_loss(ffn, xhat1, tgt, g1, b1, g2, b2, alpha, tm):
    t, d = xhat1.shape
    ni = t // tm
    inv_d = 1.0 / d

    def body(ffn_ref, xh1_ref, tgt_ref, g1_ref, b1_ref, g2_ref, b2_ref,
             dz2_ref, dz2b_ref, dg2_ref, db2_ref, loss_ref, lrow):
        i = pl.program_id(0)
        h1 = xh1_ref[...] * g1_ref[...] + b1_ref[...]
        xh2, rstd2 = _ln_stats(alpha * h1 + ffn_ref[...])
        g2v = g2_ref[...]
        diff = xh2 * g2v + b2_ref[...] - tgt_ref[...]
        dh2 = diff * inv_d
        sq = jnp.sum(diff * diff, axis=0, keepdims=True)
        dg = jnp.sum(dh2 * xh2, axis=0, keepdims=True)
        db = jnp.sum(dh2, axis=0, keepdims=True)

        @pl.when(i == 0)
        def _():
            lrow[...] = sq
            dg2_ref[...] = dg
            db2_ref[...] = db

        @pl.when(i > 0)
        def _():
            lrow[...] += sq
            dg2_ref[...] += dg
            db2_ref[...] += db

        dz2 = _ln_bwd(dh2, xh2, rstd2, g2v)
        dz2_ref[...] = dz2
        dz2b_ref[...] = dz2.astype(BF16)

        @pl.when(i == ni - 1)
        def _():
            tot = jnp.sum(lrow[...], axis=-1, keepdims=True) * (0.5 * inv_d)
            loss_ref[...] = jnp.broadcast_to(tot, (1, LANE))

    row = pl.BlockSpec((1, d), lambda i: (0, 0))
    blk = pl.BlockSpec((tm, d), lambda i: (i, 0))
    return _call(
        body, (ffn, xhat1, tgt, g1, b1, g2, b2), name="ln2_loss", grid=(ni,),
        in_specs=[blk, blk, blk, row, row, row, row],
        out_specs=[blk, blk, row, row, pl.BlockSpec((1, LANE), lambda i: (0, 0))],
        out_shape=[jax.ShapeDtypeStruct((t, d), F32), jax.ShapeDtypeStruct((t, d), BF16),
                   jax.ShapeDtypeStruct((1, d), F32), jax.ShapeDtypeStruct((1, d), F32),
                   jax.ShapeDtypeStruct((1, LANE), F32)],
        scratch_shapes=[pltpu.VMEM((1, d), F32)])


def _ln1_bwd(pre, dz2, xhat1, rstd1, g1, alpha, tm):
    t, d = dz2.shape

    def body(pre_ref, dz2_ref, xh_ref, rstd_ref, g_ref, dz1_ref, dz1b_ref, dg_ref, db_ref):
        i = pl.program_id(0)
        dh1 = alpha * dz2_ref[...] + pre_ref[...]
        xh = xh_ref[...]
        dg = jnp.sum(dh1 * xh, axis=0, keepdims=True)
        db = jnp.sum(dh1, axis=0, keepdims=True)

        @pl.when(i == 0)
        def _():
            dg_ref[...] = dg
            db_ref[...] = db

        @pl.when(i > 0)
        def _():
            dg_ref[...] += dg
            db_ref[...] += db

        dz1 = _ln_bwd(dh1, xh, rstd_ref[...], g_ref[...])
        dz1_ref[...] = dz1
        dz1b_ref[...] = dz1.astype(BF16)

    row = pl.BlockSpec((1, d), lambda i: (0, 0))
    blk = pl.BlockSpec((tm, d), lambda i: (i, 0))
    return _call(
        body, (pre, dz2, xhat1, rstd1, g1), name="ln1_bwd", grid=(t // tm,),
        in_specs=[blk, blk, blk, pl.BlockSpec((tm, 1), lambda i: (i, 0)), row],
        out_specs=[blk, blk, row, row],
        out_shape=[jax.ShapeDtypeStruct((t, d), F32), jax.ShapeDtypeStruct((t, d), BF16),
                   jax.ShapeDtypeStruct((1, d), F32), jax.ShapeDtypeStruct((1, d), F32)])


def _ln0_bwd(pre, dz1, x, g0, alpha, tm):
    t, d = x.shape

    def body(pre_ref, dz1_ref, x_ref, g_ref, dx_ref, dg_ref, db_ref):
        i = pl.program_id(0)
        dh0 = alpha * dz1_ref[...] + pre_ref[...]
        xh, rstd = _ln_stats(x_ref[...])
        dg = jnp.sum(dh0 * xh, axis=0, keepdims=True)
        db = jnp.sum(dh0, axis=0, keepdims=True)

        @pl.when(i == 0)
        def _():
            dg_ref[...] = dg
            db_ref[...] = db

        @pl.when(i > 0)
        def _():
            dg_ref[...] += dg
            db_ref[...] += db

        dx_ref[...] = _ln_bwd(dh0, xh, rstd, g_ref[...])

    row = pl.BlockSpec((1, d), lambda i: (0, 0))
    blk = pl.BlockSpec((tm, d), lambda i: (i, 0))
    return _call(
        body, (pre, dz1, x, g0), name="ln0_bwd", grid=(t // tm,),
        in_specs=[blk, blk, blk, row], out_specs=[blk, row, row],
        out_shape=[jax.ShapeDtypeStruct((t, d), F32), jax.ShapeDtypeStruct((1, d), F32),
                   jax.ShapeDtypeStruct((1, d), F32)])


def _shift_copies(ext, shifted):
    n = shifted.shape[1]
    for p in range(1, SUBLANE):
        shifted[p - 1] = ext[pl.ds(p, n), :]


def _window(ext, shifted, start, rows):
    p = start % SUBLANE
    if p == 0:
        return ext[pl.ds(start, rows), :]
    return shifted[p - 1, pl.ds(start - p, rows), :]


def _conv_fwd(p3, conv_w, conv_b, cn_g, cn_b, tc, cb, rider=None):
    _, t, w = p3.shape
    kk = conv_w.shape[0]
    off = HALO - (kk - 1)
    hb = tc // HALO

    def body(a_ref, g_ref, ap_ref, gp_ref, w_ref, b_ref, ng_ref, nb_ref, cat_ref, u1_ref, ext, sh):
        i = pl.program_id(1)
        ext[pl.ds(HALO, tc), :] = a_ref[...] * _sigmoid(g_ref[...])
        prev = ap_ref[...] * _sigmoid(gp_ref[...])
        ext[pl.ds(0, HALO), :] = jnp.where(i > 0, prev, 0.0)
        _shift_copies(ext, sh)
        for r in range(tc // ROWS):
            acc = jnp.broadcast_to(b_ref[...], (ROWS, cb))
            for k in range(kk):
                acc = acc + w_ref[k:k + 1, :] * _window(ext, sh, r * ROWS + off + k, ROWS)
            u1_ref[pl.ds(r * ROWS, ROWS), :] = acc
            for g in range(cb // LANE):
                sl = slice(g * LANE, (g + 1) * LANE)
                xh, _ = _ln_stats(acc[:, sl])
                u2 = xh * ng_ref[:, sl] + nb_ref[:, sl]
                cat_ref[pl.ds(r * ROWS, ROWS), sl] = (u2 * _sigmoid(u2)).astype(BF16)

    cur = lambda sec: pl.BlockSpec((None, tc, cb), lambda j, i: (sec, i, j))
    prev = lambda sec: pl.BlockSpec((None, HALO, cb), lambda j, i: (sec, jnp.maximum(i * hb - 1, 0), j))
    row = pl.BlockSpec((1, cb), lambda j, i: (0, j))
    return _call(
        body, (p3, p3, p3, p3, conv_w, conv_b, cn_g, cn_b), name="conv_fwd", grid=(w // cb, t // tc),
        in_specs=[cur(0), cur(1), prev(0), prev(1), pl.BlockSpec((kk, cb), lambda j, i: (0, j)), row, row, row],
        out_specs=[pl.BlockSpec((tc, cb), lambda j, i: (i, j)), pl.BlockSpec((tc, cb), lambda j, i: (i, j))],
        out_shape=[jax.ShapeDtypeStruct((t, 2 * w), BF16), jax.ShapeDtypeStruct((t, w), F32)],
        scratch_shapes=[pltpu.VMEM((tc + HALO, cb), F32),
                        pltpu.VMEM((SUBLANE - 1, tc + HALO - SUBLANE, cb), F32)], rider=rider)


def _conv_norm_bwd(dcat, u1, cn_g, cn_b, tc):
    t, w = u1.shape

    def body(du_ref, u1_ref, ng_ref, nb_ref, du1_ref, dg_ref, db_ref):
        i = pl.program_id(0)
        for g in range(w // LANE):
            sl = slice(g * LANE, (g + 1) * LANE)
            ng = ng_ref[:, sl]
            xh, rstd = _ln_stats(u1_ref[:, sl])
            u2 = xh * ng + nb_ref[:, sl]
            sg = _sigmoid(u2)
            du2 = du_ref[:, sl] * (sg * (1.0 + u2 * (1.0 - sg)))
            dg = jnp.sum(du2 * xh, axis=0, keepdims=True)
            db = jnp.sum(du2, axis=0, keepdims=True)

            @pl.when(i == 0)
            def _():
                dg_ref[:, sl] = dg
                db_ref[:, sl] = db

            @pl.when(i > 0)
            def _():
                dg_ref[:, sl] += dg
                db_ref[:, sl] += db

            du1_ref[:, sl] = _ln_bwd(du2, xh, rstd, ng)

    row = pl.BlockSpec((1, w), lambda i: (0, 0))
    blk = pl.BlockSpec((tc, w), lambda i: (i, 0))
    return pl.pallas_call(
        body, name="conv_norm_bwd", grid=(t // tc,),
        in_specs=[blk, blk, row, row], out_specs=[blk, row, row],
        out_shape=[jax.ShapeDtypeStruct((t, w), F32), jax.ShapeDtypeStruct((1, w), F32),
                   jax.ShapeDtypeStruct((1, w), F32)],
        compiler_params=_params("arbitrary"),
    )(dcat, u1, cn_g, cn_b)


def _conv_bwd(du1, p3, conv_w, tc, cb, rider=None):
    n_sec, t, w = p3.shape
    kk = conv_w.shape[0]
    off = HALO - (kk - 1)
    hb = tc // HALO
    nt = t // tc
    kpad = -(-kk // SUBLANE) * SUBLANE

    def body(d_ref, dn_ref, a_ref, g_ref, ap_ref, gp_ref, w_ref, dp_ref, dw_ref, db_ref,
             extd, extu, shd, shu, wacc, bacc):
        i = pl.program_id(1)

        @pl.when(i == 0)
        def _():
            wacc[...] = jnp.zeros_like(wacc)
            bacc[...] = jnp.zeros_like(bacc)

        extd[pl.ds(0, tc), :] = d_ref[...]
        extd[pl.ds(tc, HALO), :] = jnp.where(i < nt - 1, dn_ref[...], 0.0)
        extu[pl.ds(HALO, tc), :] = a_ref[...] * _sigmoid(g_ref[...])
        extu[pl.ds(0, HALO), :] = jnp.where(i > 0, ap_ref[...] * _sigmoid(gp_ref[...]), 0.0)
        _shift_copies(extd, shd)
        _shift_copies(extu, shu)
        for r in range(tc // ROWS):
            rows = pl.ds(r * ROWS, ROWS)
            acc = jnp.zeros((ROWS, cb), F32)
            for k in range(kk):
                acc = acc + w_ref[k:k + 1, :] * _window(extd, shd, r * ROWS + (kk - 1) - k, ROWS)
            a = a_ref[rows, :]
            sg = _sigmoid(g_ref[rows, :])
            dp_ref[0, rows, :] = (acc * sg).astype(BF16)
            dp_ref[1, rows, :] = (acc * a * sg * (1.0 - sg)).astype(BF16)
            d = d_ref[rows, :]
            bacc[...] += jnp.sum(d.reshape(ROWS // SUBLANE, SUBLANE, cb), axis=0)
            for k in range(kk):
                prod = d * _window(extu, shu, r * ROWS + off + k, ROWS)
                wacc[k] += jnp.sum(prod.reshape(ROWS // SUBLANE, SUBLANE, cb), axis=0)

        @pl.when(i == nt - 1)
        def _():
            for k in range(kk):
                dw_ref[k:k + 1, :] = jnp.sum(wacc[k], axis=0, keepdims=True)
            if kpad > kk:
                dw_ref[kk:kpad, :] = jnp.zeros((kpad - kk, cb), F32)
            db_ref[...] = jnp.sum(bacc[...], axis=0, keepdims=True)

    cur = lambda sec: pl.BlockSpec((None, tc, cb), lambda j, i: (sec, i, j))
    prev = lambda sec: pl.BlockSpec((None, HALO, cb), lambda j, i: (sec, jnp.maximum(i * hb - 1, 0), j))
    return _call(
        body, (du1, du1, p3, p3, p3, p3, conv_w), name="conv_bwd", grid=(w // cb, nt),
        in_specs=[pl.BlockSpec((tc, cb), lambda j, i: (i, j)),
                  pl.BlockSpec((HALO, cb), lambda j, i: (jnp.minimum((i + 1) * hb, t // HALO - 1), j)),
                  cur(0), cur(1), prev(0), prev(1), pl.BlockSpec((kk, cb), lambda j, i: (0, j))],
        out_specs=[pl.BlockSpec((2, tc, cb), lambda j, i: (0, i, j)),
                   pl.BlockSpec((kpad, cb), lambda j, i: (0, j)),
                   pl.BlockSpec((1, cb), lambda j, i: (0, j))],
        out_shape=[jax.ShapeDtypeStruct((n_sec, t, w), BF16), jax.ShapeDtypeStruct((kpad, w), F32),
                   jax.ShapeDtypeStruct((1, w), F32)],
        scratch_shapes=[pltpu.VMEM((tc + HALO, cb), F32), pltpu.VMEM((tc + HALO, cb), F32),
                        pltpu.VMEM((SUBLANE - 1, tc + HALO - SUBLANE, cb), F32),
                        pltpu.VMEM((SUBLANE - 1, tc + HALO - SUBLANE, cb), F32),
                        pltpu.VMEM((kk, SUBLANE, cb), F32), pltpu.VMEM((SUBLANE, cb), F32)], rider=rider)


def _ffn_act_fwd(hh3, fw, fb, tc, cb, rider=None):
    _, t, dff = hh3.shape
    kk = fw.shape[0]
    off = FHALO - (kk - 1)
    hb = tc // FHALO

    def body(g_ref, v_ref, gp_ref, w_ref, b_ref, act_ref, ext):
        i = pl.program_id(1)
        ext[pl.ds(FHALO, tc), :] = g_ref[...]
        ext[pl.ds(0, FHALO), :] = jnp.where(i > 0, gp_ref[...], 0.0)
        for r in range(tc // ROWS):
            rows = pl.ds(r * ROWS, ROWS)
            gc = jnp.broadcast_to(b_ref[...], (ROWS, cb))
            for k in range(kk):
                gc = gc + w_ref[k:k + 1, :] * ext[pl.ds(r * ROWS + off + k, ROWS), :]
            act_ref[rows, :] = (gc * _sigmoid(gc) * v_ref[rows, :]).astype(BF16)

    return _call(
        body, (hh3, hh3, hh3, fw, fb), name="ffn_act_fwd", grid=(dff // cb, t // tc),
        in_specs=[pl.BlockSpec((None, tc, cb), lambda j, i: (0, i, j)),
                  pl.BlockSpec((None, tc, cb), lambda j, i: (1, i, j)),
                  pl.BlockSpec((None, FHALO, cb), lambda j, i: (0, jnp.maximum(i * hb - 1, 0), j)),
                  pl.BlockSpec((kk, cb), lambda j, i: (0, j)),
                  pl.BlockSpec((1, cb), lambda j, i: (0, j))],
        out_specs=pl.BlockSpec((tc, cb), lambda j, i: (i, j)),
        out_shape=jax.ShapeDtypeStruct((t, dff), BF16),
        scratch_shapes=[pltpu.VMEM((tc + FHALO, cb), F32)], rider=rider)


def _ffn_act_bwd(dact, hh3, fw, fb, tc, cb):
    _, t, dff = hh3.shape
    kk = fw.shape[0]
    off = FHALO - (kk - 1)
    hb = tc // FHALO
    nt = t // tc
    te = tc + FHALO

    def body(da_ref, dan_ref, g_ref, gp_ref, gn_ref, v_ref, vn_ref, w_ref, b_ref,
             dhh_ref, dw_ref, db_ref, gext, dext, wacc, bacc):
        i = pl.program_id(1)

        @pl.when(i == 0)
        def _():
            wacc[...] = jnp.zeros_like(wacc)
            bacc[...] = jnp.zeros_like(bacc)

        gext[pl.ds(0, FHALO), :] = jnp.where(i > 0, gp_ref[...], 0.0)
        gext[pl.ds(FHALO, tc), :] = g_ref[...]
        gext[pl.ds(FHALO + tc, FHALO), :] = gn_ref[...]

        def gate_grad(r0, n, da, v):
            gc = jnp.broadcast_to(b_ref[...], (n, cb))
            for k in range(kk):
                gc = gc + w_ref[k:k + 1, :] * gext[pl.ds(r0 + off + k, n), :]
            sg = _sigmoid(gc)
            return gc * sg, da * v * (sg * (1.0 + gc * (1.0 - sg)))

        for r in range(tc // ROWS):
            rows = pl.ds(r * ROWS, ROWS)
            da = da_ref[rows, :]
            silu, dgc = gate_grad(r * ROWS, ROWS, da, v_ref[rows, :])
            dext[rows, :] = dgc
            dhh_ref[1, rows, :] = (da * silu).astype(BF16)
        _, dgc_next = gate_grad(tc, FHALO, dan_ref[...], vn_ref[...])
        dext[pl.ds(tc, FHALO), :] = jnp.where(i < nt - 1, dgc_next, 0.0)
        for r in range(tc // ROWS):
            rows = pl.ds(r * ROWS, ROWS)
            dg = jnp.zeros((ROWS, cb), F32)
            for k in range(kk):
                dg = dg + w_ref[k:k + 1, :] * dext[pl.ds(r * ROWS + (kk - 1) - k, ROWS), :]
            dhh_ref[0, rows, :] = dg.astype(BF16)
            dgc = dext[rows, :]
            bacc[...] += jnp.sum(dgc.reshape(ROWS // SUBLANE, SUBLANE, cb), axis=0)
            for k in range(kk):
                prod = dgc * gext[pl.ds(r * ROWS + off + k, ROWS), :]
                wacc[k] += jnp.sum(prod.reshape(ROWS // SUBLANE, SUBLANE, cb), axis=0)

        @pl.when(i == nt - 1)
        def _():
            for k in range(kk):
                dw_ref[k:k + 1, :] = jnp.sum(wacc[k], axis=0, keepdims=True)
            dw_ref[kk:SUBLANE, :] = jnp.zeros((SUBLANE - kk, cb), F32)
            db_ref[...] = jnp.sum(bacc[...], axis=0, keepdims=True)

    nxt = lambda i: jnp.minimum((i + 1) * hb, t // FHALO - 1)
    return pl.pallas_call(
        body, name="ffn_act_bwd", grid=(dff // cb, nt),
        in_specs=[pl.BlockSpec((tc, cb), lambda j, i: (i, j)),
                  pl.BlockSpec((FHALO, cb), lambda j, i: (nxt(i), j)),
                  pl.BlockSpec((None, tc, cb), lambda j, i: (0, i, j)),
                  pl.BlockSpec((None, FHALO, cb), lambda j, i: (0, jnp.maximum(i * hb - 1, 0), j)),
                  pl.BlockSpec((None, FHALO, cb), lambda j, i: (0, nxt(i), j)),
                  pl.BlockSpec((None, tc, cb), lambda j, i: (1, i, j)),
                  pl.BlockSpec((None, FHALO, cb), lambda j, i: (1, nxt(i), j)),
                  pl.BlockSpec((kk, cb), lambda j, i: (0, j)),
                  pl.BlockSpec((1, cb), lambda j, i: (0, j))],
        out_specs=[pl.BlockSpec((2, tc, cb), lambda j, i: (0, i, j)),
                   pl.BlockSpec((SUBLANE, cb), lambda j, i: (0, j)),
                   pl.BlockSpec((1, cb), lambda j, i: (0, j))],
        out_shape=[jax.ShapeDtypeStruct((2, t, dff), BF16), jax.ShapeDtypeStruct((SUBLANE, dff), F32),
                   jax.ShapeDtypeStruct((1, dff), F32)],
        scratch_shapes=[pltpu.VMEM((tc + 2 * FHALO, cb), F32), pltpu.VMEM((te, cb), F32),
                        pltpu.VMEM((kk, SUBLANE, cb), F32), pltpu.VMEM((SUBLANE, cb), F32)],
        compiler_params=_params("parallel", "arbitrary"),
    )(dact, dact, hh3, hh3, hh3, hh3, hh3, fw, fb)


def _chunk_consts():
    r = lax.broadcasted_iota(jnp.int32, (CHUNK, CHUNK), 0)
    c = lax.broadcasted_iota(jnp.int32, (CHUNK, CHUNK), 1)
    blk = (r // SUB) * SUB
    tri = (c <= r).astype(F32)
    start = (c < blk).astype(F32)
    end = (c < blk + SUB).astype(F32)
    return jnp.concatenate([tri, start, end, jnp.ones((SUBLANE, CHUNK), F32)], axis=0)


def _roll8(x, d):
    return pltpu.roll(x.reshape(CHUNK // SUB, SUB, LANE), d % SUB, 1).reshape(CHUNK, LANE)


def _gate_terms(q, fpre, lb):
    sf = _sigmoid(fpre)
    fg = lb + (1.0 - lb) * sf
    sq = _sigmoid(q)
    return sf, fg, 1.0 - fg, sq, q * sq


def _decays(g, consts):
    cs = _dot3(consts, g)
    b = cs[0:CHUNK]
    rs = cs[CHUNK:2 * CHUNK]
    re = cs[2 * CHUNK:3 * CHUNK]
    tot = cs[3 * CHUNK:3 * CHUNK + 1]
    return b, rs, re, tot


def _lower_bound(lb_ref):
    l0, l1 = lb_ref[0:1, :], lb_ref[1:2, :]
    mx = jnp.maximum(l0, l1)
    e0, e1 = jnp.exp(l0 - mx), jnp.exp(l1 - mx)
    return e0 / (e0 + e1)


def _scaled_keys(kt, rs, re, rowblk, i):
    scale = jnp.where(rowblk < i, jnp.exp(jnp.minimum(rs[SUB * i:SUB * i + 1, :] - re, 0.0)), 0.0)
    return kt * scale, scale


def _hgrn_fwd(p3, lb_logits, hg, cat, tb, hpb, rider=None):
    _, t, w = p3.shape
    nh = w // LANE
    nc = tb // CHUNK
    assert nh % hpb == 0

    def body(q_ref, f_ref, v_ref, og_ref, lb_ref, hg_ref, cat_in, cat_ref, o_ref, st_ref, state):
        del cat_in
        consts = _chunk_consts()
        lb_all = _lower_bound(lb_ref)
        rowblk = lax.broadcasted_iota(jnp.int32, (CHUNK, 1), 0) // SUB
        rowpos = lax.broadcasted_iota(jnp.int32, (CHUNK, 1), 0) % SUB

        @pl.when(pl.program_id(1) == 0)
        def _():
            state[...] = jnp.zeros_like(state)

        def chunk(c, carry):
            rows = pl.ds(pl.multiple_of(c * CHUNK, CHUNK), CHUNK)
            heads = range(hpb)
            sls = [slice(j * LANE, (j + 1) * LANE) for j in heads]
            v = [v_ref[rows, s] for s in sls]
            vb = [x.astype(BF16) for x in v]
            gates = [_gate_terms(q_ref[rows, s], f_ref[rows, s], lb_all[:, s]) for s in sls]
            fg = [g[1] for g in gates]
            kk = [g[2] for g in gates]
            qh = [g[4] for g in gates]
            dec = [_decays(jnp.log(x), consts) for x in fg]
            b = [x[0] for x in dec]
            rs = [x[1] for x in dec]
            re = [x[2] for x in dec]
            tot = [x[3] for x in dec]
            qt = [qh[j] * jnp.exp(b[j] - rs[j]) for j in heads]
            kt = [kk[j] * jnp.exp(re[j] - b[j]) for j in heads]
            st = [state[j] for j in heads]
            for j in heads:
                st_ref[j, c] = st[j]
            a = [jnp.zeros((CHUNK, CHUNK), F32) for _ in heads]
            for i in range(1, CHUNK // SUB):
                for j in heads:
                    ki, _ = _scaled_keys(kt[j], rs[j], re[j], rowblk, i)
                    a[j] = a[j] + _dot_nt(jnp.where(rowblk == i, qt[j], 0.0).astype(BF16), ki.astype(BF16))
            o = [_dot(a[j].astype(BF16), vb[j]) for j in heads]
            o = [o[j] + _dot_nt((qh[j] * jnp.exp(b[j])).astype(BF16), st[j].astype(BF16)) for j in heads]
            for j in heads:
                k_up = kk[j] * jnp.exp(tot[j] - b[j])
                state[j] = st[j] * jnp.exp(tot[j]) + _dot_tn(vb[j], k_up.astype(BF16))
            for j in heads:
                e, rf = None, fg[j]
                for d in range(SUB):
                    if d == 0:
                        vs, term = v[j], qh[j] * kk[j]
                    else:
                        e = rf if e is None else e * rf
                        rf = _roll8(fg[j], d)
                        vs = _roll8(v[j], d)
                        term = jnp.where(rowpos >= d, qh[j] * (1.0 - rf) * e, 0.0)
                    o[j] = o[j] + jnp.sum(term, axis=-1, keepdims=True) * vs
            for j in heads:
                og = og_ref[rows, sls[j]]
                o_ref[rows, sls[j]] = o[j]
                r = lax.rsqrt(jnp.mean(o[j] * o[j], axis=-1, keepdims=True) + RMS_EPS)
                cat_ref[rows, sls[j]] = (o[j] * r * hg_ref[:, sls[j]] * (og * _sigmoid(og))).astype(BF16)
            return carry

        lax.fori_loop(0, nc, chunk, 0)

    bw = hpb * LANE
    sec = lambda s: pl.BlockSpec((None, tb, bw), lambda h, i: (s, i, h))
    return _call(
        body, (p3, p3, p3, p3, lb_logits, hg, cat), name="hgrn_fwd", grid=(nh // hpb, t // tb),
        in_specs=[sec(2), sec(3), sec(4), sec(5),
                  pl.BlockSpec((2, bw), lambda h, i: (0, h)),
                  pl.BlockSpec((1, bw), lambda h, i: (0, h)), HBM],
        out_specs=[pl.BlockSpec((tb, bw), lambda h, i: (i, nh // hpb + h)),
                   pl.BlockSpec((tb, bw), lambda h, i: (i, h)),
                   pl.BlockSpec((hpb, nc, LANE, LANE), lambda h, i: (h, i, 0, 0))],
        out_shape=[jax.ShapeDtypeStruct(cat.shape, BF16), jax.ShapeDtypeStruct((t, w), F32),
                   jax.ShapeDtypeStruct((nh, t // CHUNK, LANE, LANE), F32)],
        scratch_shapes=[pltpu.VMEM((hpb, LANE, LANE), F32)], aliases={6: 0}, rider=rider)


def _hgrn_bwd(p3, lb_logits, hg, o_pre, states, dcat, dp3, tb, hpb, rider=None):
    n_sec, t, w = p3.shape
    nh = w // LANE
    assert nh % hpb == 0
    nc = tb // CHUNK
    nb = t // tb
    bw = hpb * LANE
    n_steps = (nh // hpb) * nb

    def body(q_ref, f_ref, v_ref, og_ref, lb_ref, hg_ref, o_ref, st_ref, dc_ref, dp_in,
             dp_ref, dlb_ref, dhg_ref, dstate, stash, lbacc, hgacc, osem):
        del dp_in
        h, i = pl.program_id(0), pl.program_id(1)
        step = h * nb + i
        slot = step % 2

        def out_copy(s, row_blk, lane_blk):
            dst = dp_ref.at[pl.ds(2, 4), pl.ds(row_blk * tb, tb), pl.ds(lane_blk * bw, bw)]
            return pltpu.make_async_copy(stash.at[s], dst, osem.at[s])

        @pl.when(step >= 2)
        def _():
            out_copy(slot, 0, 0).wait()

        def compute():
            consts = _chunk_consts()
            rr = lax.broadcasted_iota(jnp.int32, (CHUNK, CHUNK), 0)
            cc = lax.broadcasted_iota(jnp.int32, (CHUNK, CHUNK), 1)
            upper = (cc >= rr).astype(F32)
            lb_all = _lower_bound(lb_ref)
            rowblk = lax.broadcasted_iota(jnp.int32, (CHUNK, 1), 0) // SUB
            rowpos = lax.broadcasted_iota(jnp.int32, (CHUNK, 1), 0) % SUB

            @pl.when(i == 0)
            def _():
                dstate[...] = jnp.zeros_like(dstate)
                lbacc[...] = jnp.zeros_like(lbacc)
                hgacc[...] = jnp.zeros_like(hgacc)

            def head(j, c, rows):
                sl = slice(j * LANE, (j + 1) * LANE)
                lb = lb_all[:, sl]
                hgv = hg_ref[:, sl]
                q = q_ref[rows, sl]
                v = v_ref[rows, sl]
                og = og_ref[rows, sl]
                o = o_ref[rows, sl]
                dcg = dc_ref[rows, sl]
                sf, fg, kk, sq, qh = _gate_terms(q, f_ref[rows, sl], lb)
                b, rs, re, tot = _decays(jnp.log(fg), consts)
                eq = jnp.exp(b - rs)
                ek = jnp.exp(re - b)
                qt = qh * eq
                kt = kk * ek
                e_in = jnp.exp(b)
                e_up = jnp.exp(tot - b)
                e_tot = jnp.exp(tot)
                q_in = (qh * e_in).astype(BF16)
                k_up = (kk * e_up).astype(BF16)
                vb = v.astype(BF16)
                st = st_ref[j, c]
                dst = dstate[j]
                dstb = dst.astype(BF16)
                yield

                sg = _sigmoid(og)
                r = lax.rsqrt(jnp.mean(o * o, axis=-1, keepdims=True) + RMS_EPS)
                ohat = o * r
                d_og = dcg * ohat * hgv * (sg * (1.0 + og * (1.0 - sg)))
                d_on = dcg * (og * sg)
                hgacc[:, sl] += jnp.sum((d_on * ohat).reshape(CHUNK // SUBLANE, SUBLANE, LANE), axis=0)
                d_oh = d_on * hgv
                do = r * (d_oh - ohat * jnp.mean(d_oh * ohat, axis=-1, keepdims=True))
                dob = do.astype(BF16)

                da = _dot_nt(dob, vb)
                yield
                a_off = jnp.zeros((CHUNK, CHUNK), F32)
                dqt = jnp.zeros((CHUNK, LANE), F32)
                dkt = jnp.zeros((CHUNK, LANE), F32)
                for blk in range(1, CHUNK // SUB):
                    ki, scale = _scaled_keys(kt, rs, re, rowblk, blk)
                    kib = ki.astype(BF16)
                    qib = jnp.where(rowblk == blk, qt, 0.0).astype(BF16)
                    dab = jnp.where(rowblk == blk, da, 0.0).astype(BF16)
                    a_off = a_off + _dot_nt(qib, kib)
                    dqt = dqt + _dot(dab, kib)
                    dkt = dkt + _dot_tn(dab, qib) * scale
                    yield
                dqh = dqt * eq
                dk = dkt * ek
                dv = _dot_tn(a_off.astype(BF16), dob)

                dqh = dqh + _dot(dob, st.astype(BF16)) * e_in
                dk = dk + _dot(vb, dstb) * e_up
                dv = dv + _dot_nt(k_up, dstb)
                st_end = st * e_tot + _dot_tn(vb, k_up)
                carry_g = jnp.sum(st_end * dst, axis=0, keepdims=True)
                dstate[j] = dst * e_tot + _dot_tn(dob, q_in)
                yield

                e, rf = None, fg
                for d in range(SUB):
                    if d == 0:
                        a_d = jnp.sum(qh * kk, axis=-1, keepdims=True)
                        da_d = jnp.sum(do * v, axis=-1, keepdims=True)
                        dqh = dqh + da_d * kk
                        dk = dk + da_d * qh
                        dv = dv + a_d * do
                        continue
                    e = rf if e is None else e * rf
                    rf = _roll8(fg, d)
                    em = jnp.where(rowpos >= d, e, 0.0)
                    ks, vs = 1.0 - rf, _roll8(v, d)
                    a_d = jnp.sum(qh * ks * em, axis=-1, keepdims=True)
                    da_d = jnp.sum(do * vs, axis=-1, keepdims=True) * em
                    dqh = dqh + da_d * ks
                    dk = dk + _roll8(da_d * qh, -d)
                    dv = dv + _roll8(a_d * do, -d)
                yield

                dg = _dot3(upper, qh * dqh - kk * dk) + carry_g
                dfg = dg / fg - dk
                lbacc[:, sl] += jnp.sum((dfg * (1.0 - sf)).reshape(CHUNK // SUBLANE, SUBLANE, LANE), axis=0)
                stash[slot, 0, rows, sl] = (dqh * (sq * (1.0 + q * (1.0 - sq)))).astype(BF16)
                stash[slot, 1, rows, sl] = (dfg * (1.0 - lb) * sf * (1.0 - sf)).astype(BF16)
                stash[slot, 2, rows, sl] = dv.astype(BF16)
                stash[slot, 3, rows, sl] = d_og.astype(BF16)

            def chunk(cr, carry):
                c = nc - 1 - cr
                rows = pl.ds(pl.multiple_of(c * CHUNK, CHUNK), CHUNK)
                running = [head(j, c, rows) for j in range(hpb)]
                while running:
                    running = [g for g in running if next(g, StopIteration) is not StopIteration]
                return carry

            lax.fori_loop(0, nc, chunk, 0)

            @pl.when(i == nb - 1)
            def _():
                dlb_ref[...] = jnp.sum(lbacc[...], axis=0, keepdims=True)
                dhg_ref[...] = jnp.sum(hgacc[...], axis=0, keepdims=True)

        compute()
        out_copy(slot, nb - 1 - i, h).start()

        @pl.when(step == n_steps - 1)
        def _():
            out_copy(slot, 0, 0).wait()
            if n_steps >= 2:
                out_copy(1 - slot, 0, 0).wait()

    rev = lambda i: nb - 1 - i
    sec = lambda s: pl.BlockSpec((None, tb, bw), lambda h, i: (s, rev(i), h))
    return _call(
        body, (p3, p3, p3, p3, lb_logits, hg, o_pre, states, dcat, dp3), name="hgrn_bwd", grid=(nh // hpb, nb),
        in_specs=[sec(2), sec(3), sec(4), sec(5),
                  pl.BlockSpec((2, bw), lambda h, i: (0, h)),
                  pl.BlockSpec((1, bw), lambda h, i: (0, h)),
                  pl.BlockSpec((tb, bw), lambda h, i: (rev(i), h)),
                  pl.BlockSpec((hpb, nc, LANE, LANE), lambda h, i: (h, rev(i), 0, 0)),
                  pl.BlockSpec((tb, bw), lambda h, i: (rev(i), nh // hpb + h)), HBM],
        out_specs=[HBM,
                   pl.BlockSpec((1, bw), lambda h, i: (0, h)),
                   pl.BlockSpec((1, bw), lambda h, i: (0, h))],
        out_shape=[jax.ShapeDtypeStruct((n_sec, t, w), BF16), jax.ShapeDtypeStruct((1, w), F32),
                   jax.ShapeDtypeStruct((1, w), F32)],
        scratch_shapes=[pltpu.VMEM((hpb, LANE, LANE), F32), pltpu.VMEM((2, 4, tb, bw), BF16),
                        pltpu.VMEM((SUBLANE, bw), F32), pltpu.VMEM((SUBLANE, bw), F32),
                        pltpu.SemaphoreType.DMA((2,))],
        aliases={9: 0}, rider=rider)


def _place():
    x, y, c = lax.axis_index("x"), lax.axis_index("y"), lax.axis_index("c")
    chips = [(1 - x, y), (x, 1 - y), (1 - x, 1 - y)]
    return x, y, c, chips


def _rows(buf, px, py, pc, part=None):
    half = buf.shape[1] // 2
    if part is None:
        return buf.at[2 * px + py, pl.ds(pc * half, half)]
    lo, hi, n = part
    piece = half // n
    return buf.at[2 * px + py, pl.ds(pc * half + lo * piece, (hi - lo) * piece)]


def _rcopy(src, dst, send, recv, idx, to):
    return pltpu.make_async_remote_copy(src_ref=src, dst_ref=dst, send_sem=send.at[idx], recv_sem=recv.at[idx],
                                        device_id=to, device_id_type=MESH)


def _same(bufs):
    return [jax.ShapeDtypeStruct(b.shape, b.dtype) for b in bufs]


def _ride_gather_ici(bufs, parts=None):
    n = len(bufs)
    parts = parts or [None] * n

    def start(rin, rout, send, recv):
        x, y, c, chips = _place()
        for k in range(n):
            mine = _rows(rout[k], x, y, c, parts[k])
            for j, chip in enumerate(chips):
                _rcopy(mine, mine, send, recv, 3 * k + j, (*chip, c)).start()

    def finish(rin, rout, send, recv):
        x, y, c, chips = _place()
        for k in range(n):
            for j, chip in enumerate(chips):
                theirs = _rows(rout[k], *chip, c, parts[k])
                _rcopy(theirs, theirs, send, recv, 3 * k + j, (x, y, c)).wait_recv()
        for k in range(n):
            mine = _rows(rout[k], x, y, c, parts[k])
            for j in range(3):
                _rcopy(mine, mine, send, recv, 3 * k + j, (x, y, c)).wait_send()

    return _Rider(bufs, _same(bufs), {k: k for k in range(n)}, 3 * n, start, finish)


class _SemView:
    def __init__(self, ref, base):
        self.ref, self.base = ref, base

    @property
    def at(self):
        return self

    def __getitem__(self, idx):
        return self.ref.at[idx + self.base]


def _ride_both(a, b):
    nai, nao = len(a.ins), len(a.outs)

    def start(rin, rout, send, recv):
        a.start(rin[:nai], rout[:nao], send, recv)
        b.start(rin[nai:], rout[nao:], _SemView(send, a.n_sems), _SemView(recv, a.n_sems))

    def finish(rin, rout, send, recv):
        a.finish(rin[:nai], rout[:nao], send, recv)
        b.finish(rin[nai:], rout[nao:], _SemView(send, a.n_sems), _SemView(recv, a.n_sems))

    aliases = dict(a.aliases)
    aliases.update({nai + ri: nao + ro for ri, ro in b.aliases.items()})
    return _Rider(a.ins + b.ins, a.outs + b.outs, aliases, a.n_sems + b.n_sems, start, finish)


def _ride_gather_d2d(bufs):
    n = len(bufs)

    def start(rin, rout, send, recv):
        x, y, c, chips = _place()
        for k in range(n):
            for j, chip in enumerate(chips):
                got = _rows(rout[k], *chip, c)
                _rcopy(got, got, send, recv, 3 * k + j, (x, y, 1 - c)).start()

    def finish(rin, rout, send, recv):
        x, y, c, chips = _place()
        for k in range(n):
            for j, chip in enumerate(chips):
                theirs = _rows(rout[k], *chip, 1 - c)
                _rcopy(theirs, theirs, send, recv, 3 * k + j, (x, y, c)).wait_recv()
        for k in range(n):
            for j, chip in enumerate(chips):
                got = _rows(rout[k], *chip, c)
                _rcopy(got, got, send, recv, 3 * k + j, (x, y, c)).wait_send()

    return _Rider(bufs, _same(bufs), {k: k for k in range(n)}, 3 * n, start, finish)


def _ride_swap(grads):
    n = len(grads)

    def copy(k, rin, rout, send, recv):
        x, y, c, _ = _place()
        half = rin[k].shape[1] // 2
        return _rcopy(rin[k].at[:, pl.ds((1 - c) * half, half)], rout[k], send, recv, k, (x, y, 1 - c))

    def start(rin, rout, send, recv):
        for k in range(n):
            copy(k, rin, rout, send, recv).start()

    def finish(rin, rout, send, recv):
        for k in range(n):
            copy(k, rin, rout, send, recv).wait()

    outs = [jax.ShapeDtypeStruct((g.shape[0], g.shape[1] // 2, g.shape[2]), g.dtype) for g in grads]
    return _Rider(grads, outs, {}, n, start, finish)


def _ride_send_partials(parts, pieces=None, into=None):
    n = len(parts)
    pieces = pieces or [None] * n

    def cut(ref, k):
        if pieces[k] is None:
            return ref
        lo, hi, m = pieces[k]
        q = ref.shape[0] // m
        return ref.at[pl.ds(lo * q, (hi - lo) * q)]

    def copies(rin, rout, send, recv):
        x, y, c, chips = _place()
        return [_rcopy(cut(rin[k].at[2 * px + py], k), cut(rout[k].at[j], k), send, recv, 3 * k + j, (px, py, c))
                for k in range(n) for j, (px, py) in enumerate(chips)]

    def start(rin, rout, send, recv):
        for cp in copies(rin, rout, send, recv):
            cp.start()

    def finish(rin, rout, send, recv):
        for cp in copies(rin, rout, send, recv):
            cp.wait()

    if into is None:
        outs = [jax.ShapeDtypeStruct((3,) + p.shape[1:], p.dtype) for p in parts]
        return _Rider(parts, outs, {}, 3 * n, start, finish)
    return _Rider(list(parts) + list(into), _same(into), {n + k: k for k in range(n)}, 3 * n, start, finish)


def _ride_join(bufs):
    n = len(bufs)

    def half_of(buf, pc):
        half = buf.shape[0] // 2
        return buf.at[pl.ds(pc * half, half)]

    def start(rin, rout, send, recv):
        x, y, c, _ = _place()
        for k in range(n):
            mine = half_of(rout[k], c)
            _rcopy(mine, mine, send, recv, k, (x, y, 1 - c)).start()

    def finish(rin, rout, send, recv):
        x, y, c, _ = _place()
        for k in range(n):
            mine, theirs = half_of(rout[k], c), half_of(rout[k], 1 - c)
            _rcopy(mine, mine, send, recv, k, (x, y, c)).wait_send()
            _rcopy(theirs, theirs, send, recv, k, (x, y, c)).wait_recv()

    return _Rider(bufs, _same(bufs), {k: k for k in range(n)}, n, start, finish)


def _run(name, rider):
    def body(*refs):
        nri, nro = len(rider.ins), len(rider.outs)
        rin, rout = refs[:nri], refs[nri:nri + nro]
        send, recv = refs[nri + nro:]
        rider.start(rin, rout, send, recv)
        rider.finish(rin, rout, send, recv)

    return pl.pallas_call(
        body, name=name, in_specs=[HBM] * len(rider.ins), out_specs=[HBM] * len(rider.outs), out_shape=rider.outs,
        scratch_shapes=[pltpu.SemaphoreType.DMA((rider.n_sems,)), pltpu.SemaphoreType.DMA((rider.n_sems,))],
        input_output_aliases=rider.aliases,
    )(*rider.ins)


def _add_halves(name, g, other, c_idx):
    s, r, cols = g.shape
    half = r // 2
    tr = _div_tile(half, 16, 512)
    nb = half // tr

    def body(c_ref, g_ref, o_ref, q_ref):
        del c_ref
        q_ref[...] = (g_ref[...] + o_ref[...]).astype(BF16)

    return pl.pallas_call(
        body, name=name,
        grid_spec=pltpu.PrefetchScalarGridSpec(
            num_scalar_prefetch=1, grid=(s, nb),
            in_specs=[pl.BlockSpec((None, tr, cols), lambda k, i, c: (k, c[0] * nb + i, 0)),
                      pl.BlockSpec((None, tr, cols), lambda k, i, c: (k, i, 0))],
            out_specs=pl.BlockSpec((None, tr, cols), lambda k, i, c: (k, i, 0))),
        out_shape=jax.ShapeDtypeStruct((s, half, cols), BF16),
        compiler_params=_params("parallel", "parallel"),
    )(c_idx, g, other)


def _sum_partials(name, part, arrived, place_idx):
    _, half, cols = part.shape
    tr = _div_tile(half, 16, 512)
    nb = half // tr

    def body(s_ref, p_ref, a_ref, o_ref):
        del s_ref
        o_ref[...] = ((p_ref[...].astype(F32) + a_ref[0].astype(F32)) + a_ref[1].astype(F32)) + a_ref[2].astype(F32)

    return pl.pallas_call(
        body, name=name,
        grid_spec=pltpu.PrefetchScalarGridSpec(
            num_scalar_prefetch=1, grid=(nb,),
            in_specs=[pl.BlockSpec((None, tr, cols), lambda i, s: (s[0], i, 0)),
                      pl.BlockSpec((3, tr, cols), lambda i, s: (0, i, 0))],
            out_specs=pl.BlockSpec((tr, cols), lambda i, s: (s[1] * nb + i, 0))),
        out_shape=jax.ShapeDtypeStruct((2 * half, cols), F32),
        compiler_params=_params("parallel"),
    )(place_idx, part, arrived)


def _small_allreduce(wide_rows, ffn_rows, w, dff, n_wide, n_ffn):
    n_in = len(wide_rows) + len(ffn_rows)

    def body(*refs):
        ins = refs[:n_in]
        s1_ref, s2_ref, r1, r2, p1, p2, send, recv = refs[n_in:]
        x, y, c, _ = _place()
        me = 4 * x + 2 * y + c
        p1[...] = jnp.zeros_like(p1)
        p2[...] = jnp.zeros_like(p2)
        row = 0
        for ref, (_, r, m) in zip(ins, wide_rows):
            if m == 1 and r % SUBLANE == 0 and row % SUBLANE == 0:
                p1[row:row + r, :] = ref[...]
                row += r
                continue
            for rr in range(r):
                for mm in range(m):
                    p1[row:row + 1, :] = ref[rr:rr + 1, mm * w:(mm + 1) * w]
                    row += 1
        row = 0
        for ref, arr in zip(ins[len(wide_rows):], ffn_rows):
            r = arr.shape[0]
            p2[row:row + r, :] = ref[...]
            row += r
        r1[me] = p1[...]
        r2[me] = p2[...]
        cps = []
        for mask in range(1, 8):
            peer = (x ^ (mask >> 2), y ^ ((mask >> 1) & 1), c ^ (mask & 1))
            for a, (src, dst) in enumerate(((p1, r1), (p2, r2))):
                cp = pltpu.make_async_remote_copy(
                    src_ref=src, dst_ref=dst.at[me], send_sem=send.at[a, mask - 1], recv_sem=recv.at[a, mask - 1],
                    device_id=peer, device_id_type=MESH)
                cp.start()
                cps.append(cp)
        for cp in cps:
            cp.wait()
        t1, t2 = r1[0], r2[0]
        for d in range(1, 8):
            t1 = t1 + r1[d]
            t2 = t2 + r2[d]
        s1_ref[...] = t1
        s2_ref[...] = t2

    ins = [a for a, _, _ in wide_rows] + list(ffn_rows)
    return pl.pallas_call(
        body, name="small_allreduce", in_specs=[VMEM_FULL] * n_in, out_specs=[VMEM_FULL, VMEM_FULL],
        out_shape=[jax.ShapeDtypeStruct((n_wide, w), F32), jax.ShapeDtypeStruct((n_ffn, dff), F32)],
        scratch_shapes=[pltpu.VMEM((8, n_wide, w), F32), pltpu.VMEM((8, n_ffn, dff), F32),
                        pltpu.VMEM((n_wide, w), F32), pltpu.VMEM((n_ffn, dff), F32),
                        pltpu.SemaphoreType.DMA((2, 7)), pltpu.SemaphoreType.DMA((2, 7))],
        compiler_params=pltpu.CompilerParams(vmem_limit_bytes=VMEM_LIMIT),
    )(*ins)


def _adamw(w, g, m, v):
    m2 = ADAM_B1 * m + (1.0 - ADAM_B1) * g
    v2 = ADAM_B2 * v + (1.0 - ADAM_B2) * (g * g)
    m_hat = m2 / (1.0 - ADAM_B1 ** ADAM_STEP)
    v_hat = v2 / (1.0 - ADAM_B2 ** ADAM_STEP)
    delta = -ADAM_LR * (m_hat / (jnp.sqrt(v_hat) + ADAM_EPS) + ADAM_WD * w)
    return delta, m2, v2


def _adam_big(name, w, g, m, v, rider=None):
    r, c = w.shape
    tr = 128 if r % 128 == 0 else r

    def body(w_ref, g_ref, m_ref, v_ref, d_ref, m2_ref, v2_ref):
        d_ref[...], m2_ref[...], v2_ref[...] = _adamw(w_ref[...], g_ref[...], m_ref[...], v_ref[...])

    blk = pl.BlockSpec((tr, c), lambda i: (i, 0))
    return _call(
        body, (w, g, m, v), name=name, grid=(r // tr,), in_specs=[blk] * 4, out_specs=[blk] * 3,
        out_shape=[jax.ShapeDtypeStruct((r, c), F32)] * 3, rider=rider)


def _adam_small(s1, s2, cw_g, fw_g, lb_logits, triples, layout, w):
    n = len(triples)

    def body(*refs):
        s1_ref, s2_ref, cw_ref, fw_ref, lbl_ref = refs[:5]
        prm = refs[5:5 + 3 * n]
        outs = refs[5 + 3 * n:]
        for p, lay in enumerate(layout):
            w_ref, m_ref, v_ref = prm[3 * p:3 * p + 3]
            g_ref, d_ref, m2_ref, v2_ref = outs[4 * p:4 * p + 4]
            if lay[0] == "wide":
                _, row, r, pieces = lay
                for rr in range(r):
                    for mm in range(pieces):
                        g_ref[rr:rr + 1, mm * w:(mm + 1) * w] = s1_ref[row:row + 1, :]
                        row += 1
            elif lay[0] == "ffn":
                _, row, r = lay
                g_ref[...] = s2_ref[row:row + r, :]
            elif lay[0] == "cw":
                g_ref[...] = cw_ref[0:g_ref.shape[0], :]
            elif lay[0] == "fw":
                g_ref[...] = fw_ref[0:g_ref.shape[0], :]
            else:
                s0 = _lower_bound(lbl_ref)
                d0 = s1_ref[lay[1]:lay[1] + 1, :] * s0 * (1.0 - s0)
                g_ref[0:1, :] = d0
                g_ref[1:2, :] = -d0
            d_ref[...], m2_ref[...], v2_ref[...] = _adamw(w_ref[...], g_ref[...], m_ref[...], v_ref[...])

    flat = [a for tr in triples for a in tr]
    shapes = []
    for tr in triples:
        shapes.extend([jax.ShapeDtypeStruct(tr[0].shape, F32)] * 4)
    return pl.pallas_call(
        body, name="adam_small", in_specs=[VMEM_FULL] * (5 + 3 * n), out_specs=[VMEM_FULL] * (4 * n),
        out_shape=shapes, compiler_params=pltpu.CompilerParams(vmem_limit_bytes=VMEM_LIMIT),
    )(s1, s2, cw_g, fw_g, lb_logits, *flat)


def _row_tile(t):
    return 512 if t % 512 == 0 and t >= 2048 else 128


def kernel(x, emb_ln_g, emb_ln_b, w_in, conv_w, conv_b, conv_norm_g, conv_norm_b, lb_logits, hgrn_norm_g, w_out, ln1_g, ln1_b, w_ffn_up, ffn_conv_w, ffn_conv_b, w_ffn_down, ln2_g, ln2_b, loss_target, m_emb_ln_g, m_emb_ln_b, m_w_in, m_conv_w, m_conv_b, m_conv_norm_g, m_conv_norm_b, m_lb_logits, m_hgrn_norm_g, m_w_out, m_ln1_g, m_ln1_b, m_w_ffn_up, m_ffn_conv_w, m_ffn_conv_b, m_w_ffn_down, m_ln2_g, m_ln2_b, v_emb_ln_g, v_emb_ln_b, v_w_in, v_conv_w, v_conv_b, v_conv_norm_g, v_conv_norm_b, v_lb_logits, v_hgrn_norm_g, v_w_out, v_ln1_g, v_ln1_b, v_w_ffn_up, v_ffn_conv_w, v_ffn_conv_b, v_w_ffn_down, v_ln2_g, v_ln2_b):
    depth = w_in.shape[0]
    assert depth == 1 and x.shape[0] == 1
    alpha = (2.0 * depth) ** 0.25
    t, d = x.shape[1], x.shape[2]
    w = d // 2
    dff = ffn_conv_b.shape[1]
    kc = conv_w.shape[1]
    assert w % (2 * LANE) == 0 and dff % (4 * LANE) == 0 and t % 128 == 0
    tm = _row_tile(t)
    tm2 = tm // 2
    tmm = 1024 if t % 1024 == 0 and t >= 2048 else tm
    cb = 2 * LANE
    cbf = 4 * LANE
    tb = tm
    nh = w // LANE
    hpb = 4 if nh % 4 == 0 else 2

    xi = lax.axis_index("x")
    yi = lax.axis_index("y")
    ci = lax.axis_index("c")
    chip = 2 * xi + yi
    c_idx = jnp.reshape(ci, (1,)).astype(jnp.int32)
    chip_idx = jnp.reshape(chip, (1,)).astype(jnp.int32)
    place_idx = jnp.stack([chip, ci]).astype(jnp.int32)

    x2 = x[0]
    tgt = loss_target[0]
    g0, b0 = emb_ln_g.reshape(1, d), emb_ln_b.reshape(1, d)
    w_in2, w_out2, w_up2, w_dn2 = w_in[0], w_out[0], w_ffn_up[0], w_ffn_down[0]
    cw2, fw2 = conv_w[0], ffn_conv_w[0]

    b_in = _place_shard(w_in2, "place_w_in", chip_idx, BF16)
    b_out = _place_shard(w_out2, "place_w_out", chip_idx, BF16)
    b_up = _place_shard(w_up2, "place_w_up", chip_idx, BF16)
    b_dn = _place_shard(w_dn2, "place_w_down", chip_idx, BF16)
    b_cw = _place_shard(_pad_rows(cw2), "place_conv_w", chip_idx, F32)
    b_fw = _place_shard(_pad_rows(fw2), "place_ffn_conv_w", chip_idx, F32)
    first = _run("gather_first_ici", _ride_gather_ici([b_in, b_cw, b_fw]))
    w_in3, cw_full3, fw_full3 = _run("gather_first_d2d", _ride_gather_d2d(first))
    cw_full = _unshard_cols(cw_full3)[:kc]
    fw_full = _unshard_cols(fw_full3)[:fw2.shape[0]]

    h0b = _ln0(x2, g0, b0, tm)
    p3, (b_out, b_up) = _proj("in_proj", h0b, w_in3, 6, 2 * tmm if t % (2 * tmm) == 0 else tmm, w // 2,
                              rider=_ride_gather_ici([b_out, b_up], [None, (0, 1, 4)]))
    (cat, u1), (w_out3, b_up) = _conv_fwd(
        p3, cw_full, conv_b, conv_norm_g, conv_norm_b, tm2, cb,
        rider=_ride_both(_ride_gather_d2d([b_out]), _ride_gather_ici([b_up], [(1, 2, 4)])))
    w_out_full = w_out3.reshape(d, d)
    (cat, o_pre, states), got = _hgrn_fwd(p3, lb_logits, hgrn_norm_g, cat, tb, hpb,
                                          rider=_ride_gather_ici([b_up], [(2, 4, 4)]))
    (xhat1, h1b, rstd1), (w_up3,) = _mix_ln1(cat, w_out_full, x2, g0, b0, ln1_g, ln1_b, alpha, tm2,
                                             rider=_ride_gather_d2d(got))
    hh3, got = _proj("ffn_up", h1b, w_up3, 2, tmm, dff // 4, rider=_ride_gather_ici([b_dn]))
    act, (w_dn3,) = _ffn_act_fwd(hh3, fw_full, ffn_conv_b, tm, cbf, rider=_ride_gather_d2d(got))
    ks = dff // N_CHIPS
    ffn = _wgrad("ffn_down", act, w_dn3, (t, d), (t // tmm, 1, N_CHIPS),
                 pl.BlockSpec((tmm, ks), lambda i, j, k: (i, k)),
                 pl.BlockSpec((None, ks, d), lambda i, j, k: (k, 0, 0)),
                 pl.BlockSpec((tmm, d), lambda i, j, k: (i, 0)), dot=_dot)
    dz2, dz2b, dg2, db2, loss_row = _ln2_loss(ffn, xhat1, tgt, ln1_g, ln1_b, ln2_g, ln2_b, alpha, tm2)

    dact = _proj_t("ffn_down_t", dz2b, w_dn3.reshape(dff, d), tmm, ks)
    dhh3, dfw, dfb = _ffn_act_bwd(dact, hh3, fw_full, ffn_conv_b, tm, cbf)
    tt = 2 * tmm if t % (2 * tmm) == 0 else tmm
    d_w_dn = _wgrad("wgrad_down", act, dz2b, (N_CHIPS, ks, d), (N_CHIPS, 2, t // tt),
                    pl.BlockSpec((tt, ks), lambda s, j, k: (k, s)),
                    pl.BlockSpec((tt, d // 2), lambda s, j, k: (k, j)),
                    pl.BlockSpec((None, ks, d // 2), lambda s, j, k: (s, 0, j)))
    wu = 2 * dff // N_CHIPS
    tnu = wu // 2
    per_sec_u = dff // tnu
    pre1, (arr_dn,) = _wgrad(
        "up_t", dhh3, w_up3, (t, d), (t // tmm, 1, 2 * N_CHIPS),
        pl.BlockSpec((None, tmm, tnu), lambda i, j, k: (k // per_sec_u, i, k % per_sec_u)),
        pl.BlockSpec((None, d, tnu), lambda i, j, k: (k // 2, 0, k % 2)),
        pl.BlockSpec((tmm, d), lambda i, j, k: (i, 0)), dot=_dot_nt, rider=_ride_swap([d_w_dn]))
    dz1, dz1b, dg1, db1 = _ln1_bwd(pre1, dz2, xhat1, rstd1, ln1_g, alpha, tm2)
    part_dn = _add_halves("add_halves_w_down", d_w_dn, arr_dn, c_idx)
    d_w_up, (land_dn,) = _wgrad(
        "wgrad_up", h1b, dhh3, (N_CHIPS, d, wu), (N_CHIPS, 2, 2, t // tt),
        pl.BlockSpec((tt, d // 2), lambda s, r, j, k: (k, r)),
        pl.BlockSpec((None, tt, tnu), lambda s, r, j, k: ((2 * s + j) // per_sec_u, k, (2 * s + j) % per_sec_u)),
        pl.BlockSpec((None, d // 2, tnu), lambda s, r, j, k: (s, r, j)), rider=_ride_send_partials([part_dn]))
    dcat = _proj_t("out_proj_t", dz1b, w_out_full, tmm, d // 2)
    d_w_out = _wgrad("wgrad_out", cat, dz1b, (d, d), (2, 2, t // tt),
                     pl.BlockSpec((tt, d // 2), lambda r, j, k: (k, r)),
                     pl.BlockSpec((tt, d // 2), lambda r, j, k: (k, j)),
                     pl.BlockSpec((d // 2, d // 2), lambda r, j, k: (r, j))).reshape(N_CHIPS, d // N_CHIPS, d)
    du1, dcng, dcnb = _conv_norm_bwd(dcat, u1, conv_norm_g, conv_norm_b, tm)
    (dp3, dcw, dcb), (arr_up, arr_out) = _conv_bwd(du1, p3, cw_full, tm2, cb, rider=_ride_swap([d_w_up, d_w_out]))
    part_up = _add_halves("add_halves_w_up", d_w_up, arr_up, c_idx)
    part_out = _add_halves("add_halves_w_out", d_w_out, arr_out, c_idx)
    (dp3, dlb, dhg), (land_up, land_out) = _hgrn_bwd(
        p3, lb_logits, hgrn_norm_g, o_pre, states, dcat, dp3, tb, hpb,
        rider=_ride_send_partials([part_up, part_out], [(0, 3, 4), None]))
    wi = 6 * w // N_CHIPS
    tni = w // 2
    d_w_in, (land_up,) = _wgrad(
        "wgrad_in", h0b, dp3, (N_CHIPS, d, wi), (N_CHIPS, 2, wi // tni, t // tt),
        pl.BlockSpec((tt, d // 2), lambda s, r, j, k: (k, r)),
        pl.BlockSpec((None, tt, tni), lambda s, r, j, k: (((wi // tni) * s + j) // 2, k, ((wi // tni) * s + j) % 2)),
        pl.BlockSpec((None, d // 2, tni), lambda s, r, j, k: (s, r, j)),
        rider=_ride_send_partials([part_up], [(3, 4, 4)], into=[land_up]))
    (arr_in,) = _run("swap_w_in", _ride_swap([d_w_in]))
    part_in = _add_halves("add_halves_w_in", d_w_in, arr_in, c_idx)
    pre0, (land_in,) = _in_t(dp3, w_in3, tmm, rider=_ride_send_partials([part_in]))
    dx, dg0, db0 = _ln0_bwd(pre0, dz1, x2, g0, alpha, tm2)
    h_in, h_out, h_up, h_dn = [
        _sum_partials("sum_partials_" + nm, p, a, place_idx)
        for nm, p, a in (("w_in", part_in, land_in), ("w_out", part_out, land_out),
                         ("w_up", part_up, land_up), ("w_down", part_dn, land_dn))]
    (g_w_up,) = _run("join_w_up", _ride_join([h_up]))

    kpad = dcw.shape[0]
    wide = [(dcw, kpad, 1), (dg0, 1, 2), (db0, 1, 2), (dg1, 1, 2), (db1, 1, 2), (dg2, 1, 2), (db2, 1, 2),
            (dcb, 1, 1), (dcng, 1, 1), (dcnb, 1, 1), (dlb, 1, 1), (dhg, 1, 1)]
    n_wide = sum(r * m for _, r, m in wide)
    n_wide_pad = -(-n_wide // SUBLANE) * SUBLANE
    s1, s2 = _small_allreduce(wide, [dfw, dfb], w, dff, n_wide_pad, 2 * SUBLANE)
    cw_g = lax.dynamic_slice_in_dim(s1[0:kpad], chip * (w // N_CHIPS), w // N_CHIPS, axis=1)
    fw_g = lax.dynamic_slice_in_dim(s2[0:SUBLANE], chip * (dff // N_CHIPS), dff // N_CHIPS, axis=1)

    small = [
        (g0, m_emb_ln_g.reshape(1, d), v_emb_ln_g.reshape(1, d)), (b0, m_emb_ln_b.reshape(1, d), v_emb_ln_b.reshape(1, d)),
        (cw2, m_conv_w[0], v_conv_w[0]), (conv_b, m_conv_b, v_conv_b),
        (conv_norm_g, m_conv_norm_g, v_conv_norm_g), (conv_norm_b, m_conv_norm_b, v_conv_norm_b),
        (lb_logits, m_lb_logits, v_lb_logits), (hgrn_norm_g, m_hgrn_norm_g, v_hgrn_norm_g),
        (ln1_g, m_ln1_g, v_ln1_g), (ln1_b, m_ln1_b, v_ln1_b),
        (fw2, m_ffn_conv_w[0], v_ffn_conv_w[0]), (ffn_conv_b, m_ffn_conv_b, v_ffn_conv_b),
        (ln2_g, m_ln2_g, v_ln2_g), (ln2_b, m_ln2_b, v_ln2_b),
    ]
    r0 = kpad
    layout = [("wide", r0, 1, 2), ("wide", r0 + 2, 1, 2), ("cw",), ("wide", r0 + 12, 1, 1), ("wide", r0 + 13, 1, 1),
              ("wide", r0 + 14, 1, 1), ("lb", r0 + 15), ("wide", r0 + 16, 1, 1), ("wide", r0 + 4, 1, 2),
              ("wide", r0 + 6, 1, 2), ("fw",), ("ffn", SUBLANE, 1), ("wide", r0 + 8, 1, 2), ("wide", r0 + 10, 1, 2)]
    so = _adam_small(s1, s2, cw_g, fw_g, lb_logits, small, layout, w)
    sm = {nm: so[4 * i:4 * i + 4] for i, nm in enumerate(
        ["emb_ln_g", "emb_ln_b", "conv_w", "conv_b", "conv_norm_g", "conv_norm_b", "lb_logits", "hgrn_norm_g",
         "ln1_g", "ln1_b", "ffn_conv_w", "ffn_conv_b", "ln2_g", "ln2_b"])}
    adam_up, (g_w_in, g_w_out, g_w_dn) = _adam_big("adam_w_ffn_up", w_up2, g_w_up, m_w_ffn_up[0], v_w_ffn_up[0],
                                                   rider=_ride_join([h_in, h_out, h_dn]))
    bigs = {"w_ffn_up": (g_w_up,) + tuple(adam_up)}
    for nm, wt, g, m, v in (("w_in", w_in2, g_w_in, m_w_in[0], v_w_in[0]), ("w_out", w_out2, g_w_out, m_w_out[0], v_w_out[0]),
                            ("w_ffn_down", w_dn2, g_w_dn, m_w_ffn_down[0], v_w_ffn_down[0])):
        bigs[nm] = (g,) + tuple(_adam_big("adam_" + nm, wt, g, m, v))

    loss = lax.psum(loss_row[0, 0], ("x", "y", "c"))

    order = ["emb_ln_g", "emb_ln_b", "w_in", "conv_w", "conv_b", "conv_norm_g", "conv_norm_b", "lb_logits",
             "hgrn_norm_g", "w_out", "ln1_g", "ln1_b", "w_ffn_up", "ffn_conv_w", "ffn_conv_b", "w_ffn_down",
             "ln2_g", "ln2_b"]
    shapes = dict(emb_ln_g=emb_ln_g.shape, emb_ln_b=emb_ln_b.shape, w_in=w_in.shape, conv_w=conv_w.shape,
                  w_out=w_out.shape, w_ffn_up=w_ffn_up.shape, ffn_conv_w=ffn_conv_w.shape, w_ffn_down=w_ffn_down.shape)
    outs = [loss, dx.reshape(x.shape)]
    for which in range(4):
        for nm in order:
            a = bigs[nm][which] if nm in bigs else sm[nm][which]
            outs.append(a.reshape(shapes[nm]) if nm in shapes else a)
    return tuple(outs)


def _pad_rows(a):
    k = a.shape[0]
    kp = -(-k // 16) * 16
    return jnp.pad(a, ((0, kp - k), (0, 0)))


def _unshard_cols(a3):
    s, k, c = a3.shape
    return jnp.transpose(a3, (1, 0, 2)).reshape(k, s * c)
```

```python
import functools

import jax
import jax.numpy as jnp
from jax import lax
from jax.experimental import pallas as pl
from jax.experimental.pallas import tpu as pltpu

F32 = jnp.float32
BF16 = jnp.bfloat16

LN_EPS = 1e-5
RMS_EPS = 1e-6
LANE = 128
SUBLANE = 8
CHUNK = 64
SUB = 8
HALO = 32
FHALO = 8
ROWS = 64
N_CHIPS = 4
VMEM_LIMIT = 56 << 20
NEG_BIG = -1e30

ADAM_LR = 0.001
ADAM_B1 = 0.9
ADAM_B2 = 0.999
ADAM_EPS = 1e-08
ADAM_WD = 0.01
ADAM_STEP = 10

MESH = pl.DeviceIdType.MESH
HBM = pl.BlockSpec(memory_space=pl.ANY)
VMEM_FULL = pl.BlockSpec(memory_space=pltpu.VMEM)


def _params(*sem):
    return pltpu.CompilerParams(dimension_semantics=sem, vmem_limit_bytes=VMEM_LIMIT)


class _Rider:
    def __init__(self, ins, outs, aliases, n_sems, start, finish):
        self.ins, self.outs, self.aliases = list(ins), list(outs), dict(aliases)
        self.n_sems, self.start, self.finish = n_sems, start, finish


def _call(body, args, *, name, grid, in_specs, out_specs, out_shape, scratch_shapes=(), aliases=None, rider=None):
    many = isinstance(out_shape, (list, tuple))
    shapes = list(out_shape) if many else [out_shape]
    ospecs = list(out_specs) if many else [out_specs]
    aliases = dict(aliases or {})
    sem = ("arbitrary",) * len(grid)
    if rider is None:
        res = pl.pallas_call(
            body, name=name, grid=grid, in_specs=list(in_specs), out_specs=ospecs, out_shape=shapes,
            scratch_shapes=list(scratch_shapes), input_output_aliases=aliases, compiler_params=_params(*sem))(*args)
        return res if many else res[0]
    n_in, n_out, n_scr = len(args), len(shapes), len(scratch_shapes)
    nri, nro = len(rider.ins), len(rider.outs)

    def wrapped(*refs):
        ins, rin = refs[:n_in], refs[n_in:n_in + nri]
        o0 = n_in + nri
        outs, rout = refs[o0:o0 + n_out], refs[o0 + n_out:o0 + n_out + nro]
        s0 = o0 + n_out + nro
        scr, (send, recv) = refs[s0:s0 + n_scr], refs[s0 + n_scr:]
        ids = [pl.program_id(a) for a in range(len(grid))]
        first = functools.reduce(jnp.logical_and, [i == 0 for i in ids])
        last = functools.reduce(jnp.logical_and, [i == g - 1 for i, g in zip(ids, grid)])

        @pl.when(first)
        def _():
            rider.start(rin, rout, send, recv)

        body(*ins, *outs, *scr)

        @pl.when(last)
        def _():
            rider.finish(rin, rout, send, recv)

    for ri, ro in rider.aliases.items():
        aliases[n_in + ri] = n_out + ro
    res = pl.pallas_call(
        wrapped, name=name, grid=grid, in_specs=list(in_specs) + [HBM] * nri, out_specs=ospecs + [HBM] * nro,
        out_shape=shapes + rider.outs,
        scratch_shapes=list(scratch_shapes) + [pltpu.SemaphoreType.DMA((rider.n_sems,)),
                                               pltpu.SemaphoreType.DMA((rider.n_sems,))],
        input_output_aliases=aliases, compiler_params=_params(*sem))(*args, *rider.ins)
    main, extra = res[:n_out], list(res[n_out:])
    return (list(main) if many else main[0]), extra


def _div_tile(n, mult, cap):
    best = n
    for t in range(mult, min(n, cap) + 1, mult):
        if n % t == 0:
            best = t
    return best


def _sigmoid(x):
    return 1.0 / (1.0 + jnp.exp(-x))


def _ln_stats(x):
    mu = jnp.mean(x, axis=-1, keepdims=True)
    xc = x - mu
    var = jnp.mean(xc * xc, axis=-1, keepdims=True)
    rstd = lax.rsqrt(var + LN_EPS)
    return xc * rstd, rstd


def _ln_bwd(dy, xhat, rstd, g):
    dyg = dy * g
    m1 = jnp.mean(dyg, axis=-1, keepdims=True)
    m2 = jnp.mean(dyg * xhat, axis=-1, keepdims=True)
    return rstd * (dyg - m1 - xhat * m2)


def _dot_nt(a, b):
    return lax.dot_general(a, b, (((1,), (1,)), ((), ())), preferred_element_type=F32)


def _dot_tn(a, b):
    return lax.dot_general(a, b, (((0,), (0,)), ((), ())), preferred_element_type=F32)


def _dot(a, b):
    return jnp.dot(a, b, preferred_element_type=F32)


def _dot3(m, x):
    mb = m.astype(BF16)
    x1 = x.astype(BF16)
    r1 = x - x1.astype(F32)
    x2 = r1.astype(BF16)
    x3 = (r1 - x2.astype(F32)).astype(BF16)
    return _dot(mb, x1) + _dot(mb, x2) + _dot(mb, x3)


def _place_shard(x, name, chip_idx, dtype):
    r, c = x.shape
    tr = _div_tile(r, 16, 512)

    def body(s_ref, x_ref, o_ref):
        del s_ref
        o_ref[...] = x_ref[...].astype(dtype)

    return pl.pallas_call(
        body, name=name,
        grid_spec=pltpu.PrefetchScalarGridSpec(
            num_scalar_prefetch=1, grid=(r // tr,),
            in_specs=[pl.BlockSpec((tr, c), lambda i, s: (i, 0))],
            out_specs=pl.BlockSpec((None, tr, c), lambda i, s: (s[0], i, 0))),
        out_shape=jax.ShapeDtypeStruct((N_CHIPS, r, c), dtype),
        compiler_params=_params("parallel"),
    )(chip_idx, x)


def _ln0(x, g, b, tm):
    t, d = x.shape

    def body(x_ref, g_ref, b_ref, o_ref):
        xh, _ = _ln_stats(x_ref[...])
        o_ref[...] = (xh * g_ref[...] + b_ref[...]).astype(BF16)

    row = pl.BlockSpec((1, d), lambda i: (0, 0))
    return pl.pallas_call(
        body, name="ln0", grid=(t // tm,),
        in_specs=[pl.BlockSpec((tm, d), lambda i: (i, 0)), row, row],
        out_specs=pl.BlockSpec((tm, d), lambda i: (i, 0)),
        out_shape=jax.ShapeDtypeStruct((t, d), BF16),
        compiler_params=_params("parallel"),
    )(x, g, b)


def _proj(name, a, w3, n_sec, tm, tn, rider=None):
    m, k = a.shape
    s, _, ws = w3.shape
    sec_w = s * ws // n_sec
    nj = ws // tn
    per_sec = sec_w // tn

    def body(a_ref, w_ref, o_ref):
        o_ref[...] = _dot(a_ref[...], w_ref[...])

    return _call(
        body, (a, w3), name=name, grid=(s * nj, m // tm),
        in_specs=[pl.BlockSpec((tm, k), lambda j, i: (i, 0)),
                  pl.BlockSpec((None, k, tn), lambda j, i: (j // nj, 0, j % nj))],
        out_specs=pl.BlockSpec((None, tm, tn), lambda j, i: (j // per_sec, i, j % per_sec)),
        out_shape=jax.ShapeDtypeStruct((n_sec, m, sec_w), F32), rider=rider)


def _proj_t(name, a, w, tm, tn, rider=None):
    m, k = a.shape
    n = w.shape[0]

    def body(a_ref, w_ref, o_ref):
        o_ref[...] = _dot_nt(a_ref[...], w_ref[...])

    return _call(
        body, (a, w), name=name, grid=(n // tn, m // tm),
        in_specs=[pl.BlockSpec((tm, k), lambda j, i: (i, 0)),
                  pl.BlockSpec((tn, k), lambda j, i: (j, 0))],
        out_specs=pl.BlockSpec((tm, tn), lambda j, i: (i, j)),
        out_shape=jax.ShapeDtypeStruct((m, n), F32), rider=rider)


def _wgrad(name, a, b, out_shape, grid, a_spec, b_spec, o_spec, rider=None, dot=_dot_tn):
    nt = len(grid) - 1

    def body(a_ref, b_ref, o_ref):
        t = pl.program_id(nt)
        prod = dot(a_ref[...], b_ref[...])

        @pl.when(t == 0)
        def _():
            o_ref[...] = prod

        @pl.when(t > 0)
        def _():
            o_ref[...] += prod

    return _call(
        body, (a, b), name=name, grid=grid, in_specs=[a_spec, b_spec], out_specs=o_spec,
        out_shape=jax.ShapeDtypeStruct(out_shape, F32), rider=rider)


def _in_t(dp3, w_in3, tm, rider=None):
    _, t, sec_w = dp3.shape
    s, d, ws = w_in3.shape
    tk = sec_w // 2
    nq = ws // tk

    def body(*refs):
        a_refs, w_ref, o_ref = refs[:nq], refs[nq], refs[nq + 1]
        k = pl.program_id(1)
        prod = _dot_nt(a_refs[0][...], w_ref[:, 0:tk])
        for q in range(1, nq):
            prod = prod + _dot_nt(a_refs[q][...], w_ref[:, q * tk:(q + 1) * tk])

        @pl.when(k == 0)
        def _():
            o_ref[...] = prod

        @pl.when(k > 0)
        def _():
            o_ref[...] += prod

    def a_spec(q):
        return pl.BlockSpec((None, tm, tk), lambda i, k: ((nq * k + q) // 2, i, (nq * k + q) % 2))

    return _call(
        body, (dp3,) * nq + (w_in3,), name="in_t", grid=(t // tm, s),
        in_specs=[a_spec(q) for q in range(nq)] + [pl.BlockSpec((None, d, ws), lambda i, k: (k, 0, 0))],
        out_specs=pl.BlockSpec((tm, d), lambda i, k: (i, 0)),
        out_shape=jax.ShapeDtypeStruct((t, d), F32), rider=rider)


def _mix_ln1(cat, w_out, x, g0, b0, g1, b1, alpha, tm, rider=None):
    t, d = x.shape

    def body(cat_ref, w_ref, x_ref, g0_ref, b0_ref, g1_ref, b1_ref, xh_ref, h1b_ref, rstd_ref):
        mix = _dot(cat_ref[...], w_ref[...])
        xh0, _ = _ln_stats(x_ref[...])
        z1 = alpha * (xh0 * g0_ref[...] + b0_ref[...]) + mix
        xh1, rstd1 = _ln_stats(z1)
        xh_ref[...] = xh1
        h1b_ref[...] = (xh1 * g1_ref[...] + b1_ref[...]).astype(BF16)
        rstd_ref[...] = rstd1

    row = pl.BlockSpec((1, d), lambda i: (0, 0))
    blk = pl.BlockSpec((tm, d), lambda i: (i, 0))
    return _call(
        body, (cat, w_out, x, g0, b0, g1, b1), name="mix_ln1", grid=(t // tm,),
        in_specs=[blk, pl.BlockSpec((d, d), lambda i: (0, 0)), blk, row, row, row, row],
        out_specs=[blk, blk, pl.BlockSpec((tm, 1), lambda i: (i, 0))],
        out_shape=[jax.ShapeDtypeStruct((t, d), F32), jax.ShapeDtypeStruct((t, d), BF16),
                   jax.ShapeDtypeStruct((t, 1), F32)], rider=rider)


def _ln2_loss(ffn, xhat1, tgt, g1, b1, g2, b2, alpha, tm):
    t, d = xhat1.shape
    ni = t // tm
    inv_d = 1.0 / d

    def body(ffn_ref, xh1_ref, tgt_ref, g1_ref, b1_ref, g2_ref, b2_ref,
             dz2_ref, dz2b_ref, dg2_ref, db2_ref, loss_ref, lrow):
        i = pl.program_id(0)
        h1 = xh1_ref[...] * g1_ref[...] + b1_ref[...]
        xh2, rstd2 = _ln_stats(alpha * h1 + ffn_ref[...])
        g2v = g2_ref[...]
        diff = xh2 * g2v + b2_ref[...] - tgt_ref[...]
        dh2 = diff * inv_d
        sq = jnp.sum(diff * diff, axis=0, keepdims=True)
        dg = jnp.sum(dh2 * xh2, axis=0, keepdims=True)
        db = jnp.sum(dh2, axis=0, keepdims=True)

        @pl.when(i == 0)
        def _():
            lrow[...] = sq
            dg2_ref[...] = dg
            db2_ref[...] = db

        @pl.when(i > 0)
        def _():
            lrow[...] += sq
            dg2_ref[...] += dg
            db2_ref[...] += db

        dz2 = _ln_bwd(dh2, xh2, rstd2, g2v)
        dz2_ref[...] = dz2
        dz2b_ref[...] = dz2.astype(BF16)

        @pl.when(i == ni - 1)
        def _():
            tot = jnp.sum(lrow[...], axis=-1, keepdims=True) * (0.5 * inv_d)
            loss_ref[...] = jnp.broadcast_to(tot, (1, LANE))

    row = pl.BlockSpec((1, d), lambda i: (0, 0))
    blk = pl.BlockSpec((tm, d), lambda i: (i, 0))
    return _call(
        body, (ffn, xhat1, tgt, g1, b1, g2, b2), name="ln2_loss", grid=(ni,),
        in_specs=[blk, blk, blk, row, row, row, row],
        out_specs=[blk, blk, row, row, pl.BlockSpec((1, LANE), lambda i: (0, 0))],
        out_shape=[jax.ShapeDtypeStruct((t, d), F32), jax.ShapeDtypeStruct((t, d), BF16),
                   jax.ShapeDtypeStruct((1, d), F32), jax.ShapeDtypeStruct((1, d), F32),
                   jax.ShapeDtypeStruct((1, LANE), F32)],
        scratch_shapes=[pltpu.VMEM((1, d), F32)])


def _ln1_bwd(pre, dz2, xhat1, rstd1, g1, alpha, tm):
    t, d = dz2.shape

    def body(pre_ref, dz2_ref, xh_ref, rstd_ref, g_ref, dz1_ref, dz1b_ref, dg_ref, db_ref):
        i = pl.program_id(0)
        dh1 = alpha * dz2_ref[...] + pre_ref[...]
        xh = xh_ref[...]
        dg = jnp.sum(dh1 * xh, axis=0, keepdims=True)
        db = jnp.sum(dh1, axis=0, keepdims=True)

        @pl.when(i == 0)
        def _():
            dg_ref[...] = dg
            db_ref[...] = db

        @pl.when(i > 0)
        def _():
            dg_ref[...] += dg
            db_ref[...] += db

        dz1 = _ln_bwd(dh1, xh, rstd_ref[...], g_ref[...])
        dz1_ref[...] = dz1
        dz1b_ref[...] = dz1.astype(BF16)

    row = pl.BlockSpec((1, d), lambda i: (0, 0))
    blk = pl.BlockSpec((tm, d), lambda i: (i, 0))
    return _call(
        body, (pre, dz2, xhat1, rstd1, g1), name="ln1_bwd", grid=(t // tm,),
        in_specs=[blk, blk, blk, pl.BlockSpec((tm, 1), lambda i: (i, 0)), row],
        out_specs=[blk, blk, row, row],
        out_shape=[jax.ShapeDtypeStruct((t, d), F32), jax.ShapeDtypeStruct((t, d), BF16),
                   jax.ShapeDtypeStruct((1, d), F32), jax.ShapeDtypeStruct((1, d), F32)])


def _ln0_bwd(pre, dz1, x, g0, alpha, tm):
    t, d = x.shape

    def body(pre_ref, dz1_ref, x_ref, g_ref, dx_ref, dg_ref, db_ref):
        i = pl.program_id(0)
        dh0 = alpha * dz1_ref[...] + pre_ref[...]
        xh, rstd = _ln_stats(x_ref[...])
        dg = jnp.sum(dh0 * xh, axis=0, keepdims=True)
        db = jnp.sum(dh0, axis=0, keepdims=True)

        @pl.when(i == 0)
        def _():
            dg_ref[...] = dg
            db_ref[...] = db

        @pl.when(i > 0)
        def _():
            dg_ref[...] += dg
            db_ref[...] += db

        dx_ref[...] = _ln_bwd(dh0, xh, rstd, g_ref[...])

    row = pl.BlockSpec((1, d), lambda i: (0, 0))
    blk = pl.BlockSpec((tm, d), lambda i: (i, 0))
    return _call(
        body, (pre, dz1, x, g0), name="ln0_bwd", grid=(t // tm,),
        in_specs=[blk, blk, blk, row], out_specs=[blk, row, row],
        out_shape=[jax.ShapeDtypeStruct((t, d), F32), jax.ShapeDtypeStruct((1, d), F32),
                   jax.ShapeDtypeStruct((1, d), F32)])


def _shift_copies(ext, shifted):
    n = shifted.shape[1]
    for p in range(1, SUBLANE):
        shifted[p - 1] = ext[pl.ds(p, n), :]


def _window(ext, shifted, start, rows):
    p = start % SUBLANE
    if p == 0:
        return ext[pl.ds(start, rows), :]
    return shifted[p - 1, pl.ds(start - p, rows), :]


def _conv_fwd(p3, conv_w, conv_b, cn_g, cn_b, tc, cb, rider=None):
    _, t, w = p3.shape
    kk = conv_w.shape[0]
    off = HALO - (kk - 1)
    hb = tc // HALO

    def body(a_ref, g_ref, ap_ref, gp_ref, w_ref, b_ref, ng_ref, nb_ref, cat_ref, u1_ref, ext, sh):
        i = pl.program_id(1)
        ext[pl.ds(HALO, tc), :] = a_ref[...] * _sigmoid(g_ref[...])
        prev = ap_ref[...] * _sigmoid(gp_ref[...])
        ext[pl.ds(0, HALO), :] = jnp.where(i > 0, prev, 0.0)
        _shift_copies(ext, sh)
        for r in range(tc // ROWS):
            acc = jnp.broadcast_to(b_ref[...], (ROWS, cb))
            for k in range(kk):
                acc = acc + w_ref[k:k + 1, :] * _window(ext, sh, r * ROWS + off + k, ROWS)
            u1_ref[pl.ds(r * ROWS, ROWS), :] = acc
            for g in range(cb // LANE):
                sl = slice(g * LANE, (g + 1) * LANE)
                xh, _ = _ln_stats(acc[:, sl])
                u2 = xh * ng_ref[:, sl] + nb_ref[:, sl]
                cat_ref[pl.ds(r * ROWS, ROWS), sl] = (u2 * _sigmoid(u2)).astype(BF16)

    cur = lambda sec: pl.BlockSpec((None, tc, cb), lambda j, i: (sec, i, j))
    prev = lambda sec: pl.BlockSpec((None, HALO, cb), lambda j, i: (sec, jnp.maximum(i * hb - 1, 0), j))
    row = pl.BlockSpec((1, cb), lambda j, i: (0, j))
    return _call(
        body, (p3, p3, p3, p3, conv_w, conv_b, cn_g, cn_b), name="conv_fwd", grid=(w // cb, t // tc),
        in_specs=[cur(0), cur(1), prev(0), prev(1), pl.BlockSpec((kk, cb), lambda j, i: (0, j)), row, row, row],
        out_specs=[pl.BlockSpec((tc, cb), lambda j, i: (i, j)), pl.BlockSpec((tc, cb), lambda j, i: (i, j))],
        out_shape=[jax.ShapeDtypeStruct((t, 2 * w), BF16), jax.ShapeDtypeStruct((t, w), F32)],
        scratch_shapes=[pltpu.VMEM((tc + HALO, cb), F32),
                        pltpu.VMEM((SUBLANE - 1, tc + HALO - SUBLANE, cb), F32)], rider=rider)


def _conv_norm_bwd(dcat, u1, cn_g, cn_b, tc):
    t, w = u1.shape

    def body(du_ref, u1_ref, ng_ref, nb_ref, du1_ref, dg_ref, db_ref):
        i = pl.program_id(0)
        for g in range(w // LANE):
            sl = slice(g * LANE, (g + 1) * LANE)
            ng = ng_ref[:, sl]
            xh, rstd = _ln_stats(u1_ref[:, sl])
            u2 = xh * ng + nb_ref[:, sl]
            sg = _sigmoid(u2)
            du2 = du_ref[:, sl] * (sg * (1.0 + u2 * (1.0 - sg)))
            dg = jnp.sum(du2 * xh, axis=0, keepdims=True)
            db = jnp.sum(du2, axis=0, keepdims=True)

            @pl.when(i == 0)
            def _():
                dg_ref[:, sl] = dg
                db_ref[:, sl] = db

            @pl.when(i > 0)
            def _():
                dg_ref[:, sl] += dg
                db_ref[:, sl] += db

            du1_ref[:, sl] = _ln_bwd(du2, xh, rstd, ng)

    row = pl.BlockSpec((1, w), lambda i: (0, 0))
    blk = pl.BlockSpec((tc, w), lambda i: (i, 0))
    return pl.pallas_call(
        body, name="conv_norm_bwd", grid=(t // tc,),
        in_specs=[blk, blk, row, row], out_specs=[blk, row, row],
        out_shape=[jax.ShapeDtypeStruct((t, w), F32), jax.ShapeDtypeStruct((1, w), F32),
                   jax.ShapeDtypeStruct((1, w), F32)],
        compiler_params=_params("arbitrary"),
    )(dcat, u1, cn_g, cn_b)


def _conv_bwd(du1, p3, conv_w, tc, cb, rider=None):
    n_sec, t, w = p3.shape
    kk = conv_w.shape[0]
    off = HALO - (kk - 1)
    hb = tc // HALO
    nt = t // tc
    kpad = -(-kk // SUBLANE) * SUBLANE

    def body(d_ref, dn_ref, a_ref, g_ref, ap_ref, gp_ref, w_ref, dp_ref, dw_ref, db_ref,
             extd, extu, shd, shu, wacc, bacc):
        i = pl.program_id(1)

        @pl.when(i == 0)
        def _():
            wacc[...] = jnp.zeros_like(wacc)
            bacc[...] = jnp.zeros_like(bacc)

        extd[pl.ds(0, tc), :] = d_ref[...]
        extd[pl.ds(tc, HALO), :] = jnp.where(i < nt - 1, dn_ref[...], 0.0)
        extu[pl.ds(HALO, tc), :] = a_ref[...] * _sigmoid(g_ref[...])
        extu[pl.ds(0, HALO), :] = jnp.where(i > 0, ap_ref[...] * _sigmoid(gp_ref[...]), 0.0)
        _shift_copies(extd, shd)
        _shift_copies(extu, shu)
        for r in range(tc // ROWS):
            rows = pl.ds(r * ROWS, ROWS)
            acc = jnp.zeros((ROWS, cb), F32)
            for k in range(kk):
                acc = acc + w_ref[k:k + 1, :] * _window(extd, shd, r * ROWS + (kk - 1) - k, ROWS)
            a = a_ref[rows, :]
            sg = _sigmoid(g_ref[rows, :])
            dp_ref[0, rows, :] = (acc * sg).astype(BF16)
            dp_ref[1, rows, :] = (acc * a * sg * (1.0 - sg)).astype(BF16)
            d = d_ref[rows, :]
            bacc[...] += jnp.sum(d.reshape(ROWS // SUBLANE, SUBLANE, cb), axis=0)
            for k in range(kk):
                prod = d * _window(extu, shu, r * ROWS + off + k, ROWS)
                wacc[k] += jnp.sum(prod.reshape(ROWS // SUBLANE, SUBLANE, cb), axis=0)

        @pl.when(i == nt - 1)
        def _():
            for k in range(kk):
                dw_ref[k:k + 1, :] = jnp.sum(wacc[k], axis=0, keepdims=True)
            if kpad > kk:
                dw_ref[kk:kpad, :] = jnp.zeros((kpad - kk, cb), F32)
            db_ref[...] = jnp.sum(bacc[...], axis=0, keepdims=True)

    cur = lambda sec: pl.BlockSpec((None, tc, cb), lambda j, i: (sec, i, j))
    prev = lambda sec: pl.BlockSpec((None, HALO, cb), lambda j, i: (sec, jnp.maximum(i * hb - 1, 0), j))
    return _call(
        body, (du1, du1, p3, p3, p3, p3, conv_w), name="conv_bwd", grid=(w // cb, nt),
        in_specs=[pl.BlockSpec((tc, cb), lambda j, i: (i, j)),
                  pl.BlockSpec((HALO, cb), lambda j, i: (jnp.minimum((i + 1) * hb, t // HALO - 1), j)),
                  cur(0), cur(1), prev(0), prev(1), pl.BlockSpec((kk, cb), lambda j, i: (0, j))],
        out_specs=[pl.BlockSpec((2, tc, cb), lambda j, i: (0, i, j)),
                   pl.BlockSpec((kpad, cb), lambda j, i: (0, j)),
                   pl.BlockSpec((1, cb), lambda j, i: (0, j))],
        out_shape=[jax.ShapeDtypeStruct((n_sec, t, w), BF16), jax.ShapeDtypeStruct((kpad, w), F32),
                   jax.ShapeDtypeStruct((1, w), F32)],
        scratch_shapes=[pltpu.VMEM((tc + HALO, cb), F32), pltpu.VMEM((tc + HALO, cb), F32),
                        pltpu.VMEM((SUBLANE - 1, tc + HALO - SUBLANE, cb), F32),
                        pltpu.VMEM((SUBLANE - 1, tc + HALO - SUBLANE, cb), F32),
                        pltpu.VMEM((kk, SUBLANE, cb), F32), pltpu.VMEM((SUBLANE, cb), F32)], rider=rider)


def _ffn_act_fwd(hh3, fw, fb, tc, cb, rider=None):
    _, t, dff = hh3.shape
    kk = fw.shape[0]
    off = FHALO - (kk - 1)
    hb = tc // FHALO

    def body(g_ref, v_ref, gp_ref, w_ref, b_ref, act_ref, gc_ref, ext):
        i = pl.program_id(1)
        ext[pl.ds(FHALO, tc), :] = g_ref[...]
        ext[pl.ds(0, FHALO), :] = jnp.where(i > 0, gp_ref[...], 0.0)
        for r in range(tc // ROWS):
            rows = pl.ds(r * ROWS, ROWS)
            gc = jnp.broadcast_to(b_ref[...], (ROWS, cb))
            for k in range(kk):
                gc = gc + w_ref[k:k + 1, :] * ext[pl.ds(r * ROWS + off + k, ROWS), :]
            gc_ref[rows, :] = gc
            act_ref[rows, :] = (gc * _sigmoid(gc) * v_ref[rows, :]).astype(BF16)

    return _call(
        body, (hh3, hh3, hh3, fw, fb), name="ffn_act_fwd", grid=(dff // cb, t // tc),
        in_specs=[pl.BlockSpec((None, tc, cb), lambda j, i: (0, i, j)),
                  pl.BlockSpec((None, tc, cb), lambda j, i: (1, i, j)),
                  pl.BlockSpec((None, FHALO, cb), lambda j, i: (0, jnp.maximum(i * hb - 1, 0), j)),
                  pl.BlockSpec((kk, cb), lambda j, i: (0, j)),
                  pl.BlockSpec((1, cb), lambda j, i: (0, j))],
        out_specs=[pl.BlockSpec((tc, cb), lambda j, i: (i, j)), pl.BlockSpec((tc, cb), lambda j, i: (i, j))],
        out_shape=[jax.ShapeDtypeStruct((t, dff), BF16), jax.ShapeDtypeStruct((t, dff), F32)],
        scratch_shapes=[pltpu.VMEM((tc + FHALO, cb), F32)], rider=rider)


def _ffn_act_bwd(dact, hh3, gc, fw, tc, cb):
    _, t, dff = hh3.shape
    kk = fw.shape[0]
    off = FHALO - (kk - 1)
    hb = tc // FHALO
    nt = t // tc
    te = tc + FHALO

    def body(da_ref, dan_ref, g_ref, gp_ref, gc_ref, gcn_ref, v_ref, vn_ref, w_ref,
             dhh_ref, dw_ref, db_ref, gext, dext, wacc, bacc):
        i = pl.program_id(1)

        @pl.when(i == 0)
        def _():
            wacc[...] = jnp.zeros_like(wacc)
            bacc[...] = jnp.zeros_like(bacc)

        gext[pl.ds(0, FHALO), :] = jnp.where(i > 0, gp_ref[...], 0.0)
        gext[pl.ds(FHALO, tc), :] = g_ref[...]

        def gate_grad(gcv, da, v):
            sg = _sigmoid(gcv)
            return gcv * sg, da * v * (sg * (1.0 + gcv * (1.0 - sg)))

        for r in range(tc // ROWS):
            rows = pl.ds(r * ROWS, ROWS)
            da = da_ref[rows, :]
            silu, dgc = gate_grad(gc_ref[rows, :], da, v_ref[rows, :])
            dext[rows, :] = dgc
            dhh_ref[1, rows, :] = (da * silu).astype(BF16)
        _, dgc_next = gate_grad(gcn_ref[...], dan_ref[...], vn_ref[...])
        dext[pl.ds(tc, FHALO), :] = jnp.where(i < nt - 1, dgc_next, 0.0)
        for r in range(tc // ROWS):
            rows = pl.ds(r * ROWS, ROWS)
            dg = jnp.zeros((ROWS, cb), F32)
            for k in range(kk):
                dg = dg + w_ref[k:k + 1, :] * dext[pl.ds(r * ROWS + (kk - 1) - k, ROWS), :]
            dhh_ref[0, rows, :] = dg.astype(BF16)
            dgc = dext[rows, :]
            bacc[...] += jnp.sum(dgc.reshape(ROWS // SUBLANE, SUBLANE, cb), axis=0)
            for k in range(kk):
                prod = dgc * gext[pl.ds(r * ROWS + off + k, ROWS), :]
                wacc[k] += jnp.sum(prod.reshape(ROWS // SUBLANE, SUBLANE, cb), axis=0)

        @pl.when(i == nt - 1)
        def _():
            for k in range(kk):
                dw_ref[k:k + 1, :] = jnp.sum(wacc[k], axis=0, keepdims=True)
            dw_ref[kk:SUBLANE, :] = jnp.zeros((SUBLANE - kk, cb), F32)
            db_ref[...] = jnp.sum(bacc[...], axis=0, keepdims=True)

    nxt = lambda i: jnp.minimum((i + 1) * hb, t // FHALO - 1)
    return pl.pallas_call(
        body, name="ffn_act_bwd", grid=(dff // cb, nt),
        in_specs=[pl.BlockSpec((tc, cb), lambda j, i: (i, j)),
                  pl.BlockSpec((FHALO, cb), lambda j, i: (nxt(i), j)),
                  pl.BlockSpec((None, tc, cb), lambda j, i: (0, i, j)),
                  pl.BlockSpec((None, FHALO, cb), lambda j, i: (0, jnp.maximum(i * hb - 1, 0), j)),
                  pl.BlockSpec((tc, cb), lambda j, i: (i, j)),
                  pl.BlockSpec((FHALO, cb), lambda j, i: (nxt(i), j)),
                  pl.BlockSpec((None, tc, cb), lambda j, i: (1, i, j)),
                  pl.BlockSpec((None, FHALO, cb), lambda j, i: (1, nxt(i), j)),
                  pl.BlockSpec((kk, cb), lambda j, i: (0, j))],
        out_specs=[pl.BlockSpec((2, tc, cb), lambda j, i: (0, i, j)),
                   pl.BlockSpec((SUBLANE, cb), lambda j, i: (0, j)),
                   pl.BlockSpec((1, cb), lambda j, i: (0, j))],
        out_shape=[jax.ShapeDtypeStruct((2, t, dff), BF16), jax.ShapeDtypeStruct((SUBLANE, dff), F32),
                   jax.ShapeDtypeStruct((1, dff), F32)],
        scratch_shapes=[pltpu.VMEM((tc + FHALO, cb), F32), pltpu.VMEM((te, cb), F32),
                        pltpu.VMEM((kk, SUBLANE, cb), F32), pltpu.VMEM((SUBLANE, cb), F32)],
        compiler_params=_params("parallel", "arbitrary"),
    )(dact, dact, hh3, hh3, gc, gc, hh3, hh3, fw)


def _chunk_consts():
    r = lax.broadcasted_iota(jnp.int32, (CHUNK, CHUNK), 0)
    c = lax.broadcasted_iota(jnp.int32, (CHUNK, CHUNK), 1)
    blk = (r // SUB) * SUB
    tri = (c <= r).astype(F32)
    start = (c < blk).astype(F32)
    end = (c < blk + SUB).astype(F32)
    return jnp.concatenate([tri, start, end, jnp.ones((SUBLANE, CHUNK), F32)], axis=0)


def _roll8(x, d):
    return pltpu.roll(x.reshape(CHUNK // SUB, SUB, LANE), d % SUB, 1).reshape(CHUNK, LANE)


def _gate_terms(q, fpre, lb):
    sf = _sigmoid(fpre)
    fg = lb + (1.0 - lb) * sf
    sq = _sigmoid(q)
    return sf, fg, 1.0 - fg, sq, q * sq


def _decays(g, consts):
    cs = _dot3(consts, g)
    b = cs[0:CHUNK]
    rs = cs[CHUNK:2 * CHUNK]
    re = cs[2 * CHUNK:3 * CHUNK]
    tot = cs[3 * CHUNK:3 * CHUNK + 1]
    return b, rs, re, tot


def _lower_bound(lb_ref):
    l0, l1 = lb_ref[0:1, :], lb_ref[1:2, :]
    mx = jnp.maximum(l0, l1)
    e0, e1 = jnp.exp(l0 - mx), jnp.exp(l1 - mx)
    return e0 / (e0 + e1)


def _scaled_keys(kt, rs, re, rowblk, i):
    scale = jnp.where(rowblk < i, jnp.exp(jnp.minimum(rs[SUB * i:SUB * i + 1, :] - re, 0.0)), 0.0)
    return kt * scale, scale


def _hgrn_fwd(p3, lb_logits, hg, cat, tb, hpb, rider=None):
    _, t, w = p3.shape
    nh = w // LANE
    nc = tb // CHUNK
    assert nh % hpb == 0

    def body(q_ref, f_ref, v_ref, og_ref, lb_ref, hg_ref, cat_in, cat_ref, o_ref, st_ref, state):
        del cat_in
        consts = _chunk_consts()
        lb_all = _lower_bound(lb_ref)
        rowblk = lax.broadcasted_iota(jnp.int32, (CHUNK, 1), 0) // SUB
        rowpos = lax.broadcasted_iota(jnp.int32, (CHUNK, 1), 0) % SUB

        @pl.when(pl.program_id(1) == 0)
        def _():
            state[...] = jnp.zeros_like(state)

        def chunk(c, carry):
            rows = pl.ds(pl.multiple_of(c * CHUNK, CHUNK), CHUNK)
            heads = range(hpb)
            sls = [slice(j * LANE, (j + 1) * LANE) for j in heads]
            v = [v_ref[rows, s] for s in sls]
            vb = [x.astype(BF16) for x in v]
            gates = [_gate_terms(q_ref[rows, s], f_ref[rows, s], lb_all[:, s]) for s in sls]
            fg = [g[1] for g in gates]
            kk = [g[2] for g in gates]
            qh = [g[4] for g in gates]
            dec = [_decays(jnp.log(x), consts) for x in fg]
            b = [x[0] for x in dec]
            rs = [x[1] for x in dec]
            re = [x[2] for x in dec]
            tot = [x[3] for x in dec]
            qt = [qh[j] * jnp.exp(b[j] - rs[j]) for j in heads]
            kt = [kk[j] * jnp.exp(re[j] - b[j]) for j in heads]
            st = [state[j] for j in heads]
            for j in heads:
                st_ref[j, c] = st[j]
            a = [jnp.zeros((CHUNK, CHUNK), F32) for _ in heads]
            for i in range(1, CHUNK // SUB):
                for j in heads:
                    ki, _ = _scaled_keys(kt[j], rs[j], re[j], rowblk, i)
                    a[j] = a[j] + _dot_nt(jnp.where(rowblk == i, qt[j], 0.0).astype(BF16), ki.astype(BF16))
            o = [_dot(a[j].astype(BF16), vb[j]) for j in heads]
            o = [o[j] + _dot_nt((qh[j] * jnp.exp(b[j])).astype(BF16), st[j].astype(BF16)) for j in heads]
            for j in heads:
                k_up = kk[j] * jnp.exp(tot[j] - b[j])
                state[j] = st[j] * jnp.exp(tot[j]) + _dot_tn(vb[j], k_up.astype(BF16))
            for j in heads:
                e, rf = None, fg[j]
                for d in range(SUB):
                    if d == 0:
                        vs, term = v[j], qh[j] * kk[j]
                    else:
                        e = rf if e is None else e * rf
                        rf = _roll8(fg[j], d)
                        vs = _roll8(v[j], d)
                        term = jnp.where(rowpos >= d, qh[j] * (1.0 - rf) * e, 0.0)
                    o[j] = o[j] + jnp.sum(term, axis=-1, keepdims=True) * vs
            for j in heads:
                og = og_ref[rows, sls[j]]
                o_ref[rows, sls[j]] = o[j]
                r = lax.rsqrt(jnp.mean(o[j] * o[j], axis=-1, keepdims=True) + RMS_EPS)
                cat_ref[rows, sls[j]] = (o[j] * r * hg_ref[:, sls[j]] * (og * _sigmoid(og))).astype(BF16)
            return carry

        lax.fori_loop(0, nc, chunk, 0)

    bw = hpb * LANE
    sec = lambda s: pl.BlockSpec((None, tb, bw), lambda h, i: (s, i, h))
    return _call(
        body, (p3, p3, p3, p3, lb_logits, hg, cat), name="hgrn_fwd", grid=(nh // hpb, t // tb),
        in_specs=[sec(2), sec(3), sec(4), sec(5),
                  pl.BlockSpec((2, bw), lambda h, i: (0, h)),
                  pl.BlockSpec((1, bw), lambda h, i: (0, h)), HBM],
        out_specs=[pl.BlockSpec((tb, bw), lambda h, i: (i, nh // hpb + h)),
                   pl.BlockSpec((tb, bw), lambda h, i: (i, h)),
                   pl.BlockSpec((hpb, nc, LANE, LANE), lambda h, i: (h, i, 0, 0))],
        out_shape=[jax.ShapeDtypeStruct(cat.shape, BF16), jax.ShapeDtypeStruct((t, w), F32),
                   jax.ShapeDtypeStruct((nh, t // CHUNK, LANE, LANE), F32)],
        scratch_shapes=[pltpu.VMEM((hpb, LANE, LANE), F32)], aliases={6: 0}, rider=rider)


def _hgrn_bwd(p3, lb_logits, hg, o_pre, states, dcat, dp3, tb, hpb, rider=None):
    n_sec, t, w = p3.shape
    nh = w // LANE
    assert nh % hpb == 0
    nc = tb // CHUNK
    nb = t // tb
    bw = hpb * LANE
    n_steps = (nh // hpb) * nb

    def body(q_ref, f_ref, v_ref, og_ref, lb_ref, hg_ref, o_ref, st_ref, dc_ref, dp_in,
             dp_ref, dlb_ref, dhg_ref, dstate, stash, lbacc, hgacc, osem):
        del dp_in
        h, i = pl.program_id(0), pl.program_id(1)
        step = h * nb + i
        slot = step % 2

        def out_copy(s, row_blk, lane_blk):
            dst = dp_ref.at[pl.ds(2, 4), pl.ds(row_blk * tb, tb), pl.ds(lane_blk * bw, bw)]
            return pltpu.make_async_copy(stash.at[s], dst, osem.at[s])

        @pl.when(step >= 2)
        def _():
            out_copy(slot, 0, 0).wait()

        def compute():
            consts = _chunk_consts()
            rr = lax.broadcasted_iota(jnp.int32, (CHUNK, CHUNK), 0)
            cc = lax.broadcasted_iota(jnp.int32, (CHUNK, CHUNK), 1)
            upper = (cc >= rr).astype(F32)
            lb_all = _lower_bound(lb_ref)
            rowblk = lax.broadcasted_iota(jnp.int32, (CHUNK, 1), 0) // SUB
            rowpos = lax.broadcasted_iota(jnp.int32, (CHUNK, 1), 0) % SUB

            @pl.when(i == 0)
            def _():
                dstate[...] = jnp.zeros_like(dstate)
                lbacc[...] = jnp.zeros_like(lbacc)
                hgacc[...] = jnp.zeros_like(hgacc)

            def head(j, c, rows):
                sl = slice(j * LANE, (j + 1) * LANE)
                lb = lb_all[:, sl]
                hgv = hg_ref[:, sl]
                q = q_ref[rows, sl]
                v = v_ref[rows, sl]
                og = og_ref[rows, sl]
                o = o_ref[rows, sl]
                dcg = dc_ref[rows, sl]
                sf, fg, kk, sq, qh = _gate_terms(q, f_ref[rows, sl], lb)
                b, rs, re, tot = _decays(jnp.log(fg), consts)
                eq = jnp.exp(b - rs)
                ek = jnp.exp(re - b)
                qt = qh * eq
                kt = kk * ek
                e_in = jnp.exp(b)
                e_up = jnp.exp(tot - b)
                e_tot = jnp.exp(tot)
                q_in = (qh * e_in).astype(BF16)
                k_up = (kk * e_up).astype(BF16)
                vb = v.astype(BF16)
                st = st_ref[j, c]
                dst = dstate[j]
                dstb = dst.astype(BF16)
                yield

                sg = _sigmoid(og)
                r = lax.rsqrt(jnp.mean(o * o, axis=-1, keepdims=True) + RMS_EPS)
                ohat = o * r
                d_og = dcg * ohat * hgv * (sg * (1.0 + og * (1.0 - sg)))
                d_on = dcg * (og * sg)
                hgacc[:, sl] += jnp.sum((d_on * ohat).reshape(CHUNK // SUBLANE, SUBLANE, LANE), axis=0)
                d_oh = d_on * hgv
                do = r * (d_oh - ohat * jnp.mean(d_oh * ohat, axis=-1, keepdims=True))
                dob = do.astype(BF16)

                da = _dot_nt(dob, vb)
                yield
                a_off = jnp.zeros((CHUNK, CHUNK), F32)
                dqt = jnp.zeros((CHUNK, LANE), F32)
                dkt = jnp.zeros((CHUNK, LANE), F32)
                for blk in range(1, CHUNK // SUB):
                    ki, scale = _scaled_keys(kt, rs, re, rowblk, blk)
                    kib = ki.astype(BF16)
                    qib = jnp.where(rowblk == blk, qt, 0.0).astype(BF16)
                    dab = jnp.where(rowblk == blk, da, 0.0).astype(BF16)
                    a_off = a_off + _dot_nt(qib, kib)
                    dqt = dqt + _dot(dab, kib)
                    dkt = dkt + _dot_tn(dab, qib) * scale
                    yield
                dqh = dqt * eq
                dk = dkt * ek
                dv = _dot_tn(a_off.astype(BF16), dob)

                dqh = dqh + _dot(dob, st.astype(BF16)) * e_in
                dk = dk + _dot(vb, dstb) * e_up
                dv = dv + _dot_nt(k_up, dstb)
                st_end = st * e_tot + _dot_tn(vb, k_up)
                carry_g = jnp.sum(st_end * dst, axis=0, keepdims=True)
                dstate[j] = dst * e_tot + _dot_tn(dob, q_in)
                yield

                e, rf = None, fg
                for d in range(SUB):
                    if d == 0:
                        a_d = jnp.sum(qh * kk, axis=-1, keepdims=True)
                        da_d = jnp.sum(do * v, axis=-1, keepdims=True)
                        dqh = dqh + da_d * kk
                        dk = dk + da_d * qh
                        dv = dv + a_d * do
                        continue
                    e = rf if e is None else e * rf
                    rf = _roll8(fg, d)
                    em = jnp.where(rowpos >= d, e, 0.0)
                    ks, vs = 1.0 - rf, _roll8(v, d)
                    a_d = jnp.sum(qh * ks * em, axis=-1, keepdims=True)
                    da_d = jnp.sum(do * vs, axis=-1, keepdims=True) * em
                    dqh = dqh + da_d * ks
                    dk = dk + _roll8(da_d * qh, -d)
                    dv = dv + _roll8(a_d * do, -d)
                yield

                dg = _dot3(upper, qh * dqh - kk * dk) + carry_g
                dfg = dg / fg - dk
                lbacc[:, sl] += jnp.sum((dfg * (1.0 - sf)).reshape(CHUNK // SUBLANE, SUBLANE, LANE), axis=0)
                stash[slot, 0, rows, sl] = (dqh * (sq * (1.0 + q * (1.0 - sq)))).astype(BF16)
                stash[slot, 1, rows, sl] = (dfg * (1.0 - lb) * sf * (1.0 - sf)).astype(BF16)
                stash[slot, 2, rows, sl] = dv.astype(BF16)
                stash[slot, 3, rows, sl] = d_og.astype(BF16)

            def chunk(cr, carry):
                c = nc - 1 - cr
                rows = pl.ds(pl.multiple_of(c * CHUNK, CHUNK), CHUNK)
                running = [head(j, c, rows) for j in range(hpb)]
                while running:
                    running = [g for g in running if next(g, StopIteration) is not StopIteration]
                return carry

            lax.fori_loop(0, nc, chunk, 0)

            @pl.when(i == nb - 1)
            def _():
                dlb_ref[...] = jnp.sum(lbacc[...], axis=0, keepdims=True)
                dhg_ref[...] = jnp.sum(hgacc[...], axis=0, keepdims=True)

        compute()
        out_copy(slot, nb - 1 - i, h).start()

        @pl.when(step == n_steps - 1)
        def _():
            out_copy(slot, 0, 0).wait()
            if n_steps >= 2:
                out_copy(1 - slot, 0, 0).wait()

    rev = lambda i: nb - 1 - i
    sec = lambda s: pl.BlockSpec((None, tb, bw), lambda h, i: (s, rev(i), h))
    return _call(
        body, (p3, p3, p3, p3, lb_logits, hg, o_pre, states, dcat, dp3), name="hgrn_bwd", grid=(nh // hpb, nb),
        in_specs=[sec(2), sec(3), sec(4), sec(5),
                  pl.BlockSpec((2, bw), lambda h, i: (0, h)),
                  pl.BlockSpec((1, bw), lambda h, i: (0, h)),
                  pl.BlockSpec((tb, bw), lambda h, i: (rev(i), h)),
                  pl.BlockSpec((hpb, nc, LANE, LANE), lambda h, i: (h, rev(i), 0, 0)),
                  pl.BlockSpec((tb, bw), lambda h, i: (rev(i), nh // hpb + h)), HBM],
        out_specs=[HBM,
                   pl.BlockSpec((1, bw), lambda h, i: (0, h)),
                   pl.BlockSpec((1, bw), lambda h, i: (0, h))],
        out_shape=[jax.ShapeDtypeStruct((n_sec, t, w), BF16), jax.ShapeDtypeStruct((1, w), F32),
                   jax.ShapeDtypeStruct((1, w), F32)],
        scratch_shapes=[pltpu.VMEM((hpb, LANE, LANE), F32), pltpu.VMEM((2, 4, tb, bw), BF16),
                        pltpu.VMEM((SUBLANE, bw), F32), pltpu.VMEM((SUBLANE, bw), F32),
                        pltpu.SemaphoreType.DMA((2,))],
        aliases={9: 0}, rider=rider)


def _place():
    x, y, c = lax.axis_index("x"), lax.axis_index("y"), lax.axis_index("c")
    chips = [(1 - x, y), (x, 1 - y), (1 - x, 1 - y)]
    return x, y, c, chips


def _rows(buf, px, py, pc, part=None):
    half = buf.shape[1] // 2
    if part is None:
        return buf.at[2 * px + py, pl.ds(pc * half, half)]
    lo, hi, n = part
    piece = half // n
    return buf.at[2 * px + py, pl.ds(pc * half + lo * piece, (hi - lo) * piece)]


def _rcopy(src, dst, send, recv, idx, to):
    return pltpu.make_async_remote_copy(src_ref=src, dst_ref=dst, send_sem=send.at[idx], recv_sem=recv.at[idx],
                                        device_id=to, device_id_type=MESH)


def _same(bufs):
    return [jax.ShapeDtypeStruct(b.shape, b.dtype) for b in bufs]


def _ride_gather_ici(bufs, parts=None):
    n = len(bufs)
    parts = parts or [None] * n

    def start(rin, rout, send, recv):
        x, y, c, chips = _place()
        for k in range(n):
            mine = _rows(rout[k], x, y, c, parts[k])
            for j, chip in enumerate(chips):
                _rcopy(mine, mine, send, recv, 3 * k + j, (*chip, c)).start()

    def finish(rin, rout, send, recv):
        x, y, c, chips = _place()
        for k in range(n):
            for j, chip in enumerate(chips):
                theirs = _rows(rout[k], *chip, c, parts[k])
                _rcopy(theirs, theirs, send, recv, 3 * k + j, (x, y, c)).wait_recv()
        for k in range(n):
            mine = _rows(rout[k], x, y, c, parts[k])
            for j in range(3):
                _rcopy(mine, mine, send, recv, 3 * k + j, (x, y, c)).wait_send()

    return _Rider(bufs, _same(bufs), {k: k for k in range(n)}, 3 * n, start, finish)


class _SemView:
    def __init__(self, ref, base):
        self.ref, self.base = ref, base

    @property
    def at(self):
        return self

    def __getitem__(self, idx):
        return self.ref.at[idx + self.base]


def _ride_both(a, b):
    nai, nao = len(a.ins), len(a.outs)

    def start(rin, rout, send, recv):
        a.start(rin[:nai], rout[:nao], send, recv)
        b.start(rin[nai:], rout[nao:], _SemView(send, a.n_sems), _SemView(recv, a.n_sems))

    def finish(rin, rout, send, recv):
        a.finish(rin[:nai], rout[:nao], send, recv)
        b.finish(rin[nai:], rout[nao:], _SemView(send, a.n_sems), _SemView(recv, a.n_sems))

    aliases = dict(a.aliases)
    aliases.update({nai + ri: nao + ro for ri, ro in b.aliases.items()})
    return _Rider(a.ins + b.ins, a.outs + b.outs, aliases, a.n_sems + b.n_sems, start, finish)


def _ride_gather_d2d(bufs):
    n = len(bufs)

    def start(rin, rout, send, recv):
        x, y, c, chips = _place()
        for k in range(n):
            for j, chip in enumerate(chips):
                got = _rows(rout[k], *chip, c)
                _rcopy(got, got, send, recv, 3 * k + j, (x, y, 1 - c)).start()

    def finish(rin, rout, send, recv):
        x, y, c, chips = _place()
        for k in range(n):
            for j, chip in enumerate(chips):
                theirs = _rows(rout[k], *chip, 1 - c)
                _rcopy(theirs, theirs, send, recv, 3 * k + j, (x, y, c)).wait_recv()
        for k in range(n):
            for j, chip in enumerate(chips):
                got = _rows(rout[k], *chip, c)
                _rcopy(got, got, send, recv, 3 * k + j, (x, y, c)).wait_send()

    return _Rider(bufs, _same(bufs), {k: k for k in range(n)}, 3 * n, start, finish)


def _ride_swap(grads):
    n = len(grads)

    def copy(k, rin, rout, send, recv):
        x, y, c, _ = _place()
        half = rin[k].shape[1] // 2
        return _rcopy(rin[k].at[:, pl.ds((1 - c) * half, half)], rout[k], send, recv, k, (x, y, 1 - c))

    def start(rin, rout, send, recv):
        for k in range(n):
            copy(k, rin, rout, send, recv).start()

    def finish(rin, rout, send, recv):
        for k in range(n):
            copy(k, rin, rout, send, recv).wait()

    outs = [jax.ShapeDtypeStruct((g.shape[0], g.shape[1] // 2, g.shape[2]), g.dtype) for g in grads]
    return _Rider(grads, outs, {}, n, start, finish)


def _ride_send_partials(parts, pieces=None, into=None):
    n = len(parts)
    pieces = pieces or [None] * n

    def cut(ref, k):
        if pieces[k] is None:
            return ref
        lo, hi, m = pieces[k]
        q = ref.shape[0] // m
        return ref.at[pl.ds(lo * q, (hi - lo) * q)]

    def copies(rin, rout, send, recv):
        x, y, c, chips = _place()
        return [_rcopy(cut(rin[k].at[2 * px + py], k), cut(rout[k].at[j], k), send, recv, 3 * k + j, (px, py, c))
                for k in range(n) for j, (px, py) in enumerate(chips)]

    def start(rin, rout, send, recv):
        for cp in copies(rin, rout, send, recv):
            cp.start()

    def finish(rin, rout, send, recv):
        for cp in copies(rin, rout, send, recv):
            cp.wait()

    if into is None:
        outs = [jax.ShapeDtypeStruct((3,) + p.shape[1:], p.dtype) for p in parts]
        return _Rider(parts, outs, {}, 3 * n, start, finish)
    return _Rider(list(parts) + list(into), _same(into), {n + k: k for k in range(n)}, 3 * n, start, finish)


def _ride_join(bufs):
    n = len(bufs)

    def half_of(buf, pc):
        half = buf.shape[0] // 2
        return buf.at[pl.ds(pc * half, half)]

    def start(rin, rout, send, recv):
        x, y, c, _ = _place()
        for k in range(n):
            mine = half_of(rout[k], c)
            _rcopy(mine, mine, send, recv, k, (x, y, 1 - c)).start()

    def finish(rin, rout, send, recv):
        x, y, c, _ = _place()
        for k in range(n):
            mine, theirs = half_of(rout[k], c), half_of(rout[k], 1 - c)
            _rcopy(mine, mine, send, recv, k, (x, y, c)).wait_send()
            _rcopy(theirs, theirs, send, recv, k, (x, y, c)).wait_recv()

    return _Rider(bufs, _same(bufs), {k: k for k in range(n)}, n, start, finish)


def _run(name, rider):
    def body(*refs):
        nri, nro = len(rider.ins), len(rider.outs)
        rin, rout = refs[:nri], refs[nri:nri + nro]
        send, recv = refs[nri + nro:]
        rider.start(rin, rout, send, recv)
        rider.finish(rin, rout, send, recv)

    return pl.pallas_call(
        body, name=name, in_specs=[HBM] * len(rider.ins), out_specs=[HBM] * len(rider.outs), out_shape=rider.outs,
        scratch_shapes=[pltpu.SemaphoreType.DMA((rider.n_sems,)), pltpu.SemaphoreType.DMA((rider.n_sems,))],
        input_output_aliases=rider.aliases,
    )(*rider.ins)


def _add_halves(name, g, other, c_idx):
    s, r, cols = g.shape
    half = r // 2
    tr = _div_tile(half, 16, 512)
    nb = half // tr

    def body(c_ref, g_ref, o_ref, q_ref):
        del c_ref
        q_ref[...] = (g_ref[...] + o_ref[...]).astype(BF16)

    return pl.pallas_call(
        body, name=name,
        grid_spec=pltpu.PrefetchScalarGridSpec(
            num_scalar_prefetch=1, grid=(s, nb),
            in_specs=[pl.BlockSpec((None, tr, cols), lambda k, i, c: (k, c[0] * nb + i, 0)),
                      pl.BlockSpec((None, tr, cols), lambda k, i, c: (k, i, 0))],
            out_specs=pl.BlockSpec((None, tr, cols), lambda k, i, c: (k, i, 0))),
        out_shape=jax.ShapeDtypeStruct((s, half, cols), BF16),
        compiler_params=_params("parallel", "parallel"),
    )(c_idx, g, other)


def _sum_partials(name, part, arrived, place_idx):
    _, half, cols = part.shape
    tr = _div_tile(half, 16, 512)
    nb = half // tr

    def body(s_ref, p_ref, a_ref, o_ref):
        del s_ref
        o_ref[...] = ((p_ref[...].astype(F32) + a_ref[0].astype(F32)) + a_ref[1].astype(F32)) + a_ref[2].astype(F32)

    return pl.pallas_call(
        body, name=name,
        grid_spec=pltpu.PrefetchScalarGridSpec(
            num_scalar_prefetch=1, grid=(nb,),
            in_specs=[pl.BlockSpec((None, tr, cols), lambda i, s: (s[0], i, 0)),
                      pl.BlockSpec((3, tr, cols), lambda i, s: (0, i, 0))],
            out_specs=pl.BlockSpec((tr, cols), lambda i, s: (s[1] * nb + i, 0))),
        out_shape=jax.ShapeDtypeStruct((2 * half, cols), F32),
        compiler_params=_params("parallel"),
    )(place_idx, part, arrived)


def _small_allreduce(wide_rows, ffn_rows, w, dff, n_wide, n_ffn):
    n_in = len(wide_rows) + len(ffn_rows)

    def body(*refs):
        ins = refs[:n_in]
        s1_ref, s2_ref, r1, r2, p1, p2, send, recv = refs[n_in:]
        x, y, c, _ = _place()
        me = 4 * x + 2 * y + c
        p1[...] = jnp.zeros_like(p1)
        p2[...] = jnp.zeros_like(p2)
        row = 0
        for ref, (_, r, m) in zip(ins, wide_rows):
            if m == 1 and r % SUBLANE == 0 and row % SUBLANE == 0:
                p1[row:row + r, :] = ref[...]
                row += r
                continue
            for rr in range(r):
                for mm in range(m):
                    p1[row:row + 1, :] = ref[rr:rr + 1, mm * w:(mm + 1) * w]
                    row += 1
        row = 0
        for ref, arr in zip(ins[len(wide_rows):], ffn_rows):
            r = arr.shape[0]
            p2[row:row + r, :] = ref[...]
            row += r
        r1[me] = p1[...]
        r2[me] = p2[...]
        cps = []
        for mask in range(1, 8):
            peer = (x ^ (mask >> 2), y ^ ((mask >> 1) & 1), c ^ (mask & 1))
            for a, (src, dst) in enumerate(((p1, r1), (p2, r2))):
                cp = pltpu.make_async_remote_copy(
                    src_ref=src, dst_ref=dst.at[me], send_sem=send.at[a, mask - 1], recv_sem=recv.at[a, mask - 1],
                    device_id=peer, device_id_type=MESH)
                cp.start()
                cps.append(cp)
        for cp in cps:
            cp.wait()
        t1, t2 = r1[0], r2[0]
        for d in range(1, 8):
            t1 = t1 + r1[d]
            t2 = t2 + r2[d]
        s1_ref[...] = t1
        s2_ref[...] = t2

    ins = [a for a, _, _ in wide_rows] + list(ffn_rows)
    return pl.pallas_call(
        body, name="small_allreduce", in_specs=[VMEM_FULL] * n_in, out_specs=[VMEM_FULL, VMEM_FULL],
        out_shape=[jax.ShapeDtypeStruct((n_wide, w), F32), jax.ShapeDtypeStruct((n_ffn, dff), F32)],
        scratch_shapes=[pltpu.VMEM((8, n_wide, w), F32), pltpu.VMEM((8, n_ffn, dff), F32),
                        pltpu.VMEM((n_wide, w), F32), pltpu.VMEM((n_ffn, dff), F32),
                        pltpu.SemaphoreType.DMA((2, 7)), pltpu.SemaphoreType.DMA((2, 7))],
        compiler_params=pltpu.CompilerParams(vmem_limit_bytes=VMEM_LIMIT),
    )(*ins)


def _adamw(w, g, m, v):
    m2 = ADAM_B1 * m + (1.0 - ADAM_B1) * g
    v2 = ADAM_B2 * v + (1.0 - ADAM_B2) * (g * g)
    m_hat = m2 / (1.0 - ADAM_B1 ** ADAM_STEP)
    v_hat = v2 / (1.0 - ADAM_B2 ** ADAM_STEP)
    delta = -ADAM_LR * (m_hat / (jnp.sqrt(v_hat) + ADAM_EPS) + ADAM_WD * w)
    return delta, m2, v2


def _adam_big(name, w, g, m, v):
    r, c = w.shape
    tr = 128 if r % 128 == 0 else r

    def body(w_ref, g_ref, m_ref, v_ref, go_ref, d_ref, m2_ref, v2_ref):
        g = g_ref[...]
        go_ref[...] = g
        d_ref[...], m2_ref[...], v2_ref[...] = _adamw(w_ref[...], g, m_ref[...], v_ref[...])

    blk = pl.BlockSpec((tr, c), lambda i: (i, 0))
    return _call(
        body, (w, g, m, v), name=name, grid=(r // tr,), in_specs=[blk] * 4, out_specs=[blk] * 4,
        out_shape=[jax.ShapeDtypeStruct((r, c), F32)] * 4)


def _adam_small(s1, s2, cw_g, fw_g, lb_logits, triples, layout, w):
    n = len(triples)

    def body(*refs):
        s1_ref, s2_ref, cw_ref, fw_ref, lbl_ref = refs[:5]
        prm = refs[5:5 + 3 * n]
        outs = refs[5 + 3 * n:]
        for p, lay in enumerate(layout):
            w_ref, m_ref, v_ref = prm[3 * p:3 * p + 3]
            g_ref, d_ref, m2_ref, v2_ref = outs[4 * p:4 * p + 4]
            if lay[0] == "wide":
                _, row, r, pieces = lay
                for rr in range(r):
                    for mm in range(pieces):
                        g_ref[rr:rr + 1, mm * w:(mm + 1) * w] = s1_ref[row:row + 1, :]
                        row += 1
            elif lay[0] == "ffn":
                _, row, r = lay
                g_ref[...] = s2_ref[row:row + r, :]
            elif lay[0] == "cw":
                g_ref[...] = cw_ref[0:g_ref.shape[0], :]
            elif lay[0] == "fw":
                g_ref[...] = fw_ref[0:g_ref.shape[0], :]
            else:
                s0 = _lower_bound(lbl_ref)
                d0 = s1_ref[lay[1]:lay[1] + 1, :] * s0 * (1.0 - s0)
                g_ref[0:1, :] = d0
                g_ref[1:2, :] = -d0
            d_ref[...], m2_ref[...], v2_ref[...] = _adamw(w_ref[...], g_ref[...], m_ref[...], v_ref[...])

    flat = [a for tr in triples for a in tr]
    shapes = []
    for tr in triples:
        shapes.extend([jax.ShapeDtypeStruct(tr[0].shape, F32)] * 4)
    return pl.pallas_call(
        body, name="adam_small", in_specs=[VMEM_FULL] * (5 + 3 * n), out_specs=[VMEM_FULL] * (4 * n),
        out_shape=shapes, compiler_params=pltpu.CompilerParams(vmem_limit_bytes=VMEM_LIMIT),
    )(s1, s2, cw_g, fw_g, lb_logits, *flat)


def _row_tile(t):
    return 512 if t % 512 == 0 and t >= 2048 else 128


def kernel(x, emb_ln_g, emb_ln_b, w_in, conv_w, conv_b, conv_norm_g, conv_norm_b, lb_logits, hgrn_norm_g, w_out, ln1_g, ln1_b, w_ffn_up, ffn_conv_w, ffn_conv_b, w_ffn_down, ln2_g, ln2_b, loss_target, m_emb_ln_g, m_emb_ln_b, m_w_in, m_conv_w, m_conv_b, m_conv_norm_g, m_conv_norm_b, m_lb_logits, m_hgrn_norm_g, m_w_out, m_ln1_g, m_ln1_b, m_w_ffn_up, m_ffn_conv_w, m_ffn_conv_b, m_w_ffn_down, m_ln2_g, m_ln2_b, v_emb_ln_g, v_emb_ln_b, v_w_in, v_conv_w, v_conv_b, v_conv_norm_g, v_conv_norm_b, v_lb_logits, v_hgrn_norm_g, v_w_out, v_ln1_g, v_ln1_b, v_w_ffn_up, v_ffn_conv_w, v_ffn_conv_b, v_w_ffn_down, v_ln2_g, v_ln2_b):
    depth = w_in.shape[0]
    assert depth == 1 and x.shape[0] == 1
    alpha = (2.0 * depth) ** 0.25
    t, d = x.shape[1], x.shape[2]
    w = d // 2
    dff = ffn_conv_b.shape[1]
    kc = conv_w.shape[1]
    assert w % (2 * LANE) == 0 and dff % (4 * LANE) == 0 and t % 128 == 0
    tm = _row_tile(t)
    tm2 = tm // 2
    tmm = 1024 if t % 1024 == 0 and t >= 2048 else tm
    cb = 2 * LANE
    cbf = 4 * LANE
    tb = tm
    nh = w // LANE
    hpb = 4 if nh % 4 == 0 else 2

    xi = lax.axis_index("x")
    yi = lax.axis_index("y")
    ci = lax.axis_index("c")
    chip = 2 * xi + yi
    c_idx = jnp.reshape(ci, (1,)).astype(jnp.int32)
    chip_idx = jnp.reshape(chip, (1,)).astype(jnp.int32)
    place_idx = jnp.stack([chip, ci]).astype(jnp.int32)

    x2 = x[0]
    tgt = loss_target[0]
    g0, b0 = emb_ln_g.reshape(1, d), emb_ln_b.reshape(1, d)
    w_in2, w_out2, w_up2, w_dn2 = w_in[0], w_out[0], w_ffn_up[0], w_ffn_down[0]
    cw2, fw2 = conv_w[0], ffn_conv_w[0]

    b_in = _place_shard(w_in2, "place_w_in", chip_idx, BF16)
    b_out = _place_shard(w_out2, "place_w_out", chip_idx, BF16)
    b_up = _place_shard(w_up2, "place_w_up", chip_idx, BF16)
    b_dn = _place_shard(w_dn2, "place_w_down", chip_idx, BF16)
    b_cw = _place_shard(_pad_rows(cw2), "place_conv_w", chip_idx, F32)
    b_fw = _place_shard(_pad_rows(fw2), "place_ffn_conv_w", chip_idx, F32)
    first = _run("gather_first_ici", _ride_gather_ici([b_in, b_cw, b_fw]))
    w_in3, cw_full3, fw_full3 = _run("gather_first_d2d", _ride_gather_d2d(first))
    cw_full = _unshard_cols(cw_full3)[:kc]
    fw_full = _unshard_cols(fw_full3)[:fw2.shape[0]]

    h0b = _ln0(x2, g0, b0, tm)
    p3, (b_out, b_up) = _proj("in_proj", h0b, w_in3, 6, 2 * tmm if t % (2 * tmm) == 0 else tmm, w // 2,
                              rider=_ride_gather_ici([b_out, b_up], [None, (0, 1, 4)]))
    (cat, u1), (w_out3, b_up) = _conv_fwd(
        p3, cw_full, conv_b, conv_norm_g, conv_norm_b, tm2, cb,
        rider=_ride_both(_ride_gather_d2d([b_out]), _ride_gather_ici([b_up], [(1, 2, 4)])))
    w_out_full = w_out3.reshape(d, d)
    (cat, o_pre, states), got = _hgrn_fwd(p3, lb_logits, hgrn_norm_g, cat, tb, hpb,
                                          rider=_ride_gather_ici([b_up], [(2, 4, 4)]))
    (xhat1, h1b, rstd1), (w_up3,) = _mix_ln1(cat, w_out_full, x2, g0, b0, ln1_g, ln1_b, alpha, tm2,
                                             rider=_ride_gather_d2d(got))
    hh3, got = _proj("ffn_up", h1b, w_up3, 2, tmm, dff // 4, rider=_ride_gather_ici([b_dn]))
    (act, gc), (w_dn3,) = _ffn_act_fwd(hh3, fw_full, ffn_conv_b, tm, cbf, rider=_ride_gather_d2d(got))
    ks = dff // N_CHIPS
    ffn = _wgrad("ffn_down", act, w_dn3, (t, d), (t // tmm, 1, N_CHIPS),
                 pl.BlockSpec((tmm, ks), lambda i, j, k: (i, k)),
                 pl.BlockSpec((None, ks, d), lambda i, j, k: (k, 0, 0)),
                 pl.BlockSpec((tmm, d), lambda i, j, k: (i, 0)), dot=_dot)
    dz2, dz2b, dg2, db2, loss_row = _ln2_loss(ffn, xhat1, tgt, ln1_g, ln1_b, ln2_g, ln2_b, alpha, tm2)

    dact = _proj_t("ffn_down_t", dz2b, w_dn3.reshape(dff, d), tmm, ks)
    dhh3, dfw, dfb = _ffn_act_bwd(dact, hh3, gc, fw_full, tm, cbf)
    tt = 2 * tmm if t % (2 * tmm) == 0 else tmm
    d_w_dn = _wgrad("wgrad_down", act, dz2b, (N_CHIPS, ks, d), (N_CHIPS, 2, t // tt),
                    pl.BlockSpec((tt, ks), lambda s, j, k: (k, s)),
                    pl.BlockSpec((tt, d // 2), lambda s, j, k: (k, j)),
                    pl.BlockSpec((None, ks, d // 2), lambda s, j, k: (s, 0, j)))
    wu = 2 * dff // N_CHIPS
    tnu = wu // 2
    per_sec_u = dff // tnu
    pre1, (arr_dn,) = _wgrad(
        "up_t", dhh3, w_up3, (t, d), (t // tmm, 1, 2 * N_CHIPS),
        pl.BlockSpec((None, tmm, tnu), lambda i, j, k: (k // per_sec_u, i, k % per_sec_u)),
        pl.BlockSpec((None, d, tnu), lambda i, j, k: (k // 2, 0, k % 2)),
        pl.BlockSpec((tmm, d), lambda i, j, k: (i, 0)), dot=_dot_nt, rider=_ride_swap([d_w_dn]))
    dz1, dz1b, dg1, db1 = _ln1_bwd(pre1, dz2, xhat1, rstd1, ln1_g, alpha, tm2)
    part_dn = _add_halves("add_halves_w_down", d_w_dn, arr_dn, c_idx)
    d_w_up, (land_dn,) = _wgrad(
        "wgrad_up", h1b, dhh3, (N_CHIPS, d, wu), (N_CHIPS, 2, 2, t // tt),
        pl.BlockSpec((tt, d // 2), lambda s, r, j, k: (k, r)),
        pl.BlockSpec((None, tt, tnu), lambda s, r, j, k: ((2 * s + j) // per_sec_u, k, (2 * s + j) % per_sec_u)),
        pl.BlockSpec((None, d // 2, tnu), lambda s, r, j, k: (s, r, j)), rider=_ride_send_partials([part_dn]))
    dcat = _proj_t("out_proj_t", dz1b, w_out_full, tmm, d // 2)
    d_w_out = _wgrad("wgrad_out", cat, dz1b, (d, d), (2, 2, t // tt),
                     pl.BlockSpec((tt, d // 2), lambda r, j, k: (k, r)),
                     pl.BlockSpec((tt, d // 2), lambda r, j, k: (k, j)),
                     pl.BlockSpec((d // 2, d // 2), lambda r, j, k: (r, j))).reshape(N_CHIPS, d // N_CHIPS, d)
    du1, dcng, dcnb = _conv_norm_bwd(dcat, u1, conv_norm_g, conv_norm_b, tm)
    (dp3, dcw, dcb), (arr_up, arr_out) = _conv_bwd(du1, p3, cw_full, tm2, cb, rider=_ride_swap([d_w_up, d_w_out]))
    part_up = _add_halves("add_halves_w_up", d_w_up, arr_up, c_idx)
    part_out = _add_halves("add_halves_w_out", d_w_out, arr_out, c_idx)
    (dp3, dlb, dhg), (land_up, land_out) = _hgrn_bwd(
        p3, lb_logits, hgrn_norm_g, o_pre, states, dcat, dp3, tb, hpb,
        rider=_ride_send_partials([part_up, part_out], [(0, 3, 4), None]))
    wi = 6 * w // N_CHIPS
    tni = w // 2
    d_w_in, (land_up,) = _wgrad(
        "wgrad_in", h0b, dp3, (N_CHIPS, d, wi), (N_CHIPS, 2, wi // tni, t // tt),
        pl.BlockSpec((tt, d // 2), lambda s, r, j, k: (k, r)),
        pl.BlockSpec((None, tt, tni), lambda s, r, j, k: (((wi // tni) * s + j) // 2, k, ((wi // tni) * s + j) % 2)),
        pl.BlockSpec((None, d // 2, tni), lambda s, r, j, k: (s, r, j)),
        rider=_ride_send_partials([part_up], [(3, 4, 4)], into=[land_up]))
    (arr_in,) = _run("swap_w_in", _ride_swap([d_w_in]))
    part_in = _add_halves("add_halves_w_in", d_w_in, arr_in, c_idx)
    h_out, h_up, h_dn = [
        _sum_partials("sum_partials_" + nm, p, a, place_idx)
        for nm, p, a in (("w_out", part_out, land_out), ("w_up", part_up, land_up), ("w_down", part_dn, land_dn))]
    pre0, (land_in, g_w_out, g_w_up, g_w_dn) = _in_t(
        dp3, w_in3, tmm, rider=_ride_both(_ride_send_partials([part_in]), _ride_join([h_out, h_up, h_dn])))
    dx, dg0, db0 = _ln0_bwd(pre0, dz1, x2, g0, alpha, tm2)
    h_in = _sum_partials("sum_partials_w_in", part_in, land_in, place_idx)
    (g_w_in,) = _run("join_w_in", _ride_join([h_in]))

    kpad = dcw.shape[0]
    wide = [(dcw, kpad, 1), (dg0, 1, 2), (db0, 1, 2), (dg1, 1, 2), (db1, 1, 2), (dg2, 1, 2), (db2, 1, 2),
            (dcb, 1, 1), (dcng, 1, 1), (dcnb, 1, 1), (dlb, 1, 1), (dhg, 1, 1)]
    n_wide = sum(r * m for _, r, m in wide)
    n_wide_pad = -(-n_wide // SUBLANE) * SUBLANE
    s1, s2 = _small_allreduce(wide, [dfw, dfb], w, dff, n_wide_pad, 2 * SUBLANE)
    cw_g = lax.dynamic_slice_in_dim(s1[0:kpad], chip * (w // N_CHIPS), w // N_CHIPS, axis=1)
    fw_g = lax.dynamic_slice_in_dim(s2[0:SUBLANE], chip * (dff // N_CHIPS), dff // N_CHIPS, axis=1)

    small = [
        (g0, m_emb_ln_g.reshape(1, d), v_emb_ln_g.reshape(1, d)), (b0, m_emb_ln_b.reshape(1, d), v_emb_ln_b.reshape(1, d)),
        (cw2, m_conv_w[0], v_conv_w[0]), (conv_b, m_conv_b, v_conv_b),
        (conv_norm_g, m_conv_norm_g, v_conv_norm_g), (conv_norm_b, m_conv_norm_b, v_conv_norm_b),
        (lb_logits, m_lb_logits, v_lb_logits), (hgrn_norm_g, m_hgrn_norm_g, v_hgrn_norm_g),
        (ln1_g, m_ln1_g, v_ln1_g), (ln1_b, m_ln1_b, v_ln1_b),
        (fw2, m_ffn_conv_w[0], v_ffn_conv_w[0]), (ffn_conv_b, m_ffn_conv_b, v_ffn_conv_b),
        (ln2_g, m_ln2_g, v_ln2_g), (ln2_b, m_ln2_b, v_ln2_b),
    ]
    r0 = kpad
    layout = [("wide", r0, 1, 2), ("wide", r0 + 2, 1, 2), ("cw",), ("wide", r0 + 12, 1, 1), ("wide", r0 + 13, 1, 1),
              ("wide", r0 + 14, 1, 1), ("lb", r0 + 15), ("wide", r0 + 16, 1, 1), ("wide", r0 + 4, 1, 2),
              ("wide", r0 + 6, 1, 2), ("fw",), ("ffn", SUBLANE, 1), ("wide", r0 + 8, 1, 2), ("wide", r0 + 10, 1, 2)]
    so = _adam_small(s1, s2, cw_g, fw_g, lb_logits, small, layout, w)
    sm = {nm: so[4 * i:4 * i + 4] for i, nm in enumerate(
        ["emb_ln_g", "emb_ln_b", "conv_w", "conv_b", "conv_norm_g", "conv_norm_b", "lb_logits", "hgrn_norm_g",
         "ln1_g", "ln1_b", "ffn_conv_w", "ffn_conv_b", "ln2_g", "ln2_b"])}
    bigs = {}
    for nm, wt, g, m, v in (("w_in", w_in2, g_w_in, m_w_in[0], v_w_in[0]), ("w_out", w_out2, g_w_out, m_w_out[0], v_w_out[0]),
                            ("w_ffn_up", w_up2, g_w_up, m_w_ffn_up[0], v_w_ffn_up[0]),
                            ("w_ffn_down", w_dn2, g_w_dn, m_w_ffn_down[0], v_w_ffn_down[0])):
        bigs[nm] = tuple(_adam_big("adam_" + nm, wt, g, m, v))

    loss = lax.psum(loss_row[0, 0], ("x", "y", "c"))

    order = ["emb_ln_g", "emb_ln_b", "w_in", "conv_w", "conv_b", "conv_norm_g", "conv_norm_b", "lb_logits",
             "hgrn_norm_g", "w_out", "ln1_g", "ln1_b", "w_ffn_up", "ffn_conv_w", "ffn_conv_b", "w_ffn_down",
             "ln2_g", "ln2_b"]
    shapes = dict(emb_ln_g=emb_ln_g.shape, emb_ln_b=emb_ln_b.shape, w_in=w_in.shape, conv_w=conv_w.shape,
                  w_out=w_out.shape, w_ffn_up=w_ffn_up.shape, ffn_conv_w=ffn_conv_w.shape, w_ffn_down=w_ffn_down.shape)
    outs = [loss, dx.reshape(x.shape)]
    for which in range(4):
        for nm in order:
            a = bigs[nm][which] if nm in bigs else sm[nm][which]
            outs.append(a.reshape(shapes[nm]) if nm in shapes else a)
    return tuple(outs)


def _pad_rows(a):
    k = a.shape[0]
    kp = -(-k // 16) * 16
    return jnp.pad(a, ((0, kp - k), (0, 0)))


def _unshard_cols(a3):
    s, k, c = a3.shape
    return jnp.transpose(a3, (1, 0, 2)).reshape(k, s * c)
```

```python
import functools

import jax
import jax.numpy as jnp
from jax import lax
from jax.experimental import pallas as pl
from jax.experimental.pallas import tpu as pltpu

F32 = jnp.float32
BF16 = jnp.bfloat16

LN_EPS = 1e-5
RMS_EPS = 1e-6
LANE = 128
SUBLANE = 8
CHUNK = 64
SUB = 8
HALO = 32
FHALO = 8
ROWS = 64
N_CHIPS = 4
VMEM_LIMIT = 56 << 20
NEG_BIG = -1e30

ADAM_LR = 0.001
ADAM_B1 = 0.9
ADAM_B2 = 0.999
ADAM_EPS = 1e-08
ADAM_WD = 0.01
ADAM_STEP = 10

MESH = pl.DeviceIdType.MESH
HBM = pl.BlockSpec(memory_space=pl.ANY)
VMEM_FULL = pl.BlockSpec(memory_space=pltpu.VMEM)


def _params(*sem):
    return pltpu.CompilerParams(dimension_semantics=sem, vmem_limit_bytes=VMEM_LIMIT)


class _Rider:
    def __init__(self, ins, outs, aliases, n_sems, start, finish):
        self.ins, self.outs, self.aliases = list(ins), list(outs), dict(aliases)
        self.n_sems, self.start, self.finish = n_sems, start, finish


def _call(body, args, *, name, grid, in_specs, out_specs, out_shape, scratch_shapes=(), aliases=None, rider=None):
    many = isinstance(out_shape, (list, tuple))
    shapes = list(out_shape) if many else [out_shape]
    ospecs = list(out_specs) if many else [out_specs]
    aliases = dict(aliases or {})
    sem = ("arbitrary",) * len(grid)
    if rider is None:
        res = pl.pallas_call(
            body, name=name, grid=grid, in_specs=list(in_specs), out_specs=ospecs, out_shape=shapes,
            scratch_shapes=list(scratch_shapes), input_output_aliases=aliases, compiler_params=_params(*sem))(*args)
        return res if many else res[0]
    n_in, n_out, n_scr = len(args), len(shapes), len(scratch_shapes)
    nri, nro = len(rider.ins), len(rider.outs)

    def wrapped(*refs):
        ins, rin = refs[:n_in], refs[n_in:n_in + nri]
        o0 = n_in + nri
        outs, rout = refs[o0:o0 + n_out], refs[o0 + n_out:o0 + n_out + nro]
        s0 = o0 + n_out + nro
        scr, (send, recv) = refs[s0:s0 + n_scr], refs[s0 + n_scr:]
        ids = [pl.program_id(a) for a in range(len(grid))]
        first = functools.reduce(jnp.logical_and, [i == 0 for i in ids])
        last = functools.reduce(jnp.logical_and, [i == g - 1 for i, g in zip(ids, grid)])

        @pl.when(first)
        def _():
            rider.start(rin, rout, send, recv)

        body(*ins, *outs, *scr)

        @pl.when(last)
        def _():
            rider.finish(rin, rout, send, recv)

    for ri, ro in rider.aliases.items():
        aliases[n_in + ri] = n_out + ro
    res = pl.pallas_call(
        wrapped, name=name, grid=grid, in_specs=list(in_specs) + [HBM] * nri, out_specs=ospecs + [HBM] * nro,
        out_shape=shapes + rider.outs,
        scratch_shapes=list(scratch_shapes) + [pltpu.SemaphoreType.DMA((rider.n_sems,)),
                                               pltpu.SemaphoreType.DMA((rider.n_sems,))],
        input_output_aliases=aliases, compiler_params=_params(*sem))(*args, *rider.ins)
    main, extra = res[:n_out], list(res[n_out:])
    return (list(main) if many else main[0]), extra


def _div_tile(n, mult, cap):
    best = n
    for t in range(mult, min(n, cap) + 1, mult):
        if n % t == 0:
            best = t
    return best


def _sigmoid(x):
    return 1.0 / (1.0 + jnp.exp(-x))


def _ln_stats(x):
    mu = jnp.mean(x, axis=-1, keepdims=True)
    xc = x - mu
    var = jnp.mean(xc * xc, axis=-1, keepdims=True)
    rstd = lax.rsqrt(var + LN_EPS)
    return xc * rstd, rstd


def _ln_bwd(dy, xhat, rstd, g):
    dyg = dy * g
    m1 = jnp.mean(dyg, axis=-1, keepdims=True)
    m2 = jnp.mean(dyg * xhat, axis=-1, keepdims=True)
    return rstd * (dyg - m1 - xhat * m2)


def _dot_nt(a, b):
    return lax.dot_general(a, b, (((1,), (1,)), ((), ())), preferred_element_type=F32)


def _dot_tn(a, b):
    return lax.dot_general(a, b, (((0,), (0,)), ((), ())), preferred_element_type=F32)


def _dot(a, b):
    return jnp.dot(a, b, preferred_element_type=F32)


def _dot3(m, x):
    mb = m.astype(BF16)
    x1 = x.astype(BF16)
    r1 = x - x1.astype(F32)
    x2 = r1.astype(BF16)
    x3 = (r1 - x2.astype(F32)).astype(BF16)
    return _dot(mb, x1) + _dot(mb, x2) + _dot(mb, x3)


def _place_shard(x, name, chip_idx, dtype):
    r, c = x.shape
    tr = _div_tile(r, 16, 512)

    def body(s_ref, x_ref, o_ref):
        del s_ref
        o_ref[...] = x_ref[...].astype(dtype)

    return pl.pallas_call(
        body, name=name,
        grid_spec=pltpu.PrefetchScalarGridSpec(
            num_scalar_prefetch=1, grid=(r // tr,),
            in_specs=[pl.BlockSpec((tr, c), lambda i, s: (i, 0))],
            out_specs=pl.BlockSpec((None, tr, c), lambda i, s: (s[0], i, 0))),
        out_shape=jax.ShapeDtypeStruct((N_CHIPS, r, c), dtype),
        compiler_params=_params("parallel"),
    )(chip_idx, x)


def _ln0(x, g, b, tm, rider=None):
    t, d = x.shape

    def body(x_ref, g_ref, b_ref, o_ref):
        xh, _ = _ln_stats(x_ref[...])
        o_ref[...] = (xh * g_ref[...] + b_ref[...]).astype(BF16)

    row = pl.BlockSpec((1, d), lambda i: (0, 0))
    return _call(
        body, (x, g, b), name="ln0", grid=(t // tm,),
        in_specs=[pl.BlockSpec((tm, d), lambda i: (i, 0)), row, row],
        out_specs=pl.BlockSpec((tm, d), lambda i: (i, 0)),
        out_shape=jax.ShapeDtypeStruct((t, d), BF16), rider=rider)


def _proj(name, a, w3, n_sec, tm, tn, rider=None):
    m, k = a.shape
    s, _, ws = w3.shape
    sec_w = s * ws // n_sec
    nj = ws // tn
    per_sec = sec_w // tn

    def body(a_ref, w_ref, o_ref):
        o_ref[...] = _dot(a_ref[...], w_ref[...])

    return _call(
        body, (a, w3), name=name, grid=(s * nj, m // tm),
        in_specs=[pl.BlockSpec((tm, k), lambda j, i: (i, 0)),
                  pl.BlockSpec((None, k, tn), lambda j, i: (j // nj, 0, j % nj))],
        out_specs=pl.BlockSpec((None, tm, tn), lambda j, i: (j // per_sec, i, j % per_sec)),
        out_shape=jax.ShapeDtypeStruct((n_sec, m, sec_w), F32), rider=rider)


def _proj_t(name, a, w, tm, tn, rider=None):
    m, k = a.shape
    n = w.shape[0]

    def body(a_ref, w_ref, o_ref):
        o_ref[...] = _dot_nt(a_ref[...], w_ref[...])

    return _call(
        body, (a, w), name=name, grid=(n // tn, m // tm),
        in_specs=[pl.BlockSpec((tm, k), lambda j, i: (i, 0)),
                  pl.BlockSpec((tn, k), lambda j, i: (j, 0))],
        out_specs=pl.BlockSpec((tm, tn), lambda j, i: (i, j)),
        out_shape=jax.ShapeDtypeStruct((m, n), F32), rider=rider)


def _wgrad(name, a, b, out_shape, grid, a_spec, b_spec, o_spec, rider=None, dot=_dot_tn):
    nt = len(grid) - 1

    def body(a_ref, b_ref, o_ref):
        t = pl.program_id(nt)
        prod = dot(a_ref[...], b_ref[...])

        @pl.when(t == 0)
        def _():
            o_ref[...] = prod

        @pl.when(t > 0)
        def _():
            o_ref[...] += prod

    return _call(
        body, (a, b), name=name, grid=grid, in_specs=[a_spec, b_spec], out_specs=o_spec,
        out_shape=jax.ShapeDtypeStruct(out_shape, F32), rider=rider)


def _wgrad_in(h0b, dp3, n_shards, tt, rider=None):
    t, d = h0b.shape
    n_sec, _, sec_w = dp3.shape
    ws = n_sec * sec_w // n_shards
    tn = sec_w // 2
    nq = ws // tn
    tr = d // 2

    def body(*refs):
        a_ref, b_refs, o_ref = refs[0], refs[1:1 + nq], refs[1 + nq]
        k = pl.program_id(2)
        a = a_ref[...]
        prods = [_dot_tn(a, b_ref[...]) for b_ref in b_refs]

        @pl.when(k == 0)
        def _():
            for q in range(nq):
                o_ref[:, q * tn:(q + 1) * tn] = prods[q]

        @pl.when(k > 0)
        def _():
            for q in range(nq):
                o_ref[:, q * tn:(q + 1) * tn] += prods[q]

    def b_spec(q):
        return pl.BlockSpec((None, tt, tn), lambda s, r, k: ((nq * s + q) // 2, k, (nq * s + q) % 2))

    return _call(
        body, (h0b,) + (dp3,) * nq, name="wgrad_in", grid=(n_shards, d // tr, t // tt),
        in_specs=[pl.BlockSpec((tt, tr), lambda s, r, k: (k, r))] + [b_spec(q) for q in range(nq)],
        out_specs=pl.BlockSpec((None, tr, ws), lambda s, r, k: (s, r, 0)),
        out_shape=jax.ShapeDtypeStruct((n_shards, d, ws), F32), rider=rider)


def _in_t(dp3, w_in3, tm, rider=None):
    _, t, sec_w = dp3.shape
    s, d, ws = w_in3.shape
    tk = sec_w // 2
    nq = ws // tk

    def body(*refs):
        a_refs, w_ref, o_ref = refs[:nq], refs[nq], refs[nq + 1]
        k = pl.program_id(1)
        prod = _dot_nt(a_refs[0][...], w_ref[:, 0:tk])
        for q in range(1, nq):
            prod = prod + _dot_nt(a_refs[q][...], w_ref[:, q * tk:(q + 1) * tk])

        @pl.when(k == 0)
        def _():
            o_ref[...] = prod

        @pl.when(k > 0)
        def _():
            o_ref[...] += prod

    def a_spec(q):
        return pl.BlockSpec((None, tm, tk), lambda i, k: ((nq * k + q) // 2, i, (nq * k + q) % 2))

    return _call(
        body, (dp3,) * nq + (w_in3,), name="in_t", grid=(t // tm, s),
        in_specs=[a_spec(q) for q in range(nq)] + [pl.BlockSpec((None, d, ws), lambda i, k: (k, 0, 0))],
        out_specs=pl.BlockSpec((tm, d), lambda i, k: (i, 0)),
        out_shape=jax.ShapeDtypeStruct((t, d), F32), rider=rider)


def _mix_ln1(cat, w_out, x, g0, b0, g1, b1, alpha, tm, rider=None):
    t, d = x.shape

    def body(cat_ref, w_ref, x_ref, g0_ref, b0_ref, g1_ref, b1_ref, xh_ref, h1b_ref, rstd_ref):
        mix = _dot(cat_ref[...], w_ref[...])
        xh0, _ = _ln_stats(x_ref[...])
        z1 = alpha * (xh0 * g0_ref[...] + b0_ref[...]) + mix
        xh1, rstd1 = _ln_stats(z1)
        xh_ref[...] = xh1
        h1b_ref[...] = (xh1 * g1_ref[...] + b1_ref[...]).astype(BF16)
        rstd_ref[...] = rstd1

    row = pl.BlockSpec((1, d), lambda i: (0, 0))
    blk = pl.BlockSpec((tm, d), lambda i: (i, 0))
    return _call(
        body, (cat, w_out, x, g0, b0, g1, b1), name="mix_ln1", grid=(t // tm,),
        in_specs=[blk, pl.BlockSpec((d, d), lambda i: (0, 0)), blk, row, row, row, row],
        out_specs=[blk, blk, pl.BlockSpec((tm, 1), lambda i: (i, 0))],
        out_shape=[jax.ShapeDtypeStruct((t, d), F32), jax.ShapeDtypeStruct((t, d), BF16),
                   jax.ShapeDtypeStruct((t, 1), F32)], rider=rider)


def _ln2_loss(ffn, xhat1, tgt, g1, b1, g2, b2, alpha, tm):
    t, d = xhat1.shape
    ni = t // tm
    inv_d = 1.0 / d

    def body(ffn_ref, xh1_ref, tgt_ref, g1_ref, b1_ref, g2_ref, b2_ref,
             dz2_ref, dz2b_ref, dg2_ref, db2_ref, loss_ref, lrow):
        i = pl.program_id(0)
        h1 = xh1_ref[...] * g1_ref[...] + b1_ref[...]
        xh2, rstd2 = _ln_stats(alpha * h1 + ffn_ref[...])
        g2v = g2_ref[...]
        diff = xh2 * g2v + b2_ref[...] - tgt_ref[...]
        dh2 = diff * inv_d
        sq = jnp.sum(diff * diff, axis=0, keepdims=True)
        dg = jnp.sum(dh2 * xh2, axis=0, keepdims=True)
        db = jnp.sum(dh2, axis=0, keepdims=True)

        @pl.when(i == 0)
        def _():
            lrow[...] = sq
            dg2_ref[...] = dg
            db2_ref[...] = db

        @pl.when(i > 0)
        def _():
            lrow[...] += sq
            dg2_ref[...] += dg
            db2_ref[...] += db

        dz2 = _ln_bwd(dh2, xh2, rstd2, g2v)
        dz2_ref[...] = dz2
        dz2b_ref[...] = dz2.astype(BF16)

        @pl.when(i == ni - 1)
        def _():
            tot = jnp.sum(lrow[...], axis=-1, keepdims=True) * (0.5 * inv_d)
            loss_ref[...] = jnp.broadcast_to(tot, (1, LANE))

    row = pl.BlockSpec((1, d), lambda i: (0, 0))
    blk = pl.BlockSpec((tm, d), lambda i: (i, 0))
    return _call(
        body, (ffn, xhat1, tgt, g1, b1, g2, b2), name="ln2_loss", grid=(ni,),
        in_specs=[blk, blk, blk, row, row, row, row],
        out_specs=[blk, blk, row, row, pl.BlockSpec((1, LANE), lambda i: (0, 0))],
        out_shape=[jax.ShapeDtypeStruct((t, d), F32), jax.ShapeDtypeStruct((t, d), BF16),
                   jax.ShapeDtypeStruct((1, d), F32), jax.ShapeDtypeStruct((1, d), F32),
                   jax.ShapeDtypeStruct((1, LANE), F32)],
        scratch_shapes=[pltpu.VMEM((1, d), F32)])


def _ln1_bwd(pre, dz2, xhat1, rstd1, g1, alpha, tm):
    t, d = dz2.shape

    def body(pre_ref, dz2_ref, xh_ref, rstd_ref, g_ref, dz1_ref, dz1b_ref, dg_ref, db_ref):
        i = pl.program_id(0)
        dh1 = alpha * dz2_ref[...] + pre_ref[...]
        xh = xh_ref[...]
        dg = jnp.sum(dh1 * xh, axis=0, keepdims=True)
        db = jnp.sum(dh1, axis=0, keepdims=True)

        @pl.when(i == 0)
        def _():
            dg_ref[...] = dg
            db_ref[...] = db

        @pl.when(i > 0)
        def _():
            dg_ref[...] += dg
            db_ref[...] += db

        dz1 = _ln_bwd(dh1, xh, rstd_ref[...], g_ref[...])
        dz1_ref[...] = dz1
        dz1b_ref[...] = dz1.astype(BF16)

    row = pl.BlockSpec((1, d), lambda i: (0, 0))
    blk = pl.BlockSpec((tm, d), lambda i: (i, 0))
    return _call(
        body, (pre, dz2, xhat1, rstd1, g1), name="ln1_bwd", grid=(t // tm,),
        in_specs=[blk, blk, blk, pl.BlockSpec((tm, 1), lambda i: (i, 0)), row],
        out_specs=[blk, blk, row, row],
        out_shape=[jax.ShapeDtypeStruct((t, d), F32), jax.ShapeDtypeStruct((t, d), BF16),
                   jax.ShapeDtypeStruct((1, d), F32), jax.ShapeDtypeStruct((1, d), F32)])


def _ln0_bwd(pre, dz1, x, g0, alpha, tm):
    t, d = x.shape

    def body(pre_ref, dz1_ref, x_ref, g_ref, dx_ref, dg_ref, db_ref):
        i = pl.program_id(0)
        dh0 = alpha * dz1_ref[...] + pre_ref[...]
        xh, rstd = _ln_stats(x_ref[...])
        dg = jnp.sum(dh0 * xh, axis=0, keepdims=True)
        db = jnp.sum(dh0, axis=0, keepdims=True)

        @pl.when(i == 0)
        def _():
            dg_ref[...] = dg
            db_ref[...] = db

        @pl.when(i > 0)
        def _():
            dg_ref[...] += dg
            db_ref[...] += db

        dx_ref[...] = _ln_bwd(dh0, xh, rstd, g_ref[...])

    row = pl.BlockSpec((1, d), lambda i: (0, 0))
    blk = pl.BlockSpec((tm, d), lambda i: (i, 0))
    return _call(
        body, (pre, dz1, x, g0), name="ln0_bwd", grid=(t // tm,),
        in_specs=[blk, blk, blk, row], out_specs=[blk, row, row],
        out_shape=[jax.ShapeDtypeStruct((t, d), F32), jax.ShapeDtypeStruct((1, d), F32),
                   jax.ShapeDtypeStruct((1, d), F32)])


def _shift_copies(ext, shifted):
    n = shifted.shape[1]
    for p in range(1, SUBLANE):
        shifted[p - 1] = ext[pl.ds(p, n), :]


def _window(ext, shifted, start, rows):
    p = start % SUBLANE
    if p == 0:
        return ext[pl.ds(start, rows), :]
    return shifted[p - 1, pl.ds(start - p, rows), :]


def _conv_fwd(p3, conv_w, conv_b, cn_g, cn_b, tc, cb, rider=None):
    _, t, w = p3.shape
    kk = conv_w.shape[0]
    off = HALO - (kk - 1)
    hb = tc // HALO

    def body(a_ref, g_ref, ap_ref, gp_ref, w_ref, b_ref, ng_ref, nb_ref, cat_ref, u1_ref, ext, sh):
        i = pl.program_id(1)
        ext[pl.ds(HALO, tc), :] = a_ref[...] * _sigmoid(g_ref[...])
        prev = ap_ref[...] * _sigmoid(gp_ref[...])
        ext[pl.ds(0, HALO), :] = jnp.where(i > 0, prev, 0.0)
        _shift_copies(ext, sh)
        for r in range(tc // ROWS):
            acc = jnp.broadcast_to(b_ref[...], (ROWS, cb))
            for k in range(kk):
                acc = acc + w_ref[k:k + 1, :] * _window(ext, sh, r * ROWS + off + k, ROWS)
            u1_ref[pl.ds(r * ROWS, ROWS), :] = acc
            for g in range(cb // LANE):
                sl = slice(g * LANE, (g + 1) * LANE)
                xh, _ = _ln_stats(acc[:, sl])
                u2 = xh * ng_ref[:, sl] + nb_ref[:, sl]
                cat_ref[pl.ds(r * ROWS, ROWS), sl] = (u2 * _sigmoid(u2)).astype(BF16)

    cur = lambda sec: pl.BlockSpec((None, tc, cb), lambda j, i: (sec, i, j))
    prev = lambda sec: pl.BlockSpec((None, HALO, cb), lambda j, i: (sec, jnp.maximum(i * hb - 1, 0), j))
    row = pl.BlockSpec((1, cb), lambda j, i: (0, j))
    return _call(
        body, (p3, p3, p3, p3, conv_w, conv_b, cn_g, cn_b), name="conv_fwd", grid=(w // cb, t // tc),
        in_specs=[cur(0), cur(1), prev(0), prev(1), pl.BlockSpec((kk, cb), lambda j, i: (0, j)), row, row, row],
        out_specs=[pl.BlockSpec((tc, cb), lambda j, i: (i, j)), pl.BlockSpec((tc, cb), lambda j, i: (i, j))],
        out_shape=[jax.ShapeDtypeStruct((t, 2 * w), BF16), jax.ShapeDtypeStruct((t, w), F32)],
        scratch_shapes=[pltpu.VMEM((tc + HALO, cb), F32),
                        pltpu.VMEM((SUBLANE - 1, tc + HALO - SUBLANE, cb), F32)], rider=rider)


def _conv_norm_bwd(dcat, u1, cn_g, cn_b, tc):
    t, w = u1.shape

    def body(du_ref, u1_ref, ng_ref, nb_ref, du1_ref, dg_ref, db_ref):
        i = pl.program_id(0)
        for g in range(w // LANE):
            sl = slice(g * LANE, (g + 1) * LANE)
            ng = ng_ref[:, sl]
            xh, rstd = _ln_stats(u1_ref[:, sl])
            u2 = xh * ng + nb_ref[:, sl]
            sg = _sigmoid(u2)
            du2 = du_ref[:, sl] * (sg * (1.0 + u2 * (1.0 - sg)))
            dg = jnp.sum(du2 * xh, axis=0, keepdims=True)
            db = jnp.sum(du2, axis=0, keepdims=True)

            @pl.when(i == 0)
            def _():
                dg_ref[:, sl] = dg
                db_ref[:, sl] = db

            @pl.when(i > 0)
            def _():
                dg_ref[:, sl] += dg
                db_ref[:, sl] += db

            du1_ref[:, sl] = _ln_bwd(du2, xh, rstd, ng)

    row = pl.BlockSpec((1, w), lambda i: (0, 0))
    blk = pl.BlockSpec((tc, w), lambda i: (i, 0))
    return pl.pallas_call(
        body, name="conv_norm_bwd", grid=(t // tc,),
        in_specs=[blk, blk, row, row], out_specs=[blk, row, row],
        out_shape=[jax.ShapeDtypeStruct((t, w), F32), jax.ShapeDtypeStruct((1, w), F32),
                   jax.ShapeDtypeStruct((1, w), F32)],
        compiler_params=_params("arbitrary"),
    )(dcat, u1, cn_g, cn_b)


def _conv_bwd(du1, p3, conv_w, tc, cb, rider=None):
    n_sec, t, w = p3.shape
    kk = conv_w.shape[0]
    off = HALO - (kk - 1)
    hb = tc // HALO
    nt = t // tc
    kpad = -(-kk // SUBLANE) * SUBLANE

    def body(d_ref, dn_ref, a_ref, g_ref, ap_ref, gp_ref, w_ref, dp_ref, dw_ref, db_ref,
             extd, extu, shd, shu, wacc, bacc):
        i = pl.program_id(1)

        @pl.when(i == 0)
        def _():
            wacc[...] = jnp.zeros_like(wacc)
            bacc[...] = jnp.zeros_like(bacc)

        extd[pl.ds(0, tc), :] = d_ref[...]
        extd[pl.ds(tc, HALO), :] = jnp.where(i < nt - 1, dn_ref[...], 0.0)
        extu[pl.ds(HALO, tc), :] = a_ref[...] * _sigmoid(g_ref[...])
        extu[pl.ds(0, HALO), :] = jnp.where(i > 0, ap_ref[...] * _sigmoid(gp_ref[...]), 0.0)
        _shift_copies(extd, shd)
        _shift_copies(extu, shu)
        for r in range(tc // ROWS):
            rows = pl.ds(r * ROWS, ROWS)
            acc = jnp.zeros((ROWS, cb), F32)
            for k in range(kk):
                acc = acc + w_ref[k:k + 1, :] * _window(extd, shd, r * ROWS + (kk - 1) - k, ROWS)
            a = a_ref[rows, :]
            sg = _sigmoid(g_ref[rows, :])
            dp_ref[0, rows, :] = (acc * sg).astype(BF16)
            dp_ref[1, rows, :] = (acc * a * sg * (1.0 - sg)).astype(BF16)
            d = d_ref[rows, :]
            bacc[...] += jnp.sum(d.reshape(ROWS // SUBLANE, SUBLANE, cb), axis=0)
            for k in range(kk):
                prod = d * _window(extu, shu, r * ROWS + off + k, ROWS)
                wacc[k] += jnp.sum(prod.reshape(ROWS // SUBLANE, SUBLANE, cb), axis=0)

        @pl.when(i == nt - 1)
        def _():
            for k in range(kk):
                dw_ref[k:k + 1, :] = jnp.sum(wacc[k], axis=0, keepdims=True)
            if kpad > kk:
                dw_ref[kk:kpad, :] = jnp.zeros((kpad - kk, cb), F32)
            db_ref[...] = jnp.sum(bacc[...], axis=0, keepdims=True)

    cur = lambda sec: pl.BlockSpec((None, tc, cb), lambda j, i: (sec, i, j))
    prev = lambda sec: pl.BlockSpec((None, HALO, cb), lambda j, i: (sec, jnp.maximum(i * hb - 1, 0), j))
    return _call(
        body, (du1, du1, p3, p3, p3, p3, conv_w), name="conv_bwd", grid=(w // cb, nt),
        in_specs=[pl.BlockSpec((tc, cb), lambda j, i: (i, j)),
                  pl.BlockSpec((HALO, cb), lambda j, i: (jnp.minimum((i + 1) * hb, t // HALO - 1), j)),
                  cur(0), cur(1), prev(0), prev(1), pl.BlockSpec((kk, cb), lambda j, i: (0, j))],
        out_specs=[pl.BlockSpec((2, tc, cb), lambda j, i: (0, i, j)),
                   pl.BlockSpec((kpad, cb), lambda j, i: (0, j)),
                   pl.BlockSpec((1, cb), lambda j, i: (0, j))],
        out_shape=[jax.ShapeDtypeStruct((n_sec, t, w), BF16), jax.ShapeDtypeStruct((kpad, w), F32),
                   jax.ShapeDtypeStruct((1, w), F32)],
        scratch_shapes=[pltpu.VMEM((tc + HALO, cb), F32), pltpu.VMEM((tc + HALO, cb), F32),
                        pltpu.VMEM((SUBLANE - 1, tc + HALO - SUBLANE, cb), F32),
                        pltpu.VMEM((SUBLANE - 1, tc + HALO - SUBLANE, cb), F32),
                        pltpu.VMEM((kk, SUBLANE, cb), F32), pltpu.VMEM((SUBLANE, cb), F32)], rider=rider)


def _ffn_act_fwd(hh3, fw, fb, tc, cb, rider=None):
    _, t, dff = hh3.shape
    kk = fw.shape[0]
    off = FHALO - (kk - 1)
    hb = tc // FHALO

    def body(g_ref, v_ref, gp_ref, w_ref, b_ref, act_ref, ext):
        i = pl.program_id(1)
        ext[pl.ds(FHALO, tc), :] = g_ref[...]
        ext[pl.ds(0, FHALO), :] = jnp.where(i > 0, gp_ref[...], 0.0)
        for r in range(tc // ROWS):
            rows = pl.ds(r * ROWS, ROWS)
            gc = jnp.broadcast_to(b_ref[...], (ROWS, cb))
            for k in range(kk):
                gc = gc + w_ref[k:k + 1, :] * ext[pl.ds(r * ROWS + off + k, ROWS), :]
            act_ref[rows, :] = (gc * _sigmoid(gc) * v_ref[rows, :]).astype(BF16)

    return _call(
        body, (hh3, hh3, hh3, fw, fb), name="ffn_act_fwd", grid=(dff // cb, t // tc),
        in_specs=[pl.BlockSpec((None, tc, cb), lambda j, i: (0, i, j)),
                  pl.BlockSpec((None, tc, cb), lambda j, i: (1, i, j)),
                  pl.BlockSpec((None, FHALO, cb), lambda j, i: (0, jnp.maximum(i * hb - 1, 0), j)),
                  pl.BlockSpec((kk, cb), lambda j, i: (0, j)),
                  pl.BlockSpec((1, cb), lambda j, i: (0, j))],
        out_specs=pl.BlockSpec((tc, cb), lambda j, i: (i, j)),
        out_shape=jax.ShapeDtypeStruct((t, dff), BF16),
        scratch_shapes=[pltpu.VMEM((tc + FHALO, cb), F32)], rider=rider)


def _ffn_act_bwd(dact, hh3, fw, fb, tc, cb):
    _, t, dff = hh3.shape
    kk = fw.shape[0]
    off = FHALO - (kk - 1)
    hb = tc // FHALO
    nt = t // tc
    te = tc + FHALO

    def body(da_ref, dan_ref, g_ref, gp_ref, gn_ref, v_ref, vn_ref, w_ref, b_ref,
             dhh_ref, dw_ref, db_ref, gext, dext, wacc, bacc):
        i = pl.program_id(1)

        @pl.when(i == 0)
        def _():
            wacc[...] = jnp.zeros_like(wacc)
            bacc[...] = jnp.zeros_like(bacc)

        gext[pl.ds(0, FHALO), :] = jnp.where(i > 0, gp_ref[...], 0.0)
        gext[pl.ds(FHALO, tc), :] = g_ref[...]
        gext[pl.ds(FHALO + tc, FHALO), :] = gn_ref[...]

        def gate_grad(r0, n, da, v):
            gc = jnp.broadcast_to(b_ref[...], (n, cb))
            for k in range(kk):
                gc = gc + w_ref[k:k + 1, :] * gext[pl.ds(r0 + off + k, n), :]
            sg = _sigmoid(gc)
            return gc * sg, da * v * (sg * (1.0 + gc * (1.0 - sg)))

        for r in range(tc // ROWS):
            rows = pl.ds(r * ROWS, ROWS)
            da = da_ref[rows, :]
            silu, dgc = gate_grad(r * ROWS, ROWS, da, v_ref[rows, :])
            dext[rows, :] = dgc
            dhh_ref[1, rows, :] = (da * silu).astype(BF16)
        _, dgc_next = gate_grad(tc, FHALO, dan_ref[...], vn_ref[...])
        dext[pl.ds(tc, FHALO), :] = jnp.where(i < nt - 1, dgc_next, 0.0)
        for r in range(tc // ROWS):
            rows = pl.ds(r * ROWS, ROWS)
            dg = jnp.zeros((ROWS, cb), F32)
            for k in range(kk):
                dg = dg + w_ref[k:k + 1, :] * dext[pl.ds(r * ROWS + (kk - 1) - k, ROWS), :]
            dhh_ref[0, rows, :] = dg.astype(BF16)
            dgc = dext[rows, :]
            bacc[...] += jnp.sum(dgc.reshape(ROWS // SUBLANE, SUBLANE, cb), axis=0)
            for k in range(kk):
                prod = dgc * gext[pl.ds(r * ROWS + off + k, ROWS), :]
                wacc[k] += jnp.sum(prod.reshape(ROWS // SUBLANE, SUBLANE, cb), axis=0)

        @pl.when(i == nt - 1)
        def _():
            for k in range(kk):
                dw_ref[k:k + 1, :] = jnp.sum(wacc[k], axis=0, keepdims=True)
            dw_ref[kk:SUBLANE, :] = jnp.zeros((SUBLANE - kk, cb), F32)
            db_ref[...] = jnp.sum(bacc[...], axis=0, keepdims=True)

    nxt = lambda i: jnp.minimum((i + 1) * hb, t // FHALO - 1)
    return pl.pallas_call(
        body, name="ffn_act_bwd", grid=(dff // cb, nt),
        in_specs=[pl.BlockSpec((tc, cb), lambda j, i: (i, j)),
                  pl.BlockSpec((FHALO, cb), lambda j, i: (nxt(i), j)),
                  pl.BlockSpec((None, tc, cb), lambda j, i: (0, i, j)),
                  pl.BlockSpec((None, FHALO, cb), lambda j, i: (0, jnp.maximum(i * hb - 1, 0), j)),
                  pl.BlockSpec((None, FHALO, cb), lambda j, i: (0, nxt(i), j)),
                  pl.BlockSpec((None, tc, cb), lambda j, i: (1, i, j)),
                  pl.BlockSpec((None, FHALO, cb), lambda j, i: (1, nxt(i), j)),
                  pl.BlockSpec((kk, cb), lambda j, i: (0, j)),
                  pl.BlockSpec((1, cb), lambda j, i: (0, j))],
        out_specs=[pl.BlockSpec((2, tc, cb), lambda j, i: (0, i, j)),
                   pl.BlockSpec((SUBLANE, cb), lambda j, i: (0, j)),
                   pl.BlockSpec((1, cb), lambda j, i: (0, j))],
        out_shape=[jax.ShapeDtypeStruct((2, t, dff), BF16), jax.ShapeDtypeStruct((SUBLANE, dff), F32),
                   jax.ShapeDtypeStruct((1, dff), F32)],
        scratch_shapes=[pltpu.VMEM((tc + 2 * FHALO, cb), F32), pltpu.VMEM((te, cb), F32),
                        pltpu.VMEM((kk, SUBLANE, cb), F32), pltpu.VMEM((SUBLANE, cb), F32)],
        compiler_params=_params("parallel", "arbitrary"),
    )(dact, dact, hh3, hh3, hh3, hh3, hh3, fw, fb)


def _chunk_consts():
    r = lax.broadcasted_iota(jnp.int32, (CHUNK, CHUNK), 0)
    c = lax.broadcasted_iota(jnp.int32, (CHUNK, CHUNK), 1)
    blk = (r // SUB) * SUB
    tri = (c <= r).astype(F32)
    start = (c < blk).astype(F32)
    end = (c < blk + SUB).astype(F32)
    return jnp.concatenate([tri, start, end, jnp.ones((SUBLANE, CHUNK), F32)], axis=0)


def _roll8(x, d):
    return pltpu.roll(x.reshape(CHUNK // SUB, SUB, LANE), d % SUB, 1).reshape(CHUNK, LANE)


def _gate_terms(q, fpre, lb):
    sf = _sigmoid(fpre)
    fg = lb + (1.0 - lb) * sf
    sq = _sigmoid(q)
    return sf, fg, 1.0 - fg, sq, q * sq


def _decays(g, consts):
    cs = _dot3(consts, g)
    b = cs[0:CHUNK]
    rs = cs[CHUNK:2 * CHUNK]
    re = cs[2 * CHUNK:3 * CHUNK]
    tot = cs[3 * CHUNK:3 * CHUNK + 1]
    return b, rs, re, tot


def _lower_bound(lb_ref):
    l0, l1 = lb_ref[0:1, :], lb_ref[1:2, :]
    mx = jnp.maximum(l0, l1)
    e0, e1 = jnp.exp(l0 - mx), jnp.exp(l1 - mx)
    return e0 / (e0 + e1)


def _scaled_keys(kt, rs, re, rowblk, i):
    scale = jnp.where(rowblk < i, jnp.exp(jnp.minimum(rs[SUB * i:SUB * i + 1, :] - re, 0.0)), 0.0)
    return kt * scale, scale


def _hgrn_fwd(p3, lb_logits, hg, cat, tb, hpb, rider=None):
    _, t, w = p3.shape
    nh = w // LANE
    nc = tb // CHUNK
    assert nh % hpb == 0

    def body(q_ref, f_ref, v_ref, og_ref, lb_ref, hg_ref, cat_in, cat_ref, o_ref, st_ref, state):
        del cat_in
        consts = _chunk_consts()
        lb_all = _lower_bound(lb_ref)
        rowblk = lax.broadcasted_iota(jnp.int32, (CHUNK, 1), 0) // SUB
        rowpos = lax.broadcasted_iota(jnp.int32, (CHUNK, 1), 0) % SUB

        @pl.when(pl.program_id(1) == 0)
        def _():
            state[...] = jnp.zeros_like(state)

        def chunk(c, carry):
            rows = pl.ds(pl.multiple_of(c * CHUNK, CHUNK), CHUNK)
            heads = range(hpb)
            sls = [slice(j * LANE, (j + 1) * LANE) for j in heads]
            v = [v_ref[rows, s] for s in sls]
            vb = [x.astype(BF16) for x in v]
            gates = [_gate_terms(q_ref[rows, s], f_ref[rows, s], lb_all[:, s]) for s in sls]
            fg = [g[1] for g in gates]
            kk = [g[2] for g in gates]
            qh = [g[4] for g in gates]
            dec = [_decays(jnp.log(x), consts) for x in fg]
            b = [x[0] for x in dec]
            rs = [x[1] for x in dec]
            re = [x[2] for x in dec]
            tot = [x[3] for x in dec]
            qt = [qh[j] * jnp.exp(b[j] - rs[j]) for j in heads]
            kt = [kk[j] * jnp.exp(re[j] - b[j]) for j in heads]
            st = [state[j] for j in heads]
            for j in heads:
                st_ref[j, c] = st[j]
            a = [jnp.zeros((CHUNK, CHUNK), F32) for _ in heads]
            for i in range(1, CHUNK // SUB):
                for j in heads:
                    ki, _ = _scaled_keys(kt[j], rs[j], re[j], rowblk, i)
                    a[j] = a[j] + _dot_nt(jnp.where(rowblk == i, qt[j], 0.0).astype(BF16), ki.astype(BF16))
            o = [_dot(a[j].astype(BF16), vb[j]) for j in heads]
            o = [o[j] + _dot_nt((qh[j] * jnp.exp(b[j])).astype(BF16), st[j].astype(BF16)) for j in heads]
            for j in heads:
                k_up = kk[j] * jnp.exp(tot[j] - b[j])
                state[j] = st[j] * jnp.exp(tot[j]) + _dot_tn(vb[j], k_up.astype(BF16))
            for j in heads:
                e, rf = None, fg[j]
                for d in range(SUB):
                    if d == 0:
                        vs, term = v[j], qh[j] * kk[j]
                    else:
                        e = rf if e is None else e * rf
                        rf = _roll8(fg[j], d)
                        vs = _roll8(v[j], d)
                        term = jnp.where(rowpos >= d, qh[j] * (1.0 - rf) * e, 0.0)
                    o[j] = o[j] + jnp.sum(term, axis=-1, keepdims=True) * vs
            for j in heads:
                og = og_ref[rows, sls[j]]
                o_ref[rows, sls[j]] = o[j]
                r = lax.rsqrt(jnp.mean(o[j] * o[j], axis=-1, keepdims=True) + RMS_EPS)
                cat_ref[rows, sls[j]] = (o[j] * r * hg_ref[:, sls[j]] * (og * _sigmoid(og))).astype(BF16)
            return carry

        lax.fori_loop(0, nc, chunk, 0)

    bw = hpb * LANE
    sec = lambda s: pl.BlockSpec((None, tb, bw), lambda h, i: (s, i, h))
    return _call(
        body, (p3, p3, p3, p3, lb_logits, hg, cat), name="hgrn_fwd", grid=(nh // hpb, t // tb),
        in_specs=[sec(2), sec(3), sec(4), sec(5),
                  pl.BlockSpec((2, bw), lambda h, i: (0, h)),
                  pl.BlockSpec((1, bw), lambda h, i: (0, h)), HBM],
        out_specs=[pl.BlockSpec((tb, bw), lambda h, i: (i, nh // hpb + h)),
                   pl.BlockSpec((tb, bw), lambda h, i: (i, h)),
                   pl.BlockSpec((hpb, nc, LANE, LANE), lambda h, i: (h, i, 0, 0))],
        out_shape=[jax.ShapeDtypeStruct(cat.shape, BF16), jax.ShapeDtypeStruct((t, w), F32),
                   jax.ShapeDtypeStruct((nh, t // CHUNK, LANE, LANE), F32)],
        scratch_shapes=[pltpu.VMEM((hpb, LANE, LANE), F32)], aliases={6: 0}, rider=rider)


def _hgrn_bwd(p3, lb_logits, hg, o_pre, states, dcat, dp3, tb, hpb, rider=None):
    n_sec, t, w = p3.shape
    nh = w // LANE
    assert nh % hpb == 0
    nc = tb // CHUNK
    nb = t // tb
    bw = hpb * LANE
    n_steps = (nh // hpb) * nb

    def body(q_ref, f_ref, v_ref, og_ref, lb_ref, hg_ref, o_ref, st_ref, dc_ref, dp_in,
             dp_ref, dlb_ref, dhg_ref, dstate, stash, lbacc, hgacc, osem):
        del dp_in
        h, i = pl.program_id(0), pl.program_id(1)
        step = h * nb + i
        slot = step % 2

        def out_copy(s, row_blk, lane_blk):
            dst = dp_ref.at[pl.ds(2, 4), pl.ds(row_blk * tb, tb), pl.ds(lane_blk * bw, bw)]
            return pltpu.make_async_copy(stash.at[s], dst, osem.at[s])

        @pl.when(step >= 2)
        def _():
            out_copy(slot, 0, 0).wait()

        def compute():
            consts = _chunk_consts()
            rr = lax.broadcasted_iota(jnp.int32, (CHUNK, CHUNK), 0)
            cc = lax.broadcasted_iota(jnp.int32, (CHUNK, CHUNK), 1)
            upper = (cc >= rr).astype(F32)
            lb_all = _lower_bound(lb_ref)
            rowblk = lax.broadcasted_iota(jnp.int32, (CHUNK, 1), 0) // SUB
            rowpos = lax.broadcasted_iota(jnp.int32, (CHUNK, 1), 0) % SUB

            @pl.when(i == 0)
            def _():
                dstate[...] = jnp.zeros_like(dstate)
                lbacc[...] = jnp.zeros_like(lbacc)
                hgacc[...] = jnp.zeros_like(hgacc)

            def head(j, c, rows):
                sl = slice(j * LANE, (j + 1) * LANE)
                lb = lb_all[:, sl]
                hgv = hg_ref[:, sl]
                q = q_ref[rows, sl]
                v = v_ref[rows, sl]
                og = og_ref[rows, sl]
                o = o_ref[rows, sl]
                dcg = dc_ref[rows, sl]
                sf, fg, kk, sq, qh = _gate_terms(q, f_ref[rows, sl], lb)
                b, rs, re, tot = _decays(jnp.log(fg), consts)
                eq = jnp.exp(b - rs)
                ek = jnp.exp(re - b)
                qt = qh * eq
                kt = kk * ek
                e_in = jnp.exp(b)
                e_up = jnp.exp(tot - b)
                e_tot = jnp.exp(tot)
                q_in = (qh * e_in).astype(BF16)
                k_up = (kk * e_up).astype(BF16)
                vb = v.astype(BF16)
                st = st_ref[j, c]
                dst = dstate[j]
                dstb = dst.astype(BF16)
                yield

                sg = _sigmoid(og)
                r = lax.rsqrt(jnp.mean(o * o, axis=-1, keepdims=True) + RMS_EPS)
                ohat = o * r
                d_og = dcg * ohat * hgv * (sg * (1.0 + og * (1.0 - sg)))
                d_on = dcg * (og * sg)
                hgacc[:, sl] += jnp.sum((d_on * ohat).reshape(CHUNK // SUBLANE, SUBLANE, LANE), axis=0)
                d_oh = d_on * hgv
                do = r * (d_oh - ohat * jnp.mean(d_oh * ohat, axis=-1, keepdims=True))
                dob = do.astype(BF16)

                da = _dot_nt(dob, vb)
                yield
                a_off = jnp.zeros((CHUNK, CHUNK), F32)
                dqt = jnp.zeros((CHUNK, LANE), F32)
                dkt = jnp.zeros((CHUNK, LANE), F32)
                for blk in range(1, CHUNK // SUB):
                    ki, scale = _scaled_keys(kt, rs, re, rowblk, blk)
                    kib = ki.astype(BF16)
                    qib = jnp.where(rowblk == blk, qt, 0.0).astype(BF16)
                    dab = jnp.where(rowblk == blk, da, 0.0).astype(BF16)
                    a_off = a_off + _dot_nt(qib, kib)
                    dqt = dqt + _dot(dab, kib)
                    dkt = dkt + _dot_tn(dab, qib) * scale
                    yield
                dqh = dqt * eq
                dk = dkt * ek
                dv = _dot_tn(a_off.astype(BF16), dob)

                dqh = dqh + _dot(dob, st.astype(BF16)) * e_in
                dk = dk + _dot(vb, dstb) * e_up
                dv = dv + _dot_nt(k_up, dstb)
                st_end = st * e_tot + _dot_tn(vb, k_up)
                carry_g = jnp.sum(st_end * dst, axis=0, keepdims=True)
                dstate[j] = dst * e_tot + _dot_tn(dob, q_in)
                yield

                e, rf = None, fg
                for d in range(SUB):
                    if d == 0:
                        a_d = jnp.sum(qh * kk, axis=-1, keepdims=True)
                        da_d = jnp.sum(do * v, axis=-1, keepdims=True)
                        dqh = dqh + da_d * kk
                        dk = dk + da_d * qh
                        dv = dv + a_d * do
                        continue
                    e = rf if e is None else e * rf
                    rf = _roll8(fg, d)
                    em = jnp.where(rowpos >= d, e, 0.0)
                    ks, vs = 1.0 - rf, _roll8(v, d)
                    a_d = jnp.sum(qh * ks * em, axis=-1, keepdims=True)
                    da_d = jnp.sum(do * vs, axis=-1, keepdims=True) * em
                    dqh = dqh + da_d * ks
                    dk = dk + _roll8(da_d * qh, -d)
                    dv = dv + _roll8(a_d * do, -d)
                yield

                dg = _dot3(upper, qh * dqh - kk * dk) + carry_g
                dfg = dg / fg - dk
                lbacc[:, sl] += jnp.sum((dfg * (1.0 - sf)).reshape(CHUNK // SUBLANE, SUBLANE, LANE), axis=0)
                stash[slot, 0, rows, sl] = (dqh * (sq * (1.0 + q * (1.0 - sq)))).astype(BF16)
                stash[slot, 1, rows, sl] = (dfg * (1.0 - lb) * sf * (1.0 - sf)).astype(BF16)
                stash[slot, 2, rows, sl] = dv.astype(BF16)
                stash[slot, 3, rows, sl] = d_og.astype(BF16)

            def chunk(cr, carry):
                c = nc - 1 - cr
                rows = pl.ds(pl.multiple_of(c * CHUNK, CHUNK), CHUNK)
                running = [head(j, c, rows) for j in range(hpb)]
                while running:
                    running = [g for g in running if next(g, StopIteration) is not StopIteration]
                return carry

            lax.fori_loop(0, nc, chunk, 0)

            @pl.when(i == nb - 1)
            def _():
                dlb_ref[...] = jnp.sum(lbacc[...], axis=0, keepdims=True)
                dhg_ref[...] = jnp.sum(hgacc[...], axis=0, keepdims=True)

        compute()
        out_copy(slot, nb - 1 - i, h).start()

        @pl.when(step == n_steps - 1)
        def _():
            out_copy(slot, 0, 0).wait()
            if n_steps >= 2:
                out_copy(1 - slot, 0, 0).wait()

    rev = lambda i: nb - 1 - i
    sec = lambda s: pl.BlockSpec((None, tb, bw), lambda h, i: (s, rev(i), h))
    return _call(
        body, (p3, p3, p3, p3, lb_logits, hg, o_pre, states, dcat, dp3), name="hgrn_bwd", grid=(nh // hpb, nb),
        in_specs=[sec(2), sec(3), sec(4), sec(5),
                  pl.BlockSpec((2, bw), lambda h, i: (0, h)),
                  pl.BlockSpec((1, bw), lambda h, i: (0, h)),
                  pl.BlockSpec((tb, bw), lambda h, i: (rev(i), h)),
                  pl.BlockSpec((hpb, nc, LANE, LANE), lambda h, i: (h, rev(i), 0, 0)),
                  pl.BlockSpec((tb, bw), lambda h, i: (rev(i), nh // hpb + h)), HBM],
        out_specs=[HBM,
                   pl.BlockSpec((1, bw), lambda h, i: (0, h)),
                   pl.BlockSpec((1, bw), lambda h, i: (0, h))],
        out_shape=[jax.ShapeDtypeStruct((n_sec, t, w), BF16), jax.ShapeDtypeStruct((1, w), F32),
                   jax.ShapeDtypeStruct((1, w), F32)],
        scratch_shapes=[pltpu.VMEM((hpb, LANE, LANE), F32), pltpu.VMEM((2, 4, tb, bw), BF16),
                        pltpu.VMEM((SUBLANE, bw), F32), pltpu.VMEM((SUBLANE, bw), F32),
                        pltpu.SemaphoreType.DMA((2,))],
        aliases={9: 0}, rider=rider)


def _place():
    x, y, c = lax.axis_index("x"), lax.axis_index("y"), lax.axis_index("c")
    chips = [(1 - x, y), (x, 1 - y), (1 - x, 1 - y)]
    return x, y, c, chips


def _rows(buf, px, py, pc, part=None):
    half = buf.shape[1] // 2
    if part is None:
        return buf.at[2 * px + py, pl.ds(pc * half, half)]
    lo, hi, n = part
    piece = half // n
    return buf.at[2 * px + py, pl.ds(pc * half + lo * piece, (hi - lo) * piece)]


def _rcopy(src, dst, send, recv, idx, to):
    return pltpu.make_async_remote_copy(src_ref=src, dst_ref=dst, send_sem=send.at[idx], recv_sem=recv.at[idx],
                                        device_id=to, device_id_type=MESH)


def _same(bufs):
    return [jax.ShapeDtypeStruct(b.shape, b.dtype) for b in bufs]


def _ride_gather_ici(bufs, parts=None):
    n = len(bufs)
    parts = parts or [None] * n

    def start(rin, rout, send, recv):
        x, y, c, chips = _place()
        for k in range(n):
            mine = _rows(rout[k], x, y, c, parts[k])
            for j, chip in enumerate(chips):
                _rcopy(mine, mine, send, recv, 3 * k + j, (*chip, c)).start()

    def finish(rin, rout, send, recv):
        x, y, c, chips = _place()
        for k in range(n):
            for j, chip in enumerate(chips):
                theirs = _rows(rout[k], *chip, c, parts[k])
                _rcopy(theirs, theirs, send, recv, 3 * k + j, (x, y, c)).wait_recv()
        for k in range(n):
            mine = _rows(rout[k], x, y, c, parts[k])
            for j in range(3):
                _rcopy(mine, mine, send, recv, 3 * k + j, (x, y, c)).wait_send()

    return _Rider(bufs, _same(bufs), {k: k for k in range(n)}, 3 * n, start, finish)


class _SemView:
    def __init__(self, ref, base):
        self.ref, self.base = ref, base

    @property
    def at(self):
        return self

    def __getitem__(self, idx):
        return self.ref.at[idx + self.base]


def _ride_both(a, b):
    nai, nao = len(a.ins), len(a.outs)

    def start(rin, rout, send, recv):
        a.start(rin[:nai], rout[:nao], send, recv)
        b.start(rin[nai:], rout[nao:], _SemView(send, a.n_sems), _SemView(recv, a.n_sems))

    def finish(rin, rout, send, recv):
        a.finish(rin[:nai], rout[:nao], send, recv)
        b.finish(rin[nai:], rout[nao:], _SemView(send, a.n_sems), _SemView(recv, a.n_sems))

    aliases = dict(a.aliases)
    aliases.update({nai + ri: nao + ro for ri, ro in b.aliases.items()})
    return _Rider(a.ins + b.ins, a.outs + b.outs, aliases, a.n_sems + b.n_sems, start, finish)


def _ride_gather_d2d(bufs):
    n = len(bufs)

    def start(rin, rout, send, recv):
        x, y, c, chips = _place()
        for k in range(n):
            for j, chip in enumerate(chips):
                got = _rows(rout[k], *chip, c)
                _rcopy(got, got, send, recv, 3 * k + j, (x, y, 1 - c)).start()

    def finish(rin, rout, send, recv):
        x, y, c, chips = _place()
        for k in range(n):
            for j, chip in enumerate(chips):
                theirs = _rows(rout[k], *chip, 1 - c)
                _rcopy(theirs, theirs, send, recv, 3 * k + j, (x, y, c)).wait_recv()
        for k in range(n):
            for j, chip in enumerate(chips):
                got = _rows(rout[k], *chip, c)
                _rcopy(got, got, send, recv, 3 * k + j, (x, y, c)).wait_send()

    return _Rider(bufs, _same(bufs), {k: k for k in range(n)}, 3 * n, start, finish)


def _ride_swap(grads):
    n = len(grads)

    def copy(k, rin, rout, send, recv):
        x, y, c, _ = _place()
        half = rin[k].shape[1] // 2
        return _rcopy(rin[k].at[:, pl.ds((1 - c) * half, half)], rout[k], send, recv, k, (x, y, 1 - c))

    def start(rin, rout, send, recv):
        for k in range(n):
            copy(k, rin, rout, send, recv).start()

    def finish(rin, rout, send, recv):
        for k in range(n):
            copy(k, rin, rout, send, recv).wait()

    outs = [jax.ShapeDtypeStruct((g.shape[0], g.shape[1] // 2, g.shape[2]), g.dtype) for g in grads]
    return _Rider(grads, outs, {}, n, start, finish)


def _ride_send_partials(parts, pieces=None, into=None):
    n = len(parts)
    pieces = pieces or [None] * n

    def cut(ref, k):
        if pieces[k] is None:
            return ref
        lo, hi, m = pieces[k]
        q = ref.shape[0] // m
        return ref.at[pl.ds(lo * q, (hi - lo) * q)]

    def copies(rin, rout, send, recv):
        x, y, c, chips = _place()
        return [_rcopy(cut(rin[k].at[2 * px + py], k), cut(rout[k].at[j], k), send, recv, 3 * k + j, (px, py, c))
                for k in range(n) for j, (px, py) in enumerate(chips)]

    def start(rin, rout, send, recv):
        for cp in copies(rin, rout, send, recv):
            cp.start()

    def finish(rin, rout, send, recv):
        for cp in copies(rin, rout, send, recv):
            cp.wait()

    if into is None:
        outs = [jax.ShapeDtypeStruct((3,) + p.shape[1:], p.dtype) for p in parts]
        return _Rider(parts, outs, {}, 3 * n, start, finish)
    return _Rider(list(parts) + list(into), _same(into), {n + k: k for k in range(n)}, 3 * n, start, finish)


def _ride_join(bufs):
    n = len(bufs)

    def half_of(buf, pc):
        half = buf.shape[0] // 2
        return buf.at[pl.ds(pc * half, half)]

    def start(rin, rout, send, recv):
        x, y, c, _ = _place()
        for k in range(n):
            mine = half_of(rout[k], c)
            _rcopy(mine, mine, send, recv, k, (x, y, 1 - c)).start()

    def finish(rin, rout, send, recv):
        x, y, c, _ = _place()
        for k in range(n):
            mine, theirs = half_of(rout[k], c), half_of(rout[k], 1 - c)
            _rcopy(mine, mine, send, recv, k, (x, y, c)).wait_send()
            _rcopy(theirs, theirs, send, recv, k, (x, y, c)).wait_recv()

    return _Rider(bufs, _same(bufs), {k: k for k in range(n)}, n, start, finish)


def _run(name, rider):
    def body(*refs):
        nri, nro = len(rider.ins), len(rider.outs)
        rin, rout = refs[:nri], refs[nri:nri + nro]
        send, recv = refs[nri + nro:]
        rider.start(rin, rout, send, recv)
        rider.finish(rin, rout, send, recv)

    return pl.pallas_call(
        body, name=name, in_specs=[HBM] * len(rider.ins), out_specs=[HBM] * len(rider.outs), out_shape=rider.outs,
        scratch_shapes=[pltpu.SemaphoreType.DMA((rider.n_sems,)), pltpu.SemaphoreType.DMA((rider.n_sems,))],
        input_output_aliases=rider.aliases,
    )(*rider.ins)


def _add_halves(name, g, other, c_idx):
    s, r, cols = g.shape
    half = r // 2
    tr = _div_tile(half, 16, 512)
    nb = half // tr

    def body(c_ref, g_ref, o_ref, q_ref):
        del c_ref
        q_ref[...] = (g_ref[...] + o_ref[...]).astype(BF16)

    return pl.pallas_call(
        body, name=name,
        grid_spec=pltpu.PrefetchScalarGridSpec(
            num_scalar_prefetch=1, grid=(s, nb),
            in_specs=[pl.BlockSpec((None, tr, cols), lambda k, i, c: (k, c[0] * nb + i, 0)),
                      pl.BlockSpec((None, tr, cols), lambda k, i, c: (k, i, 0))],
            out_specs=pl.BlockSpec((None, tr, cols), lambda k, i, c: (k, i, 0))),
        out_shape=jax.ShapeDtypeStruct((s, half, cols), BF16),
        compiler_params=_params("parallel", "parallel"),
    )(c_idx, g, other)


def _sum_partials(name, part, arrived, place_idx):
    _, half, cols = part.shape
    tr = _div_tile(half, 16, 512)
    nb = half // tr

    def body(s_ref, p_ref, a_ref, o_ref):
        del s_ref
        o_ref[...] = ((p_ref[...].astype(F32) + a_ref[0].astype(F32)) + a_ref[1].astype(F32)) + a_ref[2].astype(F32)

    return pl.pallas_call(
        body, name=name,
        grid_spec=pltpu.PrefetchScalarGridSpec(
            num_scalar_prefetch=1, grid=(nb,),
            in_specs=[pl.BlockSpec((None, tr, cols), lambda i, s: (s[0], i, 0)),
                      pl.BlockSpec((3, tr, cols), lambda i, s: (0, i, 0))],
            out_specs=pl.BlockSpec((tr, cols), lambda i, s: (s[1] * nb + i, 0))),
        out_shape=jax.ShapeDtypeStruct((2 * half, cols), F32),
        compiler_params=_params("parallel"),
    )(place_idx, part, arrived)


def _small_allreduce(wide_rows, ffn_rows, w, dff, n_wide, n_ffn):
    n_in = len(wide_rows) + len(ffn_rows)

    def body(*refs):
        ins = refs[:n_in]
        s1_ref, s2_ref, r1, r2, p1, p2, send, recv = refs[n_in:]
        x, y, c, _ = _place()
        me = 4 * x + 2 * y + c
        p1[...] = jnp.zeros_like(p1)
        p2[...] = jnp.zeros_like(p2)
        row = 0
        for ref, (_, r, m) in zip(ins, wide_rows):
            if m == 1 and r % SUBLANE == 0 and row % SUBLANE == 0:
                p1[row:row + r, :] = ref[...]
                row += r
                continue
            for rr in range(r):
                for mm in range(m):
                    p1[row:row + 1, :] = ref[rr:rr + 1, mm * w:(mm + 1) * w]
                    row += 1
        row = 0
        for ref, arr in zip(ins[len(wide_rows):], ffn_rows):
            r = arr.shape[0]
            p2[row:row + r, :] = ref[...]
            row += r
        r1[me] = p1[...]
        r2[me] = p2[...]
        cps = []
        for mask in range(1, 8):
            peer = (x ^ (mask >> 2), y ^ ((mask >> 1) & 1), c ^ (mask & 1))
            for a, (src, dst) in enumerate(((p1, r1), (p2, r2))):
                cp = pltpu.make_async_remote_copy(
                    src_ref=src, dst_ref=dst.at[me], send_sem=send.at[a, mask - 1], recv_sem=recv.at[a, mask - 1],
                    device_id=peer, device_id_type=MESH)
                cp.start()
                cps.append(cp)
        for cp in cps:
            cp.wait()
        t1, t2 = r1[0], r2[0]
        for d in range(1, 8):
            t1 = t1 + r1[d]
            t2 = t2 + r2[d]
        s1_ref[...] = t1
        s2_ref[...] = t2

    ins = [a for a, _, _ in wide_rows] + list(ffn_rows)
    return pl.pallas_call(
        body, name="small_allreduce", in_specs=[VMEM_FULL] * n_in, out_specs=[VMEM_FULL, VMEM_FULL],
        out_shape=[jax.ShapeDtypeStruct((n_wide, w), F32), jax.ShapeDtypeStruct((n_ffn, dff), F32)],
        scratch_shapes=[pltpu.VMEM((8, n_wide, w), F32), pltpu.VMEM((8, n_ffn, dff), F32),
                        pltpu.VMEM((n_wide, w), F32), pltpu.VMEM((n_ffn, dff), F32),
                        pltpu.SemaphoreType.DMA((2, 7)), pltpu.SemaphoreType.DMA((2, 7))],
        compiler_params=pltpu.CompilerParams(vmem_limit_bytes=VMEM_LIMIT),
    )(*ins)


def _adamw(w, g, m, v):
    m2 = ADAM_B1 * m + (1.0 - ADAM_B1) * g
    v2 = ADAM_B2 * v + (1.0 - ADAM_B2) * (g * g)
    m_hat = m2 / (1.0 - ADAM_B1 ** ADAM_STEP)
    v_hat = v2 / (1.0 - ADAM_B2 ** ADAM_STEP)
    delta = -ADAM_LR * (m_hat / (jnp.sqrt(v_hat) + ADAM_EPS) + ADAM_WD * w)
    return delta, m2, v2


def _adam_big(name, w, g, m, v):
    r, c = w.shape
    tr = 128 if r % 128 == 0 else r

    def body(w_ref, g_ref, m_ref, v_ref, go_ref, d_ref, m2_ref, v2_ref):
        g = g_ref[...]
        go_ref[...] = g
        d_ref[...], m2_ref[...], v2_ref[...] = _adamw(w_ref[...], g, m_ref[...], v_ref[...])

    blk = pl.BlockSpec((tr, c), lambda i: (i, 0))
    return _call(
        body, (w, g, m, v), name=name, grid=(r // tr,), in_specs=[blk] * 4, out_specs=[blk] * 4,
        out_shape=[jax.ShapeDtypeStruct((r, c), F32)] * 4)


def _adam_small(s1, s2, cw_g, fw_g, lb_logits, triples, layout, w):
    n = len(triples)

    def body(*refs):
        s1_ref, s2_ref, cw_ref, fw_ref, lbl_ref = refs[:5]
        prm = refs[5:5 + 3 * n]
        outs = refs[5 + 3 * n:]
        for p, lay in enumerate(layout):
            w_ref, m_ref, v_ref = prm[3 * p:3 * p + 3]
            g_ref, d_ref, m2_ref, v2_ref = outs[4 * p:4 * p + 4]
            if lay[0] == "wide":
                _, row, r, pieces = lay
                for rr in range(r):
                    for mm in range(pieces):
                        g_ref[rr:rr + 1, mm * w:(mm + 1) * w] = s1_ref[row:row + 1, :]
                        row += 1
            elif lay[0] == "ffn":
                _, row, r = lay
                g_ref[...] = s2_ref[row:row + r, :]
            elif lay[0] == "cw":
                g_ref[...] = cw_ref[0:g_ref.shape[0], :]
            elif lay[0] == "fw":
                g_ref[...] = fw_ref[0:g_ref.shape[0], :]
            else:
                s0 = _lower_bound(lbl_ref)
                d0 = s1_ref[lay[1]:lay[1] + 1, :] * s0 * (1.0 - s0)
                g_ref[0:1, :] = d0
                g_ref[1:2, :] = -d0
            d_ref[...], m2_ref[...], v2_ref[...] = _adamw(w_ref[...], g_ref[...], m_ref[...], v_ref[...])

    flat = [a for tr in triples for a in tr]
    shapes = []
    for tr in triples:
        shapes.extend([jax.ShapeDtypeStruct(tr[0].shape, F32)] * 4)
    return pl.pallas_call(
        body, name="adam_small", in_specs=[VMEM_FULL] * (5 + 3 * n), out_specs=[VMEM_FULL] * (4 * n),
        out_shape=shapes, compiler_params=pltpu.CompilerParams(vmem_limit_bytes=VMEM_LIMIT),
    )(s1, s2, cw_g, fw_g, lb_logits, *flat)


def _row_tile(t):
    return 512 if t % 512 == 0 and t >= 2048 else 128


def kernel(x, emb_ln_g, emb_ln_b, w_in, conv_w, conv_b, conv_norm_g, conv_norm_b, lb_logits, hgrn_norm_g, w_out, ln1_g, ln1_b, w_ffn_up, ffn_conv_w, ffn_conv_b, w_ffn_down, ln2_g, ln2_b, loss_target, m_emb_ln_g, m_emb_ln_b, m_w_in, m_conv_w, m_conv_b, m_conv_norm_g, m_conv_norm_b, m_lb_logits, m_hgrn_norm_g, m_w_out, m_ln1_g, m_ln1_b, m_w_ffn_up, m_ffn_conv_w, m_ffn_conv_b, m_w_ffn_down, m_ln2_g, m_ln2_b, v_emb_ln_g, v_emb_ln_b, v_w_in, v_conv_w, v_conv_b, v_conv_norm_g, v_conv_norm_b, v_lb_logits, v_hgrn_norm_g, v_w_out, v_ln1_g, v_ln1_b, v_w_ffn_up, v_ffn_conv_w, v_ffn_conv_b, v_w_ffn_down, v_ln2_g, v_ln2_b):
    depth = w_in.shape[0]
    assert depth == 1 and x.shape[0] == 1
    alpha = (2.0 * depth) ** 0.25
    t, d = x.shape[1], x.shape[2]
    w = d // 2
    dff = ffn_conv_b.shape[1]
    kc = conv_w.shape[1]
    assert w % (2 * LANE) == 0 and dff % (4 * LANE) == 0 and t % 128 == 0
    tm = _row_tile(t)
    tm2 = tm // 2
    tmm = 1024 if t % 1024 == 0 and t >= 2048 else tm
    cb = 2 * LANE
    cbf = 4 * LANE
    tb = tm
    nh = w // LANE
    hpb = 4 if nh % 4 == 0 else 2

    xi = lax.axis_index("x")
    yi = lax.axis_index("y")
    ci = lax.axis_index("c")
    chip = 2 * xi + yi
    c_idx = jnp.reshape(ci, (1,)).astype(jnp.int32)
    chip_idx = jnp.reshape(chip, (1,)).astype(jnp.int32)
    place_idx = jnp.stack([chip, ci]).astype(jnp.int32)

    x2 = x[0]
    tgt = loss_target[0]
    g0, b0 = emb_ln_g.reshape(1, d), emb_ln_b.reshape(1, d)
    w_in2, w_out2, w_up2, w_dn2 = w_in[0], w_out[0], w_ffn_up[0], w_ffn_down[0]
    cw2, fw2 = conv_w[0], ffn_conv_w[0]

    b_in = _place_shard(w_in2, "place_w_in", chip_idx, BF16)
    b_out = _place_shard(w_out2, "place_w_out", chip_idx, BF16)
    b_up = _place_shard(w_up2, "place_w_up", chip_idx, BF16)
    b_dn = _place_shard(w_dn2, "place_w_down", chip_idx, BF16)
    b_cw = _place_shard(_pad_rows(cw2), "place_conv_w", chip_idx, F32)
    b_fw = _place_shard(_pad_rows(fw2), "place_ffn_conv_w", chip_idx, F32)
    h0b, (b_in,) = _ln0(x2, g0, b0, tm, rider=_ride_gather_ici([b_in], [(0, 1, 8)]))
    first = _run("gather_first_ici", _ride_gather_ici([b_in, b_cw, b_fw], [(1, 8, 8), None, None]))
    w_in3, cw_full3, fw_full3 = _run("gather_first_d2d", _ride_gather_d2d(first))
    cw_full = _unshard_cols(cw_full3)[:kc]
    fw_full = _unshard_cols(fw_full3)[:fw2.shape[0]]

    p3, (b_out, b_up) = _proj("in_proj", h0b, w_in3, 6, 2 * tmm if t % (2 * tmm) == 0 else tmm, w // 2,
                              rider=_ride_gather_ici([b_out, b_up], [None, (0, 1, 4)]))
    (cat, u1), (w_out3, b_up) = _conv_fwd(
        p3, cw_full, conv_b, conv_norm_g, conv_norm_b, tm2, cb,
        rider=_ride_both(_ride_gather_d2d([b_out]), _ride_gather_ici([b_up], [(1, 2, 4)])))
    w_out_full = w_out3.reshape(d, d)
    (cat, o_pre, states), got = _hgrn_fwd(p3, lb_logits, hgrn_norm_g, cat, tb, hpb,
                                          rider=_ride_gather_ici([b_up], [(2, 4, 4)]))
    (xhat1, h1b, rstd1), (w_up3,) = _mix_ln1(cat, w_out_full, x2, g0, b0, ln1_g, ln1_b, alpha, tm2,
                                             rider=_ride_gather_d2d(got))
    hh3, got = _proj("ffn_up", h1b, w_up3, 2, tmm, dff // 4, rider=_ride_gather_ici([b_dn]))
    act, (w_dn3,) = _ffn_act_fwd(hh3, fw_full, ffn_conv_b, tm, cbf, rider=_ride_gather_d2d(got))
    ks = dff // N_CHIPS
    ffn = _wgrad("ffn_down", act, w_dn3, (t, d), (t // tmm, 1, N_CHIPS),
                 pl.BlockSpec((tmm, ks), lambda i, j, k: (i, k)),
                 pl.BlockSpec((None, ks, d), lambda i, j, k: (k, 0, 0)),
                 pl.BlockSpec((tmm, d), lambda i, j, k: (i, 0)), dot=_dot)
    dz2, dz2b, dg2, db2, loss_row = _ln2_loss(ffn, xhat1, tgt, ln1_g, ln1_b, ln2_g, ln2_b, alpha, tm2)

    dact = _proj_t("ffn_down_t", dz2b, w_dn3.reshape(dff, d), tmm, ks)
    dhh3, dfw, dfb = _ffn_act_bwd(dact, hh3, fw_full, ffn_conv_b, tm, cbf)
    tt = 2 * tmm if t % (2 * tmm) == 0 else tmm
    d_w_dn = _wgrad("wgrad_down", act, dz2b, (N_CHIPS, ks, d), (N_CHIPS, 2, t // tt),
                    pl.BlockSpec((tt, ks), lambda s, j, k: (k, s)),
                    pl.BlockSpec((tt, d // 2), lambda s, j, k: (k, j)),
                    pl.BlockSpec((None, ks, d // 2), lambda s, j, k: (s, 0, j)))
    wu = 2 * dff // N_CHIPS
    tnu = wu // 2
    per_sec_u = dff // tnu
    pre1, (arr_dn,) = _wgrad(
        "up_t", dhh3, w_up3, (t, d), (t // tmm, 1, 2 * N_CHIPS),
        pl.BlockSpec((None, tmm, tnu), lambda i, j, k: (k // per_sec_u, i, k % per_sec_u)),
        pl.BlockSpec((None, d, tnu), lambda i, j, k: (k // 2, 0, k % 2)),
        pl.BlockSpec((tmm, d), lambda i, j, k: (i, 0)), dot=_dot_nt, rider=_ride_swap([d_w_dn]))
    dz1, dz1b, dg1, db1 = _ln1_bwd(pre1, dz2, xhat1, rstd1, ln1_g, alpha, tm2)
    part_dn = _add_halves("add_halves_w_down", d_w_dn, arr_dn, c_idx)
    d_w_up, (land_dn,) = _wgrad(
        "wgrad_up", h1b, dhh3, (N_CHIPS, d, wu), (N_CHIPS, 2, 2, t // tt),
        pl.BlockSpec((tt, d // 2), lambda s, r, j, k: (k, r)),
        pl.BlockSpec((None, tt, tnu), lambda s, r, j, k: ((2 * s + j) // per_sec_u, k, (2 * s + j) % per_sec_u)),
        pl.BlockSpec((None, d // 2, tnu), lambda s, r, j, k: (s, r, j)), rider=_ride_send_partials([part_dn]))
    dcat = _proj_t("out_proj_t", dz1b, w_out_full, tmm, d // 2)
    d_w_out = _wgrad("wgrad_out", cat, dz1b, (d, d), (2, 2, t // tt),
                     pl.BlockSpec((tt, d // 2), lambda r, j, k: (k, r)),
                     pl.BlockSpec((tt, d // 2), lambda r, j, k: (k, j)),
                     pl.BlockSpec((d // 2, d // 2), lambda r, j, k: (r, j))).reshape(N_CHIPS, d // N_CHIPS, d)
    du1, dcng, dcnb = _conv_norm_bwd(dcat, u1, conv_norm_g, conv_norm_b, tm)
    (dp3, dcw, dcb), (arr_up, arr_out) = _conv_bwd(du1, p3, cw_full, tm2, cb, rider=_ride_swap([d_w_up, d_w_out]))
    part_up = _add_halves("add_halves_w_up", d_w_up, arr_up, c_idx)
    part_out = _add_halves("add_halves_w_out", d_w_out, arr_out, c_idx)
    (dp3, dlb, dhg), (land_up, land_out) = _hgrn_bwd(
        p3, lb_logits, hgrn_norm_g, o_pre, states, dcat, dp3, tb, hpb,
        rider=_ride_send_partials([part_up, part_out], [(0, 3, 4), None]))
    d_w_in, (land_up,) = _wgrad_in(h0b, dp3, N_CHIPS, tt,
                                   rider=_ride_send_partials([part_up], [(3, 4, 4)], into=[land_up]))
    (arr_in,) = _run("swap_w_in", _ride_swap([d_w_in]))
    part_in = _add_halves("add_halves_w_in", d_w_in, arr_in, c_idx)
    h_out, h_up, h_dn = [
        _sum_partials("sum_partials_" + nm, p, a, place_idx)
        for nm, p, a in (("w_out", part_out, land_out), ("w_up", part_up, land_up), ("w_down", part_dn, land_dn))]
    pre0, (land_in, g_w_out, g_w_up, g_w_dn) = _in_t(
        dp3, w_in3, tmm, rider=_ride_both(_ride_send_partials([part_in]), _ride_join([h_out, h_up, h_dn])))
    dx, dg0, db0 = _ln0_bwd(pre0, dz1, x2, g0, alpha, tm2)
    h_in = _sum_partials("sum_partials_w_in", part_in, land_in, place_idx)
    (g_w_in,) = _run("join_w_in", _ride_join([h_in]))

    kpad = dcw.shape[0]
    wide = [(dcw, kpad, 1), (dg0, 1, 2), (db0, 1, 2), (dg1, 1, 2), (db1, 1, 2), (dg2, 1, 2), (db2, 1, 2),
            (dcb, 1, 1), (dcng, 1, 1), (dcnb, 1, 1), (dlb, 1, 1), (dhg, 1, 1)]
    n_wide = sum(r * m for _, r, m in wide)
    n_wide_pad = -(-n_wide // SUBLANE) * SUBLANE
    s1, s2 = _small_allreduce(wide, [dfw, dfb], w, dff, n_wide_pad, 2 * SUBLANE)
    cw_g = lax.dynamic_slice_in_dim(s1[0:kpad], chip * (w // N_CHIPS), w // N_CHIPS, axis=1)
    fw_g = lax.dynamic_slice_in_dim(s2[0:SUBLANE], chip * (dff // N_CHIPS), dff // N_CHIPS, axis=1)

    small = [
        (g0, m_emb_ln_g.reshape(1, d), v_emb_ln_g.reshape(1, d)), (b0, m_emb_ln_b.reshape(1, d), v_emb_ln_b.reshape(1, d)),
        (cw2, m_conv_w[0], v_conv_w[0]), (conv_b, m_conv_b, v_conv_b),
        (conv_norm_g, m_conv_norm_g, v_conv_norm_g), (conv_norm_b, m_conv_norm_b, v_conv_norm_b),
        (lb_logits, m_lb_logits, v_lb_logits), (hgrn_norm_g, m_hgrn_norm_g, v_hgrn_norm_g),
        (ln1_g, m_ln1_g, v_ln1_g), (ln1_b, m_ln1_b, v_ln1_b),
        (fw2, m_ffn_conv_w[0], v_ffn_conv_w[0]), (ffn_conv_b, m_ffn_conv_b, v_ffn_conv_b),
        (ln2_g, m_ln2_g, v_ln2_g), (ln2_b, m_ln2_b, v_ln2_b),
    ]
    r0 = kpad
    layout = [("wide", r0, 1, 2), ("wide", r0 + 2, 1, 2), ("cw",), ("wide", r0 + 12, 1, 1), ("wide", r0 + 13, 1, 1),
              ("wide", r0 + 14, 1, 1), ("lb", r0 + 15), ("wide", r0 + 16, 1, 1), ("wide", r0 + 4, 1, 2),
              ("wide", r0 + 6, 1, 2), ("fw",), ("ffn", SUBLANE, 1), ("wide", r0 + 8, 1, 2), ("wide", r0 + 10, 1, 2)]
    so = _adam_small(s1, s2, cw_g, fw_g, lb_logits, small, layout, w)
    sm = {nm: so[4 * i:4 * i + 4] for i, nm in enumerate(
        ["emb_ln_g", "emb_ln_b", "conv_w", "conv_b", "conv_norm_g", "conv_norm_b", "lb_logits", "hgrn_norm_g",
         "ln1_g", "ln1_b", "ffn_conv_w", "ffn_conv_b", "ln2_g", "ln2_b"])}
    bigs = {}
    for nm, wt, g, m, v in (("w_in", w_in2, g_w_in, m_w_in[0], v_w_in[0]), ("w_out", w_out2, g_w_out, m_w_out[0], v_w_out[0]),
                            ("w_ffn_up", w_up2, g_w_up, m_w_ffn_up[0], v_w_ffn_up[0]),
                            ("w_ffn_down", w_dn2, g_w_dn, m_w_ffn_down[0], v_w_ffn_down[0])):
        bigs[nm] = tuple(_adam_big("adam_" + nm, wt, g, m, v))

    loss = lax.psum(loss_row[0, 0], ("x", "y", "c"))

    order = ["emb_ln_g", "emb_ln_b", "w_in", "conv_w", "conv_b", "conv_norm_g", "conv_norm_b", "lb_logits",
             "hgrn_norm_g", "w_out", "ln1_g", "ln1_b", "w_ffn_up", "ffn_conv_w", "ffn_conv_b", "w_ffn_down",
             "ln2_g", "ln2_b"]
    shapes = dict(emb_ln_g=emb_ln_g.shape, emb_ln_b=emb_ln_b.shape, w_in=w_in.shape, conv_w=conv_w.shape,
                  w_out=w_out.shape, w_ffn_up=w_ffn_up.shape, ffn_conv_w=ffn_conv_w.shape, w_ffn_down=w_ffn_down.shape)
    outs = [loss, dx.reshape(x.shape)]
    for which in range(4):
        for nm in order:
            a = bigs[nm][which] if nm in bigs else sm[nm][which]
            outs.append(a.reshape(shapes[nm]) if nm in shapes else a)
    return tuple(outs)


def _pad_rows(a):
    k = a.shape[0]
    kp = -(-k // 16) * 16
    return jnp.pad(a, ((0, kp - k), (0, 0)))


def _unshard_cols(a3):
    s, k, c = a3.shape
    return jnp.transpose(a3, (1, 0, 2)).reshape(k, s * c)
```

```python
import functools

import jax
import jax.numpy as jnp
from jax import lax
from jax.experimental import pallas as pl
from jax.experimental.pallas import tpu as pltpu

F32 = jnp.float32
BF16 = jnp.bfloat16

LN_EPS = 1e-5
RMS_EPS = 1e-6
LANE = 128
SUBLANE = 8
CHUNK = 64
SUB = 8
HALO = 32
FHALO = 8
ROWS = 64
N_CHIPS = 4
VMEM_LIMIT = 56 << 20
NEG_BIG = -1e30

ADAM_LR = 0.001
ADAM_B1 = 0.9
ADAM_B2 = 0.999
ADAM_EPS = 1e-08
ADAM_WD = 0.01
ADAM_STEP = 10

MESH = pl.DeviceIdType.MESH
HBM = pl.BlockSpec(memory_space=pl.ANY)
VMEM_FULL = pl.BlockSpec(memory_space=pltpu.VMEM)


def _params(*sem):
    return pltpu.CompilerParams(dimension_semantics=sem, vmem_limit_bytes=VMEM_LIMIT)


class _Rider:
    def __init__(self, ins, outs, aliases, n_sems, start, finish):
        self.ins, self.outs, self.aliases = list(ins), list(outs), dict(aliases)
        self.n_sems, self.start, self.finish = n_sems, start, finish


def _call(body, args, *, name, grid, in_specs, out_specs, out_shape, scratch_shapes=(), aliases=None, rider=None):
    many = isinstance(out_shape, (list, tuple))
    shapes = list(out_shape) if many else [out_shape]
    ospecs = list(out_specs) if many else [out_specs]
    aliases = dict(aliases or {})
    sem = ("arbitrary",) * len(grid)
    if rider is None:
        res = pl.pallas_call(
            body, name=name, grid=grid, in_specs=list(in_specs), out_specs=ospecs, out_shape=shapes,
            scratch_shapes=list(scratch_shapes), input_output_aliases=aliases, compiler_params=_params(*sem))(*args)
        return res if many else res[0]
    n_in, n_out, n_scr = len(args), len(shapes), len(scratch_shapes)
    nri, nro = len(rider.ins), len(rider.outs)

    def wrapped(*refs):
        ins, rin = refs[:n_in], refs[n_in:n_in + nri]
        o0 = n_in + nri
        outs, rout = refs[o0:o0 + n_out], refs[o0 + n_out:o0 + n_out + nro]
        s0 = o0 + n_out + nro
        scr, (send, recv) = refs[s0:s0 + n_scr], refs[s0 + n_scr:]
        ids = [pl.program_id(a) for a in range(len(grid))]
        first = functools.reduce(jnp.logical_and, [i == 0 for i in ids])
        last = functools.reduce(jnp.logical_and, [i == g - 1 for i, g in zip(ids, grid)])

        @pl.when(first)
        def _():
            rider.start(rin, rout, send, recv)

        body(*ins, *outs, *scr)

        @pl.when(last)
        def _():
            rider.finish(rin, rout, send, recv)

    for ri, ro in rider.aliases.items():
        aliases[n_in + ri] = n_out + ro
    res = pl.pallas_call(
        wrapped, name=name, grid=grid, in_specs=list(in_specs) + [HBM] * nri, out_specs=ospecs + [HBM] * nro,
        out_shape=shapes + rider.outs,
        scratch_shapes=list(scratch_shapes) + [pltpu.SemaphoreType.DMA((rider.n_sems,)),
                                               pltpu.SemaphoreType.DMA((rider.n_sems,))],
        input_output_aliases=aliases, compiler_params=_params(*sem))(*args, *rider.ins)
    main, extra = res[:n_out], list(res[n_out:])
    return (list(main) if many else main[0]), extra


def _div_tile(n, mult, cap):
    best = n
    for t in range(mult, min(n, cap) + 1, mult):
        if n % t == 0:
            best = t
    return best


def _sigmoid(x):
    return 1.0 / (1.0 + jnp.exp(-x))


def _ln_stats(x):
    mu = jnp.mean(x, axis=-1, keepdims=True)
    xc = x - mu
    var = jnp.mean(xc * xc, axis=-1, keepdims=True)
    rstd = lax.rsqrt(var + LN_EPS)
    return xc * rstd, rstd


def _ln_bwd(dy, xhat, rstd, g):
    dyg = dy * g
    m1 = jnp.mean(dyg, axis=-1, keepdims=True)
    m2 = jnp.mean(dyg * xhat, axis=-1, keepdims=True)
    return rstd * (dyg - m1 - xhat * m2)


def _dot_nt(a, b):
    return lax.dot_general(a, b, (((1,), (1,)), ((), ())), preferred_element_type=F32)


def _dot_tn(a, b):
    return lax.dot_general(a, b, (((0,), (0,)), ((), ())), preferred_element_type=F32)


def _dot(a, b):
    return jnp.dot(a, b, preferred_element_type=F32)


def _dot3(m, x):
    mb = m.astype(BF16)
    x1 = x.astype(BF16)
    r1 = x - x1.astype(F32)
    x2 = r1.astype(BF16)
    x3 = (r1 - x2.astype(F32)).astype(BF16)
    return _dot(mb, x1) + _dot(mb, x2) + _dot(mb, x3)


def _place_shard(x, name, chip_idx, dtype):
    r, c = x.shape
    tr = _div_tile(r, 16, 512)

    def body(s_ref, x_ref, o_ref):
        del s_ref
        o_ref[...] = x_ref[...].astype(dtype)

    return pl.pallas_call(
        body, name=name,
        grid_spec=pltpu.PrefetchScalarGridSpec(
            num_scalar_prefetch=1, grid=(r // tr,),
            in_specs=[pl.BlockSpec((tr, c), lambda i, s: (i, 0))],
            out_specs=pl.BlockSpec((None, tr, c), lambda i, s: (s[0], i, 0))),
        out_shape=jax.ShapeDtypeStruct((N_CHIPS, r, c), dtype),
        compiler_params=_params("parallel"),
    )(chip_idx, x)


def _ln0(x, g, b, tm, rider=None):
    t, d = x.shape

    def body(x_ref, g_ref, b_ref, o_ref):
        xh, _ = _ln_stats(x_ref[...])
        o_ref[...] = (xh * g_ref[...] + b_ref[...]).astype(BF16)

    row = pl.BlockSpec((1, d), lambda i: (0, 0))
    return _call(
        body, (x, g, b), name="ln0", grid=(t // tm,),
        in_specs=[pl.BlockSpec((tm, d), lambda i: (i, 0)), row, row],
        out_specs=pl.BlockSpec((tm, d), lambda i: (i, 0)),
        out_shape=jax.ShapeDtypeStruct((t, d), BF16), rider=rider)


def _proj(name, a, w3, n_sec, tm, tn, rider=None):
    m, k = a.shape
    s, _, ws = w3.shape
    sec_w = s * ws // n_sec
    nj = ws // tn
    per_sec = sec_w // tn

    def body(a_ref, w_ref, o_ref):
        o_ref[...] = _dot(a_ref[...], w_ref[...])

    return _call(
        body, (a, w3), name=name, grid=(s * nj, m // tm),
        in_specs=[pl.BlockSpec((tm, k), lambda j, i: (i, 0)),
                  pl.BlockSpec((None, k, tn), lambda j, i: (j // nj, 0, j % nj))],
        out_specs=pl.BlockSpec((None, tm, tn), lambda j, i: (j // per_sec, i, j % per_sec)),
        out_shape=jax.ShapeDtypeStruct((n_sec, m, sec_w), F32), rider=rider)


def _proj_t(name, a, w, tm, tn, rider=None):
    m, k = a.shape
    n = w.shape[0]

    def body(a_ref, w_ref, o_ref):
        o_ref[...] = _dot_nt(a_ref[...], w_ref[...])

    return _call(
        body, (a, w), name=name, grid=(n // tn, m // tm),
        in_specs=[pl.BlockSpec((tm, k), lambda j, i: (i, 0)),
                  pl.BlockSpec((tn, k), lambda j, i: (j, 0))],
        out_specs=pl.BlockSpec((tm, tn), lambda j, i: (i, j)),
        out_shape=jax.ShapeDtypeStruct((m, n), F32), rider=rider)


def _wgrad(name, a, b, out_shape, grid, a_spec, b_spec, o_spec, rider=None, dot=_dot_tn):
    nt = len(grid) - 1

    def body(a_ref, b_ref, o_ref):
        t = pl.program_id(nt)
        prod = dot(a_ref[...], b_ref[...])

        @pl.when(t == 0)
        def _():
            o_ref[...] = prod

        @pl.when(t > 0)
        def _():
            o_ref[...] += prod

    return _call(
        body, (a, b), name=name, grid=grid, in_specs=[a_spec, b_spec], out_specs=o_spec,
        out_shape=jax.ShapeDtypeStruct(out_shape, F32), rider=rider)


def _wgrad_in(h0b, dp3, n_shards, tt, rider=None):
    t, d = h0b.shape
    n_sec, _, sec_w = dp3.shape
    ws = n_sec * sec_w // n_shards
    tn = sec_w // 2
    nq = ws // tn
    tr = d // 2

    def body(*refs):
        a_ref, b_refs, o_ref = refs[0], refs[1:1 + nq], refs[1 + nq]
        k = pl.program_id(2)
        a = a_ref[...]
        prods = [_dot_tn(a, b_ref[...]) for b_ref in b_refs]

        @pl.when(k == 0)
        def _():
            for q in range(nq):
                o_ref[:, q * tn:(q + 1) * tn] = prods[q]

        @pl.when(k > 0)
        def _():
            for q in range(nq):
                o_ref[:, q * tn:(q + 1) * tn] += prods[q]

    def b_spec(q):
        return pl.BlockSpec((None, tt, tn), lambda s, r, k: ((nq * s + q) // 2, k, (nq * s + q) % 2))

    return _call(
        body, (h0b,) + (dp3,) * nq, name="wgrad_in", grid=(n_shards, d // tr, t // tt),
        in_specs=[pl.BlockSpec((tt, tr), lambda s, r, k: (k, r))] + [b_spec(q) for q in range(nq)],
        out_specs=pl.BlockSpec((None, tr, ws), lambda s, r, k: (s, r, 0)),
        out_shape=jax.ShapeDtypeStruct((n_shards, d, ws), F32), rider=rider)


def _in_t(dp3, w_in3, tm, rider=None):
    _, t, sec_w = dp3.shape
    s, d, ws = w_in3.shape
    tk = sec_w // 2
    nq = ws // tk

    def body(*refs):
        a_refs, w_ref, o_ref = refs[:nq], refs[nq], refs[nq + 1]
        k = pl.program_id(1)
        prod = _dot_nt(a_refs[0][...], w_ref[:, 0:tk])
        for q in range(1, nq):
            prod = prod + _dot_nt(a_refs[q][...], w_ref[:, q * tk:(q + 1) * tk])

        @pl.when(k == 0)
        def _():
            o_ref[...] = prod

        @pl.when(k > 0)
        def _():
            o_ref[...] += prod

    def a_spec(q):
        return pl.BlockSpec((None, tm, tk), lambda i, k: ((nq * k + q) // 2, i, (nq * k + q) % 2))

    return _call(
        body, (dp3,) * nq + (w_in3,), name="in_t", grid=(t // tm, s),
        in_specs=[a_spec(q) for q in range(nq)] + [pl.BlockSpec((None, d, ws), lambda i, k: (k, 0, 0))],
        out_specs=pl.BlockSpec((tm, d), lambda i, k: (i, 0)),
        out_shape=jax.ShapeDtypeStruct((t, d), F32), rider=rider)


def _mix_ln1(cat, w_out, x, g0, b0, g1, b1, alpha, tm, rider=None):
    t, d = x.shape

    def body(cat_ref, w_ref, x_ref, g0_ref, b0_ref, g1_ref, b1_ref, xh_ref, h1b_ref, rstd_ref):
        mix = _dot(cat_ref[...], w_ref[...])
        xh0, _ = _ln_stats(x_ref[...])
        z1 = alpha * (xh0 * g0_ref[...] + b0_ref[...]) + mix
        xh1, rstd1 = _ln_stats(z1)
        xh_ref[...] = xh1
        h1b_ref[...] = (xh1 * g1_ref[...] + b1_ref[...]).astype(BF16)
        rstd_ref[...] = rstd1

    row = pl.BlockSpec((1, d), lambda i: (0, 0))
    blk = pl.BlockSpec((tm, d), lambda i: (i, 0))
    return _call(
        body, (cat, w_out, x, g0, b0, g1, b1), name="mix_ln1", grid=(t // tm,),
        in_specs=[blk, pl.BlockSpec((d, d), lambda i: (0, 0)), blk, row, row, row, row],
        out_specs=[blk, blk, pl.BlockSpec((tm, 1), lambda i: (i, 0))],
        out_shape=[jax.ShapeDtypeStruct((t, d), F32), jax.ShapeDtypeStruct((t, d), BF16),
                   jax.ShapeDtypeStruct((t, 1), F32)], rider=rider)


def _ln2_loss(ffn, xhat1, tgt, g1, b1, g2, b2, alpha, tm):
    t, d = xhat1.shape
    ni = t // tm
    inv_d = 1.0 / d

    def body(ffn_ref, xh1_ref, tgt_ref, g1_ref, b1_ref, g2_ref, b2_ref,
             dz2_ref, dz2b_ref, dg2_ref, db2_ref, loss_ref, lrow):
        i = pl.program_id(0)
        h1 = xh1_ref[...] * g1_ref[...] + b1_ref[...]
        xh2, rstd2 = _ln_stats(alpha * h1 + ffn_ref[...])
        g2v = g2_ref[...]
        diff = xh2 * g2v + b2_ref[...] - tgt_ref[...]
        dh2 = diff * inv_d
        sq = jnp.sum(diff * diff, axis=0, keepdims=True)
        dg = jnp.sum(dh2 * xh2, axis=0, keepdims=True)
        db = jnp.sum(dh2, axis=0, keepdims=True)

        @pl.when(i == 0)
        def _():
            lrow[...] = sq
            dg2_ref[...] = dg
            db2_ref[...] = db

        @pl.when(i > 0)
        def _():
            lrow[...] += sq
            dg2_ref[...] += dg
            db2_ref[...] += db

        dz2 = _ln_bwd(dh2, xh2, rstd2, g2v)
        dz2_ref[...] = dz2
        dz2b_ref[...] = dz2.astype(BF16)

        @pl.when(i == ni - 1)
        def _():
            tot = jnp.sum(lrow[...], axis=-1, keepdims=True) * (0.5 * inv_d)
            loss_ref[...] = jnp.broadcast_to(tot, (1, LANE))

    row = pl.BlockSpec((1, d), lambda i: (0, 0))
    blk = pl.BlockSpec((tm, d), lambda i: (i, 0))
    return _call(
        body, (ffn, xhat1, tgt, g1, b1, g2, b2), name="ln2_loss", grid=(ni,),
        in_specs=[blk, blk, blk, row, row, row, row],
        out_specs=[blk, blk, row, row, pl.BlockSpec((1, LANE), lambda i: (0, 0))],
        out_shape=[jax.ShapeDtypeStruct((t, d), F32), jax.ShapeDtypeStruct((t, d), BF16),
                   jax.ShapeDtypeStruct((1, d), F32), jax.ShapeDtypeStruct((1, d), F32),
                   jax.ShapeDtypeStruct((1, LANE), F32)],
        scratch_shapes=[pltpu.VMEM((1, d), F32)])


def _ln1_bwd(pre, dz2, xhat1, rstd1, g1, alpha, tm):
    t, d = dz2.shape

    def body(pre_ref, dz2_ref, xh_ref, rstd_ref, g_ref, dz1_ref, dz1b_ref, dg_ref, db_ref):
        i = pl.program_id(0)
        dh1 = alpha * dz2_ref[...] + pre_ref[...]
        xh = xh_ref[...]
        dg = jnp.sum(dh1 * xh, axis=0, keepdims=True)
        db = jnp.sum(dh1, axis=0, keepdims=True)

        @pl.when(i == 0)
        def _():
            dg_ref[...] = dg
            db_ref[...] = db

        @pl.when(i > 0)
        def _():
            dg_ref[...] += dg
            db_ref[...] += db

        dz1 = _ln_bwd(dh1, xh, rstd_ref[...], g_ref[...])
        dz1_ref[...] = dz1
        dz1b_ref[...] = dz1.astype(BF16)

    row = pl.BlockSpec((1, d), lambda i: (0, 0))
    blk = pl.BlockSpec((tm, d), lambda i: (i, 0))
    return _call(
        body, (pre, dz2, xhat1, rstd1, g1), name="ln1_bwd", grid=(t // tm,),
        in_specs=[blk, blk, blk, pl.BlockSpec((tm, 1), lambda i: (i, 0)), row],
        out_specs=[blk, blk, row, row],
        out_shape=[jax.ShapeDtypeStruct((t, d), F32), jax.ShapeDtypeStruct((t, d), BF16),
                   jax.ShapeDtypeStruct((1, d), F32), jax.ShapeDtypeStruct((1, d), F32)])


def _ln0_bwd(pre, dz1, x, g0, alpha, tm):
    t, d = x.shape

    def body(pre_ref, dz1_ref, x_ref, g_ref, dx_ref, dg_ref, db_ref):
        i = pl.program_id(0)
        dh0 = alpha * dz1_ref[...] + pre_ref[...]
        xh, rstd = _ln_stats(x_ref[...])
        dg = jnp.sum(dh0 * xh, axis=0, keepdims=True)
        db = jnp.sum(dh0, axis=0, keepdims=True)

        @pl.when(i == 0)
        def _():
            dg_ref[...] = dg
            db_ref[...] = db

        @pl.when(i > 0)
        def _():
            dg_ref[...] += dg
            db_ref[...] += db

        dx_ref[...] = _ln_bwd(dh0, xh, rstd, g_ref[...])

    row = pl.BlockSpec((1, d), lambda i: (0, 0))
    blk = pl.BlockSpec((tm, d), lambda i: (i, 0))
    return _call(
        body, (pre, dz1, x, g0), name="ln0_bwd", grid=(t // tm,),
        in_specs=[blk, blk, blk, row], out_specs=[blk, row, row],
        out_shape=[jax.ShapeDtypeStruct((t, d), F32), jax.ShapeDtypeStruct((1, d), F32),
                   jax.ShapeDtypeStruct((1, d), F32)])


def _shift_copies(ext, shifted):
    n = shifted.shape[1]
    for p in range(1, SUBLANE):
        shifted[p - 1] = ext[pl.ds(p, n), :]


def _window(ext, shifted, start, rows):
    p = start % SUBLANE
    if p == 0:
        return ext[pl.ds(start, rows), :]
    return shifted[p - 1, pl.ds(start - p, rows), :]


def _conv_fwd(p3, conv_w, conv_b, cn_g, cn_b, tc, cb, rider=None):
    _, t, w = p3.shape
    kk = conv_w.shape[0]
    off = HALO - (kk - 1)
    hb = tc // HALO

    def body(a_ref, g_ref, ap_ref, gp_ref, w_ref, b_ref, ng_ref, nb_ref, cat_ref, u1_ref, ext, sh):
        i = pl.program_id(1)
        ext[pl.ds(HALO, tc), :] = a_ref[...] * _sigmoid(g_ref[...])
        prev = ap_ref[...] * _sigmoid(gp_ref[...])
        ext[pl.ds(0, HALO), :] = jnp.where(i > 0, prev, 0.0)
        _shift_copies(ext, sh)
        for r in range(tc // ROWS):
            acc = jnp.broadcast_to(b_ref[...], (ROWS, cb))
            for k in range(kk):
                acc = acc + w_ref[k:k + 1, :] * _window(ext, sh, r * ROWS + off + k, ROWS)
            u1_ref[pl.ds(r * ROWS, ROWS), :] = acc
            for g in range(cb // LANE):
                sl = slice(g * LANE, (g + 1) * LANE)
                xh, _ = _ln_stats(acc[:, sl])
                u2 = xh * ng_ref[:, sl] + nb_ref[:, sl]
                cat_ref[pl.ds(r * ROWS, ROWS), sl] = (u2 * _sigmoid(u2)).astype(BF16)

    cur = lambda sec: pl.BlockSpec((None, tc, cb), lambda j, i: (sec, i, j))
    prev = lambda sec: pl.BlockSpec((None, HALO, cb), lambda j, i: (sec, jnp.maximum(i * hb - 1, 0), j))
    row = pl.BlockSpec((1, cb), lambda j, i: (0, j))
    return _call(
        body, (p3, p3, p3, p3, conv_w, conv_b, cn_g, cn_b), name="conv_fwd", grid=(w // cb, t // tc),
        in_specs=[cur(0), cur(1), prev(0), prev(1), pl.BlockSpec((kk, cb), lambda j, i: (0, j)), row, row, row],
        out_specs=[pl.BlockSpec((tc, cb), lambda j, i: (i, j)), pl.BlockSpec((tc, cb), lambda j, i: (i, j))],
        out_shape=[jax.ShapeDtypeStruct((t, 2 * w), BF16), jax.ShapeDtypeStruct((t, w), F32)],
        scratch_shapes=[pltpu.VMEM((tc + HALO, cb), F32),
                        pltpu.VMEM((SUBLANE - 1, tc + HALO - SUBLANE, cb), F32)], rider=rider)


def _conv_norm_bwd(dcat, u1, cn_g, cn_b, tc):
    t, w = u1.shape

    def body(du_ref, u1_ref, ng_ref, nb_ref, du1_ref, dg_ref, db_ref):
        i = pl.program_id(0)
        for g in range(w // LANE):
            sl = slice(g * LANE, (g + 1) * LANE)
            ng = ng_ref[:, sl]
            xh, rstd = _ln_stats(u1_ref[:, sl])
            u2 = xh * ng + nb_ref[:, sl]
            sg = _sigmoid(u2)
            du2 = du_ref[:, sl] * (sg * (1.0 + u2 * (1.0 - sg)))
            dg = jnp.sum(du2 * xh, axis=0, keepdims=True)
            db = jnp.sum(du2, axis=0, keepdims=True)

            @pl.when(i == 0)
            def _():
                dg_ref[:, sl] = dg
                db_ref[:, sl] = db

            @pl.when(i > 0)
            def _():
                dg_ref[:, sl] += dg
                db_ref[:, sl] += db

            du1_ref[:, sl] = _ln_bwd(du2, xh, rstd, ng)

    row = pl.BlockSpec((1, w), lambda i: (0, 0))
    blk = pl.BlockSpec((tc, w), lambda i: (i, 0))
    return pl.pallas_call(
        body, name="conv_norm_bwd", grid=(t // tc,),
        in_specs=[blk, blk, row, row], out_specs=[blk, row, row],
        out_shape=[jax.ShapeDtypeStruct((t, w), F32), jax.ShapeDtypeStruct((1, w), F32),
                   jax.ShapeDtypeStruct((1, w), F32)],
        compiler_params=_params("arbitrary"),
    )(dcat, u1, cn_g, cn_b)


def _conv_bwd(du1, p3, conv_w, tc, cb, rider=None):
    n_sec, t, w = p3.shape
    kk = conv_w.shape[0]
    off = HALO - (kk - 1)
    hb = tc // HALO
    nt = t // tc
    kpad = -(-kk // SUBLANE) * SUBLANE

    def body(d_ref, dn_ref, a_ref, g_ref, ap_ref, gp_ref, w_ref, dp_ref, dw_ref, db_ref,
             extd, extu, shd, shu, wacc, bacc):
        i = pl.program_id(1)

        @pl.when(i == 0)
        def _():
            wacc[...] = jnp.zeros_like(wacc)
            bacc[...] = jnp.zeros_like(bacc)

        extd[pl.ds(0, tc), :] = d_ref[...]
        extd[pl.ds(tc, HALO), :] = jnp.where(i < nt - 1, dn_ref[...], 0.0)
        extu[pl.ds(HALO, tc), :] = a_ref[...] * _sigmoid(g_ref[...])
        extu[pl.ds(0, HALO), :] = jnp.where(i > 0, ap_ref[...] * _sigmoid(gp_ref[...]), 0.0)
        _shift_copies(extd, shd)
        _shift_copies(extu, shu)
        for r in range(tc // ROWS):
            rows = pl.ds(r * ROWS, ROWS)
            acc = jnp.zeros((ROWS, cb), F32)
            for k in range(kk):
                acc = acc + w_ref[k:k + 1, :] * _window(extd, shd, r * ROWS + (kk - 1) - k, ROWS)
            a = a_ref[rows, :]
            sg = _sigmoid(g_ref[rows, :])
            dp_ref[0, rows, :] = (acc * sg).astype(BF16)
            dp_ref[1, rows, :] = (acc * a * sg * (1.0 - sg)).astype(BF16)
            d = d_ref[rows, :]
            bacc[...] += jnp.sum(d.reshape(ROWS // SUBLANE, SUBLANE, cb), axis=0)
            for k in range(kk):
                prod = d * _window(extu, shu, r * ROWS + off + k, ROWS)
                wacc[k] += jnp.sum(prod.reshape(ROWS // SUBLANE, SUBLANE, cb), axis=0)

        @pl.when(i == nt - 1)
        def _():
            for k in range(kk):
                dw_ref[k:k + 1, :] = jnp.sum(wacc[k], axis=0, keepdims=True)
            if kpad > kk:
                dw_ref[kk:kpad, :] = jnp.zeros((kpad - kk, cb), F32)
            db_ref[...] = jnp.sum(bacc[...], axis=0, keepdims=True)

    cur = lambda sec: pl.BlockSpec((None, tc, cb), lambda j, i: (sec, i, j))
    prev = lambda sec: pl.BlockSpec((None, HALO, cb), lambda j, i: (sec, jnp.maximum(i * hb - 1, 0), j))
    return _call(
        body, (du1, du1, p3, p3, p3, p3, conv_w), name="conv_bwd", grid=(w // cb, nt),
        in_specs=[pl.BlockSpec((tc, cb), lambda j, i: (i, j)),
                  pl.BlockSpec((HALO, cb), lambda j, i: (jnp.minimum((i + 1) * hb, t // HALO - 1), j)),
                  cur(0), cur(1), prev(0), prev(1), pl.BlockSpec((kk, cb), lambda j, i: (0, j))],
        out_specs=[pl.BlockSpec((2, tc, cb), lambda j, i: (0, i, j)),
                   pl.BlockSpec((kpad, cb), lambda j, i: (0, j)),
                   pl.BlockSpec((1, cb), lambda j, i: (0, j))],
        out_shape=[jax.ShapeDtypeStruct((n_sec, t, w), BF16), jax.ShapeDtypeStruct((kpad, w), F32),
                   jax.ShapeDtypeStruct((1, w), F32)],
        scratch_shapes=[pltpu.VMEM((tc + HALO, cb), F32), pltpu.VMEM((tc + HALO, cb), F32),
                        pltpu.VMEM((SUBLANE - 1, tc + HALO - SUBLANE, cb), F32),
                        pltpu.VMEM((SUBLANE - 1, tc + HALO - SUBLANE, cb), F32),
                        pltpu.VMEM((kk, SUBLANE, cb), F32), pltpu.VMEM((SUBLANE, cb), F32)], rider=rider)


def _ffn_act_fwd(hh3, fw, fb, tc, cb, rider=None):
    _, t, dff = hh3.shape
    kk = fw.shape[0]
    off = FHALO - (kk - 1)
    hb = tc // FHALO

    def body(g_ref, v_ref, gp_ref, w_ref, b_ref, act_ref, ext):
        i = pl.program_id(1)
        ext[pl.ds(FHALO, tc), :] = g_ref[...]
        ext[pl.ds(0, FHALO), :] = jnp.where(i > 0, gp_ref[...], 0.0)
        for r in range(tc // ROWS):
            rows = pl.ds(r * ROWS, ROWS)
            gc = jnp.broadcast_to(b_ref[...], (ROWS, cb))
            for k in range(kk):
                gc = gc + w_ref[k:k + 1, :] * ext[pl.ds(r * ROWS + off + k, ROWS), :]
            act_ref[rows, :] = (gc * _sigmoid(gc) * v_ref[rows, :]).astype(BF16)

    return _call(
        body, (hh3, hh3, hh3, fw, fb), name="ffn_act_fwd", grid=(dff // cb, t // tc),
        in_specs=[pl.BlockSpec((None, tc, cb), lambda j, i: (0, i, j)),
                  pl.BlockSpec((None, tc, cb), lambda j, i: (1, i, j)),
                  pl.BlockSpec((None, FHALO, cb), lambda j, i: (0, jnp.maximum(i * hb - 1, 0), j)),
                  pl.BlockSpec((kk, cb), lambda j, i: (0, j)),
                  pl.BlockSpec((1, cb), lambda j, i: (0, j))],
        out_specs=pl.BlockSpec((tc, cb), lambda j, i: (i, j)),
        out_shape=jax.ShapeDtypeStruct((t, dff), BF16),
        scratch_shapes=[pltpu.VMEM((tc + FHALO, cb), F32)], rider=rider)


def _ffn_act_bwd(dact, hh3, fw, fb, tc, cb):
    _, t, dff = hh3.shape
    kk = fw.shape[0]
    off = FHALO - (kk - 1)
    hb = tc // FHALO
    nt = t // tc
    te = tc + FHALO

    def body(da_ref, dan_ref, g_ref, gp_ref, gn_ref, v_ref, vn_ref, w_ref, b_ref,
             dhh_ref, dw_ref, db_ref, gext, dext, wacc, bacc):
        i = pl.program_id(1)

        @pl.when(i == 0)
        def _():
            wacc[...] = jnp.zeros_like(wacc)
            bacc[...] = jnp.zeros_like(bacc)

        gext[pl.ds(0, FHALO), :] = jnp.where(i > 0, gp_ref[...], 0.0)
        gext[pl.ds(FHALO, tc), :] = g_ref[...]
        gext[pl.ds(FHALO + tc, FHALO), :] = gn_ref[...]

        def gate_grad(r0, n, da, v):
            gc = jnp.broadcast_to(b_ref[...], (n, cb))
            for k in range(kk):
                gc = gc + w_ref[k:k + 1, :] * gext[pl.ds(r0 + off + k, n), :]
            sg = _sigmoid(gc)
            return gc * sg, da * v * (sg * (1.0 + gc * (1.0 - sg)))

        for r in range(tc // ROWS):
            rows = pl.ds(r * ROWS, ROWS)
            da = da_ref[rows, :]
            silu, dgc = gate_grad(r * ROWS, ROWS, da, v_ref[rows, :])
            dext[rows, :] = dgc
            dhh_ref[1, rows, :] = (da * silu).astype(BF16)
        _, dgc_next = gate_grad(tc, FHALO, dan_ref[...], vn_ref[...])
        dext[pl.ds(tc, FHALO), :] = jnp.where(i < nt - 1, dgc_next, 0.0)
        for r in range(tc // ROWS):
            rows = pl.ds(r * ROWS, ROWS)
            dg = jnp.zeros((ROWS, cb), F32)
            for k in range(kk):
                dg = dg + w_ref[k:k + 1, :] * dext[pl.ds(r * ROWS + (kk - 1) - k, ROWS), :]
            dhh_ref[0, rows, :] = dg.astype(BF16)
            dgc = dext[rows, :]
            bacc[...] += jnp.sum(dgc.reshape(ROWS // SUBLANE, SUBLANE, cb), axis=0)
            for k in range(kk):
                prod = dgc * gext[pl.ds(r * ROWS + off + k, ROWS), :]
                wacc[k] += jnp.sum(prod.reshape(ROWS // SUBLANE, SUBLANE, cb), axis=0)

        @pl.when(i == nt - 1)
        def _():
            for k in range(kk):
                dw_ref[k:k + 1, :] = jnp.sum(wacc[k], axis=0, keepdims=True)
            dw_ref[kk:SUBLANE, :] = jnp.zeros((SUBLANE - kk, cb), F32)
            db_ref[...] = jnp.sum(bacc[...], axis=0, keepdims=True)

    nxt = lambda i: jnp.minimum((i + 1) * hb, t // FHALO - 1)
    return pl.pallas_call(
        body, name="ffn_act_bwd", grid=(dff // cb, nt),
        in_specs=[pl.BlockSpec((tc, cb), lambda j, i: (i, j)),
                  pl.BlockSpec((FHALO, cb), lambda j, i: (nxt(i), j)),
                  pl.BlockSpec((None, tc, cb), lambda j, i: (0, i, j)),
                  pl.BlockSpec((None, FHALO, cb), lambda j, i: (0, jnp.maximum(i * hb - 1, 0), j)),
                  pl.BlockSpec((None, FHALO, cb), lambda j, i: (0, nxt(i), j)),
                  pl.BlockSpec((None, tc, cb), lambda j, i: (1, i, j)),
                  pl.BlockSpec((None, FHALO, cb), lambda j, i: (1, nxt(i), j)),
                  pl.BlockSpec((kk, cb), lambda j, i: (0, j)),
                  pl.BlockSpec((1, cb), lambda j, i: (0, j))],
        out_specs=[pl.BlockSpec((2, tc, cb), lambda j, i: (0, i, j)),
                   pl.BlockSpec((SUBLANE, cb), lambda j, i: (0, j)),
                   pl.BlockSpec((1, cb), lambda j, i: (0, j))],
        out_shape=[jax.ShapeDtypeStruct((2, t, dff), BF16), jax.ShapeDtypeStruct((SUBLANE, dff), F32),
                   jax.ShapeDtypeStruct((1, dff), F32)],
        scratch_shapes=[pltpu.VMEM((tc + 2 * FHALO, cb), F32), pltpu.VMEM((te, cb), F32),
                        pltpu.VMEM((kk, SUBLANE, cb), F32), pltpu.VMEM((SUBLANE, cb), F32)],
        compiler_params=_params("parallel", "arbitrary"),
    )(dact, dact, hh3, hh3, hh3, hh3, hh3, fw, fb)


def _chunk_consts():
    r = lax.broadcasted_iota(jnp.int32, (CHUNK, CHUNK), 0)
    c = lax.broadcasted_iota(jnp.int32, (CHUNK, CHUNK), 1)
    blk = (r // SUB) * SUB
    tri = (c <= r).astype(F32)
    start = (c < blk).astype(F32)
    end = (c < blk + SUB).astype(F32)
    return jnp.concatenate([tri, start, end, jnp.ones((SUBLANE, CHUNK), F32)], axis=0)


def _roll8(x, d):
    return pltpu.roll(x.reshape(CHUNK // SUB, SUB, LANE), d % SUB, 1).reshape(CHUNK, LANE)


def _gate_terms(q, fpre, lb):
    sf = _sigmoid(fpre)
    fg = lb + (1.0 - lb) * sf
    sq = _sigmoid(q)
    return sf, fg, 1.0 - fg, sq, q * sq


def _decays(g, consts):
    cs = _dot3(consts, g)
    b = cs[0:CHUNK]
    rs = cs[CHUNK:2 * CHUNK]
    re = cs[2 * CHUNK:3 * CHUNK]
    tot = cs[3 * CHUNK:3 * CHUNK + 1]
    return b, rs, re, tot


def _lower_bound(lb_ref):
    l0, l1 = lb_ref[0:1, :], lb_ref[1:2, :]
    mx = jnp.maximum(l0, l1)
    e0, e1 = jnp.exp(l0 - mx), jnp.exp(l1 - mx)
    return e0 / (e0 + e1)


def _scaled_keys(kt, rs, re, rowblk, i):
    scale = jnp.where(rowblk < i, jnp.exp(jnp.minimum(rs[SUB * i:SUB * i + 1, :] - re, 0.0)), 0.0)
    return kt * scale, scale


def _hgrn_fwd(p3, lb_logits, hg, cat, tb, hpb, rider=None):
    _, t, w = p3.shape
    nh = w // LANE
    nc = tb // CHUNK
    assert nh % hpb == 0

    def body(q_ref, f_ref, v_ref, og_ref, lb_ref, hg_ref, cat_in, cat_ref, o_ref, st_ref, state):
        del cat_in
        consts = _chunk_consts()
        lb_all = _lower_bound(lb_ref)
        rowblk = lax.broadcasted_iota(jnp.int32, (CHUNK, 1), 0) // SUB
        rowpos = lax.broadcasted_iota(jnp.int32, (CHUNK, 1), 0) % SUB

        @pl.when(pl.program_id(1) == 0)
        def _():
            state[...] = jnp.zeros_like(state)

        def chunk(c, carry):
            rows = pl.ds(pl.multiple_of(c * CHUNK, CHUNK), CHUNK)
            heads = range(hpb)
            sls = [slice(j * LANE, (j + 1) * LANE) for j in heads]
            v = [v_ref[rows, s] for s in sls]
            vb = [x.astype(BF16) for x in v]
            gates = [_gate_terms(q_ref[rows, s], f_ref[rows, s], lb_all[:, s]) for s in sls]
            fg = [g[1] for g in gates]
            kk = [g[2] for g in gates]
            qh = [g[4] for g in gates]
            dec = [_decays(jnp.log(x), consts) for x in fg]
            b = [x[0] for x in dec]
            rs = [x[1] for x in dec]
            re = [x[2] for x in dec]
            tot = [x[3] for x in dec]
            qt = [qh[j] * jnp.exp(b[j] - rs[j]) for j in heads]
            kt = [kk[j] * jnp.exp(re[j] - b[j]) for j in heads]
            st = [state[j] for j in heads]
            for j in heads:
                st_ref[j, c] = st[j]
            a = [jnp.zeros((CHUNK, CHUNK), F32) for _ in heads]
            for i in range(1, CHUNK // SUB):
                for j in heads:
                    ki, _ = _scaled_keys(kt[j], rs[j], re[j], rowblk, i)
                    a[j] = a[j] + _dot_nt(jnp.where(rowblk == i, qt[j], 0.0).astype(BF16), ki.astype(BF16))
            o = [_dot(a[j].astype(BF16), vb[j]) for j in heads]
            o = [o[j] + _dot_nt((qh[j] * jnp.exp(b[j])).astype(BF16), st[j].astype(BF16)) for j in heads]
            for j in heads:
                k_up = kk[j] * jnp.exp(tot[j] - b[j])
                state[j] = st[j] * jnp.exp(tot[j]) + _dot_tn(vb[j], k_up.astype(BF16))
            for j in heads:
                e, rf = None, fg[j]
                for d in range(SUB):
                    if d == 0:
                        vs, term = v[j], qh[j] * kk[j]
                    else:
                        e = rf if e is None else e * rf
                        rf = _roll8(fg[j], d)
                        vs = _roll8(v[j], d)
                        term = jnp.where(rowpos >= d, qh[j] * (1.0 - rf) * e, 0.0)
                    o[j] = o[j] + jnp.sum(term, axis=-1, keepdims=True) * vs
            for j in heads:
                og = og_ref[rows, sls[j]]
                o_ref[rows, sls[j]] = o[j]
                r = lax.rsqrt(jnp.mean(o[j] * o[j], axis=-1, keepdims=True) + RMS_EPS)
                cat_ref[rows, sls[j]] = (o[j] * r * hg_ref[:, sls[j]] * (og * _sigmoid(og))).astype(BF16)
            return carry

        lax.fori_loop(0, nc, chunk, 0)

    bw = hpb * LANE
    sec = lambda s: pl.BlockSpec((None, tb, bw), lambda h, i: (s, i, h))
    return _call(
        body, (p3, p3, p3, p3, lb_logits, hg, cat), name="hgrn_fwd", grid=(nh // hpb, t // tb),
        in_specs=[sec(2), sec(3), sec(4), sec(5),
                  pl.BlockSpec((2, bw), lambda h, i: (0, h)),
                  pl.BlockSpec((1, bw), lambda h, i: (0, h)), HBM],
        out_specs=[pl.BlockSpec((tb, bw), lambda h, i: (i, nh // hpb + h)),
                   pl.BlockSpec((tb, bw), lambda h, i: (i, h)),
                   pl.BlockSpec((hpb, nc, LANE, LANE), lambda h, i: (h, i, 0, 0))],
        out_shape=[jax.ShapeDtypeStruct(cat.shape, BF16), jax.ShapeDtypeStruct((t, w), F32),
                   jax.ShapeDtypeStruct((nh, t // CHUNK, LANE, LANE), F32)],
        scratch_shapes=[pltpu.VMEM((hpb, LANE, LANE), F32)], aliases={6: 0}, rider=rider)


def _hgrn_bwd(p3, lb_logits, hg, o_pre, states, dcat, dp3, tb, hpb, rider=None):
    n_sec, t, w = p3.shape
    nh = w // LANE
    assert nh % hpb == 0
    nc = tb // CHUNK
    nb = t // tb
    bw = hpb * LANE
    n_steps = (nh // hpb) * nb

    def body(q_ref, f_ref, v_ref, og_ref, lb_ref, hg_ref, o_ref, st_ref, dc_ref, dp_in,
             dp_ref, dlb_ref, dhg_ref, dstate, stash, lbacc, hgacc, osem):
        del dp_in
        h, i = pl.program_id(0), pl.program_id(1)
        step = h * nb + i
        slot = step % 2

        def out_copy(s, row_blk, lane_blk):
            dst = dp_ref.at[pl.ds(2, 4), pl.ds(row_blk * tb, tb), pl.ds(lane_blk * bw, bw)]
            return pltpu.make_async_copy(stash.at[s], dst, osem.at[s])

        @pl.when(step >= 2)
        def _():
            out_copy(slot, 0, 0).wait()

        def compute():
            consts = _chunk_consts()
            rr = lax.broadcasted_iota(jnp.int32, (CHUNK, CHUNK), 0)
            cc = lax.broadcasted_iota(jnp.int32, (CHUNK, CHUNK), 1)
            upper = (cc >= rr).astype(F32)
            lb_all = _lower_bound(lb_ref)
            rowblk = lax.broadcasted_iota(jnp.int32, (CHUNK, 1), 0) // SUB
            rowpos = lax.broadcasted_iota(jnp.int32, (CHUNK, 1), 0) % SUB

            @pl.when(i == 0)
            def _():
                dstate[...] = jnp.zeros_like(dstate)
                lbacc[...] = jnp.zeros_like(lbacc)
                hgacc[...] = jnp.zeros_like(hgacc)

            def head(j, c, rows):
                sl = slice(j * LANE, (j + 1) * LANE)
                lb = lb_all[:, sl]
                hgv = hg_ref[:, sl]
                q = q_ref[rows, sl]
                v = v_ref[rows, sl]
                og = og_ref[rows, sl]
                o = o_ref[rows, sl]
                dcg = dc_ref[rows, sl]
                sf, fg, kk, sq, qh = _gate_terms(q, f_ref[rows, sl], lb)
                b, rs, re, tot = _decays(jnp.log(fg), consts)
                eq = jnp.exp(b - rs)
                ek = jnp.exp(re - b)
                qt = qh * eq
                kt = kk * ek
                e_in = jnp.exp(b)
                e_up = jnp.exp(tot - b)
                e_tot = jnp.exp(tot)
                q_in = (qh * e_in).astype(BF16)
                k_up = (kk * e_up).astype(BF16)
                vb = v.astype(BF16)
                st = st_ref[j, c]
                dst = dstate[j]
                dstb = dst.astype(BF16)
                yield

                sg = _sigmoid(og)
                r = lax.rsqrt(jnp.mean(o * o, axis=-1, keepdims=True) + RMS_EPS)
                ohat = o * r
                d_og = dcg * ohat * hgv * (sg * (1.0 + og * (1.0 - sg)))
                d_on = dcg * (og * sg)
                hgacc[:, sl] += jnp.sum((d_on * ohat).reshape(CHUNK // SUBLANE, SUBLANE, LANE), axis=0)
                d_oh = d_on * hgv
                do = r * (d_oh - ohat * jnp.mean(d_oh * ohat, axis=-1, keepdims=True))
                dob = do.astype(BF16)

                da = _dot_nt(dob, vb)
                yield
                a_off = jnp.zeros((CHUNK, CHUNK), F32)
                dqt = jnp.zeros((CHUNK, LANE), F32)
                dkt = jnp.zeros((CHUNK, LANE), F32)
                for blk in range(1, CHUNK // SUB):
                    ki, scale = _scaled_keys(kt, rs, re, rowblk, blk)
                    kib = ki.astype(BF16)
                    qib = jnp.where(rowblk == blk, qt, 0.0).astype(BF16)
                    dab = jnp.where(rowblk == blk, da, 0.0).astype(BF16)
                    a_off = a_off + _dot_nt(qib, kib)
                    dqt = dqt + _dot(dab, kib)
                    dkt = dkt + _dot_tn(dab, qib) * scale
                    yield
                dqh = dqt * eq
                dk = dkt * ek
                dv = _dot_tn(a_off.astype(BF16), dob)

                dqh = dqh + _dot(dob, st.astype(BF16)) * e_in
                dk = dk + _dot(vb, dstb) * e_up
                dv = dv + _dot_nt(k_up, dstb)
                st_end = st * e_tot + _dot_tn(vb, k_up)
                carry_g = jnp.sum(st_end * dst, axis=0, keepdims=True)
                dstate[j] = dst * e_tot + _dot_tn(dob, q_in)
                yield

                e, rf = None, fg
                for d in range(SUB):
                    if d == 0:
                        a_d = jnp.sum(qh * kk, axis=-1, keepdims=True)
                        da_d = jnp.sum(do * v, axis=-1, keepdims=True)
                        dqh = dqh + da_d * kk
                        dk = dk + da_d * qh
                        dv = dv + a_d * do
                        continue
                    e = rf if e is None else e * rf
                    rf = _roll8(fg, d)
                    em = jnp.where(rowpos >= d, e, 0.0)
                    ks, vs = 1.0 - rf, _roll8(v, d)
                    a_d = jnp.sum(qh * ks * em, axis=-1, keepdims=True)
                    da_d = jnp.sum(do * vs, axis=-1, keepdims=True) * em
                    dqh = dqh + da_d * ks
                    dk = dk + _roll8(da_d * qh, -d)
                    dv = dv + _roll8(a_d * do, -d)
                yield

                dg = _dot3(upper, qh * dqh - kk * dk) + carry_g
                dfg = dg / fg - dk
                lbacc[:, sl] += jnp.sum((dfg * (1.0 - sf)).reshape(CHUNK // SUBLANE, SUBLANE, LANE), axis=0)
                stash[slot, 0, rows, sl] = (dqh * (sq * (1.0 + q * (1.0 - sq)))).astype(BF16)
                stash[slot, 1, rows, sl] = (dfg * (1.0 - lb) * sf * (1.0 - sf)).astype(BF16)
                stash[slot, 2, rows, sl] = dv.astype(BF16)
                stash[slot, 3, rows, sl] = d_og.astype(BF16)

            def chunk(cr, carry):
                c = nc - 1 - cr
                rows = pl.ds(pl.multiple_of(c * CHUNK, CHUNK), CHUNK)
                running = [head(j, c, rows) for j in range(hpb)]
                while running:
                    running = [g for g in running if next(g, StopIteration) is not StopIteration]
                return carry

            lax.fori_loop(0, nc, chunk, 0)

            @pl.when(i == nb - 1)
            def _():
                dlb_ref[...] = jnp.sum(lbacc[...], axis=0, keepdims=True)
                dhg_ref[...] = jnp.sum(hgacc[...], axis=0, keepdims=True)

        compute()
        out_copy(slot, nb - 1 - i, h).start()

        @pl.when(step == n_steps - 1)
        def _():
            out_copy(slot, 0, 0).wait()
            if n_steps >= 2:
                out_copy(1 - slot, 0, 0).wait()

    rev = lambda i: nb - 1 - i
    sec = lambda s: pl.BlockSpec((None, tb, bw), lambda h, i: (s, rev(i), h))
    return _call(
        body, (p3, p3, p3, p3, lb_logits, hg, o_pre, states, dcat, dp3), name="hgrn_bwd", grid=(nh // hpb, nb),
        in_specs=[sec(2), sec(3), sec(4), sec(5),
                  pl.BlockSpec((2, bw), lambda h, i: (0, h)),
                  pl.BlockSpec((1, bw), lambda h, i: (0, h)),
                  pl.BlockSpec((tb, bw), lambda h, i: (rev(i), h)),
                  pl.BlockSpec((hpb, nc, LANE, LANE), lambda h, i: (h, rev(i), 0, 0)),
                  pl.BlockSpec((tb, bw), lambda h, i: (rev(i), nh // hpb + h)), HBM],
        out_specs=[HBM,
                   pl.BlockSpec((1, bw), lambda h, i: (0, h)),
                   pl.BlockSpec((1, bw), lambda h, i: (0, h))],
        out_shape=[jax.ShapeDtypeStruct((n_sec, t, w), BF16), jax.ShapeDtypeStruct((1, w), F32),
                   jax.ShapeDtypeStruct((1, w), F32)],
        scratch_shapes=[pltpu.VMEM((hpb, LANE, LANE), F32), pltpu.VMEM((2, 4, tb, bw), BF16),
                        pltpu.VMEM((SUBLANE, bw), F32), pltpu.VMEM((SUBLANE, bw), F32),
                        pltpu.SemaphoreType.DMA((2,))],
        aliases={9: 0}, rider=rider)


def _place():
    x, y, c = lax.axis_index("x"), lax.axis_index("y"), lax.axis_index("c")
    chips = [(1 - x, y), (x, 1 - y), (1 - x, 1 - y)]
    return x, y, c, chips


def _rows(buf, px, py, pc, part=None):
    half = buf.shape[1] // 2
    if part is None:
        return buf.at[2 * px + py, pl.ds(pc * half, half)]
    lo, hi, n = part
    piece = half // n
    return buf.at[2 * px + py, pl.ds(pc * half + lo * piece, (hi - lo) * piece)]


def _rcopy(src, dst, send, recv, idx, to):
    return pltpu.make_async_remote_copy(src_ref=src, dst_ref=dst, send_sem=send.at[idx], recv_sem=recv.at[idx],
                                        device_id=to, device_id_type=MESH)


def _same(bufs):
    return [jax.ShapeDtypeStruct(b.shape, b.dtype) for b in bufs]


def _ride_gather_ici(bufs, parts=None):
    n = len(bufs)
    parts = parts or [None] * n

    def start(rin, rout, send, recv):
        x, y, c, chips = _place()
        for k in range(n):
            mine = _rows(rout[k], x, y, c, parts[k])
            for j, chip in enumerate(chips):
                _rcopy(mine, mine, send, recv, 3 * k + j, (*chip, c)).start()

    def finish(rin, rout, send, recv):
        x, y, c, chips = _place()
        for k in range(n):
            for j, chip in enumerate(chips):
                theirs = _rows(rout[k], *chip, c, parts[k])
                _rcopy(theirs, theirs, send, recv, 3 * k + j, (x, y, c)).wait_recv()
        for k in range(n):
            mine = _rows(rout[k], x, y, c, parts[k])
            for j in range(3):
                _rcopy(mine, mine, send, recv, 3 * k + j, (x, y, c)).wait_send()

    return _Rider(bufs, _same(bufs), {k: k for k in range(n)}, 3 * n, start, finish)


class _SemView:
    def __init__(self, ref, base):
        self.ref, self.base = ref, base

    @property
    def at(self):
        return self

    def __getitem__(self, idx):
        return self.ref.at[idx + self.base]


def _ride_both(a, b):
    nai, nao = len(a.ins), len(a.outs)

    def start(rin, rout, send, recv):
        a.start(rin[:nai], rout[:nao], send, recv)
        b.start(rin[nai:], rout[nao:], _SemView(send, a.n_sems), _SemView(recv, a.n_sems))

    def finish(rin, rout, send, recv):
        a.finish(rin[:nai], rout[:nao], send, recv)
        b.finish(rin[nai:], rout[nao:], _SemView(send, a.n_sems), _SemView(recv, a.n_sems))

    aliases = dict(a.aliases)
    aliases.update({nai + ri: nao + ro for ri, ro in b.aliases.items()})
    return _Rider(a.ins + b.ins, a.outs + b.outs, aliases, a.n_sems + b.n_sems, start, finish)


def _ride_gather_d2d(bufs):
    n = len(bufs)

    def start(rin, rout, send, recv):
        x, y, c, chips = _place()
        for k in range(n):
            for j, chip in enumerate(chips):
                got = _rows(rout[k], *chip, c)
                _rcopy(got, got, send, recv, 3 * k + j, (x, y, 1 - c)).start()

    def finish(rin, rout, send, recv):
        x, y, c, chips = _place()
        for k in range(n):
            for j, chip in enumerate(chips):
                theirs = _rows(rout[k], *chip, 1 - c)
                _rcopy(theirs, theirs, send, recv, 3 * k + j, (x, y, c)).wait_recv()
        for k in range(n):
            for j, chip in enumerate(chips):
                got = _rows(rout[k], *chip, c)
                _rcopy(got, got, send, recv, 3 * k + j, (x, y, c)).wait_send()

    return _Rider(bufs, _same(bufs), {k: k for k in range(n)}, 3 * n, start, finish)


def _ride_swap(grads):
    n = len(grads)

    def copy(k, rin, rout, send, recv):
        x, y, c, _ = _place()
        half = rin[k].shape[1] // 2
        return _rcopy(rin[k].at[:, pl.ds((1 - c) * half, half)], rout[k], send, recv, k, (x, y, 1 - c))

    def start(rin, rout, send, recv):
        for k in range(n):
            copy(k, rin, rout, send, recv).start()

    def finish(rin, rout, send, recv):
        for k in range(n):
            copy(k, rin, rout, send, recv).wait()

    outs = [jax.ShapeDtypeStruct((g.shape[0], g.shape[1] // 2, g.shape[2]), g.dtype) for g in grads]
    return _Rider(grads, outs, {}, n, start, finish)


def _ride_send_partials(parts, pieces=None, into=None):
    n = len(parts)
    pieces = pieces or [None] * n

    def cut(ref, k):
        if pieces[k] is None:
            return ref
        lo, hi, m = pieces[k]
        q = ref.shape[0] // m
        return ref.at[pl.ds(lo * q, (hi - lo) * q)]

    def copies(rin, rout, send, recv):
        x, y, c, chips = _place()
        return [_rcopy(cut(rin[k].at[2 * px + py], k), cut(rout[k].at[j], k), send, recv, 3 * k + j, (px, py, c))
                for k in range(n) for j, (px, py) in enumerate(chips)]

    def start(rin, rout, send, recv):
        for cp in copies(rin, rout, send, recv):
            cp.start()

    def finish(rin, rout, send, recv):
        for cp in copies(rin, rout, send, recv):
            cp.wait()

    if into is None:
        outs = [jax.ShapeDtypeStruct((3,) + p.shape[1:], p.dtype) for p in parts]
        return _Rider(parts, outs, {}, 3 * n, start, finish)
    return _Rider(list(parts) + list(into), _same(into), {n + k: k for k in range(n)}, 3 * n, start, finish)


def _ride_join(bufs):
    n = len(bufs)

    def half_of(buf, pc):
        half = buf.shape[0] // 2
        return buf.at[pl.ds(pc * half, half)]

    def start(rin, rout, send, recv):
        x, y, c, _ = _place()
        for k in range(n):
            mine = half_of(rout[k], c)
            _rcopy(mine, mine, send, recv, k, (x, y, 1 - c)).start()

    def finish(rin, rout, send, recv):
        x, y, c, _ = _place()
        for k in range(n):
            mine, theirs = half_of(rout[k], c), half_of(rout[k], 1 - c)
            _rcopy(mine, mine, send, recv, k, (x, y, c)).wait_send()
            _rcopy(theirs, theirs, send, recv, k, (x, y, c)).wait_recv()

    return _Rider(bufs, _same(bufs), {k: k for k in range(n)}, n, start, finish)


def _run(name, rider):
    def body(*refs):
        nri, nro = len(rider.ins), len(rider.outs)
        rin, rout = refs[:nri], refs[nri:nri + nro]
        send, recv = refs[nri + nro:]
        rider.start(rin, rout, send, recv)
        rider.finish(rin, rout, send, recv)

    return pl.pallas_call(
        body, name=name, in_specs=[HBM] * len(rider.ins), out_specs=[HBM] * len(rider.outs), out_shape=rider.outs,
        scratch_shapes=[pltpu.SemaphoreType.DMA((rider.n_sems,)), pltpu.SemaphoreType.DMA((rider.n_sems,))],
        input_output_aliases=rider.aliases,
    )(*rider.ins)


def _add_halves(name, g, other, c_idx):
    s, r, cols = g.shape
    half = r // 2
    tr = _div_tile(half, 16, 512)
    nb = half // tr

    def body(c_ref, g_ref, o_ref, q_ref):
        del c_ref
        q_ref[...] = (g_ref[...] + o_ref[...]).astype(BF16)

    return pl.pallas_call(
        body, name=name,
        grid_spec=pltpu.PrefetchScalarGridSpec(
            num_scalar_prefetch=1, grid=(s, nb),
            in_specs=[pl.BlockSpec((None, tr, cols), lambda k, i, c: (k, c[0] * nb + i, 0)),
                      pl.BlockSpec((None, tr, cols), lambda k, i, c: (k, i, 0))],
            out_specs=pl.BlockSpec((None, tr, cols), lambda k, i, c: (k, i, 0))),
        out_shape=jax.ShapeDtypeStruct((s, half, cols), BF16),
        compiler_params=_params("parallel", "parallel"),
    )(c_idx, g, other)


def _sum_partials(name, part, arrived, place_idx):
    _, half, cols = part.shape
    tr = _div_tile(half, 16, 512)
    nb = half // tr

    def body(s_ref, p_ref, a_ref, o_ref):
        del s_ref
        o_ref[...] = ((p_ref[...].astype(F32) + a_ref[0].astype(F32)) + a_ref[1].astype(F32)) + a_ref[2].astype(F32)

    return pl.pallas_call(
        body, name=name,
        grid_spec=pltpu.PrefetchScalarGridSpec(
            num_scalar_prefetch=1, grid=(nb,),
            in_specs=[pl.BlockSpec((None, tr, cols), lambda i, s: (s[0], i, 0)),
                      pl.BlockSpec((3, tr, cols), lambda i, s: (0, i, 0))],
            out_specs=pl.BlockSpec((tr, cols), lambda i, s: (s[1] * nb + i, 0))),
        out_shape=jax.ShapeDtypeStruct((2 * half, cols), F32),
        compiler_params=_params("parallel"),
    )(place_idx, part, arrived)


def _pack_small(name, wide_rows, ffn_rows, w, dff, n_wide, n_ffn):
    n_in = len(wide_rows) + len(ffn_rows)

    def body(*refs):
        ins, outs = refs[:n_in], refs[n_in:]
        p1 = outs[0]
        p1[...] = jnp.zeros_like(p1)
        row = 0
        for ref, (_, r, m) in zip(ins, wide_rows):
            if m == 1 and r % SUBLANE == 0 and row % SUBLANE == 0:
                p1[row:row + r, :] = ref[...]
                row += r
                continue
            for rr in range(r):
                for mm in range(m):
                    p1[row:row + 1, :] = ref[rr:rr + 1, mm * w:(mm + 1) * w]
                    row += 1
        if ffn_rows:
            p2 = outs[1]
            p2[...] = jnp.zeros_like(p2)
            row = 0
            for ref, arr in zip(ins[len(wide_rows):], ffn_rows):
                r = arr.shape[0]
                p2[row:row + r, :] = ref[...]
                row += r

    shapes = [jax.ShapeDtypeStruct((n_wide, w), F32)] + ([jax.ShapeDtypeStruct((n_ffn, dff), F32)] if ffn_rows else [])
    return pl.pallas_call(
        body, name=name, in_specs=[VMEM_FULL] * n_in, out_specs=[VMEM_FULL] * len(shapes), out_shape=shapes,
        compiler_params=pltpu.CompilerParams(vmem_limit_bytes=VMEM_LIMIT),
    )(*[a for a, _, _ in wide_rows], *ffn_rows)


def _ride_exchange8(packs):
    n = len(packs)

    def copies(rin, rout, send, recv):
        x, y, c, _ = _place()
        me = 4 * x + 2 * y + c
        out = []
        for a in range(n):
            for mask in range(1, 8):
                peer = (x ^ (mask >> 2), y ^ ((mask >> 1) & 1), c ^ (mask & 1))
                out.append(_rcopy(rin[a], rout[a].at[me], send, recv, 8 * a + mask, peer))
        own = [pltpu.make_async_copy(rin[a], rout[a].at[me], send.at[8 * a]) for a in range(n)]
        return out, own

    def start(rin, rout, send, recv):
        remote, own = copies(rin, rout, send, recv)
        for cp in remote + own:
            cp.start()

    def finish(rin, rout, send, recv):
        remote, own = copies(rin, rout, send, recv)
        for cp in remote + own:
            cp.wait()

    outs = [jax.ShapeDtypeStruct((8,) + p.shape, p.dtype) for p in packs]
    return _Rider(packs, outs, {}, 8 * n, start, finish)


def _sum_small(name, slots):
    def body(*refs):
        n = len(refs) // 2
        for r_ref, s_ref in zip(refs[:n], refs[n:]):
            tot = r_ref[0]
            for d in range(1, 8):
                tot = tot + r_ref[d]
            s_ref[...] = tot

    return pl.pallas_call(
        body, name=name, in_specs=[VMEM_FULL] * len(slots), out_specs=[VMEM_FULL] * len(slots),
        out_shape=[jax.ShapeDtypeStruct(s.shape[1:], s.dtype) for s in slots],
        compiler_params=pltpu.CompilerParams(vmem_limit_bytes=VMEM_LIMIT),
    )(*slots)


def _adamw(w, g, m, v):
    m2 = ADAM_B1 * m + (1.0 - ADAM_B1) * g
    v2 = ADAM_B2 * v + (1.0 - ADAM_B2) * (g * g)
    m_hat = m2 / (1.0 - ADAM_B1 ** ADAM_STEP)
    v_hat = v2 / (1.0 - ADAM_B2 ** ADAM_STEP)
    delta = -ADAM_LR * (m_hat / (jnp.sqrt(v_hat) + ADAM_EPS) + ADAM_WD * w)
    return delta, m2, v2


def _adam_big(name, w, g, m, v):
    r, c = w.shape
    tr = 128 if r % 128 == 0 else r

    def body(w_ref, g_ref, m_ref, v_ref, go_ref, d_ref, m2_ref, v2_ref):
        g = g_ref[...]
        go_ref[...] = g
        d_ref[...], m2_ref[...], v2_ref[...] = _adamw(w_ref[...], g, m_ref[...], v_ref[...])

    blk = pl.BlockSpec((tr, c), lambda i: (i, 0))
    return _call(
        body, (w, g, m, v), name=name, grid=(r // tr,), in_specs=[blk] * 4, out_specs=[blk] * 4,
        out_shape=[jax.ShapeDtypeStruct((r, c), F32)] * 4)


def _adam_small(s1, s2, s3, cw_g, fw_g, lb_logits, triples, layout, w):
    n = len(triples)

    def body(*refs):
        s1_ref, s2_ref, s3_ref, cw_ref, fw_ref, lbl_ref = refs[:6]
        prm = refs[6:6 + 3 * n]
        outs = refs[6 + 3 * n:]
        for p, lay in enumerate(layout):
            w_ref, m_ref, v_ref = prm[3 * p:3 * p + 3]
            g_ref, d_ref, m2_ref, v2_ref = outs[4 * p:4 * p + 4]
            if lay[0] in ("wide", "late"):
                _, row, r, pieces = lay
                src = s1_ref if lay[0] == "wide" else s3_ref
                for rr in range(r):
                    for mm in range(pieces):
                        g_ref[rr:rr + 1, mm * w:(mm + 1) * w] = src[row:row + 1, :]
                        row += 1
            elif lay[0] == "ffn":
                _, row, r = lay
                g_ref[...] = s2_ref[row:row + r, :]
            elif lay[0] == "cw":
                g_ref[...] = cw_ref[0:g_ref.shape[0], :]
            elif lay[0] == "fw":
                g_ref[...] = fw_ref[0:g_ref.shape[0], :]
            else:
                s0 = _lower_bound(lbl_ref)
                d0 = s1_ref[lay[1]:lay[1] + 1, :] * s0 * (1.0 - s0)
                g_ref[0:1, :] = d0
                g_ref[1:2, :] = -d0
            d_ref[...], m2_ref[...], v2_ref[...] = _adamw(w_ref[...], g_ref[...], m_ref[...], v_ref[...])

    flat = [a for tr in triples for a in tr]
    shapes = []
    for tr in triples:
        shapes.extend([jax.ShapeDtypeStruct(tr[0].shape, F32)] * 4)
    return pl.pallas_call(
        body, name="adam_small", in_specs=[VMEM_FULL] * (6 + 3 * n), out_specs=[VMEM_FULL] * (4 * n),
        out_shape=shapes, compiler_params=pltpu.CompilerParams(vmem_limit_bytes=VMEM_LIMIT),
    )(s1, s2, s3, cw_g, fw_g, lb_logits, *flat)


def _row_tile(t):
    return 512 if t % 512 == 0 and t >= 2048 else 128


def kernel(x, emb_ln_g, emb_ln_b, w_in, conv_w, conv_b, conv_norm_g, conv_norm_b, lb_logits, hgrn_norm_g, w_out, ln1_g, ln1_b, w_ffn_up, ffn_conv_w, ffn_conv_b, w_ffn_down, ln2_g, ln2_b, loss_target, m_emb_ln_g, m_emb_ln_b, m_w_in, m_conv_w, m_conv_b, m_conv_norm_g, m_conv_norm_b, m_lb_logits, m_hgrn_norm_g, m_w_out, m_ln1_g, m_ln1_b, m_w_ffn_up, m_ffn_conv_w, m_ffn_conv_b, m_w_ffn_down, m_ln2_g, m_ln2_b, v_emb_ln_g, v_emb_ln_b, v_w_in, v_conv_w, v_conv_b, v_conv_norm_g, v_conv_norm_b, v_lb_logits, v_hgrn_norm_g, v_w_out, v_ln1_g, v_ln1_b, v_w_ffn_up, v_ffn_conv_w, v_ffn_conv_b, v_w_ffn_down, v_ln2_g, v_ln2_b):
    depth = w_in.shape[0]
    assert depth == 1 and x.shape[0] == 1
    alpha = (2.0 * depth) ** 0.25
    t, d = x.shape[1], x.shape[2]
    w = d // 2
    dff = ffn_conv_b.shape[1]
    kc = conv_w.shape[1]
    assert w % (2 * LANE) == 0 and dff % (4 * LANE) == 0 and t % 128 == 0
    tm = _row_tile(t)
    tm2 = tm // 2
    tmm = 1024 if t % 1024 == 0 and t >= 2048 else tm
    cb = 2 * LANE
    cbf = 4 * LANE
    tb = tm
    nh = w // LANE
    hpb = 4 if nh % 4 == 0 else 2

    xi = lax.axis_index("x")
    yi = lax.axis_index("y")
    ci = lax.axis_index("c")
    chip = 2 * xi + yi
    c_idx = jnp.reshape(ci, (1,)).astype(jnp.int32)
    chip_idx = jnp.reshape(chip, (1,)).astype(jnp.int32)
    place_idx = jnp.stack([chip, ci]).astype(jnp.int32)

    x2 = x[0]
    tgt = loss_target[0]
    g0, b0 = emb_ln_g.reshape(1, d), emb_ln_b.reshape(1, d)
    w_in2, w_out2, w_up2, w_dn2 = w_in[0], w_out[0], w_ffn_up[0], w_ffn_down[0]
    cw2, fw2 = conv_w[0], ffn_conv_w[0]

    b_in = _place_shard(w_in2, "place_w_in", chip_idx, BF16)
    b_out = _place_shard(w_out2, "place_w_out", chip_idx, BF16)
    b_up = _place_shard(w_up2, "place_w_up", chip_idx, BF16)
    b_dn = _place_shard(w_dn2, "place_w_down", chip_idx, BF16)
    b_cw = _place_shard(_pad_rows(cw2), "place_conv_w", chip_idx, F32)
    b_fw = _place_shard(_pad_rows(fw2), "place_ffn_conv_w", chip_idx, F32)
    h0b, (b_in,) = _ln0(x2, g0, b0, tm, rider=_ride_gather_ici([b_in], [(0, 1, 8)]))
    first = _run("gather_first_ici", _ride_gather_ici([b_in, b_cw, b_fw], [(1, 8, 8), None, None]))
    w_in3, cw_full3, fw_full3 = _run("gather_first_d2d", _ride_gather_d2d(first))
    cw_full = _unshard_cols(cw_full3)[:kc]
    fw_full = _unshard_cols(fw_full3)[:fw2.shape[0]]

    p3, (b_out, b_up) = _proj("in_proj", h0b, w_in3, 6, 2 * tmm if t % (2 * tmm) == 0 else tmm, w // 2,
                              rider=_ride_gather_ici([b_out, b_up], [None, (0, 1, 4)]))
    (cat, u1), (w_out3, b_up) = _conv_fwd(
        p3, cw_full, conv_b, conv_norm_g, conv_norm_b, tm2, cb,
        rider=_ride_both(_ride_gather_d2d([b_out]), _ride_gather_ici([b_up], [(1, 2, 4)])))
    w_out_full = w_out3.reshape(d, d)
    (cat, o_pre, states), got = _hgrn_fwd(p3, lb_logits, hgrn_norm_g, cat, tb, hpb,
                                          rider=_ride_gather_ici([b_up], [(2, 4, 4)]))
    (xhat1, h1b, rstd1), (w_up3,) = _mix_ln1(cat, w_out_full, x2, g0, b0, ln1_g, ln1_b, alpha, tm2,
                                             rider=_ride_gather_d2d(got))
    hh3, got = _proj("ffn_up", h1b, w_up3, 2, tmm, dff // 4, rider=_ride_gather_ici([b_dn]))
    act, (w_dn3,) = _ffn_act_fwd(hh3, fw_full, ffn_conv_b, tm, cbf, rider=_ride_gather_d2d(got))
    ks = dff // N_CHIPS
    ffn = _wgrad("ffn_down", act, w_dn3, (t, d), (t // tmm, 1, N_CHIPS),
                 pl.BlockSpec((tmm, ks), lambda i, j, k: (i, k)),
                 pl.BlockSpec((None, ks, d), lambda i, j, k: (k, 0, 0)),
                 pl.BlockSpec((tmm, d), lambda i, j, k: (i, 0)), dot=_dot)
    dz2, dz2b, dg2, db2, loss_row = _ln2_loss(ffn, xhat1, tgt, ln1_g, ln1_b, ln2_g, ln2_b, alpha, tm2)

    dact = _proj_t("ffn_down_t", dz2b, w_dn3.reshape(dff, d), tmm, ks)
    dhh3, dfw, dfb = _ffn_act_bwd(dact, hh3, fw_full, ffn_conv_b, tm, cbf)
    tt = 2 * tmm if t % (2 * tmm) == 0 else tmm
    d_w_dn = _wgrad("wgrad_down", act, dz2b, (N_CHIPS, ks, d), (N_CHIPS, 2, t // tt),
                    pl.BlockSpec((tt, ks), lambda s, j, k: (k, s)),
                    pl.BlockSpec((tt, d // 2), lambda s, j, k: (k, j)),
                    pl.BlockSpec((None, ks, d // 2), lambda s, j, k: (s, 0, j)))
    wu = 2 * dff // N_CHIPS
    tnu = wu // 2
    per_sec_u = dff // tnu
    pre1, (arr_dn,) = _wgrad(
        "up_t", dhh3, w_up3, (t, d), (t // tmm, 1, 2 * N_CHIPS),
        pl.BlockSpec((None, tmm, tnu), lambda i, j, k: (k // per_sec_u, i, k % per_sec_u)),
        pl.BlockSpec((None, d, tnu), lambda i, j, k: (k // 2, 0, k % 2)),
        pl.BlockSpec((tmm, d), lambda i, j, k: (i, 0)), dot=_dot_nt, rider=_ride_swap([d_w_dn]))
    dz1, dz1b, dg1, db1 = _ln1_bwd(pre1, dz2, xhat1, rstd1, ln1_g, alpha, tm2)
    part_dn = _add_halves("add_halves_w_down", d_w_dn, arr_dn, c_idx)
    d_w_up, (land_dn,) = _wgrad(
        "wgrad_up", h1b, dhh3, (N_CHIPS, d, wu), (N_CHIPS, 2, 2, t // tt),
        pl.BlockSpec((tt, d // 2), lambda s, r, j, k: (k, r)),
        pl.BlockSpec((None, tt, tnu), lambda s, r, j, k: ((2 * s + j) // per_sec_u, k, (2 * s + j) % per_sec_u)),
        pl.BlockSpec((None, d // 2, tnu), lambda s, r, j, k: (s, r, j)), rider=_ride_send_partials([part_dn]))
    dcat = _proj_t("out_proj_t", dz1b, w_out_full, tmm, d // 2)
    d_w_out = _wgrad("wgrad_out", cat, dz1b, (d, d), (2, 2, t // tt),
                     pl.BlockSpec((tt, d // 2), lambda r, j, k: (k, r)),
                     pl.BlockSpec((tt, d // 2), lambda r, j, k: (k, j)),
                     pl.BlockSpec((d // 2, d // 2), lambda r, j, k: (r, j))).reshape(N_CHIPS, d // N_CHIPS, d)
    du1, dcng, dcnb = _conv_norm_bwd(dcat, u1, conv_norm_g, conv_norm_b, tm)
    (dp3, dcw, dcb), (arr_up, arr_out) = _conv_bwd(du1, p3, cw_full, tm2, cb, rider=_ride_swap([d_w_up, d_w_out]))
    part_up = _add_halves("add_halves_w_up", d_w_up, arr_up, c_idx)
    part_out = _add_halves("add_halves_w_out", d_w_out, arr_out, c_idx)
    (dp3, dlb, dhg), (land_up, land_out) = _hgrn_bwd(
        p3, lb_logits, hgrn_norm_g, o_pre, states, dcat, dp3, tb, hpb,
        rider=_ride_send_partials([part_up, part_out], [(0, 3, 4), None]))
    kpad = dcw.shape[0]
    wide = [(dcw, kpad, 1), (dg1, 1, 2), (db1, 1, 2), (dg2, 1, 2), (db2, 1, 2),
            (dcb, 1, 1), (dcng, 1, 1), (dcnb, 1, 1), (dlb, 1, 1), (dhg, 1, 1)]
    n_wide = -(-sum(r * m for _, r, m in wide) // SUBLANE) * SUBLANE
    packs = _pack_small("pack_small", wide, [dfw, dfb], w, dff, n_wide, 2 * SUBLANE)
    d_w_in, (land_up, slots1, slots2) = _wgrad_in(
        h0b, dp3, N_CHIPS, tt,
        rider=_ride_both(_ride_send_partials([part_up], [(3, 4, 4)], into=[land_up]), _ride_exchange8(packs)))
    s1, s2 = _sum_small("sum_small", [slots1, slots2])
    (arr_in,) = _run("swap_w_in", _ride_swap([d_w_in]))
    part_in = _add_halves("add_halves_w_in", d_w_in, arr_in, c_idx)
    h_out, h_up, h_dn = [
        _sum_partials("sum_partials_" + nm, p, a, place_idx)
        for nm, p, a in (("w_out", part_out, land_out), ("w_up", part_up, land_up), ("w_down", part_dn, land_dn))]
    pre0, (land_in, g_w_out, g_w_up, g_w_dn) = _in_t(
        dp3, w_in3, tmm, rider=_ride_both(_ride_send_partials([part_in]), _ride_join([h_out, h_up, h_dn])))
    dx, dg0, db0 = _ln0_bwd(pre0, dz1, x2, g0, alpha, tm2)
    h_in = _sum_partials("sum_partials_w_in", part_in, land_in, place_idx)
    (g_w_in,) = _run("join_w_in", _ride_join([h_in]))

    late = _pack_small("pack_late", [(dg0, 1, 2), (db0, 1, 2)], [], w, dff, SUBLANE, 0)
    (s3,) = _sum_small("sum_late", _run("exchange_late", _ride_exchange8(late)))
    cw_g = lax.dynamic_slice_in_dim(s1[0:kpad], chip * (w // N_CHIPS), w // N_CHIPS, axis=1)
    fw_g = lax.dynamic_slice_in_dim(s2[0:SUBLANE], chip * (dff // N_CHIPS), dff // N_CHIPS, axis=1)

    small = [
        (g0, m_emb_ln_g.reshape(1, d), v_emb_ln_g.reshape(1, d)), (b0, m_emb_ln_b.reshape(1, d), v_emb_ln_b.reshape(1, d)),
        (cw2, m_conv_w[0], v_conv_w[0]), (conv_b, m_conv_b, v_conv_b),
        (conv_norm_g, m_conv_norm_g, v_conv_norm_g), (conv_norm_b, m_conv_norm_b, v_conv_norm_b),
        (lb_logits, m_lb_logits, v_lb_logits), (hgrn_norm_g, m_hgrn_norm_g, v_hgrn_norm_g),
        (ln1_g, m_ln1_g, v_ln1_g), (ln1_b, m_ln1_b, v_ln1_b),
        (fw2, m_ffn_conv_w[0], v_ffn_conv_w[0]), (ffn_conv_b, m_ffn_conv_b, v_ffn_conv_b),
        (ln2_g, m_ln2_g, v_ln2_g), (ln2_b, m_ln2_b, v_ln2_b),
    ]
    r0 = kpad
    layout = [("late", 0, 1, 2), ("late", 2, 1, 2), ("cw",), ("wide", r0 + 8, 1, 1), ("wide", r0 + 9, 1, 1),
              ("wide", r0 + 10, 1, 1), ("lb", r0 + 11), ("wide", r0 + 12, 1, 1), ("wide", r0, 1, 2),
              ("wide", r0 + 2, 1, 2), ("fw",), ("ffn", SUBLANE, 1), ("wide", r0 + 4, 1, 2), ("wide", r0 + 6, 1, 2)]
    so = _adam_small(s1, s2, s3, cw_g, fw_g, lb_logits, small, layout, w)
    sm = {nm: so[4 * i:4 * i + 4] for i, nm in enumerate(
        ["emb_ln_g", "emb_ln_b", "conv_w", "conv_b", "conv_norm_g", "conv_norm_b", "lb_logits", "hgrn_norm_g",
         "ln1_g", "ln1_b", "ffn_conv_w", "ffn_conv_b", "ln2_g", "ln2_b"])}
    bigs = {}
    for nm, wt, g, m, v in (("w_in", w_in2, g_w_in, m_w_in[0], v_w_in[0]), ("w_out", w_out2, g_w_out, m_w_out[0], v_w_out[0]),
                            ("w_ffn_up", w_up2, g_w_up, m_w_ffn_up[0], v_w_ffn_up[0]),
                            ("w_ffn_down", w_dn2, g_w_dn, m_w_ffn_down[0], v_w_ffn_down[0])):
        bigs[nm] = tuple(_adam_big("adam_" + nm, wt, g, m, v))

    loss = lax.psum(loss_row[0, 0], ("x", "y", "c"))

    order = ["emb_ln_g", "emb_ln_b", "w_in", "conv_w", "conv_b", "conv_norm_g", "conv_norm_b", "lb_logits",
             "hgrn_norm_g", "w_out", "ln1_g", "ln1_b", "w_ffn_up", "ffn_conv_w", "ffn_conv_b", "w_ffn_down",
             "ln2_g", "ln2_b"]
    shapes = dict(emb_ln_g=emb_ln_g.shape, emb_ln_b=emb_ln_b.shape, w_in=w_in.shape, conv_w=conv_w.shape,
                  w_out=w_out.shape, w_ffn_up=w_ffn_up.shape, ffn_conv_w=ffn_conv_w.shape, w_ffn_down=w_ffn_down.shape)
    outs = [loss, dx.reshape(x.shape)]
    for which in range(4):
        for nm in order:
            a = bigs[nm][which] if nm in bigs else sm[nm][which]
            outs.append(a.reshape(shapes[nm]) if nm in shapes else a)
    return tuple(outs)


def _pad_rows(a):
    k = a.shape[0]
    kp = -(-k // 16) * 16
    return jnp.pad(a, ((0, kp - k), (0, 0)))


def _unshard_cols(a3):
    s, k, c = a3.shape
    return jnp.transpose(a3, (1, 0, 2)).reshape(k, s * c)
```

```python
import functools

import jax
import jax.numpy as jnp
from jax import lax
from jax.experimental import pallas as pl
from jax.experimental.pallas import tpu as pltpu

F32 = jnp.float32
BF16 = jnp.bfloat16

LN_EPS = 1e-5
RMS_EPS = 1e-6
LANE = 128
SUBLANE = 8
CHUNK = 64
SUB = 8
HALO = 32
FHALO = 8
ROWS = 64
N_CHIPS = 4
VMEM_LIMIT = 56 << 20
NEG_BIG = -1e30

ADAM_LR = 0.001
ADAM_B1 = 0.9
ADAM_B2 = 0.999
ADAM_EPS = 1e-08
ADAM_WD = 0.01
ADAM_STEP = 10

MESH = pl.DeviceIdType.MESH
HBM = pl.BlockSpec(memory_space=pl.ANY)
VMEM_FULL = pl.BlockSpec(memory_space=pltpu.VMEM)


def _params(*sem):
    return pltpu.CompilerParams(dimension_semantics=sem, vmem_limit_bytes=VMEM_LIMIT)


class _Rider:
    def __init__(self, ins, outs, aliases, n_sems, start, finish):
        self.ins, self.outs, self.aliases = list(ins), list(outs), dict(aliases)
        self.n_sems, self.start, self.finish = n_sems, start, finish


def _call(body, args, *, name, grid, in_specs, out_specs, out_shape, scratch_shapes=(), aliases=None, rider=None,
          prefetch=()):
    many = isinstance(out_shape, (list, tuple))
    shapes = list(out_shape) if many else [out_shape]
    ospecs = list(out_specs) if many else [out_specs]
    npf = len(prefetch)
    aliases = {npf + i: o for i, o in (aliases or {}).items()}
    sem = ("arbitrary",) * len(grid)
    n_in, n_out, n_scr = len(args), len(shapes), len(scratch_shapes)
    if rider is None:
        res = pl.pallas_call(
            body, name=name,
            grid_spec=pltpu.PrefetchScalarGridSpec(
                num_scalar_prefetch=npf, grid=grid, in_specs=list(in_specs), out_specs=ospecs,
                scratch_shapes=list(scratch_shapes)),
            out_shape=shapes, input_output_aliases=aliases, compiler_params=_params(*sem))(*prefetch, *args)
        return res if many else res[0]
    nri, nro = len(rider.ins), len(rider.outs)

    def wrapped(*refs):
        pre, refs = refs[:npf], refs[npf:]
        ins, rin = refs[:n_in], refs[n_in:n_in + nri]
        o0 = n_in + nri
        outs, rout = refs[o0:o0 + n_out], refs[o0 + n_out:o0 + n_out + nro]
        s0 = o0 + n_out + nro
        scr, (send, recv) = refs[s0:s0 + n_scr], refs[s0 + n_scr:]
        ids = [pl.program_id(a) for a in range(len(grid))]
        first = functools.reduce(jnp.logical_and, [i == 0 for i in ids])
        last = functools.reduce(jnp.logical_and, [i == g - 1 for i, g in zip(ids, grid)])

        @pl.when(first)
        def _():
            rider.start(rin, rout, send, recv)

        body(*pre, *ins, *outs, *scr)

        @pl.when(last)
        def _():
            rider.finish(rin, rout, send, recv)

    for ri, ro in rider.aliases.items():
        aliases[npf + n_in + ri] = n_out + ro
    res = pl.pallas_call(
        wrapped, name=name,
        grid_spec=pltpu.PrefetchScalarGridSpec(
            num_scalar_prefetch=npf, grid=grid, in_specs=list(in_specs) + [HBM] * nri,
            out_specs=ospecs + [HBM] * nro,
            scratch_shapes=list(scratch_shapes) + [pltpu.SemaphoreType.DMA((rider.n_sems,)),
                                                   pltpu.SemaphoreType.DMA((rider.n_sems,))]),
        out_shape=shapes + rider.outs,
        input_output_aliases=aliases, compiler_params=_params(*sem))(*prefetch, *args, *rider.ins)
    main, extra = res[:n_out], list(res[n_out:])
    return (list(main) if many else main[0]), extra


def _div_tile(n, mult, cap):
    best = n
    for t in range(mult, min(n, cap) + 1, mult):
        if n % t == 0:
            best = t
    return best


def _sigmoid(x):
    return 1.0 / (1.0 + jnp.exp(-x))


def _ln_stats(x):
    mu = jnp.mean(x, axis=-1, keepdims=True)
    xc = x - mu
    var = jnp.mean(xc * xc, axis=-1, keepdims=True)
    rstd = lax.rsqrt(var + LN_EPS)
    return xc * rstd, rstd


def _ln_bwd(dy, xhat, rstd, g):
    dyg = dy * g
    m1 = jnp.mean(dyg, axis=-1, keepdims=True)
    m2 = jnp.mean(dyg * xhat, axis=-1, keepdims=True)
    return rstd * (dyg - m1 - xhat * m2)


def _dot_nt(a, b):
    return lax.dot_general(a, b, (((1,), (1,)), ((), ())), preferred_element_type=F32)


def _dot_tn(a, b):
    return lax.dot_general(a, b, (((0,), (0,)), ((), ())), preferred_element_type=F32)


def _dot(a, b):
    return jnp.dot(a, b, preferred_element_type=F32)


def _dot3(m, x):
    mb = m.astype(BF16)
    x1 = x.astype(BF16)
    r1 = x - x1.astype(F32)
    x2 = r1.astype(BF16)
    x3 = (r1 - x2.astype(F32)).astype(BF16)
    return _dot(mb, x1) + _dot(mb, x2) + _dot(mb, x3)


def _place_shard(x, name, chip_idx, dtype):
    r, c = x.shape
    tr = _div_tile(r, 16, 512)

    def body(s_ref, x_ref, o_ref):
        del s_ref
        o_ref[...] = x_ref[...].astype(dtype)

    return pl.pallas_call(
        body, name=name,
        grid_spec=pltpu.PrefetchScalarGridSpec(
            num_scalar_prefetch=1, grid=(r // tr,),
            in_specs=[pl.BlockSpec((tr, c), lambda i, s: (i, 0))],
            out_specs=pl.BlockSpec((None, tr, c), lambda i, s: (s[0], i, 0))),
        out_shape=jax.ShapeDtypeStruct((N_CHIPS, r, c), dtype),
        compiler_params=_params("parallel"),
    )(chip_idx, x)


def _ln0(x, g, b, tm, rider=None):
    t, d = x.shape

    def body(x_ref, g_ref, b_ref, o_ref):
        xh, _ = _ln_stats(x_ref[...])
        o_ref[...] = (xh * g_ref[...] + b_ref[...]).astype(BF16)

    row = pl.BlockSpec((1, d), lambda i: (0, 0))
    return _call(
        body, (x, g, b), name="ln0", grid=(t // tm,),
        in_specs=[pl.BlockSpec((tm, d), lambda i: (i, 0)), row, row],
        out_specs=pl.BlockSpec((tm, d), lambda i: (i, 0)),
        out_shape=jax.ShapeDtypeStruct((t, d), BF16), rider=rider)


def _proj(name, a, w3, n_sec, tm, tn, rider=None):
    m, k = a.shape
    s, _, ws = w3.shape
    sec_w = s * ws // n_sec
    nj = ws // tn
    per_sec = sec_w // tn

    def body(a_ref, w_ref, o_ref):
        o_ref[...] = _dot(a_ref[...], w_ref[...])

    return _call(
        body, (a, w3), name=name, grid=(s * nj, m // tm),
        in_specs=[pl.BlockSpec((tm, k), lambda j, i: (i, 0)),
                  pl.BlockSpec((None, k, tn), lambda j, i: (j // nj, 0, j % nj))],
        out_specs=pl.BlockSpec((None, tm, tn), lambda j, i: (j // per_sec, i, j % per_sec)),
        out_shape=jax.ShapeDtypeStruct((n_sec, m, sec_w), F32), rider=rider)


def _proj_own(name, a, w_own, n_sec, n_shards, tm, tn, chip_idx, rider=None):
    m, k = a.shape
    ws = w_own.shape[1]
    sec_w = n_shards * ws // n_sec
    nj = ws // tn
    per_sec = sec_w // tn

    def body(c_ref, a_ref, w_ref, o_ref):
        del c_ref
        o_ref[...] = _dot(a_ref[...], w_ref[...].astype(BF16))

    return _call(
        body, (a, w_own), name=name, grid=(nj, m // tm), prefetch=(chip_idx,),
        in_specs=[pl.BlockSpec((tm, k), lambda j, i, c: (i, 0)), pl.BlockSpec((k, tn), lambda j, i, c: (0, j))],
        out_specs=pl.BlockSpec((None, tm, tn),
                               lambda j, i, c: ((nj * c[0] + j) // per_sec, i, (nj * c[0] + j) % per_sec)),
        out_shape=jax.ShapeDtypeStruct((n_sec, m, sec_w), F32), rider=rider)


def _proj_rest(name, a, w3, p3, tm, tn, chip_idx, rider=None):
    m, k = a.shape
    s, _, ws = w3.shape
    _, _, sec_w = p3.shape
    nj = ws // tn
    per_sec = sec_w // tn

    def block(j, c):
        return nj * ((c[0] + 1 + j // nj) % s) + j % nj

    def body(c_ref, a_ref, w_ref, p_in, o_ref):
        del c_ref, p_in
        o_ref[...] = _dot(a_ref[...], w_ref[...])

    return _call(
        body, (a, w3, p3), name=name, grid=((s - 1) * nj, m // tm), prefetch=(chip_idx,),
        in_specs=[pl.BlockSpec((tm, k), lambda j, i, c: (i, 0)),
                  pl.BlockSpec((None, k, tn), lambda j, i, c: (block(j, c) // nj, 0, block(j, c) % nj)), HBM],
        out_specs=pl.BlockSpec((None, tm, tn), lambda j, i, c: (block(j, c) // per_sec, i, block(j, c) % per_sec)),
        out_shape=jax.ShapeDtypeStruct(p3.shape, F32), aliases={2: 0}, rider=rider)


def _proj_t(name, a, w, tm, tn, rider=None):
    m, k = a.shape
    n = w.shape[0]

    def body(a_ref, w_ref, o_ref):
        o_ref[...] = _dot_nt(a_ref[...], w_ref[...])

    return _call(
        body, (a, w), name=name, grid=(n // tn, m // tm),
        in_specs=[pl.BlockSpec((tm, k), lambda j, i: (i, 0)),
                  pl.BlockSpec((tn, k), lambda j, i: (j, 0))],
        out_specs=pl.BlockSpec((tm, tn), lambda j, i: (i, j)),
        out_shape=jax.ShapeDtypeStruct((m, n), F32), rider=rider)


def _wgrad(name, a, b, out_shape, grid, a_spec, b_spec, o_spec, rider=None, dot=_dot_tn):
    nt = len(grid) - 1

    def body(a_ref, b_ref, o_ref):
        t = pl.program_id(nt)
        prod = dot(a_ref[...], b_ref[...])

        @pl.when(t == 0)
        def _():
            o_ref[...] = prod

        @pl.when(t > 0)
        def _():
            o_ref[...] += prod

    return _call(
        body, (a, b), name=name, grid=grid, in_specs=[a_spec, b_spec], out_specs=o_spec,
        out_shape=jax.ShapeDtypeStruct(out_shape, F32), rider=rider)


def _wgrad_in(name, h0b, dp3, n_shards, tt, half_idx, rider=None):
    t, d = h0b.shape
    n_sec, _, sec_w = dp3.shape
    ws = n_sec * sec_w // n_shards
    tn = sec_w // 2
    nq = ws // tn
    tr = d // 2

    def body(*refs):
        a_ref, b_refs, o_ref = refs[1], refs[2:2 + nq], refs[2 + nq]
        k = pl.program_id(1)
        a = a_ref[...]
        prods = [_dot_tn(a, b_ref[...]) for b_ref in b_refs]

        @pl.when(k == 0)
        def _():
            for q in range(nq):
                o_ref[:, q * tn:(q + 1) * tn] = prods[q]

        @pl.when(k > 0)
        def _():
            for q in range(nq):
                o_ref[:, q * tn:(q + 1) * tn] += prods[q]

    def b_spec(q):
        return pl.BlockSpec((None, tt, tn), lambda s, k, h: ((nq * s + q) // 2, k, (nq * s + q) % 2))

    return _call(
        body, (h0b,) + (dp3,) * nq, name=name, grid=(n_shards, t // tt), prefetch=(half_idx,),
        in_specs=[pl.BlockSpec((tt, tr), lambda s, k, h: (k, h[0]))] + [b_spec(q) for q in range(nq)],
        out_specs=pl.BlockSpec((None, tr, ws), lambda s, k, h: (s, 0, 0)),
        out_shape=jax.ShapeDtypeStruct((n_shards, tr, ws), F32), rider=rider)


def _in_t(dp3, w_in3, tm, rider=None):
    _, t, sec_w = dp3.shape
    s, d, ws = w_in3.shape
    tk = sec_w // 2
    nq = ws // tk

    def body(*refs):
        a_refs, w_ref, o_ref = refs[:nq], refs[nq], refs[nq + 1]
        k = pl.program_id(1)
        prod = _dot_nt(a_refs[0][...], w_ref[:, 0:tk])
        for q in range(1, nq):
            prod = prod + _dot_nt(a_refs[q][...], w_ref[:, q * tk:(q + 1) * tk])

        @pl.when(k == 0)
        def _():
            o_ref[...] = prod

        @pl.when(k > 0)
        def _():
            o_ref[...] += prod

    def a_spec(q):
        return pl.BlockSpec((None, tm, tk), lambda i, k: ((nq * k + q) // 2, i, (nq * k + q) % 2))

    return _call(
        body, (dp3,) * nq + (w_in3,), name="in_t", grid=(t // tm, s),
        in_specs=[a_spec(q) for q in range(nq)] + [pl.BlockSpec((None, d, ws), lambda i, k: (k, 0, 0))],
        out_specs=pl.BlockSpec((tm, d), lambda i, k: (i, 0)),
        out_shape=jax.ShapeDtypeStruct((t, d), F32), rider=rider)


def _mix_ln1(cat, w_out, x, g0, b0, g1, b1, alpha, tm, rider=None):
    t, d = x.shape

    def body(cat_ref, w_ref, x_ref, g0_ref, b0_ref, g1_ref, b1_ref, xh_ref, h1b_ref, rstd_ref):
        mix = _dot(cat_ref[...], w_ref[...])
        xh0, _ = _ln_stats(x_ref[...])
        z1 = alpha * (xh0 * g0_ref[...] + b0_ref[...]) + mix
        xh1, rstd1 = _ln_stats(z1)
        xh_ref[...] = xh1
        h1b_ref[...] = (xh1 * g1_ref[...] + b1_ref[...]).astype(BF16)
        rstd_ref[...] = rstd1

    row = pl.BlockSpec((1, d), lambda i: (0, 0))
    blk = pl.BlockSpec((tm, d), lambda i: (i, 0))
    return _call(
        body, (cat, w_out, x, g0, b0, g1, b1), name="mix_ln1", grid=(t // tm,),
        in_specs=[blk, pl.BlockSpec((d, d), lambda i: (0, 0)), blk, row, row, row, row],
        out_specs=[blk, blk, pl.BlockSpec((tm, 1), lambda i: (i, 0))],
        out_shape=[jax.ShapeDtypeStruct((t, d), F32), jax.ShapeDtypeStruct((t, d), BF16),
                   jax.ShapeDtypeStruct((t, 1), F32)], rider=rider)


def _ln2_loss(ffn, xhat1, tgt, g1, b1, g2, b2, alpha, tm):
    t, d = xhat1.shape
    ni = t // tm
    inv_d = 1.0 / d

    def body(ffn_ref, xh1_ref, tgt_ref, g1_ref, b1_ref, g2_ref, b2_ref,
             dz2_ref, dz2b_ref, dg2_ref, db2_ref, loss_ref, lrow):
        i = pl.program_id(0)
        h1 = xh1_ref[...] * g1_ref[...] + b1_ref[...]
        xh2, rstd2 = _ln_stats(alpha * h1 + ffn_ref[...])
        g2v = g2_ref[...]
        diff = xh2 * g2v + b2_ref[...] - tgt_ref[...]
        dh2 = diff * inv_d
        sq = jnp.sum(diff * diff, axis=0, keepdims=True)
        dg = jnp.sum(dh2 * xh2, axis=0, keepdims=True)
        db = jnp.sum(dh2, axis=0, keepdims=True)

        @pl.when(i == 0)
        def _():
            lrow[...] = sq
            dg2_ref[...] = dg
            db2_ref[...] = db

        @pl.when(i > 0)
        def _():
            lrow[...] += sq
            dg2_ref[...] += dg
            db2_ref[...] += db

        dz2 = _ln_bwd(dh2, xh2, rstd2, g2v)
        dz2_ref[...] = dz2
        dz2b_ref[...] = dz2.astype(BF16)

        @pl.when(i == ni - 1)
        def _():
            tot = jnp.sum(lrow[...], axis=-1, keepdims=True) * (0.5 * inv_d)
            loss_ref[...] = jnp.broadcast_to(tot, (1, LANE))

    row = pl.BlockSpec((1, d), lambda i: (0, 0))
    blk = pl.BlockSpec((tm, d), lambda i: (i, 0))
    return _call(
        body, (ffn, xhat1, tgt, g1, b1, g2, b2), name="ln2_loss", grid=(ni,),
        in_specs=[blk, blk, blk, row, row, row, row],
        out_specs=[blk, blk, row, row, pl.BlockSpec((1, LANE), lambda i: (0, 0))],
        out_shape=[jax.ShapeDtypeStruct((t, d), F32), jax.ShapeDtypeStruct((t, d), BF16),
                   jax.ShapeDtypeStruct((1, d), F32), jax.ShapeDtypeStruct((1, d), F32),
                   jax.ShapeDtypeStruct((1, LANE), F32)],
        scratch_shapes=[pltpu.VMEM((1, d), F32)])


def _ln1_bwd(pre, dz2, xhat1, rstd1, g1, alpha, tm):
    t, d = dz2.shape

    def body(pre_ref, dz2_ref, xh_ref, rstd_ref, g_ref, dz1_ref, dz1b_ref, dg_ref, db_ref):
        i = pl.program_id(0)
        dh1 = alpha * dz2_ref[...] + pre_ref[...]
        xh = xh_ref[...]
        dg = jnp.sum(dh1 * xh, axis=0, keepdims=True)
        db = jnp.sum(dh1, axis=0, keepdims=True)

        @pl.when(i == 0)
        def _():
            dg_ref[...] = dg
            db_ref[...] = db

        @pl.when(i > 0)
        def _():
            dg_ref[...] += dg
            db_ref[...] += db

        dz1 = _ln_bwd(dh1, xh, rstd_ref[...], g_ref[...])
        dz1_ref[...] = dz1
        dz1b_ref[...] = dz1.astype(BF16)

    row = pl.BlockSpec((1, d), lambda i: (0, 0))
    blk = pl.BlockSpec((tm, d), lambda i: (i, 0))
    return _call(
        body, (pre, dz2, xhat1, rstd1, g1), name="ln1_bwd", grid=(t // tm,),
        in_specs=[blk, blk, blk, pl.BlockSpec((tm, 1), lambda i: (i, 0)), row],
        out_specs=[blk, blk, row, row],
        out_shape=[jax.ShapeDtypeStruct((t, d), F32), jax.ShapeDtypeStruct((t, d), BF16),
                   jax.ShapeDtypeStruct((1, d), F32), jax.ShapeDtypeStruct((1, d), F32)])


def _ln0_bwd(pre, dz1, x, g0, alpha, tm):
    t, d = x.shape

    def body(pre_ref, dz1_ref, x_ref, g_ref, dx_ref, dg_ref, db_ref):
        i = pl.program_id(0)
        dh0 = alpha * dz1_ref[...] + pre_ref[...]
        xh, rstd = _ln_stats(x_ref[...])
        dg = jnp.sum(dh0 * xh, axis=0, keepdims=True)
        db = jnp.sum(dh0, axis=0, keepdims=True)

        @pl.when(i == 0)
        def _():
            dg_ref[...] = dg
            db_ref[...] = db

        @pl.when(i > 0)
        def _():
            dg_ref[...] += dg
            db_ref[...] += db

        dx_ref[...] = _ln_bwd(dh0, xh, rstd, g_ref[...])

    row = pl.BlockSpec((1, d), lambda i: (0, 0))
    blk = pl.BlockSpec((tm, d), lambda i: (i, 0))
    return _call(
        body, (pre, dz1, x, g0), name="ln0_bwd", grid=(t // tm,),
        in_specs=[blk, blk, blk, row], out_specs=[blk, row, row],
        out_shape=[jax.ShapeDtypeStruct((t, d), F32), jax.ShapeDtypeStruct((1, d), F32),
                   jax.ShapeDtypeStruct((1, d), F32)])


def _shift_copies(ext, shifted):
    n = shifted.shape[1]
    for p in range(1, SUBLANE):
        shifted[p - 1] = ext[pl.ds(p, n), :]


def _window(ext, shifted, start, rows):
    p = start % SUBLANE
    if p == 0:
        return ext[pl.ds(start, rows), :]
    return shifted[p - 1, pl.ds(start - p, rows), :]


def _conv_fwd(p3, conv_w, conv_b, cn_g, cn_b, tc, cb, rider=None):
    _, t, w = p3.shape
    kk = conv_w.shape[0]
    off = HALO - (kk - 1)
    hb = tc // HALO

    def body(a_ref, g_ref, ap_ref, gp_ref, w_ref, b_ref, ng_ref, nb_ref, cat_ref, u1_ref, ext, sh):
        i = pl.program_id(1)
        ext[pl.ds(HALO, tc), :] = a_ref[...] * _sigmoid(g_ref[...])
        prev = ap_ref[...] * _sigmoid(gp_ref[...])
        ext[pl.ds(0, HALO), :] = jnp.where(i > 0, prev, 0.0)
        _shift_copies(ext, sh)
        for r in range(tc // ROWS):
            acc = jnp.broadcast_to(b_ref[...], (ROWS, cb))
            for k in range(kk):
                acc = acc + w_ref[k:k + 1, :] * _window(ext, sh, r * ROWS + off + k, ROWS)
            u1_ref[pl.ds(r * ROWS, ROWS), :] = acc
            for g in range(cb // LANE):
                sl = slice(g * LANE, (g + 1) * LANE)
                xh, _ = _ln_stats(acc[:, sl])
                u2 = xh * ng_ref[:, sl] + nb_ref[:, sl]
                cat_ref[pl.ds(r * ROWS, ROWS), sl] = (u2 * _sigmoid(u2)).astype(BF16)

    cur = lambda sec: pl.BlockSpec((None, tc, cb), lambda j, i: (sec, i, j))
    prev = lambda sec: pl.BlockSpec((None, HALO, cb), lambda j, i: (sec, jnp.maximum(i * hb - 1, 0), j))
    row = pl.BlockSpec((1, cb), lambda j, i: (0, j))
    return _call(
        body, (p3, p3, p3, p3, conv_w, conv_b, cn_g, cn_b), name="conv_fwd", grid=(w // cb, t // tc),
        in_specs=[cur(0), cur(1), prev(0), prev(1), pl.BlockSpec((kk, cb), lambda j, i: (0, j)), row, row, row],
        out_specs=[pl.BlockSpec((tc, cb), lambda j, i: (i, j)), pl.BlockSpec((tc, cb), lambda j, i: (i, j))],
        out_shape=[jax.ShapeDtypeStruct((t, 2 * w), BF16), jax.ShapeDtypeStruct((t, w), F32)],
        scratch_shapes=[pltpu.VMEM((tc + HALO, cb), F32),
                        pltpu.VMEM((SUBLANE - 1, tc + HALO - SUBLANE, cb), F32)], rider=rider)


def _conv_norm_bwd(dcat, u1, cn_g, cn_b, tc):
    t, w = u1.shape

    def body(du_ref, u1_ref, ng_ref, nb_ref, du1_ref, dg_ref, db_ref):
        i = pl.program_id(0)
        for g in range(w // LANE):
            sl = slice(g * LANE, (g + 1) * LANE)
            ng = ng_ref[:, sl]
            xh, rstd = _ln_stats(u1_ref[:, sl])
            u2 = xh * ng + nb_ref[:, sl]
            sg = _sigmoid(u2)
            du2 = du_ref[:, sl] * (sg * (1.0 + u2 * (1.0 - sg)))
            dg = jnp.sum(du2 * xh, axis=0, keepdims=True)
            db = jnp.sum(du2, axis=0, keepdims=True)

            @pl.when(i == 0)
            def _():
                dg_ref[:, sl] = dg
                db_ref[:, sl] = db

            @pl.when(i > 0)
            def _():
                dg_ref[:, sl] += dg
                db_ref[:, sl] += db

            du1_ref[:, sl] = _ln_bwd(du2, xh, rstd, ng)

    row = pl.BlockSpec((1, w), lambda i: (0, 0))
    blk = pl.BlockSpec((tc, w), lambda i: (i, 0))
    return pl.pallas_call(
        body, name="conv_norm_bwd", grid=(t // tc,),
        in_specs=[blk, blk, row, row], out_specs=[blk, row, row],
        out_shape=[jax.ShapeDtypeStruct((t, w), F32), jax.ShapeDtypeStruct((1, w), F32),
                   jax.ShapeDtypeStruct((1, w), F32)],
        compiler_params=_params("arbitrary"),
    )(dcat, u1, cn_g, cn_b)


def _conv_bwd(du1, p3, conv_w, tc, cb, rider=None):
    n_sec, t, w = p3.shape
    kk = conv_w.shape[0]
    off = HALO - (kk - 1)
    hb = tc // HALO
    nt = t // tc
    kpad = -(-kk // SUBLANE) * SUBLANE

    def body(d_ref, dn_ref, a_ref, g_ref, ap_ref, gp_ref, w_ref, dp_ref, dw_ref, db_ref,
             extd, extu, shd, shu, wacc, bacc):
        i = pl.program_id(1)

        @pl.when(i == 0)
        def _():
            wacc[...] = jnp.zeros_like(wacc)
            bacc[...] = jnp.zeros_like(bacc)

        extd[pl.ds(0, tc), :] = d_ref[...]
        extd[pl.ds(tc, HALO), :] = jnp.where(i < nt - 1, dn_ref[...], 0.0)
        extu[pl.ds(HALO, tc), :] = a_ref[...] * _sigmoid(g_ref[...])
        extu[pl.ds(0, HALO), :] = jnp.where(i > 0, ap_ref[...] * _sigmoid(gp_ref[...]), 0.0)
        _shift_copies(extd, shd)
        _shift_copies(extu, shu)
        for r in range(tc // ROWS):
            rows = pl.ds(r * ROWS, ROWS)
            acc = jnp.zeros((ROWS, cb), F32)
            for k in range(kk):
                acc = acc + w_ref[k:k + 1, :] * _window(extd, shd, r * ROWS + (kk - 1) - k, ROWS)
            a = a_ref[rows, :]
            sg = _sigmoid(g_ref[rows, :])
            dp_ref[0, rows, :] = (acc * sg).astype(BF16)
            dp_ref[1, rows, :] = (acc * a * sg * (1.0 - sg)).astype(BF16)
            d = d_ref[rows, :]
            bacc[...] += jnp.sum(d.reshape(ROWS // SUBLANE, SUBLANE, cb), axis=0)
            for k in range(kk):
                prod = d * _window(extu, shu, r * ROWS + off + k, ROWS)
                wacc[k] += jnp.sum(prod.reshape(ROWS // SUBLANE, SUBLANE, cb), axis=0)

        @pl.when(i == nt - 1)
        def _():
            for k in range(kk):
                dw_ref[k:k + 1, :] = jnp.sum(wacc[k], axis=0, keepdims=True)
            if kpad > kk:
                dw_ref[kk:kpad, :] = jnp.zeros((kpad - kk, cb), F32)
            db_ref[...] = jnp.sum(bacc[...], axis=0, keepdims=True)

    cur = lambda sec: pl.BlockSpec((None, tc, cb), lambda j, i: (sec, i, j))
    prev = lambda sec: pl.BlockSpec((None, HALO, cb), lambda j, i: (sec, jnp.maximum(i * hb - 1, 0), j))
    return _call(
        body, (du1, du1, p3, p3, p3, p3, conv_w), name="conv_bwd", grid=(w // cb, nt),
        in_specs=[pl.BlockSpec((tc, cb), lambda j, i: (i, j)),
                  pl.BlockSpec((HALO, cb), lambda j, i: (jnp.minimum((i + 1) * hb, t // HALO - 1), j)),
                  cur(0), cur(1), prev(0), prev(1), pl.BlockSpec((kk, cb), lambda j, i: (0, j))],
        out_specs=[pl.BlockSpec((2, tc, cb), lambda j, i: (0, i, j)),
                   pl.BlockSpec((kpad, cb), lambda j, i: (0, j)),
                   pl.BlockSpec((1, cb), lambda j, i: (0, j))],
        out_shape=[jax.ShapeDtypeStruct((n_sec, t, w), BF16), jax.ShapeDtypeStruct((kpad, w), F32),
                   jax.ShapeDtypeStruct((1, w), F32)],
        scratch_shapes=[pltpu.VMEM((tc + HALO, cb), F32), pltpu.VMEM((tc + HALO, cb), F32),
                        pltpu.VMEM((SUBLANE - 1, tc + HALO - SUBLANE, cb), F32),
                        pltpu.VMEM((SUBLANE - 1, tc + HALO - SUBLANE, cb), F32),
                        pltpu.VMEM((kk, SUBLANE, cb), F32), pltpu.VMEM((SUBLANE, cb), F32)], rider=rider)


def _ffn_act_fwd(hh3, fw, fb, tc, cb, rider=None):
    _, t, dff = hh3.shape
    kk = fw.shape[0]
    off = FHALO - (kk - 1)
    hb = tc // FHALO

    def body(g_ref, v_ref, gp_ref, w_ref, b_ref, act_ref, ext):
        i = pl.program_id(1)
        ext[pl.ds(FHALO, tc), :] = g_ref[...]
        ext[pl.ds(0, FHALO), :] = jnp.where(i > 0, gp_ref[...], 0.0)
        for r in range(tc // ROWS):
            rows = pl.ds(r * ROWS, ROWS)
            gc = jnp.broadcast_to(b_ref[...], (ROWS, cb))
            for k in range(kk):
                gc = gc + w_ref[k:k + 1, :] * ext[pl.ds(r * ROWS + off + k, ROWS), :]
            act_ref[rows, :] = (gc * _sigmoid(gc) * v_ref[rows, :]).astype(BF16)

    return _call(
        body, (hh3, hh3, hh3, fw, fb), name="ffn_act_fwd", grid=(dff // cb, t // tc),
        in_specs=[pl.BlockSpec((None, tc, cb), lambda j, i: (0, i, j)),
                  pl.BlockSpec((None, tc, cb), lambda j, i: (1, i, j)),
                  pl.BlockSpec((None, FHALO, cb), lambda j, i: (0, jnp.maximum(i * hb - 1, 0), j)),
                  pl.BlockSpec((kk, cb), lambda j, i: (0, j)),
                  pl.BlockSpec((1, cb), lambda j, i: (0, j))],
        out_specs=pl.BlockSpec((tc, cb), lambda j, i: (i, j)),
        out_shape=jax.ShapeDtypeStruct((t, dff), BF16),
        scratch_shapes=[pltpu.VMEM((tc + FHALO, cb), F32)], rider=rider)


def _ffn_act_bwd(dact, hh3, fw, fb, tc, cb):
    _, t, dff = hh3.shape
    kk = fw.shape[0]
    off = FHALO - (kk - 1)
    hb = tc // FHALO
    nt = t // tc
    te = tc + FHALO

    def body(da_ref, dan_ref, g_ref, gp_ref, gn_ref, v_ref, vn_ref, w_ref, b_ref,
             dhh_ref, dw_ref, db_ref, gext, dext, wacc, bacc):
        i = pl.program_id(1)

        @pl.when(i == 0)
        def _():
            wacc[...] = jnp.zeros_like(wacc)
            bacc[...] = jnp.zeros_like(bacc)

        gext[pl.ds(0, FHALO), :] = jnp.where(i > 0, gp_ref[...], 0.0)
        gext[pl.ds(FHALO, tc), :] = g_ref[...]
        gext[pl.ds(FHALO + tc, FHALO), :] = gn_ref[...]

        def gate_grad(r0, n, da, v):
            gc = jnp.broadcast_to(b_ref[...], (n, cb))
            for k in range(kk):
                gc = gc + w_ref[k:k + 1, :] * gext[pl.ds(r0 + off + k, n), :]
            sg = _sigmoid(gc)
            return gc * sg, da * v * (sg * (1.0 + gc * (1.0 - sg)))

        for r in range(tc // ROWS):
            rows = pl.ds(r * ROWS, ROWS)
            da = da_ref[rows, :]
            silu, dgc = gate_grad(r * ROWS, ROWS, da, v_ref[rows, :])
            dext[rows, :] = dgc
            dhh_ref[1, rows, :] = (da * silu).astype(BF16)
        _, dgc_next = gate_grad(tc, FHALO, dan_ref[...], vn_ref[...])
        dext[pl.ds(tc, FHALO), :] = jnp.where(i < nt - 1, dgc_next, 0.0)
        for r in range(tc // ROWS):
            rows = pl.ds(r * ROWS, ROWS)
            dg = jnp.zeros((ROWS, cb), F32)
            for k in range(kk):
                dg = dg + w_ref[k:k + 1, :] * dext[pl.ds(r * ROWS + (kk - 1) - k, ROWS), :]
            dhh_ref[0, rows, :] = dg.astype(BF16)
            dgc = dext[rows, :]
            bacc[...] += jnp.sum(dgc.reshape(ROWS // SUBLANE, SUBLANE, cb), axis=0)
            for k in range(kk):
                prod = dgc * gext[pl.ds(r * ROWS + off + k, ROWS), :]
                wacc[k] += jnp.sum(prod.reshape(ROWS // SUBLANE, SUBLANE, cb), axis=0)

        @pl.when(i == nt - 1)
        def _():
            for k in range(kk):
                dw_ref[k:k + 1, :] = jnp.sum(wacc[k], axis=0, keepdims=True)
            dw_ref[kk:SUBLANE, :] = jnp.zeros((SUBLANE - kk, cb), F32)
            db_ref[...] = jnp.sum(bacc[...], axis=0, keepdims=True)

    nxt = lambda i: jnp.minimum((i + 1) * hb, t // FHALO - 1)
    return pl.pallas_call(
        body, name="ffn_act_bwd", grid=(dff // cb, nt),
        in_specs=[pl.BlockSpec((tc, cb), lambda j, i: (i, j)),
                  pl.BlockSpec((FHALO, cb), lambda j, i: (nxt(i), j)),
                  pl.BlockSpec((None, tc, cb), lambda j, i: (0, i, j)),
                  pl.BlockSpec((None, FHALO, cb), lambda j, i: (0, jnp.maximum(i * hb - 1, 0), j)),
                  pl.BlockSpec((None, FHALO, cb), lambda j, i: (0, nxt(i), j)),
                  pl.BlockSpec((None, tc, cb), lambda j, i: (1, i, j)),
                  pl.BlockSpec((None, FHALO, cb), lambda j, i: (1, nxt(i), j)),
                  pl.BlockSpec((kk, cb), lambda j, i: (0, j)),
                  pl.BlockSpec((1, cb), lambda j, i: (0, j))],
        out_specs=[pl.BlockSpec((2, tc, cb), lambda j, i: (0, i, j)),
                   pl.BlockSpec((SUBLANE, cb), lambda j, i: (0, j)),
                   pl.BlockSpec((1, cb), lambda j, i: (0, j))],
        out_shape=[jax.ShapeDtypeStruct((2, t, dff), BF16), jax.ShapeDtypeStruct((SUBLANE, dff), F32),
                   jax.ShapeDtypeStruct((1, dff), F32)],
        scratch_shapes=[pltpu.VMEM((tc + 2 * FHALO, cb), F32), pltpu.VMEM((te, cb), F32),
                        pltpu.VMEM((kk, SUBLANE, cb), F32), pltpu.VMEM((SUBLANE, cb), F32)],
        compiler_params=_params("parallel", "arbitrary"),
    )(dact, dact, hh3, hh3, hh3, hh3, hh3, fw, fb)


def _chunk_consts():
    r = lax.broadcasted_iota(jnp.int32, (CHUNK, CHUNK), 0)
    c = lax.broadcasted_iota(jnp.int32, (CHUNK, CHUNK), 1)
    blk = (r // SUB) * SUB
    tri = (c <= r).astype(F32)
    start = (c < blk).astype(F32)
    end = (c < blk + SUB).astype(F32)
    return jnp.concatenate([tri, start, end, jnp.ones((SUBLANE, CHUNK), F32)], axis=0)


def _roll8(x, d):
    return pltpu.roll(x.reshape(CHUNK // SUB, SUB, LANE), d % SUB, 1).reshape(CHUNK, LANE)


def _gate_terms(q, fpre, lb):
    sf = _sigmoid(fpre)
    fg = lb + (1.0 - lb) * sf
    sq = _sigmoid(q)
    return sf, fg, 1.0 - fg, sq, q * sq


def _decays(g, consts):
    cs = _dot3(consts, g)
    b = cs[0:CHUNK]
    rs = cs[CHUNK:2 * CHUNK]
    re = cs[2 * CHUNK:3 * CHUNK]
    tot = cs[3 * CHUNK:3 * CHUNK + 1]
    return b, rs, re, tot


def _lower_bound(lb_ref):
    l0, l1 = lb_ref[0:1, :], lb_ref[1:2, :]
    mx = jnp.maximum(l0, l1)
    e0, e1 = jnp.exp(l0 - mx), jnp.exp(l1 - mx)
    return e0 / (e0 + e1)


def _scaled_keys(kt, rs, re, rowblk, i):
    scale = jnp.where(rowblk < i, jnp.exp(jnp.minimum(rs[SUB * i:SUB * i + 1, :] - re, 0.0)), 0.0)
    return kt * scale, scale


def _hgrn_fwd(p3, lb_logits, hg, cat, tb, hpb, rider=None):
    _, t, w = p3.shape
    nh = w // LANE
    nc = tb // CHUNK
    assert nh % hpb == 0

    def body(q_ref, f_ref, v_ref, og_ref, lb_ref, hg_ref, cat_in, cat_ref, o_ref, st_ref, state):
        del cat_in
        consts = _chunk_consts()
        lb_all = _lower_bound(lb_ref)
        rowblk = lax.broadcasted_iota(jnp.int32, (CHUNK, 1), 0) // SUB
        rowpos = lax.broadcasted_iota(jnp.int32, (CHUNK, 1), 0) % SUB

        @pl.when(pl.program_id(1) == 0)
        def _():
            state[...] = jnp.zeros_like(state)

        def chunk(c, carry):
            rows = pl.ds(pl.multiple_of(c * CHUNK, CHUNK), CHUNK)
            heads = range(hpb)
            sls = [slice(j * LANE, (j + 1) * LANE) for j in heads]
            v = [v_ref[rows, s] for s in sls]
            vb = [x.astype(BF16) for x in v]
            gates = [_gate_terms(q_ref[rows, s], f_ref[rows, s], lb_all[:, s]) for s in sls]
            fg = [g[1] for g in gates]
            kk = [g[2] for g in gates]
            qh = [g[4] for g in gates]
            dec = [_decays(jnp.log(x), consts) for x in fg]
            b = [x[0] for x in dec]
            rs = [x[1] for x in dec]
            re = [x[2] for x in dec]
            tot = [x[3] for x in dec]
            qt = [qh[j] * jnp.exp(b[j] - rs[j]) for j in heads]
            kt = [kk[j] * jnp.exp(re[j] - b[j]) for j in heads]
            st = [state[j] for j in heads]
            for j in heads:
                st_ref[j, c] = st[j]
            a = [jnp.zeros((CHUNK, CHUNK), F32) for _ in heads]
            for i in range(1, CHUNK // SUB):
                for j in heads:
                    ki, _ = _scaled_keys(kt[j], rs[j], re[j], rowblk, i)
                    a[j] = a[j] + _dot_nt(jnp.where(rowblk == i, qt[j], 0.0).astype(BF16), ki.astype(BF16))
            o = [_dot(a[j].astype(BF16), vb[j]) for j in heads]
            o = [o[j] + _dot_nt((qh[j] * jnp.exp(b[j])).astype(BF16), st[j].astype(BF16)) for j in heads]
            for j in heads:
                k_up = kk[j] * jnp.exp(tot[j] - b[j])
                state[j] = st[j] * jnp.exp(tot[j]) + _dot_tn(vb[j], k_up.astype(BF16))
            for j in heads:
                e, rf = None, fg[j]
                for d in range(SUB):
                    if d == 0:
                        vs, term = v[j], qh[j] * kk[j]
                    else:
                        e = rf if e is None else e * rf
                        rf = _roll8(fg[j], d)
                        vs = _roll8(v[j], d)
                        term = jnp.where(rowpos >= d, qh[j] * (1.0 - rf) * e, 0.0)
                    o[j] = o[j] + jnp.sum(term, axis=-1, keepdims=True) * vs
            for j in heads:
                og = og_ref[rows, sls[j]]
                o_ref[rows, sls[j]] = o[j]
                r = lax.rsqrt(jnp.mean(o[j] * o[j], axis=-1, keepdims=True) + RMS_EPS)
                cat_ref[rows, sls[j]] = (o[j] * r * hg_ref[:, sls[j]] * (og * _sigmoid(og))).astype(BF16)
            return carry

        lax.fori_loop(0, nc, chunk, 0)

    bw = hpb * LANE
    sec = lambda s: pl.BlockSpec((None, tb, bw), lambda h, i: (s, i, h))
    return _call(
        body, (p3, p3, p3, p3, lb_logits, hg, cat), name="hgrn_fwd", grid=(nh // hpb, t // tb),
        in_specs=[sec(2), sec(3), sec(4), sec(5),
                  pl.BlockSpec((2, bw), lambda h, i: (0, h)),
                  pl.BlockSpec((1, bw), lambda h, i: (0, h)), HBM],
        out_specs=[pl.BlockSpec((tb, bw), lambda h, i: (i, nh // hpb + h)),
                   pl.BlockSpec((tb, bw), lambda h, i: (i, h)),
                   pl.BlockSpec((hpb, nc, LANE, LANE), lambda h, i: (h, i, 0, 0))],
        out_shape=[jax.ShapeDtypeStruct(cat.shape, BF16), jax.ShapeDtypeStruct((t, w), F32),
                   jax.ShapeDtypeStruct((nh, t // CHUNK, LANE, LANE), F32)],
        scratch_shapes=[pltpu.VMEM((hpb, LANE, LANE), F32)], aliases={6: 0}, rider=rider)


def _hgrn_bwd(p3, lb_logits, hg, o_pre, states, dcat, dp3, tb, hpb, rider=None):
    n_sec, t, w = p3.shape
    nh = w // LANE
    assert nh % hpb == 0
    nc = tb // CHUNK
    nb = t // tb
    bw = hpb * LANE
    n_steps = (nh // hpb) * nb

    def body(q_ref, f_ref, v_ref, og_ref, lb_ref, hg_ref, o_ref, st_ref, dc_ref, dp_in,
             dp_ref, dlb_ref, dhg_ref, dstate, stash, lbacc, hgacc, osem):
        del dp_in
        h, i = pl.program_id(0), pl.program_id(1)
        step = h * nb + i
        slot = step % 2

        def out_copy(s, row_blk, lane_blk):
            dst = dp_ref.at[pl.ds(2, 4), pl.ds(row_blk * tb, tb), pl.ds(lane_blk * bw, bw)]
            return pltpu.make_async_copy(stash.at[s], dst, osem.at[s])

        @pl.when(step >= 2)
        def _():
            out_copy(slot, 0, 0).wait()

        def compute():
            consts = _chunk_consts()
            rr = lax.broadcasted_iota(jnp.int32, (CHUNK, CHUNK), 0)
            cc = lax.broadcasted_iota(jnp.int32, (CHUNK, CHUNK), 1)
            upper = (cc >= rr).astype(F32)
            lb_all = _lower_bound(lb_ref)
            rowblk = lax.broadcasted_iota(jnp.int32, (CHUNK, 1), 0) // SUB
            rowpos = lax.broadcasted_iota(jnp.int32, (CHUNK, 1), 0) % SUB

            @pl.when(i == 0)
            def _():
                dstate[...] = jnp.zeros_like(dstate)
                lbacc[...] = jnp.zeros_like(lbacc)
                hgacc[...] = jnp.zeros_like(hgacc)

            def head(j, c, rows):
                sl = slice(j * LANE, (j + 1) * LANE)
                lb = lb_all[:, sl]
                hgv = hg_ref[:, sl]
                q = q_ref[rows, sl]
                v = v_ref[rows, sl]
                og = og_ref[rows, sl]
                o = o_ref[rows, sl]
                dcg = dc_ref[rows, sl]
                sf, fg, kk, sq, qh = _gate_terms(q, f_ref[rows, sl], lb)
                b, rs, re, tot = _decays(jnp.log(fg), consts)
                eq = jnp.exp(b - rs)
                ek = jnp.exp(re - b)
                qt = qh * eq
                kt = kk * ek
                e_in = jnp.exp(b)
                e_up = jnp.exp(tot - b)
                e_tot = jnp.exp(tot)
                q_in = (qh * e_in).astype(BF16)
                k_up = (kk * e_up).astype(BF16)
                vb = v.astype(BF16)
                st = st_ref[j, c]
                dst = dstate[j]
                dstb = dst.astype(BF16)
                yield

                sg = _sigmoid(og)
                r = lax.rsqrt(jnp.mean(o * o, axis=-1, keepdims=True) + RMS_EPS)
                ohat = o * r
                d_og = dcg * ohat * hgv * (sg * (1.0 + og * (1.0 - sg)))
                d_on = dcg * (og * sg)
                hgacc[:, sl] += jnp.sum((d_on * ohat).reshape(CHUNK // SUBLANE, SUBLANE, LANE), axis=0)
                d_oh = d_on * hgv
                do = r * (d_oh - ohat * jnp.mean(d_oh * ohat, axis=-1, keepdims=True))
                dob = do.astype(BF16)

                da = _dot_nt(dob, vb)
                yield
                a_off = jnp.zeros((CHUNK, CHUNK), F32)
                dqt = jnp.zeros((CHUNK, LANE), F32)
                dkt = jnp.zeros((CHUNK, LANE), F32)
                for blk in range(1, CHUNK // SUB):
                    ki, scale = _scaled_keys(kt, rs, re, rowblk, blk)
                    kib = ki.astype(BF16)
                    qib = jnp.where(rowblk == blk, qt, 0.0).astype(BF16)
                    dab = jnp.where(rowblk == blk, da, 0.0).astype(BF16)
                    a_off = a_off + _dot_nt(qib, kib)
                    dqt = dqt + _dot(dab, kib)
                    dkt = dkt + _dot_tn(dab, qib) * scale
                    yield
                dqh = dqt * eq
                dk = dkt * ek
                dv = _dot_tn(a_off.astype(BF16), dob)

                dqh = dqh + _dot(dob, st.astype(BF16)) * e_in
                dk = dk + _dot(vb, dstb) * e_up
                dv = dv + _dot_nt(k_up, dstb)
                st_end = st * e_tot + _dot_tn(vb, k_up)
                carry_g = jnp.sum(st_end * dst, axis=0, keepdims=True)
                dstate[j] = dst * e_tot + _dot_tn(dob, q_in)
                yield

                e, rf = None, fg
                for d in range(SUB):
                    if d == 0:
                        a_d = jnp.sum(qh * kk, axis=-1, keepdims=True)
                        da_d = jnp.sum(do * v, axis=-1, keepdims=True)
                        dqh = dqh + da_d * kk
                        dk = dk + da_d * qh
                        dv = dv + a_d * do
                        continue
                    e = rf if e is None else e * rf
                    rf = _roll8(fg, d)
                    em = jnp.where(rowpos >= d, e, 0.0)
                    ks, vs = 1.0 - rf, _roll8(v, d)
                    a_d = jnp.sum(qh * ks * em, axis=-1, keepdims=True)
                    da_d = jnp.sum(do * vs, axis=-1, keepdims=True) * em
                    dqh = dqh + da_d * ks
                    dk = dk + _roll8(da_d * qh, -d)
                    dv = dv + _roll8(a_d * do, -d)
                yield

                dg = _dot3(upper, qh * dqh - kk * dk) + carry_g
                dfg = dg / fg - dk
                lbacc[:, sl] += jnp.sum((dfg * (1.0 - sf)).reshape(CHUNK // SUBLANE, SUBLANE, LANE), axis=0)
                stash[slot, 0, rows, sl] = (dqh * (sq * (1.0 + q * (1.0 - sq)))).astype(BF16)
                stash[slot, 1, rows, sl] = (dfg * (1.0 - lb) * sf * (1.0 - sf)).astype(BF16)
                stash[slot, 2, rows, sl] = dv.astype(BF16)
                stash[slot, 3, rows, sl] = d_og.astype(BF16)

            def chunk(cr, carry):
                c = nc - 1 - cr
                rows = pl.ds(pl.multiple_of(c * CHUNK, CHUNK), CHUNK)
                running = [head(j, c, rows) for j in range(hpb)]
                while running:
                    running = [g for g in running if next(g, StopIteration) is not StopIteration]
                return carry

            lax.fori_loop(0, nc, chunk, 0)

            @pl.when(i == nb - 1)
            def _():
                dlb_ref[...] = jnp.sum(lbacc[...], axis=0, keepdims=True)
                dhg_ref[...] = jnp.sum(hgacc[...], axis=0, keepdims=True)

        compute()
        out_copy(slot, nb - 1 - i, h).start()

        @pl.when(step == n_steps - 1)
        def _():
            out_copy(slot, 0, 0).wait()
            if n_steps >= 2:
                out_copy(1 - slot, 0, 0).wait()

    rev = lambda i: nb - 1 - i
    sec = lambda s: pl.BlockSpec((None, tb, bw), lambda h, i: (s, rev(i), h))
    return _call(
        body, (p3, p3, p3, p3, lb_logits, hg, o_pre, states, dcat, dp3), name="hgrn_bwd", grid=(nh // hpb, nb),
        in_specs=[sec(2), sec(3), sec(4), sec(5),
                  pl.BlockSpec((2, bw), lambda h, i: (0, h)),
                  pl.BlockSpec((1, bw), lambda h, i: (0, h)),
                  pl.BlockSpec((tb, bw), lambda h, i: (rev(i), h)),
                  pl.BlockSpec((hpb, nc, LANE, LANE), lambda h, i: (h, rev(i), 0, 0)),
                  pl.BlockSpec((tb, bw), lambda h, i: (rev(i), nh // hpb + h)), HBM],
        out_specs=[HBM,
                   pl.BlockSpec((1, bw), lambda h, i: (0, h)),
                   pl.BlockSpec((1, bw), lambda h, i: (0, h))],
        out_shape=[jax.ShapeDtypeStruct((n_sec, t, w), BF16), jax.ShapeDtypeStruct((1, w), F32),
                   jax.ShapeDtypeStruct((1, w), F32)],
        scratch_shapes=[pltpu.VMEM((hpb, LANE, LANE), F32), pltpu.VMEM((2, 4, tb, bw), BF16),
                        pltpu.VMEM((SUBLANE, bw), F32), pltpu.VMEM((SUBLANE, bw), F32),
                        pltpu.SemaphoreType.DMA((2,))],
        aliases={9: 0}, rider=rider)


def _place():
    x, y, c = lax.axis_index("x"), lax.axis_index("y"), lax.axis_index("c")
    chips = [(1 - x, y), (x, 1 - y), (1 - x, 1 - y)]
    return x, y, c, chips


def _rows(buf, px, py, pc, part=None):
    half = buf.shape[1] // 2
    if part is None:
        return buf.at[2 * px + py, pl.ds(pc * half, half)]
    lo, hi, n = part
    piece = half // n
    return buf.at[2 * px + py, pl.ds(pc * half + lo * piece, (hi - lo) * piece)]


def _rcopy(src, dst, send, recv, idx, to):
    return pltpu.make_async_remote_copy(src_ref=src, dst_ref=dst, send_sem=send.at[idx], recv_sem=recv.at[idx],
                                        device_id=to, device_id_type=MESH)


def _same(bufs):
    return [jax.ShapeDtypeStruct(b.shape, b.dtype) for b in bufs]


def _ride_gather_ici(bufs, parts=None):
    n = len(bufs)
    parts = parts or [None] * n

    def start(rin, rout, send, recv):
        x, y, c, chips = _place()
        for k in range(n):
            mine = _rows(rout[k], x, y, c, parts[k])
            for j, chip in enumerate(chips):
                _rcopy(mine, mine, send, recv, 3 * k + j, (*chip, c)).start()

    def finish(rin, rout, send, recv):
        x, y, c, chips = _place()
        for k in range(n):
            for j, chip in enumerate(chips):
                theirs = _rows(rout[k], *chip, c, parts[k])
                _rcopy(theirs, theirs, send, recv, 3 * k + j, (x, y, c)).wait_recv()
        for k in range(n):
            mine = _rows(rout[k], x, y, c, parts[k])
            for j in range(3):
                _rcopy(mine, mine, send, recv, 3 * k + j, (x, y, c)).wait_send()

    return _Rider(bufs, _same(bufs), {k: k for k in range(n)}, 3 * n, start, finish)


class _SemView:
    def __init__(self, ref, base):
        self.ref, self.base = ref, base

    @property
    def at(self):
        return self

    def __getitem__(self, idx):
        return self.ref.at[idx + self.base]


def _ride_both(a, b):
    nai, nao = len(a.ins), len(a.outs)

    def start(rin, rout, send, recv):
        a.start(rin[:nai], rout[:nao], send, recv)
        b.start(rin[nai:], rout[nao:], _SemView(send, a.n_sems), _SemView(recv, a.n_sems))

    def finish(rin, rout, send, recv):
        a.finish(rin[:nai], rout[:nao], send, recv)
        b.finish(rin[nai:], rout[nao:], _SemView(send, a.n_sems), _SemView(recv, a.n_sems))

    aliases = dict(a.aliases)
    aliases.update({nai + ri: nao + ro for ri, ro in b.aliases.items()})
    return _Rider(a.ins + b.ins, a.outs + b.outs, aliases, a.n_sems + b.n_sems, start, finish)


def _ride_gather_d2d(bufs):
    n = len(bufs)

    def start(rin, rout, send, recv):
        x, y, c, chips = _place()
        for k in range(n):
            for j, chip in enumerate(chips):
                got = _rows(rout[k], *chip, c)
                _rcopy(got, got, send, recv, 3 * k + j, (x, y, 1 - c)).start()

    def finish(rin, rout, send, recv):
        x, y, c, chips = _place()
        for k in range(n):
            for j, chip in enumerate(chips):
                theirs = _rows(rout[k], *chip, 1 - c)
                _rcopy(theirs, theirs, send, recv, 3 * k + j, (x, y, c)).wait_recv()
        for k in range(n):
            for j, chip in enumerate(chips):
                got = _rows(rout[k], *chip, c)
                _rcopy(got, got, send, recv, 3 * k + j, (x, y, c)).wait_send()

    return _Rider(bufs, _same(bufs), {k: k for k in range(n)}, 3 * n, start, finish)


def _ride_swap(grads):
    n = len(grads)

    def copy(k, rin, rout, send, recv):
        x, y, c, _ = _place()
        half = rin[k].shape[1] // 2
        return _rcopy(rin[k].at[:, pl.ds((1 - c) * half, half)], rout[k], send, recv, k, (x, y, 1 - c))

    def start(rin, rout, send, recv):
        for k in range(n):
            copy(k, rin, rout, send, recv).start()

    def finish(rin, rout, send, recv):
        for k in range(n):
            copy(k, rin, rout, send, recv).wait()

    outs = [jax.ShapeDtypeStruct((g.shape[0], g.shape[1] // 2, g.shape[2]), g.dtype) for g in grads]
    return _Rider(grads, outs, {}, n, start, finish)


def _ride_to_sibling(arrs):
    n = len(arrs)

    def copy(k, rin, rout, send, recv):
        x, y, c, _ = _place()
        return _rcopy(rin[k], rout[k], send, recv, k, (x, y, 1 - c))

    def start(rin, rout, send, recv):
        for k in range(n):
            copy(k, rin, rout, send, recv).start()

    def finish(rin, rout, send, recv):
        for k in range(n):
            copy(k, rin, rout, send, recv).wait()

    return _Rider(arrs, _same(arrs), {}, n, start, finish)


def _add_cast(name, a, b):
    s, r, cols = a.shape
    tr = _div_tile(r, 16, 512)

    def body(a_ref, b_ref, o_ref):
        o_ref[...] = (a_ref[...] + b_ref[...]).astype(BF16)

    blk = pl.BlockSpec((None, tr, cols), lambda k, i: (k, i, 0))
    return _call(body, (a, b), name=name, grid=(s, r // tr), in_specs=[blk, blk], out_specs=blk,
                 out_shape=jax.ShapeDtypeStruct(a.shape, BF16))


def _ride_send_partials(parts, pieces=None, into=None):
    n = len(parts)
    pieces = pieces or [None] * n

    def cut(ref, k):
        if pieces[k] is None:
            return ref
        lo, hi, m = pieces[k]
        q = ref.shape[0] // m
        return ref.at[pl.ds(lo * q, (hi - lo) * q)]

    def copies(rin, rout, send, recv):
        x, y, c, chips = _place()
        return [_rcopy(cut(rin[k].at[2 * px + py], k), cut(rout[k].at[j], k), send, recv, 3 * k + j, (px, py, c))
                for k in range(n) for j, (px, py) in enumerate(chips)]

    def start(rin, rout, send, recv):
        for cp in copies(rin, rout, send, recv):
            cp.start()

    def finish(rin, rout, send, recv):
        for cp in copies(rin, rout, send, recv):
            cp.wait()

    if into is None:
        outs = [jax.ShapeDtypeStruct((3,) + p.shape[1:], p.dtype) for p in parts]
        return _Rider(parts, outs, {}, 3 * n, start, finish)
    return _Rider(list(parts) + list(into), _same(into), {n + k: k for k in range(n)}, 3 * n, start, finish)


def _ride_join(bufs):
    n = len(bufs)

    def half_of(buf, pc):
        half = buf.shape[0] // 2
        return buf.at[pl.ds(pc * half, half)]

    def start(rin, rout, send, recv):
        x, y, c, _ = _place()
        for k in range(n):
            mine = half_of(rout[k], c)
            _rcopy(mine, mine, send, recv, k, (x, y, 1 - c)).start()

    def finish(rin, rout, send, recv):
        x, y, c, _ = _place()
        for k in range(n):
            mine, theirs = half_of(rout[k], c), half_of(rout[k], 1 - c)
            _rcopy(mine, mine, send, recv, k, (x, y, c)).wait_send()
            _rcopy(theirs, theirs, send, recv, k, (x, y, c)).wait_recv()

    return _Rider(bufs, _same(bufs), {k: k for k in range(n)}, n, start, finish)


def _run(name, rider):
    def body(*refs):
        nri, nro = len(rider.ins), len(rider.outs)
        rin, rout = refs[:nri], refs[nri:nri + nro]
        send, recv = refs[nri + nro:]
        rider.start(rin, rout, send, recv)
        rider.finish(rin, rout, send, recv)

    return pl.pallas_call(
        body, name=name, in_specs=[HBM] * len(rider.ins), out_specs=[HBM] * len(rider.outs), out_shape=rider.outs,
        scratch_shapes=[pltpu.SemaphoreType.DMA((rider.n_sems,)), pltpu.SemaphoreType.DMA((rider.n_sems,))],
        input_output_aliases=rider.aliases,
    )(*rider.ins)


def _add_halves(name, g, other, c_idx):
    s, r, cols = g.shape
    half = r // 2
    tr = _div_tile(half, 16, 512)
    nb = half // tr

    def body(c_ref, g_ref, o_ref, q_ref):
        del c_ref
        q_ref[...] = (g_ref[...] + o_ref[...]).astype(BF16)

    return pl.pallas_call(
        body, name=name,
        grid_spec=pltpu.PrefetchScalarGridSpec(
            num_scalar_prefetch=1, grid=(s, nb),
            in_specs=[pl.BlockSpec((None, tr, cols), lambda k, i, c: (k, c[0] * nb + i, 0)),
                      pl.BlockSpec((None, tr, cols), lambda k, i, c: (k, i, 0))],
            out_specs=pl.BlockSpec((None, tr, cols), lambda k, i, c: (k, i, 0))),
        out_shape=jax.ShapeDtypeStruct((s, half, cols), BF16),
        compiler_params=_params("parallel", "parallel"),
    )(c_idx, g, other)


def _sum_partials(name, part, arrived, place_idx):
    _, half, cols = part.shape
    tr = _div_tile(half, 16, 512)
    nb = half // tr

    def body(s_ref, p_ref, a_ref, o_ref):
        del s_ref
        o_ref[...] = ((p_ref[...].astype(F32) + a_ref[0].astype(F32)) + a_ref[1].astype(F32)) + a_ref[2].astype(F32)

    return pl.pallas_call(
        body, name=name,
        grid_spec=pltpu.PrefetchScalarGridSpec(
            num_scalar_prefetch=1, grid=(nb,),
            in_specs=[pl.BlockSpec((None, tr, cols), lambda i, s: (s[0], i, 0)),
                      pl.BlockSpec((3, tr, cols), lambda i, s: (0, i, 0))],
            out_specs=pl.BlockSpec((tr, cols), lambda i, s: (s[1] * nb + i, 0))),
        out_shape=jax.ShapeDtypeStruct((2 * half, cols), F32),
        compiler_params=_params("parallel"),
    )(place_idx, part, arrived)


def _pack_small(name, wide_rows, ffn_rows, w, dff, n_wide, n_ffn):
    n_in = len(wide_rows) + len(ffn_rows)

    def body(*refs):
        ins, outs = refs[:n_in], refs[n_in:]
        p1 = outs[0]
        p1[...] = jnp.zeros_like(p1)
        row = 0
        for ref, (_, r, m) in zip(ins, wide_rows):
            if m == 1 and r % SUBLANE == 0 and row % SUBLANE == 0:
                p1[row:row + r, :] = ref[...]
                row += r
                continue
            for rr in range(r):
                for mm in range(m):
                    p1[row:row + 1, :] = ref[rr:rr + 1, mm * w:(mm + 1) * w]
                    row += 1
        if ffn_rows:
            p2 = outs[1]
            p2[...] = jnp.zeros_like(p2)
            row = 0
            for ref, arr in zip(ins[len(wide_rows):], ffn_rows):
                r = arr.shape[0]
                p2[row:row + r, :] = ref[...]
                row += r

    shapes = [jax.ShapeDtypeStruct((n_wide, w), F32)] + ([jax.ShapeDtypeStruct((n_ffn, dff), F32)] if ffn_rows else [])
    return pl.pallas_call(
        body, name=name, in_specs=[VMEM_FULL] * n_in, out_specs=[VMEM_FULL] * len(shapes), out_shape=shapes,
        compiler_params=pltpu.CompilerParams(vmem_limit_bytes=VMEM_LIMIT),
    )(*[a for a, _, _ in wide_rows], *ffn_rows)


def _ride_exchange8(packs):
    n = len(packs)

    def copies(rin, rout, send, recv):
        x, y, c, _ = _place()
        me = 4 * x + 2 * y + c
        out = []
        for a in range(n):
            for mask in range(1, 8):
                peer = (x ^ (mask >> 2), y ^ ((mask >> 1) & 1), c ^ (mask & 1))
                out.append(_rcopy(rin[a], rout[a].at[me], send, recv, 8 * a + mask, peer))
        own = [pltpu.make_async_copy(rin[a], rout[a].at[me], send.at[8 * a]) for a in range(n)]
        return out, own

    def start(rin, rout, send, recv):
        remote, own = copies(rin, rout, send, recv)
        for cp in remote + own:
            cp.start()

    def finish(rin, rout, send, recv):
        remote, own = copies(rin, rout, send, recv)
        for cp in remote + own:
            cp.wait()

    outs = [jax.ShapeDtypeStruct((8,) + p.shape, p.dtype) for p in packs]
    return _Rider(packs, outs, {}, 8 * n, start, finish)


def _sum_small(name, slots):
    def body(*refs):
        n = len(refs) // 2
        for r_ref, s_ref in zip(refs[:n], refs[n:]):
            tot = r_ref[0]
            for d in range(1, 8):
                tot = tot + r_ref[d]
            s_ref[...] = tot

    return pl.pallas_call(
        body, name=name, in_specs=[VMEM_FULL] * len(slots), out_specs=[VMEM_FULL] * len(slots),
        out_shape=[jax.ShapeDtypeStruct(s.shape[1:], s.dtype) for s in slots],
        compiler_params=pltpu.CompilerParams(vmem_limit_bytes=VMEM_LIMIT),
    )(*slots)


def _adamw(w, g, m, v):
    m2 = ADAM_B1 * m + (1.0 - ADAM_B1) * g
    v2 = ADAM_B2 * v + (1.0 - ADAM_B2) * (g * g)
    m_hat = m2 / (1.0 - ADAM_B1 ** ADAM_STEP)
    v_hat = v2 / (1.0 - ADAM_B2 ** ADAM_STEP)
    delta = -ADAM_LR * (m_hat / (jnp.sqrt(v_hat) + ADAM_EPS) + ADAM_WD * w)
    return delta, m2, v2


def _adam_big(name, w, g, m, v):
    r, c = w.shape
    tr = 128 if r % 128 == 0 else r

    def body(w_ref, g_ref, m_ref, v_ref, go_ref, d_ref, m2_ref, v2_ref):
        g = g_ref[...]
        go_ref[...] = g
        d_ref[...], m2_ref[...], v2_ref[...] = _adamw(w_ref[...], g, m_ref[...], v_ref[...])

    blk = pl.BlockSpec((tr, c), lambda i: (i, 0))
    return _call(
        body, (w, g, m, v), name=name, grid=(r // tr,), in_specs=[blk] * 4, out_specs=[blk] * 4,
        out_shape=[jax.ShapeDtypeStruct((r, c), F32)] * 4)


def _adam_small(s1, s2, s3, cw_g, fw_g, lb_logits, triples, layout, w):
    n = len(triples)

    def body(*refs):
        s1_ref, s2_ref, s3_ref, cw_ref, fw_ref, lbl_ref = refs[:6]
        prm = refs[6:6 + 3 * n]
        outs = refs[6 + 3 * n:]
        for p, lay in enumerate(layout):
            w_ref, m_ref, v_ref = prm[3 * p:3 * p + 3]
            g_ref, d_ref, m2_ref, v2_ref = outs[4 * p:4 * p + 4]
            if lay[0] in ("wide", "late"):
                _, row, r, pieces = lay
                src = s1_ref if lay[0] == "wide" else s3_ref
                for rr in range(r):
                    for mm in range(pieces):
                        g_ref[rr:rr + 1, mm * w:(mm + 1) * w] = src[row:row + 1, :]
                        row += 1
            elif lay[0] == "ffn":
                _, row, r = lay
                g_ref[...] = s2_ref[row:row + r, :]
            elif lay[0] == "cw":
                g_ref[...] = cw_ref[0:g_ref.shape[0], :]
            elif lay[0] == "fw":
                g_ref[...] = fw_ref[0:g_ref.shape[0], :]
            else:
                s0 = _lower_bound(lbl_ref)
                d0 = s1_ref[lay[1]:lay[1] + 1, :] * s0 * (1.0 - s0)
                g_ref[0:1, :] = d0
                g_ref[1:2, :] = -d0
            d_ref[...], m2_ref[...], v2_ref[...] = _adamw(w_ref[...], g_ref[...], m_ref[...], v_ref[...])

    flat = [a for tr in triples for a in tr]
    shapes = []
    for tr in triples:
        shapes.extend([jax.ShapeDtypeStruct(tr[0].shape, F32)] * 4)
    return pl.pallas_call(
        body, name="adam_small", in_specs=[VMEM_FULL] * (6 + 3 * n), out_specs=[VMEM_FULL] * (4 * n),
        out_shape=shapes, compiler_params=pltpu.CompilerParams(vmem_limit_bytes=VMEM_LIMIT),
    )(s1, s2, s3, cw_g, fw_g, lb_logits, *flat)


def _row_tile(t):
    return 512 if t % 512 == 0 and t >= 2048 else 128


def kernel(x, emb_ln_g, emb_ln_b, w_in, conv_w, conv_b, conv_norm_g, conv_norm_b, lb_logits, hgrn_norm_g, w_out, ln1_g, ln1_b, w_ffn_up, ffn_conv_w, ffn_conv_b, w_ffn_down, ln2_g, ln2_b, loss_target, m_emb_ln_g, m_emb_ln_b, m_w_in, m_conv_w, m_conv_b, m_conv_norm_g, m_conv_norm_b, m_lb_logits, m_hgrn_norm_g, m_w_out, m_ln1_g, m_ln1_b, m_w_ffn_up, m_ffn_conv_w, m_ffn_conv_b, m_w_ffn_down, m_ln2_g, m_ln2_b, v_emb_ln_g, v_emb_ln_b, v_w_in, v_conv_w, v_conv_b, v_conv_norm_g, v_conv_norm_b, v_lb_logits, v_hgrn_norm_g, v_w_out, v_ln1_g, v_ln1_b, v_w_ffn_up, v_ffn_conv_w, v_ffn_conv_b, v_w_ffn_down, v_ln2_g, v_ln2_b):
    depth = w_in.shape[0]
    assert depth == 1 and x.shape[0] == 1
    alpha = (2.0 * depth) ** 0.25
    t, d = x.shape[1], x.shape[2]
    w = d // 2
    dff = ffn_conv_b.shape[1]
    kc = conv_w.shape[1]
    assert w % (2 * LANE) == 0 and dff % (4 * LANE) == 0 and t % 128 == 0
    tm = _row_tile(t)
    tm2 = tm // 2
    tmm = 1024 if t % 1024 == 0 and t >= 2048 else tm
    cb = 2 * LANE
    cbf = 4 * LANE
    tb = tm
    nh = w // LANE
    hpb = 4 if nh % 4 == 0 else 2

    xi = lax.axis_index("x")
    yi = lax.axis_index("y")
    ci = lax.axis_index("c")
    chip = 2 * xi + yi
    c_idx = jnp.reshape(ci, (1,)).astype(jnp.int32)
    chip_idx = jnp.reshape(chip, (1,)).astype(jnp.int32)
    place_idx = jnp.stack([chip, ci]).astype(jnp.int32)

    x2 = x[0]
    tgt = loss_target[0]
    g0, b0 = emb_ln_g.reshape(1, d), emb_ln_b.reshape(1, d)
    w_in2, w_out2, w_up2, w_dn2 = w_in[0], w_out[0], w_ffn_up[0], w_ffn_down[0]
    cw2, fw2 = conv_w[0], ffn_conv_w[0]

    b_in = _place_shard(w_in2, "place_w_in", chip_idx, BF16)
    b_out = _place_shard(w_out2, "place_w_out", chip_idx, BF16)
    b_up = _place_shard(w_up2, "place_w_up", chip_idx, BF16)
    b_dn = _place_shard(w_dn2, "place_w_down", chip_idx, BF16)
    b_cw = _place_shard(_pad_rows(cw2), "place_conv_w", chip_idx, F32)
    b_fw = _place_shard(_pad_rows(fw2), "place_ffn_conv_w", chip_idx, F32)
    tmi = 2 * tmm if t % (2 * tmm) == 0 else tmm
    h0b, (b_in,) = _ln0(x2, g0, b0, tm, rider=_ride_gather_ici([b_in], [(0, 1, 8)]))
    p3, first = _proj_own("in_proj_own", h0b, w_in2, 6, N_CHIPS, tmi, w // 2, chip_idx,
                          rider=_ride_gather_ici([b_in, b_cw, b_fw], [(1, 8, 8), None, None]))
    w_in3, cw_full3, fw_full3 = _run("gather_first_d2d", _ride_gather_d2d(first))
    cw_full = _unshard_cols(cw_full3)[:kc]
    fw_full = _unshard_cols(fw_full3)[:fw2.shape[0]]

    p3, (b_out, b_up) = _proj_rest("in_proj", h0b, w_in3, p3, tmi, w // 2, chip_idx,
                                   rider=_ride_gather_ici([b_out, b_up], [None, (0, 1, 4)]))
    (cat, u1), (w_out3, b_up) = _conv_fwd(
        p3, cw_full, conv_b, conv_norm_g, conv_norm_b, tm2, cb,
        rider=_ride_both(_ride_gather_d2d([b_out]), _ride_gather_ici([b_up], [(1, 2, 4)])))
    w_out_full = w_out3.reshape(d, d)
    (cat, o_pre, states), got = _hgrn_fwd(p3, lb_logits, hgrn_norm_g, cat, tb, hpb,
                                          rider=_ride_gather_ici([b_up], [(2, 4, 4)]))
    (xhat1, h1b, rstd1), (w_up3,) = _mix_ln1(cat, w_out_full, x2, g0, b0, ln1_g, ln1_b, alpha, tm2,
                                             rider=_ride_gather_d2d(got))
    hh3, got = _proj("ffn_up", h1b, w_up3, 2, tmm, dff // 4, rider=_ride_gather_ici([b_dn]))
    act, (w_dn3,) = _ffn_act_fwd(hh3, fw_full, ffn_conv_b, tm, cbf, rider=_ride_gather_d2d(got))
    ks = dff // N_CHIPS
    ffn = _wgrad("ffn_down", act, w_dn3, (t, d), (t // tmm, 1, N_CHIPS),
                 pl.BlockSpec((tmm, ks), lambda i, j, k: (i, k)),
                 pl.BlockSpec((None, ks, d), lambda i, j, k: (k, 0, 0)),
                 pl.BlockSpec((tmm, d), lambda i, j, k: (i, 0)), dot=_dot)
    dz2, dz2b, dg2, db2, loss_row = _ln2_loss(ffn, xhat1, tgt, ln1_g, ln1_b, ln2_g, ln2_b, alpha, tm2)

    dact = _proj_t("ffn_down_t", dz2b, w_dn3.reshape(dff, d), tmm, ks)
    dhh3, dfw, dfb = _ffn_act_bwd(dact, hh3, fw_full, ffn_conv_b, tm, cbf)
    tt = 2 * tmm if t % (2 * tmm) == 0 else tmm
    d_w_dn = _wgrad("wgrad_down", act, dz2b, (N_CHIPS, ks, d), (N_CHIPS, 2, t // tt),
                    pl.BlockSpec((tt, ks), lambda s, j, k: (k, s)),
                    pl.BlockSpec((tt, d // 2), lambda s, j, k: (k, j)),
                    pl.BlockSpec((None, ks, d // 2), lambda s, j, k: (s, 0, j)))
    wu = 2 * dff // N_CHIPS
    tnu = wu // 2
    per_sec_u = dff // tnu
    pre1, (arr_dn,) = _wgrad(
        "up_t", dhh3, w_up3, (t, d), (t // tmm, 1, 2 * N_CHIPS),
        pl.BlockSpec((None, tmm, tnu), lambda i, j, k: (k // per_sec_u, i, k % per_sec_u)),
        pl.BlockSpec((None, d, tnu), lambda i, j, k: (k // 2, 0, k % 2)),
        pl.BlockSpec((tmm, d), lambda i, j, k: (i, 0)), dot=_dot_nt, rider=_ride_swap([d_w_dn]))
    dz1, dz1b, dg1, db1 = _ln1_bwd(pre1, dz2, xhat1, rstd1, ln1_g, alpha, tm2)
    part_dn = _add_halves("add_halves_w_down", d_w_dn, arr_dn, c_idx)
    d_w_up, (land_dn,) = _wgrad(
        "wgrad_up", h1b, dhh3, (N_CHIPS, d, wu), (N_CHIPS, 2, 2, t // tt),
        pl.BlockSpec((tt, d // 2), lambda s, r, j, k: (k, r)),
        pl.BlockSpec((None, tt, tnu), lambda s, r, j, k: ((2 * s + j) // per_sec_u, k, (2 * s + j) % per_sec_u)),
        pl.BlockSpec((None, d // 2, tnu), lambda s, r, j, k: (s, r, j)), rider=_ride_send_partials([part_dn]))
    dcat = _proj_t("out_proj_t", dz1b, w_out_full, tmm, d // 2)
    d_w_out = _wgrad("wgrad_out", cat, dz1b, (d, d), (2, 2, t // tt),
                     pl.BlockSpec((tt, d // 2), lambda r, j, k: (k, r)),
                     pl.BlockSpec((tt, d // 2), lambda r, j, k: (k, j)),
                     pl.BlockSpec((d // 2, d // 2), lambda r, j, k: (r, j))).reshape(N_CHIPS, d // N_CHIPS, d)
    du1, dcng, dcnb = _conv_norm_bwd(dcat, u1, conv_norm_g, conv_norm_b, tm)
    (dp3, dcw, dcb), (arr_up, arr_out) = _conv_bwd(du1, p3, cw_full, tm2, cb, rider=_ride_swap([d_w_up, d_w_out]))
    part_up = _add_halves("add_halves_w_up", d_w_up, arr_up, c_idx)
    part_out = _add_halves("add_halves_w_out", d_w_out, arr_out, c_idx)
    (dp3, dlb, dhg), (land_up, land_out) = _hgrn_bwd(
        p3, lb_logits, hgrn_norm_g, o_pre, states, dcat, dp3, tb, hpb,
        rider=_ride_send_partials([part_up, part_out], [(0, 3, 4), None]))
    kpad = dcw.shape[0]
    wide = [(dcw, kpad, 1), (dg1, 1, 2), (db1, 1, 2), (dg2, 1, 2), (db2, 1, 2),
            (dcb, 1, 1), (dcng, 1, 1), (dcnb, 1, 1), (dlb, 1, 1), (dhg, 1, 1)]
    n_wide = -(-sum(r * m for _, r, m in wide) // SUBLANE) * SUBLANE
    packs = _pack_small("pack_small", wide, [dfw, dfb], w, dff, n_wide, 2 * SUBLANE)
    d_in_theirs, (land_up, slots1, slots2) = _wgrad_in(
        "wgrad_in_a", h0b, dp3, N_CHIPS, tt, 1 - c_idx,
        rider=_ride_both(_ride_send_partials([part_up], [(3, 4, 4)], into=[land_up]), _ride_exchange8(packs)))
    s1, s2 = _sum_small("sum_small", [slots1, slots2])
    d_in_mine, (arr_in,) = _wgrad_in("wgrad_in_b", h0b, dp3, N_CHIPS, tt, c_idx,
                                     rider=_ride_to_sibling([d_in_theirs]))
    part_in = _add_cast("add_halves_w_in", d_in_mine, arr_in)
    h_out, h_up, h_dn = [
        _sum_partials("sum_partials_" + nm, p, a, place_idx)
        for nm, p, a in (("w_out", part_out, land_out), ("w_up", part_up, land_up), ("w_down", part_dn, land_dn))]
    pre0, (land_in, g_w_out, g_w_up, g_w_dn) = _in_t(
        dp3, w_in3, tmm, rider=_ride_both(_ride_send_partials([part_in]), _ride_join([h_out, h_up, h_dn])))
    dx, dg0, db0 = _ln0_bwd(pre0, dz1, x2, g0, alpha, tm2)
    h_in = _sum_partials("sum_partials_w_in", part_in, land_in, place_idx)
    (g_w_in,) = _run("join_w_in", _ride_join([h_in]))

    late = _pack_small("pack_late", [(dg0, 1, 2), (db0, 1, 2)], [], w, dff, SUBLANE, 0)
    (s3,) = _sum_small("sum_late", _run("exchange_late", _ride_exchange8(late)))
    cw_g = lax.dynamic_slice_in_dim(s1[0:kpad], chip * (w // N_CHIPS), w // N_CHIPS, axis=1)
    fw_g = lax.dynamic_slice_in_dim(s2[0:SUBLANE], chip * (dff // N_CHIPS), dff // N_CHIPS, axis=1)

    small = [
        (g0, m_emb_ln_g.reshape(1, d), v_emb_ln_g.reshape(1, d)), (b0, m_emb_ln_b.reshape(1, d), v_emb_ln_b.reshape(1, d)),
        (cw2, m_conv_w[0], v_conv_w[0]), (conv_b, m_conv_b, v_conv_b),
        (conv_norm_g, m_conv_norm_g, v_conv_norm_g), (conv_norm_b, m_conv_norm_b, v_conv_norm_b),
        (lb_logits, m_lb_logits, v_lb_logits), (hgrn_norm_g, m_hgrn_norm_g, v_hgrn_norm_g),
        (ln1_g, m_ln1_g, v_ln1_g), (ln1_b, m_ln1_b, v_ln1_b),
        (fw2, m_ffn_conv_w[0], v_ffn_conv_w[0]), (ffn_conv_b, m_ffn_conv_b, v_ffn_conv_b),
        (ln2_g, m_ln2_g, v_ln2_g), (ln2_b, m_ln2_b, v_ln2_b),
    ]
    r0 = kpad
    layout = [("late", 0, 1, 2), ("late", 2, 1, 2), ("cw",), ("wide", r0 + 8, 1, 1), ("wide", r0 + 9, 1, 1),
              ("wide", r0 + 10, 1, 1), ("lb", r0 + 11), ("wide", r0 + 12, 1, 1), ("wide", r0, 1, 2),
              ("wide", r0 + 2, 1, 2), ("fw",), ("ffn", SUBLANE, 1), ("wide", r0 + 4, 1, 2), ("wide", r0 + 6, 1, 2)]
    so = _adam_small(s1, s2, s3, cw_g, fw_g, lb_logits, small, layout, w)
    sm = {nm: so[4 * i:4 * i + 4] for i, nm in enumerate(
        ["emb_ln_g", "emb_ln_b", "conv_w", "conv_b", "conv_norm_g", "conv_norm_b", "lb_logits", "hgrn_norm_g",
         "ln1_g", "ln1_b", "ffn_conv_w", "ffn_conv_b", "ln2_g", "ln2_b"])}
    bigs = {}
    for nm, wt, g, m, v in (("w_in", w_in2, g_w_in, m_w_in[0], v_w_in[0]), ("w_out", w_out2, g_w_out, m_w_out[0], v_w_out[0]),
                            ("w_ffn_up", w_up2, g_w_up, m_w_ffn_up[0], v_w_ffn_up[0]),
                            ("w_ffn_down", w_dn2, g_w_dn, m_w_ffn_down[0], v_w_ffn_down[0])):
        bigs[nm] = tuple(_adam_big("adam_" + nm, wt, g, m, v))

    loss = lax.psum(loss_row[0, 0], ("x", "y", "c"))

    order = ["emb_ln_g", "emb_ln_b", "w_in", "conv_w", "conv_b", "conv_norm_g", "conv_norm_b", "lb_logits",
             "hgrn_norm_g", "w_out", "ln1_g", "ln1_b", "w_ffn_up", "ffn_conv_w", "ffn_conv_b", "w_ffn_down",
             "ln2_g", "ln2_b"]
    shapes = dict(emb_ln_g=emb_ln_g.shape, emb_ln_b=emb_ln_b.shape, w_in=w_in.shape, conv_w=conv_w.shape,
                  w_out=w_out.shape, w_ffn_up=w_ffn_up.shape, ffn_conv_w=ffn_conv_w.shape, w_ffn_down=w_ffn_down.shape)
    outs = [loss, dx.reshape(x.shape)]
    for which in range(4):
        for nm in order:
            a = bigs[nm][which] if nm in bigs else sm[nm][which]
            outs.append(a.reshape(shapes[nm]) if nm in shapes else a)
    return tuple(outs)


def _pad_rows(a):
    k = a.shape[0]
    kp = -(-k // 16) * 16
    return jnp.pad(a, ((0, kp - k), (0, 0)))


def _unshard_cols(a3):
    s, k, c = a3.shape
    return jnp.transpose(a3, (1, 0, 2)).reshape(k, s * c)
```

```python
import functools

import jax
import jax.numpy as jnp
from jax import lax
from jax.experimental import pallas as pl
from jax.experimental.pallas import tpu as pltpu

F32 = jnp.float32
BF16 = jnp.bfloat16

LN_EPS = 1e-5
RMS_EPS = 1e-6
LANE = 128
SUBLANE = 8
CHUNK = 64
SUB = 8
HALO = 32
FHALO = 8
ROWS = 64
N_CHIPS = 4
VMEM_LIMIT = 56 << 20
NEG_BIG = -1e30

ADAM_LR = 0.001
ADAM_B1 = 0.9
ADAM_B2 = 0.999
ADAM_EPS = 1e-08
ADAM_WD = 0.01
ADAM_STEP = 10

MESH = pl.DeviceIdType.MESH
HBM = pl.BlockSpec(memory_space=pl.ANY)
VMEM_FULL = pl.BlockSpec(memory_space=pltpu.VMEM)


def _params(*sem):
    return pltpu.CompilerParams(dimension_semantics=sem, vmem_limit_bytes=VMEM_LIMIT)


class _Rider:
    def __init__(self, ins, outs, aliases, n_sems, start, finish):
        self.ins, self.outs, self.aliases = list(ins), list(outs), dict(aliases)
        self.n_sems, self.start, self.finish = n_sems, start, finish


def _call(body, args, *, name, grid, in_specs, out_specs, out_shape, scratch_shapes=(), aliases=None, rider=None,
          prefetch=()):
    many = isinstance(out_shape, (list, tuple))
    shapes = list(out_shape) if many else [out_shape]
    ospecs = list(out_specs) if many else [out_specs]
    npf = len(prefetch)
    aliases = {npf + i: o for i, o in (aliases or {}).items()}
    sem = ("arbitrary",) * len(grid)
    n_in, n_out, n_scr = len(args), len(shapes), len(scratch_shapes)
    if rider is None:
        res = pl.pallas_call(
            body, name=name,
            grid_spec=pltpu.PrefetchScalarGridSpec(
                num_scalar_prefetch=npf, grid=grid, in_specs=list(in_specs), out_specs=ospecs,
                scratch_shapes=list(scratch_shapes)),
            out_shape=shapes, input_output_aliases=aliases, compiler_params=_params(*sem))(*prefetch, *args)
        return res if many else res[0]
    nri, nro = len(rider.ins), len(rider.outs)

    def wrapped(*refs):
        pre, refs = refs[:npf], refs[npf:]
        ins, rin = refs[:n_in], refs[n_in:n_in + nri]
        o0 = n_in + nri
        outs, rout = refs[o0:o0 + n_out], refs[o0 + n_out:o0 + n_out + nro]
        s0 = o0 + n_out + nro
        scr, (send, recv) = refs[s0:s0 + n_scr], refs[s0 + n_scr:]
        ids = [pl.program_id(a) for a in range(len(grid))]
        first = functools.reduce(jnp.logical_and, [i == 0 for i in ids])
        last = functools.reduce(jnp.logical_and, [i == g - 1 for i, g in zip(ids, grid)])

        @pl.when(first)
        def _():
            rider.start(rin, rout, send, recv)

        body(*pre, *ins, *outs, *scr)

        @pl.when(last)
        def _():
            rider.finish(rin, rout, send, recv)

    for ri, ro in rider.aliases.items():
        aliases[npf + n_in + ri] = n_out + ro
    res = pl.pallas_call(
        wrapped, name=name,
        grid_spec=pltpu.PrefetchScalarGridSpec(
            num_scalar_prefetch=npf, grid=grid, in_specs=list(in_specs) + [HBM] * nri,
            out_specs=ospecs + [HBM] * nro,
            scratch_shapes=list(scratch_shapes) + [pltpu.SemaphoreType.DMA((rider.n_sems,)),
                                                   pltpu.SemaphoreType.DMA((rider.n_sems,))]),
        out_shape=shapes + rider.outs,
        input_output_aliases=aliases, compiler_params=_params(*sem))(*prefetch, *args, *rider.ins)
    main, extra = res[:n_out], list(res[n_out:])
    return (list(main) if many else main[0]), extra


def _div_tile(n, mult, cap):
    best = n
    for t in range(mult, min(n, cap) + 1, mult):
        if n % t == 0:
            best = t
    return best


def _sigmoid(x):
    return 1.0 / (1.0 + jnp.exp(-x))


def _ln_stats(x):
    mu = jnp.mean(x, axis=-1, keepdims=True)
    xc = x - mu
    var = jnp.mean(xc * xc, axis=-1, keepdims=True)
    rstd = lax.rsqrt(var + LN_EPS)
    return xc * rstd, rstd


def _ln_bwd(dy, xhat, rstd, g):
    dyg = dy * g
    m1 = jnp.mean(dyg, axis=-1, keepdims=True)
    m2 = jnp.mean(dyg * xhat, axis=-1, keepdims=True)
    return rstd * (dyg - m1 - xhat * m2)


def _dot_nt(a, b):
    return lax.dot_general(a, b, (((1,), (1,)), ((), ())), preferred_element_type=F32)


def _dot_tn(a, b):
    return lax.dot_general(a, b, (((0,), (0,)), ((), ())), preferred_element_type=F32)


def _dot(a, b):
    return jnp.dot(a, b, preferred_element_type=F32)


def _dot3(m, x):
    mb = m.astype(BF16)
    x1 = x.astype(BF16)
    r1 = x - x1.astype(F32)
    x2 = r1.astype(BF16)
    x3 = (r1 - x2.astype(F32)).astype(BF16)
    return _dot(mb, x1) + _dot(mb, x2) + _dot(mb, x3)


def _place_shard(x, name, chip_idx, dtype):
    r, c = x.shape
    tr = _div_tile(r, 16, 512)

    def body(s_ref, x_ref, o_ref):
        del s_ref
        o_ref[...] = x_ref[...].astype(dtype)

    return pl.pallas_call(
        body, name=name,
        grid_spec=pltpu.PrefetchScalarGridSpec(
            num_scalar_prefetch=1, grid=(r // tr,),
            in_specs=[pl.BlockSpec((tr, c), lambda i, s: (i, 0))],
            out_specs=pl.BlockSpec((None, tr, c), lambda i, s: (s[0], i, 0))),
        out_shape=jax.ShapeDtypeStruct((N_CHIPS, r, c), dtype),
        compiler_params=_params("parallel"),
    )(chip_idx, x)


def _ln0(x, g, b, tm, rider=None):
    t, d = x.shape

    def body(x_ref, g_ref, b_ref, o_ref):
        xh, _ = _ln_stats(x_ref[...])
        o_ref[...] = (xh * g_ref[...] + b_ref[...]).astype(BF16)

    row = pl.BlockSpec((1, d), lambda i: (0, 0))
    return _call(
        body, (x, g, b), name="ln0", grid=(t // tm,),
        in_specs=[pl.BlockSpec((tm, d), lambda i: (i, 0)), row, row],
        out_specs=pl.BlockSpec((tm, d), lambda i: (i, 0)),
        out_shape=jax.ShapeDtypeStruct((t, d), BF16), rider=rider)


def _proj(name, a, w3, n_sec, tm, tn, rider=None):
    m, k = a.shape
    s, _, ws = w3.shape
    sec_w = s * ws // n_sec
    nj = ws // tn
    per_sec = sec_w // tn

    def body(a_ref, w_ref, o_ref):
        o_ref[...] = _dot(a_ref[...], w_ref[...])

    return _call(
        body, (a, w3), name=name, grid=(s * nj, m // tm),
        in_specs=[pl.BlockSpec((tm, k), lambda j, i: (i, 0)),
                  pl.BlockSpec((None, k, tn), lambda j, i: (j // nj, 0, j % nj))],
        out_specs=pl.BlockSpec((None, tm, tn), lambda j, i: (j // per_sec, i, j % per_sec)),
        out_shape=jax.ShapeDtypeStruct((n_sec, m, sec_w), F32), rider=rider)


def _proj_t(name, a, w, tm, tn, rider=None):
    m, k = a.shape
    n = w.shape[0]

    def body(a_ref, w_ref, o_ref):
        o_ref[...] = _dot_nt(a_ref[...], w_ref[...])

    return _call(
        body, (a, w), name=name, grid=(n // tn, m // tm),
        in_specs=[pl.BlockSpec((tm, k), lambda j, i: (i, 0)),
                  pl.BlockSpec((tn, k), lambda j, i: (j, 0))],
        out_specs=pl.BlockSpec((tm, tn), lambda j, i: (i, j)),
        out_shape=jax.ShapeDtypeStruct((m, n), F32), rider=rider)


def _wgrad(name, a, b, out_shape, grid, a_spec, b_spec, o_spec, rider=None, dot=_dot_tn):
    nt = len(grid) - 1

    def body(a_ref, b_ref, o_ref):
        t = pl.program_id(nt)
        prod = dot(a_ref[...], b_ref[...])

        @pl.when(t == 0)
        def _():
            o_ref[...] = prod

        @pl.when(t > 0)
        def _():
            o_ref[...] += prod

    return _call(
        body, (a, b), name=name, grid=grid, in_specs=[a_spec, b_spec], out_specs=o_spec,
        out_shape=jax.ShapeDtypeStruct(out_shape, F32), rider=rider)


def _wgrad_in(h0b, dp3, n_shards, tt, rider=None):
    t, d = h0b.shape
    n_sec, _, sec_w = dp3.shape
    ws = n_sec * sec_w // n_shards
    tn = sec_w // 2
    nq = ws // tn
    tr = d // 2

    def body(*refs):
        a_ref, b_refs, o_ref = refs[0], refs[1:1 + nq], refs[1 + nq]
        k = pl.program_id(2)
        a = a_ref[...]
        prods = [_dot_tn(a, b_ref[...]) for b_ref in b_refs]

        @pl.when(k == 0)
        def _():
            for q in range(nq):
                o_ref[:, q * tn:(q + 1) * tn] = prods[q]

        @pl.when(k > 0)
        def _():
            for q in range(nq):
                o_ref[:, q * tn:(q + 1) * tn] += prods[q]

    def b_spec(q):
        return pl.BlockSpec((None, tt, tn), lambda s, r, k: ((nq * s + q) // 2, k, (nq * s + q) % 2))

    return _call(
        body, (h0b,) + (dp3,) * nq, name="wgrad_in", grid=(n_shards, d // tr, t // tt),
        in_specs=[pl.BlockSpec((tt, tr), lambda s, r, k: (k, r))] + [b_spec(q) for q in range(nq)],
        out_specs=pl.BlockSpec((None, tr, ws), lambda s, r, k: (s, r, 0)),
        out_shape=jax.ShapeDtypeStruct((n_shards, d, ws), F32), rider=rider)


def _in_t(dp3, w_in3, tm, rider=None):
    _, t, sec_w = dp3.shape
    s, d, ws = w_in3.shape
    tk = sec_w // 2
    nq = ws // tk

    def body(*refs):
        a_refs, w_ref, o_ref = refs[:nq], refs[nq], refs[nq + 1]
        k = pl.program_id(1)
        prod = _dot_nt(a_refs[0][...], w_ref[:, 0:tk])
        for q in range(1, nq):
            prod = prod + _dot_nt(a_refs[q][...], w_ref[:, q * tk:(q + 1) * tk])

        @pl.when(k == 0)
        def _():
            o_ref[...] = prod

        @pl.when(k > 0)
        def _():
            o_ref[...] += prod

    def a_spec(q):
        return pl.BlockSpec((None, tm, tk), lambda i, k: ((nq * k + q) // 2, i, (nq * k + q) % 2))

    return _call(
        body, (dp3,) * nq + (w_in3,), name="in_t", grid=(t // tm, s),
        in_specs=[a_spec(q) for q in range(nq)] + [pl.BlockSpec((None, d, ws), lambda i, k: (k, 0, 0))],
        out_specs=pl.BlockSpec((tm, d), lambda i, k: (i, 0)),
        out_shape=jax.ShapeDtypeStruct((t, d), F32), rider=rider)


def _mix_ln1(cat, w_out, x, g0, b0, g1, b1, alpha, tm, rider=None):
    t, d = x.shape

    def body(cat_ref, w_ref, x_ref, g0_ref, b0_ref, g1_ref, b1_ref, xh_ref, h1b_ref, rstd_ref):
        mix = _dot(cat_ref[...], w_ref[...])
        xh0, _ = _ln_stats(x_ref[...])
        z1 = alpha * (xh0 * g0_ref[...] + b0_ref[...]) + mix
        xh1, rstd1 = _ln_stats(z1)
        xh_ref[...] = xh1
        h1b_ref[...] = (xh1 * g1_ref[...] + b1_ref[...]).astype(BF16)
        rstd_ref[...] = rstd1

    row = pl.BlockSpec((1, d), lambda i: (0, 0))
    blk = pl.BlockSpec((tm, d), lambda i: (i, 0))
    return _call(
        body, (cat, w_out, x, g0, b0, g1, b1), name="mix_ln1", grid=(t // tm,),
        in_specs=[blk, pl.BlockSpec((d, d), lambda i: (0, 0)), blk, row, row, row, row],
        out_specs=[blk, blk, pl.BlockSpec((tm, 1), lambda i: (i, 0))],
        out_shape=[jax.ShapeDtypeStruct((t, d), F32), jax.ShapeDtypeStruct((t, d), BF16),
                   jax.ShapeDtypeStruct((t, 1), F32)], rider=rider)


def _ln2_loss(ffn, xhat1, tgt, g1, b1, g2, b2, alpha, tm):
    t, d = xhat1.shape
    ni = t // tm
    inv_d = 1.0 / d

    def body(ffn_ref, xh1_ref, tgt_ref, g1_ref, b1_ref, g2_ref, b2_ref,
             dz2_ref, dz2b_ref, dg2_ref, db2_ref, loss_ref, lrow):
        i = pl.program_id(0)
        h1 = xh1_ref[...] * g1_ref[...] + b1_ref[...]
        xh2, rstd2 = _ln_stats(alpha * h1 + ffn_ref[...])
        g2v = g2_ref[...]
        diff = xh2 * g2v + b2_ref[...] - tgt_ref[...]
        dh2 = diff * inv_d
        sq = jnp.sum(diff * diff, axis=0, keepdims=True)
        dg = jnp.sum(dh2 * xh2, axis=0, keepdims=True)
        db = jnp.sum(dh2, axis=0, keepdims=True)

        @pl.when(i == 0)
        def _():
            lrow[...] = sq
            dg2_ref[...] = dg
            db2_ref[...] = db

        @pl.when(i > 0)
        def _():
            lrow[...] += sq
            dg2_ref[...] += dg
            db2_ref[...] += db

        dz2 = _ln_bwd(dh2, xh2, rstd2, g2v)
        dz2_ref[...] = dz2
        dz2b_ref[...] = dz2.astype(BF16)

        @pl.when(i == ni - 1)
        def _():
            tot = jnp.sum(lrow[...], axis=-1, keepdims=True) * (0.5 * inv_d)
            loss_ref[...] = jnp.broadcast_to(tot, (1, LANE))

    row = pl.BlockSpec((1, d), lambda i: (0, 0))
    blk = pl.BlockSpec((tm, d), lambda i: (i, 0))
    return _call(
        body, (ffn, xhat1, tgt, g1, b1, g2, b2), name="ln2_loss", grid=(ni,),
        in_specs=[blk, blk, blk, row, row, row, row],
        out_specs=[blk, blk, row, row, pl.BlockSpec((1, LANE), lambda i: (0, 0))],
        out_shape=[jax.ShapeDtypeStruct((t, d), F32), jax.ShapeDtypeStruct((t, d), BF16),
                   jax.ShapeDtypeStruct((1, d), F32), jax.ShapeDtypeStruct((1, d), F32),
                   jax.ShapeDtypeStruct((1, LANE), F32)],
        scratch_shapes=[pltpu.VMEM((1, d), F32)])


def _ln1_bwd(pre, dz2, xhat1, rstd1, g1, alpha, tm):
    t, d = dz2.shape

    def body(pre_ref, dz2_ref, xh_ref, rstd_ref, g_ref, dz1_ref, dz1b_ref, dg_ref, db_ref):
        i = pl.program_id(0)
        dh1 = alpha * dz2_ref[...] + pre_ref[...]
        xh = xh_ref[...]
        dg = jnp.sum(dh1 * xh, axis=0, keepdims=True)
        db = jnp.sum(dh1, axis=0, keepdims=True)

        @pl.when(i == 0)
        def _():
            dg_ref[...] = dg
            db_ref[...] = db

        @pl.when(i > 0)
        def _():
            dg_ref[...] += dg
            db_ref[...] += db

        dz1 = _ln_bwd(dh1, xh, rstd_ref[...], g_ref[...])
        dz1_ref[...] = dz1
        dz1b_ref[...] = dz1.astype(BF16)

    row = pl.BlockSpec((1, d), lambda i: (0, 0))
    blk = pl.BlockSpec((tm, d), lambda i: (i, 0))
    return _call(
        body, (pre, dz2, xhat1, rstd1, g1), name="ln1_bwd", grid=(t // tm,),
        in_specs=[blk, blk, blk, pl.BlockSpec((tm, 1), lambda i: (i, 0)), row],
        out_specs=[blk, blk, row, row],
        out_shape=[jax.ShapeDtypeStruct((t, d), F32), jax.ShapeDtypeStruct((t, d), BF16),
                   jax.ShapeDtypeStruct((1, d), F32), jax.ShapeDtypeStruct((1, d), F32)])


def _ln0_bwd(pre, dz1, x, g0, alpha, tm):
    t, d = x.shape

    def body(pre_ref, dz1_ref, x_ref, g_ref, dx_ref, dg_ref, db_ref):
        i = pl.program_id(0)
        dh0 = alpha * dz1_ref[...] + pre_ref[...]
        xh, rstd = _ln_stats(x_ref[...])
        dg = jnp.sum(dh0 * xh, axis=0, keepdims=True)
        db = jnp.sum(dh0, axis=0, keepdims=True)

        @pl.when(i == 0)
        def _():
            dg_ref[...] = dg
            db_ref[...] = db

        @pl.when(i > 0)
        def _():
            dg_ref[...] += dg
            db_ref[...] += db

        dx_ref[...] = _ln_bwd(dh0, xh, rstd, g_ref[...])

    row = pl.BlockSpec((1, d), lambda i: (0, 0))
    blk = pl.BlockSpec((tm, d), lambda i: (i, 0))
    return _call(
        body, (pre, dz1, x, g0), name="ln0_bwd", grid=(t // tm,),
        in_specs=[blk, blk, blk, row], out_specs=[blk, row, row],
        out_shape=[jax.ShapeDtypeStruct((t, d), F32), jax.ShapeDtypeStruct((1, d), F32),
                   jax.ShapeDtypeStruct((1, d), F32)])


def _shift_copies(ext, shifted):
    n = shifted.shape[1]
    for p in range(1, SUBLANE):
        shifted[p - 1] = ext[pl.ds(p, n), :]


def _window(ext, shifted, start, rows):
    p = start % SUBLANE
    if p == 0:
        return ext[pl.ds(start, rows), :]
    return shifted[p - 1, pl.ds(start - p, rows), :]


def _conv_fwd(p3, conv_w, conv_b, cn_g, cn_b, tc, cb, rider=None):
    _, t, w = p3.shape
    kk = conv_w.shape[0]
    off = HALO - (kk - 1)
    hb = tc // HALO

    def body(a_ref, g_ref, ap_ref, gp_ref, w_ref, b_ref, ng_ref, nb_ref, cat_ref, u1_ref, ext, sh):
        i = pl.program_id(1)
        ext[pl.ds(HALO, tc), :] = a_ref[...] * _sigmoid(g_ref[...])
        prev = ap_ref[...] * _sigmoid(gp_ref[...])
        ext[pl.ds(0, HALO), :] = jnp.where(i > 0, prev, 0.0)
        _shift_copies(ext, sh)
        for r in range(tc // ROWS):
            acc = jnp.broadcast_to(b_ref[...], (ROWS, cb))
            for k in range(kk):
                acc = acc + w_ref[k:k + 1, :] * _window(ext, sh, r * ROWS + off + k, ROWS)
            u1_ref[pl.ds(r * ROWS, ROWS), :] = acc
            for g in range(cb // LANE):
                sl = slice(g * LANE, (g + 1) * LANE)
                xh, _ = _ln_stats(acc[:, sl])
                u2 = xh * ng_ref[:, sl] + nb_ref[:, sl]
                cat_ref[pl.ds(r * ROWS, ROWS), sl] = (u2 * _sigmoid(u2)).astype(BF16)

    cur = lambda sec: pl.BlockSpec((None, tc, cb), lambda j, i: (sec, i, j))
    prev = lambda sec: pl.BlockSpec((None, HALO, cb), lambda j, i: (sec, jnp.maximum(i * hb - 1, 0), j))
    row = pl.BlockSpec((1, cb), lambda j, i: (0, j))
    return _call(
        body, (p3, p3, p3, p3, conv_w, conv_b, cn_g, cn_b), name="conv_fwd", grid=(w // cb, t // tc),
        in_specs=[cur(0), cur(1), prev(0), prev(1), pl.BlockSpec((kk, cb), lambda j, i: (0, j)), row, row, row],
        out_specs=[pl.BlockSpec((tc, cb), lambda j, i: (i, j)), pl.BlockSpec((tc, cb), lambda j, i: (i, j))],
        out_shape=[jax.ShapeDtypeStruct((t, 2 * w), BF16), jax.ShapeDtypeStruct((t, w), F32)],
        scratch_shapes=[pltpu.VMEM((tc + HALO, cb), F32),
                        pltpu.VMEM((SUBLANE - 1, tc + HALO - SUBLANE, cb), F32)], rider=rider)


def _conv_norm_bwd(dcat, u1, cn_g, cn_b, tc):
    t, w = u1.shape

    def body(du_ref, u1_ref, ng_ref, nb_ref, du1_ref, dg_ref, db_ref):
        i = pl.program_id(0)
        for g in range(w // LANE):
            sl = slice(g * LANE, (g + 1) * LANE)
            ng = ng_ref[:, sl]
            xh, rstd = _ln_stats(u1_ref[:, sl])
            u2 = xh * ng + nb_ref[:, sl]
            sg = _sigmoid(u2)
            du2 = du_ref[:, sl] * (sg * (1.0 + u2 * (1.0 - sg)))
            dg = jnp.sum(du2 * xh, axis=0, keepdims=True)
            db = jnp.sum(du2, axis=0, keepdims=True)

            @pl.when(i == 0)
            def _():
                dg_ref[:, sl] = dg
                db_ref[:, sl] = db

            @pl.when(i > 0)
            def _():
                dg_ref[:, sl] += dg
                db_ref[:, sl] += db

            du1_ref[:, sl] = _ln_bwd(du2, xh, rstd, ng)

    row = pl.BlockSpec((1, w), lambda i: (0, 0))
    blk = pl.BlockSpec((tc, w), lambda i: (i, 0))
    return pl.pallas_call(
        body, name="conv_norm_bwd", grid=(t // tc,),
        in_specs=[blk, blk, row, row], out_specs=[blk, row, row],
        out_shape=[jax.ShapeDtypeStruct((t, w), F32), jax.ShapeDtypeStruct((1, w), F32),
                   jax.ShapeDtypeStruct((1, w), F32)],
        compiler_params=_params("arbitrary"),
    )(dcat, u1, cn_g, cn_b)


def _conv_bwd(du1, p3, conv_w, tc, cb, rider=None):
    n_sec, t, w = p3.shape
    kk = conv_w.shape[0]
    off = HALO - (kk - 1)
    hb = tc // HALO
    nt = t // tc
    kpad = -(-kk // SUBLANE) * SUBLANE

    def body(d_ref, dn_ref, a_ref, g_ref, ap_ref, gp_ref, w_ref, dp_ref, dw_ref, db_ref,
             extd, extu, shd, shu, wacc, bacc):
        i = pl.program_id(1)

        @pl.when(i == 0)
        def _():
            wacc[...] = jnp.zeros_like(wacc)
            bacc[...] = jnp.zeros_like(bacc)

        extd[pl.ds(0, tc), :] = d_ref[...]
        extd[pl.ds(tc, HALO), :] = jnp.where(i < nt - 1, dn_ref[...], 0.0)
        extu[pl.ds(HALO, tc), :] = a_ref[...] * _sigmoid(g_ref[...])
        extu[pl.ds(0, HALO), :] = jnp.where(i > 0, ap_ref[...] * _sigmoid(gp_ref[...]), 0.0)
        _shift_copies(extd, shd)
        _shift_copies(extu, shu)
        for r in range(tc // ROWS):
            rows = pl.ds(r * ROWS, ROWS)
            acc = jnp.zeros((ROWS, cb), F32)
            for k in range(kk):
                acc = acc + w_ref[k:k + 1, :] * _window(extd, shd, r * ROWS + (kk - 1) - k, ROWS)
            a = a_ref[rows, :]
            sg = _sigmoid(g_ref[rows, :])
            dp_ref[0, rows, :] = (acc * sg).astype(BF16)
            dp_ref[1, rows, :] = (acc * a * sg * (1.0 - sg)).astype(BF16)
            d = d_ref[rows, :]
            bacc[...] += jnp.sum(d.reshape(ROWS // SUBLANE, SUBLANE, cb), axis=0)
            for k in range(kk):
                prod = d * _window(extu, shu, r * ROWS + off + k, ROWS)
                wacc[k] += jnp.sum(prod.reshape(ROWS // SUBLANE, SUBLANE, cb), axis=0)

        @pl.when(i == nt - 1)
        def _():
            for k in range(kk):
                dw_ref[k:k + 1, :] = jnp.sum(wacc[k], axis=0, keepdims=True)
            if kpad > kk:
                dw_ref[kk:kpad, :] = jnp.zeros((kpad - kk, cb), F32)
            db_ref[...] = jnp.sum(bacc[...], axis=0, keepdims=True)

    cur = lambda sec: pl.BlockSpec((None, tc, cb), lambda j, i: (sec, i, j))
    prev = lambda sec: pl.BlockSpec((None, HALO, cb), lambda j, i: (sec, jnp.maximum(i * hb - 1, 0), j))
    return _call(
        body, (du1, du1, p3, p3, p3, p3, conv_w), name="conv_bwd", grid=(w // cb, nt),
        in_specs=[pl.BlockSpec((tc, cb), lambda j, i: (i, j)),
                  pl.BlockSpec((HALO, cb), lambda j, i: (jnp.minimum((i + 1) * hb, t // HALO - 1), j)),
                  cur(0), cur(1), prev(0), prev(1), pl.BlockSpec((kk, cb), lambda j, i: (0, j))],
        out_specs=[pl.BlockSpec((2, tc, cb), lambda j, i: (0, i, j)),
                   pl.BlockSpec((kpad, cb), lambda j, i: (0, j)),
                   pl.BlockSpec((1, cb), lambda j, i: (0, j))],
        out_shape=[jax.ShapeDtypeStruct((n_sec, t, w), BF16), jax.ShapeDtypeStruct((kpad, w), F32),
                   jax.ShapeDtypeStruct((1, w), F32)],
        scratch_shapes=[pltpu.VMEM((tc + HALO, cb), F32), pltpu.VMEM((tc + HALO, cb), F32),
                        pltpu.VMEM((SUBLANE - 1, tc + HALO - SUBLANE, cb), F32),
                        pltpu.VMEM((SUBLANE - 1, tc + HALO - SUBLANE, cb), F32),
                        pltpu.VMEM((kk, SUBLANE, cb), F32), pltpu.VMEM((SUBLANE, cb), F32)], rider=rider)


def _ffn_act_fwd(hh3, fw, fb, tc, cb, rider=None):
    _, t, dff = hh3.shape
    kk = fw.shape[0]
    off = FHALO - (kk - 1)
    hb = tc // FHALO

    def body(g_ref, v_ref, gp_ref, w_ref, b_ref, act_ref, ext):
        i = pl.program_id(1)
        ext[pl.ds(FHALO, tc), :] = g_ref[...]
        ext[pl.ds(0, FHALO), :] = jnp.where(i > 0, gp_ref[...], 0.0)
        for r in range(tc // ROWS):
            rows = pl.ds(r * ROWS, ROWS)
            gc = jnp.broadcast_to(b_ref[...], (ROWS, cb))
            for k in range(kk):
                gc = gc + w_ref[k:k + 1, :] * ext[pl.ds(r * ROWS + off + k, ROWS), :]
            act_ref[rows, :] = (gc * _sigmoid(gc) * v_ref[rows, :]).astype(BF16)

    return _call(
        body, (hh3, hh3, hh3, fw, fb), name="ffn_act_fwd", grid=(dff // cb, t // tc),
        in_specs=[pl.BlockSpec((None, tc, cb), lambda j, i: (0, i, j)),
                  pl.BlockSpec((None, tc, cb), lambda j, i: (1, i, j)),
                  pl.BlockSpec((None, FHALO, cb), lambda j, i: (0, jnp.maximum(i * hb - 1, 0), j)),
                  pl.BlockSpec((kk, cb), lambda j, i: (0, j)),
                  pl.BlockSpec((1, cb), lambda j, i: (0, j))],
        out_specs=pl.BlockSpec((tc, cb), lambda j, i: (i, j)),
        out_shape=jax.ShapeDtypeStruct((t, dff), BF16),
        scratch_shapes=[pltpu.VMEM((tc + FHALO, cb), F32)], rider=rider)


def _ffn_act_bwd(dact, hh3, fw, fb, tc, cb):
    _, t, dff = hh3.shape
    kk = fw.shape[0]
    off = FHALO - (kk - 1)
    hb = tc // FHALO
    nt = t // tc
    te = tc + FHALO

    def body(da_ref, dan_ref, g_ref, gp_ref, gn_ref, v_ref, vn_ref, w_ref, b_ref,
             dhh_ref, dw_ref, db_ref, gext, dext, wacc, bacc):
        i = pl.program_id(1)

        @pl.when(i == 0)
        def _():
            wacc[...] = jnp.zeros_like(wacc)
            bacc[...] = jnp.zeros_like(bacc)

        gext[pl.ds(0, FHALO), :] = jnp.where(i > 0, gp_ref[...], 0.0)
        gext[pl.ds(FHALO, tc), :] = g_ref[...]
        gext[pl.ds(FHALO + tc, FHALO), :] = gn_ref[...]

        def gate_grad(r0, n, da, v):
            gc = jnp.broadcast_to(b_ref[...], (n, cb))
            for k in range(kk):
                gc = gc + w_ref[k:k + 1, :] * gext[pl.ds(r0 + off + k, n), :]
            sg = _sigmoid(gc)
            return gc * sg, da * v * (sg * (1.0 + gc * (1.0 - sg)))

        for r in range(tc // ROWS):
            rows = pl.ds(r * ROWS, ROWS)
            da = da_ref[rows, :]
            silu, dgc = gate_grad(r * ROWS, ROWS, da, v_ref[rows, :])
            dext[rows, :] = dgc
            dhh_ref[1, rows, :] = (da * silu).astype(BF16)
        _, dgc_next = gate_grad(tc, FHALO, dan_ref[...], vn_ref[...])
        dext[pl.ds(tc, FHALO), :] = jnp.where(i < nt - 1, dgc_next, 0.0)
        for r in range(tc // ROWS):
            rows = pl.ds(r * ROWS, ROWS)
            dg = jnp.zeros((ROWS, cb), F32)
            for k in range(kk):
                dg = dg + w_ref[k:k + 1, :] * dext[pl.ds(r * ROWS + (kk - 1) - k, ROWS), :]
            dhh_ref[0, rows, :] = dg.astype(BF16)
            dgc = dext[rows, :]
            bacc[...] += jnp.sum(dgc.reshape(ROWS // SUBLANE, SUBLANE, cb), axis=0)
            for k in range(kk):
                prod = dgc * gext[pl.ds(r * ROWS + off + k, ROWS), :]
                wacc[k] += jnp.sum(prod.reshape(ROWS // SUBLANE, SUBLANE, cb), axis=0)

        @pl.when(i == nt - 1)
        def _():
            for k in range(kk):
                dw_ref[k:k + 1, :] = jnp.sum(wacc[k], axis=0, keepdims=True)
            dw_ref[kk:SUBLANE, :] = jnp.zeros((SUBLANE - kk, cb), F32)
            db_ref[...] = jnp.sum(bacc[...], axis=0, keepdims=True)

    nxt = lambda i: jnp.minimum((i + 1) * hb, t // FHALO - 1)
    return pl.pallas_call(
        body, name="ffn_act_bwd", grid=(dff // cb, nt),
        in_specs=[pl.BlockSpec((tc, cb), lambda j, i: (i, j)),
                  pl.BlockSpec((FHALO, cb), lambda j, i: (nxt(i), j)),
                  pl.BlockSpec((None, tc, cb), lambda j, i: (0, i, j)),
                  pl.BlockSpec((None, FHALO, cb), lambda j, i: (0, jnp.maximum(i * hb - 1, 0), j)),
                  pl.BlockSpec((None, FHALO, cb), lambda j, i: (0, nxt(i), j)),
                  pl.BlockSpec((None, tc, cb), lambda j, i: (1, i, j)),
                  pl.BlockSpec((None, FHALO, cb), lambda j, i: (1, nxt(i), j)),
                  pl.BlockSpec((kk, cb), lambda j, i: (0, j)),
                  pl.BlockSpec((1, cb), lambda j, i: (0, j))],
        out_specs=[pl.BlockSpec((2, tc, cb), lambda j, i: (0, i, j)),
                   pl.BlockSpec((SUBLANE, cb), lambda j, i: (0, j)),
                   pl.BlockSpec((1, cb), lambda j, i: (0, j))],
        out_shape=[jax.ShapeDtypeStruct((2, t, dff), BF16), jax.ShapeDtypeStruct((SUBLANE, dff), F32),
                   jax.ShapeDtypeStruct((1, dff), F32)],
        scratch_shapes=[pltpu.VMEM((tc + 2 * FHALO, cb), F32), pltpu.VMEM((te, cb), F32),
                        pltpu.VMEM((kk, SUBLANE, cb), F32), pltpu.VMEM((SUBLANE, cb), F32)],
        compiler_params=_params("parallel", "arbitrary"),
    )(dact, dact, hh3, hh3, hh3, hh3, hh3, fw, fb)


def _chunk_consts():
    r = lax.broadcasted_iota(jnp.int32, (CHUNK, CHUNK), 0)
    c = lax.broadcasted_iota(jnp.int32, (CHUNK, CHUNK), 1)
    blk = (r // SUB) * SUB
    tri = (c <= r).astype(F32)
    start = (c < blk).astype(F32)
    end = (c < blk + SUB).astype(F32)
    return jnp.concatenate([tri, start, end, jnp.ones((SUBLANE, CHUNK), F32)], axis=0)


def _roll8(x, d):
    return pltpu.roll(x.reshape(CHUNK // SUB, SUB, LANE), d % SUB, 1).reshape(CHUNK, LANE)


def _gate_terms(q, fpre, lb):
    sf = _sigmoid(fpre)
    fg = lb + (1.0 - lb) * sf
    sq = _sigmoid(q)
    return sf, fg, 1.0 - fg, sq, q * sq


def _decays(g, consts):
    cs = _dot3(consts, g)
    b = cs[0:CHUNK]
    rs = cs[CHUNK:2 * CHUNK]
    re = cs[2 * CHUNK:3 * CHUNK]
    tot = cs[3 * CHUNK:3 * CHUNK + 1]
    return b, rs, re, tot


def _lower_bound(lb_ref):
    l0, l1 = lb_ref[0:1, :], lb_ref[1:2, :]
    mx = jnp.maximum(l0, l1)
    e0, e1 = jnp.exp(l0 - mx), jnp.exp(l1 - mx)
    return e0 / (e0 + e1)


def _scaled_keys(kt, rs, re, rowblk, i):
    scale = jnp.where(rowblk < i, jnp.exp(jnp.minimum(rs[SUB * i:SUB * i + 1, :] - re, 0.0)), 0.0)
    return kt * scale, scale


def _hgrn_fwd(p3, lb_logits, hg, cat, tb, hpb, rider=None):
    _, t, w = p3.shape
    nh = w // LANE
    nc = tb // CHUNK
    assert nh % hpb == 0

    def body(q_ref, f_ref, v_ref, og_ref, lb_ref, hg_ref, cat_in, cat_ref, o_ref, st_ref, state):
        del cat_in
        consts = _chunk_consts()
        lb_all = _lower_bound(lb_ref)
        rowblk = lax.broadcasted_iota(jnp.int32, (CHUNK, 1), 0) // SUB
        rowpos = lax.broadcasted_iota(jnp.int32, (CHUNK, 1), 0) % SUB

        @pl.when(pl.program_id(1) == 0)
        def _():
            state[...] = jnp.zeros_like(state)

        def chunk(c, carry):
            rows = pl.ds(pl.multiple_of(c * CHUNK, CHUNK), CHUNK)
            heads = range(hpb)
            sls = [slice(j * LANE, (j + 1) * LANE) for j in heads]
            v = [v_ref[rows, s] for s in sls]
            vb = [x.astype(BF16) for x in v]
            gates = [_gate_terms(q_ref[rows, s], f_ref[rows, s], lb_all[:, s]) for s in sls]
            fg = [g[1] for g in gates]
            kk = [g[2] for g in gates]
            qh = [g[4] for g in gates]
            dec = [_decays(jnp.log(x), consts) for x in fg]
            b = [x[0] for x in dec]
            rs = [x[1] for x in dec]
            re = [x[2] for x in dec]
            tot = [x[3] for x in dec]
            qt = [qh[j] * jnp.exp(b[j] - rs[j]) for j in heads]
            kt = [kk[j] * jnp.exp(re[j] - b[j]) for j in heads]
            st = [state[j] for j in heads]
            for j in heads:
                st_ref[j, c] = st[j]
            a = [jnp.zeros((CHUNK, CHUNK), F32) for _ in heads]
            for i in range(1, CHUNK // SUB):
                for j in heads:
                    ki, _ = _scaled_keys(kt[j], rs[j], re[j], rowblk, i)
                    a[j] = a[j] + _dot_nt(jnp.where(rowblk == i, qt[j], 0.0).astype(BF16), ki.astype(BF16))
            o = [_dot(a[j].astype(BF16), vb[j]) for j in heads]
            o = [o[j] + _dot_nt((qh[j] * jnp.exp(b[j])).astype(BF16), st[j].astype(BF16)) for j in heads]
            for j in heads:
                k_up = kk[j] * jnp.exp(tot[j] - b[j])
                state[j] = st[j] * jnp.exp(tot[j]) + _dot_tn(vb[j], k_up.astype(BF16))
            for j in heads:
                e, rf = None, fg[j]
                for d in range(SUB):
                    if d == 0:
                        vs, term = v[j], qh[j] * kk[j]
                    else:
                        e = rf if e is None else e * rf
                        rf = _roll8(fg[j], d)
                        vs = _roll8(v[j], d)
                        term = jnp.where(rowpos >= d, qh[j] * (1.0 - rf) * e, 0.0)
                    o[j] = o[j] + jnp.sum(term, axis=-1, keepdims=True) * vs
            for j in heads:
                og = og_ref[rows, sls[j]]
                o_ref[rows, sls[j]] = o[j]
                r = lax.rsqrt(jnp.mean(o[j] * o[j], axis=-1, keepdims=True) + RMS_EPS)
                cat_ref[rows, sls[j]] = (o[j] * r * hg_ref[:, sls[j]] * (og * _sigmoid(og))).astype(BF16)
            return carry

        lax.fori_loop(0, nc, chunk, 0)

    bw = hpb * LANE
    sec = lambda s: pl.BlockSpec((None, tb, bw), lambda h, i: (s, i, h))
    return _call(
        body, (p3, p3, p3, p3, lb_logits, hg, cat), name="hgrn_fwd", grid=(nh // hpb, t // tb),
        in_specs=[sec(2), sec(3), sec(4), sec(5),
                  pl.BlockSpec((2, bw), lambda h, i: (0, h)),
                  pl.BlockSpec((1, bw), lambda h, i: (0, h)), HBM],
        out_specs=[pl.BlockSpec((tb, bw), lambda h, i: (i, nh // hpb + h)),
                   pl.BlockSpec((tb, bw), lambda h, i: (i, h)),
                   pl.BlockSpec((hpb, nc, LANE, LANE), lambda h, i: (h, i, 0, 0))],
        out_shape=[jax.ShapeDtypeStruct(cat.shape, BF16), jax.ShapeDtypeStruct((t, w), F32),
                   jax.ShapeDtypeStruct((nh, t // CHUNK, LANE, LANE), F32)],
        scratch_shapes=[pltpu.VMEM((hpb, LANE, LANE), F32)], aliases={6: 0}, rider=rider)


def _hgrn_bwd(p3, lb_logits, hg, o_pre, states, dcat, dp3, tb, hpb, rider=None):
    n_sec, t, w = p3.shape
    nh = w // LANE
    assert nh % hpb == 0
    nc = tb // CHUNK
    nb = t // tb
    bw = hpb * LANE
    n_steps = (nh // hpb) * nb

    def body(q_ref, f_ref, v_ref, og_ref, lb_ref, hg_ref, o_ref, st_ref, dc_ref, dp_in,
             dp_ref, dlb_ref, dhg_ref, dstate, stash, lbacc, hgacc, osem):
        del dp_in
        h, i = pl.program_id(0), pl.program_id(1)
        step = h * nb + i
        slot = step % 2

        def out_copy(s, row_blk, lane_blk):
            dst = dp_ref.at[pl.ds(2, 4), pl.ds(row_blk * tb, tb), pl.ds(lane_blk * bw, bw)]
            return pltpu.make_async_copy(stash.at[s], dst, osem.at[s])

        @pl.when(step >= 2)
        def _():
            out_copy(slot, 0, 0).wait()

        def compute():
            consts = _chunk_consts()
            rr = lax.broadcasted_iota(jnp.int32, (CHUNK, CHUNK), 0)
            cc = lax.broadcasted_iota(jnp.int32, (CHUNK, CHUNK), 1)
            upper = (cc >= rr).astype(F32)
            lb_all = _lower_bound(lb_ref)
            rowblk = lax.broadcasted_iota(jnp.int32, (CHUNK, 1), 0) // SUB
            rowpos = lax.broadcasted_iota(jnp.int32, (CHUNK, 1), 0) % SUB

            @pl.when(i == 0)
            def _():
                dstate[...] = jnp.zeros_like(dstate)
                lbacc[...] = jnp.zeros_like(lbacc)
                hgacc[...] = jnp.zeros_like(hgacc)

            def head(j, c, rows):
                sl = slice(j * LANE, (j + 1) * LANE)
                lb = lb_all[:, sl]
                hgv = hg_ref[:, sl]
                q = q_ref[rows, sl]
                v = v_ref[rows, sl]
                og = og_ref[rows, sl]
                o = o_ref[rows, sl]
                dcg = dc_ref[rows, sl]
                sf, fg, kk, sq, qh = _gate_terms(q, f_ref[rows, sl], lb)
                b, rs, re, tot = _decays(jnp.log(fg), consts)
                eq = jnp.exp(b - rs)
                ek = jnp.exp(re - b)
                qt = qh * eq
                kt = kk * ek
                e_in = jnp.exp(b)
                e_up = jnp.exp(tot - b)
                e_tot = jnp.exp(tot)
                q_in = (qh * e_in).astype(BF16)
                k_up = (kk * e_up).astype(BF16)
                vb = v.astype(BF16)
                st = st_ref[j, c]
                dst = dstate[j]
                dstb = dst.astype(BF16)
                yield

                sg = _sigmoid(og)
                r = lax.rsqrt(jnp.mean(o * o, axis=-1, keepdims=True) + RMS_EPS)
                ohat = o * r
                d_og = dcg * ohat * hgv * (sg * (1.0 + og * (1.0 - sg)))
                d_on = dcg * (og * sg)
                hgacc[:, sl] += jnp.sum((d_on * ohat).reshape(CHUNK // SUBLANE, SUBLANE, LANE), axis=0)
                d_oh = d_on * hgv
                do = r * (d_oh - ohat * jnp.mean(d_oh * ohat, axis=-1, keepdims=True))
                dob = do.astype(BF16)

                da = _dot_nt(dob, vb)
                yield
                a_off = jnp.zeros((CHUNK, CHUNK), F32)
                dqt = jnp.zeros((CHUNK, LANE), F32)
                dkt = jnp.zeros((CHUNK, LANE), F32)
                for blk in range(1, CHUNK // SUB):
                    ki, scale = _scaled_keys(kt, rs, re, rowblk, blk)
                    kib = ki.astype(BF16)
                    qib = jnp.where(rowblk == blk, qt, 0.0).astype(BF16)
                    dab = jnp.where(rowblk == blk, da, 0.0).astype(BF16)
                    a_off = a_off + _dot_nt(qib, kib)
                    dqt = dqt + _dot(dab, kib)
                    dkt = dkt + _dot_tn(dab, qib) * scale
                    yield
                dqh = dqt * eq
                dk = dkt * ek
                dv = _dot_tn(a_off.astype(BF16), dob)

                dqh = dqh + _dot(dob, st.astype(BF16)) * e_in
                dk = dk + _dot(vb, dstb) * e_up
                dv = dv + _dot_nt(k_up, dstb)
                st_end = st * e_tot + _dot_tn(vb, k_up)
                carry_g = jnp.sum(st_end * dst, axis=0, keepdims=True)
                dstate[j] = dst * e_tot + _dot_tn(dob, q_in)
                yield

                e, rf = None, fg
                for d in range(SUB):
                    if d == 0:
                        a_d = jnp.sum(qh * kk, axis=-1, keepdims=True)
                        da_d = jnp.sum(do * v, axis=-1, keepdims=True)
                        dqh = dqh + da_d * kk
                        dk = dk + da_d * qh
                        dv = dv + a_d * do
                        continue
                    e = rf if e is None else e * rf
                    rf = _roll8(fg, d)
                    em = jnp.where(rowpos >= d, e, 0.0)
                    ks, vs = 1.0 - rf, _roll8(v, d)
                    a_d = jnp.sum(qh * ks * em, axis=-1, keepdims=True)
                    da_d = jnp.sum(do * vs, axis=-1, keepdims=True) * em
                    dqh = dqh + da_d * ks
                    dk = dk + _roll8(da_d * qh, -d)
                    dv = dv + _roll8(a_d * do, -d)
                yield

                dg = _dot3(upper, qh * dqh - kk * dk) + carry_g
                dfg = dg / fg - dk
                lbacc[:, sl] += jnp.sum((dfg * (1.0 - sf)).reshape(CHUNK // SUBLANE, SUBLANE, LANE), axis=0)
                stash[slot, 0, rows, sl] = (dqh * (sq * (1.0 + q * (1.0 - sq)))).astype(BF16)
                stash[slot, 1, rows, sl] = (dfg * (1.0 - lb) * sf * (1.0 - sf)).astype(BF16)
                stash[slot, 2, rows, sl] = dv.astype(BF16)
                stash[slot, 3, rows, sl] = d_og.astype(BF16)

            def chunk(cr, carry):
                c = nc - 1 - cr
                rows = pl.ds(pl.multiple_of(c * CHUNK, CHUNK), CHUNK)
                running = [head(j, c, rows) for j in range(hpb)]
                while running:
                    running = [g for g in running if next(g, StopIteration) is not StopIteration]
                return carry

            lax.fori_loop(0, nc, chunk, 0)

            @pl.when(i == nb - 1)
            def _():
                dlb_ref[...] = jnp.sum(lbacc[...], axis=0, keepdims=True)
                dhg_ref[...] = jnp.sum(hgacc[...], axis=0, keepdims=True)

        compute()
        out_copy(slot, nb - 1 - i, h).start()

        @pl.when(step == n_steps - 1)
        def _():
            out_copy(slot, 0, 0).wait()
            if n_steps >= 2:
                out_copy(1 - slot, 0, 0).wait()

    rev = lambda i: nb - 1 - i
    sec = lambda s: pl.BlockSpec((None, tb, bw), lambda h, i: (s, rev(i), h))
    return _call(
        body, (p3, p3, p3, p3, lb_logits, hg, o_pre, states, dcat, dp3), name="hgrn_bwd", grid=(nh // hpb, nb),
        in_specs=[sec(2), sec(3), sec(4), sec(5),
                  pl.BlockSpec((2, bw), lambda h, i: (0, h)),
                  pl.BlockSpec((1, bw), lambda h, i: (0, h)),
                  pl.BlockSpec((tb, bw), lambda h, i: (rev(i), h)),
                  pl.BlockSpec((hpb, nc, LANE, LANE), lambda h, i: (h, rev(i), 0, 0)),
                  pl.BlockSpec((tb, bw), lambda h, i: (rev(i), nh // hpb + h)), HBM],
        out_specs=[HBM,
                   pl.BlockSpec((1, bw), lambda h, i: (0, h)),
                   pl.BlockSpec((1, bw), lambda h, i: (0, h))],
        out_shape=[jax.ShapeDtypeStruct((n_sec, t, w), BF16), jax.ShapeDtypeStruct((1, w), F32),
                   jax.ShapeDtypeStruct((1, w), F32)],
        scratch_shapes=[pltpu.VMEM((hpb, LANE, LANE), F32), pltpu.VMEM((2, 4, tb, bw), BF16),
                        pltpu.VMEM((SUBLANE, bw), F32), pltpu.VMEM((SUBLANE, bw), F32),
                        pltpu.SemaphoreType.DMA((2,))],
        aliases={9: 0}, rider=rider)


def _place():
    x, y, c = lax.axis_index("x"), lax.axis_index("y"), lax.axis_index("c")
    chips = [(1 - x, y), (x, 1 - y), (1 - x, 1 - y)]
    return x, y, c, chips


def _rows(buf, px, py, pc, part=None):
    half = buf.shape[1] // 2
    if part is None:
        return buf.at[2 * px + py, pl.ds(pc * half, half)]
    lo, hi, n = part
    piece = half // n
    return buf.at[2 * px + py, pl.ds(pc * half + lo * piece, (hi - lo) * piece)]


def _rcopy(src, dst, send, recv, idx, to):
    return pltpu.make_async_remote_copy(src_ref=src, dst_ref=dst, send_sem=send.at[idx], recv_sem=recv.at[idx],
                                        device_id=to, device_id_type=MESH)


def _same(bufs):
    return [jax.ShapeDtypeStruct(b.shape, b.dtype) for b in bufs]


def _ride_gather_ici(bufs, parts=None):
    n = len(bufs)
    parts = parts or [None] * n

    def start(rin, rout, send, recv):
        x, y, c, chips = _place()
        for k in range(n):
            mine = _rows(rout[k], x, y, c, parts[k])
            for j, chip in enumerate(chips):
                _rcopy(mine, mine, send, recv, 3 * k + j, (*chip, c)).start()

    def finish(rin, rout, send, recv):
        x, y, c, chips = _place()
        for k in range(n):
            for j, chip in enumerate(chips):
                theirs = _rows(rout[k], *chip, c, parts[k])
                _rcopy(theirs, theirs, send, recv, 3 * k + j, (x, y, c)).wait_recv()
        for k in range(n):
            mine = _rows(rout[k], x, y, c, parts[k])
            for j in range(3):
                _rcopy(mine, mine, send, recv, 3 * k + j, (x, y, c)).wait_send()

    return _Rider(bufs, _same(bufs), {k: k for k in range(n)}, 3 * n, start, finish)


class _SemView:
    def __init__(self, ref, base):
        self.ref, self.base = ref, base

    @property
    def at(self):
        return self

    def __getitem__(self, idx):
        return self.ref.at[idx + self.base]


def _ride_both(a, b):
    nai, nao = len(a.ins), len(a.outs)

    def start(rin, rout, send, recv):
        a.start(rin[:nai], rout[:nao], send, recv)
        b.start(rin[nai:], rout[nao:], _SemView(send, a.n_sems), _SemView(recv, a.n_sems))

    def finish(rin, rout, send, recv):
        a.finish(rin[:nai], rout[:nao], send, recv)
        b.finish(rin[nai:], rout[nao:], _SemView(send, a.n_sems), _SemView(recv, a.n_sems))

    aliases = dict(a.aliases)
    aliases.update({nai + ri: nao + ro for ri, ro in b.aliases.items()})
    return _Rider(a.ins + b.ins, a.outs + b.outs, aliases, a.n_sems + b.n_sems, start, finish)


def _ride_gather_d2d(bufs):
    n = len(bufs)

    def start(rin, rout, send, recv):
        x, y, c, chips = _place()
        for k in range(n):
            for j, chip in enumerate(chips):
                got = _rows(rout[k], *chip, c)
                _rcopy(got, got, send, recv, 3 * k + j, (x, y, 1 - c)).start()

    def finish(rin, rout, send, recv):
        x, y, c, chips = _place()
        for k in range(n):
            for j, chip in enumerate(chips):
                theirs = _rows(rout[k], *chip, 1 - c)
                _rcopy(theirs, theirs, send, recv, 3 * k + j, (x, y, c)).wait_recv()
        for k in range(n):
            for j, chip in enumerate(chips):
                got = _rows(rout[k], *chip, c)
                _rcopy(got, got, send, recv, 3 * k + j, (x, y, c)).wait_send()

    return _Rider(bufs, _same(bufs), {k: k for k in range(n)}, 3 * n, start, finish)


def _ride_swap(grads, shards=None, into=None):
    n = len(grads)
    shards = shards or [(0, g.shape[0]) for g in grads]

    def copy(k, rin, rout, send, recv):
        x, y, c, _ = _place()
        half = rin[k].shape[1] // 2
        lo, hi = shards[k]
        return _rcopy(rin[k].at[pl.ds(lo, hi - lo), pl.ds((1 - c) * half, half)], rout[k].at[pl.ds(lo, hi - lo)],
                      send, recv, k, (x, y, 1 - c))

    def start(rin, rout, send, recv):
        for k in range(n):
            copy(k, rin, rout, send, recv).start()

    def finish(rin, rout, send, recv):
        for k in range(n):
            copy(k, rin, rout, send, recv).wait()

    if into is None:
        outs = [jax.ShapeDtypeStruct((g.shape[0], g.shape[1] // 2, g.shape[2]), g.dtype) for g in grads]
        return _Rider(grads, outs, {}, n, start, finish)
    return _Rider(list(grads) + list(into), _same(into), {n + k: k for k in range(n)}, n, start, finish)


def _ride_send_partials(parts, pieces=None, into=None):
    n = len(parts)
    pieces = pieces or [None] * n

    def cut(ref, k):
        if pieces[k] is None:
            return ref
        lo, hi, m = pieces[k]
        q = ref.shape[0] // m
        return ref.at[pl.ds(lo * q, (hi - lo) * q)]

    def copies(rin, rout, send, recv):
        x, y, c, chips = _place()
        return [_rcopy(cut(rin[k].at[2 * px + py], k), cut(rout[k].at[j], k), send, recv, 3 * k + j, (px, py, c))
                for k in range(n) for j, (px, py) in enumerate(chips)]

    def start(rin, rout, send, recv):
        for cp in copies(rin, rout, send, recv):
            cp.start()

    def finish(rin, rout, send, recv):
        for cp in copies(rin, rout, send, recv):
            cp.wait()

    if into is None:
        outs = [jax.ShapeDtypeStruct((3,) + p.shape[1:], p.dtype) for p in parts]
        return _Rider(parts, outs, {}, 3 * n, start, finish)
    return _Rider(list(parts) + list(into), _same(into), {n + k: k for k in range(n)}, 3 * n, start, finish)


def _ride_join(bufs):
    n = len(bufs)

    def half_of(buf, pc):
        half = buf.shape[0] // 2
        return buf.at[pl.ds(pc * half, half)]

    def start(rin, rout, send, recv):
        x, y, c, _ = _place()
        for k in range(n):
            mine = half_of(rout[k], c)
            _rcopy(mine, mine, send, recv, k, (x, y, 1 - c)).start()

    def finish(rin, rout, send, recv):
        x, y, c, _ = _place()
        for k in range(n):
            mine, theirs = half_of(rout[k], c), half_of(rout[k], 1 - c)
            _rcopy(mine, mine, send, recv, k, (x, y, c)).wait_send()
            _rcopy(theirs, theirs, send, recv, k, (x, y, c)).wait_recv()

    return _Rider(bufs, _same(bufs), {k: k for k in range(n)}, n, start, finish)


def _run(name, rider):
    def body(*refs):
        nri, nro = len(rider.ins), len(rider.outs)
        rin, rout = refs[:nri], refs[nri:nri + nro]
        send, recv = refs[nri + nro:]
        rider.start(rin, rout, send, recv)
        rider.finish(rin, rout, send, recv)

    return pl.pallas_call(
        body, name=name, in_specs=[HBM] * len(rider.ins), out_specs=[HBM] * len(rider.outs), out_shape=rider.outs,
        scratch_shapes=[pltpu.SemaphoreType.DMA((rider.n_sems,)), pltpu.SemaphoreType.DMA((rider.n_sems,))],
        input_output_aliases=rider.aliases,
    )(*rider.ins)


def _add_halves(name, g, other, c_idx):
    s, r, cols = g.shape
    half = r // 2
    tr = _div_tile(half, 16, 512)
    nb = half // tr

    def body(c_ref, g_ref, o_ref, q_ref):
        del c_ref
        q_ref[...] = (g_ref[...] + o_ref[...]).astype(BF16)

    return pl.pallas_call(
        body, name=name,
        grid_spec=pltpu.PrefetchScalarGridSpec(
            num_scalar_prefetch=1, grid=(s, nb),
            in_specs=[pl.BlockSpec((None, tr, cols), lambda k, i, c: (k, c[0] * nb + i, 0)),
                      pl.BlockSpec((None, tr, cols), lambda k, i, c: (k, i, 0))],
            out_specs=pl.BlockSpec((None, tr, cols), lambda k, i, c: (k, i, 0))),
        out_shape=jax.ShapeDtypeStruct((s, half, cols), BF16),
        compiler_params=_params("parallel", "parallel"),
    )(c_idx, g, other)


def _sum_partials(name, part, arrived, place_idx):
    _, half, cols = part.shape
    tr = _div_tile(half, 16, 512)
    nb = half // tr

    def body(s_ref, p_ref, a_ref, o_ref):
        del s_ref
        o_ref[...] = ((p_ref[...].astype(F32) + a_ref[0].astype(F32)) + a_ref[1].astype(F32)) + a_ref[2].astype(F32)

    return pl.pallas_call(
        body, name=name,
        grid_spec=pltpu.PrefetchScalarGridSpec(
            num_scalar_prefetch=1, grid=(nb,),
            in_specs=[pl.BlockSpec((None, tr, cols), lambda i, s: (s[0], i, 0)),
                      pl.BlockSpec((3, tr, cols), lambda i, s: (0, i, 0))],
            out_specs=pl.BlockSpec((tr, cols), lambda i, s: (s[1] * nb + i, 0))),
        out_shape=jax.ShapeDtypeStruct((2 * half, cols), F32),
        compiler_params=_params("parallel"),
    )(place_idx, part, arrived)


def _pack_small(name, wide_rows, ffn_rows, w, dff, n_wide, n_ffn):
    n_in = len(wide_rows) + len(ffn_rows)

    def body(*refs):
        ins, outs = refs[:n_in], refs[n_in:]
        p1 = outs[0]
        p1[...] = jnp.zeros_like(p1)
        row = 0
        for ref, (_, r, m) in zip(ins, wide_rows):
            if m == 1 and r % SUBLANE == 0 and row % SUBLANE == 0:
                p1[row:row + r, :] = ref[...]
                row += r
                continue
            for rr in range(r):
                for mm in range(m):
                    p1[row:row + 1, :] = ref[rr:rr + 1, mm * w:(mm + 1) * w]
                    row += 1
        if ffn_rows:
            p2 = outs[1]
            p2[...] = jnp.zeros_like(p2)
            row = 0
            for ref, arr in zip(ins[len(wide_rows):], ffn_rows):
                r = arr.shape[0]
                p2[row:row + r, :] = ref[...]
                row += r

    shapes = [jax.ShapeDtypeStruct((n_wide, w), F32)] + ([jax.ShapeDtypeStruct((n_ffn, dff), F32)] if ffn_rows else [])
    return pl.pallas_call(
        body, name=name, in_specs=[VMEM_FULL] * n_in, out_specs=[VMEM_FULL] * len(shapes), out_shape=shapes,
        compiler_params=pltpu.CompilerParams(vmem_limit_bytes=VMEM_LIMIT),
    )(*[a for a, _, _ in wide_rows], *ffn_rows)


def _ride_exchange8(packs):
    n = len(packs)

    def copies(rin, rout, send, recv):
        x, y, c, _ = _place()
        me = 4 * x + 2 * y + c
        out = []
        for a in range(n):
            for mask in range(1, 8):
                peer = (x ^ (mask >> 2), y ^ ((mask >> 1) & 1), c ^ (mask & 1))
                out.append(_rcopy(rin[a], rout[a].at[me], send, recv, 8 * a + mask, peer))
        own = [pltpu.make_async_copy(rin[a], rout[a].at[me], send.at[8 * a]) for a in range(n)]
        return out, own

    def start(rin, rout, send, recv):
        remote, own = copies(rin, rout, send, recv)
        for cp in remote + own:
            cp.start()

    def finish(rin, rout, send, recv):
        remote, own = copies(rin, rout, send, recv)
        for cp in remote + own:
            cp.wait()

    outs = [jax.ShapeDtypeStruct((8,) + p.shape, p.dtype) for p in packs]
    return _Rider(packs, outs, {}, 8 * n, start, finish)


def _sum_small(name, slots):
    def body(*refs):
        n = len(refs) // 2
        for r_ref, s_ref in zip(refs[:n], refs[n:]):
            tot = r_ref[0]
            for d in range(1, 8):
                tot = tot + r_ref[d]
            s_ref[...] = tot

    return pl.pallas_call(
        body, name=name, in_specs=[VMEM_FULL] * len(slots), out_specs=[VMEM_FULL] * len(slots),
        out_shape=[jax.ShapeDtypeStruct(s.shape[1:], s.dtype) for s in slots],
        compiler_params=pltpu.CompilerParams(vmem_limit_bytes=VMEM_LIMIT),
    )(*slots)


def _adamw(w, g, m, v):
    m2 = ADAM_B1 * m + (1.0 - ADAM_B1) * g
    v2 = ADAM_B2 * v + (1.0 - ADAM_B2) * (g * g)
    m_hat = m2 / (1.0 - ADAM_B1 ** ADAM_STEP)
    v_hat = v2 / (1.0 - ADAM_B2 ** ADAM_STEP)
    delta = -ADAM_LR * (m_hat / (jnp.sqrt(v_hat) + ADAM_EPS) + ADAM_WD * w)
    return delta, m2, v2


def _adam_big(name, w, g, m, v):
    r, c = w.shape
    tr = 128 if r % 128 == 0 else r

    def body(w_ref, g_ref, m_ref, v_ref, go_ref, d_ref, m2_ref, v2_ref):
        g = g_ref[...]
        go_ref[...] = g
        d_ref[...], m2_ref[...], v2_ref[...] = _adamw(w_ref[...], g, m_ref[...], v_ref[...])

    blk = pl.BlockSpec((tr, c), lambda i: (i, 0))
    return _call(
        body, (w, g, m, v), name=name, grid=(r // tr,), in_specs=[blk] * 4, out_specs=[blk] * 4,
        out_shape=[jax.ShapeDtypeStruct((r, c), F32)] * 4)


def _adam_small(s1, s2, s3, cw_g, fw_g, lb_logits, triples, layout, w):
    n = len(triples)

    def body(*refs):
        s1_ref, s2_ref, s3_ref, cw_ref, fw_ref, lbl_ref = refs[:6]
        prm = refs[6:6 + 3 * n]
        outs = refs[6 + 3 * n:]
        for p, lay in enumerate(layout):
            w_ref, m_ref, v_ref = prm[3 * p:3 * p + 3]
            g_ref, d_ref, m2_ref, v2_ref = outs[4 * p:4 * p + 4]
            if lay[0] in ("wide", "late"):
                _, row, r, pieces = lay
                src = s1_ref if lay[0] == "wide" else s3_ref
                for rr in range(r):
                    for mm in range(pieces):
                        g_ref[rr:rr + 1, mm * w:(mm + 1) * w] = src[row:row + 1, :]
                        row += 1
            elif lay[0] == "ffn":
                _, row, r = lay
                g_ref[...] = s2_ref[row:row + r, :]
            elif lay[0] == "cw":
                g_ref[...] = cw_ref[0:g_ref.shape[0], :]
            elif lay[0] == "fw":
                g_ref[...] = fw_ref[0:g_ref.shape[0], :]
            else:
                s0 = _lower_bound(lbl_ref)
                d0 = s1_ref[lay[1]:lay[1] + 1, :] * s0 * (1.0 - s0)
                g_ref[0:1, :] = d0
                g_ref[1:2, :] = -d0
            d_ref[...], m2_ref[...], v2_ref[...] = _adamw(w_ref[...], g_ref[...], m_ref[...], v_ref[...])

    flat = [a for tr in triples for a in tr]
    shapes = []
    for tr in triples:
        shapes.extend([jax.ShapeDtypeStruct(tr[0].shape, F32)] * 4)
    return pl.pallas_call(
        body, name="adam_small", in_specs=[VMEM_FULL] * (6 + 3 * n), out_specs=[VMEM_FULL] * (4 * n),
        out_shape=shapes, compiler_params=pltpu.CompilerParams(vmem_limit_bytes=VMEM_LIMIT),
    )(s1, s2, s3, cw_g, fw_g, lb_logits, *flat)


def _row_tile(t):
    return 512 if t % 512 == 0 and t >= 2048 else 128


def kernel(x, emb_ln_g, emb_ln_b, w_in, conv_w, conv_b, conv_norm_g, conv_norm_b, lb_logits, hgrn_norm_g, w_out, ln1_g, ln1_b, w_ffn_up, ffn_conv_w, ffn_conv_b, w_ffn_down, ln2_g, ln2_b, loss_target, m_emb_ln_g, m_emb_ln_b, m_w_in, m_conv_w, m_conv_b, m_conv_norm_g, m_conv_norm_b, m_lb_logits, m_hgrn_norm_g, m_w_out, m_ln1_g, m_ln1_b, m_w_ffn_up, m_ffn_conv_w, m_ffn_conv_b, m_w_ffn_down, m_ln2_g, m_ln2_b, v_emb_ln_g, v_emb_ln_b, v_w_in, v_conv_w, v_conv_b, v_conv_norm_g, v_conv_norm_b, v_lb_logits, v_hgrn_norm_g, v_w_out, v_ln1_g, v_ln1_b, v_w_ffn_up, v_ffn_conv_w, v_ffn_conv_b, v_w_ffn_down, v_ln2_g, v_ln2_b):
    depth = w_in.shape[0]
    assert depth == 1 and x.shape[0] == 1
    alpha = (2.0 * depth) ** 0.25
    t, d = x.shape[1], x.shape[2]
    w = d // 2
    dff = ffn_conv_b.shape[1]
    kc = conv_w.shape[1]
    assert w % (2 * LANE) == 0 and dff % (4 * LANE) == 0 and t % 128 == 0
    tm = _row_tile(t)
    tm2 = tm // 2
    tmm = 1024 if t % 1024 == 0 and t >= 2048 else tm
    cb = 2 * LANE
    cbf = 4 * LANE
    tb = tm
    nh = w // LANE
    hpb = 4 if nh % 4 == 0 else 2

    xi = lax.axis_index("x")
    yi = lax.axis_index("y")
    ci = lax.axis_index("c")
    chip = 2 * xi + yi
    c_idx = jnp.reshape(ci, (1,)).astype(jnp.int32)
    chip_idx = jnp.reshape(chip, (1,)).astype(jnp.int32)
    place_idx = jnp.stack([chip, ci]).astype(jnp.int32)

    x2 = x[0]
    tgt = loss_target[0]
    g0, b0 = emb_ln_g.reshape(1, d), emb_ln_b.reshape(1, d)
    w_in2, w_out2, w_up2, w_dn2 = w_in[0], w_out[0], w_ffn_up[0], w_ffn_down[0]
    cw2, fw2 = conv_w[0], ffn_conv_w[0]

    b_in = _place_shard(w_in2, "place_w_in", chip_idx, BF16)
    b_out = _place_shard(w_out2, "place_w_out", chip_idx, BF16)
    b_up = _place_shard(w_up2, "place_w_up", chip_idx, BF16)
    b_dn = _place_shard(w_dn2, "place_w_down", chip_idx, BF16)
    b_cw = _place_shard(_pad_rows(cw2), "place_conv_w", chip_idx, F32)
    b_fw = _place_shard(_pad_rows(fw2), "place_ffn_conv_w", chip_idx, F32)
    h0b, (b_in,) = _ln0(x2, g0, b0, tm, rider=_ride_gather_ici([b_in], [(0, 1, 8)]))
    first = _run("gather_first_ici", _ride_gather_ici([b_in, b_cw, b_fw], [(1, 8, 8), None, None]))
    w_in3, cw_full3, fw_full3 = _run("gather_first_d2d", _ride_gather_d2d(first))
    cw_full = _unshard_cols(cw_full3)[:kc]
    fw_full = _unshard_cols(fw_full3)[:fw2.shape[0]]

    p3, (b_out, b_up) = _proj("in_proj", h0b, w_in3, 6, 2 * tmm if t % (2 * tmm) == 0 else tmm, w // 2,
                              rider=_ride_gather_ici([b_out, b_up], [None, (0, 1, 4)]))
    (cat, u1), (w_out3, b_up) = _conv_fwd(
        p3, cw_full, conv_b, conv_norm_g, conv_norm_b, tm2, cb,
        rider=_ride_both(_ride_gather_d2d([b_out]), _ride_gather_ici([b_up], [(1, 2, 4)])))
    w_out_full = w_out3.reshape(d, d)
    (cat, o_pre, states), got = _hgrn_fwd(p3, lb_logits, hgrn_norm_g, cat, tb, hpb,
                                          rider=_ride_gather_ici([b_up], [(2, 4, 4)]))
    (xhat1, h1b, rstd1), (w_up3,) = _mix_ln1(cat, w_out_full, x2, g0, b0, ln1_g, ln1_b, alpha, tm2,
                                             rider=_ride_gather_d2d(got))
    hh3, got = _proj("ffn_up", h1b, w_up3, 2, tmm, dff // 4, rider=_ride_gather_ici([b_dn]))
    act, (w_dn3,) = _ffn_act_fwd(hh3, fw_full, ffn_conv_b, tm, cbf, rider=_ride_gather_d2d(got))
    ks = dff // N_CHIPS
    ffn = _wgrad("ffn_down", act, w_dn3, (t, d), (t // tmm, 1, N_CHIPS),
                 pl.BlockSpec((tmm, ks), lambda i, j, k: (i, k)),
                 pl.BlockSpec((None, ks, d), lambda i, j, k: (k, 0, 0)),
                 pl.BlockSpec((tmm, d), lambda i, j, k: (i, 0)), dot=_dot)
    dz2, dz2b, dg2, db2, loss_row = _ln2_loss(ffn, xhat1, tgt, ln1_g, ln1_b, ln2_g, ln2_b, alpha, tm2)

    dact = _proj_t("ffn_down_t", dz2b, w_dn3.reshape(dff, d), tmm, ks)
    dhh3, dfw, dfb = _ffn_act_bwd(dact, hh3, fw_full, ffn_conv_b, tm, cbf)
    tt = 2 * tmm if t % (2 * tmm) == 0 else tmm
    d_w_dn = _wgrad("wgrad_down", act, dz2b, (N_CHIPS, ks, d), (N_CHIPS, 2, t // tt),
                    pl.BlockSpec((tt, ks), lambda s, j, k: (k, s)),
                    pl.BlockSpec((tt, d // 2), lambda s, j, k: (k, j)),
                    pl.BlockSpec((None, ks, d // 2), lambda s, j, k: (s, 0, j)))
    wu = 2 * dff // N_CHIPS
    tnu = wu // 2
    per_sec_u = dff // tnu
    pre1, (arr_dn,) = _wgrad(
        "up_t", dhh3, w_up3, (t, d), (t // tmm, 1, 2 * N_CHIPS),
        pl.BlockSpec((None, tmm, tnu), lambda i, j, k: (k // per_sec_u, i, k % per_sec_u)),
        pl.BlockSpec((None, d, tnu), lambda i, j, k: (k // 2, 0, k % 2)),
        pl.BlockSpec((tmm, d), lambda i, j, k: (i, 0)), dot=_dot_nt, rider=_ride_swap([d_w_dn]))
    dz1, dz1b, dg1, db1 = _ln1_bwd(pre1, dz2, xhat1, rstd1, ln1_g, alpha, tm2)
    part_dn = _add_halves("add_halves_w_down", d_w_dn, arr_dn, c_idx)
    d_w_up, (land_dn,) = _wgrad(
        "wgrad_up", h1b, dhh3, (N_CHIPS, d, wu), (N_CHIPS, 2, 2, t // tt),
        pl.BlockSpec((tt, d // 2), lambda s, r, j, k: (k, r)),
        pl.BlockSpec((None, tt, tnu), lambda s, r, j, k: ((2 * s + j) // per_sec_u, k, (2 * s + j) % per_sec_u)),
        pl.BlockSpec((None, d // 2, tnu), lambda s, r, j, k: (s, r, j)), rider=_ride_send_partials([part_dn]))
    dcat, (arr_up,) = _proj_t("out_proj_t", dz1b, w_out_full, tmm, d // 2, rider=_ride_swap([d_w_up], [(0, 2)]))
    d_w_out, (arr_up,) = _wgrad("wgrad_out", cat, dz1b, (d, d), (2, 2, t // tt),
                                pl.BlockSpec((tt, d // 2), lambda r, j, k: (k, r)),
                                pl.BlockSpec((tt, d // 2), lambda r, j, k: (k, j)),
                                pl.BlockSpec((d // 2, d // 2), lambda r, j, k: (r, j)),
                                rider=_ride_swap([d_w_up], [(2, 4)], into=[arr_up]))
    d_w_out = d_w_out.reshape(N_CHIPS, d // N_CHIPS, d)
    part_up = _add_halves("add_halves_w_up", d_w_up, arr_up, c_idx)
    du1, dcng, dcnb = _conv_norm_bwd(dcat, u1, conv_norm_g, conv_norm_b, tm)
    (dp3, dcw, dcb), (arr_out, land_up) = _conv_bwd(
        du1, p3, cw_full, tm2, cb,
        rider=_ride_both(_ride_swap([d_w_out]), _ride_send_partials([part_up], [(0, 1, 4)])))
    part_out = _add_halves("add_halves_w_out", d_w_out, arr_out, c_idx)
    (dp3, dlb, dhg), (land_up, land_out) = _hgrn_bwd(
        p3, lb_logits, hgrn_norm_g, o_pre, states, dcat, dp3, tb, hpb,
        rider=_ride_both(_ride_send_partials([part_up], [(1, 3, 4)], into=[land_up]),
                         _ride_send_partials([part_out])))
    kpad = dcw.shape[0]
    wide = [(dcw, kpad, 1), (dg1, 1, 2), (db1, 1, 2), (dg2, 1, 2), (db2, 1, 2),
            (dcb, 1, 1), (dcng, 1, 1), (dcnb, 1, 1), (dlb, 1, 1), (dhg, 1, 1)]
    n_wide = -(-sum(r * m for _, r, m in wide) // SUBLANE) * SUBLANE
    packs = _pack_small("pack_small", wide, [dfw, dfb], w, dff, n_wide, 2 * SUBLANE)
    d_w_in, (land_up, slots1, slots2) = _wgrad_in(
        h0b, dp3, N_CHIPS, tt,
        rider=_ride_both(_ride_send_partials([part_up], [(3, 4, 4)], into=[land_up]), _ride_exchange8(packs)))
    s1, s2 = _sum_small("sum_small", [slots1, slots2])
    (arr_in,) = _run("swap_w_in", _ride_swap([d_w_in]))
    part_in = _add_halves("add_halves_w_in", d_w_in, arr_in, c_idx)
    h_out, h_up, h_dn = [
        _sum_partials("sum_partials_" + nm, p, a, place_idx)
        for nm, p, a in (("w_out", part_out, land_out), ("w_up", part_up, land_up), ("w_down", part_dn, land_dn))]
    pre0, (land_in, g_w_out, g_w_up, g_w_dn) = _in_t(
        dp3, w_in3, tmm, rider=_ride_both(_ride_send_partials([part_in]), _ride_join([h_out, h_up, h_dn])))
    dx, dg0, db0 = _ln0_bwd(pre0, dz1, x2, g0, alpha, tm2)
    h_in = _sum_partials("sum_partials_w_in", part_in, land_in, place_idx)
    (g_w_in,) = _run("join_w_in", _ride_join([h_in]))

    late = _pack_small("pack_late", [(dg0, 1, 2), (db0, 1, 2)], [], w, dff, SUBLANE, 0)
    (s3,) = _sum_small("sum_late", _run("exchange_late", _ride_exchange8(late)))
    cw_g = lax.dynamic_slice_in_dim(s1[0:kpad], chip * (w // N_CHIPS), w // N_CHIPS, axis=1)
    fw_g = lax.dynamic_slice_in_dim(s2[0:SUBLANE], chip * (dff // N_CHIPS), dff // N_CHIPS, axis=1)

    small = [
        (g0, m_emb_ln_g.reshape(1, d), v_emb_ln_g.reshape(1, d)), (b0, m_emb_ln_b.reshape(1, d), v_emb_ln_b.reshape(1, d)),
        (cw2, m_conv_w[0], v_conv_w[0]), (conv_b, m_conv_b, v_conv_b),
        (conv_norm_g, m_conv_norm_g, v_conv_norm_g), (conv_norm_b, m_conv_norm_b, v_conv_norm_b),
        (lb_logits, m_lb_logits, v_lb_logits), (hgrn_norm_g, m_hgrn_norm_g, v_hgrn_norm_g),
        (ln1_g, m_ln1_g, v_ln1_g), (ln1_b, m_ln1_b, v_ln1_b),
        (fw2, m_ffn_conv_w[0], v_ffn_conv_w[0]), (ffn_conv_b, m_ffn_conv_b, v_ffn_conv_b),
        (ln2_g, m_ln2_g, v_ln2_g), (ln2_b, m_ln2_b, v_ln2_b),
    ]
    r0 = kpad
    layout = [("late", 0, 1, 2), ("late", 2, 1, 2), ("cw",), ("wide", r0 + 8, 1, 1), ("wide", r0 + 9, 1, 1),
              ("wide", r0 + 10, 1, 1), ("lb", r0 + 11), ("wide", r0 + 12, 1, 1), ("wide", r0, 1, 2),
              ("wide", r0 + 2, 1, 2), ("fw",), ("ffn", SUBLANE, 1), ("wide", r0 + 4, 1, 2), ("wide", r0 + 6, 1, 2)]
    so = _adam_small(s1, s2, s3, cw_g, fw_g, lb_logits, small, layout, w)
    sm = {nm: so[4 * i:4 * i + 4] for i, nm in enumerate(
        ["emb_ln_g", "emb_ln_b", "conv_w", "conv_b", "conv_norm_g", "conv_norm_b", "lb_logits", "hgrn_norm_g",
         "ln1_g", "ln1_b", "ffn_conv_w", "ffn_conv_b", "ln2_g", "ln2_b"])}
    bigs = {}
    for nm, wt, g, m, v in (("w_in", w_in2, g_w_in, m_w_in[0], v_w_in[0]), ("w_out", w_out2, g_w_out, m_w_out[0], v_w_out[0]),
                            ("w_ffn_up", w_up2, g_w_up, m_w_ffn_up[0], v_w_ffn_up[0]),
                            ("w_ffn_down", w_dn2, g_w_dn, m_w_ffn_down[0], v_w_ffn_down[0])):
        bigs[nm] = tuple(_adam_big("adam_" + nm, wt, g, m, v))

    loss = lax.psum(loss_row[0, 0], ("x", "y", "c"))

    order = ["emb_ln_g", "emb_ln_b", "w_in", "conv_w", "conv_b", "conv_norm_g", "conv_norm_b", "lb_logits",
             "hgrn_norm_g", "w_out", "ln1_g", "ln1_b", "w_ffn_up", "ffn_conv_w", "ffn_conv_b", "w_ffn_down",
             "ln2_g", "ln2_b"]
    shapes = dict(emb_ln_g=emb_ln_g.shape, emb_ln_b=emb_ln_b.shape, w_in=w_in.shape, conv_w=conv_w.shape,
                  w_out=w_out.shape, w_ffn_up=w_ffn_up.shape, ffn_conv_w=ffn_conv_w.shape, w_ffn_down=w_ffn_down.shape)
    outs = [loss, dx.reshape(x.shape)]
    for which in range(4):
        for nm in order:
            a = bigs[nm][which] if nm in bigs else sm[nm][which]
            outs.append(a.reshape(shapes[nm]) if nm in shapes else a)
    return tuple(outs)


def _pad_rows(a):
    k = a.shape[0]
    kp = -(-k // 16) * 16
    return jnp.pad(a, ((0, kp - k), (0, 0)))


def _unshard_cols(a3):
    s, k, c = a3.shape
    return jnp.transpose(a3, (1, 0, 2)).reshape(k, s * c)
```

```python
import functools

import jax
import jax.numpy as jnp
from jax import lax
from jax.experimental import pallas as pl
from jax.experimental.pallas import tpu as pltpu

F32 = jnp.float32
BF16 = jnp.bfloat16

LN_EPS = 1e-5
RMS_EPS = 1e-6
LANE = 128
SUBLANE = 8
CHUNK = 64
SUB = 8
HALO = 32
FHALO = 8
ROWS = 64
N_CHIPS = 4
VMEM_LIMIT = 56 << 20
NEG_BIG = -1e30

ADAM_LR = 0.001
ADAM_B1 = 0.9
ADAM_B2 = 0.999
ADAM_EPS = 1e-08
ADAM_WD = 0.01
ADAM_STEP = 10

MESH = pl.DeviceIdType.MESH
HBM = pl.BlockSpec(memory_space=pl.ANY)
VMEM_FULL = pl.BlockSpec(memory_space=pltpu.VMEM)


def _params(*sem):
    return pltpu.CompilerParams(dimension_semantics=sem, vmem_limit_bytes=VMEM_LIMIT)


class _Rider:
    def __init__(self, ins, outs, aliases, n_sems, start, finish):
        self.ins, self.outs, self.aliases = list(ins), list(outs), dict(aliases)
        self.n_sems, self.start, self.finish = n_sems, start, finish


def _call(body, args, *, name, grid, in_specs, out_specs, out_shape, scratch_shapes=(), aliases=None, rider=None):
    many = isinstance(out_shape, (list, tuple))
    shapes = list(out_shape) if many else [out_shape]
    ospecs = list(out_specs) if many else [out_specs]
    aliases = dict(aliases or {})
    sem = ("arbitrary",) * len(grid)
    if rider is None:
        res = pl.pallas_call(
            body, name=name, grid=grid, in_specs=list(in_specs), out_specs=ospecs, out_shape=shapes,
            scratch_shapes=list(scratch_shapes), input_output_aliases=aliases, compiler_params=_params(*sem))(*args)
        return res if many else res[0]
    n_in, n_out, n_scr = len(args), len(shapes), len(scratch_shapes)
    nri, nro = len(rider.ins), len(rider.outs)

    def wrapped(*refs):
        ins, rin = refs[:n_in], refs[n_in:n_in + nri]
        o0 = n_in + nri
        outs, rout = refs[o0:o0 + n_out], refs[o0 + n_out:o0 + n_out + nro]
        s0 = o0 + n_out + nro
        scr, (send, recv) = refs[s0:s0 + n_scr], refs[s0 + n_scr:]
        ids = [pl.program_id(a) for a in range(len(grid))]
        first = functools.reduce(jnp.logical_and, [i == 0 for i in ids])
        last = functools.reduce(jnp.logical_and, [i == g - 1 for i, g in zip(ids, grid)])

        @pl.when(first)
        def _():
            rider.start(rin, rout, send, recv)

        body(*ins, *outs, *scr)

        @pl.when(last)
        def _():
            rider.finish(rin, rout, send, recv)

    for ri, ro in rider.aliases.items():
        aliases[n_in + ri] = n_out + ro
    res = pl.pallas_call(
        wrapped, name=name, grid=grid, in_specs=list(in_specs) + [HBM] * nri, out_specs=ospecs + [HBM] * nro,
        out_shape=shapes + rider.outs,
        scratch_shapes=list(scratch_shapes) + [pltpu.SemaphoreType.DMA((rider.n_sems,)),
                                               pltpu.SemaphoreType.DMA((rider.n_sems,))],
        input_output_aliases=aliases, compiler_params=_params(*sem))(*args, *rider.ins)
    main, extra = res[:n_out], list(res[n_out:])
    return (list(main) if many else main[0]), extra


def _div_tile(n, mult, cap):
    best = n
    for t in range(mult, min(n, cap) + 1, mult):
        if n % t == 0:
            best = t
    return best


def _sigmoid(x):
    return 1.0 / (1.0 + jnp.exp(-x))


def _ln_stats(x):
    mu = jnp.mean(x, axis=-1, keepdims=True)
    xc = x - mu
    var = jnp.mean(xc * xc, axis=-1, keepdims=True)
    rstd = lax.rsqrt(var + LN_EPS)
    return xc * rstd, rstd


def _ln_bwd(dy, xhat, rstd, g):
    dyg = dy * g
    m1 = jnp.mean(dyg, axis=-1, keepdims=True)
    m2 = jnp.mean(dyg * xhat, axis=-1, keepdims=True)
    return rstd * (dyg - m1 - xhat * m2)


def _dot_nt(a, b):
    return lax.dot_general(a, b, (((1,), (1,)), ((), ())), preferred_element_type=F32)


def _dot_tn(a, b):
    return lax.dot_general(a, b, (((0,), (0,)), ((), ())), preferred_element_type=F32)


def _dot(a, b):
    return jnp.dot(a, b, preferred_element_type=F32)


def _dot3(m, x):
    mb = m.astype(BF16)
    x1 = x.astype(BF16)
    r1 = x - x1.astype(F32)
    x2 = r1.astype(BF16)
    x3 = (r1 - x2.astype(F32)).astype(BF16)
    return _dot(mb, x1) + _dot(mb, x2) + _dot(mb, x3)


def _place_shard(x, name, chip_idx, dtype):
    r, c = x.shape
    tr = _div_tile(r, 16, 512)

    def body(s_ref, x_ref, o_ref):
        del s_ref
        o_ref[...] = x_ref[...].astype(dtype)

    return pl.pallas_call(
        body, name=name,
        grid_spec=pltpu.PrefetchScalarGridSpec(
            num_scalar_prefetch=1, grid=(r // tr,),
            in_specs=[pl.BlockSpec((tr, c), lambda i, s: (i, 0))],
            out_specs=pl.BlockSpec((None, tr, c), lambda i, s: (s[0], i, 0))),
        out_shape=jax.ShapeDtypeStruct((N_CHIPS, r, c), dtype),
        compiler_params=_params("parallel"),
    )(chip_idx, x)


def _ln0(x, g, b, tm, rider=None):
    t, d = x.shape

    def body(x_ref, g_ref, b_ref, o_ref):
        xh, _ = _ln_stats(x_ref[...])
        o_ref[...] = (xh * g_ref[...] + b_ref[...]).astype(BF16)

    row = pl.BlockSpec((1, d), lambda i: (0, 0))
    return _call(
        body, (x, g, b), name="ln0", grid=(t // tm,),
        in_specs=[pl.BlockSpec((tm, d), lambda i: (i, 0)), row, row],
        out_specs=pl.BlockSpec((tm, d), lambda i: (i, 0)),
        out_shape=jax.ShapeDtypeStruct((t, d), BF16), rider=rider)


def _proj(name, a, w3, n_sec, tm, tn, rider=None):
    m, k = a.shape
    s, _, ws = w3.shape
    sec_w = s * ws // n_sec
    nj = ws // tn
    per_sec = sec_w // tn

    def body(a_ref, w_ref, o_ref):
        o_ref[...] = _dot(a_ref[...], w_ref[...])

    return _call(
        body, (a, w3), name=name, grid=(s * nj, m // tm),
        in_specs=[pl.BlockSpec((tm, k), lambda j, i: (i, 0)),
                  pl.BlockSpec((None, k, tn), lambda j, i: (j // nj, 0, j % nj))],
        out_specs=pl.BlockSpec((None, tm, tn), lambda j, i: (j // per_sec, i, j % per_sec)),
        out_shape=jax.ShapeDtypeStruct((n_sec, m, sec_w), F32), rider=rider)


def _proj_t(name, a, w, tm, tn, rider=None):
    m, k = a.shape
    n = w.shape[0]

    def body(a_ref, w_ref, o_ref):
        o_ref[...] = _dot_nt(a_ref[...], w_ref[...])

    return _call(
        body, (a, w), name=name, grid=(n // tn, m // tm),
        in_specs=[pl.BlockSpec((tm, k), lambda j, i: (i, 0)),
                  pl.BlockSpec((tn, k), lambda j, i: (j, 0))],
        out_specs=pl.BlockSpec((tm, tn), lambda j, i: (i, j)),
        out_shape=jax.ShapeDtypeStruct((m, n), F32), rider=rider)


def _wgrad(name, a, b, out_shape, grid, a_spec, b_spec, o_spec, rider=None, dot=_dot_tn):
    nt = len(grid) - 1

    def body(a_ref, b_ref, o_ref):
        t = pl.program_id(nt)
        prod = dot(a_ref[...], b_ref[...])

        @pl.when(t == 0)
        def _():
            o_ref[...] = prod

        @pl.when(t > 0)
        def _():
            o_ref[...] += prod

    return _call(
        body, (a, b), name=name, grid=grid, in_specs=[a_spec, b_spec], out_specs=o_spec,
        out_shape=jax.ShapeDtypeStruct(out_shape, F32), rider=rider)


def _wgrad_in(h0b, dp3, n_shards, tt, rider=None):
    t, d = h0b.shape
    n_sec, _, sec_w = dp3.shape
    ws = n_sec * sec_w // n_shards
    tn = sec_w // 2
    nq = ws // tn
    tr = d // 2

    def body(*refs):
        a_ref, b_refs, o_ref = refs[0], refs[1:1 + nq], refs[1 + nq]
        k = pl.program_id(2)
        a = a_ref[...]
        prods = [_dot_tn(a, b_ref[...]) for b_ref in b_refs]

        @pl.when(k == 0)
        def _():
            for q in range(nq):
                o_ref[:, q * tn:(q + 1) * tn] = prods[q]

        @pl.when(k > 0)
        def _():
            for q in range(nq):
                o_ref[:, q * tn:(q + 1) * tn] += prods[q]

    def b_spec(q):
        return pl.BlockSpec((None, tt, tn), lambda s, r, k: ((nq * s + q) // 2, k, (nq * s + q) % 2))

    return _call(
        body, (h0b,) + (dp3,) * nq, name="wgrad_in", grid=(n_shards, d // tr, t // tt),
        in_specs=[pl.BlockSpec((tt, tr), lambda s, r, k: (k, r))] + [b_spec(q) for q in range(nq)],
        out_specs=pl.BlockSpec((None, tr, ws), lambda s, r, k: (s, r, 0)),
        out_shape=jax.ShapeDtypeStruct((n_shards, d, ws), F32), rider=rider)


def _in_t(dp3, w_in3, tm, rider=None):
    _, t, sec_w = dp3.shape
    s, d, ws = w_in3.shape
    tk = sec_w // 2
    nq = ws // tk

    def body(*refs):
        a_refs, w_ref, o_ref = refs[:nq], refs[nq], refs[nq + 1]
        k = pl.program_id(1)
        prod = _dot_nt(a_refs[0][...], w_ref[:, 0:tk])
        for q in range(1, nq):
            prod = prod + _dot_nt(a_refs[q][...], w_ref[:, q * tk:(q + 1) * tk])

        @pl.when(k == 0)
        def _():
            o_ref[...] = prod

        @pl.when(k > 0)
        def _():
            o_ref[...] += prod

    def a_spec(q):
        return pl.BlockSpec((None, tm, tk), lambda i, k: ((nq * k + q) // 2, i, (nq * k + q) % 2))

    return _call(
        body, (dp3,) * nq + (w_in3,), name="in_t", grid=(t // tm, s),
        in_specs=[a_spec(q) for q in range(nq)] + [pl.BlockSpec((None, d, ws), lambda i, k: (k, 0, 0))],
        out_specs=pl.BlockSpec((tm, d), lambda i, k: (i, 0)),
        out_shape=jax.ShapeDtypeStruct((t, d), F32), rider=rider)


def _mix_ln1(cat, w_out, x, g0, b0, g1, b1, alpha, tm, rider=None):
    t, d = x.shape

    def body(cat_ref, w_ref, x_ref, g0_ref, b0_ref, g1_ref, b1_ref, xh_ref, h1b_ref, rstd_ref):
        mix = _dot(cat_ref[...], w_ref[...])
        xh0, _ = _ln_stats(x_ref[...])
        z1 = alpha * (xh0 * g0_ref[...] + b0_ref[...]) + mix
        xh1, rstd1 = _ln_stats(z1)
        xh_ref[...] = xh1
        h1b_ref[...] = (xh1 * g1_ref[...] + b1_ref[...]).astype(BF16)
        rstd_ref[...] = rstd1

    row = pl.BlockSpec((1, d), lambda i: (0, 0))
    blk = pl.BlockSpec((tm, d), lambda i: (i, 0))
    return _call(
        body, (cat, w_out, x, g0, b0, g1, b1), name="mix_ln1", grid=(t // tm,),
        in_specs=[blk, pl.BlockSpec((d, d), lambda i: (0, 0)), blk, row, row, row, row],
        out_specs=[blk, blk, pl.BlockSpec((tm, 1), lambda i: (i, 0))],
        out_shape=[jax.ShapeDtypeStruct((t, d), F32), jax.ShapeDtypeStruct((t, d), BF16),
                   jax.ShapeDtypeStruct((t, 1), F32)], rider=rider)


def _ln2_loss(ffn, xhat1, tgt, g1, b1, g2, b2, alpha, tm):
    t, d = xhat1.shape
    ni = t // tm
    inv_d = 1.0 / d

    def body(ffn_ref, xh1_ref, tgt_ref, g1_ref, b1_ref, g2_ref, b2_ref,
             dz2_ref, dz2b_ref, dg2_ref, db2_ref, loss_ref, lrow):
        i = pl.program_id(0)
        h1 = xh1_ref[...] * g1_ref[...] + b1_ref[...]
        xh2, rstd2 = _ln_stats(alpha * h1 + ffn_ref[...])
        g2v = g2_ref[...]
        diff = xh2 * g2v + b2_ref[...] - tgt_ref[...]
        dh2 = diff * inv_d
        sq = jnp.sum(diff * diff, axis=0, keepdims=True)
        dg = jnp.sum(dh2 * xh2, axis=0, keepdims=True)
        db = jnp.sum(dh2, axis=0, keepdims=True)

        @pl.when(i == 0)
        def _():
            lrow[...] = sq
            dg2_ref[...] = dg
            db2_ref[...] = db

        @pl.when(i > 0)
        def _():
            lrow[...] += sq
            dg2_ref[...] += dg
            db2_ref[...] += db

        dz2 = _ln_bwd(dh2, xh2, rstd2, g2v)
        dz2_ref[...] = dz2
        dz2b_ref[...] = dz2.astype(BF16)

        @pl.when(i == ni - 1)
        def _():
            tot = jnp.sum(lrow[...], axis=-1, keepdims=True) * (0.5 * inv_d)
            loss_ref[...] = jnp.broadcast_to(tot, (1, LANE))

    row = pl.BlockSpec((1, d), lambda i: (0, 0))
    blk = pl.BlockSpec((tm, d), lambda i: (i, 0))
    return _call(
        body, (ffn, xhat1, tgt, g1, b1, g2, b2), name="ln2_loss", grid=(ni,),
        in_specs=[blk, blk, blk, row, row, row, row],
        out_specs=[blk, blk, row, row, pl.BlockSpec((1, LANE), lambda i: (0, 0))],
        out_shape=[jax.ShapeDtypeStruct((t, d), F32), jax.ShapeDtypeStruct((t, d), BF16),
                   jax.ShapeDtypeStruct((1, d), F32), jax.ShapeDtypeStruct((1, d), F32),
                   jax.ShapeDtypeStruct((1, LANE), F32)],
        scratch_shapes=[pltpu.VMEM((1, d), F32)])


def _ln1_bwd(pre, dz2, xhat1, rstd1, g1, alpha, tm):
    t, d = dz2.shape

    def body(pre_ref, dz2_ref, xh_ref, rstd_ref, g_ref, dz1_ref, dz1b_ref, dg_ref, db_ref):
        i = pl.program_id(0)
        dh1 = alpha * dz2_ref[...] + pre_ref[...]
        xh = xh_ref[...]
        dg = jnp.sum(dh1 * xh, axis=0, keepdims=True)
        db = jnp.sum(dh1, axis=0, keepdims=True)

        @pl.when(i == 0)
        def _():
            dg_ref[...] = dg
            db_ref[...] = db

        @pl.when(i > 0)
        def _():
            dg_ref[...] += dg
            db_ref[...] += db

        dz1 = _ln_bwd(dh1, xh, rstd_ref[...], g_ref[...])
        dz1_ref[...] = dz1
        dz1b_ref[...] = dz1.astype(BF16)

    row = pl.BlockSpec((1, d), lambda i: (0, 0))
    blk = pl.BlockSpec((tm, d), lambda i: (i, 0))
    return _call(
        body, (pre, dz2, xhat1, rstd1, g1), name="ln1_bwd", grid=(t // tm,),
        in_specs=[blk, blk, blk, pl.BlockSpec((tm, 1), lambda i: (i, 0)), row],
        out_specs=[blk, blk, row, row],
        out_shape=[jax.ShapeDtypeStruct((t, d), F32), jax.ShapeDtypeStruct((t, d), BF16),
                   jax.ShapeDtypeStruct((1, d), F32), jax.ShapeDtypeStruct((1, d), F32)])


def _ln0_bwd(pre, dz1, x, g0, alpha, tm):
    t, d = x.shape

    def body(pre_ref, dz1_ref, x_ref, g_ref, dx_ref, dg_ref, db_ref):
        i = pl.program_id(0)
        dh0 = alpha * dz1_ref[...] + pre_ref[...]
        xh, rstd = _ln_stats(x_ref[...])
        dg = jnp.sum(dh0 * xh, axis=0, keepdims=True)
        db = jnp.sum(dh0, axis=0, keepdims=True)

        @pl.when(i == 0)
        def _():
            dg_ref[...] = dg
            db_ref[...] = db

        @pl.when(i > 0)
        def _():
            dg_ref[...] += dg
            db_ref[...] += db

        dx_ref[...] = _ln_bwd(dh0, xh, rstd, g_ref[...])

    row = pl.BlockSpec((1, d), lambda i: (0, 0))
    blk = pl.BlockSpec((tm, d), lambda i: (i, 0))
    return _call(
        body, (pre, dz1, x, g0), name="ln0_bwd", grid=(t // tm,),
        in_specs=[blk, blk, blk, row], out_specs=[blk, row, row],
        out_shape=[jax.ShapeDtypeStruct((t, d), F32), jax.ShapeDtypeStruct((1, d), F32),
                   jax.ShapeDtypeStruct((1, d), F32)])


def _shift_copies(ext, shifted):
    n = shifted.shape[1]
    for p in range(1, SUBLANE):
        shifted[p - 1] = ext[pl.ds(p, n), :]


def _window(ext, shifted, start, rows):
    p = start % SUBLANE
    if p == 0:
        return ext[pl.ds(start, rows), :]
    return shifted[p - 1, pl.ds(start - p, rows), :]


def _conv_fwd(p3, conv_w, conv_b, cn_g, cn_b, tc, cb, rider=None):
    _, t, w = p3.shape
    kk = conv_w.shape[0]
    off = HALO - (kk - 1)
    hb = tc // HALO

    def body(a_ref, g_ref, ap_ref, gp_ref, w_ref, b_ref, ng_ref, nb_ref, cat_ref, u1_ref, ext, sh):
        i = pl.program_id(1)
        ext[pl.ds(HALO, tc), :] = a_ref[...] * _sigmoid(g_ref[...])
        prev = ap_ref[...] * _sigmoid(gp_ref[...])
        ext[pl.ds(0, HALO), :] = jnp.where(i > 0, prev, 0.0)
        _shift_copies(ext, sh)
        for r in range(tc // ROWS):
            acc = jnp.broadcast_to(b_ref[...], (ROWS, cb))
            for k in range(kk):
                acc = acc + w_ref[k:k + 1, :] * _window(ext, sh, r * ROWS + off + k, ROWS)
            u1_ref[pl.ds(r * ROWS, ROWS), :] = acc
            for g in range(cb // LANE):
                sl = slice(g * LANE, (g + 1) * LANE)
                xh, _ = _ln_stats(acc[:, sl])
                u2 = xh * ng_ref[:, sl] + nb_ref[:, sl]
                cat_ref[pl.ds(r * ROWS, ROWS), sl] = (u2 * _sigmoid(u2)).astype(BF16)

    cur = lambda sec: pl.BlockSpec((None, tc, cb), lambda j, i: (sec, i, j))
    prev = lambda sec: pl.BlockSpec((None, HALO, cb), lambda j, i: (sec, jnp.maximum(i * hb - 1, 0), j))
    row = pl.BlockSpec((1, cb), lambda j, i: (0, j))
    return _call(
        body, (p3, p3, p3, p3, conv_w, conv_b, cn_g, cn_b), name="conv_fwd", grid=(w // cb, t // tc),
        in_specs=[cur(0), cur(1), prev(0), prev(1), pl.BlockSpec((kk, cb), lambda j, i: (0, j)), row, row, row],
        out_specs=[pl.BlockSpec((tc, cb), lambda j, i: (i, j)), pl.BlockSpec((tc, cb), lambda j, i: (i, j))],
        out_shape=[jax.ShapeDtypeStruct((t, 2 * w), BF16), jax.ShapeDtypeStruct((t, w), F32)],
        scratch_shapes=[pltpu.VMEM((tc + HALO, cb), F32),
                        pltpu.VMEM((SUBLANE - 1, tc + HALO - SUBLANE, cb), F32)], rider=rider)


def _conv_norm_bwd(dcat, u1, cn_g, cn_b, tc):
    t, w = u1.shape

    def body(du_ref, u1_ref, ng_ref, nb_ref, du1_ref, dg_ref, db_ref):
        i = pl.program_id(0)
        for g in range(w // LANE):
            sl = slice(g * LANE, (g + 1) * LANE)
            ng = ng_ref[:, sl]
            xh, rstd = _ln_stats(u1_ref[:, sl])
            u2 = xh * ng + nb_ref[:, sl]
            sg = _sigmoid(u2)
            du2 = du_ref[:, sl] * (sg * (1.0 + u2 * (1.0 - sg)))
            dg = jnp.sum(du2 * xh, axis=0, keepdims=True)
            db = jnp.sum(du2, axis=0, keepdims=True)

            @pl.when(i == 0)
            def _():
                dg_ref[:, sl] = dg
                db_ref[:, sl] = db

            @pl.when(i > 0)
            def _():
                dg_ref[:, sl] += dg
                db_ref[:, sl] += db

            du1_ref[:, sl] = _ln_bwd(du2, xh, rstd, ng)

    row = pl.BlockSpec((1, w), lambda i: (0, 0))
    blk = pl.BlockSpec((tc, w), lambda i: (i, 0))
    return pl.pallas_call(
        body, name="conv_norm_bwd", grid=(t // tc,),
        in_specs=[blk, blk, row, row], out_specs=[blk, row, row],
        out_shape=[jax.ShapeDtypeStruct((t, w), F32), jax.ShapeDtypeStruct((1, w), F32),
                   jax.ShapeDtypeStruct((1, w), F32)],
        compiler_params=_params("arbitrary"),
    )(dcat, u1, cn_g, cn_b)


def _conv_bwd(du1, p3, conv_w, tc, cb, rider=None):
    n_sec, t, w = p3.shape
    kk = conv_w.shape[0]
    off = HALO - (kk - 1)
    hb = tc // HALO
    nt = t // tc
    kpad = -(-kk // SUBLANE) * SUBLANE

    def body(d_ref, dn_ref, a_ref, g_ref, ap_ref, gp_ref, w_ref, dp_ref, dw_ref, db_ref,
             extd, extu, shd, shu, wacc, bacc):
        i = pl.program_id(1)

        @pl.when(i == 0)
        def _():
            wacc[...] = jnp.zeros_like(wacc)
            bacc[...] = jnp.zeros_like(bacc)

        extd[pl.ds(0, tc), :] = d_ref[...]
        extd[pl.ds(tc, HALO), :] = jnp.where(i < nt - 1, dn_ref[...], 0.0)
        extu[pl.ds(HALO, tc), :] = a_ref[...] * _sigmoid(g_ref[...])
        extu[pl.ds(0, HALO), :] = jnp.where(i > 0, ap_ref[...] * _sigmoid(gp_ref[...]), 0.0)
        _shift_copies(extd, shd)
        _shift_copies(extu, shu)
        for r in range(tc // ROWS):
            rows = pl.ds(r * ROWS, ROWS)
            acc = jnp.zeros((ROWS, cb), F32)
            for k in range(kk):
                acc = acc + w_ref[k:k + 1, :] * _window(extd, shd, r * ROWS + (kk - 1) - k, ROWS)
            a = a_ref[rows, :]
            sg = _sigmoid(g_ref[rows, :])
            dp_ref[0, rows, :] = (acc * sg).astype(BF16)
            dp_ref[1, rows, :] = (acc * a * sg * (1.0 - sg)).astype(BF16)
            d = d_ref[rows, :]
            bacc[...] += jnp.sum(d.reshape(ROWS // SUBLANE, SUBLANE, cb), axis=0)
            for k in range(kk):
                prod = d * _window(extu, shu, r * ROWS + off + k, ROWS)
                wacc[k] += jnp.sum(prod.reshape(ROWS // SUBLANE, SUBLANE, cb), axis=0)

        @pl.when(i == nt - 1)
        def _():
            for k in range(kk):
                dw_ref[k:k + 1, :] = jnp.sum(wacc[k], axis=0, keepdims=True)
            if kpad > kk:
                dw_ref[kk:kpad, :] = jnp.zeros((kpad - kk, cb), F32)
            db_ref[...] = jnp.sum(bacc[...], axis=0, keepdims=True)

    cur = lambda sec: pl.BlockSpec((None, tc, cb), lambda j, i: (sec, i, j))
    prev = lambda sec: pl.BlockSpec((None, HALO, cb), lambda j, i: (sec, jnp.maximum(i * hb - 1, 0), j))
    return _call(
        body, (du1, du1, p3, p3, p3, p3, conv_w), name="conv_bwd", grid=(w // cb, nt),
        in_specs=[pl.BlockSpec((tc, cb), lambda j, i: (i, j)),
                  pl.BlockSpec((HALO, cb), lambda j, i: (jnp.minimum((i + 1) * hb, t // HALO - 1), j)),
                  cur(0), cur(1), prev(0), prev(1), pl.BlockSpec((kk, cb), lambda j, i: (0, j))],
        out_specs=[pl.BlockSpec((2, tc, cb), lambda j, i: (0, i, j)),
                   pl.BlockSpec((kpad, cb), lambda j, i: (0, j)),
                   pl.BlockSpec((1, cb), lambda j, i: (0, j))],
        out_shape=[jax.ShapeDtypeStruct((n_sec, t, w), BF16), jax.ShapeDtypeStruct((kpad, w), F32),
                   jax.ShapeDtypeStruct((1, w), F32)],
        scratch_shapes=[pltpu.VMEM((tc + HALO, cb), F32), pltpu.VMEM((tc + HALO, cb), F32),
                        pltpu.VMEM((SUBLANE - 1, tc + HALO - SUBLANE, cb), F32),
                        pltpu.VMEM((SUBLANE - 1, tc + HALO - SUBLANE, cb), F32),
                        pltpu.VMEM((kk, SUBLANE, cb), F32), pltpu.VMEM((SUBLANE, cb), F32)], rider=rider)


def _ffn_act_fwd(hh3, fw, fb, tc, cb, rider=None):
    _, t, dff = hh3.shape
    kk = fw.shape[0]
    off = FHALO - (kk - 1)
    hb = tc // FHALO

    def body(g_ref, v_ref, gp_ref, w_ref, b_ref, act_ref, ext):
        i = pl.program_id(1)
        ext[pl.ds(FHALO, tc), :] = g_ref[...]
        ext[pl.ds(0, FHALO), :] = jnp.where(i > 0, gp_ref[...], 0.0)
        for r in range(tc // ROWS):
            rows = pl.ds(r * ROWS, ROWS)
            gc = jnp.broadcast_to(b_ref[...], (ROWS, cb))
            for k in range(kk):
                gc = gc + w_ref[k:k + 1, :] * ext[pl.ds(r * ROWS + off + k, ROWS), :]
            act_ref[rows, :] = (gc * _sigmoid(gc) * v_ref[rows, :]).astype(BF16)

    return _call(
        body, (hh3, hh3, hh3, fw, fb), name="ffn_act_fwd", grid=(dff // cb, t // tc),
        in_specs=[pl.BlockSpec((None, tc, cb), lambda j, i: (0, i, j)),
                  pl.BlockSpec((None, tc, cb), lambda j, i: (1, i, j)),
                  pl.BlockSpec((None, FHALO, cb), lambda j, i: (0, jnp.maximum(i * hb - 1, 0), j)),
                  pl.BlockSpec((kk, cb), lambda j, i: (0, j)),
                  pl.BlockSpec((1, cb), lambda j, i: (0, j))],
        out_specs=pl.BlockSpec((tc, cb), lambda j, i: (i, j)),
        out_shape=jax.ShapeDtypeStruct((t, dff), BF16),
        scratch_shapes=[pltpu.VMEM((tc + FHALO, cb), F32)], rider=rider)


def _ffn_act_bwd(dact, hh3, fw, fb, tc, cb):
    _, t, dff = hh3.shape
    kk = fw.shape[0]
    off = FHALO - (kk - 1)
    hb = tc // FHALO
    nt = t // tc
    te = tc + FHALO

    def body(da_ref, dan_ref, g_ref, gp_ref, gn_ref, v_ref, vn_ref, w_ref, b_ref,
             dhh_ref, dw_ref, db_ref, gext, dext, wacc, bacc):
        i = pl.program_id(1)

        @pl.when(i == 0)
        def _():
            wacc[...] = jnp.zeros_like(wacc)
            bacc[...] = jnp.zeros_like(bacc)

        gext[pl.ds(0, FHALO), :] = jnp.where(i > 0, gp_ref[...], 0.0)
        gext[pl.ds(FHALO, tc), :] = g_ref[...]
        gext[pl.ds(FHALO + tc, FHALO), :] = gn_ref[...]

        def gate_grad(r0, n, da, v):
            gc = jnp.broadcast_to(b_ref[...], (n, cb))
            for k in range(kk):
                gc = gc + w_ref[k:k + 1, :] * gext[pl.ds(r0 + off + k, n), :]
            sg = _sigmoid(gc)
            return gc * sg, da * v * (sg * (1.0 + gc * (1.0 - sg)))

        for r in range(tc // ROWS):
            rows = pl.ds(r * ROWS, ROWS)
            da = da_ref[rows, :]
            silu, dgc = gate_grad(r * ROWS, ROWS, da, v_ref[rows, :])
            dext[rows, :] = dgc
            dhh_ref[1, rows, :] = (da * silu).astype(BF16)
        _, dgc_next = gate_grad(tc, FHALO, dan_ref[...], vn_ref[...])
        dext[pl.ds(tc, FHALO), :] = jnp.where(i < nt - 1, dgc_next, 0.0)
        for r in range(tc // ROWS):
            rows = pl.ds(r * ROWS, ROWS)
            dg = jnp.zeros((ROWS, cb), F32)
            for k in range(kk):
                dg = dg + w_ref[k:k + 1, :] * dext[pl.ds(r * ROWS + (kk - 1) - k, ROWS), :]
            dhh_ref[0, rows, :] = dg.astype(BF16)
            dgc = dext[rows, :]
            bacc[...] += jnp.sum(dgc.reshape(ROWS // SUBLANE, SUBLANE, cb), axis=0)
            for k in range(kk):
                prod = dgc * gext[pl.ds(r * ROWS + off + k, ROWS), :]
                wacc[k] += jnp.sum(prod.reshape(ROWS // SUBLANE, SUBLANE, cb), axis=0)

        @pl.when(i == nt - 1)
        def _():
            for k in range(kk):
                dw_ref[k:k + 1, :] = jnp.sum(wacc[k], axis=0, keepdims=True)
            dw_ref[kk:SUBLANE, :] = jnp.zeros((SUBLANE - kk, cb), F32)
            db_ref[...] = jnp.sum(bacc[...], axis=0, keepdims=True)

    nxt = lambda i: jnp.minimum((i + 1) * hb, t // FHALO - 1)
    return pl.pallas_call(
        body, name="ffn_act_bwd", grid=(dff // cb, nt),
        in_specs=[pl.BlockSpec((tc, cb), lambda j, i: (i, j)),
                  pl.BlockSpec((FHALO, cb), lambda j, i: (nxt(i), j)),
                  pl.BlockSpec((None, tc, cb), lambda j, i: (0, i, j)),
                  pl.BlockSpec((None, FHALO, cb), lambda j, i: (0, jnp.maximum(i * hb - 1, 0), j)),
                  pl.BlockSpec((None, FHALO, cb), lambda j, i: (0, nxt(i), j)),
                  pl.BlockSpec((None, tc, cb), lambda j, i: (1, i, j)),
                  pl.BlockSpec((None, FHALO, cb), lambda j, i: (1, nxt(i), j)),
                  pl.BlockSpec((kk, cb), lambda j, i: (0, j)),
                  pl.BlockSpec((1, cb), lambda j, i: (0, j))],
        out_specs=[pl.BlockSpec((2, tc, cb), lambda j, i: (0, i, j)),
                   pl.BlockSpec((SUBLANE, cb), lambda j, i: (0, j)),
                   pl.BlockSpec((1, cb), lambda j, i: (0, j))],
        out_shape=[jax.ShapeDtypeStruct((2, t, dff), BF16), jax.ShapeDtypeStruct((SUBLANE, dff), F32),
                   jax.ShapeDtypeStruct((1, dff), F32)],
        scratch_shapes=[pltpu.VMEM((tc + 2 * FHALO, cb), F32), pltpu.VMEM((te, cb), F32),
                        pltpu.VMEM((kk, SUBLANE, cb), F32), pltpu.VMEM((SUBLANE, cb), F32)],
        compiler_params=_params("parallel", "arbitrary"),
    )(dact, dact, hh3, hh3, hh3, hh3, hh3, fw, fb)


def _chunk_consts():
    r = lax.broadcasted_iota(jnp.int32, (CHUNK, CHUNK), 0)
    c = lax.broadcasted_iota(jnp.int32, (CHUNK, CHUNK), 1)
    return (c <= r).astype(F32)


def _roll8(x, d):
    return pltpu.roll(x.reshape(CHUNK // SUB, SUB, LANE), d % SUB, 1).reshape(CHUNK, LANE)


def _gate_terms(q, fpre, lb):
    sf = _sigmoid(fpre)
    fg = lb + (1.0 - lb) * sf
    sq = _sigmoid(q)
    return sf, fg, 1.0 - fg, sq, q * sq


def _decays(g, consts):
    b = _dot3(consts, g)
    nb = CHUNK // SUB
    ends = b.reshape(nb, SUB, LANE)[:, SUB - 1:SUB, :]
    re3 = jnp.broadcast_to(ends, (nb, SUB, LANE))
    rs3 = jnp.concatenate([jnp.zeros((1, SUB, LANE), F32), re3[:nb - 1]], axis=0)
    return b, rs3.reshape(CHUNK, LANE), re3.reshape(CHUNK, LANE), b[CHUNK - 1:CHUNK]


def _lower_bound(lb_ref):
    l0, l1 = lb_ref[0:1, :], lb_ref[1:2, :]
    mx = jnp.maximum(l0, l1)
    e0, e1 = jnp.exp(l0 - mx), jnp.exp(l1 - mx)
    return e0 / (e0 + e1)


def _scaled_keys(kt, rs, re, rowblk, i):
    scale = jnp.where(rowblk < i, jnp.exp(rs[SUB * i:SUB * i + 1, :] - re), 0.0)
    return kt * scale, scale


def _hgrn_fwd(p3, lb_logits, hg, cat, tb, hpb, rider=None):
    _, t, w = p3.shape
    nh = w // LANE
    nc = tb // CHUNK
    assert nh % hpb == 0

    def body(q_ref, f_ref, v_ref, og_ref, lb_ref, hg_ref, cat_in, cat_ref, o_ref, st_ref, state):
        del cat_in
        consts = _chunk_consts()
        lb_all = _lower_bound(lb_ref)
        rowblk = lax.broadcasted_iota(jnp.int32, (CHUNK, 1), 0) // SUB
        rowpos = lax.broadcasted_iota(jnp.int32, (CHUNK, 1), 0) % SUB

        @pl.when(pl.program_id(1) == 0)
        def _():
            state[...] = jnp.zeros_like(state)

        def chunk(c, carry):
            rows = pl.ds(pl.multiple_of(c * CHUNK, CHUNK), CHUNK)
            heads = range(hpb)
            sls = [slice(j * LANE, (j + 1) * LANE) for j in heads]
            v = [v_ref[rows, s] for s in sls]
            vb = [x.astype(BF16) for x in v]
            gates = [_gate_terms(q_ref[rows, s], f_ref[rows, s], lb_all[:, s]) for s in sls]
            fg = [g[1] for g in gates]
            kk = [g[2] for g in gates]
            qh = [g[4] for g in gates]
            dec = [_decays(jnp.log(x), consts) for x in fg]
            b = [x[0] for x in dec]
            rs = [x[1] for x in dec]
            re = [x[2] for x in dec]
            tot = [x[3] for x in dec]
            qt = [qh[j] * jnp.exp(b[j] - rs[j]) for j in heads]
            kt = [kk[j] * jnp.exp(re[j] - b[j]) for j in heads]
            st = [state[j] for j in heads]
            for j in heads:
                st_ref[j, c] = st[j]
            a = [jnp.zeros((CHUNK, CHUNK), F32) for _ in heads]
            for i in range(1, CHUNK // SUB):
                for j in heads:
                    ki, _ = _scaled_keys(kt[j], rs[j], re[j], rowblk, i)
                    a[j] = a[j] + _dot_nt(jnp.where(rowblk == i, qt[j], 0.0).astype(BF16), ki.astype(BF16))
            o = [_dot(a[j].astype(BF16), vb[j]) for j in heads]
            o = [o[j] + _dot_nt((qh[j] * jnp.exp(b[j])).astype(BF16), st[j].astype(BF16)) for j in heads]
            for j in heads:
                k_up = kk[j] * jnp.exp(tot[j] - b[j])
                state[j] = st[j] * jnp.exp(tot[j]) + _dot_tn(vb[j], k_up.astype(BF16))
            for j in heads:
                e, rf = None, fg[j]
                for d in range(SUB):
                    if d == 0:
                        vs, term = v[j], qh[j] * kk[j]
                    else:
                        e = rf if e is None else e * rf
                        rf = _roll8(fg[j], d)
                        vs = _roll8(v[j], d)
                        term = jnp.where(rowpos >= d, qh[j] * (1.0 - rf) * e, 0.0)
                    o[j] = o[j] + jnp.sum(term, axis=-1, keepdims=True) * vs
            for j in heads:
                og = og_ref[rows, sls[j]]
                o_ref[rows, sls[j]] = o[j]
                r = lax.rsqrt(jnp.mean(o[j] * o[j], axis=-1, keepdims=True) + RMS_EPS)
                cat_ref[rows, sls[j]] = (o[j] * r * hg_ref[:, sls[j]] * (og * _sigmoid(og))).astype(BF16)
            return carry

        lax.fori_loop(0, nc, chunk, 0)

    bw = hpb * LANE
    sec = lambda s: pl.BlockSpec((None, tb, bw), lambda h, i: (s, i, h))
    return _call(
        body, (p3, p3, p3, p3, lb_logits, hg, cat), name="hgrn_fwd", grid=(nh // hpb, t // tb),
        in_specs=[sec(2), sec(3), sec(4), sec(5),
                  pl.BlockSpec((2, bw), lambda h, i: (0, h)),
                  pl.BlockSpec((1, bw), lambda h, i: (0, h)), HBM],
        out_specs=[pl.BlockSpec((tb, bw), lambda h, i: (i, nh // hpb + h)),
                   pl.BlockSpec((tb, bw), lambda h, i: (i, h)),
                   pl.BlockSpec((hpb, nc, LANE, LANE), lambda h, i: (h, i, 0, 0))],
        out_shape=[jax.ShapeDtypeStruct(cat.shape, BF16), jax.ShapeDtypeStruct((t, w), F32),
                   jax.ShapeDtypeStruct((nh, t // CHUNK, LANE, LANE), F32)],
        scratch_shapes=[pltpu.VMEM((hpb, LANE, LANE), F32)], aliases={6: 0}, rider=rider)


def _hgrn_bwd(p3, lb_logits, hg, o_pre, states, dcat, dp3, tb, hpb, rider=None):
    n_sec, t, w = p3.shape
    nh = w // LANE
    assert nh % hpb == 0
    nc = tb // CHUNK
    nb = t // tb
    bw = hpb * LANE
    n_steps = (nh // hpb) * nb

    def body(q_ref, f_ref, v_ref, og_ref, lb_ref, hg_ref, o_ref, st_ref, dc_ref, dp_in,
             dp_ref, dlb_ref, dhg_ref, dstate, stash, lbacc, hgacc, osem):
        del dp_in
        h, i = pl.program_id(0), pl.program_id(1)
        step = h * nb + i
        slot = step % 2

        def out_copy(s, row_blk, lane_blk):
            dst = dp_ref.at[pl.ds(2, 4), pl.ds(row_blk * tb, tb), pl.ds(lane_blk * bw, bw)]
            return pltpu.make_async_copy(stash.at[s], dst, osem.at[s])

        @pl.when(step >= 2)
        def _():
            out_copy(slot, 0, 0).wait()

        def compute():
            consts = _chunk_consts()
            rr = lax.broadcasted_iota(jnp.int32, (CHUNK, CHUNK), 0)
            cc = lax.broadcasted_iota(jnp.int32, (CHUNK, CHUNK), 1)
            upper = (cc >= rr).astype(F32)
            lb_all = _lower_bound(lb_ref)
            rowblk = lax.broadcasted_iota(jnp.int32, (CHUNK, 1), 0) // SUB
            rowpos = lax.broadcasted_iota(jnp.int32, (CHUNK, 1), 0) % SUB

            @pl.when(i == 0)
            def _():
                dstate[...] = jnp.zeros_like(dstate)
                lbacc[...] = jnp.zeros_like(lbacc)
                hgacc[...] = jnp.zeros_like(hgacc)

            def head(j, c, rows):
                sl = slice(j * LANE, (j + 1) * LANE)
                lb = lb_all[:, sl]
                hgv = hg_ref[:, sl]
                q = q_ref[rows, sl]
                v = v_ref[rows, sl]
                og = og_ref[rows, sl]
                o = o_ref[rows, sl]
                dcg = dc_ref[rows, sl]
                sf, fg, kk, sq, qh = _gate_terms(q, f_ref[rows, sl], lb)
                b, rs, re, tot = _decays(jnp.log(fg), consts)
                eq = jnp.exp(b - rs)
                ek = jnp.exp(re - b)
                qt = qh * eq
                kt = kk * ek
                e_in = jnp.exp(b)
                e_up = jnp.exp(tot - b)
                e_tot = jnp.exp(tot)
                q_in = (qh * e_in).astype(BF16)
                k_up = (kk * e_up).astype(BF16)
                vb = v.astype(BF16)
                st = st_ref[j, c]
                dst = dstate[j]
                dstb = dst.astype(BF16)
                yield

                sg = _sigmoid(og)
                r = lax.rsqrt(jnp.mean(o * o, axis=-1, keepdims=True) + RMS_EPS)
                ohat = o * r
                d_og = dcg * ohat * hgv * (sg * (1.0 + og * (1.0 - sg)))
                d_on = dcg * (og * sg)
                hgacc[:, sl] += jnp.sum((d_on * ohat).reshape(CHUNK // SUBLANE, SUBLANE, LANE), axis=0)
                d_oh = d_on * hgv
                do = r * (d_oh - ohat * jnp.mean(d_oh * ohat, axis=-1, keepdims=True))
                dob = do.astype(BF16)

                da = _dot_nt(dob, vb)
                yield
                a_off = jnp.zeros((CHUNK, CHUNK), F32)
                dqt = jnp.zeros((CHUNK, LANE), F32)
                dkt = jnp.zeros((CHUNK, LANE), F32)
                for blk in range(1, CHUNK // SUB):
                    ki, scale = _scaled_keys(kt, rs, re, rowblk, blk)
                    kib = ki.astype(BF16)
                    qib = jnp.where(rowblk == blk, qt, 0.0).astype(BF16)
                    dab = jnp.where(rowblk == blk, da, 0.0).astype(BF16)
                    a_off = a_off + _dot_nt(qib, kib)
                    dqt = dqt + _dot(dab, kib)
                    dkt = dkt + _dot_tn(dab, qib) * scale
                    yield
                dqh = dqt * eq
                dk = dkt * ek
                dv = _dot_tn(a_off.astype(BF16), dob)

                dqh = dqh + _dot(dob, st.astype(BF16)) * e_in
                dk = dk + _dot(vb, dstb) * e_up
                dv = dv + _dot_nt(k_up, dstb)
                st_end = st * e_tot + _dot_tn(vb, k_up)
                carry_g = jnp.sum(st_end * dst, axis=0, keepdims=True)
                dstate[j] = dst * e_tot + _dot_tn(dob, q_in)
                yield

                e, rf = None, fg
                for d in range(SUB):
                    if d == 0:
                        a_d = jnp.sum(qh * kk, axis=-1, keepdims=True)
                        da_d = jnp.sum(do * v, axis=-1, keepdims=True)
                        dqh = dqh + da_d * kk
                        dk = dk + da_d * qh
                        dv = dv + a_d * do
                        continue
                    e = rf if e is None else e * rf
                    rf = _roll8(fg, d)
                    em = jnp.where(rowpos >= d, e, 0.0)
                    ks, vs = 1.0 - rf, _roll8(v, d)
                    a_d = jnp.sum(qh * ks * em, axis=-1, keepdims=True)
                    da_d = jnp.sum(do * vs, axis=-1, keepdims=True) * em
                    dqh = dqh + da_d * ks
                    dk = dk + _roll8(da_d * qh, -d)
                    dv = dv + _roll8(a_d * do, -d)
                yield

                dg = _dot3(upper, qh * dqh - kk * dk) + carry_g
                dfg = dg / fg - dk
                lbacc[:, sl] += jnp.sum((dfg * (1.0 - sf)).reshape(CHUNK // SUBLANE, SUBLANE, LANE), axis=0)
                stash[slot, 0, rows, sl] = (dqh * (sq * (1.0 + q * (1.0 - sq)))).astype(BF16)
                stash[slot, 1, rows, sl] = (dfg * (1.0 - lb) * sf * (1.0 - sf)).astype(BF16)
                stash[slot, 2, rows, sl] = dv.astype(BF16)
                stash[slot, 3, rows, sl] = d_og.astype(BF16)

            def chunk(cr, carry):
                c = nc - 1 - cr
                rows = pl.ds(pl.multiple_of(c * CHUNK, CHUNK), CHUNK)
                running = [head(j, c, rows) for j in range(hpb)]
                while running:
                    running = [g for g in running if next(g, StopIteration) is not StopIteration]
                return carry

            lax.fori_loop(0, nc, chunk, 0)

            @pl.when(i == nb - 1)
            def _():
                dlb_ref[...] = jnp.sum(lbacc[...], axis=0, keepdims=True)
                dhg_ref[...] = jnp.sum(hgacc[...], axis=0, keepdims=True)

        compute()
        out_copy(slot, nb - 1 - i, h).start()

        @pl.when(step == n_steps - 1)
        def _():
            out_copy(slot, 0, 0).wait()
            if n_steps >= 2:
                out_copy(1 - slot, 0, 0).wait()

    rev = lambda i: nb - 1 - i
    sec = lambda s: pl.BlockSpec((None, tb, bw), lambda h, i: (s, rev(i), h))
    return _call(
        body, (p3, p3, p3, p3, lb_logits, hg, o_pre, states, dcat, dp3), name="hgrn_bwd", grid=(nh // hpb, nb),
        in_specs=[sec(2), sec(3), sec(4), sec(5),
                  pl.BlockSpec((2, bw), lambda h, i: (0, h)),
                  pl.BlockSpec((1, bw), lambda h, i: (0, h)),
                  pl.BlockSpec((tb, bw), lambda h, i: (rev(i), h)),
                  pl.BlockSpec((hpb, nc, LANE, LANE), lambda h, i: (h, rev(i), 0, 0)),
                  pl.BlockSpec((tb, bw), lambda h, i: (rev(i), nh // hpb + h)), HBM],
        out_specs=[HBM,
                   pl.BlockSpec((1, bw), lambda h, i: (0, h)),
                   pl.BlockSpec((1, bw), lambda h, i: (0, h))],
        out_shape=[jax.ShapeDtypeStruct((n_sec, t, w), BF16), jax.ShapeDtypeStruct((1, w), F32),
                   jax.ShapeDtypeStruct((1, w), F32)],
        scratch_shapes=[pltpu.VMEM((hpb, LANE, LANE), F32), pltpu.VMEM((2, 4, tb, bw), BF16),
                        pltpu.VMEM((SUBLANE, bw), F32), pltpu.VMEM((SUBLANE, bw), F32),
                        pltpu.SemaphoreType.DMA((2,))],
        aliases={9: 0}, rider=rider)


def _place():
    x, y, c = lax.axis_index("x"), lax.axis_index("y"), lax.axis_index("c")
    chips = [(1 - x, y), (x, 1 - y), (1 - x, 1 - y)]
    return x, y, c, chips


def _rows(buf, px, py, pc, part=None):
    half = buf.shape[1] // 2
    if part is None:
        return buf.at[2 * px + py, pl.ds(pc * half, half)]
    lo, hi, n = part
    piece = half // n
    return buf.at[2 * px + py, pl.ds(pc * half + lo * piece, (hi - lo) * piece)]


def _rcopy(src, dst, send, recv, idx, to):
    return pltpu.make_async_remote_copy(src_ref=src, dst_ref=dst, send_sem=send.at[idx], recv_sem=recv.at[idx],
                                        device_id=to, device_id_type=MESH)


def _same(bufs):
    return [jax.ShapeDtypeStruct(b.shape, b.dtype) for b in bufs]


def _ride_gather_ici(bufs, parts=None):
    n = len(bufs)
    parts = parts or [None] * n

    def start(rin, rout, send, recv):
        x, y, c, chips = _place()
        for k in range(n):
            mine = _rows(rout[k], x, y, c, parts[k])
            for j, chip in enumerate(chips):
                _rcopy(mine, mine, send, recv, 3 * k + j, (*chip, c)).start()

    def finish(rin, rout, send, recv):
        x, y, c, chips = _place()
        for k in range(n):
            for j, chip in enumerate(chips):
                theirs = _rows(rout[k], *chip, c, parts[k])
                _rcopy(theirs, theirs, send, recv, 3 * k + j, (x, y, c)).wait_recv()
        for k in range(n):
            mine = _rows(rout[k], x, y, c, parts[k])
            for j in range(3):
                _rcopy(mine, mine, send, recv, 3 * k + j, (x, y, c)).wait_send()

    return _Rider(bufs, _same(bufs), {k: k for k in range(n)}, 3 * n, start, finish)


class _SemView:
    def __init__(self, ref, base):
        self.ref, self.base = ref, base

    @property
    def at(self):
        return self

    def __getitem__(self, idx):
        return self.ref.at[idx + self.base]


def _ride_both(a, b):
    nai, nao = len(a.ins), len(a.outs)

    def start(rin, rout, send, recv):
        a.start(rin[:nai], rout[:nao], send, recv)
        b.start(rin[nai:], rout[nao:], _SemView(send, a.n_sems), _SemView(recv, a.n_sems))

    def finish(rin, rout, send, recv):
        a.finish(rin[:nai], rout[:nao], send, recv)
        b.finish(rin[nai:], rout[nao:], _SemView(send, a.n_sems), _SemView(recv, a.n_sems))

    aliases = dict(a.aliases)
    aliases.update({nai + ri: nao + ro for ri, ro in b.aliases.items()})
    return _Rider(a.ins + b.ins, a.outs + b.outs, aliases, a.n_sems + b.n_sems, start, finish)


def _ride_gather_d2d(bufs):
    n = len(bufs)

    def start(rin, rout, send, recv):
        x, y, c, chips = _place()
        for k in range(n):
            for j, chip in enumerate(chips):
                got = _rows(rout[k], *chip, c)
                _rcopy(got, got, send, recv, 3 * k + j, (x, y, 1 - c)).start()

    def finish(rin, rout, send, recv):
        x, y, c, chips = _place()
        for k in range(n):
            for j, chip in enumerate(chips):
                theirs = _rows(rout[k], *chip, 1 - c)
                _rcopy(theirs, theirs, send, recv, 3 * k + j, (x, y, c)).wait_recv()
        for k in range(n):
            for j, chip in enumerate(chips):
                got = _rows(rout[k], *chip, c)
                _rcopy(got, got, send, recv, 3 * k + j, (x, y, c)).wait_send()

    return _Rider(bufs, _same(bufs), {k: k for k in range(n)}, 3 * n, start, finish)


def _ride_swap(grads):
    n = len(grads)

    def copy(k, rin, rout, send, recv):
        x, y, c, _ = _place()
        half = rin[k].shape[1] // 2
        return _rcopy(rin[k].at[:, pl.ds((1 - c) * half, half)], rout[k], send, recv, k, (x, y, 1 - c))

    def start(rin, rout, send, recv):
        for k in range(n):
            copy(k, rin, rout, send, recv).start()

    def finish(rin, rout, send, recv):
        for k in range(n):
            copy(k, rin, rout, send, recv).wait()

    outs = [jax.ShapeDtypeStruct((g.shape[0], g.shape[1] // 2, g.shape[2]), g.dtype) for g in grads]
    return _Rider(grads, outs, {}, n, start, finish)


def _ride_send_partials(parts, pieces=None, into=None):
    n = len(parts)
    pieces = pieces or [None] * n

    def cut(ref, k):
        if pieces[k] is None:
            return ref
        lo, hi, m = pieces[k]
        q = ref.shape[0] // m
        return ref.at[pl.ds(lo * q, (hi - lo) * q)]

    def copies(rin, rout, send, recv):
        x, y, c, chips = _place()
        return [_rcopy(cut(rin[k].at[2 * px + py], k), cut(rout[k].at[j], k), send, recv, 3 * k + j, (px, py, c))
                for k in range(n) for j, (px, py) in enumerate(chips)]

    def start(rin, rout, send, recv):
        for cp in copies(rin, rout, send, recv):
            cp.start()

    def finish(rin, rout, send, recv):
        for cp in copies(rin, rout, send, recv):
            cp.wait()

    if into is None:
        outs = [jax.ShapeDtypeStruct((3,) + p.shape[1:], p.dtype) for p in parts]
        return _Rider(parts, outs, {}, 3 * n, start, finish)
    return _Rider(list(parts) + list(into), _same(into), {n + k: k for k in range(n)}, 3 * n, start, finish)


def _ride_join(bufs):
    n = len(bufs)

    def half_of(buf, pc):
        half = buf.shape[0] // 2
        return buf.at[pl.ds(pc * half, half)]

    def start(rin, rout, send, recv):
        x, y, c, _ = _place()
        for k in range(n):
            mine = half_of(rout[k], c)
            _rcopy(mine, mine, send, recv, k, (x, y, 1 - c)).start()

    def finish(rin, rout, send, recv):
        x, y, c, _ = _place()
        for k in range(n):
            mine, theirs = half_of(rout[k], c), half_of(rout[k], 1 - c)
            _rcopy(mine, mine, send, recv, k, (x, y, c)).wait_send()
            _rcopy(theirs, theirs, send, recv, k, (x, y, c)).wait_recv()

    return _Rider(bufs, _same(bufs), {k: k for k in range(n)}, n, start, finish)


def _run(name, rider):
    def body(*refs):
        nri, nro = len(rider.ins), len(rider.outs)
        rin, rout = refs[:nri], refs[nri:nri + nro]
        send, recv = refs[nri + nro:]
        rider.start(rin, rout, send, recv)
        rider.finish(rin, rout, send, recv)

    return pl.pallas_call(
        body, name=name, in_specs=[HBM] * len(rider.ins), out_specs=[HBM] * len(rider.outs), out_shape=rider.outs,
        scratch_shapes=[pltpu.SemaphoreType.DMA((rider.n_sems,)), pltpu.SemaphoreType.DMA((rider.n_sems,))],
        input_output_aliases=rider.aliases,
    )(*rider.ins)


def _add_halves(name, g, other, c_idx):
    s, r, cols = g.shape
    half = r // 2
    tr = _div_tile(half, 16, 512)
    nb = half // tr

    def body(c_ref, g_ref, o_ref, q_ref):
        del c_ref
        q_ref[...] = (g_ref[...] + o_ref[...]).astype(BF16)

    return pl.pallas_call(
        body, name=name,
        grid_spec=pltpu.PrefetchScalarGridSpec(
            num_scalar_prefetch=1, grid=(s, nb),
            in_specs=[pl.BlockSpec((None, tr, cols), lambda k, i, c: (k, c[0] * nb + i, 0)),
                      pl.BlockSpec((None, tr, cols), lambda k, i, c: (k, i, 0))],
            out_specs=pl.BlockSpec((None, tr, cols), lambda k, i, c: (k, i, 0))),
        out_shape=jax.ShapeDtypeStruct((s, half, cols), BF16),
        compiler_params=_params("parallel", "parallel"),
    )(c_idx, g, other)


def _sum_partials(name, part, arrived, place_idx):
    _, half, cols = part.shape
    tr = _div_tile(half, 16, 512)
    nb = half // tr

    def body(s_ref, p_ref, a_ref, o_ref):
        del s_ref
        o_ref[...] = ((p_ref[...].astype(F32) + a_ref[0].astype(F32)) + a_ref[1].astype(F32)) + a_ref[2].astype(F32)

    return pl.pallas_call(
        body, name=name,
        grid_spec=pltpu.PrefetchScalarGridSpec(
            num_scalar_prefetch=1, grid=(nb,),
            in_specs=[pl.BlockSpec((None, tr, cols), lambda i, s: (s[0], i, 0)),
                      pl.BlockSpec((3, tr, cols), lambda i, s: (0, i, 0))],
            out_specs=pl.BlockSpec((tr, cols), lambda i, s: (s[1] * nb + i, 0))),
        out_shape=jax.ShapeDtypeStruct((2 * half, cols), F32),
        compiler_params=_params("parallel"),
    )(place_idx, part, arrived)


def _pack_small(name, wide_rows, ffn_rows, w, dff, n_wide, n_ffn):
    n_in = len(wide_rows) + len(ffn_rows)

    def body(*refs):
        ins, outs = refs[:n_in], refs[n_in:]
        p1 = outs[0]
        p1[...] = jnp.zeros_like(p1)
        row = 0
        for ref, (_, r, m) in zip(ins, wide_rows):
            if m == 1 and r % SUBLANE == 0 and row % SUBLANE == 0:
                p1[row:row + r, :] = ref[...]
                row += r
                continue
            for rr in range(r):
                for mm in range(m):
                    p1[row:row + 1, :] = ref[rr:rr + 1, mm * w:(mm + 1) * w]
                    row += 1
        if ffn_rows:
            p2 = outs[1]
            p2[...] = jnp.zeros_like(p2)
            row = 0
            for ref, arr in zip(ins[len(wide_rows):], ffn_rows):
                r = arr.shape[0]
                p2[row:row + r, :] = ref[...]
                row += r

    shapes = [jax.ShapeDtypeStruct((n_wide, w), F32)] + ([jax.ShapeDtypeStruct((n_ffn, dff), F32)] if ffn_rows else [])
    return pl.pallas_call(
        body, name=name, in_specs=[VMEM_FULL] * n_in, out_specs=[VMEM_FULL] * len(shapes), out_shape=shapes,
        compiler_params=pltpu.CompilerParams(vmem_limit_bytes=VMEM_LIMIT),
    )(*[a for a, _, _ in wide_rows], *ffn_rows)


def _ride_exchange8(packs):
    n = len(packs)

    def copies(rin, rout, send, recv):
        x, y, c, _ = _place()
        me = 4 * x + 2 * y + c
        out = []
        for a in range(n):
            for mask in range(1, 8):
                peer = (x ^ (mask >> 2), y ^ ((mask >> 1) & 1), c ^ (mask & 1))
                out.append(_rcopy(rin[a], rout[a].at[me], send, recv, 8 * a + mask, peer))
        own = [pltpu.make_async_copy(rin[a], rout[a].at[me], send.at[8 * a]) for a in range(n)]
        return out, own

    def start(rin, rout, send, recv):
        remote, own = copies(rin, rout, send, recv)
        for cp in remote + own:
            cp.start()

    def finish(rin, rout, send, recv):
        remote, own = copies(rin, rout, send, recv)
        for cp in remote + own:
            cp.wait()

    outs = [jax.ShapeDtypeStruct((8,) + p.shape, p.dtype) for p in packs]
    return _Rider(packs, outs, {}, 8 * n, start, finish)


def _sum_small(name, slots):
    def body(*refs):
        n = len(refs) // 2
        for r_ref, s_ref in zip(refs[:n], refs[n:]):
            tot = r_ref[0]
            for d in range(1, 8):
                tot = tot + r_ref[d]
            s_ref[...] = tot

    return pl.pallas_call(
        body, name=name, in_specs=[VMEM_FULL] * len(slots), out_specs=[VMEM_FULL] * len(slots),
        out_shape=[jax.ShapeDtypeStruct(s.shape[1:], s.dtype) for s in slots],
        compiler_params=pltpu.CompilerParams(vmem_limit_bytes=VMEM_LIMIT),
    )(*slots)


def _adamw(w, g, m, v):
    m2 = ADAM_B1 * m + (1.0 - ADAM_B1) * g
    v2 = ADAM_B2 * v + (1.0 - ADAM_B2) * (g * g)
    m_hat = m2 / (1.0 - ADAM_B1 ** ADAM_STEP)
    v_hat = v2 / (1.0 - ADAM_B2 ** ADAM_STEP)
    delta = -ADAM_LR * (m_hat / (jnp.sqrt(v_hat) + ADAM_EPS) + ADAM_WD * w)
    return delta, m2, v2


def _adam_big(name, w, g, m, v):
    r, c = w.shape
    tr = 128 if r % 128 == 0 else r

    def body(w_ref, g_ref, m_ref, v_ref, go_ref, d_ref, m2_ref, v2_ref):
        g = g_ref[...]
        go_ref[...] = g
        d_ref[...], m2_ref[...], v2_ref[...] = _adamw(w_ref[...], g, m_ref[...], v_ref[...])

    blk = pl.BlockSpec((tr, c), lambda i: (i, 0))
    return _call(
        body, (w, g, m, v), name=name, grid=(r // tr,), in_specs=[blk] * 4, out_specs=[blk] * 4,
        out_shape=[jax.ShapeDtypeStruct((r, c), F32)] * 4)


def _adam_small(s1, s2, s3, cw_g, fw_g, lb_logits, triples, layout, w):
    n = len(triples)

    def body(*refs):
        s1_ref, s2_ref, s3_ref, cw_ref, fw_ref, lbl_ref = refs[:6]
        prm = refs[6:6 + 3 * n]
        outs = refs[6 + 3 * n:]
        for p, lay in enumerate(layout):
            w_ref, m_ref, v_ref = prm[3 * p:3 * p + 3]
            g_ref, d_ref, m2_ref, v2_ref = outs[4 * p:4 * p + 4]
            if lay[0] in ("wide", "late"):
                _, row, r, pieces = lay
                src = s1_ref if lay[0] == "wide" else s3_ref
                for rr in range(r):
                    for mm in range(pieces):
                        g_ref[rr:rr + 1, mm * w:(mm + 1) * w] = src[row:row + 1, :]
                        row += 1
            elif lay[0] == "ffn":
                _, row, r = lay
                g_ref[...] = s2_ref[row:row + r, :]
            elif lay[0] == "cw":
                g_ref[...] = cw_ref[0:g_ref.shape[0], :]
            elif lay[0] == "fw":
                g_ref[...] = fw_ref[0:g_ref.shape[0], :]
            else:
                s0 = _lower_bound(lbl_ref)
                d0 = s1_ref[lay[1]:lay[1] + 1, :] * s0 * (1.0 - s0)
                g_ref[0:1, :] = d0
                g_ref[1:2, :] = -d0
            d_ref[...], m2_ref[...], v2_ref[...] = _adamw(w_ref[...], g_ref[...], m_ref[...], v_ref[...])

    flat = [a for tr in triples for a in tr]
    shapes = []
    for tr in triples:
        shapes.extend([jax.ShapeDtypeStruct(tr[0].shape, F32)] * 4)
    return pl.pallas_call(
        body, name="adam_small", in_specs=[VMEM_FULL] * (6 + 3 * n), out_specs=[VMEM_FULL] * (4 * n),
        out_shape=shapes, compiler_params=pltpu.CompilerParams(vmem_limit_bytes=VMEM_LIMIT),
    )(s1, s2, s3, cw_g, fw_g, lb_logits, *flat)


def _row_tile(t):
    return 512 if t % 512 == 0 and t >= 2048 else 128


def kernel(x, emb_ln_g, emb_ln_b, w_in, conv_w, conv_b, conv_norm_g, conv_norm_b, lb_logits, hgrn_norm_g, w_out, ln1_g, ln1_b, w_ffn_up, ffn_conv_w, ffn_conv_b, w_ffn_down, ln2_g, ln2_b, loss_target, m_emb_ln_g, m_emb_ln_b, m_w_in, m_conv_w, m_conv_b, m_conv_norm_g, m_conv_norm_b, m_lb_logits, m_hgrn_norm_g, m_w_out, m_ln1_g, m_ln1_b, m_w_ffn_up, m_ffn_conv_w, m_ffn_conv_b, m_w_ffn_down, m_ln2_g, m_ln2_b, v_emb_ln_g, v_emb_ln_b, v_w_in, v_conv_w, v_conv_b, v_conv_norm_g, v_conv_norm_b, v_lb_logits, v_hgrn_norm_g, v_w_out, v_ln1_g, v_ln1_b, v_w_ffn_up, v_ffn_conv_w, v_ffn_conv_b, v_w_ffn_down, v_ln2_g, v_ln2_b):
    depth = w_in.shape[0]
    assert depth == 1 and x.shape[0] == 1
    alpha = (2.0 * depth) ** 0.25
    t, d = x.shape[1], x.shape[2]
    w = d // 2
    dff = ffn_conv_b.shape[1]
    kc = conv_w.shape[1]
    assert w % (2 * LANE) == 0 and dff % (4 * LANE) == 0 and t % 128 == 0
    tm = _row_tile(t)
    tm2 = tm // 2
    tmm = 1024 if t % 1024 == 0 and t >= 2048 else tm
    cb = 2 * LANE
    cbf = 4 * LANE
    tb = tm
    nh = w // LANE
    hpb = 4 if nh % 4 == 0 else 2

    xi = lax.axis_index("x")
    yi = lax.axis_index("y")
    ci = lax.axis_index("c")
    chip = 2 * xi + yi
    c_idx = jnp.reshape(ci, (1,)).astype(jnp.int32)
    chip_idx = jnp.reshape(chip, (1,)).astype(jnp.int32)
    place_idx = jnp.stack([chip, ci]).astype(jnp.int32)

    x2 = x[0]
    tgt = loss_target[0]
    g0, b0 = emb_ln_g.reshape(1, d), emb_ln_b.reshape(1, d)
    w_in2, w_out2, w_up2, w_dn2 = w_in[0], w_out[0], w_ffn_up[0], w_ffn_down[0]
    cw2, fw2 = conv_w[0], ffn_conv_w[0]

    b_in = _place_shard(w_in2, "place_w_in", chip_idx, BF16)
    b_out = _place_shard(w_out2, "place_w_out", chip_idx, BF16)
    b_up = _place_shard(w_up2, "place_w_up", chip_idx, BF16)
    b_dn = _place_shard(w_dn2, "place_w_down", chip_idx, BF16)
    b_cw = _place_shard(_pad_rows(cw2), "place_conv_w", chip_idx, F32)
    b_fw = _place_shard(_pad_rows(fw2), "place_ffn_conv_w", chip_idx, F32)
    h0b, (b_in,) = _ln0(x2, g0, b0, tm, rider=_ride_gather_ici([b_in], [(0, 1, 8)]))
    first = _run("gather_first_ici", _ride_gather_ici([b_in, b_cw, b_fw], [(1, 8, 8), None, None]))
    w_in3, cw_full3, fw_full3 = _run("gather_first_d2d", _ride_gather_d2d(first))
    cw_full = _unshard_cols(cw_full3)[:kc]
    fw_full = _unshard_cols(fw_full3)[:fw2.shape[0]]

    p3, (b_out, b_up) = _proj("in_proj", h0b, w_in3, 6, 2 * tmm if t % (2 * tmm) == 0 else tmm, w // 2,
                              rider=_ride_gather_ici([b_out, b_up], [None, (0, 1, 4)]))
    (cat, u1), (w_out3, b_up) = _conv_fwd(
        p3, cw_full, conv_b, conv_norm_g, conv_norm_b, tm2, cb,
        rider=_ride_both(_ride_gather_d2d([b_out]), _ride_gather_ici([b_up], [(1, 2, 4)])))
    w_out_full = w_out3.reshape(d, d)
    (cat, o_pre, states), got = _hgrn_fwd(p3, lb_logits, hgrn_norm_g, cat, tb, hpb,
                                          rider=_ride_gather_ici([b_up], [(2, 4, 4)]))
    (xhat1, h1b, rstd1), (w_up3,) = _mix_ln1(cat, w_out_full, x2, g0, b0, ln1_g, ln1_b, alpha, tm2,
                                             rider=_ride_gather_d2d(got))
    hh3, got = _proj("ffn_up", h1b, w_up3, 2, tmm, dff // 4, rider=_ride_gather_ici([b_dn]))
    act, (w_dn3,) = _ffn_act_fwd(hh3, fw_full, ffn_conv_b, tm, cbf, rider=_ride_gather_d2d(got))
    ks = dff // N_CHIPS
    ffn = _wgrad("ffn_down", act, w_dn3, (t, d), (t // tmm, 1, N_CHIPS),
                 pl.BlockSpec((tmm, ks), lambda i, j, k: (i, k)),
                 pl.BlockSpec((None, ks, d), lambda i, j, k: (k, 0, 0)),
                 pl.BlockSpec((tmm, d), lambda i, j, k: (i, 0)), dot=_dot)
    dz2, dz2b, dg2, db2, loss_row = _ln2_loss(ffn, xhat1, tgt, ln1_g, ln1_b, ln2_g, ln2_b, alpha, tm2)

    dact = _proj_t("ffn_down_t", dz2b, w_dn3.reshape(dff, d), tmm, ks)
    dhh3, dfw, dfb = _ffn_act_bwd(dact, hh3, fw_full, ffn_conv_b, tm, cbf)
    tt = 2 * tmm if t % (2 * tmm) == 0 else tmm
    d_w_dn = _wgrad("wgrad_down", act, dz2b, (N_CHIPS, ks, d), (N_CHIPS, 2, t // tt),
                    pl.BlockSpec((tt, ks), lambda s, j, k: (k, s)),
                    pl.BlockSpec((tt, d // 2), lambda s, j, k: (k, j)),
                    pl.BlockSpec((None, ks, d // 2), lambda s, j, k: (s, 0, j)))
    wu = 2 * dff // N_CHIPS
    tnu = wu // 2
    per_sec_u = dff // tnu
    pre1, (arr_dn,) = _wgrad(
        "up_t", dhh3, w_up3, (t, d), (t // tmm, 1, 2 * N_CHIPS),
        pl.BlockSpec((None, tmm, tnu), lambda i, j, k: (k // per_sec_u, i, k % per_sec_u)),
        pl.BlockSpec((None, d, tnu), lambda i, j, k: (k // 2, 0, k % 2)),
        pl.BlockSpec((tmm, d), lambda i, j, k: (i, 0)), dot=_dot_nt, rider=_ride_swap([d_w_dn]))
    dz1, dz1b, dg1, db1 = _ln1_bwd(pre1, dz2, xhat1, rstd1, ln1_g, alpha, tm2)
    part_dn = _add_halves("add_halves_w_down", d_w_dn, arr_dn, c_idx)
    d_w_up, (land_dn,) = _wgrad(
        "wgrad_up", h1b, dhh3, (N_CHIPS, d, wu), (N_CHIPS, 2, 2, t // tt),
        pl.BlockSpec((tt, d // 2), lambda s, r, j, k: (k, r)),
        pl.BlockSpec((None, tt, tnu), lambda s, r, j, k: ((2 * s + j) // per_sec_u, k, (2 * s + j) % per_sec_u)),
        pl.BlockSpec((None, d // 2, tnu), lambda s, r, j, k: (s, r, j)), rider=_ride_send_partials([part_dn]))
    dcat = _proj_t("out_proj_t", dz1b, w_out_full, tmm, d // 2)
    d_w_out = _wgrad("wgrad_out", cat, dz1b, (d, d), (2, 2, t // tt),
                     pl.BlockSpec((tt, d // 2), lambda r, j, k: (k, r)),
                     pl.BlockSpec((tt, d // 2), lambda r, j, k: (k, j)),
                     pl.BlockSpec((d // 2, d // 2), lambda r, j, k: (r, j))).reshape(N_CHIPS, d // N_CHIPS, d)
    du1, dcng, dcnb = _conv_norm_bwd(dcat, u1, conv_norm_g, conv_norm_b, tm)
    (dp3, dcw, dcb), (arr_up, arr_out) = _conv_bwd(du1, p3, cw_full, tm2, cb, rider=_ride_swap([d_w_up, d_w_out]))
    part_up = _add_halves("add_halves_w_up", d_w_up, arr_up, c_idx)
    part_out = _add_halves("add_halves_w_out", d_w_out, arr_out, c_idx)
    (dp3, dlb, dhg), (land_up, land_out) = _hgrn_bwd(
        p3, lb_logits, hgrn_norm_g, o_pre, states, dcat, dp3, tb, hpb,
        rider=_ride_send_partials([part_up, part_out], [(0, 3, 4), None]))
    kpad = dcw.shape[0]
    wide = [(dcw, kpad, 1), (dg1, 1, 2), (db1, 1, 2), (dg2, 1, 2), (db2, 1, 2),
            (dcb, 1, 1), (dcng, 1, 1), (dcnb, 1, 1), (dlb, 1, 1), (dhg, 1, 1)]
    n_wide = -(-sum(r * m for _, r, m in wide) // SUBLANE) * SUBLANE
    packs = _pack_small("pack_small", wide, [dfw, dfb], w, dff, n_wide, 2 * SUBLANE)
    d_w_in, (land_up, slots1, slots2) = _wgrad_in(
        h0b, dp3, N_CHIPS, tt,
        rider=_ride_both(_ride_send_partials([part_up], [(3, 4, 4)], into=[land_up]), _ride_exchange8(packs)))
    s1, s2 = _sum_small("sum_small", [slots1, slots2])
    (arr_in,) = _run("swap_w_in", _ride_swap([d_w_in]))
    part_in = _add_halves("add_halves_w_in", d_w_in, arr_in, c_idx)
    h_out, h_up, h_dn = [
        _sum_partials("sum_partials_" + nm, p, a, place_idx)
        for nm, p, a in (("w_out", part_out, land_out), ("w_up", part_up, land_up), ("w_down", part_dn, land_dn))]
    pre0, (land_in, g_w_out, g_w_up, g_w_dn) = _in_t(
        dp3, w_in3, tmm, rider=_ride_both(_ride_send_partials([part_in]), _ride_join([h_out, h_up, h_dn])))
    dx, dg0, db0 = _ln0_bwd(pre0, dz1, x2, g0, alpha, tm2)
    h_in = _sum_partials("sum_partials_w_in", part_in, land_in, place_idx)
    (g_w_in,) = _run("join_w_in", _ride_join([h_in]))

    late = _pack_small("pack_late", [(dg0, 1, 2), (db0, 1, 2)], [], w, dff, SUBLANE, 0)
    (s3,) = _sum_small("sum_late", _run("exchange_late", _ride_exchange8(late)))
    cw_g = lax.dynamic_slice_in_dim(s1[0:kpad], chip * (w // N_CHIPS), w // N_CHIPS, axis=1)
    fw_g = lax.dynamic_slice_in_dim(s2[0:SUBLANE], chip * (dff // N_CHIPS), dff // N_CHIPS, axis=1)

    small = [
        (g0, m_emb_ln_g.reshape(1, d), v_emb_ln_g.reshape(1, d)), (b0, m_emb_ln_b.reshape(1, d), v_emb_ln_b.reshape(1, d)),
        (cw2, m_conv_w[0], v_conv_w[0]), (conv_b, m_conv_b, v_conv_b),
        (conv_norm_g, m_conv_norm_g, v_conv_norm_g), (conv_norm_b, m_conv_norm_b, v_conv_norm_b),
        (lb_logits, m_lb_logits, v_lb_logits), (hgrn_norm_g, m_hgrn_norm_g, v_hgrn_norm_g),
        (ln1_g, m_ln1_g, v_ln1_g), (ln1_b, m_ln1_b, v_ln1_b),
        (fw2, m_ffn_conv_w[0], v_ffn_conv_w[0]), (ffn_conv_b, m_ffn_conv_b, v_ffn_conv_b),
        (ln2_g, m_ln2_g, v_ln2_g), (ln2_b, m_ln2_b, v_ln2_b),
    ]
    r0 = kpad
    layout = [("late", 0, 1, 2), ("late", 2, 1, 2), ("cw",), ("wide", r0 + 8, 1, 1), ("wide", r0 + 9, 1, 1),
              ("wide", r0 + 10, 1, 1), ("lb", r0 + 11), ("wide", r0 + 12, 1, 1), ("wide", r0, 1, 2),
              ("wide", r0 + 2, 1, 2), ("fw",), ("ffn", SUBLANE, 1), ("wide", r0 + 4, 1, 2), ("wide", r0 + 6, 1, 2)]
    so = _adam_small(s1, s2, s3, cw_g, fw_g, lb_logits, small, layout, w)
    sm = {nm: so[4 * i:4 * i + 4] for i, nm in enumerate(
        ["emb_ln_g", "emb_ln_b", "conv_w", "conv_b", "conv_norm_g", "conv_norm_b", "lb_logits", "hgrn_norm_g",
         "ln1_g", "ln1_b", "ffn_conv_w", "ffn_conv_b", "ln2_g", "ln2_b"])}
    bigs = {}
    for nm, wt, g, m, v in (("w_in", w_in2, g_w_in, m_w_in[0], v_w_in[0]), ("w_out", w_out2, g_w_out, m_w_out[0], v_w_out[0]),
                            ("w_ffn_up", w_up2, g_w_up, m_w_ffn_up[0], v_w_ffn_up[0]),
                            ("w_ffn_down", w_dn2, g_w_dn, m_w_ffn_down[0], v_w_ffn_down[0])):
        bigs[nm] = tuple(_adam_big("adam_" + nm, wt, g, m, v))

    loss = lax.psum(loss_row[0, 0], ("x", "y", "c"))

    order = ["emb_ln_g", "emb_ln_b", "w_in", "conv_w", "conv_b", "conv_norm_g", "conv_norm_b", "lb_logits",
             "hgrn_norm_g", "w_out", "ln1_g", "ln1_b", "w_ffn_up", "ffn_conv_w", "ffn_conv_b", "w_ffn_down",
             "ln2_g", "ln2_b"]
    shapes = dict(emb_ln_g=emb_ln_g.shape, emb_ln_b=emb_ln_b.shape, w_in=w_in.shape, conv_w=conv_w.shape,
                  w_out=w_out.shape, w_ffn_up=w_ffn_up.shape, ffn_conv_w=ffn_conv_w.shape, w_ffn_down=w_ffn_down.shape)
    outs = [loss, dx.reshape(x.shape)]
    for which in range(4):
        for nm in order:
            a = bigs[nm][which] if nm in bigs else sm[nm][which]
            outs.append(a.reshape(shapes[nm]) if nm in shapes else a)
    return tuple(outs)


def _pad_rows(a):
    k = a.shape[0]
    kp = -(-k // 16) * 16
    return jnp.pad(a, ((0, kp - k), (0, 0)))


def _unshard_cols(a3):
    s, k, c = a3.shape
    return jnp.transpose(a3, (1, 0, 2)).reshape(k, s * c)
```

```python
import functools

import jax
import jax.numpy as jnp
from jax import lax
from jax.experimental import pallas as pl
from jax.experimental.pallas import tpu as pltpu

F32 = jnp.float32
BF16 = jnp.bfloat16

LN_EPS = 1e-5
RMS_EPS = 1e-6
LANE = 128
SUBLANE = 8
CHUNK = 64
SUB = 8
HALO = 32
FHALO = 8
ROWS = 64
N_CHIPS = 4
VMEM_LIMIT = 56 << 20
NEG_BIG = -1e30

ADAM_LR = 0.001
ADAM_B1 = 0.9
ADAM_B2 = 0.999
ADAM_EPS = 1e-08
ADAM_WD = 0.01
ADAM_STEP = 10

MESH = pl.DeviceIdType.MESH
HBM = pl.BlockSpec(memory_space=pl.ANY)
VMEM_FULL = pl.BlockSpec(memory_space=pltpu.VMEM)


def _params(*sem):
    return pltpu.CompilerParams(dimension_semantics=sem, vmem_limit_bytes=VMEM_LIMIT)


class _Rider:
    def __init__(self, ins, outs, aliases, n_sems, start, finish):
        self.ins, self.outs, self.aliases = list(ins), list(outs), dict(aliases)
        self.n_sems, self.start, self.finish = n_sems, start, finish


def _call(body, args, *, name, grid, in_specs, out_specs, out_shape, scratch_shapes=(), aliases=None, rider=None):
    many = isinstance(out_shape, (list, tuple))
    shapes = list(out_shape) if many else [out_shape]
    ospecs = list(out_specs) if many else [out_specs]
    aliases = dict(aliases or {})
    sem = ("arbitrary",) * len(grid)
    if rider is None:
        res = pl.pallas_call(
            body, name=name, grid=grid, in_specs=list(in_specs), out_specs=ospecs, out_shape=shapes,
            scratch_shapes=list(scratch_shapes), input_output_aliases=aliases, compiler_params=_params(*sem))(*args)
        return res if many else res[0]
    n_in, n_out, n_scr = len(args), len(shapes), len(scratch_shapes)
    nri, nro = len(rider.ins), len(rider.outs)

    def wrapped(*refs):
        ins, rin = refs[:n_in], refs[n_in:n_in + nri]
        o0 = n_in + nri
        outs, rout = refs[o0:o0 + n_out], refs[o0 + n_out:o0 + n_out + nro]
        s0 = o0 + n_out + nro
        scr, (send, recv) = refs[s0:s0 + n_scr], refs[s0 + n_scr:]
        ids = [pl.program_id(a) for a in range(len(grid))]
        first = functools.reduce(jnp.logical_and, [i == 0 for i in ids])
        last = functools.reduce(jnp.logical_and, [i == g - 1 for i, g in zip(ids, grid)])

        @pl.when(first)
        def _():
            rider.start(rin, rout, send, recv)

        body(*ins, *outs, *scr)

        @pl.when(last)
        def _():
            rider.finish(rin, rout, send, recv)

    for ri, ro in rider.aliases.items():
        aliases[n_in + ri] = n_out + ro
    res = pl.pallas_call(
        wrapped, name=name, grid=grid, in_specs=list(in_specs) + [HBM] * nri, out_specs=ospecs + [HBM] * nro,
        out_shape=shapes + rider.outs,
        scratch_shapes=list(scratch_shapes) + [pltpu.SemaphoreType.DMA((rider.n_sems,)),
                                               pltpu.SemaphoreType.DMA((rider.n_sems,))],
        input_output_aliases=aliases, compiler_params=_params(*sem))(*args, *rider.ins)
    main, extra = res[:n_out], list(res[n_out:])
    return (list(main) if many else main[0]), extra


def _div_tile(n, mult, cap):
    best = n
    for t in range(mult, min(n, cap) + 1, mult):
        if n % t == 0:
            best = t
    return best


def _sigmoid(x):
    return 1.0 / (1.0 + jnp.exp(-x))


def _ln_stats(x):
    mu = jnp.mean(x, axis=-1, keepdims=True)
    xc = x - mu
    var = jnp.mean(xc * xc, axis=-1, keepdims=True)
    rstd = lax.rsqrt(var + LN_EPS)
    return xc * rstd, rstd


def _ln_bwd(dy, xhat, rstd, g):
    dyg = dy * g
    m1 = jnp.mean(dyg, axis=-1, keepdims=True)
    m2 = jnp.mean(dyg * xhat, axis=-1, keepdims=True)
    return rstd * (dyg - m1 - xhat * m2)


def _dot_nt(a, b):
    return lax.dot_general(a, b, (((1,), (1,)), ((), ())), preferred_element_type=F32)


def _dot_tn(a, b):
    return lax.dot_general(a, b, (((0,), (0,)), ((), ())), preferred_element_type=F32)


def _dot(a, b):
    return jnp.dot(a, b, preferred_element_type=F32)


def _dot3(m, x):
    mb = m.astype(BF16)
    x1 = x.astype(BF16)
    r1 = x - x1.astype(F32)
    x2 = r1.astype(BF16)
    x3 = (r1 - x2.astype(F32)).astype(BF16)
    return _dot(mb, x1) + _dot(mb, x2) + _dot(mb, x3)


def _place_shard(x, name, chip_idx, dtype):
    r, c = x.shape
    tr = _div_tile(r, 16, 512)

    def body(s_ref, x_ref, o_ref):
        del s_ref
        o_ref[...] = x_ref[...].astype(dtype)

    return pl.pallas_call(
        body, name=name,
        grid_spec=pltpu.PrefetchScalarGridSpec(
            num_scalar_prefetch=1, grid=(r // tr,),
            in_specs=[pl.BlockSpec((tr, c), lambda i, s: (i, 0))],
            out_specs=pl.BlockSpec((None, tr, c), lambda i, s: (s[0], i, 0))),
        out_shape=jax.ShapeDtypeStruct((N_CHIPS, r, c), dtype),
        compiler_params=_params("parallel"),
    )(chip_idx, x)


def _ln0(x, g, b, tm, rider=None):
    t, d = x.shape

    def body(x_ref, g_ref, b_ref, o_ref):
        xh, _ = _ln_stats(x_ref[...])
        o_ref[...] = (xh * g_ref[...] + b_ref[...]).astype(BF16)

    row = pl.BlockSpec((1, d), lambda i: (0, 0))
    return _call(
        body, (x, g, b), name="ln0", grid=(t // tm,),
        in_specs=[pl.BlockSpec((tm, d), lambda i: (i, 0)), row, row],
        out_specs=pl.BlockSpec((tm, d), lambda i: (i, 0)),
        out_shape=jax.ShapeDtypeStruct((t, d), BF16), rider=rider)


def _proj(name, a, w3, n_sec, tm, tn, rider=None):
    m, k = a.shape
    s, _, ws = w3.shape
    sec_w = s * ws // n_sec
    nj = ws // tn
    per_sec = sec_w // tn

    def body(a_ref, w_ref, o_ref):
        o_ref[...] = _dot(a_ref[...], w_ref[...])

    return _call(
        body, (a, w3), name=name, grid=(s * nj, m // tm),
        in_specs=[pl.BlockSpec((tm, k), lambda j, i: (i, 0)),
                  pl.BlockSpec((None, k, tn), lambda j, i: (j // nj, 0, j % nj))],
        out_specs=pl.BlockSpec((None, tm, tn), lambda j, i: (j // per_sec, i, j % per_sec)),
        out_shape=jax.ShapeDtypeStruct((n_sec, m, sec_w), F32), rider=rider)


def _proj_t(name, a, w, tm, tn, rider=None):
    m, k = a.shape
    n = w.shape[0]

    def body(a_ref, w_ref, o_ref):
        o_ref[...] = _dot_nt(a_ref[...], w_ref[...])

    return _call(
        body, (a, w), name=name, grid=(n // tn, m // tm),
        in_specs=[pl.BlockSpec((tm, k), lambda j, i: (i, 0)),
                  pl.BlockSpec((tn, k), lambda j, i: (j, 0))],
        out_specs=pl.BlockSpec((tm, tn), lambda j, i: (i, j)),
        out_shape=jax.ShapeDtypeStruct((m, n), F32), rider=rider)


def _wgrad(name, a, b, out_shape, grid, a_spec, b_spec, o_spec, rider=None, dot=_dot_tn):
    nt = len(grid) - 1

    def body(a_ref, b_ref, o_ref):
        t = pl.program_id(nt)
        prod = dot(a_ref[...], b_ref[...])

        @pl.when(t == 0)
        def _():
            o_ref[...] = prod

        @pl.when(t > 0)
        def _():
            o_ref[...] += prod

    return _call(
        body, (a, b), name=name, grid=grid, in_specs=[a_spec, b_spec], out_specs=o_spec,
        out_shape=jax.ShapeDtypeStruct(out_shape, F32), rider=rider)


def _wgrad_in(h0b, dp3, n_shards, tt, rider=None):
    t, d = h0b.shape
    n_sec, _, sec_w = dp3.shape
    ws = n_sec * sec_w // n_shards
    tn = sec_w // 2
    nq = ws // tn
    tr = d // 2

    def body(*refs):
        a_ref, b_refs, o_ref = refs[0], refs[1:1 + nq], refs[1 + nq]
        k = pl.program_id(2)
        a = a_ref[...]
        prods = [_dot_tn(a, b_ref[...]) for b_ref in b_refs]

        @pl.when(k == 0)
        def _():
            for q in range(nq):
                o_ref[:, q * tn:(q + 1) * tn] = prods[q]

        @pl.when(k > 0)
        def _():
            for q in range(nq):
                o_ref[:, q * tn:(q + 1) * tn] += prods[q]

    def b_spec(q):
        return pl.BlockSpec((None, tt, tn), lambda s, r, k: ((nq * s + q) // 2, k, (nq * s + q) % 2))

    return _call(
        body, (h0b,) + (dp3,) * nq, name="wgrad_in", grid=(n_shards, d // tr, t // tt),
        in_specs=[pl.BlockSpec((tt, tr), lambda s, r, k: (k, r))] + [b_spec(q) for q in range(nq)],
        out_specs=pl.BlockSpec((None, tr, ws), lambda s, r, k: (s, r, 0)),
        out_shape=jax.ShapeDtypeStruct((n_shards, d, ws), F32), rider=rider)


def _in_t(dp3, w_in3, tm, rider=None):
    _, t, sec_w = dp3.shape
    s, d, ws = w_in3.shape
    tk = sec_w // 2
    nq = ws // tk

    def body(*refs):
        a_refs, w_ref, o_ref = refs[:nq], refs[nq], refs[nq + 1]
        k = pl.program_id(1)
        prod = _dot_nt(a_refs[0][...], w_ref[:, 0:tk])
        for q in range(1, nq):
            prod = prod + _dot_nt(a_refs[q][...], w_ref[:, q * tk:(q + 1) * tk])

        @pl.when(k == 0)
        def _():
            o_ref[...] = prod

        @pl.when(k > 0)
        def _():
            o_ref[...] += prod

    def a_spec(q):
        return pl.BlockSpec((None, tm, tk), lambda i, k: ((nq * k + q) // 2, i, (nq * k + q) % 2))

    return _call(
        body, (dp3,) * nq + (w_in3,), name="in_t", grid=(t // tm, s),
        in_specs=[a_spec(q) for q in range(nq)] + [pl.BlockSpec((None, d, ws), lambda i, k: (k, 0, 0))],
        out_specs=pl.BlockSpec((tm, d), lambda i, k: (i, 0)),
        out_shape=jax.ShapeDtypeStruct((t, d), F32), rider=rider)


def _mix_ln1(cat, w_out, x, g0, b0, g1, b1, alpha, tm, rider=None):
    t, d = x.shape

    def body(cat_ref, w_ref, x_ref, g0_ref, b0_ref, g1_ref, b1_ref, xh_ref, h1b_ref, rstd_ref):
        mix = _dot(cat_ref[...], w_ref[...])
        xh0, _ = _ln_stats(x_ref[...])
        z1 = alpha * (xh0 * g0_ref[...] + b0_ref[...]) + mix
        xh1, rstd1 = _ln_stats(z1)
        xh_ref[...] = xh1
        h1b_ref[...] = (xh1 * g1_ref[...] + b1_ref[...]).astype(BF16)
        rstd_ref[...] = rstd1

    row = pl.BlockSpec((1, d), lambda i: (0, 0))
    blk = pl.BlockSpec((tm, d), lambda i: (i, 0))
    return _call(
        body, (cat, w_out, x, g0, b0, g1, b1), name="mix_ln1", grid=(t // tm,),
        in_specs=[blk, pl.BlockSpec((d, d), lambda i: (0, 0)), blk, row, row, row, row],
        out_specs=[blk, blk, pl.BlockSpec((tm, 1), lambda i: (i, 0))],
        out_shape=[jax.ShapeDtypeStruct((t, d), F32), jax.ShapeDtypeStruct((t, d), BF16),
                   jax.ShapeDtypeStruct((t, 1), F32)], rider=rider)


def _ln2_loss(ffn, xhat1, tgt, g1, b1, g2, b2, alpha, tm):
    t, d = xhat1.shape
    ni = t // tm
    inv_d = 1.0 / d

    def body(ffn_ref, xh1_ref, tgt_ref, g1_ref, b1_ref, g2_ref, b2_ref,
             dz2_ref, dz2b_ref, dg2_ref, db2_ref, loss_ref, lrow):
        i = pl.program_id(0)
        h1 = xh1_ref[...] * g1_ref[...] + b1_ref[...]
        xh2, rstd2 = _ln_stats(alpha * h1 + ffn_ref[...])
        g2v = g2_ref[...]
        diff = xh2 * g2v + b2_ref[...] - tgt_ref[...]
        dh2 = diff * inv_d
        sq = jnp.sum(diff * diff, axis=0, keepdims=True)
        dg = jnp.sum(dh2 * xh2, axis=0, keepdims=True)
        db = jnp.sum(dh2, axis=0, keepdims=True)

        @pl.when(i == 0)
        def _():
            lrow[...] = sq
            dg2_ref[...] = dg
            db2_ref[...] = db

        @pl.when(i > 0)
        def _():
            lrow[...] += sq
            dg2_ref[...] += dg
            db2_ref[...] += db

        dz2 = _ln_bwd(dh2, xh2, rstd2, g2v)
        dz2_ref[...] = dz2
        dz2b_ref[...] = dz2.astype(BF16)

        @pl.when(i == ni - 1)
        def _():
            tot = jnp.sum(lrow[...], axis=-1, keepdims=True) * (0.5 * inv_d)
            loss_ref[...] = jnp.broadcast_to(tot, (1, LANE))

    row = pl.BlockSpec((1, d), lambda i: (0, 0))
    blk = pl.BlockSpec((tm, d), lambda i: (i, 0))
    return _call(
        body, (ffn, xhat1, tgt, g1, b1, g2, b2), name="ln2_loss", grid=(ni,),
        in_specs=[blk, blk, blk, row, row, row, row],
        out_specs=[blk, blk, row, row, pl.BlockSpec((1, LANE), lambda i: (0, 0))],
        out_shape=[jax.ShapeDtypeStruct((t, d), F32), jax.ShapeDtypeStruct((t, d), BF16),
                   jax.ShapeDtypeStruct((1, d), F32), jax.ShapeDtypeStruct((1, d), F32),
                   jax.ShapeDtypeStruct((1, LANE), F32)],
        scratch_shapes=[pltpu.VMEM((1, d), F32)])


def _ln1_bwd(pre, dz2, xhat1, rstd1, g1, alpha, tm):
    t, d = dz2.shape

    def body(pre_ref, dz2_ref, xh_ref, rstd_ref, g_ref, dz1_ref, dz1b_ref, dg_ref, db_ref):
        i = pl.program_id(0)
        dh1 = alpha * dz2_ref[...] + pre_ref[...]
        xh = xh_ref[...]
        dg = jnp.sum(dh1 * xh, axis=0, keepdims=True)
        db = jnp.sum(dh1, axis=0, keepdims=True)

        @pl.when(i == 0)
        def _():
            dg_ref[...] = dg
            db_ref[...] = db

        @pl.when(i > 0)
        def _():
            dg_ref[...] += dg
            db_ref[...] += db

        dz1 = _ln_bwd(dh1, xh, rstd_ref[...], g_ref[...])
        dz1_ref[...] = dz1
        dz1b_ref[...] = dz1.astype(BF16)

    row = pl.BlockSpec((1, d), lambda i: (0, 0))
    blk = pl.BlockSpec((tm, d), lambda i: (i, 0))
    return _call(
        body, (pre, dz2, xhat1, rstd1, g1), name="ln1_bwd", grid=(t // tm,),
        in_specs=[blk, blk, blk, pl.BlockSpec((tm, 1), lambda i: (i, 0)), row],
        out_specs=[blk, blk, row, row],
        out_shape=[jax.ShapeDtypeStruct((t, d), F32), jax.ShapeDtypeStruct((t, d), BF16),
                   jax.ShapeDtypeStruct((1, d), F32), jax.ShapeDtypeStruct((1, d), F32)])


def _ln0_bwd(pre, dz1, x, g0, alpha, tm):
    t, d = x.shape

    def body(pre_ref, dz1_ref, x_ref, g_ref, dx_ref, dg_ref, db_ref):
        i = pl.program_id(0)
        dh0 = alpha * dz1_ref[...] + pre_ref[...]
        xh, rstd = _ln_stats(x_ref[...])
        dg = jnp.sum(dh0 * xh, axis=0, keepdims=True)
        db = jnp.sum(dh0, axis=0, keepdims=True)

        @pl.when(i == 0)
        def _():
            dg_ref[...] = dg
            db_ref[...] = db

        @pl.when(i > 0)
        def _():
            dg_ref[...] += dg
            db_ref[...] += db

        dx_ref[...] = _ln_bwd(dh0, xh, rstd, g_ref[...])

    row = pl.BlockSpec((1, d), lambda i: (0, 0))
    blk = pl.BlockSpec((tm, d), lambda i: (i, 0))
    return _call(
        body, (pre, dz1, x, g0), name="ln0_bwd", grid=(t // tm,),
        in_specs=[blk, blk, blk, row], out_specs=[blk, row, row],
        out_shape=[jax.ShapeDtypeStruct((t, d), F32), jax.ShapeDtypeStruct((1, d), F32),
                   jax.ShapeDtypeStruct((1, d), F32)])


def _shift_copies(ext, shifted):
    n = shifted.shape[1]
    for p in range(1, SUBLANE):
        shifted[p - 1] = ext[pl.ds(p, n), :]


def _window(ext, shifted, start, rows):
    p = start % SUBLANE
    if p == 0:
        return ext[pl.ds(start, rows), :]
    return shifted[p - 1, pl.ds(start - p, rows), :]


def _conv_fwd(p3, conv_w, conv_b, cn_g, cn_b, tc, cb, rider=None):
    _, t, w = p3.shape
    kk = conv_w.shape[0]
    off = HALO - (kk - 1)
    hb = tc // HALO

    def body(a_ref, g_ref, ap_ref, gp_ref, w_ref, b_ref, ng_ref, nb_ref, cat_ref, u1_ref, ext, sh):
        i = pl.program_id(1)
        ext[pl.ds(HALO, tc), :] = a_ref[...] * _sigmoid(g_ref[...])
        prev = ap_ref[...] * _sigmoid(gp_ref[...])
        ext[pl.ds(0, HALO), :] = jnp.where(i > 0, prev, 0.0)
        _shift_copies(ext, sh)
        for r in range(tc // ROWS):
            acc = jnp.broadcast_to(b_ref[...], (ROWS, cb))
            for k in range(kk):
                acc = acc + w_ref[k:k + 1, :] * _window(ext, sh, r * ROWS + off + k, ROWS)
            u1_ref[pl.ds(r * ROWS, ROWS), :] = acc
            for g in range(cb // LANE):
                sl = slice(g * LANE, (g + 1) * LANE)
                xh, _ = _ln_stats(acc[:, sl])
                u2 = xh * ng_ref[:, sl] + nb_ref[:, sl]
                cat_ref[pl.ds(r * ROWS, ROWS), sl] = (u2 * _sigmoid(u2)).astype(BF16)

    cur = lambda sec: pl.BlockSpec((None, tc, cb), lambda j, i: (sec, i, j))
    prev = lambda sec: pl.BlockSpec((None, HALO, cb), lambda j, i: (sec, jnp.maximum(i * hb - 1, 0), j))
    row = pl.BlockSpec((1, cb), lambda j, i: (0, j))
    return _call(
        body, (p3, p3, p3, p3, conv_w, conv_b, cn_g, cn_b), name="conv_fwd", grid=(w // cb, t // tc),
        in_specs=[cur(0), cur(1), prev(0), prev(1), pl.BlockSpec((kk, cb), lambda j, i: (0, j)), row, row, row],
        out_specs=[pl.BlockSpec((tc, cb), lambda j, i: (i, j)), pl.BlockSpec((tc, cb), lambda j, i: (i, j))],
        out_shape=[jax.ShapeDtypeStruct((t, 2 * w), BF16), jax.ShapeDtypeStruct((t, w), F32)],
        scratch_shapes=[pltpu.VMEM((tc + HALO, cb), F32),
                        pltpu.VMEM((SUBLANE - 1, tc + HALO - SUBLANE, cb), F32)], rider=rider)


def _conv_norm_bwd(dcat, u1, cn_g, cn_b, tc):
    t, w = u1.shape

    def body(du_ref, u1_ref, ng_ref, nb_ref, du1_ref, dg_ref, db_ref):
        i = pl.program_id(0)
        for g in range(w // LANE):
            sl = slice(g * LANE, (g + 1) * LANE)
            ng = ng_ref[:, sl]
            xh, rstd = _ln_stats(u1_ref[:, sl])
            u2 = xh * ng + nb_ref[:, sl]
            sg = _sigmoid(u2)
            du2 = du_ref[:, sl] * (sg * (1.0 + u2 * (1.0 - sg)))
            dg = jnp.sum(du2 * xh, axis=0, keepdims=True)
            db = jnp.sum(du2, axis=0, keepdims=True)

            @pl.when(i == 0)
            def _():
                dg_ref[:, sl] = dg
                db_ref[:, sl] = db

            @pl.when(i > 0)
            def _():
                dg_ref[:, sl] += dg
                db_ref[:, sl] += db

            du1_ref[:, sl] = _ln_bwd(du2, xh, rstd, ng)

    row = pl.BlockSpec((1, w), lambda i: (0, 0))
    blk = pl.BlockSpec((tc, w), lambda i: (i, 0))
    return pl.pallas_call(
        body, name="conv_norm_bwd", grid=(t // tc,),
        in_specs=[blk, blk, row, row], out_specs=[blk, row, row],
        out_shape=[jax.ShapeDtypeStruct((t, w), F32), jax.ShapeDtypeStruct((1, w), F32),
                   jax.ShapeDtypeStruct((1, w), F32)],
        compiler_params=_params("arbitrary"),
    )(dcat, u1, cn_g, cn_b)


def _conv_bwd(du1, p3, conv_w, tc, cb, rider=None):
    n_sec, t, w = p3.shape
    kk = conv_w.shape[0]
    off = HALO - (kk - 1)
    hb = tc // HALO
    nt = t // tc
    kpad = -(-kk // SUBLANE) * SUBLANE

    def body(d_ref, dn_ref, a_ref, g_ref, ap_ref, gp_ref, w_ref, dp_ref, dw_ref, db_ref,
             extd, extu, shd, shu, wacc, bacc):
        i = pl.program_id(1)

        @pl.when(i == 0)
        def _():
            wacc[...] = jnp.zeros_like(wacc)
            bacc[...] = jnp.zeros_like(bacc)

        extd[pl.ds(0, tc), :] = d_ref[...]
        extd[pl.ds(tc, HALO), :] = jnp.where(i < nt - 1, dn_ref[...], 0.0)
        extu[pl.ds(HALO, tc), :] = a_ref[...] * _sigmoid(g_ref[...])
        extu[pl.ds(0, HALO), :] = jnp.where(i > 0, ap_ref[...] * _sigmoid(gp_ref[...]), 0.0)
        _shift_copies(extd, shd)
        _shift_copies(extu, shu)
        for r in range(tc // ROWS):
            rows = pl.ds(r * ROWS, ROWS)
            acc = jnp.zeros((ROWS, cb), F32)
            for k in range(kk):
                acc = acc + w_ref[k:k + 1, :] * _window(extd, shd, r * ROWS + (kk - 1) - k, ROWS)
            a = a_ref[rows, :]
            sg = _sigmoid(g_ref[rows, :])
            dp_ref[0, rows, :] = (acc * sg).astype(BF16)
            dp_ref[1, rows, :] = (acc * a * sg * (1.0 - sg)).astype(BF16)
            d = d_ref[rows, :]
            bacc[...] += jnp.sum(d.reshape(ROWS // SUBLANE, SUBLANE, cb), axis=0)
            for k in range(kk):
                prod = d * _window(extu, shu, r * ROWS + off + k, ROWS)
                wacc[k] += jnp.sum(prod.reshape(ROWS // SUBLANE, SUBLANE, cb), axis=0)

        @pl.when(i == nt - 1)
        def _():
            for k in range(kk):
                dw_ref[k:k + 1, :] = jnp.sum(wacc[k], axis=0, keepdims=True)
            if kpad > kk:
                dw_ref[kk:kpad, :] = jnp.zeros((kpad - kk, cb), F32)
            db_ref[...] = jnp.sum(bacc[...], axis=0, keepdims=True)

    cur = lambda sec: pl.BlockSpec((None, tc, cb), lambda j, i: (sec, i, j))
    prev = lambda sec: pl.BlockSpec((None, HALO, cb), lambda j, i: (sec, jnp.maximum(i * hb - 1, 0), j))
    return _call(
        body, (du1, du1, p3, p3, p3, p3, conv_w), name="conv_bwd", grid=(w // cb, nt),
        in_specs=[pl.BlockSpec((tc, cb), lambda j, i: (i, j)),
                  pl.BlockSpec((HALO, cb), lambda j, i: (jnp.minimum((i + 1) * hb, t // HALO - 1), j)),
                  cur(0), cur(1), prev(0), prev(1), pl.BlockSpec((kk, cb), lambda j, i: (0, j))],
        out_specs=[pl.BlockSpec((2, tc, cb), lambda j, i: (0, i, j)),
                   pl.BlockSpec((kpad, cb), lambda j, i: (0, j)),
                   pl.BlockSpec((1, cb), lambda j, i: (0, j))],
        out_shape=[jax.ShapeDtypeStruct((n_sec, t, w), BF16), jax.ShapeDtypeStruct((kpad, w), F32),
                   jax.ShapeDtypeStruct((1, w), F32)],
        scratch_shapes=[pltpu.VMEM((tc + HALO, cb), F32), pltpu.VMEM((tc + HALO, cb), F32),
                        pltpu.VMEM((SUBLANE - 1, tc + HALO - SUBLANE, cb), F32),
                        pltpu.VMEM((SUBLANE - 1, tc + HALO - SUBLANE, cb), F32),
                        pltpu.VMEM((kk, SUBLANE, cb), F32), pltpu.VMEM((SUBLANE, cb), F32)], rider=rider)


def _ffn_act_fwd(hh3, fw, fb, tc, cb, rider=None):
    _, t, dff = hh3.shape
    kk = fw.shape[0]
    off = FHALO - (kk - 1)
    hb = tc // FHALO

    def body(g_ref, v_ref, gp_ref, w_ref, b_ref, act_ref, ext):
        i = pl.program_id(1)
        ext[pl.ds(FHALO, tc), :] = g_ref[...]
        ext[pl.ds(0, FHALO), :] = jnp.where(i > 0, gp_ref[...], 0.0)
        for r in range(tc // ROWS):
            rows = pl.ds(r * ROWS, ROWS)
            gc = jnp.broadcast_to(b_ref[...], (ROWS, cb))
            for k in range(kk):
                gc = gc + w_ref[k:k + 1, :] * ext[pl.ds(r * ROWS + off + k, ROWS), :]
            act_ref[rows, :] = (gc * _sigmoid(gc) * v_ref[rows, :]).astype(BF16)

    return _call(
        body, (hh3, hh3, hh3, fw, fb), name="ffn_act_fwd", grid=(dff // cb, t // tc),
        in_specs=[pl.BlockSpec((None, tc, cb), lambda j, i: (0, i, j)),
                  pl.BlockSpec((None, tc, cb), lambda j, i: (1, i, j)),
                  pl.BlockSpec((None, FHALO, cb), lambda j, i: (0, jnp.maximum(i * hb - 1, 0), j)),
                  pl.BlockSpec((kk, cb), lambda j, i: (0, j)),
                  pl.BlockSpec((1, cb), lambda j, i: (0, j))],
        out_specs=pl.BlockSpec((tc, cb), lambda j, i: (i, j)),
        out_shape=jax.ShapeDtypeStruct((t, dff), BF16),
        scratch_shapes=[pltpu.VMEM((tc + FHALO, cb), F32)], rider=rider)


def _ffn_act_bwd(dact, hh3, fw, fb, tc, cb):
    _, t, dff = hh3.shape
    kk = fw.shape[0]
    off = FHALO - (kk - 1)
    hb = tc // FHALO
    nt = t // tc
    te = tc + FHALO
    shifts = sorted({(off + k) % SUBLANE for k in range(kk)} - {0})

    def body(da_ref, dan_ref, g_ref, gp_ref, gn_ref, v_ref, vn_ref, w_ref, b_ref,
             dhh_ref, dw_ref, db_ref, gext, gsh, dext, wacc, bacc):
        i = pl.program_id(1)

        @pl.when(i == 0)
        def _():
            wacc[...] = jnp.zeros_like(wacc)
            bacc[...] = jnp.zeros_like(bacc)

        gext[pl.ds(0, FHALO), :] = jnp.where(i > 0, gp_ref[...], 0.0)
        gext[pl.ds(FHALO, tc), :] = g_ref[...]
        gext[pl.ds(FHALO + tc, FHALO), :] = gn_ref[...]
        for s, p in enumerate(shifts):
            gsh[s] = gext[pl.ds(p, te), :]

        def gwin(k, r0, n):
            p = (off + k) % SUBLANE
            if p == 0:
                return gext[pl.ds(r0 + off + k, n), :]
            return gsh[shifts.index(p), pl.ds(r0 + off + k - p, n), :]

        def gate_grad(r0, n, da, v):
            gc = jnp.broadcast_to(b_ref[...], (n, cb))
            for k in range(kk):
                gc = gc + w_ref[k:k + 1, :] * gwin(k, r0, n)
            sg = _sigmoid(gc)
            return gc * sg, da * v * (sg * (1.0 + gc * (1.0 - sg)))

        for r in range(tc // ROWS):
            rows = pl.ds(r * ROWS, ROWS)
            da = da_ref[rows, :]
            silu, dgc = gate_grad(r * ROWS, ROWS, da, v_ref[rows, :])
            dext[rows, :] = dgc
            dhh_ref[1, rows, :] = (da * silu).astype(BF16)
        _, dgc_next = gate_grad(tc, FHALO, dan_ref[...], vn_ref[...])
        dext[pl.ds(tc, FHALO), :] = jnp.where(i < nt - 1, dgc_next, 0.0)
        for r in range(tc // ROWS):
            rows = pl.ds(r * ROWS, ROWS)
            dg = jnp.zeros((ROWS, cb), F32)
            for k in range(kk):
                dg = dg + w_ref[k:k + 1, :] * dext[pl.ds(r * ROWS + (kk - 1) - k, ROWS), :]
            dhh_ref[0, rows, :] = dg.astype(BF16)
            dgc = dext[rows, :]
            bacc[...] += jnp.sum(dgc.reshape(ROWS // SUBLANE, SUBLANE, cb), axis=0)
            for k in range(kk):
                prod = dgc * gwin(k, r * ROWS, ROWS)
                wacc[k] += jnp.sum(prod.reshape(ROWS // SUBLANE, SUBLANE, cb), axis=0)

        @pl.when(i == nt - 1)
        def _():
            for k in range(kk):
                dw_ref[k:k + 1, :] = jnp.sum(wacc[k], axis=0, keepdims=True)
            dw_ref[kk:SUBLANE, :] = jnp.zeros((SUBLANE - kk, cb), F32)
            db_ref[...] = jnp.sum(bacc[...], axis=0, keepdims=True)

    nxt = lambda i: jnp.minimum((i + 1) * hb, t // FHALO - 1)
    return pl.pallas_call(
        body, name="ffn_act_bwd", grid=(dff // cb, nt),
        in_specs=[pl.BlockSpec((tc, cb), lambda j, i: (i, j)),
                  pl.BlockSpec((FHALO, cb), lambda j, i: (nxt(i), j)),
                  pl.BlockSpec((None, tc, cb), lambda j, i: (0, i, j)),
                  pl.BlockSpec((None, FHALO, cb), lambda j, i: (0, jnp.maximum(i * hb - 1, 0), j)),
                  pl.BlockSpec((None, FHALO, cb), lambda j, i: (0, nxt(i), j)),
                  pl.BlockSpec((None, tc, cb), lambda j, i: (1, i, j)),
                  pl.BlockSpec((None, FHALO, cb), lambda j, i: (1, nxt(i), j)),
                  pl.BlockSpec((kk, cb), lambda j, i: (0, j)),
                  pl.BlockSpec((1, cb), lambda j, i: (0, j))],
        out_specs=[pl.BlockSpec((2, tc, cb), lambda j, i: (0, i, j)),
                   pl.BlockSpec((SUBLANE, cb), lambda j, i: (0, j)),
                   pl.BlockSpec((1, cb), lambda j, i: (0, j))],
        out_shape=[jax.ShapeDtypeStruct((2, t, dff), BF16), jax.ShapeDtypeStruct((SUBLANE, dff), F32),
                   jax.ShapeDtypeStruct((1, dff), F32)],
        scratch_shapes=[pltpu.VMEM((tc + 2 * FHALO, cb), F32), pltpu.VMEM((len(shifts), te, cb), F32),
                        pltpu.VMEM((te, cb), F32),
                        pltpu.VMEM((kk, SUBLANE, cb), F32), pltpu.VMEM((SUBLANE, cb), F32)],
        compiler_params=_params("parallel", "arbitrary"),
    )(dact, dact, hh3, hh3, hh3, hh3, hh3, fw, fb)


def _chunk_consts():
    r = lax.broadcasted_iota(jnp.int32, (CHUNK, CHUNK), 0)
    c = lax.broadcasted_iota(jnp.int32, (CHUNK, CHUNK), 1)
    return (c <= r).astype(F32)


def _roll8(x, d):
    return pltpu.roll(x.reshape(CHUNK // SUB, SUB, LANE), d % SUB, 1).reshape(CHUNK, LANE)


def _gate_terms(q, fpre, lb):
    sf = _sigmoid(fpre)
    fg = lb + (1.0 - lb) * sf
    sq = _sigmoid(q)
    return sf, fg, 1.0 - fg, sq, q * sq


def _decays(g, consts):
    b = _dot3(consts, g)
    nb = CHUNK // SUB
    ends = b.reshape(nb, SUB, LANE)[:, SUB - 1:SUB, :]
    re3 = jnp.broadcast_to(ends, (nb, SUB, LANE))
    rs3 = jnp.concatenate([jnp.zeros((1, SUB, LANE), F32), re3[:nb - 1]], axis=0)
    return b, rs3.reshape(CHUNK, LANE), re3.reshape(CHUNK, LANE), b[CHUNK - 1:CHUNK]


def _lower_bound(lb_ref):
    l0, l1 = lb_ref[0:1, :], lb_ref[1:2, :]
    mx = jnp.maximum(l0, l1)
    e0, e1 = jnp.exp(l0 - mx), jnp.exp(l1 - mx)
    return e0 / (e0 + e1)


BF16_ROWS = 16


def _scaled_keys(kt, rs, re, i):
    n = SUB * i
    scale = jnp.exp(rs[n:n + 1, :] - re[:n])
    live = kt[:n] * scale
    m = -(-n // BF16_ROWS) * BF16_ROWS
    if m > n:
        live = jnp.concatenate([live, jnp.zeros((m - n, LANE), F32)], axis=0)
    parts = [live.astype(BF16)]
    if m < CHUNK:
        parts.append(jnp.zeros((CHUNK - m, LANE), BF16))
    return jnp.concatenate(parts, axis=0), scale


def _only_block(x, i):
    w0 = (SUB * i // BF16_ROWS) * BF16_ROWS
    win = x[w0:w0 + BF16_ROWS]
    inblk = lax.broadcasted_iota(jnp.int32, (BF16_ROWS, 1), 0) // SUB == (SUB * i - w0) // SUB
    parts = [jnp.where(inblk, win, 0.0).astype(BF16)]
    if w0:
        parts.insert(0, jnp.zeros((w0, x.shape[1]), BF16))
    if w0 + BF16_ROWS < CHUNK:
        parts.append(jnp.zeros((CHUNK - w0 - BF16_ROWS, x.shape[1]), BF16))
    return jnp.concatenate(parts, axis=0)


def _hgrn_fwd(p3, lb_logits, hg, cat, tb, hpb, rider=None):
    _, t, w = p3.shape
    nh = w // LANE
    nc = tb // CHUNK
    assert nh % hpb == 0

    def body(q_ref, f_ref, v_ref, og_ref, lb_ref, hg_ref, cat_in, cat_ref, o_ref, st_ref, state):
        del cat_in
        consts = _chunk_consts()
        lb_all = _lower_bound(lb_ref)
        rowpos = lax.broadcasted_iota(jnp.int32, (CHUNK, 1), 0) % SUB

        @pl.when(pl.program_id(1) == 0)
        def _():
            state[...] = jnp.zeros_like(state)

        def chunk(c, carry):
            rows = pl.ds(pl.multiple_of(c * CHUNK, CHUNK), CHUNK)
            heads = range(hpb)
            sls = [slice(j * LANE, (j + 1) * LANE) for j in heads]
            v = [v_ref[rows, s] for s in sls]
            vb = [x.astype(BF16) for x in v]
            gates = [_gate_terms(q_ref[rows, s], f_ref[rows, s], lb_all[:, s]) for s in sls]
            fg = [g[1] for g in gates]
            kk = [g[2] for g in gates]
            qh = [g[4] for g in gates]
            dec = [_decays(jnp.log(x), consts) for x in fg]
            b = [x[0] for x in dec]
            rs = [x[1] for x in dec]
            re = [x[2] for x in dec]
            tot = [x[3] for x in dec]
            qt = [qh[j] * jnp.exp(b[j] - rs[j]) for j in heads]
            kt = [kk[j] * jnp.exp(re[j] - b[j]) for j in heads]
            st = [state[j] for j in heads]
            for j in heads:
                st_ref[j, c] = st[j]
            a = [jnp.zeros((CHUNK, CHUNK), F32) for _ in heads]
            for i in range(1, CHUNK // SUB):
                for j in heads:
                    kib, _ = _scaled_keys(kt[j], rs[j], re[j], i)
                    a[j] = a[j] + _dot_nt(_only_block(qt[j], i), kib)
            o = [_dot(a[j].astype(BF16), vb[j]) for j in heads]
            o = [o[j] + _dot_nt((qh[j] * jnp.exp(b[j])).astype(BF16), st[j].astype(BF16)) for j in heads]
            for j in heads:
                k_up = kk[j] * jnp.exp(tot[j] - b[j])
                state[j] = st[j] * jnp.exp(tot[j]) + _dot_tn(vb[j], k_up.astype(BF16))
            for j in heads:
                e, rf = None, fg[j]
                for d in range(SUB):
                    if d == 0:
                        vs, term = v[j], qh[j] * kk[j]
                    else:
                        e = rf if e is None else e * rf
                        rf = _roll8(fg[j], d)
                        vs = _roll8(v[j], d)
                        term = jnp.where(rowpos >= d, qh[j] * (1.0 - rf) * e, 0.0)
                    o[j] = o[j] + jnp.sum(term, axis=-1, keepdims=True) * vs
            for j in heads:
                og = og_ref[rows, sls[j]]
                o_ref[rows, sls[j]] = o[j]
                r = lax.rsqrt(jnp.mean(o[j] * o[j], axis=-1, keepdims=True) + RMS_EPS)
                cat_ref[rows, sls[j]] = (o[j] * r * hg_ref[:, sls[j]] * (og * _sigmoid(og))).astype(BF16)
            return carry

        lax.fori_loop(0, nc, chunk, 0)

    bw = hpb * LANE
    sec = lambda s: pl.BlockSpec((None, tb, bw), lambda h, i: (s, i, h))
    return _call(
        body, (p3, p3, p3, p3, lb_logits, hg, cat), name="hgrn_fwd", grid=(nh // hpb, t // tb),
        in_specs=[sec(2), sec(3), sec(4), sec(5),
                  pl.BlockSpec((2, bw), lambda h, i: (0, h)),
                  pl.BlockSpec((1, bw), lambda h, i: (0, h)), HBM],
        out_specs=[pl.BlockSpec((tb, bw), lambda h, i: (i, nh // hpb + h)),
                   pl.BlockSpec((tb, bw), lambda h, i: (i, h)),
                   pl.BlockSpec((hpb, nc, LANE, LANE), lambda h, i: (h, i, 0, 0))],
        out_shape=[jax.ShapeDtypeStruct(cat.shape, BF16), jax.ShapeDtypeStruct((t, w), F32),
                   jax.ShapeDtypeStruct((nh, t // CHUNK, LANE, LANE), F32)],
        scratch_shapes=[pltpu.VMEM((hpb, LANE, LANE), F32)], aliases={6: 0}, rider=rider)


def _hgrn_bwd(p3, lb_logits, hg, o_pre, states, dcat, dp3, tb, hpb, rider=None):
    n_sec, t, w = p3.shape
    nh = w // LANE
    assert nh % hpb == 0
    nc = tb // CHUNK
    nb = t // tb
    bw = hpb * LANE
    n_steps = (nh // hpb) * nb

    def body(q_ref, f_ref, v_ref, og_ref, lb_ref, hg_ref, o_ref, st_ref, dc_ref, dp_in,
             dp_ref, dlb_ref, dhg_ref, dstate, stash, lbacc, hgacc, osem):
        del dp_in
        h, i = pl.program_id(0), pl.program_id(1)
        step = h * nb + i
        slot = step % 2

        def out_copy(s, row_blk, lane_blk):
            dst = dp_ref.at[pl.ds(2, 4), pl.ds(row_blk * tb, tb), pl.ds(lane_blk * bw, bw)]
            return pltpu.make_async_copy(stash.at[s], dst, osem.at[s])

        @pl.when(step >= 2)
        def _():
            out_copy(slot, 0, 0).wait()

        def compute():
            consts = _chunk_consts()
            rr = lax.broadcasted_iota(jnp.int32, (CHUNK, CHUNK), 0)
            cc = lax.broadcasted_iota(jnp.int32, (CHUNK, CHUNK), 1)
            upper = (cc >= rr).astype(F32)
            lb_all = _lower_bound(lb_ref)
            rowpos = lax.broadcasted_iota(jnp.int32, (CHUNK, 1), 0) % SUB

            @pl.when(i == 0)
            def _():
                dstate[...] = jnp.zeros_like(dstate)
                lbacc[...] = jnp.zeros_like(lbacc)
                hgacc[...] = jnp.zeros_like(hgacc)

            def head(j, c, rows):
                sl = slice(j * LANE, (j + 1) * LANE)
                lb = lb_all[:, sl]
                hgv = hg_ref[:, sl]
                q = q_ref[rows, sl]
                v = v_ref[rows, sl]
                og = og_ref[rows, sl]
                o = o_ref[rows, sl]
                dcg = dc_ref[rows, sl]
                sf, fg, kk, sq, qh = _gate_terms(q, f_ref[rows, sl], lb)
                b, rs, re, tot = _decays(jnp.log(fg), consts)
                eq = jnp.exp(b - rs)
                ek = jnp.exp(re - b)
                qt = qh * eq
                kt = kk * ek
                e_in = jnp.exp(b)
                e_up = jnp.exp(tot - b)
                e_tot = jnp.exp(tot)
                q_in = (qh * e_in).astype(BF16)
                k_up = (kk * e_up).astype(BF16)
                vb = v.astype(BF16)
                st = st_ref[j, c]
                dst = dstate[j]
                dstb = dst.astype(BF16)
                yield

                sg = _sigmoid(og)
                r = lax.rsqrt(jnp.mean(o * o, axis=-1, keepdims=True) + RMS_EPS)
                ohat = o * r
                d_og = dcg * ohat * hgv * (sg * (1.0 + og * (1.0 - sg)))
                d_on = dcg * (og * sg)
                hgacc[:, sl] += jnp.sum((d_on * ohat).reshape(CHUNK // SUBLANE, SUBLANE, LANE), axis=0)
                d_oh = d_on * hgv
                do = r * (d_oh - ohat * jnp.mean(d_oh * ohat, axis=-1, keepdims=True))
                dob = do.astype(BF16)

                da = _dot_nt(dob, vb)
                yield
                a_off = jnp.zeros((CHUNK, CHUNK), F32)
                dqt = jnp.zeros((CHUNK, LANE), F32)
                dkt = jnp.zeros((CHUNK, LANE), F32)
                for blk in range(1, CHUNK // SUB):
                    n = SUB * blk
                    kib, scale = _scaled_keys(kt, rs, re, blk)
                    qib = _only_block(qt, blk)
                    dab = _only_block(da, blk)
                    a_off = a_off + _dot_nt(qib, kib)
                    dqt = dqt + _dot(dab, kib)
                    dkt = jnp.concatenate([dkt[:n] + _dot_tn(dab, qib)[:n] * scale, dkt[n:]], axis=0)
                    yield
                dqh = dqt * eq
                dk = dkt * ek
                dv = _dot_tn(a_off.astype(BF16), dob)

                dqh = dqh + _dot(dob, st.astype(BF16)) * e_in
                dk = dk + _dot(vb, dstb) * e_up
                dv = dv + _dot_nt(k_up, dstb)
                st_end = st * e_tot + _dot_tn(vb, k_up)
                carry_g = jnp.sum(st_end * dst, axis=0, keepdims=True)
                dstate[j] = dst * e_tot + _dot_tn(dob, q_in)
                yield

                e, rf = None, fg
                for d in range(SUB):
                    if d == 0:
                        a_d = jnp.sum(qh * kk, axis=-1, keepdims=True)
                        da_d = jnp.sum(do * v, axis=-1, keepdims=True)
                        dqh = dqh + da_d * kk
                        dk = dk + da_d * qh
                        dv = dv + a_d * do
                        continue
                    e = rf if e is None else e * rf
                    rf = _roll8(fg, d)
                    em = jnp.where(rowpos >= d, e, 0.0)
                    ks, vs = 1.0 - rf, _roll8(v, d)
                    a_d = jnp.sum(qh * ks * em, axis=-1, keepdims=True)
                    da_d = jnp.sum(do * vs, axis=-1, keepdims=True) * em
                    dqh = dqh + da_d * ks
                    dk = dk + _roll8(da_d * qh, -d)
                    dv = dv + _roll8(a_d * do, -d)
                yield

                dg = _dot3(upper, qh * dqh - kk * dk) + carry_g
                dfg = dg / fg - dk
                lbacc[:, sl] += jnp.sum((dfg * (1.0 - sf)).reshape(CHUNK // SUBLANE, SUBLANE, LANE), axis=0)
                stash[slot, 0, rows, sl] = (dqh * (sq * (1.0 + q * (1.0 - sq)))).astype(BF16)
                stash[slot, 1, rows, sl] = (dfg * (1.0 - lb) * sf * (1.0 - sf)).astype(BF16)
                stash[slot, 2, rows, sl] = dv.astype(BF16)
                stash[slot, 3, rows, sl] = d_og.astype(BF16)

            def chunk(cr, carry):
                c = nc - 1 - cr
                rows = pl.ds(pl.multiple_of(c * CHUNK, CHUNK), CHUNK)
                running = [head(j, c, rows) for j in range(hpb)]
                while running:
                    running = [g for g in running if next(g, StopIteration) is not StopIteration]
                return carry

            lax.fori_loop(0, nc, chunk, 0)

            @pl.when(i == nb - 1)
            def _():
                dlb_ref[...] = jnp.sum(lbacc[...], axis=0, keepdims=True)
                dhg_ref[...] = jnp.sum(hgacc[...], axis=0, keepdims=True)

        compute()
        out_copy(slot, nb - 1 - i, h).start()

        @pl.when(step == n_steps - 1)
        def _():
            out_copy(slot, 0, 0).wait()
            if n_steps >= 2:
                out_copy(1 - slot, 0, 0).wait()

    rev = lambda i: nb - 1 - i
    sec = lambda s: pl.BlockSpec((None, tb, bw), lambda h, i: (s, rev(i), h))
    return _call(
        body, (p3, p3, p3, p3, lb_logits, hg, o_pre, states, dcat, dp3), name="hgrn_bwd", grid=(nh // hpb, nb),
        in_specs=[sec(2), sec(3), sec(4), sec(5),
                  pl.BlockSpec((2, bw), lambda h, i: (0, h)),
                  pl.BlockSpec((1, bw), lambda h, i: (0, h)),
                  pl.BlockSpec((tb, bw), lambda h, i: (rev(i), h)),
                  pl.BlockSpec((hpb, nc, LANE, LANE), lambda h, i: (h, rev(i), 0, 0)),
                  pl.BlockSpec((tb, bw), lambda h, i: (rev(i), nh // hpb + h)), HBM],
        out_specs=[HBM,
                   pl.BlockSpec((1, bw), lambda h, i: (0, h)),
                   pl.BlockSpec((1, bw), lambda h, i: (0, h))],
        out_shape=[jax.ShapeDtypeStruct((n_sec, t, w), BF16), jax.ShapeDtypeStruct((1, w), F32),
                   jax.ShapeDtypeStruct((1, w), F32)],
        scratch_shapes=[pltpu.VMEM((hpb, LANE, LANE), F32), pltpu.VMEM((2, 4, tb, bw), BF16),
                        pltpu.VMEM((SUBLANE, bw), F32), pltpu.VMEM((SUBLANE, bw), F32),
                        pltpu.SemaphoreType.DMA((2,))],
        aliases={9: 0}, rider=rider)


def _place():
    x, y, c = lax.axis_index("x"), lax.axis_index("y"), lax.axis_index("c")
    chips = [(1 - x, y), (x, 1 - y), (1 - x, 1 - y)]
    return x, y, c, chips


def _rows(buf, px, py, pc, part=None):
    half = buf.shape[1] // 2
    if part is None:
        return buf.at[2 * px + py, pl.ds(pc * half, half)]
    lo, hi, n = part
    piece = half // n
    return buf.at[2 * px + py, pl.ds(pc * half + lo * piece, (hi - lo) * piece)]


def _rcopy(src, dst, send, recv, idx, to):
    return pltpu.make_async_remote_copy(src_ref=src, dst_ref=dst, send_sem=send.at[idx], recv_sem=recv.at[idx],
                                        device_id=to, device_id_type=MESH)


def _same(bufs):
    return [jax.ShapeDtypeStruct(b.shape, b.dtype) for b in bufs]


def _ride_gather_ici(bufs, parts=None):
    n = len(bufs)
    parts = parts or [None] * n

    def start(rin, rout, send, recv):
        x, y, c, chips = _place()
        for k in range(n):
            mine = _rows(rout[k], x, y, c, parts[k])
            for j, chip in enumerate(chips):
                _rcopy(mine, mine, send, recv, 3 * k + j, (*chip, c)).start()

    def finish(rin, rout, send, recv):
        x, y, c, chips = _place()
        for k in range(n):
            for j, chip in enumerate(chips):
                theirs = _rows(rout[k], *chip, c, parts[k])
                _rcopy(theirs, theirs, send, recv, 3 * k + j, (x, y, c)).wait_recv()
        for k in range(n):
            mine = _rows(rout[k], x, y, c, parts[k])
            for j in range(3):
                _rcopy(mine, mine, send, recv, 3 * k + j, (x, y, c)).wait_send()

    return _Rider(bufs, _same(bufs), {k: k for k in range(n)}, 3 * n, start, finish)


class _SemView:
    def __init__(self, ref, base):
        self.ref, self.base = ref, base

    @property
    def at(self):
        return self

    def __getitem__(self, idx):
        return self.ref.at[idx + self.base]


def _ride_both(a, b):
    nai, nao = len(a.ins), len(a.outs)

    def start(rin, rout, send, recv):
        a.start(rin[:nai], rout[:nao], send, recv)
        b.start(rin[nai:], rout[nao:], _SemView(send, a.n_sems), _SemView(recv, a.n_sems))

    def finish(rin, rout, send, recv):
        a.finish(rin[:nai], rout[:nao], send, recv)
        b.finish(rin[nai:], rout[nao:], _SemView(send, a.n_sems), _SemView(recv, a.n_sems))

    aliases = dict(a.aliases)
    aliases.update({nai + ri: nao + ro for ri, ro in b.aliases.items()})
    return _Rider(a.ins + b.ins, a.outs + b.outs, aliases, a.n_sems + b.n_sems, start, finish)


def _ride_gather_d2d(bufs):
    n = len(bufs)

    def start(rin, rout, send, recv):
        x, y, c, chips = _place()
        for k in range(n):
            for j, chip in enumerate(chips):
                got = _rows(rout[k], *chip, c)
                _rcopy(got, got, send, recv, 3 * k + j, (x, y, 1 - c)).start()

    def finish(rin, rout, send, recv):
        x, y, c, chips = _place()
        for k in range(n):
            for j, chip in enumerate(chips):
                theirs = _rows(rout[k], *chip, 1 - c)
                _rcopy(theirs, theirs, send, recv, 3 * k + j, (x, y, c)).wait_recv()
        for k in range(n):
            for j, chip in enumerate(chips):
                got = _rows(rout[k], *chip, c)
                _rcopy(got, got, send, recv, 3 * k + j, (x, y, c)).wait_send()

    return _Rider(bufs, _same(bufs), {k: k for k in range(n)}, 3 * n, start, finish)


def _ride_swap(grads):
    n = len(grads)

    def copy(k, rin, rout, send, recv):
        x, y, c, _ = _place()
        half = rin[k].shape[1] // 2
        return _rcopy(rin[k].at[:, pl.ds((1 - c) * half, half)], rout[k], send, recv, k, (x, y, 1 - c))

    def start(rin, rout, send, recv):
        for k in range(n):
            copy(k, rin, rout, send, recv).start()

    def finish(rin, rout, send, recv):
        for k in range(n):
            copy(k, rin, rout, send, recv).wait()

    outs = [jax.ShapeDtypeStruct((g.shape[0], g.shape[1] // 2, g.shape[2]), g.dtype) for g in grads]
    return _Rider(grads, outs, {}, n, start, finish)


def _ride_send_partials(parts, pieces=None, into=None):
    n = len(parts)
    pieces = pieces or [None] * n

    def cut(ref, k):
        if pieces[k] is None:
            return ref
        lo, hi, m = pieces[k]
        q = ref.shape[0] // m
        return ref.at[pl.ds(lo * q, (hi - lo) * q)]

    def copies(rin, rout, send, recv):
        x, y, c, chips = _place()
        return [_rcopy(cut(rin[k].at[2 * px + py], k), cut(rout[k].at[j], k), send, recv, 3 * k + j, (px, py, c))
                for k in range(n) for j, (px, py) in enumerate(chips)]

    def start(rin, rout, send, recv):
        for cp in copies(rin, rout, send, recv):
            cp.start()

    def finish(rin, rout, send, recv):
        for cp in copies(rin, rout, send, recv):
            cp.wait()

    if into is None:
        outs = [jax.ShapeDtypeStruct((3,) + p.shape[1:], p.dtype) for p in parts]
        return _Rider(parts, outs, {}, 3 * n, start, finish)
    return _Rider(list(parts) + list(into), _same(into), {n + k: k for k in range(n)}, 3 * n, start, finish)


def _ride_join(bufs):
    n = len(bufs)

    def half_of(buf, pc):
        half = buf.shape[0] // 2
        return buf.at[pl.ds(pc * half, half)]

    def start(rin, rout, send, recv):
        x, y, c, _ = _place()
        for k in range(n):
            mine = half_of(rout[k], c)
            _rcopy(mine, mine, send, recv, k, (x, y, 1 - c)).start()

    def finish(rin, rout, send, recv):
        x, y, c, _ = _place()
        for k in range(n):
            mine, theirs = half_of(rout[k], c), half_of(rout[k], 1 - c)
            _rcopy(mine, mine, send, recv, k, (x, y, c)).wait_send()
            _rcopy(theirs, theirs, send, recv, k, (x, y, c)).wait_recv()

    return _Rider(bufs, _same(bufs), {k: k for k in range(n)}, n, start, finish)


def _run(name, rider):
    def body(*refs):
        nri, nro = len(rider.ins), len(rider.outs)
        rin, rout = refs[:nri], refs[nri:nri + nro]
        send, recv = refs[nri + nro:]
        rider.start(rin, rout, send, recv)
        rider.finish(rin, rout, send, recv)

    return pl.pallas_call(
        body, name=name, in_specs=[HBM] * len(rider.ins), out_specs=[HBM] * len(rider.outs), out_shape=rider.outs,
        scratch_shapes=[pltpu.SemaphoreType.DMA((rider.n_sems,)), pltpu.SemaphoreType.DMA((rider.n_sems,))],
        input_output_aliases=rider.aliases,
    )(*rider.ins)


def _add_halves(name, g, other, c_idx):
    s, r, cols = g.shape
    half = r // 2
    tr = _div_tile(half, 16, 512)
    nb = half // tr

    def body(c_ref, g_ref, o_ref, q_ref):
        del c_ref
        q_ref[...] = (g_ref[...] + o_ref[...]).astype(BF16)

    return pl.pallas_call(
        body, name=name,
        grid_spec=pltpu.PrefetchScalarGridSpec(
            num_scalar_prefetch=1, grid=(s, nb),
            in_specs=[pl.BlockSpec((None, tr, cols), lambda k, i, c: (k, c[0] * nb + i, 0)),
                      pl.BlockSpec((None, tr, cols), lambda k, i, c: (k, i, 0))],
            out_specs=pl.BlockSpec((None, tr, cols), lambda k, i, c: (k, i, 0))),
        out_shape=jax.ShapeDtypeStruct((s, half, cols), BF16),
        compiler_params=_params("parallel", "parallel"),
    )(c_idx, g, other)


def _sum_partials(name, part, arrived, place_idx):
    _, half, cols = part.shape
    tr = _div_tile(half, 16, 512)
    nb = half // tr

    def body(s_ref, p_ref, a_ref, o_ref):
        del s_ref
        o_ref[...] = ((p_ref[...].astype(F32) + a_ref[0].astype(F32)) + a_ref[1].astype(F32)) + a_ref[2].astype(F32)

    return pl.pallas_call(
        body, name=name,
        grid_spec=pltpu.PrefetchScalarGridSpec(
            num_scalar_prefetch=1, grid=(nb,),
            in_specs=[pl.BlockSpec((None, tr, cols), lambda i, s: (s[0], i, 0)),
                      pl.BlockSpec((3, tr, cols), lambda i, s: (0, i, 0))],
            out_specs=pl.BlockSpec((tr, cols), lambda i, s: (s[1] * nb + i, 0))),
        out_shape=jax.ShapeDtypeStruct((2 * half, cols), F32),
        compiler_params=_params("parallel"),
    )(place_idx, part, arrived)


def _pack_small(name, wide_rows, ffn_rows, w, dff, n_wide, n_ffn):
    n_in = len(wide_rows) + len(ffn_rows)

    def body(*refs):
        ins, outs = refs[:n_in], refs[n_in:]
        p1 = outs[0]
        p1[...] = jnp.zeros_like(p1)
        row = 0
        for ref, (_, r, m) in zip(ins, wide_rows):
            if m == 1 and r % SUBLANE == 0 and row % SUBLANE == 0:
                p1[row:row + r, :] = ref[...]
                row += r
                continue
            for rr in range(r):
                for mm in range(m):
                    p1[row:row + 1, :] = ref[rr:rr + 1, mm * w:(mm + 1) * w]
                    row += 1
        if ffn_rows:
            p2 = outs[1]
            p2[...] = jnp.zeros_like(p2)
            row = 0
            for ref, arr in zip(ins[len(wide_rows):], ffn_rows):
                r = arr.shape[0]
                p2[row:row + r, :] = ref[...]
                row += r

    shapes = [jax.ShapeDtypeStruct((n_wide, w), F32)] + ([jax.ShapeDtypeStruct((n_ffn, dff), F32)] if ffn_rows else [])
    return pl.pallas_call(
        body, name=name, in_specs=[VMEM_FULL] * n_in, out_specs=[VMEM_FULL] * len(shapes), out_shape=shapes,
        compiler_params=pltpu.CompilerParams(vmem_limit_bytes=VMEM_LIMIT),
    )(*[a for a, _, _ in wide_rows], *ffn_rows)


def _ride_exchange8(packs):
    n = len(packs)

    def copies(rin, rout, send, recv):
        x, y, c, _ = _place()
        me = 4 * x + 2 * y + c
        out = []
        for a in range(n):
            for mask in range(1, 8):
                peer = (x ^ (mask >> 2), y ^ ((mask >> 1) & 1), c ^ (mask & 1))
                out.append(_rcopy(rin[a], rout[a].at[me], send, recv, 8 * a + mask, peer))
        own = [pltpu.make_async_copy(rin[a], rout[a].at[me], send.at[8 * a]) for a in range(n)]
        return out, own

    def start(rin, rout, send, recv):
        remote, own = copies(rin, rout, send, recv)
        for cp in remote + own:
            cp.start()

    def finish(rin, rout, send, recv):
        remote, own = copies(rin, rout, send, recv)
        for cp in remote + own:
            cp.wait()

    outs = [jax.ShapeDtypeStruct((8,) + p.shape, p.dtype) for p in packs]
    return _Rider(packs, outs, {}, 8 * n, start, finish)


def _sum_small(name, slots):
    def body(*refs):
        n = len(refs) // 2
        for r_ref, s_ref in zip(refs[:n], refs[n:]):
            tot = r_ref[0]
            for d in range(1, 8):
                tot = tot + r_ref[d]
            s_ref[...] = tot

    return pl.pallas_call(
        body, name=name, in_specs=[VMEM_FULL] * len(slots), out_specs=[VMEM_FULL] * len(slots),
        out_shape=[jax.ShapeDtypeStruct(s.shape[1:], s.dtype) for s in slots],
        compiler_params=pltpu.CompilerParams(vmem_limit_bytes=VMEM_LIMIT),
    )(*slots)


def _adamw(w, g, m, v):
    m2 = ADAM_B1 * m + (1.0 - ADAM_B1) * g
    v2 = ADAM_B2 * v + (1.0 - ADAM_B2) * (g * g)
    m_hat = m2 / (1.0 - ADAM_B1 ** ADAM_STEP)
    v_hat = v2 / (1.0 - ADAM_B2 ** ADAM_STEP)
    delta = -ADAM_LR * (m_hat / (jnp.sqrt(v_hat) + ADAM_EPS) + ADAM_WD * w)
    return delta, m2, v2


def _adam_big(name, w, g, m, v):
    r, c = w.shape
    tr = 128 if r % 128 == 0 else r

    def body(w_ref, g_ref, m_ref, v_ref, go_ref, d_ref, m2_ref, v2_ref):
        g = g_ref[...]
        go_ref[...] = g
        d_ref[...], m2_ref[...], v2_ref[...] = _adamw(w_ref[...], g, m_ref[...], v_ref[...])

    blk = pl.BlockSpec((tr, c), lambda i: (i, 0))
    return _call(
        body, (w, g, m, v), name=name, grid=(r // tr,), in_specs=[blk] * 4, out_specs=[blk] * 4,
        out_shape=[jax.ShapeDtypeStruct((r, c), F32)] * 4)


def _adam_small(s1, s2, s3, cw_g, fw_g, lb_logits, triples, layout, w):
    n = len(triples)

    def body(*refs):
        s1_ref, s2_ref, s3_ref, cw_ref, fw_ref, lbl_ref = refs[:6]
        prm = refs[6:6 + 3 * n]
        outs = refs[6 + 3 * n:]
        for p, lay in enumerate(layout):
            w_ref, m_ref, v_ref = prm[3 * p:3 * p + 3]
            g_ref, d_ref, m2_ref, v2_ref = outs[4 * p:4 * p + 4]
            if lay[0] in ("wide", "late"):
                _, row, r, pieces = lay
                src = s1_ref if lay[0] == "wide" else s3_ref
                for rr in range(r):
                    for mm in range(pieces):
                        g_ref[rr:rr + 1, mm * w:(mm + 1) * w] = src[row:row + 1, :]
                        row += 1
            elif lay[0] == "ffn":
                _, row, r = lay
                g_ref[...] = s2_ref[row:row + r, :]
            elif lay[0] == "cw":
                g_ref[...] = cw_ref[0:g_ref.shape[0], :]
            elif lay[0] == "fw":
                g_ref[...] = fw_ref[0:g_ref.shape[0], :]
            else:
                s0 = _lower_bound(lbl_ref)
                d0 = s1_ref[lay[1]:lay[1] + 1, :] * s0 * (1.0 - s0)
                g_ref[0:1, :] = d0
                g_ref[1:2, :] = -d0
            d_ref[...], m2_ref[...], v2_ref[...] = _adamw(w_ref[...], g_ref[...], m_ref[...], v_ref[...])

    flat = [a for tr in triples for a in tr]
    shapes = []
    for tr in triples:
        shapes.extend([jax.ShapeDtypeStruct(tr[0].shape, F32)] * 4)
    return pl.pallas_call(
        body, name="adam_small", in_specs=[VMEM_FULL] * (6 + 3 * n), out_specs=[VMEM_FULL] * (4 * n),
        out_shape=shapes, compiler_params=pltpu.CompilerParams(vmem_limit_bytes=VMEM_LIMIT),
    )(s1, s2, s3, cw_g, fw_g, lb_logits, *flat)


def _row_tile(t):
    return 512 if t % 512 == 0 and t >= 2048 else 128


def kernel(x, emb_ln_g, emb_ln_b, w_in, conv_w, conv_b, conv_norm_g, conv_norm_b, lb_logits, hgrn_norm_g, w_out, ln1_g, ln1_b, w_ffn_up, ffn_conv_w, ffn_conv_b, w_ffn_down, ln2_g, ln2_b, loss_target, m_emb_ln_g, m_emb_ln_b, m_w_in, m_conv_w, m_conv_b, m_conv_norm_g, m_conv_norm_b, m_lb_logits, m_hgrn_norm_g, m_w_out, m_ln1_g, m_ln1_b, m_w_ffn_up, m_ffn_conv_w, m_ffn_conv_b, m_w_ffn_down, m_ln2_g, m_ln2_b, v_emb_ln_g, v_emb_ln_b, v_w_in, v_conv_w, v_conv_b, v_conv_norm_g, v_conv_norm_b, v_lb_logits, v_hgrn_norm_g, v_w_out, v_ln1_g, v_ln1_b, v_w_ffn_up, v_ffn_conv_w, v_ffn_conv_b, v_w_ffn_down, v_ln2_g, v_ln2_b):
    depth = w_in.shape[0]
    assert depth == 1 and x.shape[0] == 1
    alpha = (2.0 * depth) ** 0.25
    t, d = x.shape[1], x.shape[2]
    w = d // 2
    dff = ffn_conv_b.shape[1]
    kc = conv_w.shape[1]
    assert w % (2 * LANE) == 0 and dff % (4 * LANE) == 0 and t % 128 == 0
    tm = _row_tile(t)
    tm2 = tm // 2
    tmm = 1024 if t % 1024 == 0 and t >= 2048 else tm
    cb = 2 * LANE
    cbf = 4 * LANE
    tb = tm
    nh = w // LANE
    hpb = 4 if nh % 4 == 0 else 2

    xi = lax.axis_index("x")
    yi = lax.axis_index("y")
    ci = lax.axis_index("c")
    chip = 2 * xi + yi
    c_idx = jnp.reshape(ci, (1,)).astype(jnp.int32)
    chip_idx = jnp.reshape(chip, (1,)).astype(jnp.int32)
    place_idx = jnp.stack([chip, ci]).astype(jnp.int32)

    x2 = x[0]
    tgt = loss_target[0]
    g0, b0 = emb_ln_g.reshape(1, d), emb_ln_b.reshape(1, d)
    w_in2, w_out2, w_up2, w_dn2 = w_in[0], w_out[0], w_ffn_up[0], w_ffn_down[0]
    cw2, fw2 = conv_w[0], ffn_conv_w[0]

    b_in = _place_shard(w_in2, "place_w_in", chip_idx, BF16)
    b_out = _place_shard(w_out2, "place_w_out", chip_idx, BF16)
    b_up = _place_shard(w_up2, "place_w_up", chip_idx, BF16)
    b_dn = _place_shard(w_dn2, "place_w_down", chip_idx, BF16)
    b_cw = _place_shard(_pad_rows(cw2), "place_conv_w", chip_idx, F32)
    b_fw = _place_shard(_pad_rows(fw2), "place_ffn_conv_w", chip_idx, F32)
    h0b, (b_in,) = _ln0(x2, g0, b0, tm, rider=_ride_gather_ici([b_in], [(0, 1, 8)]))
    first = _run("gather_first_ici", _ride_gather_ici([b_in, b_cw, b_fw], [(1, 8, 8), None, None]))
    w_in3, cw_full3, fw_full3 = _run("gather_first_d2d", _ride_gather_d2d(first))
    cw_full = _unshard_cols(cw_full3)[:kc]
    fw_full = _unshard_cols(fw_full3)[:fw2.shape[0]]

    p3, (b_out, b_up) = _proj("in_proj", h0b, w_in3, 6, 2 * tmm if t % (2 * tmm) == 0 else tmm, w // 2,
                              rider=_ride_gather_ici([b_out, b_up], [None, (0, 1, 4)]))
    (cat, u1), (w_out3, b_up) = _conv_fwd(
        p3, cw_full, conv_b, conv_norm_g, conv_norm_b, tm2, cb,
        rider=_ride_both(_ride_gather_d2d([b_out]), _ride_gather_ici([b_up], [(1, 2, 4)])))
    w_out_full = w_out3.reshape(d, d)
    (cat, o_pre, states), got = _hgrn_fwd(p3, lb_logits, hgrn_norm_g, cat, tb, hpb,
                                          rider=_ride_gather_ici([b_up], [(2, 4, 4)]))
    (xhat1, h1b, rstd1), (w_up3,) = _mix_ln1(cat, w_out_full, x2, g0, b0, ln1_g, ln1_b, alpha, tm2,
                                             rider=_ride_gather_d2d(got))
    hh3, got = _proj("ffn_up", h1b, w_up3, 2, tmm, dff // 4, rider=_ride_gather_ici([b_dn]))
    act, (w_dn3,) = _ffn_act_fwd(hh3, fw_full, ffn_conv_b, tm, cbf, rider=_ride_gather_d2d(got))
    ks = dff // N_CHIPS
    ffn = _wgrad("ffn_down", act, w_dn3, (t, d), (t // tmm, 1, N_CHIPS),
                 pl.BlockSpec((tmm, ks), lambda i, j, k: (i, k)),
                 pl.BlockSpec((None, ks, d), lambda i, j, k: (k, 0, 0)),
                 pl.BlockSpec((tmm, d), lambda i, j, k: (i, 0)), dot=_dot)
    dz2, dz2b, dg2, db2, loss_row = _ln2_loss(ffn, xhat1, tgt, ln1_g, ln1_b, ln2_g, ln2_b, alpha, tm2)

    dact = _proj_t("ffn_down_t", dz2b, w_dn3.reshape(dff, d), tmm, ks)
    dhh3, dfw, dfb = _ffn_act_bwd(dact, hh3, fw_full, ffn_conv_b, tm, cbf)
    tt = 2 * tmm if t % (2 * tmm) == 0 else tmm
    d_w_dn = _wgrad("wgrad_down", act, dz2b, (N_CHIPS, ks, d), (N_CHIPS, 2, t // tt),
                    pl.BlockSpec((tt, ks), lambda s, j, k: (k, s)),
                    pl.BlockSpec((tt, d // 2), lambda s, j, k: (k, j)),
                    pl.BlockSpec((None, ks, d // 2), lambda s, j, k: (s, 0, j)))
    wu = 2 * dff // N_CHIPS
    tnu = wu // 2
    per_sec_u = dff // tnu
    pre1, (arr_dn,) = _wgrad(
        "up_t", dhh3, w_up3, (t, d), (t // tmm, 1, 2 * N_CHIPS),
        pl.BlockSpec((None, tmm, tnu), lambda i, j, k: (k // per_sec_u, i, k % per_sec_u)),
        pl.BlockSpec((None, d, tnu), lambda i, j, k: (k // 2, 0, k % 2)),
        pl.BlockSpec((tmm, d), lambda i, j, k: (i, 0)), dot=_dot_nt, rider=_ride_swap([d_w_dn]))
    dz1, dz1b, dg1, db1 = _ln1_bwd(pre1, dz2, xhat1, rstd1, ln1_g, alpha, tm2)
    part_dn = _add_halves("add_halves_w_down", d_w_dn, arr_dn, c_idx)
    d_w_up, (land_dn,) = _wgrad(
        "wgrad_up", h1b, dhh3, (N_CHIPS, d, wu), (N_CHIPS, 2, 2, t // tt),
        pl.BlockSpec((tt, d // 2), lambda s, r, j, k: (k, r)),
        pl.BlockSpec((None, tt, tnu), lambda s, r, j, k: ((2 * s + j) // per_sec_u, k, (2 * s + j) % per_sec_u)),
        pl.BlockSpec((None, d // 2, tnu), lambda s, r, j, k: (s, r, j)), rider=_ride_send_partials([part_dn]))
    dcat = _proj_t("out_proj_t", dz1b, w_out_full, tmm, d // 2)
    d_w_out = _wgrad("wgrad_out", cat, dz1b, (d, d), (2, 2, t // tt),
                     pl.BlockSpec((tt, d // 2), lambda r, j, k: (k, r)),
                     pl.BlockSpec((tt, d // 2), lambda r, j, k: (k, j)),
                     pl.BlockSpec((d // 2, d // 2), lambda r, j, k: (r, j))).reshape(N_CHIPS, d // N_CHIPS, d)
    du1, dcng, dcnb = _conv_norm_bwd(dcat, u1, conv_norm_g, conv_norm_b, tm)
    (dp3, dcw, dcb), (arr_up, arr_out) = _conv_bwd(du1, p3, cw_full, tm2, cb, rider=_ride_swap([d_w_up, d_w_out]))
    part_up = _add_halves("add_halves_w_up", d_w_up, arr_up, c_idx)
    part_out = _add_halves("add_halves_w_out", d_w_out, arr_out, c_idx)
    (dp3, dlb, dhg), (land_up, land_out) = _hgrn_bwd(
        p3, lb_logits, hgrn_norm_g, o_pre, states, dcat, dp3, tb, hpb,
        rider=_ride_send_partials([part_up, part_out], [(0, 3, 4), None]))
    kpad = dcw.shape[0]
    wide = [(dcw, kpad, 1), (dg1, 1, 2), (db1, 1, 2), (dg2, 1, 2), (db2, 1, 2),
            (dcb, 1, 1), (dcng, 1, 1), (dcnb, 1, 1), (dlb, 1, 1), (dhg, 1, 1)]
    n_wide = -(-sum(r * m for _, r, m in wide) // SUBLANE) * SUBLANE
    packs = _pack_small("pack_small", wide, [dfw, dfb], w, dff, n_wide, 2 * SUBLANE)
    d_w_in, (land_up, slots1, slots2) = _wgrad_in(
        h0b, dp3, N_CHIPS, tt,
        rider=_ride_both(_ride_send_partials([part_up], [(3, 4, 4)], into=[land_up]), _ride_exchange8(packs)))
    s1, s2 = _sum_small("sum_small", [slots1, slots2])
    (arr_in,) = _run("swap_w_in", _ride_swap([d_w_in]))
    part_in = _add_halves("add_halves_w_in", d_w_in, arr_in, c_idx)
    h_out, h_up, h_dn = [
        _sum_partials("sum_partials_" + nm, p, a, place_idx)
        for nm, p, a in (("w_out", part_out, land_out), ("w_up", part_up, land_up), ("w_down", part_dn, land_dn))]
    pre0, (land_in, g_w_out, g_w_up, g_w_dn) = _in_t(
        dp3, w_in3, tmm, rider=_ride_both(_ride_send_partials([part_in]), _ride_join([h_out, h_up, h_dn])))
    dx, dg0, db0 = _ln0_bwd(pre0, dz1, x2, g0, alpha, tm2)
    h_in = _sum_partials("sum_partials_w_in", part_in, land_in, place_idx)
    (g_w_in,) = _run("join_w_in", _ride_join([h_in]))

    late = _pack_small("pack_late", [(dg0, 1, 2), (db0, 1, 2)], [], w, dff, SUBLANE, 0)
    (s3,) = _sum_small("sum_late", _run("exchange_late", _ride_exchange8(late)))
    cw_g = lax.dynamic_slice_in_dim(s1[0:kpad], chip * (w // N_CHIPS), w // N_CHIPS, axis=1)
    fw_g = lax.dynamic_slice_in_dim(s2[0:SUBLANE], chip * (dff // N_CHIPS), dff // N_CHIPS, axis=1)

    small = [
        (g0, m_emb_ln_g.reshape(1, d), v_emb_ln_g.reshape(1, d)), (b0, m_emb_ln_b.reshape(1, d), v_emb_ln_b.reshape(1, d)),
        (cw2, m_conv_w[0], v_conv_w[0]), (conv_b, m_conv_b, v_conv_b),
        (conv_norm_g, m_conv_norm_g, v_conv_norm_g), (conv_norm_b, m_conv_norm_b, v_conv_norm_b),
        (lb_logits, m_lb_logits, v_lb_logits), (hgrn_norm_g, m_hgrn_norm_g, v_hgrn_norm_g),
        (ln1_g, m_ln1_g, v_ln1_g), (ln1_b, m_ln1_b, v_ln1_b),
        (fw2, m_ffn_conv_w[0], v_ffn_conv_w[0]), (ffn_conv_b, m_ffn_conv_b, v_ffn_conv_b),
        (ln2_g, m_ln2_g, v_ln2_g), (ln2_b, m_ln2_b, v_ln2_b),
    ]
    r0 = kpad
    layout = [("late", 0, 1, 2), ("late", 2, 1, 2), ("cw",), ("wide", r0 + 8, 1, 1), ("wide", r0 + 9, 1, 1),
              ("wide", r0 + 10, 1, 1), ("lb", r0 + 11), ("wide", r0 + 12, 1, 1), ("wide", r0, 1, 2),
              ("wide", r0 + 2, 1, 2), ("fw",), ("ffn", SUBLANE, 1), ("wide", r0 + 4, 1, 2), ("wide", r0 + 6, 1, 2)]
    so = _adam_small(s1, s2, s3, cw_g, fw_g, lb_logits, small, layout, w)
    sm = {nm: so[4 * i:4 * i + 4] for i, nm in enumerate(
        ["emb_ln_g", "emb_ln_b", "conv_w", "conv_b", "conv_norm_g", "conv_norm_b", "lb_logits", "hgrn_norm_g",
         "ln1_g", "ln1_b", "ffn_conv_w", "ffn_conv_b", "ln2_g", "ln2_b"])}
    bigs = {}
    for nm, wt, g, m, v in (("w_in", w_in2, g_w_in, m_w_in[0], v_w_in[0]), ("w_out", w_out2, g_w_out, m_w_out[0], v_w_out[0]),
                            ("w_ffn_up", w_up2, g_w_up, m_w_ffn_up[0], v_w_ffn_up[0]),
                            ("w_ffn_down", w_dn2, g_w_dn, m_w_ffn_down[0], v_w_ffn_down[0])):
        bigs[nm] = tuple(_adam_big("adam_" + nm, wt, g, m, v))

    loss = lax.psum(loss_row[0, 0], ("x", "y", "c"))

    order = ["emb_ln_g", "emb_ln_b", "w_in", "conv_w", "conv_b", "conv_norm_g", "conv_norm_b", "lb_logits",
             "hgrn_norm_g", "w_out", "ln1_g", "ln1_b", "w_ffn_up", "ffn_conv_w", "ffn_conv_b", "w_ffn_down",
             "ln2_g", "ln2_b"]
    shapes = dict(emb_ln_g=emb_ln_g.shape, emb_ln_b=emb_ln_b.shape, w_in=w_in.shape, conv_w=conv_w.shape,
                  w_out=w_out.shape, w_ffn_up=w_ffn_up.shape, ffn_conv_w=ffn_conv_w.shape, w_ffn_down=w_ffn_down.shape)
    outs = [loss, dx.reshape(x.shape)]
    for which in range(4):
        for nm in order:
            a = bigs[nm][which] if nm in bigs else sm[nm][which]
            outs.append(a.reshape(shapes[nm]) if nm in shapes else a)
    return tuple(outs)


def _pad_rows(a):
    k = a.shape[0]
    kp = -(-k // 16) * 16
    return jnp.pad(a, ((0, kp - k), (0, 0)))


def _unshard_cols(a3):
    s, k, c = a3.shape
    return jnp.transpose(a3, (1, 0, 2)).reshape(k, s * c)
```

```python
import functools

import jax
import jax.numpy as jnp
from jax import lax
from jax.experimental import pallas as pl
from jax.experimental.pallas import tpu as pltpu

F32 = jnp.float32
BF16 = jnp.bfloat16

LN_EPS = 1e-5
RMS_EPS = 1e-6
LANE = 128
SUBLANE = 8
CHUNK = 64
SUB = 8
HALO = 32
FHALO = 8
ROWS = 64
N_CHIPS = 4
VMEM_LIMIT = 56 << 20
NEG_BIG = -1e30

ADAM_LR = 0.001
ADAM_B1 = 0.9
ADAM_B2 = 0.999
ADAM_EPS = 1e-08
ADAM_WD = 0.01
ADAM_STEP = 10

MESH = pl.DeviceIdType.MESH
HBM = pl.BlockSpec(memory_space=pl.ANY)
VMEM_FULL = pl.BlockSpec(memory_space=pltpu.VMEM)


def _params(*sem):
    return pltpu.CompilerParams(dimension_semantics=sem, vmem_limit_bytes=VMEM_LIMIT)


class _Rider:
    def __init__(self, ins, outs, aliases, n_sems, start, finish):
        self.ins, self.outs, self.aliases = list(ins), list(outs), dict(aliases)
        self.n_sems, self.start, self.finish = n_sems, start, finish


def _call(body, args, *, name, grid, in_specs, out_specs, out_shape, scratch_shapes=(), aliases=None, rider=None):
    many = isinstance(out_shape, (list, tuple))
    shapes = list(out_shape) if many else [out_shape]
    ospecs = list(out_specs) if many else [out_specs]
    aliases = dict(aliases or {})
    sem = ("arbitrary",) * len(grid)
    if rider is None:
        res = pl.pallas_call(
            body, name=name, grid=grid, in_specs=list(in_specs), out_specs=ospecs, out_shape=shapes,
            scratch_shapes=list(scratch_shapes), input_output_aliases=aliases, compiler_params=_params(*sem))(*args)
        return res if many else res[0]
    n_in, n_out, n_scr = len(args), len(shapes), len(scratch_shapes)
    nri, nro = len(rider.ins), len(rider.outs)

    def wrapped(*refs):
        ins, rin = refs[:n_in], refs[n_in:n_in + nri]
        o0 = n_in + nri
        outs, rout = refs[o0:o0 + n_out], refs[o0 + n_out:o0 + n_out + nro]
        s0 = o0 + n_out + nro
        scr, (send, recv) = refs[s0:s0 + n_scr], refs[s0 + n_scr:]
        ids = [pl.program_id(a) for a in range(len(grid))]
        first = functools.reduce(jnp.logical_and, [i == 0 for i in ids])
        last = functools.reduce(jnp.logical_and, [i == g - 1 for i, g in zip(ids, grid)])

        @pl.when(first)
        def _():
            rider.start(rin, rout, send, recv)

        body(*ins, *outs, *scr)

        @pl.when(last)
        def _():
            rider.finish(rin, rout, send, recv)

    for ri, ro in rider.aliases.items():
        aliases[n_in + ri] = n_out + ro
    res = pl.pallas_call(
        wrapped, name=name, grid=grid, in_specs=list(in_specs) + [HBM] * nri, out_specs=ospecs + [HBM] * nro,
        out_shape=shapes + rider.outs,
        scratch_shapes=list(scratch_shapes) + [pltpu.SemaphoreType.DMA((rider.n_sems,)),
                                               pltpu.SemaphoreType.DMA((rider.n_sems,))],
        input_output_aliases=aliases, compiler_params=_params(*sem))(*args, *rider.ins)
    main, extra = res[:n_out], list(res[n_out:])
    return (list(main) if many else main[0]), extra


def _div_tile(n, mult, cap):
    best = n
    for t in range(mult, min(n, cap) + 1, mult):
        if n % t == 0:
            best = t
    return best


def _sigmoid(x):
    return 1.0 / (1.0 + jnp.exp(-x))


def _ln_stats(x):
    mu = jnp.mean(x, axis=-1, keepdims=True)
    xc = x - mu
    var = jnp.mean(xc * xc, axis=-1, keepdims=True)
    rstd = lax.rsqrt(var + LN_EPS)
    return xc * rstd, rstd


def _ln_bwd(dy, xhat, rstd, g):
    dyg = dy * g
    m1 = jnp.mean(dyg, axis=-1, keepdims=True)
    m2 = jnp.mean(dyg * xhat, axis=-1, keepdims=True)
    return rstd * (dyg - m1 - xhat * m2)


def _dot_nt(a, b):
    return lax.dot_general(a, b, (((1,), (1,)), ((), ())), preferred_element_type=F32)


def _dot_tn(a, b):
    return lax.dot_general(a, b, (((0,), (0,)), ((), ())), preferred_element_type=F32)


def _dot(a, b):
    return jnp.dot(a, b, preferred_element_type=F32)


def _dot3(m, x):
    mb = m.astype(BF16)
    x1 = x.astype(BF16)
    r1 = x - x1.astype(F32)
    x2 = r1.astype(BF16)
    x3 = (r1 - x2.astype(F32)).astype(BF16)
    return _dot(mb, x1) + _dot(mb, x2) + _dot(mb, x3)


def _place_shard(x, name, chip_idx, dtype):
    r, c = x.shape
    tr = _div_tile(r, 16, 512)

    def body(s_ref, x_ref, o_ref):
        del s_ref
        o_ref[...] = x_ref[...].astype(dtype)

    return pl.pallas_call(
        body, name=name,
        grid_spec=pltpu.PrefetchScalarGridSpec(
            num_scalar_prefetch=1, grid=(r // tr,),
            in_specs=[pl.BlockSpec((tr, c), lambda i, s: (i, 0))],
            out_specs=pl.BlockSpec((None, tr, c), lambda i, s: (s[0], i, 0))),
        out_shape=jax.ShapeDtypeStruct((N_CHIPS, r, c), dtype),
        compiler_params=_params("parallel"),
    )(chip_idx, x)


def _ln0(x, g, b, tm, rider=None):
    t, d = x.shape

    def body(x_ref, g_ref, b_ref, o_ref):
        xh, _ = _ln_stats(x_ref[...])
        o_ref[...] = (xh * g_ref[...] + b_ref[...]).astype(BF16)

    row = pl.BlockSpec((1, d), lambda i: (0, 0))
    return _call(
        body, (x, g, b), name="ln0", grid=(t // tm,),
        in_specs=[pl.BlockSpec((tm, d), lambda i: (i, 0)), row, row],
        out_specs=pl.BlockSpec((tm, d), lambda i: (i, 0)),
        out_shape=jax.ShapeDtypeStruct((t, d), BF16), rider=rider)


def _proj(name, a, w3, n_sec, tm, tn, rider=None):
    m, k = a.shape
    s, _, ws = w3.shape
    sec_w = s * ws // n_sec
    nj = ws // tn
    per_sec = sec_w // tn

    def body(a_ref, w_ref, o_ref):
        o_ref[...] = _dot(a_ref[...], w_ref[...])

    return _call(
        body, (a, w3), name=name, grid=(s * nj, m // tm),
        in_specs=[pl.BlockSpec((tm, k), lambda j, i: (i, 0)),
                  pl.BlockSpec((None, k, tn), lambda j, i: (j // nj, 0, j % nj))],
        out_specs=pl.BlockSpec((None, tm, tn), lambda j, i: (j // per_sec, i, j % per_sec)),
        out_shape=jax.ShapeDtypeStruct((n_sec, m, sec_w), F32), rider=rider)


def _proj_t(name, a, w, tm, tn, rider=None):
    m, k = a.shape
    n = w.shape[0]

    def body(a_ref, w_ref, o_ref):
        o_ref[...] = _dot_nt(a_ref[...], w_ref[...])

    return _call(
        body, (a, w), name=name, grid=(n // tn, m // tm),
        in_specs=[pl.BlockSpec((tm, k), lambda j, i: (i, 0)),
                  pl.BlockSpec((tn, k), lambda j, i: (j, 0))],
        out_specs=pl.BlockSpec((tm, tn), lambda j, i: (i, j)),
        out_shape=jax.ShapeDtypeStruct((m, n), F32), rider=rider)


def _wgrad(name, a, b, out_shape, grid, a_spec, b_spec, o_spec, rider=None, dot=_dot_tn):
    nt = len(grid) - 1

    def body(a_ref, b_ref, o_ref):
        t = pl.program_id(nt)
        prod = dot(a_ref[...], b_ref[...])

        @pl.when(t == 0)
        def _():
            o_ref[...] = prod

        @pl.when(t > 0)
        def _():
            o_ref[...] += prod

    return _call(
        body, (a, b), name=name, grid=grid, in_specs=[a_spec, b_spec], out_specs=o_spec,
        out_shape=jax.ShapeDtypeStruct(out_shape, F32), rider=rider)


def _wgrad_in(h0b, dp3, n_shards, tt, rider=None):
    t, d = h0b.shape
    n_sec, _, sec_w = dp3.shape
    ws = n_sec * sec_w // n_shards
    tn = sec_w // 2
    nq = ws // tn
    tr = d // 2

    def body(*refs):
        a_ref, b_refs, o_ref = refs[0], refs[1:1 + nq], refs[1 + nq]
        k = pl.program_id(2)
        a = a_ref[...]
        prods = [_dot_tn(a, b_ref[...]) for b_ref in b_refs]

        @pl.when(k == 0)
        def _():
            for q in range(nq):
                o_ref[:, q * tn:(q + 1) * tn] = prods[q]

        @pl.when(k > 0)
        def _():
            for q in range(nq):
                o_ref[:, q * tn:(q + 1) * tn] += prods[q]

    def b_spec(q):
        return pl.BlockSpec((None, tt, tn), lambda s, r, k: ((nq * s + q) // 2, k, (nq * s + q) % 2))

    return _call(
        body, (h0b,) + (dp3,) * nq, name="wgrad_in", grid=(n_shards, d // tr, t // tt),
        in_specs=[pl.BlockSpec((tt, tr), lambda s, r, k: (k, r))] + [b_spec(q) for q in range(nq)],
        out_specs=pl.BlockSpec((None, tr, ws), lambda s, r, k: (s, r, 0)),
        out_shape=jax.ShapeDtypeStruct((n_shards, d, ws), F32), rider=rider)


def _in_t(dp3, w_in3, tm, rider=None):
    _, t, sec_w = dp3.shape
    s, d, ws = w_in3.shape
    tk = sec_w // 2
    nq = ws // tk

    def body(*refs):
        a_refs, w_ref, o_ref = refs[:nq], refs[nq], refs[nq + 1]
        k = pl.program_id(1)
        prod = _dot_nt(a_refs[0][...], w_ref[:, 0:tk])
        for q in range(1, nq):
            prod = prod + _dot_nt(a_refs[q][...], w_ref[:, q * tk:(q + 1) * tk])

        @pl.when(k == 0)
        def _():
            o_ref[...] = prod

        @pl.when(k > 0)
        def _():
            o_ref[...] += prod

    def a_spec(q):
        return pl.BlockSpec((None, tm, tk), lambda i, k: ((nq * k + q) // 2, i, (nq * k + q) % 2))

    return _call(
        body, (dp3,) * nq + (w_in3,), name="in_t", grid=(t // tm, s),
        in_specs=[a_spec(q) for q in range(nq)] + [pl.BlockSpec((None, d, ws), lambda i, k: (k, 0, 0))],
        out_specs=pl.BlockSpec((tm, d), lambda i, k: (i, 0)),
        out_shape=jax.ShapeDtypeStruct((t, d), F32), rider=rider)


def _mix_ln1(cat, w_out, x, g0, b0, g1, b1, alpha, tm, rider=None):
    t, d = x.shape

    def body(cat_ref, w_ref, x_ref, g0_ref, b0_ref, g1_ref, b1_ref, xh_ref, h1b_ref, rstd_ref):
        mix = _dot(cat_ref[...], w_ref[...])
        xh0, _ = _ln_stats(x_ref[...])
        z1 = alpha * (xh0 * g0_ref[...] + b0_ref[...]) + mix
        xh1, rstd1 = _ln_stats(z1)
        xh_ref[...] = xh1
        h1b_ref[...] = (xh1 * g1_ref[...] + b1_ref[...]).astype(BF16)
        rstd_ref[...] = rstd1

    row = pl.BlockSpec((1, d), lambda i: (0, 0))
    blk = pl.BlockSpec((tm, d), lambda i: (i, 0))
    return _call(
        body, (cat, w_out, x, g0, b0, g1, b1), name="mix_ln1", grid=(t // tm,),
        in_specs=[blk, pl.BlockSpec((d, d), lambda i: (0, 0)), blk, row, row, row, row],
        out_specs=[blk, blk, pl.BlockSpec((tm, 1), lambda i: (i, 0))],
        out_shape=[jax.ShapeDtypeStruct((t, d), F32), jax.ShapeDtypeStruct((t, d), BF16),
                   jax.ShapeDtypeStruct((t, 1), F32)], rider=rider)


def _ln2_loss(ffn, xhat1, tgt, g1, b1, g2, b2, alpha, tm):
    t, d = xhat1.shape
    ni = t // tm
    inv_d = 1.0 / d

    def body(ffn_ref, xh1_ref, tgt_ref, g1_ref, b1_ref, g2_ref, b2_ref,
             dz2_ref, dz2b_ref, dg2_ref, db2_ref, loss_ref, lrow):
        i = pl.program_id(0)
        h1 = xh1_ref[...] * g1_ref[...] + b1_ref[...]
        xh2, rstd2 = _ln_stats(alpha * h1 + ffn_ref[...])
        g2v = g2_ref[...]
        diff = xh2 * g2v + b2_ref[...] - tgt_ref[...]
        dh2 = diff * inv_d
        sq = jnp.sum(diff * diff, axis=0, keepdims=True)
        dg = jnp.sum(dh2 * xh2, axis=0, keepdims=True)
        db = jnp.sum(dh2, axis=0, keepdims=True)

        @pl.when(i == 0)
        def _():
            lrow[...] = sq
            dg2_ref[...] = dg
            db2_ref[...] = db

        @pl.when(i > 0)
        def _():
            lrow[...] += sq
            dg2_ref[...] += dg
            db2_ref[...] += db

        dz2 = _ln_bwd(dh2, xh2, rstd2, g2v)
        dz2_ref[...] = dz2
        dz2b_ref[...] = dz2.astype(BF16)

        @pl.when(i == ni - 1)
        def _():
            tot = jnp.sum(lrow[...], axis=-1, keepdims=True) * (0.5 * inv_d)
            loss_ref[...] = jnp.broadcast_to(tot, (1, LANE))

    row = pl.BlockSpec((1, d), lambda i: (0, 0))
    blk = pl.BlockSpec((tm, d), lambda i: (i, 0))
    return _call(
        body, (ffn, xhat1, tgt, g1, b1, g2, b2), name="ln2_loss", grid=(ni,),
        in_specs=[blk, blk, blk, row, row, row, row],
        out_specs=[blk, blk, row, row, pl.BlockSpec((1, LANE), lambda i: (0, 0))],
        out_shape=[jax.ShapeDtypeStruct((t, d), F32), jax.ShapeDtypeStruct((t, d), BF16),
                   jax.ShapeDtypeStruct((1, d), F32), jax.ShapeDtypeStruct((1, d), F32),
                   jax.ShapeDtypeStruct((1, LANE), F32)],
        scratch_shapes=[pltpu.VMEM((1, d), F32)])


def _ln1_bwd(pre, dz2, xhat1, rstd1, g1, alpha, tm):
    t, d = dz2.shape

    def body(pre_ref, dz2_ref, xh_ref, rstd_ref, g_ref, dz1_ref, dz1b_ref, dg_ref, db_ref):
        i = pl.program_id(0)
        dh1 = alpha * dz2_ref[...] + pre_ref[...]
        xh = xh_ref[...]
        dg = jnp.sum(dh1 * xh, axis=0, keepdims=True)
        db = jnp.sum(dh1, axis=0, keepdims=True)

        @pl.when(i == 0)
        def _():
            dg_ref[...] = dg
            db_ref[...] = db

        @pl.when(i > 0)
        def _():
            dg_ref[...] += dg
            db_ref[...] += db

        dz1 = _ln_bwd(dh1, xh, rstd_ref[...], g_ref[...])
        dz1_ref[...] = dz1
        dz1b_ref[...] = dz1.astype(BF16)

    row = pl.BlockSpec((1, d), lambda i: (0, 0))
    blk = pl.BlockSpec((tm, d), lambda i: (i, 0))
    return _call(
        body, (pre, dz2, xhat1, rstd1, g1), name="ln1_bwd", grid=(t // tm,),
        in_specs=[blk, blk, blk, pl.BlockSpec((tm, 1), lambda i: (i, 0)), row],
        out_specs=[blk, blk, row, row],
        out_shape=[jax.ShapeDtypeStruct((t, d), F32), jax.ShapeDtypeStruct((t, d), BF16),
                   jax.ShapeDtypeStruct((1, d), F32), jax.ShapeDtypeStruct((1, d), F32)])


def _ln0_bwd(pre, dz1, x, g0, alpha, tm):
    t, d = x.shape

    def body(pre_ref, dz1_ref, x_ref, g_ref, dx_ref, dg_ref, db_ref):
        i = pl.program_id(0)
        dh0 = alpha * dz1_ref[...] + pre_ref[...]
        xh, rstd = _ln_stats(x_ref[...])
        dg = jnp.sum(dh0 * xh, axis=0, keepdims=True)
        db = jnp.sum(dh0, axis=0, keepdims=True)

        @pl.when(i == 0)
        def _():
            dg_ref[...] = dg
            db_ref[...] = db

        @pl.when(i > 0)
        def _():
            dg_ref[...] += dg
            db_ref[...] += db

        dx_ref[...] = _ln_bwd(dh0, xh, rstd, g_ref[...])

    row = pl.BlockSpec((1, d), lambda i: (0, 0))
    blk = pl.BlockSpec((tm, d), lambda i: (i, 0))
    return _call(
        body, (pre, dz1, x, g0), name="ln0_bwd", grid=(t // tm,),
        in_specs=[blk, blk, blk, row], out_specs=[blk, row, row],
        out_shape=[jax.ShapeDtypeStruct((t, d), F32), jax.ShapeDtypeStruct((1, d), F32),
                   jax.ShapeDtypeStruct((1, d), F32)])


def _shift_copies(ext, shifted):
    n = shifted.shape[1]
    for p in range(1, SUBLANE):
        shifted[p - 1] = ext[pl.ds(p, n), :]


def _window(ext, shifted, start, rows):
    p = start % SUBLANE
    if p == 0:
        return ext[pl.ds(start, rows), :]
    return shifted[p - 1, pl.ds(start - p, rows), :]


def _conv_fwd(p3, conv_w, conv_b, cn_g, cn_b, tc, cb, rider=None):
    _, t, w = p3.shape
    kk = conv_w.shape[0]
    off = HALO - (kk - 1)
    hb = tc // HALO

    def body(a_ref, g_ref, ap_ref, gp_ref, w_ref, b_ref, ng_ref, nb_ref, cat_ref, u1_ref, ext, sh):
        i = pl.program_id(1)
        ext[pl.ds(HALO, tc), :] = a_ref[...] * _sigmoid(g_ref[...])
        prev = ap_ref[...] * _sigmoid(gp_ref[...])
        ext[pl.ds(0, HALO), :] = jnp.where(i > 0, prev, 0.0)
        _shift_copies(ext, sh)
        for r in range(tc // ROWS):
            acc = jnp.broadcast_to(b_ref[...], (ROWS, cb))
            for k in range(kk):
                acc = acc + w_ref[k:k + 1, :] * _window(ext, sh, r * ROWS + off + k, ROWS)
            u1_ref[pl.ds(r * ROWS, ROWS), :] = acc
            for g in range(cb // LANE):
                sl = slice(g * LANE, (g + 1) * LANE)
                xh, _ = _ln_stats(acc[:, sl])
                u2 = xh * ng_ref[:, sl] + nb_ref[:, sl]
                cat_ref[pl.ds(r * ROWS, ROWS), sl] = (u2 * _sigmoid(u2)).astype(BF16)

    cur = lambda sec: pl.BlockSpec((None, tc, cb), lambda j, i: (sec, i, j))
    prev = lambda sec: pl.BlockSpec((None, HALO, cb), lambda j, i: (sec, jnp.maximum(i * hb - 1, 0), j))
    row = pl.BlockSpec((1, cb), lambda j, i: (0, j))
    return _call(
        body, (p3, p3, p3, p3, conv_w, conv_b, cn_g, cn_b), name="conv_fwd", grid=(w // cb, t // tc),
        in_specs=[cur(0), cur(1), prev(0), prev(1), pl.BlockSpec((kk, cb), lambda j, i: (0, j)), row, row, row],
        out_specs=[pl.BlockSpec((tc, cb), lambda j, i: (i, j)), pl.BlockSpec((tc, cb), lambda j, i: (i, j))],
        out_shape=[jax.ShapeDtypeStruct((t, 2 * w), BF16), jax.ShapeDtypeStruct((t, w), F32)],
        scratch_shapes=[pltpu.VMEM((tc + HALO, cb), F32),
                        pltpu.VMEM((SUBLANE - 1, tc + HALO - SUBLANE, cb), F32)], rider=rider)


def _conv_norm_bwd(dcat, u1, cn_g, cn_b, tc):
    t, w = u1.shape

    def body(du_ref, u1_ref, ng_ref, nb_ref, du1_ref, dg_ref, db_ref):
        i = pl.program_id(0)
        for g in range(w // LANE):
            sl = slice(g * LANE, (g + 1) * LANE)
            ng = ng_ref[:, sl]
            xh, rstd = _ln_stats(u1_ref[:, sl])
            u2 = xh * ng + nb_ref[:, sl]
            sg = _sigmoid(u2)
            du2 = du_ref[:, sl] * (sg * (1.0 + u2 * (1.0 - sg)))
            dg = jnp.sum(du2 * xh, axis=0, keepdims=True)
            db = jnp.sum(du2, axis=0, keepdims=True)

            @pl.when(i == 0)
            def _():
                dg_ref[:, sl] = dg
                db_ref[:, sl] = db

            @pl.when(i > 0)
            def _():
                dg_ref[:, sl] += dg
                db_ref[:, sl] += db

            du1_ref[:, sl] = _ln_bwd(du2, xh, rstd, ng)

    row = pl.BlockSpec((1, w), lambda i: (0, 0))
    blk = pl.BlockSpec((tc, w), lambda i: (i, 0))
    return pl.pallas_call(
        body, name="conv_norm_bwd", grid=(t // tc,),
        in_specs=[blk, blk, row, row], out_specs=[blk, row, row],
        out_shape=[jax.ShapeDtypeStruct((t, w), F32), jax.ShapeDtypeStruct((1, w), F32),
                   jax.ShapeDtypeStruct((1, w), F32)],
        compiler_params=_params("arbitrary"),
    )(dcat, u1, cn_g, cn_b)


def _conv_bwd(du1, p3, conv_w, tc, cb, rider=None):
    n_sec, t, w = p3.shape
    kk = conv_w.shape[0]
    off = HALO - (kk - 1)
    hb = tc // HALO
    nt = t // tc
    kpad = -(-kk // SUBLANE) * SUBLANE

    def body(d_ref, dn_ref, a_ref, g_ref, ap_ref, gp_ref, w_ref, dp_ref, dw_ref, db_ref,
             extd, extu, shd, shu, wacc, bacc):
        i = pl.program_id(1)

        @pl.when(i == 0)
        def _():
            wacc[...] = jnp.zeros_like(wacc)
            bacc[...] = jnp.zeros_like(bacc)

        extd[pl.ds(0, tc), :] = d_ref[...]
        extd[pl.ds(tc, HALO), :] = jnp.where(i < nt - 1, dn_ref[...], 0.0)
        extu[pl.ds(HALO, tc), :] = a_ref[...] * _sigmoid(g_ref[...])
        extu[pl.ds(0, HALO), :] = jnp.where(i > 0, ap_ref[...] * _sigmoid(gp_ref[...]), 0.0)
        _shift_copies(extd, shd)
        _shift_copies(extu, shu)
        for r in range(tc // ROWS):
            rows = pl.ds(r * ROWS, ROWS)
            acc = jnp.zeros((ROWS, cb), F32)
            for k in range(kk):
                acc = acc + w_ref[k:k + 1, :] * _window(extd, shd, r * ROWS + (kk - 1) - k, ROWS)
            a = a_ref[rows, :]
            sg = _sigmoid(g_ref[rows, :])
            dp_ref[0, rows, :] = (acc * sg).astype(BF16)
            dp_ref[1, rows, :] = (acc * a * sg * (1.0 - sg)).astype(BF16)
            d = d_ref[rows, :]
            bacc[...] += jnp.sum(d.reshape(ROWS // SUBLANE, SUBLANE, cb), axis=0)
            for k in range(kk):
                prod = d * _window(extu, shu, r * ROWS + off + k, ROWS)
                wacc[k] += jnp.sum(prod.reshape(ROWS // SUBLANE, SUBLANE, cb), axis=0)

        @pl.when(i == nt - 1)
        def _():
            for k in range(kk):
                dw_ref[k:k + 1, :] = jnp.sum(wacc[k], axis=0, keepdims=True)
            if kpad > kk:
                dw_ref[kk:kpad, :] = jnp.zeros((kpad - kk, cb), F32)
            db_ref[...] = jnp.sum(bacc[...], axis=0, keepdims=True)

    cur = lambda sec: pl.BlockSpec((None, tc, cb), lambda j, i: (sec, i, j))
    prev = lambda sec: pl.BlockSpec((None, HALO, cb), lambda j, i: (sec, jnp.maximum(i * hb - 1, 0), j))
    return _call(
        body, (du1, du1, p3, p3, p3, p3, conv_w), name="conv_bwd", grid=(w // cb, nt),
        in_specs=[pl.BlockSpec((tc, cb), lambda j, i: (i, j)),
                  pl.BlockSpec((HALO, cb), lambda j, i: (jnp.minimum((i + 1) * hb, t // HALO - 1), j)),
                  cur(0), cur(1), prev(0), prev(1), pl.BlockSpec((kk, cb), lambda j, i: (0, j))],
        out_specs=[pl.BlockSpec((2, tc, cb), lambda j, i: (0, i, j)),
                   pl.BlockSpec((kpad, cb), lambda j, i: (0, j)),
                   pl.BlockSpec((1, cb), lambda j, i: (0, j))],
        out_shape=[jax.ShapeDtypeStruct((n_sec, t, w), BF16), jax.ShapeDtypeStruct((kpad, w), F32),
                   jax.ShapeDtypeStruct((1, w), F32)],
        scratch_shapes=[pltpu.VMEM((tc + HALO, cb), F32), pltpu.VMEM((tc + HALO, cb), F32),
                        pltpu.VMEM((SUBLANE - 1, tc + HALO - SUBLANE, cb), F32),
                        pltpu.VMEM((SUBLANE - 1, tc + HALO - SUBLANE, cb), F32),
                        pltpu.VMEM((kk, SUBLANE, cb), F32), pltpu.VMEM((SUBLANE, cb), F32)], rider=rider)


def _ffn_act_fwd(hh3, fw, fb, tc, cb, rider=None):
    _, t, dff = hh3.shape
    kk = fw.shape[0]
    off = FHALO - (kk - 1)
    hb = tc // FHALO

    def body(g_ref, v_ref, gp_ref, w_ref, b_ref, act_ref, ext):
        i = pl.program_id(1)
        ext[pl.ds(FHALO, tc), :] = g_ref[...]
        ext[pl.ds(0, FHALO), :] = jnp.where(i > 0, gp_ref[...], 0.0)
        for r in range(tc // ROWS):
            rows = pl.ds(r * ROWS, ROWS)
            gc = jnp.broadcast_to(b_ref[...], (ROWS, cb))
            for k in range(kk):
                gc = gc + w_ref[k:k + 1, :] * ext[pl.ds(r * ROWS + off + k, ROWS), :]
            act_ref[rows, :] = (gc * _sigmoid(gc) * v_ref[rows, :]).astype(BF16)

    return _call(
        body, (hh3, hh3, hh3, fw, fb), name="ffn_act_fwd", grid=(dff // cb, t // tc),
        in_specs=[pl.BlockSpec((None, tc, cb), lambda j, i: (0, i, j)),
                  pl.BlockSpec((None, tc, cb), lambda j, i: (1, i, j)),
                  pl.BlockSpec((None, FHALO, cb), lambda j, i: (0, jnp.maximum(i * hb - 1, 0), j)),
                  pl.BlockSpec((kk, cb), lambda j, i: (0, j)),
                  pl.BlockSpec((1, cb), lambda j, i: (0, j))],
        out_specs=pl.BlockSpec((tc, cb), lambda j, i: (i, j)),
        out_shape=jax.ShapeDtypeStruct((t, dff), BF16),
        scratch_shapes=[pltpu.VMEM((tc + FHALO, cb), F32)], rider=rider)


def _ffn_act_bwd(dact, hh3, fw, fb, tc, cb):
    _, t, dff = hh3.shape
    kk = fw.shape[0]
    off = FHALO - (kk - 1)
    hb = tc // FHALO
    nt = t // tc
    te = tc + FHALO
    shifts = sorted({(off + k) % SUBLANE for k in range(kk)} - {0})

    def body(da_ref, dan_ref, g_ref, gp_ref, gn_ref, v_ref, vn_ref, w_ref, b_ref,
             dhh_ref, dw_ref, db_ref, gext, gsh, dext, wacc, bacc):
        i = pl.program_id(1)

        @pl.when(i == 0)
        def _():
            wacc[...] = jnp.zeros_like(wacc)
            bacc[...] = jnp.zeros_like(bacc)

        gext[pl.ds(0, FHALO), :] = jnp.where(i > 0, gp_ref[...], 0.0)
        gext[pl.ds(FHALO, tc), :] = g_ref[...]
        gext[pl.ds(FHALO + tc, FHALO), :] = gn_ref[...]
        for s, p in enumerate(shifts):
            gsh[s] = gext[pl.ds(p, te), :]

        def gwin(k, r0, n):
            p = (off + k) % SUBLANE
            if p == 0:
                return gext[pl.ds(r0 + off + k, n), :]
            return gsh[shifts.index(p), pl.ds(r0 + off + k - p, n), :]

        def gate_grad(r0, n, da, v):
            gc = jnp.broadcast_to(b_ref[...], (n, cb))
            for k in range(kk):
                gc = gc + w_ref[k:k + 1, :] * gwin(k, r0, n)
            sg = _sigmoid(gc)
            return gc * sg, da * v * (sg * (1.0 + gc * (1.0 - sg)))

        for r in range(tc // ROWS):
            rows = pl.ds(r * ROWS, ROWS)
            da = da_ref[rows, :]
            silu, dgc = gate_grad(r * ROWS, ROWS, da, v_ref[rows, :])
            dext[rows, :] = dgc
            dhh_ref[1, rows, :] = (da * silu).astype(BF16)
        _, dgc_next = gate_grad(tc, FHALO, dan_ref[...], vn_ref[...])
        dext[pl.ds(tc, FHALO), :] = jnp.where(i < nt - 1, dgc_next, 0.0)
        for r in range(tc // ROWS):
            rows = pl.ds(r * ROWS, ROWS)
            dg = jnp.zeros((ROWS, cb), F32)
            for k in range(kk):
                dg = dg + w_ref[k:k + 1, :] * dext[pl.ds(r * ROWS + (kk - 1) - k, ROWS), :]
            dhh_ref[0, rows, :] = dg.astype(BF16)
            dgc = dext[rows, :]
            bacc[...] += jnp.sum(dgc.reshape(ROWS // SUBLANE, SUBLANE, cb), axis=0)
            for k in range(kk):
                prod = dgc * gwin(k, r * ROWS, ROWS)
                wacc[k] += jnp.sum(prod.reshape(ROWS // SUBLANE, SUBLANE, cb), axis=0)

        @pl.when(i == nt - 1)
        def _():
            for k in range(kk):
                dw_ref[k:k + 1, :] = jnp.sum(wacc[k], axis=0, keepdims=True)
            dw_ref[kk:SUBLANE, :] = jnp.zeros((SUBLANE - kk, cb), F32)
            db_ref[...] = jnp.sum(bacc[...], axis=0, keepdims=True)

    nxt = lambda i: jnp.minimum((i + 1) * hb, t // FHALO - 1)
    return pl.pallas_call(
        body, name="ffn_act_bwd", grid=(dff // cb, nt),
        in_specs=[pl.BlockSpec((tc, cb), lambda j, i: (i, j)),
                  pl.BlockSpec((FHALO, cb), lambda j, i: (nxt(i), j)),
                  pl.BlockSpec((None, tc, cb), lambda j, i: (0, i, j)),
                  pl.BlockSpec((None, FHALO, cb), lambda j, i: (0, jnp.maximum(i * hb - 1, 0), j)),
                  pl.BlockSpec((None, FHALO, cb), lambda j, i: (0, nxt(i), j)),
                  pl.BlockSpec((None, tc, cb), lambda j, i: (1, i, j)),
                  pl.BlockSpec((None, FHALO, cb), lambda j, i: (1, nxt(i), j)),
                  pl.BlockSpec((kk, cb), lambda j, i: (0, j)),
                  pl.BlockSpec((1, cb), lambda j, i: (0, j))],
        out_specs=[pl.BlockSpec((2, tc, cb), lambda j, i: (0, i, j)),
                   pl.BlockSpec((SUBLANE, cb), lambda j, i: (0, j)),
                   pl.BlockSpec((1, cb), lambda j, i: (0, j))],
        out_shape=[jax.ShapeDtypeStruct((2, t, dff), BF16), jax.ShapeDtypeStruct((SUBLANE, dff), F32),
                   jax.ShapeDtypeStruct((1, dff), F32)],
        scratch_shapes=[pltpu.VMEM((tc + 2 * FHALO, cb), F32), pltpu.VMEM((len(shifts), te, cb), F32),
                        pltpu.VMEM((te, cb), F32),
                        pltpu.VMEM((kk, SUBLANE, cb), F32), pltpu.VMEM((SUBLANE, cb), F32)],
        compiler_params=_params("parallel", "arbitrary"),
    )(dact, dact, hh3, hh3, hh3, hh3, hh3, fw, fb)


def _chunk_consts():
    r = lax.broadcasted_iota(jnp.int32, (CHUNK, CHUNK), 0)
    c = lax.broadcasted_iota(jnp.int32, (CHUNK, CHUNK), 1)
    return (c <= r).astype(F32)


def _roll8(x, d):
    return pltpu.roll(x.reshape(CHUNK // SUB, SUB, LANE), d % SUB, 1).reshape(CHUNK, LANE)


def _gate_terms(q, fpre, lb):
    sf = _sigmoid(fpre)
    fg = lb + (1.0 - lb) * sf
    sq = _sigmoid(q)
    return sf, fg, 1.0 - fg, sq, q * sq


def _decays(g, consts):
    b = _dot3(consts, g)
    nb = CHUNK // SUB
    ends = b.reshape(nb, SUB, LANE)[:, SUB - 1:SUB, :]
    re3 = jnp.broadcast_to(ends, (nb, SUB, LANE))
    rs3 = jnp.concatenate([jnp.zeros((1, SUB, LANE), F32), re3[:nb - 1]], axis=0)
    return b, rs3.reshape(CHUNK, LANE), re3.reshape(CHUNK, LANE), b[CHUNK - 1:CHUNK]


def _lower_bound(lb_ref):
    l0, l1 = lb_ref[0:1, :], lb_ref[1:2, :]
    mx = jnp.maximum(l0, l1)
    e0, e1 = jnp.exp(l0 - mx), jnp.exp(l1 - mx)
    return e0 / (e0 + e1)


BF16_ROWS = 16


def _scaled_keys(kt, rs, re, i):
    n = SUB * i
    scale = jnp.exp(rs[n:n + 1, :] - re[:n])
    live = kt[:n] * scale
    m = -(-n // BF16_ROWS) * BF16_ROWS
    if m > n:
        live = jnp.concatenate([live, jnp.zeros((m - n, LANE), F32)], axis=0)
    parts = [live.astype(BF16)]
    if m < CHUNK:
        parts.append(jnp.zeros((CHUNK - m, LANE), BF16))
    return jnp.concatenate(parts, axis=0), scale


def _only_block(x, i):
    w0 = (SUB * i // BF16_ROWS) * BF16_ROWS
    win = x[w0:w0 + BF16_ROWS]
    inblk = lax.broadcasted_iota(jnp.int32, (BF16_ROWS, 1), 0) // SUB == (SUB * i - w0) // SUB
    parts = [jnp.where(inblk, win, 0.0).astype(BF16)]
    if w0:
        parts.insert(0, jnp.zeros((w0, x.shape[1]), BF16))
    if w0 + BF16_ROWS < CHUNK:
        parts.append(jnp.zeros((CHUNK - w0 - BF16_ROWS, x.shape[1]), BF16))
    return jnp.concatenate(parts, axis=0)


def _hgrn_fwd(p3, lb_logits, hg, cat, tb, hpb, rider=None):
    _, t, w = p3.shape
    nh = w // LANE
    nc = tb // CHUNK
    assert nh % hpb == 0

    def body(q_ref, f_ref, v_ref, og_ref, lb_ref, hg_ref, cat_in, cat_ref, o_ref, st_ref, state):
        del cat_in
        consts = _chunk_consts()
        lb_all = _lower_bound(lb_ref)
        rowpos = lax.broadcasted_iota(jnp.int32, (CHUNK, 1), 0) % SUB

        @pl.when(pl.program_id(1) == 0)
        def _():
            state[...] = jnp.zeros_like(state)

        def chunk(c, carry):
            rows = pl.ds(pl.multiple_of(c * CHUNK, CHUNK), CHUNK)
            heads = range(hpb)
            sls = [slice(j * LANE, (j + 1) * LANE) for j in heads]
            v = [v_ref[rows, s] for s in sls]
            vb = [x.astype(BF16) for x in v]
            gates = [_gate_terms(q_ref[rows, s], f_ref[rows, s], lb_all[:, s]) for s in sls]
            fg = [g[1] for g in gates]
            kk = [g[2] for g in gates]
            qh = [g[4] for g in gates]
            dec = [_decays(jnp.log(x), consts) for x in fg]
            b = [x[0] for x in dec]
            rs = [x[1] for x in dec]
            re = [x[2] for x in dec]
            tot = [x[3] for x in dec]
            qt = [qh[j] * jnp.exp(b[j] - rs[j]) for j in heads]
            kt = [kk[j] * jnp.exp(re[j] - b[j]) for j in heads]
            st = [state[j] for j in heads]
            for j in heads:
                st_ref[j, c] = st[j]
            a = [jnp.zeros((CHUNK, CHUNK), F32) for _ in heads]
            for i in range(1, CHUNK // SUB):
                for j in heads:
                    kib, _ = _scaled_keys(kt[j], rs[j], re[j], i)
                    a[j] = a[j] + _dot_nt(_only_block(qt[j], i), kib)
            o = [_dot(a[j].astype(BF16), vb[j]) for j in heads]
            o = [o[j] + _dot_nt((qh[j] * jnp.exp(b[j])).astype(BF16), st[j].astype(BF16)) for j in heads]
            for j in heads:
                k_up = kk[j] * jnp.exp(tot[j] - b[j])
                state[j] = st[j] * jnp.exp(tot[j]) + _dot_tn(vb[j], k_up.astype(BF16))
            for j in heads:
                e, rf = None, fg[j]
                for d in range(SUB):
                    if d == 0:
                        vs, term = v[j], qh[j] * kk[j]
                    else:
                        e = rf if e is None else e * rf
                        rf = _roll8(fg[j], d)
                        vs = _roll8(v[j], d)
                        term = jnp.where(rowpos >= d, qh[j] * (1.0 - rf) * e, 0.0)
                    o[j] = o[j] + jnp.sum(term, axis=-1, keepdims=True) * vs
            for j in heads:
                og = og_ref[rows, sls[j]]
                o_ref[rows, sls[j]] = o[j]
                r = lax.rsqrt(jnp.mean(o[j] * o[j], axis=-1, keepdims=True) + RMS_EPS)
                cat_ref[rows, sls[j]] = (o[j] * r * hg_ref[:, sls[j]] * (og * _sigmoid(og))).astype(BF16)
            return carry

        lax.fori_loop(0, nc, chunk, 0)

    bw = hpb * LANE
    sec = lambda s: pl.BlockSpec((None, tb, bw), lambda h, i: (s, i, h))
    return _call(
        body, (p3, p3, p3, p3, lb_logits, hg, cat), name="hgrn_fwd", grid=(nh // hpb, t // tb),
        in_specs=[sec(2), sec(3), sec(4), sec(5),
                  pl.BlockSpec((2, bw), lambda h, i: (0, h)),
                  pl.BlockSpec((1, bw), lambda h, i: (0, h)), HBM],
        out_specs=[pl.BlockSpec((tb, bw), lambda h, i: (i, nh // hpb + h)),
                   pl.BlockSpec((tb, bw), lambda h, i: (i, h)),
                   pl.BlockSpec((hpb, nc, LANE, LANE), lambda h, i: (h, i, 0, 0))],
        out_shape=[jax.ShapeDtypeStruct(cat.shape, BF16), jax.ShapeDtypeStruct((t, w), F32),
                   jax.ShapeDtypeStruct((nh, t // CHUNK, LANE, LANE), F32)],
        scratch_shapes=[pltpu.VMEM((hpb, LANE, LANE), F32)], aliases={6: 0}, rider=rider)


def _hgrn_bwd(p3, lb_logits, hg, o_pre, states, dcat, dp3, tb, hpb, rider=None):
    n_sec, t, w = p3.shape
    nh = w // LANE
    assert nh % hpb == 0
    nc = tb // CHUNK
    nb = t // tb
    bw = hpb * LANE
    n_steps = (nh // hpb) * nb

    def body(q_ref, f_ref, v_ref, og_ref, lb_ref, hg_ref, o_ref, st_ref, dc_ref, dp_in,
             dp_ref, dlb_ref, dhg_ref, dstate, stash, lbacc, hgacc, osem):
        del dp_in
        h, i = pl.program_id(0), pl.program_id(1)
        step = h * nb + i
        slot = step % 2

        def out_copy(s, row_blk, lane_blk):
            dst = dp_ref.at[pl.ds(2, 4), pl.ds(row_blk * tb, tb), pl.ds(lane_blk * bw, bw)]
            return pltpu.make_async_copy(stash.at[s], dst, osem.at[s])

        @pl.when(step >= 2)
        def _():
            out_copy(slot, 0, 0).wait()

        def compute():
            consts = _chunk_consts()
            rr = lax.broadcasted_iota(jnp.int32, (CHUNK, CHUNK), 0)
            cc = lax.broadcasted_iota(jnp.int32, (CHUNK, CHUNK), 1)
            upper = (cc >= rr).astype(F32)
            lb_all = _lower_bound(lb_ref)
            rowpos = lax.broadcasted_iota(jnp.int32, (CHUNK, 1), 0) % SUB

            @pl.when(i == 0)
            def _():
                dstate[...] = jnp.zeros_like(dstate)
                lbacc[...] = jnp.zeros_like(lbacc)
                hgacc[...] = jnp.zeros_like(hgacc)

            def head(j, c, rows):
                sl = slice(j * LANE, (j + 1) * LANE)
                lb = lb_all[:, sl]
                hgv = hg_ref[:, sl]
                q = q_ref[rows, sl]
                v = v_ref[rows, sl]
                og = og_ref[rows, sl]
                o = o_ref[rows, sl]
                dcg = dc_ref[rows, sl]
                sf, fg, kk, sq, qh = _gate_terms(q, f_ref[rows, sl], lb)
                b, rs, re, tot = _decays(jnp.log(fg), consts)
                eq = jnp.exp(b - rs)
                ek = jnp.exp(re - b)
                qt = qh * eq
                kt = kk * ek
                e_in = jnp.exp(b)
                e_up = jnp.exp(tot - b)
                e_tot = jnp.exp(tot)
                q_in = (qh * e_in).astype(BF16)
                k_up = (kk * e_up).astype(BF16)
                vb = v.astype(BF16)
                st = st_ref[j, c]
                dst = dstate[j]
                dstb = dst.astype(BF16)
                yield

                sg = _sigmoid(og)
                r = lax.rsqrt(jnp.mean(o * o, axis=-1, keepdims=True) + RMS_EPS)
                ohat = o * r
                d_og = dcg * ohat * hgv * (sg * (1.0 + og * (1.0 - sg)))
                d_on = dcg * (og * sg)
                hgacc[:, sl] += jnp.sum((d_on * ohat).reshape(CHUNK // SUBLANE, SUBLANE, LANE), axis=0)
                d_oh = d_on * hgv
                do = r * (d_oh - ohat * jnp.mean(d_oh * ohat, axis=-1, keepdims=True))
                dob = do.astype(BF16)

                da = _dot_nt(dob, vb)
                yield
                a_off = jnp.zeros((CHUNK, CHUNK), F32)
                dqt = jnp.zeros((CHUNK, LANE), F32)
                dkt = jnp.zeros((CHUNK, LANE), F32)
                for blk in range(1, CHUNK // SUB):
                    n = SUB * blk
                    kib, scale = _scaled_keys(kt, rs, re, blk)
                    qib = _only_block(qt, blk)
                    dab = _only_block(da, blk)
                    a_off = a_off + _dot_nt(qib, kib)
                    dqt = dqt + _dot(dab, kib)
                    dkt = jnp.concatenate([dkt[:n] + _dot_tn(dab, qib)[:n] * scale, dkt[n:]], axis=0)
                    yield
                dqh = dqt * eq
                dk = dkt * ek
                dv = _dot_tn(a_off.astype(BF16), dob)

                dqh = dqh + _dot(dob, st.astype(BF16)) * e_in
                dk = dk + _dot(vb, dstb) * e_up
                dv = dv + _dot_nt(k_up, dstb)
                st_end = st * e_tot + _dot_tn(vb, k_up)
                carry_g = jnp.sum(st_end * dst, axis=0, keepdims=True)
                dstate[j] = dst * e_tot + _dot_tn(dob, q_in)
                yield

                e, rf = None, fg
                for d in range(SUB):
                    if d == 0:
                        a_d = jnp.sum(qh * kk, axis=-1, keepdims=True)
                        da_d = jnp.sum(do * v, axis=-1, keepdims=True)
                        dqh = dqh + da_d * kk
                        dk = dk + da_d * qh
                        dv = dv + a_d * do
                        continue
                    e = rf if e is None else e * rf
                    rf = _roll8(fg, d)
                    em = jnp.where(rowpos >= d, e, 0.0)
                    ks, vs = 1.0 - rf, _roll8(v, d)
                    a_d = jnp.sum(qh * ks * em, axis=-1, keepdims=True)
                    da_d = jnp.sum(do * vs, axis=-1, keepdims=True) * em
                    dqh = dqh + da_d * ks
                    dk = dk + _roll8(da_d * qh, -d)
                    dv = dv + _roll8(a_d * do, -d)
                yield

                dg = _dot3(upper, qh * dqh - kk * dk) + carry_g
                dfg = dg / fg - dk
                lbacc[:, sl] += jnp.sum((dfg * (1.0 - sf)).reshape(CHUNK // SUBLANE, SUBLANE, LANE), axis=0)
                stash[slot, 0, rows, sl] = (dqh * (sq * (1.0 + q * (1.0 - sq)))).astype(BF16)
                stash[slot, 1, rows, sl] = (dfg * (1.0 - lb) * sf * (1.0 - sf)).astype(BF16)
                stash[slot, 2, rows, sl] = dv.astype(BF16)
                stash[slot, 3, rows, sl] = d_og.astype(BF16)

            def chunk(cr, carry):
                c = nc - 1 - cr
                rows = pl.ds(pl.multiple_of(c * CHUNK, CHUNK), CHUNK)
                running = [head(j, c, rows) for j in range(hpb)]
                while running:
                    running = [g for g in running if next(g, StopIteration) is not StopIteration]
                return carry

            lax.fori_loop(0, nc, chunk, 0)

            @pl.when(i == nb - 1)
            def _():
                dlb_ref[...] = jnp.sum(lbacc[...], axis=0, keepdims=True)
                dhg_ref[...] = jnp.sum(hgacc[...], axis=0, keepdims=True)

        compute()
        out_copy(slot, nb - 1 - i, h).start()

        @pl.when(step == n_steps - 1)
        def _():
            out_copy(slot, 0, 0).wait()
            if n_steps >= 2:
                out_copy(1 - slot, 0, 0).wait()

    rev = lambda i: nb - 1 - i
    sec = lambda s: pl.BlockSpec((None, tb, bw), lambda h, i: (s, rev(i), h))
    return _call(
        body, (p3, p3, p3, p3, lb_logits, hg, o_pre, states, dcat, dp3), name="hgrn_bwd", grid=(nh // hpb, nb),
        in_specs=[sec(2), sec(3), sec(4), sec(5),
                  pl.BlockSpec((2, bw), lambda h, i: (0, h)),
                  pl.BlockSpec((1, bw), lambda h, i: (0, h)),
                  pl.BlockSpec((tb, bw), lambda h, i: (rev(i), h)),
                  pl.BlockSpec((hpb, nc, LANE, LANE), lambda h, i: (h, rev(i), 0, 0)),
                  pl.BlockSpec((tb, bw), lambda h, i: (rev(i), nh // hpb + h)), HBM],
        out_specs=[HBM,
                   pl.BlockSpec((1, bw), lambda h, i: (0, h)),
                   pl.BlockSpec((1, bw), lambda h, i: (0, h))],
        out_shape=[jax.ShapeDtypeStruct((n_sec, t, w), BF16), jax.ShapeDtypeStruct((1, w), F32),
                   jax.ShapeDtypeStruct((1, w), F32)],
        scratch_shapes=[pltpu.VMEM((hpb, LANE, LANE), F32), pltpu.VMEM((2, 4, tb, bw), BF16),
                        pltpu.VMEM((SUBLANE, bw), F32), pltpu.VMEM((SUBLANE, bw), F32),
                        pltpu.SemaphoreType.DMA((2,))],
        aliases={9: 0}, rider=rider)


def _place():
    x, y, c = lax.axis_index("x"), lax.axis_index("y"), lax.axis_index("c")
    chips = [(1 - x, y), (x, 1 - y), (1 - x, 1 - y)]
    return x, y, c, chips


def _rows(buf, px, py, pc, part=None):
    half = buf.shape[1] // 2
    if part is None:
        return buf.at[2 * px + py, pl.ds(pc * half, half)]
    lo, hi, n = part
    piece = half // n
    return buf.at[2 * px + py, pl.ds(pc * half + lo * piece, (hi - lo) * piece)]


def _rcopy(src, dst, send, recv, idx, to):
    return pltpu.make_async_remote_copy(src_ref=src, dst_ref=dst, send_sem=send.at[idx], recv_sem=recv.at[idx],
                                        device_id=to, device_id_type=MESH)


def _same(bufs):
    return [jax.ShapeDtypeStruct(b.shape, b.dtype) for b in bufs]


def _ride_gather_ici(bufs, parts=None):
    n = len(bufs)
    parts = parts or [None] * n

    def start(rin, rout, send, recv):
        x, y, c, chips = _place()
        for k in range(n):
            mine = _rows(rout[k], x, y, c, parts[k])
            for j, chip in enumerate(chips):
                _rcopy(mine, mine, send, recv, 3 * k + j, (*chip, c)).start()

    def finish(rin, rout, send, recv):
        x, y, c, chips = _place()
        for k in range(n):
            for j, chip in enumerate(chips):
                theirs = _rows(rout[k], *chip, c, parts[k])
                _rcopy(theirs, theirs, send, recv, 3 * k + j, (x, y, c)).wait_recv()
        for k in range(n):
            mine = _rows(rout[k], x, y, c, parts[k])
            for j in range(3):
                _rcopy(mine, mine, send, recv, 3 * k + j, (x, y, c)).wait_send()

    return _Rider(bufs, _same(bufs), {k: k for k in range(n)}, 3 * n, start, finish)


class _SemView:
    def __init__(self, ref, base):
        self.ref, self.base = ref, base

    @property
    def at(self):
        return self

    def __getitem__(self, idx):
        return self.ref.at[idx + self.base]


def _ride_both(a, b):
    nai, nao = len(a.ins), len(a.outs)

    def start(rin, rout, send, recv):
        a.start(rin[:nai], rout[:nao], send, recv)
        b.start(rin[nai:], rout[nao:], _SemView(send, a.n_sems), _SemView(recv, a.n_sems))

    def finish(rin, rout, send, recv):
        a.finish(rin[:nai], rout[:nao], send, recv)
        b.finish(rin[nai:], rout[nao:], _SemView(send, a.n_sems), _SemView(recv, a.n_sems))

    aliases = dict(a.aliases)
    aliases.update({nai + ri: nao + ro for ri, ro in b.aliases.items()})
    return _Rider(a.ins + b.ins, a.outs + b.outs, aliases, a.n_sems + b.n_sems, start, finish)


def _ride_gather_d2d(bufs):
    n = len(bufs)

    def start(rin, rout, send, recv):
        x, y, c, chips = _place()
        for k in range(n):
            for j, chip in enumerate(chips):
                got = _rows(rout[k], *chip, c)
                _rcopy(got, got, send, recv, 3 * k + j, (x, y, 1 - c)).start()

    def finish(rin, rout, send, recv):
        x, y, c, chips = _place()
        for k in range(n):
            for j, chip in enumerate(chips):
                theirs = _rows(rout[k], *chip, 1 - c)
                _rcopy(theirs, theirs, send, recv, 3 * k + j, (x, y, c)).wait_recv()
        for k in range(n):
            for j, chip in enumerate(chips):
                got = _rows(rout[k], *chip, c)
                _rcopy(got, got, send, recv, 3 * k + j, (x, y, c)).wait_send()

    return _Rider(bufs, _same(bufs), {k: k for k in range(n)}, 3 * n, start, finish)


def _ride_swap(grads):
    n = len(grads)

    def copy(k, rin, rout, send, recv):
        x, y, c, _ = _place()
        half = rin[k].shape[1] // 2
        return _rcopy(rin[k].at[:, pl.ds((1 - c) * half, half)], rout[k], send, recv, k, (x, y, 1 - c))

    def start(rin, rout, send, recv):
        for k in range(n):
            copy(k, rin, rout, send, recv).start()

    def finish(rin, rout, send, recv):
        for k in range(n):
            copy(k, rin, rout, send, recv).wait()

    outs = [jax.ShapeDtypeStruct((g.shape[0], g.shape[1] // 2, g.shape[2]), g.dtype) for g in grads]
    return _Rider(grads, outs, {}, n, start, finish)


def _ride_send_partials(parts, pieces=None, into=None):
    n = len(parts)
    pieces = pieces or [None] * n

    def cut(ref, k):
        if pieces[k] is None:
            return ref
        lo, hi, m = pieces[k]
        q = ref.shape[0] // m
        return ref.at[pl.ds(lo * q, (hi - lo) * q)]

    def copies(rin, rout, send, recv):
        x, y, c, chips = _place()
        return [_rcopy(cut(rin[k].at[2 * px + py], k), cut(rout[k].at[j], k), send, recv, 3 * k + j, (px, py, c))
                for k in range(n) for j, (px, py) in enumerate(chips)]

    def start(rin, rout, send, recv):
        for cp in copies(rin, rout, send, recv):
            cp.start()

    def finish(rin, rout, send, recv):
        for cp in copies(rin, rout, send, recv):
            cp.wait()

    if into is None:
        outs = [jax.ShapeDtypeStruct((3,) + p.shape[1:], p.dtype) for p in parts]
        return _Rider(parts, outs, {}, 3 * n, start, finish)
    return _Rider(list(parts) + list(into), _same(into), {n + k: k for k in range(n)}, 3 * n, start, finish)


def _ride_join(bufs):
    n = len(bufs)

    def half_of(buf, pc):
        half = buf.shape[0] // 2
        return buf.at[pl.ds(pc * half, half)]

    def start(rin, rout, send, recv):
        x, y, c, _ = _place()
        for k in range(n):
            mine = half_of(rout[k], c)
            _rcopy(mine, mine, send, recv, k, (x, y, 1 - c)).start()

    def finish(rin, rout, send, recv):
        x, y, c, _ = _place()
        for k in range(n):
            mine, theirs = half_of(rout[k], c), half_of(rout[k], 1 - c)
            _rcopy(mine, mine, send, recv, k, (x, y, c)).wait_send()
            _rcopy(theirs, theirs, send, recv, k, (x, y, c)).wait_recv()

    return _Rider(bufs, _same(bufs), {k: k for k in range(n)}, n, start, finish)


def _run(name, rider):
    def body(*refs):
        nri, nro = len(rider.ins), len(rider.outs)
        rin, rout = refs[:nri], refs[nri:nri + nro]
        send, recv = refs[nri + nro:]
        rider.start(rin, rout, send, recv)
        rider.finish(rin, rout, send, recv)

    return pl.pallas_call(
        body, name=name, in_specs=[HBM] * len(rider.ins), out_specs=[HBM] * len(rider.outs), out_shape=rider.outs,
        scratch_shapes=[pltpu.SemaphoreType.DMA((rider.n_sems,)), pltpu.SemaphoreType.DMA((rider.n_sems,))],
        input_output_aliases=rider.aliases,
    )(*rider.ins)


def _add_halves(name, g, other, c_idx):
    s, r, cols = g.shape
    half = r // 2
    tr = _div_tile(half, 16, 512)
    nb = half // tr

    def body(c_ref, g_ref, o_ref, q_ref):
        del c_ref
        q_ref[...] = (g_ref[...] + o_ref[...]).astype(BF16)

    return pl.pallas_call(
        body, name=name,
        grid_spec=pltpu.PrefetchScalarGridSpec(
            num_scalar_prefetch=1, grid=(s, nb),
            in_specs=[pl.BlockSpec((None, tr, cols), lambda k, i, c: (k, c[0] * nb + i, 0)),
                      pl.BlockSpec((None, tr, cols), lambda k, i, c: (k, i, 0))],
            out_specs=pl.BlockSpec((None, tr, cols), lambda k, i, c: (k, i, 0))),
        out_shape=jax.ShapeDtypeStruct((s, half, cols), BF16),
        compiler_params=_params("parallel", "parallel"),
    )(c_idx, g, other)


def _sum_partials(name, part, arrived, place_idx):
    _, half, cols = part.shape
    tr = _div_tile(half, 16, 512)
    nb = half // tr

    def body(s_ref, p_ref, a_ref, o_ref):
        del s_ref
        o_ref[...] = ((p_ref[...].astype(F32) + a_ref[0].astype(F32)) + a_ref[1].astype(F32)) + a_ref[2].astype(F32)

    return pl.pallas_call(
        body, name=name,
        grid_spec=pltpu.PrefetchScalarGridSpec(
            num_scalar_prefetch=1, grid=(nb,),
            in_specs=[pl.BlockSpec((None, tr, cols), lambda i, s: (s[0], i, 0)),
                      pl.BlockSpec((3, tr, cols), lambda i, s: (0, i, 0))],
            out_specs=pl.BlockSpec((tr, cols), lambda i, s: (s[1] * nb + i, 0))),
        out_shape=jax.ShapeDtypeStruct((2 * half, cols), F32),
        compiler_params=_params("parallel"),
    )(place_idx, part, arrived)


def _pack_small(name, wide_rows, ffn_rows, w, dff, n_wide, n_ffn):
    n_in = len(wide_rows) + len(ffn_rows)

    def body(*refs):
        ins, outs = refs[:n_in], refs[n_in:]
        p1 = outs[0]
        p1[...] = jnp.zeros_like(p1)
        row = 0
        for ref, (_, r, m) in zip(ins, wide_rows):
            if m == 1 and r % SUBLANE == 0 and row % SUBLANE == 0:
                p1[row:row + r, :] = ref[...]
                row += r
                continue
            for rr in range(r):
                for mm in range(m):
                    p1[row:row + 1, :] = ref[rr:rr + 1, mm * w:(mm + 1) * w]
                    row += 1
        if ffn_rows:
            p2 = outs[1]
            p2[...] = jnp.zeros_like(p2)
            row = 0
            for ref, arr in zip(ins[len(wide_rows):], ffn_rows):
                r = arr.shape[0]
                p2[row:row + r, :] = ref[...]
                row += r

    shapes = [jax.ShapeDtypeStruct((n_wide, w), F32)] + ([jax.ShapeDtypeStruct((n_ffn, dff), F32)] if ffn_rows else [])
    return pl.pallas_call(
        body, name=name, in_specs=[VMEM_FULL] * n_in, out_specs=[VMEM_FULL] * len(shapes), out_shape=shapes,
        compiler_params=pltpu.CompilerParams(vmem_limit_bytes=VMEM_LIMIT),
    )(*[a for a, _, _ in wide_rows], *ffn_rows)


def _ride_exchange8(packs):
    n = len(packs)

    def copies(rin, rout, send, recv):
        x, y, c, _ = _place()
        me = 4 * x + 2 * y + c
        out = []
        for a in range(n):
            for mask in range(1, 8):
                peer = (x ^ (mask >> 2), y ^ ((mask >> 1) & 1), c ^ (mask & 1))
                out.append(_rcopy(rin[a], rout[a].at[me], send, recv, 8 * a + mask, peer))
        own = [pltpu.make_async_copy(rin[a], rout[a].at[me], send.at[8 * a]) for a in range(n)]
        return out, own

    def start(rin, rout, send, recv):
        remote, own = copies(rin, rout, send, recv)
        for cp in remote + own:
            cp.start()

    def finish(rin, rout, send, recv):
        remote, own = copies(rin, rout, send, recv)
        for cp in remote + own:
            cp.wait()

    outs = [jax.ShapeDtypeStruct((8,) + p.shape, p.dtype) for p in packs]
    return _Rider(packs, outs, {}, 8 * n, start, finish)


def _sum_small(name, slots):
    def body(*refs):
        n = len(refs) // 2
        for r_ref, s_ref in zip(refs[:n], refs[n:]):
            tot = r_ref[0]
            for d in range(1, 8):
                tot = tot + r_ref[d]
            s_ref[...] = tot

    return pl.pallas_call(
        body, name=name, in_specs=[VMEM_FULL] * len(slots), out_specs=[VMEM_FULL] * len(slots),
        out_shape=[jax.ShapeDtypeStruct(s.shape[1:], s.dtype) for s in slots],
        compiler_params=pltpu.CompilerParams(vmem_limit_bytes=VMEM_LIMIT),
    )(*slots)


def _adamw(w, g, m, v):
    m2 = ADAM_B1 * m + (1.0 - ADAM_B1) * g
    v2 = ADAM_B2 * v + (1.0 - ADAM_B2) * (g * g)
    m_hat = m2 / (1.0 - ADAM_B1 ** ADAM_STEP)
    v_hat = v2 / (1.0 - ADAM_B2 ** ADAM_STEP)
    delta = -ADAM_LR * (m_hat / (jnp.sqrt(v_hat) + ADAM_EPS) + ADAM_WD * w)
    return delta, m2, v2


def _adam_big(name, w, g, m, v):
    r, c = w.shape
    tr = 128 if r % 128 == 0 else r

    def body(w_ref, g_ref, m_ref, v_ref, go_ref, d_ref, m2_ref, v2_ref):
        g = g_ref[...]
        go_ref[...] = g
        d_ref[...], m2_ref[...], v2_ref[...] = _adamw(w_ref[...], g, m_ref[...], v_ref[...])

    blk = pl.BlockSpec((tr, c), lambda i: (i, 0))
    return _call(
        body, (w, g, m, v), name=name, grid=(r // tr,), in_specs=[blk] * 4, out_specs=[blk] * 4,
        out_shape=[jax.ShapeDtypeStruct((r, c), F32)] * 4)


def _adam_small(s1, s2, s3, cw_g, fw_g, lb_logits, triples, layout, w):
    n = len(triples)

    def body(*refs):
        s1_ref, s2_ref, s3_ref, cw_ref, fw_ref, lbl_ref = refs[:6]
        prm = refs[6:6 + 3 * n]
        outs = refs[6 + 3 * n:]
        for p, lay in enumerate(layout):
            w_ref, m_ref, v_ref = prm[3 * p:3 * p + 3]
            g_ref, d_ref, m2_ref, v2_ref = outs[4 * p:4 * p + 4]
            if lay[0] in ("wide", "late"):
                _, row, r, pieces = lay
                src = s1_ref if lay[0] == "wide" else s3_ref
                for rr in range(r):
                    for mm in range(pieces):
                        g_ref[rr:rr + 1, mm * w:(mm + 1) * w] = src[row:row + 1, :]
                        row += 1
            elif lay[0] == "ffn":
                _, row, r = lay
                g_ref[...] = s2_ref[row:row + r, :]
            elif lay[0] == "cw":
                g_ref[...] = cw_ref[0:g_ref.shape[0], :]
            elif lay[0] == "fw":
                g_ref[...] = fw_ref[0:g_ref.shape[0], :]
            else:
                s0 = _lower_bound(lbl_ref)
                d0 = s1_ref[lay[1]:lay[1] + 1, :] * s0 * (1.0 - s0)
                g_ref[0:1, :] = d0
                g_ref[1:2, :] = -d0
            d_ref[...], m2_ref[...], v2_ref[...] = _adamw(w_ref[...], g_ref[...], m_ref[...], v_ref[...])

    flat = [a for tr in triples for a in tr]
    shapes = []
    for tr in triples:
        shapes.extend([jax.ShapeDtypeStruct(tr[0].shape, F32)] * 4)
    return pl.pallas_call(
        body, name="adam_small", in_specs=[VMEM_FULL] * (6 + 3 * n), out_specs=[VMEM_FULL] * (4 * n),
        out_shape=shapes, compiler_params=pltpu.CompilerParams(vmem_limit_bytes=VMEM_LIMIT),
    )(s1, s2, s3, cw_g, fw_g, lb_logits, *flat)


def _row_tile(t):
    return 512 if t % 512 == 0 and t >= 2048 else 128


def kernel(x, emb_ln_g, emb_ln_b, w_in, conv_w, conv_b, conv_norm_g, conv_norm_b, lb_logits, hgrn_norm_g, w_out, ln1_g, ln1_b, w_ffn_up, ffn_conv_w, ffn_conv_b, w_ffn_down, ln2_g, ln2_b, loss_target, m_emb_ln_g, m_emb_ln_b, m_w_in, m_conv_w, m_conv_b, m_conv_norm_g, m_conv_norm_b, m_lb_logits, m_hgrn_norm_g, m_w_out, m_ln1_g, m_ln1_b, m_w_ffn_up, m_ffn_conv_w, m_ffn_conv_b, m_w_ffn_down, m_ln2_g, m_ln2_b, v_emb_ln_g, v_emb_ln_b, v_w_in, v_conv_w, v_conv_b, v_conv_norm_g, v_conv_norm_b, v_lb_logits, v_hgrn_norm_g, v_w_out, v_ln1_g, v_ln1_b, v_w_ffn_up, v_ffn_conv_w, v_ffn_conv_b, v_w_ffn_down, v_ln2_g, v_ln2_b):
    depth = w_in.shape[0]
    assert depth == 1 and x.shape[0] == 1
    alpha = (2.0 * depth) ** 0.25
    t, d = x.shape[1], x.shape[2]
    w = d // 2
    dff = ffn_conv_b.shape[1]
    kc = conv_w.shape[1]
    assert w % (2 * LANE) == 0 and dff % (4 * LANE) == 0 and t % 128 == 0
    tm = _row_tile(t)
    tm2 = tm // 2
    tmm = 1024 if t % 1024 == 0 and t >= 2048 else tm
    cb = 2 * LANE
    cbf = 4 * LANE
    tb = tm
    nh = w // LANE
    hpb = 4 if nh % 4 == 0 else 2

    xi = lax.axis_index("x")
    yi = lax.axis_index("y")
    ci = lax.axis_index("c")
    chip = 2 * xi + yi
    c_idx = jnp.reshape(ci, (1,)).astype(jnp.int32)
    chip_idx = jnp.reshape(chip, (1,)).astype(jnp.int32)
    place_idx = jnp.stack([chip, ci]).astype(jnp.int32)

    x2 = x[0]
    tgt = loss_target[0]
    g0, b0 = emb_ln_g.reshape(1, d), emb_ln_b.reshape(1, d)
    w_in2, w_out2, w_up2, w_dn2 = w_in[0], w_out[0], w_ffn_up[0], w_ffn_down[0]
    cw2, fw2 = conv_w[0], ffn_conv_w[0]

    b_in = _place_shard(w_in2, "place_w_in", chip_idx, BF16)
    b_out = _place_shard(w_out2, "place_w_out", chip_idx, BF16)
    b_up = _place_shard(w_up2, "place_w_up", chip_idx, BF16)
    b_dn = _place_shard(w_dn2, "place_w_down", chip_idx, BF16)
    b_cw = _place_shard(_pad_rows(cw2), "place_conv_w", chip_idx, F32)
    b_fw = _place_shard(_pad_rows(fw2), "place_ffn_conv_w", chip_idx, F32)
    h0b, (b_in,) = _ln0(x2, g0, b0, tm, rider=_ride_gather_ici([b_in], [(0, 1, 8)]))
    first = _run("gather_first_ici", _ride_gather_ici([b_in, b_cw, b_fw], [(1, 8, 8), None, None]))
    w_in3, cw_full3, fw_full3 = _run("gather_first_d2d", _ride_gather_d2d(first))
    cw_full = _unshard_cols(cw_full3)[:kc]
    fw_full = _unshard_cols(fw_full3)[:fw2.shape[0]]

    p3, (b_out, b_up) = _proj("in_proj", h0b, w_in3, 6, 2 * tmm if t % (2 * tmm) == 0 else tmm, w // 2,
                              rider=_ride_gather_ici([b_out, b_up], [None, (0, 1, 4)]))
    (cat, u1), (w_out3, b_up) = _conv_fwd(
        p3, cw_full, conv_b, conv_norm_g, conv_norm_b, tm2, cb,
        rider=_ride_both(_ride_gather_d2d([b_out]), _ride_gather_ici([b_up], [(1, 2, 4)])))
    w_out_full = w_out3.reshape(d, d)
    (cat, o_pre, states), got = _hgrn_fwd(p3, lb_logits, hgrn_norm_g, cat, tb, hpb,
                                          rider=_ride_gather_ici([b_up], [(2, 4, 4)]))
    (xhat1, h1b, rstd1), (w_up3,) = _mix_ln1(cat, w_out_full, x2, g0, b0, ln1_g, ln1_b, alpha, tm2,
                                             rider=_ride_gather_d2d(got))
    hh3, got = _proj("ffn_up", h1b, w_up3, 2, tmm, dff // 4, rider=_ride_gather_ici([b_dn]))
    act, (w_dn3,) = _ffn_act_fwd(hh3, fw_full, ffn_conv_b, tm, cbf, rider=_ride_gather_d2d(got))
    ks = dff // N_CHIPS
    ffn = _wgrad("ffn_down", act, w_dn3, (t, d), (t // tmm, 1, N_CHIPS),
                 pl.BlockSpec((tmm, ks), lambda i, j, k: (i, k)),
                 pl.BlockSpec((None, ks, d), lambda i, j, k: (k, 0, 0)),
                 pl.BlockSpec((tmm, d), lambda i, j, k: (i, 0)), dot=_dot)
    dz2, dz2b, dg2, db2, loss_row = _ln2_loss(ffn, xhat1, tgt, ln1_g, ln1_b, ln2_g, ln2_b, alpha, tm2)

    dact = _proj_t("ffn_down_t", dz2b, w_dn3.reshape(dff, d), tmm, ks)
    dhh3, dfw, dfb = _ffn_act_bwd(dact, hh3, fw_full, ffn_conv_b, tm, cbf)
    tt = 2 * tmm if t % (2 * tmm) == 0 else tmm
    d_w_dn = _wgrad("wgrad_down", act, dz2b, (N_CHIPS, ks, d), (N_CHIPS, 2, t // tt),
                    pl.BlockSpec((tt, ks), lambda s, j, k: (k, s)),
                    pl.BlockSpec((tt, d // 2), lambda s, j, k: (k, j)),
                    pl.BlockSpec((None, ks, d // 2), lambda s, j, k: (s, 0, j)))
    wu = 2 * dff // N_CHIPS
    tnu = wu // 2
    per_sec_u = dff // tnu
    pre1, (arr_dn,) = _wgrad(
        "up_t", dhh3, w_up3, (t, d), (t // tmm, 1, 2 * N_CHIPS),
        pl.BlockSpec((None, tmm, tnu), lambda i, j, k: (k // per_sec_u, i, k % per_sec_u)),
        pl.BlockSpec((None, d, tnu), lambda i, j, k: (k // 2, 0, k % 2)),
        pl.BlockSpec((tmm, d), lambda i, j, k: (i, 0)), dot=_dot_nt, rider=_ride_swap([d_w_dn]))
    dz1, dz1b, dg1, db1 = _ln1_bwd(pre1, dz2, xhat1, rstd1, ln1_g, alpha, tm2)
    part_dn = _add_halves("add_halves_w_down", d_w_dn, arr_dn, c_idx)
    d_w_up, (land_dn,) = _wgrad(
        "wgrad_up", h1b, dhh3, (N_CHIPS, d, wu), (N_CHIPS, 2, 2, t // tt),
        pl.BlockSpec((tt, d // 2), lambda s, r, j, k: (k, r)),
        pl.BlockSpec((None, tt, tnu), lambda s, r, j, k: ((2 * s + j) // per_sec_u, k, (2 * s + j) % per_sec_u)),
        pl.BlockSpec((None, d // 2, tnu), lambda s, r, j, k: (s, r, j)), rider=_ride_send_partials([part_dn]))
    dcat = _proj_t("out_proj_t", dz1b, w_out_full, tmm, d // 2)
    d_w_out = _wgrad("wgrad_out", cat, dz1b, (d, d), (2, 2, t // tt),
                     pl.BlockSpec((tt, d // 2), lambda r, j, k: (k, r)),
                     pl.BlockSpec((tt, d // 2), lambda r, j, k: (k, j)),
                     pl.BlockSpec((d // 2, d // 2), lambda r, j, k: (r, j))).reshape(N_CHIPS, d // N_CHIPS, d)
    du1, dcng, dcnb = _conv_norm_bwd(dcat, u1, conv_norm_g, conv_norm_b, tm)
    (dp3, dcw, dcb), (arr_up, arr_out) = _conv_bwd(du1, p3, cw_full, tm2, cb, rider=_ride_swap([d_w_up, d_w_out]))
    part_up = _add_halves("add_halves_w_up", d_w_up, arr_up, c_idx)
    part_out = _add_halves("add_halves_w_out", d_w_out, arr_out, c_idx)
    (dp3, dlb, dhg), (land_up,) = _hgrn_bwd(
        p3, lb_logits, hgrn_norm_g, o_pre, states, dcat, dp3, tb, hpb,
        rider=_ride_send_partials([part_up], [(0, 3, 4)]))
    kpad = dcw.shape[0]
    wide = [(dcw, kpad, 1), (dg1, 1, 2), (db1, 1, 2), (dg2, 1, 2), (db2, 1, 2),
            (dcb, 1, 1), (dcng, 1, 1), (dcnb, 1, 1), (dlb, 1, 1), (dhg, 1, 1)]
    n_wide = -(-sum(r * m for _, r, m in wide) // SUBLANE) * SUBLANE
    packs = _pack_small("pack_small", wide, [dfw, dfb], w, dff, n_wide, 2 * SUBLANE)
    d_w_in, (land_up, land_out, slots1, slots2) = _wgrad_in(
        h0b, dp3, N_CHIPS, tt,
        rider=_ride_both(_ride_both(_ride_send_partials([part_up], [(3, 4, 4)], into=[land_up]),
                                    _ride_send_partials([part_out])), _ride_exchange8(packs)))
    s1, s2 = _sum_small("sum_small", [slots1, slots2])
    (arr_in,) = _run("swap_w_in", _ride_swap([d_w_in]))
    part_in = _add_halves("add_halves_w_in", d_w_in, arr_in, c_idx)
    h_out, h_up, h_dn = [
        _sum_partials("sum_partials_" + nm, p, a, place_idx)
        for nm, p, a in (("w_out", part_out, land_out), ("w_up", part_up, land_up), ("w_down", part_dn, land_dn))]
    pre0, (land_in, g_w_out, g_w_up, g_w_dn) = _in_t(
        dp3, w_in3, tmm, rider=_ride_both(_ride_send_partials([part_in]), _ride_join([h_out, h_up, h_dn])))
    dx, dg0, db0 = _ln0_bwd(pre0, dz1, x2, g0, alpha, tm2)
    h_in = _sum_partials("sum_partials_w_in", part_in, land_in, place_idx)
    (g_w_in,) = _run("join_w_in", _ride_join([h_in]))

    late = _pack_small("pack_late", [(dg0, 1, 2), (db0, 1, 2)], [], w, dff, SUBLANE, 0)
    (s3,) = _sum_small("sum_late", _run("exchange_late", _ride_exchange8(late)))
    cw_g = lax.dynamic_slice_in_dim(s1[0:kpad], chip * (w // N_CHIPS), w // N_CHIPS, axis=1)
    fw_g = lax.dynamic_slice_in_dim(s2[0:SUBLANE], chip * (dff // N_CHIPS), dff // N_CHIPS, axis=1)

    small = [
        (g0, m_emb_ln_g.reshape(1, d), v_emb_ln_g.reshape(1, d)), (b0, m_emb_ln_b.reshape(1, d), v_emb_ln_b.reshape(1, d)),
        (cw2, m_conv_w[0], v_conv_w[0]), (conv_b, m_conv_b, v_conv_b),
        (conv_norm_g, m_conv_norm_g, v_conv_norm_g), (conv_norm_b, m_conv_norm_b, v_conv_norm_b),
        (lb_logits, m_lb_logits, v_lb_logits), (hgrn_norm_g, m_hgrn_norm_g, v_hgrn_norm_g),
        (ln1_g, m_ln1_g, v_ln1_g), (ln1_b, m_ln1_b, v_ln1_b),
        (fw2, m_ffn_conv_w[0], v_ffn_conv_w[0]), (ffn_conv_b, m_ffn_conv_b, v_ffn_conv_b),
        (ln2_g, m_ln2_g, v_ln2_g), (ln2_b, m_ln2_b, v_ln2_b),
    ]
    r0 = kpad
    layout = [("late", 0, 1, 2), ("late", 2, 1, 2), ("cw",), ("wide", r0 + 8, 1, 1), ("wide", r0 + 9, 1, 1),
              ("wide", r0 + 10, 1, 1), ("lb", r0 + 11), ("wide", r0 + 12, 1, 1), ("wide", r0, 1, 2),
              ("wide", r0 + 2, 1, 2), ("fw",), ("ffn", SUBLANE, 1), ("wide", r0 + 4, 1, 2), ("wide", r0 + 6, 1, 2)]
    so = _adam_small(s1, s2, s3, cw_g, fw_g, lb_logits, small, layout, w)
    sm = {nm: so[4 * i:4 * i + 4] for i, nm in enumerate(
        ["emb_ln_g", "emb_ln_b", "conv_w", "conv_b", "conv_norm_g", "conv_norm_b", "lb_logits", "hgrn_norm_g",
         "ln1_g", "ln1_b", "ffn_conv_w", "ffn_conv_b", "ln2_g", "ln2_b"])}
    bigs = {}
    for nm, wt, g, m, v in (("w_in", w_in2, g_w_in, m_w_in[0], v_w_in[0]), ("w_out", w_out2, g_w_out, m_w_out[0], v_w_out[0]),
                            ("w_ffn_up", w_up2, g_w_up, m_w_ffn_up[0], v_w_ffn_up[0]),
                            ("w_ffn_down", w_dn2, g_w_dn, m_w_ffn_down[0], v_w_ffn_down[0])):
        bigs[nm] = tuple(_adam_big("adam_" + nm, wt, g, m, v))

    loss = lax.psum(loss_row[0, 0], ("x", "y", "c"))

    order = ["emb_ln_g", "emb_ln_b", "w_in", "conv_w", "conv_b", "conv_norm_g", "conv_norm_b", "lb_logits",
             "hgrn_norm_g", "w_out", "ln1_g", "ln1_b", "w_ffn_up", "ffn_conv_w", "ffn_conv_b", "w_ffn_down",
             "ln2_g", "ln2_b"]
    shapes = dict(emb_ln_g=emb_ln_g.shape, emb_ln_b=emb_ln_b.shape, w_in=w_in.shape, conv_w=conv_w.shape,
                  w_out=w_out.shape, w_ffn_up=w_ffn_up.shape, ffn_conv_w=ffn_conv_w.shape, w_ffn_down=w_ffn_down.shape)
    outs = [loss, dx.reshape(x.shape)]
    for which in range(4):
        for nm in order:
            a = bigs[nm][which] if nm in bigs else sm[nm][which]
            outs.append(a.reshape(shapes[nm]) if nm in shapes else a)
    return tuple(outs)


def _pad_rows(a):
    k = a.shape[0]
    kp = -(-k // 16) * 16
    return jnp.pad(a, ((0, kp - k), (0, 0)))


def _unshard_cols(a3):
    s, k, c = a3.shape
    return jnp.transpose(a3, (1, 0, 2)).reshape(k, s * c)
```

```python
import functools

import jax
import jax.numpy as jnp
from jax import lax
from jax.experimental import pallas as pl
from jax.experimental.pallas import tpu as pltpu

F32 = jnp.float32
BF16 = jnp.bfloat16

LN_EPS = 1e-5
RMS_EPS = 1e-6
LANE = 128
SUBLANE = 8
CHUNK = 64
SUB = 8
HALO = 32
FHALO = 8
ROWS = 64
N_CHIPS = 4
VMEM_LIMIT = 56 << 20
NEG_BIG = -1e30

ADAM_LR = 0.001
ADAM_B1 = 0.9
ADAM_B2 = 0.999
ADAM_EPS = 1e-08
ADAM_WD = 0.01
ADAM_STEP = 10

MESH = pl.DeviceIdType.MESH
HBM = pl.BlockSpec(memory_space=pl.ANY)
VMEM_FULL = pl.BlockSpec(memory_space=pltpu.VMEM)


def _params(*sem):
    return pltpu.CompilerParams(dimension_semantics=sem, vmem_limit_bytes=VMEM_LIMIT)


class _Rider:
    def __init__(self, ins, outs, aliases, n_sems, start, finish):
        self.ins, self.outs, self.aliases = list(ins), list(outs), dict(aliases)
        self.n_sems, self.start, self.finish = n_sems, start, finish


def _call(body, args, *, name, grid, in_specs, out_specs, out_shape, scratch_shapes=(), aliases=None, rider=None):
    many = isinstance(out_shape, (list, tuple))
    shapes = list(out_shape) if many else [out_shape]
    ospecs = list(out_specs) if many else [out_specs]
    aliases = dict(aliases or {})
    sem = ("arbitrary",) * len(grid)
    if rider is None:
        res = pl.pallas_call(
            body, name=name, grid=grid, in_specs=list(in_specs), out_specs=ospecs, out_shape=shapes,
            scratch_shapes=list(scratch_shapes), input_output_aliases=aliases, compiler_params=_params(*sem))(*args)
        return res if many else res[0]
    n_in, n_out, n_scr = len(args), len(shapes), len(scratch_shapes)
    nri, nro = len(rider.ins), len(rider.outs)

    def wrapped(*refs):
        ins, rin = refs[:n_in], refs[n_in:n_in + nri]
        o0 = n_in + nri
        outs, rout = refs[o0:o0 + n_out], refs[o0 + n_out:o0 + n_out + nro]
        s0 = o0 + n_out + nro
        scr, (send, recv) = refs[s0:s0 + n_scr], refs[s0 + n_scr:]
        ids = [pl.program_id(a) for a in range(len(grid))]
        first = functools.reduce(jnp.logical_and, [i == 0 for i in ids])
        last = functools.reduce(jnp.logical_and, [i == g - 1 for i, g in zip(ids, grid)])

        @pl.when(first)
        def _():
            rider.start(rin, rout, send, recv)

        body(*ins, *outs, *scr)

        @pl.when(last)
        def _():
            rider.finish(rin, rout, send, recv)

    for ri, ro in rider.aliases.items():
        aliases[n_in + ri] = n_out + ro
    res = pl.pallas_call(
        wrapped, name=name, grid=grid, in_specs=list(in_specs) + [HBM] * nri, out_specs=ospecs + [HBM] * nro,
        out_shape=shapes + rider.outs,
        scratch_shapes=list(scratch_shapes) + [pltpu.SemaphoreType.DMA((rider.n_sems,)),
                                               pltpu.SemaphoreType.DMA((rider.n_sems,))],
        input_output_aliases=aliases, compiler_params=_params(*sem))(*args, *rider.ins)
    main, extra = res[:n_out], list(res[n_out:])
    return (list(main) if many else main[0]), extra


def _div_tile(n, mult, cap):
    best = n
    for t in range(mult, min(n, cap) + 1, mult):
        if n % t == 0:
            best = t
    return best


def _sigmoid(x):
    return 1.0 / (1.0 + jnp.exp(-x))


def _ln_stats(x):
    mu = jnp.mean(x, axis=-1, keepdims=True)
    xc = x - mu
    var = jnp.mean(xc * xc, axis=-1, keepdims=True)
    rstd = lax.rsqrt(var + LN_EPS)
    return xc * rstd, rstd


def _ln_bwd(dy, xhat, rstd, g):
    dyg = dy * g
    m1 = jnp.mean(dyg, axis=-1, keepdims=True)
    m2 = jnp.mean(dyg * xhat, axis=-1, keepdims=True)
    return rstd * (dyg - m1 - xhat * m2)


def _dot_nt(a, b):
    return lax.dot_general(a, b, (((1,), (1,)), ((), ())), preferred_element_type=F32)


def _dot_tn(a, b):
    return lax.dot_general(a, b, (((0,), (0,)), ((), ())), preferred_element_type=F32)


def _dot(a, b):
    return jnp.dot(a, b, preferred_element_type=F32)


def _dot3(m, x):
    mb = m.astype(BF16)
    x1 = x.astype(BF16)
    r1 = x - x1.astype(F32)
    x2 = r1.astype(BF16)
    x3 = (r1 - x2.astype(F32)).astype(BF16)
    return _dot(mb, x1) + _dot(mb, x2) + _dot(mb, x3)


def _place_shard(x, name, chip_idx, dtype):
    r, c = x.shape
    tr = _div_tile(r, 16, 512)

    def body(s_ref, x_ref, o_ref):
        del s_ref
        o_ref[...] = x_ref[...].astype(dtype)

    return pl.pallas_call(
        body, name=name,
        grid_spec=pltpu.PrefetchScalarGridSpec(
            num_scalar_prefetch=1, grid=(r // tr,),
            in_specs=[pl.BlockSpec((tr, c), lambda i, s: (i, 0))],
            out_specs=pl.BlockSpec((None, tr, c), lambda i, s: (s[0], i, 0))),
        out_shape=jax.ShapeDtypeStruct((N_CHIPS, r, c), dtype),
        compiler_params=_params("parallel"),
    )(chip_idx, x)


def _ln0(x, g, b, tm, rider=None):
    t, d = x.shape

    def body(x_ref, g_ref, b_ref, o_ref):
        xh, _ = _ln_stats(x_ref[...])
        o_ref[...] = (xh * g_ref[...] + b_ref[...]).astype(BF16)

    row = pl.BlockSpec((1, d), lambda i: (0, 0))
    return _call(
        body, (x, g, b), name="ln0", grid=(t // tm,),
        in_specs=[pl.BlockSpec((tm, d), lambda i: (i, 0)), row, row],
        out_specs=pl.BlockSpec((tm, d), lambda i: (i, 0)),
        out_shape=jax.ShapeDtypeStruct((t, d), BF16), rider=rider)


def _proj(name, a, w3, n_sec, tm, tn, rider=None):
    m, k = a.shape
    s, _, ws = w3.shape
    sec_w = s * ws // n_sec
    nj = ws // tn
    per_sec = sec_w // tn

    def body(a_ref, w_ref, o_ref):
        o_ref[...] = _dot(a_ref[...], w_ref[...])

    return _call(
        body, (a, w3), name=name, grid=(s * nj, m // tm),
        in_specs=[pl.BlockSpec((tm, k), lambda j, i: (i, 0)),
                  pl.BlockSpec((None, k, tn), lambda j, i: (j // nj, 0, j % nj))],
        out_specs=pl.BlockSpec((None, tm, tn), lambda j, i: (j // per_sec, i, j % per_sec)),
        out_shape=jax.ShapeDtypeStruct((n_sec, m, sec_w), F32), rider=rider)


def _proj_t(name, a, w, tm, tn, rider=None):
    m, k = a.shape
    n = w.shape[0]

    def body(a_ref, w_ref, o_ref):
        o_ref[...] = _dot_nt(a_ref[...], w_ref[...])

    return _call(
        body, (a, w), name=name, grid=(n // tn, m // tm),
        in_specs=[pl.BlockSpec((tm, k), lambda j, i: (i, 0)),
                  pl.BlockSpec((tn, k), lambda j, i: (j, 0))],
        out_specs=pl.BlockSpec((tm, tn), lambda j, i: (i, j)),
        out_shape=jax.ShapeDtypeStruct((m, n), F32), rider=rider)


def _wgrad(name, a, b, out_shape, grid, a_spec, b_spec, o_spec, rider=None, dot=_dot_tn):
    nt = len(grid) - 1

    def body(a_ref, b_ref, o_ref):
        t = pl.program_id(nt)
        prod = dot(a_ref[...], b_ref[...])

        @pl.when(t == 0)
        def _():
            o_ref[...] = prod

        @pl.when(t > 0)
        def _():
            o_ref[...] += prod

    return _call(
        body, (a, b), name=name, grid=grid, in_specs=[a_spec, b_spec], out_specs=o_spec,
        out_shape=jax.ShapeDtypeStruct(out_shape, F32), rider=rider)


def _wgrad_in(h0b, dp3, n_shards, tt, rider=None):
    t, d = h0b.shape
    n_sec, _, sec_w = dp3.shape
    ws = n_sec * sec_w // n_shards
    tn = sec_w // 2
    nq = ws // tn
    tr = d // 2

    def body(*refs):
        a_ref, b_refs, o_ref = refs[0], refs[1:1 + nq], refs[1 + nq]
        k = pl.program_id(2)
        a = a_ref[...]
        prods = [_dot_tn(a, b_ref[...]) for b_ref in b_refs]

        @pl.when(k == 0)
        def _():
            for q in range(nq):
                o_ref[:, q * tn:(q + 1) * tn] = prods[q]

        @pl.when(k > 0)
        def _():
            for q in range(nq):
                o_ref[:, q * tn:(q + 1) * tn] += prods[q]

    def b_spec(q):
        return pl.BlockSpec((None, tt, tn), lambda s, r, k: ((nq * s + q) // 2, k, (nq * s + q) % 2))

    return _call(
        body, (h0b,) + (dp3,) * nq, name="wgrad_in", grid=(n_shards, d // tr, t // tt),
        in_specs=[pl.BlockSpec((tt, tr), lambda s, r, k: (k, r))] + [b_spec(q) for q in range(nq)],
        out_specs=pl.BlockSpec((None, tr, ws), lambda s, r, k: (s, r, 0)),
        out_shape=jax.ShapeDtypeStruct((n_shards, d, ws), F32), rider=rider)


def _in_t(dp3, w_in3, tm, rider=None):
    _, t, sec_w = dp3.shape
    s, d, ws = w_in3.shape
    tk = sec_w // 2
    nq = ws // tk

    def body(*refs):
        a_refs, w_ref, o_ref = refs[:nq], refs[nq], refs[nq + 1]
        k = pl.program_id(1)
        prod = _dot_nt(a_refs[0][...], w_ref[:, 0:tk])
        for q in range(1, nq):
            prod = prod + _dot_nt(a_refs[q][...], w_ref[:, q * tk:(q + 1) * tk])

        @pl.when(k == 0)
        def _():
            o_ref[...] = prod

        @pl.when(k > 0)
        def _():
            o_ref[...] += prod

    def a_spec(q):
        return pl.BlockSpec((None, tm, tk), lambda i, k: ((nq * k + q) // 2, i, (nq * k + q) % 2))

    return _call(
        body, (dp3,) * nq + (w_in3,), name="in_t", grid=(t // tm, s),
        in_specs=[a_spec(q) for q in range(nq)] + [pl.BlockSpec((None, d, ws), lambda i, k: (k, 0, 0))],
        out_specs=pl.BlockSpec((tm, d), lambda i, k: (i, 0)),
        out_shape=jax.ShapeDtypeStruct((t, d), F32), rider=rider)


def _mix_ln1(cat, w_out, x, g0, b0, g1, b1, alpha, tm, rider=None):
    t, d = x.shape

    def body(cat_ref, w_ref, x_ref, g0_ref, b0_ref, g1_ref, b1_ref, xh_ref, h1b_ref, rstd_ref):
        mix = _dot(cat_ref[...], w_ref[...])
        xh0, _ = _ln_stats(x_ref[...])
        z1 = alpha * (xh0 * g0_ref[...] + b0_ref[...]) + mix
        xh1, rstd1 = _ln_stats(z1)
        xh_ref[...] = xh1
        h1b_ref[...] = (xh1 * g1_ref[...] + b1_ref[...]).astype(BF16)
        rstd_ref[...] = rstd1

    row = pl.BlockSpec((1, d), lambda i: (0, 0))
    blk = pl.BlockSpec((tm, d), lambda i: (i, 0))
    return _call(
        body, (cat, w_out, x, g0, b0, g1, b1), name="mix_ln1", grid=(t // tm,),
        in_specs=[blk, pl.BlockSpec((d, d), lambda i: (0, 0)), blk, row, row, row, row],
        out_specs=[blk, blk, pl.BlockSpec((tm, 1), lambda i: (i, 0))],
        out_shape=[jax.ShapeDtypeStruct((t, d), F32), jax.ShapeDtypeStruct((t, d), BF16),
                   jax.ShapeDtypeStruct((t, 1), F32)], rider=rider)


def _ln2_loss(ffn, xhat1, tgt, g1, b1, g2, b2, alpha, tm):
    t, d = xhat1.shape
    ni = t // tm
    inv_d = 1.0 / d

    def body(ffn_ref, xh1_ref, tgt_ref, g1_ref, b1_ref, g2_ref, b2_ref,
             dz2_ref, dz2b_ref, dg2_ref, db2_ref, loss_ref, lrow):
        i = pl.program_id(0)
        h1 = xh1_ref[...] * g1_ref[...] + b1_ref[...]
        xh2, rstd2 = _ln_stats(alpha * h1 + ffn_ref[...])
        g2v = g2_ref[...]
        diff = xh2 * g2v + b2_ref[...] - tgt_ref[...]
        dh2 = diff * inv_d
        sq = jnp.sum(diff * diff, axis=0, keepdims=True)
        dg = jnp.sum(dh2 * xh2, axis=0, keepdims=True)
        db = jnp.sum(dh2, axis=0, keepdims=True)

        @pl.when(i == 0)
        def _():
            lrow[...] = sq
            dg2_ref[...] = dg
            db2_ref[...] = db

        @pl.when(i > 0)
        def _():
            lrow[...] += sq
            dg2_ref[...] += dg
            db2_ref[...] += db

        dz2 = _ln_bwd(dh2, xh2, rstd2, g2v)
        dz2_ref[...] = dz2
        dz2b_ref[...] = dz2.astype(BF16)

        @pl.when(i == ni - 1)
        def _():
            tot = jnp.sum(lrow[...], axis=-1, keepdims=True) * (0.5 * inv_d)
            loss_ref[...] = jnp.broadcast_to(tot, (1, LANE))

    row = pl.BlockSpec((1, d), lambda i: (0, 0))
    blk = pl.BlockSpec((tm, d), lambda i: (i, 0))
    return _call(
        body, (ffn, xhat1, tgt, g1, b1, g2, b2), name="ln2_loss", grid=(ni,),
        in_specs=[blk, blk, blk, row, row, row, row],
        out_specs=[blk, blk, row, row, pl.BlockSpec((1, LANE), lambda i: (0, 0))],
        out_shape=[jax.ShapeDtypeStruct((t, d), F32), jax.ShapeDtypeStruct((t, d), BF16),
                   jax.ShapeDtypeStruct((1, d), F32), jax.ShapeDtypeStruct((1, d), F32),
                   jax.ShapeDtypeStruct((1, LANE), F32)],
        scratch_shapes=[pltpu.VMEM((1, d), F32)])


def _ln1_bwd(pre, dz2, xhat1, rstd1, g1, alpha, tm):
    t, d = dz2.shape

    def body(pre_ref, dz2_ref, xh_ref, rstd_ref, g_ref, dz1_ref, dz1b_ref, dg_ref, db_ref):
        i = pl.program_id(0)
        dh1 = alpha * dz2_ref[...] + pre_ref[...]
        xh = xh_ref[...]
        dg = jnp.sum(dh1 * xh, axis=0, keepdims=True)
        db = jnp.sum(dh1, axis=0, keepdims=True)

        @pl.when(i == 0)
        def _():
            dg_ref[...] = dg
            db_ref[...] = db

        @pl.when(i > 0)
        def _():
            dg_ref[...] += dg
            db_ref[...] += db

        dz1 = _ln_bwd(dh1, xh, rstd_ref[...], g_ref[...])
        dz1_ref[...] = dz1
        dz1b_ref[...] = dz1.astype(BF16)

    row = pl.BlockSpec((1, d), lambda i: (0, 0))
    blk = pl.BlockSpec((tm, d), lambda i: (i, 0))
    return _call(
        body, (pre, dz2, xhat1, rstd1, g1), name="ln1_bwd", grid=(t // tm,),
        in_specs=[blk, blk, blk, pl.BlockSpec((tm, 1), lambda i: (i, 0)), row],
        out_specs=[blk, blk, row, row],
        out_shape=[jax.ShapeDtypeStruct((t, d), F32), jax.ShapeDtypeStruct((t, d), BF16),
                   jax.ShapeDtypeStruct((1, d), F32), jax.ShapeDtypeStruct((1, d), F32)])


def _ln0_bwd(pre, dz1, x, g0, alpha, tm):
    t, d = x.shape

    def body(pre_ref, dz1_ref, x_ref, g_ref, dx_ref, dg_ref, db_ref):
        i = pl.program_id(0)
        dh0 = alpha * dz1_ref[...] + pre_ref[...]
        xh, rstd = _ln_stats(x_ref[...])
        dg = jnp.sum(dh0 * xh, axis=0, keepdims=True)
        db = jnp.sum(dh0, axis=0, keepdims=True)

        @pl.when(i == 0)
        def _():
            dg_ref[...] = dg
            db_ref[...] = db

        @pl.when(i > 0)
        def _():
            dg_ref[...] += dg
            db_ref[...] += db

        dx_ref[...] = _ln_bwd(dh0, xh, rstd, g_ref[...])

    row = pl.BlockSpec((1, d), lambda i: (0, 0))
    blk = pl.BlockSpec((tm, d), lambda i: (i, 0))
    return _call(
        body, (pre, dz1, x, g0), name="ln0_bwd", grid=(t // tm,),
        in_specs=[blk, blk, blk, row], out_specs=[blk, row, row],
        out_shape=[jax.ShapeDtypeStruct((t, d), F32), jax.ShapeDtypeStruct((1, d), F32),
                   jax.ShapeDtypeStruct((1, d), F32)])


def _shift_copies(ext, shifted):
    n = shifted.shape[1]
    for p in range(1, SUBLANE):
        shifted[p - 1] = ext[pl.ds(p, n), :]


def _window(ext, shifted, start, rows):
    p = start % SUBLANE
    if p == 0:
        return ext[pl.ds(start, rows), :]
    return shifted[p - 1, pl.ds(start - p, rows), :]


def _conv_fwd(p3, conv_w, conv_b, cn_g, cn_b, tc, cb, rider=None):
    _, t, w = p3.shape
    kk = conv_w.shape[0]
    off = HALO - (kk - 1)
    hb = tc // HALO

    def body(a_ref, g_ref, ap_ref, gp_ref, w_ref, b_ref, ng_ref, nb_ref, cat_ref, u1_ref, ext, sh):
        i = pl.program_id(1)
        ext[pl.ds(HALO, tc), :] = a_ref[...] * _sigmoid(g_ref[...])
        prev = ap_ref[...] * _sigmoid(gp_ref[...])
        ext[pl.ds(0, HALO), :] = jnp.where(i > 0, prev, 0.0)
        _shift_copies(ext, sh)
        for r in range(tc // ROWS):
            acc = jnp.broadcast_to(b_ref[...], (ROWS, cb))
            for k in range(kk):
                acc = acc + w_ref[k:k + 1, :] * _window(ext, sh, r * ROWS + off + k, ROWS)
            u1_ref[pl.ds(r * ROWS, ROWS), :] = acc
            for g in range(cb // LANE):
                sl = slice(g * LANE, (g + 1) * LANE)
                xh, _ = _ln_stats(acc[:, sl])
                u2 = xh * ng_ref[:, sl] + nb_ref[:, sl]
                cat_ref[pl.ds(r * ROWS, ROWS), sl] = (u2 * _sigmoid(u2)).astype(BF16)

    cur = lambda sec: pl.BlockSpec((None, tc, cb), lambda j, i: (sec, i, j))
    prev = lambda sec: pl.BlockSpec((None, HALO, cb), lambda j, i: (sec, jnp.maximum(i * hb - 1, 0), j))
    row = pl.BlockSpec((1, cb), lambda j, i: (0, j))
    return _call(
        body, (p3, p3, p3, p3, conv_w, conv_b, cn_g, cn_b), name="conv_fwd", grid=(w // cb, t // tc),
        in_specs=[cur(0), cur(1), prev(0), prev(1), pl.BlockSpec((kk, cb), lambda j, i: (0, j)), row, row, row],
        out_specs=[pl.BlockSpec((tc, cb), lambda j, i: (i, j)), pl.BlockSpec((tc, cb), lambda j, i: (i, j))],
        out_shape=[jax.ShapeDtypeStruct((t, 2 * w), BF16), jax.ShapeDtypeStruct((t, w), F32)],
        scratch_shapes=[pltpu.VMEM((tc + HALO, cb), F32),
                        pltpu.VMEM((SUBLANE - 1, tc + HALO - SUBLANE, cb), F32)], rider=rider)


def _conv_norm_bwd(dcat, u1, cn_g, cn_b, tc):
    t, w = u1.shape

    def body(du_ref, u1_ref, ng_ref, nb_ref, du1_ref, dg_ref, db_ref):
        i = pl.program_id(0)
        for g in range(w // LANE):
            sl = slice(g * LANE, (g + 1) * LANE)
            ng = ng_ref[:, sl]
            xh, rstd = _ln_stats(u1_ref[:, sl])
            u2 = xh * ng + nb_ref[:, sl]
            sg = _sigmoid(u2)
            du2 = du_ref[:, sl] * (sg * (1.0 + u2 * (1.0 - sg)))
            dg = jnp.sum(du2 * xh, axis=0, keepdims=True)
            db = jnp.sum(du2, axis=0, keepdims=True)

            @pl.when(i == 0)
            def _():
                dg_ref[:, sl] = dg
                db_ref[:, sl] = db

            @pl.when(i > 0)
            def _():
                dg_ref[:, sl] += dg
                db_ref[:, sl] += db

            du1_ref[:, sl] = _ln_bwd(du2, xh, rstd, ng)

    row = pl.BlockSpec((1, w), lambda i: (0, 0))
    blk = pl.BlockSpec((tc, w), lambda i: (i, 0))
    return pl.pallas_call(
        body, name="conv_norm_bwd", grid=(t // tc,),
        in_specs=[blk, blk, row, row], out_specs=[blk, row, row],
        out_shape=[jax.ShapeDtypeStruct((t, w), F32), jax.ShapeDtypeStruct((1, w), F32),
                   jax.ShapeDtypeStruct((1, w), F32)],
        compiler_params=_params("arbitrary"),
    )(dcat, u1, cn_g, cn_b)


def _conv_bwd(du1, p3, conv_w, tc, cb, rider=None):
    n_sec, t, w = p3.shape
    kk = conv_w.shape[0]
    off = HALO - (kk - 1)
    hb = tc // HALO
    nt = t // tc
    kpad = -(-kk // SUBLANE) * SUBLANE

    def body(d_ref, dn_ref, a_ref, g_ref, ap_ref, gp_ref, w_ref, dp_ref, dw_ref, db_ref,
             extd, extu, shd, shu, wacc, bacc):
        i = pl.program_id(1)

        @pl.when(i == 0)
        def _():
            wacc[...] = jnp.zeros_like(wacc)
            bacc[...] = jnp.zeros_like(bacc)

        extd[pl.ds(0, tc), :] = d_ref[...]
        extd[pl.ds(tc, HALO), :] = jnp.where(i < nt - 1, dn_ref[...], 0.0)
        extu[pl.ds(HALO, tc), :] = a_ref[...] * _sigmoid(g_ref[...])
        extu[pl.ds(0, HALO), :] = jnp.where(i > 0, ap_ref[...] * _sigmoid(gp_ref[...]), 0.0)
        _shift_copies(extd, shd)
        _shift_copies(extu, shu)
        for r in range(tc // ROWS):
            rows = pl.ds(r * ROWS, ROWS)
            acc = jnp.zeros((ROWS, cb), F32)
            for k in range(kk):
                acc = acc + w_ref[k:k + 1, :] * _window(extd, shd, r * ROWS + (kk - 1) - k, ROWS)
            a = a_ref[rows, :]
            sg = _sigmoid(g_ref[rows, :])
            dp_ref[0, rows, :] = (acc * sg).astype(BF16)
            dp_ref[1, rows, :] = (acc * a * sg * (1.0 - sg)).astype(BF16)
            d = d_ref[rows, :]
            bacc[...] += jnp.sum(d.reshape(ROWS // SUBLANE, SUBLANE, cb), axis=0)
            for k in range(kk):
                prod = d * _window(extu, shu, r * ROWS + off + k, ROWS)
                wacc[k] += jnp.sum(prod.reshape(ROWS // SUBLANE, SUBLANE, cb), axis=0)

        @pl.when(i == nt - 1)
        def _():
            for k in range(kk):
                dw_ref[k:k + 1, :] = jnp.sum(wacc[k], axis=0, keepdims=True)
            if kpad > kk:
                dw_ref[kk:kpad, :] = jnp.zeros((kpad - kk, cb), F32)
            db_ref[...] = jnp.sum(bacc[...], axis=0, keepdims=True)

    cur = lambda sec: pl.BlockSpec((None, tc, cb), lambda j, i: (sec, i, j))
    prev = lambda sec: pl.BlockSpec((None, HALO, cb), lambda j, i: (sec, jnp.maximum(i * hb - 1, 0), j))
    return _call(
        body, (du1, du1, p3, p3, p3, p3, conv_w), name="conv_bwd", grid=(w // cb, nt),
        in_specs=[pl.BlockSpec((tc, cb), lambda j, i: (i, j)),
                  pl.BlockSpec((HALO, cb), lambda j, i: (jnp.minimum((i + 1) * hb, t // HALO - 1), j)),
                  cur(0), cur(1), prev(0), prev(1), pl.BlockSpec((kk, cb), lambda j, i: (0, j))],
        out_specs=[pl.BlockSpec((2, tc, cb), lambda j, i: (0, i, j)),
                   pl.BlockSpec((kpad, cb), lambda j, i: (0, j)),
                   pl.BlockSpec((1, cb), lambda j, i: (0, j))],
        out_shape=[jax.ShapeDtypeStruct((n_sec, t, w), BF16), jax.ShapeDtypeStruct((kpad, w), F32),
                   jax.ShapeDtypeStruct((1, w), F32)],
        scratch_shapes=[pltpu.VMEM((tc + HALO, cb), F32), pltpu.VMEM((tc + HALO, cb), F32),
                        pltpu.VMEM((SUBLANE - 1, tc + HALO - SUBLANE, cb), F32),
                        pltpu.VMEM((SUBLANE - 1, tc + HALO - SUBLANE, cb), F32),
                        pltpu.VMEM((kk, SUBLANE, cb), F32), pltpu.VMEM((SUBLANE, cb), F32)], rider=rider)


def _ffn_act_fwd(hh3, fw, fb, tc, cb, rider=None):
    _, t, dff = hh3.shape
    kk = fw.shape[0]
    off = FHALO - (kk - 1)
    hb = tc // FHALO

    def body(g_ref, v_ref, gp_ref, w_ref, b_ref, act_ref, ext):
        i = pl.program_id(1)
        ext[pl.ds(FHALO, tc), :] = g_ref[...]
        ext[pl.ds(0, FHALO), :] = jnp.where(i > 0, gp_ref[...], 0.0)
        for r in range(tc // ROWS):
            rows = pl.ds(r * ROWS, ROWS)
            gc = jnp.broadcast_to(b_ref[...], (ROWS, cb))
            for k in range(kk):
                gc = gc + w_ref[k:k + 1, :] * ext[pl.ds(r * ROWS + off + k, ROWS), :]
            act_ref[rows, :] = (gc * _sigmoid(gc) * v_ref[rows, :]).astype(BF16)

    return _call(
        body, (hh3, hh3, hh3, fw, fb), name="ffn_act_fwd", grid=(dff // cb, t // tc),
        in_specs=[pl.BlockSpec((None, tc, cb), lambda j, i: (0, i, j)),
                  pl.BlockSpec((None, tc, cb), lambda j, i: (1, i, j)),
                  pl.BlockSpec((None, FHALO, cb), lambda j, i: (0, jnp.maximum(i * hb - 1, 0), j)),
                  pl.BlockSpec((kk, cb), lambda j, i: (0, j)),
                  pl.BlockSpec((1, cb), lambda j, i: (0, j))],
        out_specs=pl.BlockSpec((tc, cb), lambda j, i: (i, j)),
        out_shape=jax.ShapeDtypeStruct((t, dff), BF16),
        scratch_shapes=[pltpu.VMEM((tc + FHALO, cb), F32)], rider=rider)


def _ffn_act_bwd(dact, hh3, fw, fb, tc, cb):
    _, t, dff = hh3.shape
    kk = fw.shape[0]
    off = FHALO - (kk - 1)
    hb = tc // FHALO
    nt = t // tc
    te = tc + FHALO
    shifts = sorted({(off + k) % SUBLANE for k in range(kk)} - {0})

    def body(da_ref, dan_ref, g_ref, gp_ref, gn_ref, v_ref, vn_ref, w_ref, b_ref,
             dhh_ref, dw_ref, db_ref, gext, gsh, dext, wacc, bacc):
        i = pl.program_id(1)

        @pl.when(i == 0)
        def _():
            wacc[...] = jnp.zeros_like(wacc)
            bacc[...] = jnp.zeros_like(bacc)

        gext[pl.ds(0, FHALO), :] = jnp.where(i > 0, gp_ref[...], 0.0)
        gext[pl.ds(FHALO, tc), :] = g_ref[...]
        gext[pl.ds(FHALO + tc, FHALO), :] = gn_ref[...]
        for s, p in enumerate(shifts):
            gsh[s] = gext[pl.ds(p, te), :]

        def gwin(k, r0, n):
            p = (off + k) % SUBLANE
            if p == 0:
                return gext[pl.ds(r0 + off + k, n), :]
            return gsh[shifts.index(p), pl.ds(r0 + off + k - p, n), :]

        def gate_grad(r0, n, da, v):
            gc = jnp.broadcast_to(b_ref[...], (n, cb))
            for k in range(kk):
                gc = gc + w_ref[k:k + 1, :] * gwin(k, r0, n)
            sg = _sigmoid(gc)
            return gc * sg, da * v * (sg * (1.0 + gc * (1.0 - sg)))

        for r in range(tc // ROWS):
            rows = pl.ds(r * ROWS, ROWS)
            da = da_ref[rows, :]
            silu, dgc = gate_grad(r * ROWS, ROWS, da, v_ref[rows, :])
            dext[rows, :] = dgc
            dhh_ref[1, rows, :] = (da * silu).astype(BF16)
        _, dgc_next = gate_grad(tc, FHALO, dan_ref[...], vn_ref[...])
        dext[pl.ds(tc, FHALO), :] = jnp.where(i < nt - 1, dgc_next, 0.0)
        for r in range(tc // ROWS):
            rows = pl.ds(r * ROWS, ROWS)
            dg = jnp.zeros((ROWS, cb), F32)
            for k in range(kk):
                dg = dg + w_ref[k:k + 1, :] * dext[pl.ds(r * ROWS + (kk - 1) - k, ROWS), :]
            dhh_ref[0, rows, :] = dg.astype(BF16)
            dgc = dext[rows, :]
            bacc[...] += jnp.sum(dgc.reshape(ROWS // SUBLANE, SUBLANE, cb), axis=0)
            for k in range(kk):
                prod = dgc * gwin(k, r * ROWS, ROWS)
                wacc[k] += jnp.sum(prod.reshape(ROWS // SUBLANE, SUBLANE, cb), axis=0)

        @pl.when(i == nt - 1)
        def _():
            for k in range(kk):
                dw_ref[k:k + 1, :] = jnp.sum(wacc[k], axis=0, keepdims=True)
            dw_ref[kk:SUBLANE, :] = jnp.zeros((SUBLANE - kk, cb), F32)
            db_ref[...] = jnp.sum(bacc[...], axis=0, keepdims=True)

    nxt = lambda i: jnp.minimum((i + 1) * hb, t // FHALO - 1)
    return pl.pallas_call(
        body, name="ffn_act_bwd", grid=(dff // cb, nt),
        in_specs=[pl.BlockSpec((tc, cb), lambda j, i: (i, j)),
                  pl.BlockSpec((FHALO, cb), lambda j, i: (nxt(i), j)),
                  pl.BlockSpec((None, tc, cb), lambda j, i: (0, i, j)),
                  pl.BlockSpec((None, FHALO, cb), lambda j, i: (0, jnp.maximum(i * hb - 1, 0), j)),
                  pl.BlockSpec((None, FHALO, cb), lambda j, i: (0, nxt(i), j)),
                  pl.BlockSpec((None, tc, cb), lambda j, i: (1, i, j)),
                  pl.BlockSpec((None, FHALO, cb), lambda j, i: (1, nxt(i), j)),
                  pl.BlockSpec((kk, cb), lambda j, i: (0, j)),
                  pl.BlockSpec((1, cb), lambda j, i: (0, j))],
        out_specs=[pl.BlockSpec((2, tc, cb), lambda j, i: (0, i, j)),
                   pl.BlockSpec((SUBLANE, cb), lambda j, i: (0, j)),
                   pl.BlockSpec((1, cb), lambda j, i: (0, j))],
        out_shape=[jax.ShapeDtypeStruct((2, t, dff), BF16), jax.ShapeDtypeStruct((SUBLANE, dff), F32),
                   jax.ShapeDtypeStruct((1, dff), F32)],
        scratch_shapes=[pltpu.VMEM((tc + 2 * FHALO, cb), F32), pltpu.VMEM((len(shifts), te, cb), F32),
                        pltpu.VMEM((te, cb), F32),
                        pltpu.VMEM((kk, SUBLANE, cb), F32), pltpu.VMEM((SUBLANE, cb), F32)],
        compiler_params=_params("parallel", "arbitrary"),
    )(dact, dact, hh3, hh3, hh3, hh3, hh3, fw, fb)


def _chunk_consts():
    r = lax.broadcasted_iota(jnp.int32, (CHUNK, CHUNK), 0)
    c = lax.broadcasted_iota(jnp.int32, (CHUNK, CHUNK), 1)
    return (c <= r).astype(F32)


def _roll8(x, d):
    return pltpu.roll(x.reshape(CHUNK // SUB, SUB, LANE), d % SUB, 1).reshape(CHUNK, LANE)


def _gate_terms(q, fpre, lb):
    sf = _sigmoid(fpre)
    fg = lb + (1.0 - lb) * sf
    sq = _sigmoid(q)
    return sf, fg, 1.0 - fg, sq, q * sq


def _decays(g, consts):
    b = _dot3(consts, g)
    nb = CHUNK // SUB
    ends = b.reshape(nb, SUB, LANE)[:, SUB - 1:SUB, :]
    re3 = jnp.broadcast_to(ends, (nb, SUB, LANE))
    rs3 = jnp.concatenate([jnp.zeros((1, SUB, LANE), F32), re3[:nb - 1]], axis=0)
    return b, rs3.reshape(CHUNK, LANE), re3.reshape(CHUNK, LANE), b[CHUNK - 1:CHUNK]


def _lower_bound(lb_ref):
    l0, l1 = lb_ref[0:1, :], lb_ref[1:2, :]
    mx = jnp.maximum(l0, l1)
    e0, e1 = jnp.exp(l0 - mx), jnp.exp(l1 - mx)
    return e0 / (e0 + e1)


BF16_ROWS = 16


def _scaled_keys(kt, rs, re, i):
    n = SUB * i
    scale = jnp.exp(rs[n:n + 1, :] - re[:n])
    live = kt[:n] * scale
    m = -(-n // BF16_ROWS) * BF16_ROWS
    if m > n:
        live = jnp.concatenate([live, jnp.zeros((m - n, LANE), F32)], axis=0)
    parts = [live.astype(BF16)]
    if m < CHUNK:
        parts.append(jnp.zeros((CHUNK - m, LANE), BF16))
    return jnp.concatenate(parts, axis=0), scale


def _only_block(x, i):
    w0 = (SUB * i // BF16_ROWS) * BF16_ROWS
    win = x[w0:w0 + BF16_ROWS]
    inblk = lax.broadcasted_iota(jnp.int32, (BF16_ROWS, 1), 0) // SUB == (SUB * i - w0) // SUB
    parts = [jnp.where(inblk, win, 0.0).astype(BF16)]
    if w0:
        parts.insert(0, jnp.zeros((w0, x.shape[1]), BF16))
    if w0 + BF16_ROWS < CHUNK:
        parts.append(jnp.zeros((CHUNK - w0 - BF16_ROWS, x.shape[1]), BF16))
    return jnp.concatenate(parts, axis=0)


def _hgrn_fwd(p3, lb_logits, hg, cat, tb, hpb, rider=None):
    _, t, w = p3.shape
    nh = w // LANE
    nc = tb // CHUNK
    assert nh % hpb == 0

    def body(q_ref, f_ref, v_ref, og_ref, lb_ref, hg_ref, cat_in, cat_ref, o_ref, st_ref, state):
        del cat_in
        consts = _chunk_consts()
        lb_all = _lower_bound(lb_ref)
        rowpos = lax.broadcasted_iota(jnp.int32, (CHUNK, 1), 0) % SUB

        @pl.when(pl.program_id(1) == 0)
        def _():
            state[...] = jnp.zeros_like(state)

        def chunk(c, carry):
            rows = pl.ds(pl.multiple_of(c * CHUNK, CHUNK), CHUNK)
            heads = range(hpb)
            sls = [slice(j * LANE, (j + 1) * LANE) for j in heads]
            v = [v_ref[rows, s] for s in sls]
            vb = [x.astype(BF16) for x in v]
            gates = [_gate_terms(q_ref[rows, s], f_ref[rows, s], lb_all[:, s]) for s in sls]
            fg = [g[1] for g in gates]
            kk = [g[2] for g in gates]
            qh = [g[4] for g in gates]
            dec = [_decays(jnp.log(x), consts) for x in fg]
            b = [x[0] for x in dec]
            rs = [x[1] for x in dec]
            re = [x[2] for x in dec]
            tot = [x[3] for x in dec]
            qt = [qh[j] * jnp.exp(b[j] - rs[j]) for j in heads]
            kt = [kk[j] * jnp.exp(re[j] - b[j]) for j in heads]
            st = [state[j] for j in heads]
            for j in heads:
                st_ref[j, c] = st[j]
            a = [jnp.zeros((CHUNK, CHUNK), F32) for _ in heads]
            for i in range(1, CHUNK // SUB):
                for j in heads:
                    kib, _ = _scaled_keys(kt[j], rs[j], re[j], i)
                    a[j] = a[j] + _dot_nt(_only_block(qt[j], i), kib)
            o = [_dot(a[j].astype(BF16), vb[j]) for j in heads]
            o = [o[j] + _dot_nt((qh[j] * jnp.exp(b[j])).astype(BF16), st[j].astype(BF16)) for j in heads]
            for j in heads:
                k_up = kk[j] * jnp.exp(tot[j] - b[j])
                state[j] = st[j] * jnp.exp(tot[j]) + _dot_tn(vb[j], k_up.astype(BF16))
            for j in heads:
                e, rf = None, fg[j]
                for d in range(SUB):
                    if d == 0:
                        vs, term = v[j], qh[j] * kk[j]
                    else:
                        e = rf if e is None else e * rf
                        rf = _roll8(fg[j], d)
                        vs = _roll8(v[j], d)
                        term = jnp.where(rowpos >= d, qh[j] * (1.0 - rf) * e, 0.0)
                    o[j] = o[j] + jnp.sum(term, axis=-1, keepdims=True) * vs
            for j in heads:
                og = og_ref[rows, sls[j]]
                o_ref[rows, sls[j]] = o[j]
                r = lax.rsqrt(jnp.mean(o[j] * o[j], axis=-1, keepdims=True) + RMS_EPS)
                cat_ref[rows, sls[j]] = (o[j] * r * hg_ref[:, sls[j]] * (og * _sigmoid(og))).astype(BF16)
            return carry

        lax.fori_loop(0, nc, chunk, 0)

    bw = hpb * LANE
    sec = lambda s: pl.BlockSpec((None, tb, bw), lambda h, i: (s, i, h))
    return _call(
        body, (p3, p3, p3, p3, lb_logits, hg, cat), name="hgrn_fwd", grid=(nh // hpb, t // tb),
        in_specs=[sec(2), sec(3), sec(4), sec(5),
                  pl.BlockSpec((2, bw), lambda h, i: (0, h)),
                  pl.BlockSpec((1, bw), lambda h, i: (0, h)), HBM],
        out_specs=[pl.BlockSpec((tb, bw), lambda h, i: (i, nh // hpb + h)),
                   pl.BlockSpec((tb, bw), lambda h, i: (i, h)),
                   pl.BlockSpec((hpb, nc, LANE, LANE), lambda h, i: (h, i, 0, 0))],
        out_shape=[jax.ShapeDtypeStruct(cat.shape, BF16), jax.ShapeDtypeStruct((t, w), F32),
                   jax.ShapeDtypeStruct((nh, t // CHUNK, LANE, LANE), F32)],
        scratch_shapes=[pltpu.VMEM((hpb, LANE, LANE), F32)], aliases={6: 0}, rider=rider)


def _hgrn_bwd(p3, lb_logits, hg, o_pre, states, dcat, dp3, tb, hpb, rider=None):
    n_sec, t, w = p3.shape
    nh = w // LANE
    assert nh % hpb == 0
    nc = tb // CHUNK
    nb = t // tb
    bw = hpb * LANE
    n_steps = (nh // hpb) * nb

    def body(q_ref, f_ref, v_ref, og_ref, lb_ref, hg_ref, o_ref, st_ref, dc_ref, dp_in,
             dp_ref, dlb_ref, dhg_ref, dstate, stash, lbacc, hgacc, osem):
        del dp_in
        h, i = pl.program_id(0), pl.program_id(1)
        step = h * nb + i
        slot = step % 2

        def out_copy(s, row_blk, lane_blk):
            dst = dp_ref.at[pl.ds(2, 4), pl.ds(row_blk * tb, tb), pl.ds(lane_blk * bw, bw)]
            return pltpu.make_async_copy(stash.at[s], dst, osem.at[s])

        @pl.when(step >= 2)
        def _():
            out_copy(slot, 0, 0).wait()

        def compute():
            consts = _chunk_consts()
            rr = lax.broadcasted_iota(jnp.int32, (CHUNK, CHUNK), 0)
            cc = lax.broadcasted_iota(jnp.int32, (CHUNK, CHUNK), 1)
            upper = (cc >= rr).astype(F32)
            lb_all = _lower_bound(lb_ref)
            rowpos = lax.broadcasted_iota(jnp.int32, (CHUNK, 1), 0) % SUB

            @pl.when(i == 0)
            def _():
                dstate[...] = jnp.zeros_like(dstate)
                lbacc[...] = jnp.zeros_like(lbacc)
                hgacc[...] = jnp.zeros_like(hgacc)

            def head(j, c, rows):
                sl = slice(j * LANE, (j + 1) * LANE)
                lb = lb_all[:, sl]
                hgv = hg_ref[:, sl]
                q = q_ref[rows, sl]
                v = v_ref[rows, sl]
                og = og_ref[rows, sl]
                o = o_ref[rows, sl]
                dcg = dc_ref[rows, sl]
                sf, fg, kk, sq, qh = _gate_terms(q, f_ref[rows, sl], lb)
                b, rs, re, tot = _decays(jnp.log(fg), consts)
                eq = jnp.exp(b - rs)
                ek = jnp.exp(re - b)
                qt = qh * eq
                kt = kk * ek
                e_in = jnp.exp(b)
                e_up = jnp.exp(tot - b)
                e_tot = jnp.exp(tot)
                q_in = (qh * e_in).astype(BF16)
                k_up = (kk * e_up).astype(BF16)
                vb = v.astype(BF16)
                st = st_ref[j, c]
                dst = dstate[j]
                dstb = dst.astype(BF16)
                yield

                sg = _sigmoid(og)
                r = lax.rsqrt(jnp.mean(o * o, axis=-1, keepdims=True) + RMS_EPS)
                ohat = o * r
                d_og = dcg * ohat * hgv * (sg * (1.0 + og * (1.0 - sg)))
                d_on = dcg * (og * sg)
                hgacc[:, sl] += jnp.sum((d_on * ohat).reshape(CHUNK // SUBLANE, SUBLANE, LANE), axis=0)
                d_oh = d_on * hgv
                do = r * (d_oh - ohat * jnp.mean(d_oh * ohat, axis=-1, keepdims=True))
                dob = do.astype(BF16)

                da = _dot_nt(dob, vb)
                yield
                a_off = jnp.zeros((CHUNK, CHUNK), F32)
                dqt = jnp.zeros((CHUNK, LANE), F32)
                dkt = jnp.zeros((CHUNK, LANE), F32)
                for blk in range(1, CHUNK // SUB):
                    n = SUB * blk
                    kib, scale = _scaled_keys(kt, rs, re, blk)
                    qib = _only_block(qt, blk)
                    dab = _only_block(da, blk)
                    a_off = a_off + _dot_nt(qib, kib)
                    dqt = dqt + _dot(dab, kib)
                    dkt = jnp.concatenate([dkt[:n] + _dot_tn(dab, qib)[:n] * scale, dkt[n:]], axis=0)
                    yield
                dqh = dqt * eq
                dk = dkt * ek
                dv = _dot_tn(a_off.astype(BF16), dob)

                dqh = dqh + _dot(dob, st.astype(BF16)) * e_in
                dk = dk + _dot(vb, dstb) * e_up
                dv = dv + _dot_nt(k_up, dstb)
                st_end = st * e_tot + _dot_tn(vb, k_up)
                carry_g = jnp.sum(st_end * dst, axis=0, keepdims=True)
                dstate[j] = dst * e_tot + _dot_tn(dob, q_in)
                yield

                e, rf = None, fg
                for d in range(SUB):
                    if d == 0:
                        a_d = jnp.sum(qh * kk, axis=-1, keepdims=True)
                        da_d = jnp.sum(do * v, axis=-1, keepdims=True)
                        dqh = dqh + da_d * kk
                        dk = dk + da_d * qh
                        dv = dv + a_d * do
                        continue
                    e = rf if e is None else e * rf
                    rf = _roll8(fg, d)
                    em = jnp.where(rowpos >= d, e, 0.0)
                    ks, vs = 1.0 - rf, _roll8(v, d)
                    a_d = jnp.sum(qh * ks * em, axis=-1, keepdims=True)
                    da_d = jnp.sum(do * vs, axis=-1, keepdims=True) * em
                    dqh = dqh + da_d * ks
                    dk = dk + _roll8(da_d * qh, -d)
                    dv = dv + _roll8(a_d * do, -d)
                yield

                dg = _dot3(upper, qh * dqh - kk * dk) + carry_g
                dfg = dg / fg - dk
                lbacc[:, sl] += jnp.sum((dfg * (1.0 - sf)).reshape(CHUNK // SUBLANE, SUBLANE, LANE), axis=0)
                stash[slot, 0, rows, sl] = (dqh * (sq * (1.0 + q * (1.0 - sq)))).astype(BF16)
                stash[slot, 1, rows, sl] = (dfg * (1.0 - lb) * sf * (1.0 - sf)).astype(BF16)
                stash[slot, 2, rows, sl] = dv.astype(BF16)
                stash[slot, 3, rows, sl] = d_og.astype(BF16)

            def chunk(cr, carry):
                c = nc - 1 - cr
                rows = pl.ds(pl.multiple_of(c * CHUNK, CHUNK), CHUNK)
                running = [head(j, c, rows) for j in range(hpb)]
                while running:
                    running = [g for g in running if next(g, StopIteration) is not StopIteration]
                return carry

            lax.fori_loop(0, nc, chunk, 0)

            @pl.when(i == nb - 1)
            def _():
                dlb_ref[...] = jnp.sum(lbacc[...], axis=0, keepdims=True)
                dhg_ref[...] = jnp.sum(hgacc[...], axis=0, keepdims=True)

        compute()
        out_copy(slot, nb - 1 - i, h).start()

        @pl.when(step == n_steps - 1)
        def _():
            out_copy(slot, 0, 0).wait()
            if n_steps >= 2:
                out_copy(1 - slot, 0, 0).wait()

    rev = lambda i: nb - 1 - i
    sec = lambda s: pl.BlockSpec((None, tb, bw), lambda h, i: (s, rev(i), h))
    return _call(
        body, (p3, p3, p3, p3, lb_logits, hg, o_pre, states, dcat, dp3), name="hgrn_bwd", grid=(nh // hpb, nb),
        in_specs=[sec(2), sec(3), sec(4), sec(5),
                  pl.BlockSpec((2, bw), lambda h, i: (0, h)),
                  pl.BlockSpec((1, bw), lambda h, i: (0, h)),
                  pl.BlockSpec((tb, bw), lambda h, i: (rev(i), h)),
                  pl.BlockSpec((hpb, nc, LANE, LANE), lambda h, i: (h, rev(i), 0, 0)),
                  pl.BlockSpec((tb, bw), lambda h, i: (rev(i), nh // hpb + h)), HBM],
        out_specs=[HBM,
                   pl.BlockSpec((1, bw), lambda h, i: (0, h)),
                   pl.BlockSpec((1, bw), lambda h, i: (0, h))],
        out_shape=[jax.ShapeDtypeStruct((n_sec, t, w), BF16), jax.ShapeDtypeStruct((1, w), F32),
                   jax.ShapeDtypeStruct((1, w), F32)],
        scratch_shapes=[pltpu.VMEM((hpb, LANE, LANE), F32), pltpu.VMEM((2, 4, tb, bw), BF16),
                        pltpu.VMEM((SUBLANE, bw), F32), pltpu.VMEM((SUBLANE, bw), F32),
                        pltpu.SemaphoreType.DMA((2,))],
        aliases={9: 0}, rider=rider)


def _place():
    x, y, c = lax.axis_index("x"), lax.axis_index("y"), lax.axis_index("c")
    chips = [(1 - x, y), (x, 1 - y), (1 - x, 1 - y)]
    return x, y, c, chips


def _rows(buf, px, py, pc, part=None):
    half = buf.shape[1] // 2
    if part is None:
        return buf.at[2 * px + py, pl.ds(pc * half, half)]
    lo, hi, n = part
    piece = half // n
    return buf.at[2 * px + py, pl.ds(pc * half + lo * piece, (hi - lo) * piece)]


def _rcopy(src, dst, send, recv, idx, to):
    return pltpu.make_async_remote_copy(src_ref=src, dst_ref=dst, send_sem=send.at[idx], recv_sem=recv.at[idx],
                                        device_id=to, device_id_type=MESH)


def _same(bufs):
    return [jax.ShapeDtypeStruct(b.shape, b.dtype) for b in bufs]


def _ride_gather_ici(bufs, parts=None):
    n = len(bufs)
    parts = parts or [None] * n

    def start(rin, rout, send, recv):
        x, y, c, chips = _place()
        for k in range(n):
            mine = _rows(rout[k], x, y, c, parts[k])
            for j, chip in enumerate(chips):
                _rcopy(mine, mine, send, recv, 3 * k + j, (*chip, c)).start()

    def finish(rin, rout, send, recv):
        x, y, c, chips = _place()
        for k in range(n):
            for j, chip in enumerate(chips):
                theirs = _rows(rout[k], *chip, c, parts[k])
                _rcopy(theirs, theirs, send, recv, 3 * k + j, (x, y, c)).wait_recv()
        for k in range(n):
            mine = _rows(rout[k], x, y, c, parts[k])
            for j in range(3):
                _rcopy(mine, mine, send, recv, 3 * k + j, (x, y, c)).wait_send()

    return _Rider(bufs, _same(bufs), {k: k for k in range(n)}, 3 * n, start, finish)


class _SemView:
    def __init__(self, ref, base):
        self.ref, self.base = ref, base

    @property
    def at(self):
        return self

    def __getitem__(self, idx):
        return self.ref.at[idx + self.base]


def _ride_both(a, b):
    nai, nao = len(a.ins), len(a.outs)

    def start(rin, rout, send, recv):
        a.start(rin[:nai], rout[:nao], send, recv)
        b.start(rin[nai:], rout[nao:], _SemView(send, a.n_sems), _SemView(recv, a.n_sems))

    def finish(rin, rout, send, recv):
        a.finish(rin[:nai], rout[:nao], send, recv)
        b.finish(rin[nai:], rout[nao:], _SemView(send, a.n_sems), _SemView(recv, a.n_sems))

    aliases = dict(a.aliases)
    aliases.update({nai + ri: nao + ro for ri, ro in b.aliases.items()})
    return _Rider(a.ins + b.ins, a.outs + b.outs, aliases, a.n_sems + b.n_sems, start, finish)


def _ride_gather_d2d(bufs):
    n = len(bufs)

    def start(rin, rout, send, recv):
        x, y, c, chips = _place()
        for k in range(n):
            for j, chip in enumerate(chips):
                got = _rows(rout[k], *chip, c)
                _rcopy(got, got, send, recv, 3 * k + j, (x, y, 1 - c)).start()

    def finish(rin, rout, send, recv):
        x, y, c, chips = _place()
        for k in range(n):
            for j, chip in enumerate(chips):
                theirs = _rows(rout[k], *chip, 1 - c)
                _rcopy(theirs, theirs, send, recv, 3 * k + j, (x, y, c)).wait_recv()
        for k in range(n):
            for j, chip in enumerate(chips):
                got = _rows(rout[k], *chip, c)
                _rcopy(got, got, send, recv, 3 * k + j, (x, y, c)).wait_send()

    return _Rider(bufs, _same(bufs), {k: k for k in range(n)}, 3 * n, start, finish)


def _ride_swap(grads):
    n = len(grads)

    def copy(k, rin, rout, send, recv):
        x, y, c, _ = _place()
        half = rin[k].shape[1] // 2
        return _rcopy(rin[k].at[:, pl.ds((1 - c) * half, half)], rout[k], send, recv, k, (x, y, 1 - c))

    def start(rin, rout, send, recv):
        for k in range(n):
            copy(k, rin, rout, send, recv).start()

    def finish(rin, rout, send, recv):
        for k in range(n):
            copy(k, rin, rout, send, recv).wait()

    outs = [jax.ShapeDtypeStruct((g.shape[0], g.shape[1] // 2, g.shape[2]), g.dtype) for g in grads]
    return _Rider(grads, outs, {}, n, start, finish)


def _ride_send_partials(parts, pieces=None, into=None):
    n = len(parts)
    pieces = pieces or [None] * n

    def cut(ref, k):
        if pieces[k] is None:
            return ref
        lo, hi, m = pieces[k]
        q = ref.shape[0] // m
        return ref.at[pl.ds(lo * q, (hi - lo) * q)]

    def copies(rin, rout, send, recv):
        x, y, c, chips = _place()
        return [_rcopy(cut(rin[k].at[2 * px + py], k), cut(rout[k].at[j], k), send, recv, 3 * k + j, (px, py, c))
                for k in range(n) for j, (px, py) in enumerate(chips)]

    def start(rin, rout, send, recv):
        for cp in copies(rin, rout, send, recv):
            cp.start()

    def finish(rin, rout, send, recv):
        for cp in copies(rin, rout, send, recv):
            cp.wait()

    if into is None:
        outs = [jax.ShapeDtypeStruct((3,) + p.shape[1:], p.dtype) for p in parts]
        return _Rider(parts, outs, {}, 3 * n, start, finish)
    return _Rider(list(parts) + list(into), _same(into), {n + k: k for k in range(n)}, 3 * n, start, finish)


def _ride_join(bufs):
    n = len(bufs)

    def half_of(buf, pc):
        half = buf.shape[0] // 2
        return buf.at[pl.ds(pc * half, half)]

    def start(rin, rout, send, recv):
        x, y, c, _ = _place()
        for k in range(n):
            mine = half_of(rout[k], c)
            _rcopy(mine, mine, send, recv, k, (x, y, 1 - c)).start()

    def finish(rin, rout, send, recv):
        x, y, c, _ = _place()
        for k in range(n):
            mine, theirs = half_of(rout[k], c), half_of(rout[k], 1 - c)
            _rcopy(mine, mine, send, recv, k, (x, y, c)).wait_send()
            _rcopy(theirs, theirs, send, recv, k, (x, y, c)).wait_recv()

    return _Rider(bufs, _same(bufs), {k: k for k in range(n)}, n, start, finish)


def _run(name, rider):
    def body(*refs):
        nri, nro = len(rider.ins), len(rider.outs)
        rin, rout = refs[:nri], refs[nri:nri + nro]
        send, recv = refs[nri + nro:]
        rider.start(rin, rout, send, recv)
        rider.finish(rin, rout, send, recv)

    return pl.pallas_call(
        body, name=name, in_specs=[HBM] * len(rider.ins), out_specs=[HBM] * len(rider.outs), out_shape=rider.outs,
        scratch_shapes=[pltpu.SemaphoreType.DMA((rider.n_sems,)), pltpu.SemaphoreType.DMA((rider.n_sems,))],
        input_output_aliases=rider.aliases,
    )(*rider.ins)


def _add_halves(name, g, other, c_idx):
    s, r, cols = g.shape
    half = r // 2
    tr = _div_tile(half, 16, 512)
    nb = half // tr

    def body(c_ref, g_ref, o_ref, q_ref):
        del c_ref
        q_ref[...] = (g_ref[...] + o_ref[...]).astype(BF16)

    return pl.pallas_call(
        body, name=name,
        grid_spec=pltpu.PrefetchScalarGridSpec(
            num_scalar_prefetch=1, grid=(s, nb),
            in_specs=[pl.BlockSpec((None, tr, cols), lambda k, i, c: (k, c[0] * nb + i, 0)),
                      pl.BlockSpec((None, tr, cols), lambda k, i, c: (k, i, 0))],
            out_specs=pl.BlockSpec((None, tr, cols), lambda k, i, c: (k, i, 0))),
        out_shape=jax.ShapeDtypeStruct((s, half, cols), BF16),
        compiler_params=_params("parallel", "parallel"),
    )(c_idx, g, other)


def _sum_partials(name, part, arrived, place_idx):
    _, half, cols = part.shape
    tr = _div_tile(half, 16, 512)
    nb = half // tr

    def body(s_ref, p_ref, a_ref, o_ref):
        del s_ref
        o_ref[...] = ((p_ref[...].astype(F32) + a_ref[0].astype(F32)) + a_ref[1].astype(F32)) + a_ref[2].astype(F32)

    return pl.pallas_call(
        body, name=name,
        grid_spec=pltpu.PrefetchScalarGridSpec(
            num_scalar_prefetch=1, grid=(nb,),
            in_specs=[pl.BlockSpec((None, tr, cols), lambda i, s: (s[0], i, 0)),
                      pl.BlockSpec((3, tr, cols), lambda i, s: (0, i, 0))],
            out_specs=pl.BlockSpec((tr, cols), lambda i, s: (s[1] * nb + i, 0))),
        out_shape=jax.ShapeDtypeStruct((2 * half, cols), F32),
        compiler_params=_params("parallel"),
    )(place_idx, part, arrived)


def _pack_small(name, wide_rows, ffn_rows, w, dff, n_wide, n_ffn):
    n_in = len(wide_rows) + len(ffn_rows)

    def body(*refs):
        ins, outs = refs[:n_in], refs[n_in:]
        p1 = outs[0]
        p1[...] = jnp.zeros_like(p1)
        row = 0
        for ref, (_, r, m) in zip(ins, wide_rows):
            if m == 1 and r % SUBLANE == 0 and row % SUBLANE == 0:
                p1[row:row + r, :] = ref[...]
                row += r
                continue
            for rr in range(r):
                for mm in range(m):
                    p1[row:row + 1, :] = ref[rr:rr + 1, mm * w:(mm + 1) * w]
                    row += 1
        if ffn_rows:
            p2 = outs[1]
            p2[...] = jnp.zeros_like(p2)
            row = 0
            for ref, arr in zip(ins[len(wide_rows):], ffn_rows):
                r = arr.shape[0]
                p2[row:row + r, :] = ref[...]
                row += r

    shapes = [jax.ShapeDtypeStruct((n_wide, w), F32)] + ([jax.ShapeDtypeStruct((n_ffn, dff), F32)] if ffn_rows else [])
    return pl.pallas_call(
        body, name=name, in_specs=[VMEM_FULL] * n_in, out_specs=[VMEM_FULL] * len(shapes), out_shape=shapes,
        compiler_params=pltpu.CompilerParams(vmem_limit_bytes=VMEM_LIMIT),
    )(*[a for a, _, _ in wide_rows], *ffn_rows)


def _ride_exchange8(packs):
    n = len(packs)

    def copies(rin, rout, send, recv):
        x, y, c, _ = _place()
        me = 4 * x + 2 * y + c
        out = []
        for a in range(n):
            for mask in range(1, 8):
                peer = (x ^ (mask >> 2), y ^ ((mask >> 1) & 1), c ^ (mask & 1))
                out.append(_rcopy(rin[a], rout[a].at[me], send, recv, 8 * a + mask, peer))
        own = [pltpu.make_async_copy(rin[a], rout[a].at[me], send.at[8 * a]) for a in range(n)]
        return out, own

    def start(rin, rout, send, recv):
        remote, own = copies(rin, rout, send, recv)
        for cp in remote + own:
            cp.start()

    def finish(rin, rout, send, recv):
        remote, own = copies(rin, rout, send, recv)
        for cp in remote + own:
            cp.wait()

    outs = [jax.ShapeDtypeStruct((8,) + p.shape, p.dtype) for p in packs]
    return _Rider(packs, outs, {}, 8 * n, start, finish)


def _sum_small(name, slots):
    def body(*refs):
        n = len(refs) // 2
        for r_ref, s_ref in zip(refs[:n], refs[n:]):
            tot = r_ref[0]
            for d in range(1, 8):
                tot = tot + r_ref[d]
            s_ref[...] = tot

    return pl.pallas_call(
        body, name=name, in_specs=[VMEM_FULL] * len(slots), out_specs=[VMEM_FULL] * len(slots),
        out_shape=[jax.ShapeDtypeStruct(s.shape[1:], s.dtype) for s in slots],
        compiler_params=pltpu.CompilerParams(vmem_limit_bytes=VMEM_LIMIT),
    )(*slots)


def _adamw(w, g, m, v):
    m2 = ADAM_B1 * m + (1.0 - ADAM_B1) * g
    v2 = ADAM_B2 * v + (1.0 - ADAM_B2) * (g * g)
    m_hat = m2 / (1.0 - ADAM_B1 ** ADAM_STEP)
    v_hat = v2 / (1.0 - ADAM_B2 ** ADAM_STEP)
    delta = -ADAM_LR * (m_hat / (jnp.sqrt(v_hat) + ADAM_EPS) + ADAM_WD * w)
    return delta, m2, v2


def _adam_big(name, w, g, m, v):
    r, c = w.shape
    tr = 128 if r % 128 == 0 else r

    def body(w_ref, g_ref, m_ref, v_ref, go_ref, d_ref, m2_ref, v2_ref):
        g = g_ref[...]
        go_ref[...] = g
        d_ref[...], m2_ref[...], v2_ref[...] = _adamw(w_ref[...], g, m_ref[...], v_ref[...])

    blk = pl.BlockSpec((tr, c), lambda i: (i, 0))
    return _call(
        body, (w, g, m, v), name=name, grid=(r // tr,), in_specs=[blk] * 4, out_specs=[blk] * 4,
        out_shape=[jax.ShapeDtypeStruct((r, c), F32)] * 4)


def _adam_small(s1, s2, s3, cw_g, fw_g, lb_logits, triples, layout, w):
    n = len(triples)

    def body(*refs):
        s1_ref, s2_ref, s3_ref, cw_ref, fw_ref, lbl_ref = refs[:6]
        prm = refs[6:6 + 3 * n]
        outs = refs[6 + 3 * n:]
        for p, lay in enumerate(layout):
            w_ref, m_ref, v_ref = prm[3 * p:3 * p + 3]
            g_ref, d_ref, m2_ref, v2_ref = outs[4 * p:4 * p + 4]
            if lay[0] in ("wide", "late"):
                _, row, r, pieces = lay
                src = s1_ref if lay[0] == "wide" else s3_ref
                for rr in range(r):
                    for mm in range(pieces):
                        g_ref[rr:rr + 1, mm * w:(mm + 1) * w] = src[row:row + 1, :]
                        row += 1
            elif lay[0] == "ffn":
                _, row, r = lay
                g_ref[...] = s2_ref[row:row + r, :]
            elif lay[0] == "cw":
                g_ref[...] = cw_ref[0:g_ref.shape[0], :]
            elif lay[0] == "fw":
                g_ref[...] = fw_ref[0:g_ref.shape[0], :]
            else:
                s0 = _lower_bound(lbl_ref)
                d0 = s1_ref[lay[1]:lay[1] + 1, :] * s0 * (1.0 - s0)
                g_ref[0:1, :] = d0
                g_ref[1:2, :] = -d0
            d_ref[...], m2_ref[...], v2_ref[...] = _adamw(w_ref[...], g_ref[...], m_ref[...], v_ref[...])

    flat = [a for tr in triples for a in tr]
    shapes = []
    for tr in triples:
        shapes.extend([jax.ShapeDtypeStruct(tr[0].shape, F32)] * 4)
    return pl.pallas_call(
        body, name="adam_small", in_specs=[VMEM_FULL] * (6 + 3 * n), out_specs=[VMEM_FULL] * (4 * n),
        out_shape=shapes, compiler_params=pltpu.CompilerParams(vmem_limit_bytes=VMEM_LIMIT),
    )(s1, s2, s3, cw_g, fw_g, lb_logits, *flat)


def _row_tile(t):
    return 512 if t % 512 == 0 and t >= 2048 else 128


def kernel(x, emb_ln_g, emb_ln_b, w_in, conv_w, conv_b, conv_norm_g, conv_norm_b, lb_logits, hgrn_norm_g, w_out, ln1_g, ln1_b, w_ffn_up, ffn_conv_w, ffn_conv_b, w_ffn_down, ln2_g, ln2_b, loss_target, m_emb_ln_g, m_emb_ln_b, m_w_in, m_conv_w, m_conv_b, m_conv_norm_g, m_conv_norm_b, m_lb_logits, m_hgrn_norm_g, m_w_out, m_ln1_g, m_ln1_b, m_w_ffn_up, m_ffn_conv_w, m_ffn_conv_b, m_w_ffn_down, m_ln2_g, m_ln2_b, v_emb_ln_g, v_emb_ln_b, v_w_in, v_conv_w, v_conv_b, v_conv_norm_g, v_conv_norm_b, v_lb_logits, v_hgrn_norm_g, v_w_out, v_ln1_g, v_ln1_b, v_w_ffn_up, v_ffn_conv_w, v_ffn_conv_b, v_w_ffn_down, v_ln2_g, v_ln2_b):
    depth = w_in.shape[0]
    assert depth == 1 and x.shape[0] == 1
    alpha = (2.0 * depth) ** 0.25
    t, d = x.shape[1], x.shape[2]
    w = d // 2
    dff = ffn_conv_b.shape[1]
    kc = conv_w.shape[1]
    assert w % (2 * LANE) == 0 and dff % (4 * LANE) == 0 and t % 128 == 0
    tm = _row_tile(t)
    tm2 = tm // 2
    tmm = 1024 if t % 1024 == 0 and t >= 2048 else tm
    cb = 2 * LANE
    cbf = 4 * LANE
    tb = tm
    nh = w // LANE
    hpb = 8 if nh % 8 == 0 else 2

    xi = lax.axis_index("x")
    yi = lax.axis_index("y")
    ci = lax.axis_index("c")
    chip = 2 * xi + yi
    c_idx = jnp.reshape(ci, (1,)).astype(jnp.int32)
    chip_idx = jnp.reshape(chip, (1,)).astype(jnp.int32)
    place_idx = jnp.stack([chip, ci]).astype(jnp.int32)

    x2 = x[0]
    tgt = loss_target[0]
    g0, b0 = emb_ln_g.reshape(1, d), emb_ln_b.reshape(1, d)
    w_in2, w_out2, w_up2, w_dn2 = w_in[0], w_out[0], w_ffn_up[0], w_ffn_down[0]
    cw2, fw2 = conv_w[0], ffn_conv_w[0]

    b_in = _place_shard(w_in2, "place_w_in", chip_idx, BF16)
    b_out = _place_shard(w_out2, "place_w_out", chip_idx, BF16)
    b_up = _place_shard(w_up2, "place_w_up", chip_idx, BF16)
    b_dn = _place_shard(w_dn2, "place_w_down", chip_idx, BF16)
    b_cw = _place_shard(_pad_rows(cw2), "place_conv_w", chip_idx, F32)
    b_fw = _place_shard(_pad_rows(fw2), "place_ffn_conv_w", chip_idx, F32)
    h0b, (b_in,) = _ln0(x2, g0, b0, tm, rider=_ride_gather_ici([b_in], [(0, 1, 8)]))
    first = _run("gather_first_ici", _ride_gather_ici([b_in, b_cw, b_fw], [(1, 8, 8), None, None]))
    w_in3, cw_full3, fw_full3 = _run("gather_first_d2d", _ride_gather_d2d(first))
    cw_full = _unshard_cols(cw_full3)[:kc]
    fw_full = _unshard_cols(fw_full3)[:fw2.shape[0]]

    p3, (b_out, b_up) = _proj("in_proj", h0b, w_in3, 6, 2 * tmm if t % (2 * tmm) == 0 else tmm, w // 2,
                              rider=_ride_gather_ici([b_out, b_up], [None, (0, 1, 4)]))
    (cat, u1), (w_out3, b_up) = _conv_fwd(
        p3, cw_full, conv_b, conv_norm_g, conv_norm_b, tm2, cb,
        rider=_ride_both(_ride_gather_d2d([b_out]), _ride_gather_ici([b_up], [(1, 2, 4)])))
    w_out_full = w_out3.reshape(d, d)
    (cat, o_pre, states), got = _hgrn_fwd(p3, lb_logits, hgrn_norm_g, cat, tb, hpb,
                                          rider=_ride_gather_ici([b_up], [(2, 4, 4)]))
    (xhat1, h1b, rstd1), (w_up3,) = _mix_ln1(cat, w_out_full, x2, g0, b0, ln1_g, ln1_b, alpha, tm2,
                                             rider=_ride_gather_d2d(got))
    hh3, got = _proj("ffn_up", h1b, w_up3, 2, tmm, dff // 4, rider=_ride_gather_ici([b_dn]))
    act, (w_dn3,) = _ffn_act_fwd(hh3, fw_full, ffn_conv_b, tm, cbf, rider=_ride_gather_d2d(got))
    ks = dff // N_CHIPS
    ffn = _wgrad("ffn_down", act, w_dn3, (t, d), (t // tmm, 1, N_CHIPS),
                 pl.BlockSpec((tmm, ks), lambda i, j, k: (i, k)),
                 pl.BlockSpec((None, ks, d), lambda i, j, k: (k, 0, 0)),
                 pl.BlockSpec((tmm, d), lambda i, j, k: (i, 0)), dot=_dot)
    dz2, dz2b, dg2, db2, loss_row = _ln2_loss(ffn, xhat1, tgt, ln1_g, ln1_b, ln2_g, ln2_b, alpha, tm2)

    dact = _proj_t("ffn_down_t", dz2b, w_dn3.reshape(dff, d), tmm, ks)
    dhh3, dfw, dfb = _ffn_act_bwd(dact, hh3, fw_full, ffn_conv_b, tm, cbf)
    tt = 2 * tmm if t % (2 * tmm) == 0 else tmm
    d_w_dn = _wgrad("wgrad_down", act, dz2b, (N_CHIPS, ks, d), (N_CHIPS, 2, t // tt),
                    pl.BlockSpec((tt, ks), lambda s, j, k: (k, s)),
                    pl.BlockSpec((tt, d // 2), lambda s, j, k: (k, j)),
                    pl.BlockSpec((None, ks, d // 2), lambda s, j, k: (s, 0, j)))
    wu = 2 * dff // N_CHIPS
    tnu = wu // 2
    per_sec_u = dff // tnu
    pre1, (arr_dn,) = _wgrad(
        "up_t", dhh3, w_up3, (t, d), (t // tmm, 1, 2 * N_CHIPS),
        pl.BlockSpec((None, tmm, tnu), lambda i, j, k: (k // per_sec_u, i, k % per_sec_u)),
        pl.BlockSpec((None, d, tnu), lambda i, j, k: (k // 2, 0, k % 2)),
        pl.BlockSpec((tmm, d), lambda i, j, k: (i, 0)), dot=_dot_nt, rider=_ride_swap([d_w_dn]))
    dz1, dz1b, dg1, db1 = _ln1_bwd(pre1, dz2, xhat1, rstd1, ln1_g, alpha, tm2)
    part_dn = _add_halves("add_halves_w_down", d_w_dn, arr_dn, c_idx)
    d_w_up, (land_dn,) = _wgrad(
        "wgrad_up", h1b, dhh3, (N_CHIPS, d, wu), (N_CHIPS, 2, 2, t // tt),
        pl.BlockSpec((tt, d // 2), lambda s, r, j, k: (k, r)),
        pl.BlockSpec((None, tt, tnu), lambda s, r, j, k: ((2 * s + j) // per_sec_u, k, (2 * s + j) % per_sec_u)),
        pl.BlockSpec((None, d // 2, tnu), lambda s, r, j, k: (s, r, j)), rider=_ride_send_partials([part_dn]))
    dcat = _proj_t("out_proj_t", dz1b, w_out_full, tmm, d // 2)
    d_w_out = _wgrad("wgrad_out", cat, dz1b, (d, d), (2, 2, t // tt),
                     pl.BlockSpec((tt, d // 2), lambda r, j, k: (k, r)),
                     pl.BlockSpec((tt, d // 2), lambda r, j, k: (k, j)),
                     pl.BlockSpec((d // 2, d // 2), lambda r, j, k: (r, j))).reshape(N_CHIPS, d // N_CHIPS, d)
    du1, dcng, dcnb = _conv_norm_bwd(dcat, u1, conv_norm_g, conv_norm_b, tm)
    (dp3, dcw, dcb), (arr_up, arr_out) = _conv_bwd(du1, p3, cw_full, tm2, cb, rider=_ride_swap([d_w_up, d_w_out]))
    part_up = _add_halves("add_halves_w_up", d_w_up, arr_up, c_idx)
    part_out = _add_halves("add_halves_w_out", d_w_out, arr_out, c_idx)
    (dp3, dlb, dhg), (land_up, land_out) = _hgrn_bwd(
        p3, lb_logits, hgrn_norm_g, o_pre, states, dcat, dp3, tb, hpb,
        rider=_ride_send_partials([part_up, part_out], [(0, 3, 4), None]))
    kpad = dcw.shape[0]
    wide = [(dcw, kpad, 1), (dg1, 1, 2), (db1, 1, 2), (dg2, 1, 2), (db2, 1, 2),
            (dcb, 1, 1), (dcng, 1, 1), (dcnb, 1, 1), (dlb, 1, 1), (dhg, 1, 1)]
    n_wide = -(-sum(r * m for _, r, m in wide) // SUBLANE) * SUBLANE
    packs = _pack_small("pack_small", wide, [dfw, dfb], w, dff, n_wide, 2 * SUBLANE)
    d_w_in, (land_up, slots1, slots2) = _wgrad_in(
        h0b, dp3, N_CHIPS, tt,
        rider=_ride_both(_ride_send_partials([part_up], [(3, 4, 4)], into=[land_up]), _ride_exchange8(packs)))
    s1, s2 = _sum_small("sum_small", [slots1, slots2])
    (arr_in,) = _run("swap_w_in", _ride_swap([d_w_in]))
    part_in = _add_halves("add_halves_w_in", d_w_in, arr_in, c_idx)
    h_out, h_up, h_dn = [
        _sum_partials("sum_partials_" + nm, p, a, place_idx)
        for nm, p, a in (("w_out", part_out, land_out), ("w_up", part_up, land_up), ("w_down", part_dn, land_dn))]
    pre0, (land_in, g_w_out, g_w_up, g_w_dn) = _in_t(
        dp3, w_in3, tmm, rider=_ride_both(_ride_send_partials([part_in]), _ride_join([h_out, h_up, h_dn])))
    dx, dg0, db0 = _ln0_bwd(pre0, dz1, x2, g0, alpha, tm2)
    h_in = _sum_partials("sum_partials_w_in", part_in, land_in, place_idx)
    (g_w_in,) = _run("join_w_in", _ride_join([h_in]))

    late = _pack_small("pack_late", [(dg0, 1, 2), (db0, 1, 2)], [], w, dff, SUBLANE, 0)
    (s3,) = _sum_small("sum_late", _run("exchange_late", _ride_exchange8(late)))
    cw_g = lax.dynamic_slice_in_dim(s1[0:kpad], chip * (w // N_CHIPS), w // N_CHIPS, axis=1)
    fw_g = lax.dynamic_slice_in_dim(s2[0:SUBLANE], chip * (dff // N_CHIPS), dff // N_CHIPS, axis=1)

    small = [
        (g0, m_emb_ln_g.reshape(1, d), v_emb_ln_g.reshape(1, d)), (b0, m_emb_ln_b.reshape(1, d), v_emb_ln_b.reshape(1, d)),
        (cw2, m_conv_w[0], v_conv_w[0]), (conv_b, m_conv_b, v_conv_b),
        (conv_norm_g, m_conv_norm_g, v_conv_norm_g), (conv_norm_b, m_conv_norm_b, v_conv_norm_b),
        (lb_logits, m_lb_logits, v_lb_logits), (hgrn_norm_g, m_hgrn_norm_g, v_hgrn_norm_g),
        (ln1_g, m_ln1_g, v_ln1_g), (ln1_b, m_ln1_b, v_ln1_b),
        (fw2, m_ffn_conv_w[0], v_ffn_conv_w[0]), (ffn_conv_b, m_ffn_conv_b, v_ffn_conv_b),
        (ln2_g, m_ln2_g, v_ln2_g), (ln2_b, m_ln2_b, v_ln2_b),
    ]
    r0 = kpad
    layout = [("late", 0, 1, 2), ("late", 2, 1, 2), ("cw",), ("wide", r0 + 8, 1, 1), ("wide", r0 + 9, 1, 1),
              ("wide", r0 + 10, 1, 1), ("lb", r0 + 11), ("wide", r0 + 12, 1, 1), ("wide", r0, 1, 2),
              ("wide", r0 + 2, 1, 2), ("fw",), ("ffn", SUBLANE, 1), ("wide", r0 + 4, 1, 2), ("wide", r0 + 6, 1, 2)]
    so = _adam_small(s1, s2, s3, cw_g, fw_g, lb_logits, small, layout, w)
    sm = {nm: so[4 * i:4 * i + 4] for i, nm in enumerate(
        ["emb_ln_g", "emb_ln_b", "conv_w", "conv_b", "conv_norm_g", "conv_norm_b", "lb_logits", "hgrn_norm_g",
         "ln1_g", "ln1_b", "ffn_conv_w", "ffn_conv_b", "ln2_g", "ln2_b"])}
    bigs = {}
    for nm, wt, g, m, v in (("w_in", w_in2, g_w_in, m_w_in[0], v_w_in[0]), ("w_out", w_out2, g_w_out, m_w_out[0], v_w_out[0]),
                            ("w_ffn_up", w_up2, g_w_up, m_w_ffn_up[0], v_w_ffn_up[0]),
                            ("w_ffn_down", w_dn2, g_w_dn, m_w_ffn_down[0], v_w_ffn_down[0])):
        bigs[nm] = tuple(_adam_big("adam_" + nm, wt, g, m, v))

    loss = lax.psum(loss_row[0, 0], ("x", "y", "c"))

    order = ["emb_ln_g", "emb_ln_b", "w_in", "conv_w", "conv_b", "conv_norm_g", "conv_norm_b", "lb_logits",
             "hgrn_norm_g", "w_out", "ln1_g", "ln1_b", "w_ffn_up", "ffn_conv_w", "ffn_conv_b", "w_ffn_down",
             "ln2_g", "ln2_b"]
    shapes = dict(emb_ln_g=emb_ln_g.shape, emb_ln_b=emb_ln_b.shape, w_in=w_in.shape, conv_w=conv_w.shape,
                  w_out=w_out.shape, w_ffn_up=w_ffn_up.shape, ffn_conv_w=ffn_conv_w.shape, w_ffn_down=w_ffn_down.shape)
    outs = [loss, dx.reshape(x.shape)]
    for which in range(4):
        for nm in order:
            a = bigs[nm][which] if nm in bigs else sm[nm][which]
            outs.append(a.reshape(shapes[nm]) if nm in shapes else a)
    return tuple(outs)


def _pad_rows(a):
    k = a.shape[0]
    kp = -(-k // 16) * 16
    return jnp.pad(a, ((0, kp - k), (0, 0)))


def _unshard_cols(a3):
    s, k, c = a3.shape
    return jnp.transpose(a3, (1, 0, 2)).reshape(k, s * c)
```

```python
import functools

import jax
import jax.numpy as jnp
from jax import lax
from jax.experimental import pallas as pl
from jax.experimental.pallas import tpu as pltpu

F32 = jnp.float32
BF16 = jnp.bfloat16

LN_EPS = 1e-5
RMS_EPS = 1e-6
LANE = 128
SUBLANE = 8
CHUNK = 64
SUB = 8
HALO = 32
FHALO = 8
ROWS = 64
N_CHIPS = 4
VMEM_LIMIT = 56 << 20
NEG_BIG = -1e30

ADAM_LR = 0.001
ADAM_B1 = 0.9
ADAM_B2 = 0.999
ADAM_EPS = 1e-08
ADAM_WD = 0.01
ADAM_STEP = 10

MESH = pl.DeviceIdType.MESH
LOCAL_COPY_PRIORITY = 1
HBM = pl.BlockSpec(memory_space=pl.ANY)
VMEM_FULL = pl.BlockSpec(memory_space=pltpu.VMEM)


def _params(*sem):
    return pltpu.CompilerParams(dimension_semantics=sem, vmem_limit_bytes=VMEM_LIMIT)


class _Rider:
    def __init__(self, ins, outs, aliases, n_sems, start, finish):
        self.ins, self.outs, self.aliases = list(ins), list(outs), dict(aliases)
        self.n_sems, self.start, self.finish = n_sems, start, finish


def _call(body, args, *, name, grid, in_specs, out_specs, out_shape, scratch_shapes=(), aliases=None, rider=None):
    many = isinstance(out_shape, (list, tuple))
    shapes = list(out_shape) if many else [out_shape]
    ospecs = list(out_specs) if many else [out_specs]
    aliases = dict(aliases or {})
    sem = ("arbitrary",) * len(grid)
    if rider is None:
        res = pl.pallas_call(
            body, name=name, grid=grid, in_specs=list(in_specs), out_specs=ospecs, out_shape=shapes,
            scratch_shapes=list(scratch_shapes), input_output_aliases=aliases, compiler_params=_params(*sem))(*args)
        return res if many else res[0]
    n_in, n_out, n_scr = len(args), len(shapes), len(scratch_shapes)
    nri, nro = len(rider.ins), len(rider.outs)

    def wrapped(*refs):
        ins, rin = refs[:n_in], refs[n_in:n_in + nri]
        o0 = n_in + nri
        outs, rout = refs[o0:o0 + n_out], refs[o0 + n_out:o0 + n_out + nro]
        s0 = o0 + n_out + nro
        scr, (send, recv) = refs[s0:s0 + n_scr], refs[s0 + n_scr:]
        ids = [pl.program_id(a) for a in range(len(grid))]
        first = functools.reduce(jnp.logical_and, [i == 0 for i in ids])
        last = functools.reduce(jnp.logical_and, [i == g - 1 for i, g in zip(ids, grid)])

        @pl.when(first)
        def _():
            rider.start(rin, rout, send, recv)

        body(*ins, *outs, *scr)

        @pl.when(last)
        def _():
            rider.finish(rin, rout, send, recv)

    for ri, ro in rider.aliases.items():
        aliases[n_in + ri] = n_out + ro
    res = pl.pallas_call(
        wrapped, name=name, grid=grid, in_specs=list(in_specs) + [HBM] * nri, out_specs=ospecs + [HBM] * nro,
        out_shape=shapes + rider.outs,
        scratch_shapes=list(scratch_shapes) + [pltpu.SemaphoreType.DMA((rider.n_sems,)),
                                               pltpu.SemaphoreType.DMA((rider.n_sems,))],
        input_output_aliases=aliases, compiler_params=_params(*sem))(*args, *rider.ins)
    main, extra = res[:n_out], list(res[n_out:])
    return (list(main) if many else main[0]), extra


def _div_tile(n, mult, cap):
    best = n
    for t in range(mult, min(n, cap) + 1, mult):
        if n % t == 0:
            best = t
    return best


def _sigmoid(x):
    return 1.0 / (1.0 + jnp.exp(-x))


def _ln_stats(x):
    mu = jnp.mean(x, axis=-1, keepdims=True)
    xc = x - mu
    var = jnp.mean(xc * xc, axis=-1, keepdims=True)
    rstd = lax.rsqrt(var + LN_EPS)
    return xc * rstd, rstd


def _ln_bwd(dy, xhat, rstd, g):
    dyg = dy * g
    m1 = jnp.mean(dyg, axis=-1, keepdims=True)
    m2 = jnp.mean(dyg * xhat, axis=-1, keepdims=True)
    return rstd * (dyg - m1 - xhat * m2)


def _dot_nt(a, b):
    return lax.dot_general(a, b, (((1,), (1,)), ((), ())), preferred_element_type=F32)


def _dot_tn(a, b):
    return lax.dot_general(a, b, (((0,), (0,)), ((), ())), preferred_element_type=F32)


def _dot(a, b):
    return jnp.dot(a, b, preferred_element_type=F32)


def _dot3(m, x):
    mb = m.astype(BF16)
    x1 = x.astype(BF16)
    r1 = x - x1.astype(F32)
    x2 = r1.astype(BF16)
    x3 = (r1 - x2.astype(F32)).astype(BF16)
    return _dot(mb, x1) + _dot(mb, x2) + _dot(mb, x3)


def _place_shard(x, name, chip_idx, dtype):
    r, c = x.shape
    tr = _div_tile(r, 16, 512)

    def body(s_ref, x_ref, o_ref):
        del s_ref
        o_ref[...] = x_ref[...].astype(dtype)

    return pl.pallas_call(
        body, name=name,
        grid_spec=pltpu.PrefetchScalarGridSpec(
            num_scalar_prefetch=1, grid=(r // tr,),
            in_specs=[pl.BlockSpec((tr, c), lambda i, s: (i, 0))],
            out_specs=pl.BlockSpec((None, tr, c), lambda i, s: (s[0], i, 0))),
        out_shape=jax.ShapeDtypeStruct((N_CHIPS, r, c), dtype),
        compiler_params=_params("parallel"),
    )(chip_idx, x)


def _ln0(x, g, b, tm, rider=None):
    t, d = x.shape

    def body(x_ref, g_ref, b_ref, o_ref):
        xh, _ = _ln_stats(x_ref[...])
        o_ref[...] = (xh * g_ref[...] + b_ref[...]).astype(BF16)

    row = pl.BlockSpec((1, d), lambda i: (0, 0))
    return _call(
        body, (x, g, b), name="ln0", grid=(t // tm,),
        in_specs=[pl.BlockSpec((tm, d), lambda i: (i, 0)), row, row],
        out_specs=pl.BlockSpec((tm, d), lambda i: (i, 0)),
        out_shape=jax.ShapeDtypeStruct((t, d), BF16), rider=rider)


def _proj(name, a, w3, n_sec, tm, tn, rider=None):
    m, k = a.shape
    s, _, ws = w3.shape
    sec_w = s * ws // n_sec
    nj = ws // tn
    per_sec = sec_w // tn

    def body(a_ref, w_ref, o_ref):
        o_ref[...] = _dot(a_ref[...], w_ref[...])

    return _call(
        body, (a, w3), name=name, grid=(s * nj, m // tm),
        in_specs=[pl.BlockSpec((tm, k), lambda j, i: (i, 0)),
                  pl.BlockSpec((None, k, tn), lambda j, i: (j // nj, 0, j % nj))],
        out_specs=pl.BlockSpec((None, tm, tn), lambda j, i: (j // per_sec, i, j % per_sec)),
        out_shape=jax.ShapeDtypeStruct((n_sec, m, sec_w), F32), rider=rider)


def _proj_t(name, a, w, tm, tn, rider=None):
    m, k = a.shape
    n = w.shape[0]

    def body(a_ref, w_ref, o_ref):
        o_ref[...] = _dot_nt(a_ref[...], w_ref[...])

    return _call(
        body, (a, w), name=name, grid=(n // tn, m // tm),
        in_specs=[pl.BlockSpec((tm, k), lambda j, i: (i, 0)),
                  pl.BlockSpec((tn, k), lambda j, i: (j, 0))],
        out_specs=pl.BlockSpec((tm, tn), lambda j, i: (i, j)),
        out_shape=jax.ShapeDtypeStruct((m, n), F32), rider=rider)


def _wgrad(name, a, b, out_shape, grid, a_spec, b_spec, o_spec, rider=None, dot=_dot_tn):
    nt = len(grid) - 1

    def body(a_ref, b_ref, o_ref):
        t = pl.program_id(nt)
        prod = dot(a_ref[...], b_ref[...])

        @pl.when(t == 0)
        def _():
            o_ref[...] = prod

        @pl.when(t > 0)
        def _():
            o_ref[...] += prod

    return _call(
        body, (a, b), name=name, grid=grid, in_specs=[a_spec, b_spec], out_specs=o_spec,
        out_shape=jax.ShapeDtypeStruct(out_shape, F32), rider=rider)


def _wgrad_in(h0b, dp3, n_shards, tt, rider=None):
    t, d = h0b.shape
    n_sec, _, sec_w = dp3.shape
    ws = n_sec * sec_w // n_shards
    tn = sec_w // 2
    nq = ws // tn
    tr = d // 2

    def body(*refs):
        a_ref, b_refs, o_ref = refs[0], refs[1:1 + nq], refs[1 + nq]
        k = pl.program_id(2)
        a = a_ref[...]
        prods = [_dot_tn(a, b_ref[...]) for b_ref in b_refs]

        @pl.when(k == 0)
        def _():
            for q in range(nq):
                o_ref[:, q * tn:(q + 1) * tn] = prods[q]

        @pl.when(k > 0)
        def _():
            for q in range(nq):
                o_ref[:, q * tn:(q + 1) * tn] += prods[q]

    def b_spec(q):
        return pl.BlockSpec((None, tt, tn), lambda s, r, k: ((nq * s + q) // 2, k, (nq * s + q) % 2))

    return _call(
        body, (h0b,) + (dp3,) * nq, name="wgrad_in", grid=(n_shards, d // tr, t // tt),
        in_specs=[pl.BlockSpec((tt, tr), lambda s, r, k: (k, r))] + [b_spec(q) for q in range(nq)],
        out_specs=pl.BlockSpec((None, tr, ws), lambda s, r, k: (s, r, 0)),
        out_shape=jax.ShapeDtypeStruct((n_shards, d, ws), F32), rider=rider)


def _in_t(dp3, w_in3, tm, rider=None):
    _, t, sec_w = dp3.shape
    s, d, ws = w_in3.shape
    tk = sec_w // 2
    nq = ws // tk

    def body(*refs):
        a_refs, w_ref, o_ref = refs[:nq], refs[nq], refs[nq + 1]
        k = pl.program_id(1)
        prod = _dot_nt(a_refs[0][...], w_ref[:, 0:tk])
        for q in range(1, nq):
            prod = prod + _dot_nt(a_refs[q][...], w_ref[:, q * tk:(q + 1) * tk])

        @pl.when(k == 0)
        def _():
            o_ref[...] = prod

        @pl.when(k > 0)
        def _():
            o_ref[...] += prod

    def a_spec(q):
        return pl.BlockSpec((None, tm, tk), lambda i, k: ((nq * k + q) // 2, i, (nq * k + q) % 2))

    return _call(
        body, (dp3,) * nq + (w_in3,), name="in_t", grid=(t // tm, s),
        in_specs=[a_spec(q) for q in range(nq)] + [pl.BlockSpec((None, d, ws), lambda i, k: (k, 0, 0))],
        out_specs=pl.BlockSpec((tm, d), lambda i, k: (i, 0)),
        out_shape=jax.ShapeDtypeStruct((t, d), F32), rider=rider)


def _mix_ln1(cat, w_out, x, g0, b0, g1, b1, alpha, tm, rider=None):
    t, d = x.shape

    def body(cat_ref, w_ref, x_ref, g0_ref, b0_ref, g1_ref, b1_ref, xh_ref, h1b_ref, rstd_ref):
        mix = _dot(cat_ref[...], w_ref[...])
        xh0, _ = _ln_stats(x_ref[...])
        z1 = alpha * (xh0 * g0_ref[...] + b0_ref[...]) + mix
        xh1, rstd1 = _ln_stats(z1)
        xh_ref[...] = xh1
        h1b_ref[...] = (xh1 * g1_ref[...] + b1_ref[...]).astype(BF16)
        rstd_ref[...] = rstd1

    row = pl.BlockSpec((1, d), lambda i: (0, 0))
    blk = pl.BlockSpec((tm, d), lambda i: (i, 0))
    return _call(
        body, (cat, w_out, x, g0, b0, g1, b1), name="mix_ln1", grid=(t // tm,),
        in_specs=[blk, pl.BlockSpec((d, d), lambda i: (0, 0)), blk, row, row, row, row],
        out_specs=[blk, blk, pl.BlockSpec((tm, 1), lambda i: (i, 0))],
        out_shape=[jax.ShapeDtypeStruct((t, d), F32), jax.ShapeDtypeStruct((t, d), BF16),
                   jax.ShapeDtypeStruct((t, 1), F32)], rider=rider)


def _ln2_loss(ffn, xhat1, tgt, g1, b1, g2, b2, alpha, tm):
    t, d = xhat1.shape
    ni = t // tm
    inv_d = 1.0 / d

    def body(ffn_ref, xh1_ref, tgt_ref, g1_ref, b1_ref, g2_ref, b2_ref,
             dz2_ref, dz2b_ref, dg2_ref, db2_ref, loss_ref, lrow):
        i = pl.program_id(0)
        h1 = xh1_ref[...] * g1_ref[...] + b1_ref[...]
        xh2, rstd2 = _ln_stats(alpha * h1 + ffn_ref[...])
        g2v = g2_ref[...]
        diff = xh2 * g2v + b2_ref[...] - tgt_ref[...]
        dh2 = diff * inv_d
        sq = jnp.sum(diff * diff, axis=0, keepdims=True)
        dg = jnp.sum(dh2 * xh2, axis=0, keepdims=True)
        db = jnp.sum(dh2, axis=0, keepdims=True)

        @pl.when(i == 0)
        def _():
            lrow[...] = sq
            dg2_ref[...] = dg
            db2_ref[...] = db

        @pl.when(i > 0)
        def _():
            lrow[...] += sq
            dg2_ref[...] += dg
            db2_ref[...] += db

        dz2 = _ln_bwd(dh2, xh2, rstd2, g2v)
        dz2_ref[...] = dz2
        dz2b_ref[...] = dz2.astype(BF16)

        @pl.when(i == ni - 1)
        def _():
            tot = jnp.sum(lrow[...], axis=-1, keepdims=True) * (0.5 * inv_d)
            loss_ref[...] = jnp.broadcast_to(tot, (1, LANE))

    row = pl.BlockSpec((1, d), lambda i: (0, 0))
    blk = pl.BlockSpec((tm, d), lambda i: (i, 0))
    return _call(
        body, (ffn, xhat1, tgt, g1, b1, g2, b2), name="ln2_loss", grid=(ni,),
        in_specs=[blk, blk, blk, row, row, row, row],
        out_specs=[blk, blk, row, row, pl.BlockSpec((1, LANE), lambda i: (0, 0))],
        out_shape=[jax.ShapeDtypeStruct((t, d), F32), jax.ShapeDtypeStruct((t, d), BF16),
                   jax.ShapeDtypeStruct((1, d), F32), jax.ShapeDtypeStruct((1, d), F32),
                   jax.ShapeDtypeStruct((1, LANE), F32)],
        scratch_shapes=[pltpu.VMEM((1, d), F32)])


def _ln1_bwd(pre, dz2, xhat1, rstd1, g1, alpha, tm):
    t, d = dz2.shape

    def body(pre_ref, dz2_ref, xh_ref, rstd_ref, g_ref, dz1_ref, dz1b_ref, dg_ref, db_ref):
        i = pl.program_id(0)
        dh1 = alpha * dz2_ref[...] + pre_ref[...]
        xh = xh_ref[...]
        dg = jnp.sum(dh1 * xh, axis=0, keepdims=True)
        db = jnp.sum(dh1, axis=0, keepdims=True)

        @pl.when(i == 0)
        def _():
            dg_ref[...] = dg
            db_ref[...] = db

        @pl.when(i > 0)
        def _():
            dg_ref[...] += dg
            db_ref[...] += db

        dz1 = _ln_bwd(dh1, xh, rstd_ref[...], g_ref[...])
        dz1_ref[...] = dz1
        dz1b_ref[...] = dz1.astype(BF16)

    row = pl.BlockSpec((1, d), lambda i: (0, 0))
    blk = pl.BlockSpec((tm, d), lambda i: (i, 0))
    return _call(
        body, (pre, dz2, xhat1, rstd1, g1), name="ln1_bwd", grid=(t // tm,),
        in_specs=[blk, blk, blk, pl.BlockSpec((tm, 1), lambda i: (i, 0)), row],
        out_specs=[blk, blk, row, row],
        out_shape=[jax.ShapeDtypeStruct((t, d), F32), jax.ShapeDtypeStruct((t, d), BF16),
                   jax.ShapeDtypeStruct((1, d), F32), jax.ShapeDtypeStruct((1, d), F32)])


def _ln0_bwd(pre, dz1, x, g0, alpha, tm):
    t, d = x.shape

    def body(pre_ref, dz1_ref, x_ref, g_ref, dx_ref, dg_ref, db_ref):
        i = pl.program_id(0)
        dh0 = alpha * dz1_ref[...] + pre_ref[...]
        xh, rstd = _ln_stats(x_ref[...])
        dg = jnp.sum(dh0 * xh, axis=0, keepdims=True)
        db = jnp.sum(dh0, axis=0, keepdims=True)

        @pl.when(i == 0)
        def _():
            dg_ref[...] = dg
            db_ref[...] = db

        @pl.when(i > 0)
        def _():
            dg_ref[...] += dg
            db_ref[...] += db

        dx_ref[...] = _ln_bwd(dh0, xh, rstd, g_ref[...])

    row = pl.BlockSpec((1, d), lambda i: (0, 0))
    blk = pl.BlockSpec((tm, d), lambda i: (i, 0))
    return _call(
        body, (pre, dz1, x, g0), name="ln0_bwd", grid=(t // tm,),
        in_specs=[blk, blk, blk, row], out_specs=[blk, row, row],
        out_shape=[jax.ShapeDtypeStruct((t, d), F32), jax.ShapeDtypeStruct((1, d), F32),
                   jax.ShapeDtypeStruct((1, d), F32)])


def _shift_copies(ext, shifted):
    n = shifted.shape[1]
    for p in range(1, SUBLANE):
        shifted[p - 1] = ext[pl.ds(p, n), :]


def _window(ext, shifted, start, rows):
    p = start % SUBLANE
    if p == 0:
        return ext[pl.ds(start, rows), :]
    return shifted[p - 1, pl.ds(start - p, rows), :]


def _conv_fwd(p3, conv_w, conv_b, cn_g, cn_b, tc, cb, rider=None):
    _, t, w = p3.shape
    kk = conv_w.shape[0]
    off = HALO - (kk - 1)
    hb = tc // HALO

    def body(a_ref, g_ref, ap_ref, gp_ref, w_ref, b_ref, ng_ref, nb_ref, cat_ref, u1_ref, ext, sh):
        i = pl.program_id(1)
        ext[pl.ds(HALO, tc), :] = a_ref[...] * _sigmoid(g_ref[...])
        prev = ap_ref[...] * _sigmoid(gp_ref[...])
        ext[pl.ds(0, HALO), :] = jnp.where(i > 0, prev, 0.0)
        _shift_copies(ext, sh)
        for r in range(tc // ROWS):
            acc = jnp.broadcast_to(b_ref[...], (ROWS, cb))
            for k in range(kk):
                acc = acc + w_ref[k:k + 1, :] * _window(ext, sh, r * ROWS + off + k, ROWS)
            u1_ref[pl.ds(r * ROWS, ROWS), :] = acc
            for g in range(cb // LANE):
                sl = slice(g * LANE, (g + 1) * LANE)
                xh, _ = _ln_stats(acc[:, sl])
                u2 = xh * ng_ref[:, sl] + nb_ref[:, sl]
                cat_ref[pl.ds(r * ROWS, ROWS), sl] = (u2 * _sigmoid(u2)).astype(BF16)

    cur = lambda sec: pl.BlockSpec((None, tc, cb), lambda j, i: (sec, i, j))
    prev = lambda sec: pl.BlockSpec((None, HALO, cb), lambda j, i: (sec, jnp.maximum(i * hb - 1, 0), j))
    row = pl.BlockSpec((1, cb), lambda j, i: (0, j))
    return _call(
        body, (p3, p3, p3, p3, conv_w, conv_b, cn_g, cn_b), name="conv_fwd", grid=(w // cb, t // tc),
        in_specs=[cur(0), cur(1), prev(0), prev(1), pl.BlockSpec((kk, cb), lambda j, i: (0, j)), row, row, row],
        out_specs=[pl.BlockSpec((tc, cb), lambda j, i: (i, j)), pl.BlockSpec((tc, cb), lambda j, i: (i, j))],
        out_shape=[jax.ShapeDtypeStruct((t, 2 * w), BF16), jax.ShapeDtypeStruct((t, w), F32)],
        scratch_shapes=[pltpu.VMEM((tc + HALO, cb), F32),
                        pltpu.VMEM((SUBLANE - 1, tc + HALO - SUBLANE, cb), F32)], rider=rider)


def _conv_norm_bwd(dcat, u1, cn_g, cn_b, tc):
    t, w = u1.shape

    def body(du_ref, u1_ref, ng_ref, nb_ref, du1_ref, dg_ref, db_ref):
        i = pl.program_id(0)
        for g in range(w // LANE):
            sl = slice(g * LANE, (g + 1) * LANE)
            ng = ng_ref[:, sl]
            xh, rstd = _ln_stats(u1_ref[:, sl])
            u2 = xh * ng + nb_ref[:, sl]
            sg = _sigmoid(u2)
            du2 = du_ref[:, sl] * (sg * (1.0 + u2 * (1.0 - sg)))
            dg = jnp.sum(du2 * xh, axis=0, keepdims=True)
            db = jnp.sum(du2, axis=0, keepdims=True)

            @pl.when(i == 0)
            def _():
                dg_ref[:, sl] = dg
                db_ref[:, sl] = db

            @pl.when(i > 0)
            def _():
                dg_ref[:, sl] += dg
                db_ref[:, sl] += db

            du1_ref[:, sl] = _ln_bwd(du2, xh, rstd, ng)

    row = pl.BlockSpec((1, w), lambda i: (0, 0))
    blk = pl.BlockSpec((tc, w), lambda i: (i, 0))
    return pl.pallas_call(
        body, name="conv_norm_bwd", grid=(t // tc,),
        in_specs=[blk, blk, row, row], out_specs=[blk, row, row],
        out_shape=[jax.ShapeDtypeStruct((t, w), F32), jax.ShapeDtypeStruct((1, w), F32),
                   jax.ShapeDtypeStruct((1, w), F32)],
        compiler_params=_params("arbitrary"),
    )(dcat, u1, cn_g, cn_b)


def _conv_bwd(du1, p3, conv_w, tc, cb, rider=None):
    n_sec, t, w = p3.shape
    kk = conv_w.shape[0]
    off = HALO - (kk - 1)
    hb = tc // HALO
    nt = t // tc
    kpad = -(-kk // SUBLANE) * SUBLANE

    def body(d_ref, dn_ref, a_ref, g_ref, ap_ref, gp_ref, w_ref, dp_ref, dw_ref, db_ref,
             extd, extu, shd, shu, wacc, bacc):
        i = pl.program_id(1)

        @pl.when(i == 0)
        def _():
            wacc[...] = jnp.zeros_like(wacc)
            bacc[...] = jnp.zeros_like(bacc)

        extd[pl.ds(0, tc), :] = d_ref[...]
        extd[pl.ds(tc, HALO), :] = jnp.where(i < nt - 1, dn_ref[...], 0.0)
        extu[pl.ds(HALO, tc), :] = a_ref[...] * _sigmoid(g_ref[...])
        extu[pl.ds(0, HALO), :] = jnp.where(i > 0, ap_ref[...] * _sigmoid(gp_ref[...]), 0.0)
        _shift_copies(extd, shd)
        _shift_copies(extu, shu)
        for r in range(tc // ROWS):
            rows = pl.ds(r * ROWS, ROWS)
            acc = jnp.zeros((ROWS, cb), F32)
            for k in range(kk):
                acc = acc + w_ref[k:k + 1, :] * _window(extd, shd, r * ROWS + (kk - 1) - k, ROWS)
            a = a_ref[rows, :]
            sg = _sigmoid(g_ref[rows, :])
            dp_ref[0, rows, :] = (acc * sg).astype(BF16)
            dp_ref[1, rows, :] = (acc * a * sg * (1.0 - sg)).astype(BF16)
            d = d_ref[rows, :]
            bacc[...] += jnp.sum(d.reshape(ROWS // SUBLANE, SUBLANE, cb), axis=0)
            for k in range(kk):
                prod = d * _window(extu, shu, r * ROWS + off + k, ROWS)
                wacc[k] += jnp.sum(prod.reshape(ROWS // SUBLANE, SUBLANE, cb), axis=0)

        @pl.when(i == nt - 1)
        def _():
            for k in range(kk):
                dw_ref[k:k + 1, :] = jnp.sum(wacc[k], axis=0, keepdims=True)
            if kpad > kk:
                dw_ref[kk:kpad, :] = jnp.zeros((kpad - kk, cb), F32)
            db_ref[...] = jnp.sum(bacc[...], axis=0, keepdims=True)

    cur = lambda sec: pl.BlockSpec((None, tc, cb), lambda j, i: (sec, i, j))
    prev = lambda sec: pl.BlockSpec((None, HALO, cb), lambda j, i: (sec, jnp.maximum(i * hb - 1, 0), j))
    return _call(
        body, (du1, du1, p3, p3, p3, p3, conv_w), name="conv_bwd", grid=(w // cb, nt),
        in_specs=[pl.BlockSpec((tc, cb), lambda j, i: (i, j)),
                  pl.BlockSpec((HALO, cb), lambda j, i: (jnp.minimum((i + 1) * hb, t // HALO - 1), j)),
                  cur(0), cur(1), prev(0), prev(1), pl.BlockSpec((kk, cb), lambda j, i: (0, j))],
        out_specs=[pl.BlockSpec((2, tc, cb), lambda j, i: (0, i, j)),
                   pl.BlockSpec((kpad, cb), lambda j, i: (0, j)),
                   pl.BlockSpec((1, cb), lambda j, i: (0, j))],
        out_shape=[jax.ShapeDtypeStruct((n_sec, t, w), BF16), jax.ShapeDtypeStruct((kpad, w), F32),
                   jax.ShapeDtypeStruct((1, w), F32)],
        scratch_shapes=[pltpu.VMEM((tc + HALO, cb), F32), pltpu.VMEM((tc + HALO, cb), F32),
                        pltpu.VMEM((SUBLANE - 1, tc + HALO - SUBLANE, cb), F32),
                        pltpu.VMEM((SUBLANE - 1, tc + HALO - SUBLANE, cb), F32),
                        pltpu.VMEM((kk, SUBLANE, cb), F32), pltpu.VMEM((SUBLANE, cb), F32)], rider=rider)


def _ffn_act_fwd(hh3, fw, fb, tc, cb, rider=None):
    _, t, dff = hh3.shape
    kk = fw.shape[0]
    off = FHALO - (kk - 1)
    hb = tc // FHALO

    def body(g_ref, v_ref, gp_ref, w_ref, b_ref, act_ref, ext):
        i = pl.program_id(1)
        ext[pl.ds(FHALO, tc), :] = g_ref[...]
        ext[pl.ds(0, FHALO), :] = jnp.where(i > 0, gp_ref[...], 0.0)
        for r in range(tc // ROWS):
            rows = pl.ds(r * ROWS, ROWS)
            gc = jnp.broadcast_to(b_ref[...], (ROWS, cb))
            for k in range(kk):
                gc = gc + w_ref[k:k + 1, :] * ext[pl.ds(r * ROWS + off + k, ROWS), :]
            act_ref[rows, :] = (gc * _sigmoid(gc) * v_ref[rows, :]).astype(BF16)

    return _call(
        body, (hh3, hh3, hh3, fw, fb), name="ffn_act_fwd", grid=(dff // cb, t // tc),
        in_specs=[pl.BlockSpec((None, tc, cb), lambda j, i: (0, i, j)),
                  pl.BlockSpec((None, tc, cb), lambda j, i: (1, i, j)),
                  pl.BlockSpec((None, FHALO, cb), lambda j, i: (0, jnp.maximum(i * hb - 1, 0), j)),
                  pl.BlockSpec((kk, cb), lambda j, i: (0, j)),
                  pl.BlockSpec((1, cb), lambda j, i: (0, j))],
        out_specs=pl.BlockSpec((tc, cb), lambda j, i: (i, j)),
        out_shape=jax.ShapeDtypeStruct((t, dff), BF16),
        scratch_shapes=[pltpu.VMEM((tc + FHALO, cb), F32)], rider=rider)


def _ffn_act_bwd(dact, hh3, fw, fb, tc, cb):
    _, t, dff = hh3.shape
    kk = fw.shape[0]
    off = FHALO - (kk - 1)
    hb = tc // FHALO
    nt = t // tc
    te = tc + FHALO
    shifts = sorted({(off + k) % SUBLANE for k in range(kk)} - {0})

    def body(da_ref, dan_ref, g_ref, gp_ref, gn_ref, v_ref, vn_ref, w_ref, b_ref,
             dhh_ref, dw_ref, db_ref, gext, gsh, dext, wacc, bacc):
        i = pl.program_id(1)

        @pl.when(i == 0)
        def _():
            wacc[...] = jnp.zeros_like(wacc)
            bacc[...] = jnp.zeros_like(bacc)

        gext[pl.ds(0, FHALO), :] = jnp.where(i > 0, gp_ref[...], 0.0)
        gext[pl.ds(FHALO, tc), :] = g_ref[...]
        gext[pl.ds(FHALO + tc, FHALO), :] = gn_ref[...]
        for s, p in enumerate(shifts):
            gsh[s] = gext[pl.ds(p, te), :]

        def gwin(k, r0, n):
            p = (off + k) % SUBLANE
            if p == 0:
                return gext[pl.ds(r0 + off + k, n), :]
            return gsh[shifts.index(p), pl.ds(r0 + off + k - p, n), :]

        def gate_grad(r0, n, da, v):
            gc = jnp.broadcast_to(b_ref[...], (n, cb))
            for k in range(kk):
                gc = gc + w_ref[k:k + 1, :] * gwin(k, r0, n)
            sg = _sigmoid(gc)
            return gc * sg, da * v * (sg * (1.0 + gc * (1.0 - sg)))

        for r in range(tc // ROWS):
            rows = pl.ds(r * ROWS, ROWS)
            da = da_ref[rows, :]
            silu, dgc = gate_grad(r * ROWS, ROWS, da, v_ref[rows, :])
            dext[rows, :] = dgc
            dhh_ref[1, rows, :] = (da * silu).astype(BF16)
        _, dgc_next = gate_grad(tc, FHALO, dan_ref[...], vn_ref[...])
        dext[pl.ds(tc, FHALO), :] = jnp.where(i < nt - 1, dgc_next, 0.0)
        for r in range(tc // ROWS):
            rows = pl.ds(r * ROWS, ROWS)
            dg = jnp.zeros((ROWS, cb), F32)
            for k in range(kk):
                dg = dg + w_ref[k:k + 1, :] * dext[pl.ds(r * ROWS + (kk - 1) - k, ROWS), :]
            dhh_ref[0, rows, :] = dg.astype(BF16)
            dgc = dext[rows, :]
            bacc[...] += jnp.sum(dgc.reshape(ROWS // SUBLANE, SUBLANE, cb), axis=0)
            for k in range(kk):
                prod = dgc * gwin(k, r * ROWS, ROWS)
                wacc[k] += jnp.sum(prod.reshape(ROWS // SUBLANE, SUBLANE, cb), axis=0)

        @pl.when(i == nt - 1)
        def _():
            for k in range(kk):
                dw_ref[k:k + 1, :] = jnp.sum(wacc[k], axis=0, keepdims=True)
            dw_ref[kk:SUBLANE, :] = jnp.zeros((SUBLANE - kk, cb), F32)
            db_ref[...] = jnp.sum(bacc[...], axis=0, keepdims=True)

    nxt = lambda i: jnp.minimum((i + 1) * hb, t // FHALO - 1)
    return pl.pallas_call(
        body, name="ffn_act_bwd", grid=(dff // cb, nt),
        in_specs=[pl.BlockSpec((tc, cb), lambda j, i: (i, j)),
                  pl.BlockSpec((FHALO, cb), lambda j, i: (nxt(i), j)),
                  pl.BlockSpec((None, tc, cb), lambda j, i: (0, i, j)),
                  pl.BlockSpec((None, FHALO, cb), lambda j, i: (0, jnp.maximum(i * hb - 1, 0), j)),
                  pl.BlockSpec((None, FHALO, cb), lambda j, i: (0, nxt(i), j)),
                  pl.BlockSpec((None, tc, cb), lambda j, i: (1, i, j)),
                  pl.BlockSpec((None, FHALO, cb), lambda j, i: (1, nxt(i), j)),
                  pl.BlockSpec((kk, cb), lambda j, i: (0, j)),
                  pl.BlockSpec((1, cb), lambda j, i: (0, j))],
        out_specs=[pl.BlockSpec((2, tc, cb), lambda j, i: (0, i, j)),
                   pl.BlockSpec((SUBLANE, cb), lambda j, i: (0, j)),
                   pl.BlockSpec((1, cb), lambda j, i: (0, j))],
        out_shape=[jax.ShapeDtypeStruct((2, t, dff), BF16), jax.ShapeDtypeStruct((SUBLANE, dff), F32),
                   jax.ShapeDtypeStruct((1, dff), F32)],
        scratch_shapes=[pltpu.VMEM((tc + 2 * FHALO, cb), F32), pltpu.VMEM((len(shifts), te, cb), F32),
                        pltpu.VMEM((te, cb), F32),
                        pltpu.VMEM((kk, SUBLANE, cb), F32), pltpu.VMEM((SUBLANE, cb), F32)],
        compiler_params=_params("parallel", "arbitrary"),
    )(dact, dact, hh3, hh3, hh3, hh3, hh3, fw, fb)


def _chunk_consts():
    r = lax.broadcasted_iota(jnp.int32, (CHUNK, CHUNK), 0)
    c = lax.broadcasted_iota(jnp.int32, (CHUNK, CHUNK), 1)
    return (c <= r).astype(F32)


def _roll8(x, d):
    return pltpu.roll(x.reshape(CHUNK // SUB, SUB, LANE), d % SUB, 1).reshape(CHUNK, LANE)


def _gate_terms(q, fpre, lb):
    sf = _sigmoid(fpre)
    fg = lb + (1.0 - lb) * sf
    sq = _sigmoid(q)
    return sf, fg, 1.0 - fg, sq, q * sq


def _decays(g, consts):
    b = _dot3(consts, g)
    nb = CHUNK // SUB
    ends = b.reshape(nb, SUB, LANE)[:, SUB - 1:SUB, :]
    re3 = jnp.broadcast_to(ends, (nb, SUB, LANE))
    rs3 = jnp.concatenate([jnp.zeros((1, SUB, LANE), F32), re3[:nb - 1]], axis=0)
    return b, rs3.reshape(CHUNK, LANE), re3.reshape(CHUNK, LANE), b[CHUNK - 1:CHUNK]


def _lower_bound(lb_ref):
    l0, l1 = lb_ref[0:1, :], lb_ref[1:2, :]
    mx = jnp.maximum(l0, l1)
    e0, e1 = jnp.exp(l0 - mx), jnp.exp(l1 - mx)
    return e0 / (e0 + e1)


BF16_ROWS = 16


def _scaled_keys(kt, rs, re, i):
    n = SUB * i
    scale = jnp.exp(rs[n:n + 1, :] - re[:n])
    live = kt[:n] * scale
    m = -(-n // BF16_ROWS) * BF16_ROWS
    if m > n:
        live = jnp.concatenate([live, jnp.zeros((m - n, LANE), F32)], axis=0)
    parts = [live.astype(BF16)]
    if m < CHUNK:
        parts.append(jnp.zeros((CHUNK - m, LANE), BF16))
    return jnp.concatenate(parts, axis=0), scale


def _only_block(x, i):
    w0 = (SUB * i // BF16_ROWS) * BF16_ROWS
    win = x[w0:w0 + BF16_ROWS]
    inblk = lax.broadcasted_iota(jnp.int32, (BF16_ROWS, 1), 0) // SUB == (SUB * i - w0) // SUB
    parts = [jnp.where(inblk, win, 0.0).astype(BF16)]
    if w0:
        parts.insert(0, jnp.zeros((w0, x.shape[1]), BF16))
    if w0 + BF16_ROWS < CHUNK:
        parts.append(jnp.zeros((CHUNK - w0 - BF16_ROWS, x.shape[1]), BF16))
    return jnp.concatenate(parts, axis=0)


def _hgrn_fwd(p3, lb_logits, hg, cat, tb, hpb, rider=None):
    _, t, w = p3.shape
    nh = w // LANE
    nc = tb // CHUNK
    assert nh % hpb == 0

    def body(q_ref, f_ref, v_ref, og_ref, lb_ref, hg_ref, cat_in, cat_ref, o_ref, st_ref, state):
        del cat_in
        consts = _chunk_consts()
        lb_all = _lower_bound(lb_ref)
        rowpos = lax.broadcasted_iota(jnp.int32, (CHUNK, 1), 0) % SUB

        @pl.when(pl.program_id(1) == 0)
        def _():
            state[...] = jnp.zeros_like(state)

        def chunk(c, carry):
            rows = pl.ds(pl.multiple_of(c * CHUNK, CHUNK), CHUNK)
            heads = range(hpb)
            sls = [slice(j * LANE, (j + 1) * LANE) for j in heads]
            v = [v_ref[rows, s] for s in sls]
            vb = [x.astype(BF16) for x in v]
            gates = [_gate_terms(q_ref[rows, s], f_ref[rows, s], lb_all[:, s]) for s in sls]
            fg = [g[1] for g in gates]
            kk = [g[2] for g in gates]
            qh = [g[4] for g in gates]
            dec = [_decays(jnp.log(x), consts) for x in fg]
            b = [x[0] for x in dec]
            rs = [x[1] for x in dec]
            re = [x[2] for x in dec]
            tot = [x[3] for x in dec]
            qt = [qh[j] * jnp.exp(b[j] - rs[j]) for j in heads]
            kt = [kk[j] * jnp.exp(re[j] - b[j]) for j in heads]
            st = [state[j] for j in heads]
            for j in heads:
                st_ref[j, c] = st[j]
            a = [jnp.zeros((CHUNK, CHUNK), F32) for _ in heads]
            for i in range(1, CHUNK // SUB):
                for j in heads:
                    kib, _ = _scaled_keys(kt[j], rs[j], re[j], i)
                    a[j] = a[j] + _dot_nt(_only_block(qt[j], i), kib)
            o = [_dot(a[j].astype(BF16), vb[j]) for j in heads]
            o = [o[j] + _dot_nt((qh[j] * jnp.exp(b[j])).astype(BF16), st[j].astype(BF16)) for j in heads]
            for j in heads:
                k_up = kk[j] * jnp.exp(tot[j] - b[j])
                state[j] = st[j] * jnp.exp(tot[j]) + _dot_tn(vb[j], k_up.astype(BF16))
            for j in heads:
                e, rf = None, fg[j]
                for d in range(SUB):
                    if d == 0:
                        vs, term = v[j], qh[j] * kk[j]
                    else:
                        e = rf if e is None else e * rf
                        rf = _roll8(fg[j], d)
                        vs = _roll8(v[j], d)
                        term = jnp.where(rowpos >= d, qh[j] * (1.0 - rf) * e, 0.0)
                    o[j] = o[j] + jnp.sum(term, axis=-1, keepdims=True) * vs
            for j in heads:
                og = og_ref[rows, sls[j]]
                o_ref[rows, sls[j]] = o[j]
                r = lax.rsqrt(jnp.mean(o[j] * o[j], axis=-1, keepdims=True) + RMS_EPS)
                cat_ref[rows, sls[j]] = (o[j] * r * hg_ref[:, sls[j]] * (og * _sigmoid(og))).astype(BF16)
            return carry

        lax.fori_loop(0, nc, chunk, 0)

    bw = hpb * LANE
    sec = lambda s: pl.BlockSpec((None, tb, bw), lambda h, i: (s, i, h))
    return _call(
        body, (p3, p3, p3, p3, lb_logits, hg, cat), name="hgrn_fwd", grid=(nh // hpb, t // tb),
        in_specs=[sec(2), sec(3), sec(4), sec(5),
                  pl.BlockSpec((2, bw), lambda h, i: (0, h)),
                  pl.BlockSpec((1, bw), lambda h, i: (0, h)), HBM],
        out_specs=[pl.BlockSpec((tb, bw), lambda h, i: (i, nh // hpb + h)),
                   pl.BlockSpec((tb, bw), lambda h, i: (i, h)),
                   pl.BlockSpec((hpb, nc, LANE, LANE), lambda h, i: (h, i, 0, 0))],
        out_shape=[jax.ShapeDtypeStruct(cat.shape, BF16), jax.ShapeDtypeStruct((t, w), F32),
                   jax.ShapeDtypeStruct((nh, t // CHUNK, LANE, LANE), F32)],
        scratch_shapes=[pltpu.VMEM((hpb, LANE, LANE), F32)], aliases={6: 0}, rider=rider)


def _hgrn_bwd(p3, lb_logits, hg, o_pre, states, dcat, dp3, tb, hpb, rider=None):
    n_sec, t, w = p3.shape
    nh = w // LANE
    assert nh % hpb == 0
    nc = tb // CHUNK
    nb = t // tb
    bw = hpb * LANE
    n_steps = (nh // hpb) * nb

    def body(q_ref, f_ref, v_ref, og_ref, lb_ref, hg_ref, o_ref, st_ref, dc_ref, dp_in,
             dp_ref, dlb_ref, dhg_ref, dstate, stash, lbacc, hgacc, osem):
        del dp_in
        h, i = pl.program_id(0), pl.program_id(1)
        step = h * nb + i
        slot = step % 2

        def out_copy(s, row_blk, lane_blk):
            dst = dp_ref.at[pl.ds(2, 4), pl.ds(row_blk * tb, tb), pl.ds(lane_blk * bw, bw)]
            return pltpu.make_async_copy(stash.at[s], dst, osem.at[s])

        @pl.when(step >= 2)
        def _():
            out_copy(slot, 0, 0).wait()

        def compute():
            consts = _chunk_consts()
            rr = lax.broadcasted_iota(jnp.int32, (CHUNK, CHUNK), 0)
            cc = lax.broadcasted_iota(jnp.int32, (CHUNK, CHUNK), 1)
            upper = (cc >= rr).astype(F32)
            lb_all = _lower_bound(lb_ref)
            rowpos = lax.broadcasted_iota(jnp.int32, (CHUNK, 1), 0) % SUB

            @pl.when(i == 0)
            def _():
                dstate[...] = jnp.zeros_like(dstate)
                lbacc[...] = jnp.zeros_like(lbacc)
                hgacc[...] = jnp.zeros_like(hgacc)

            def head(j, c, rows):
                sl = slice(j * LANE, (j + 1) * LANE)
                lb = lb_all[:, sl]
                hgv = hg_ref[:, sl]
                q = q_ref[rows, sl]
                v = v_ref[rows, sl]
                og = og_ref[rows, sl]
                o = o_ref[rows, sl]
                dcg = dc_ref[rows, sl]
                sf, fg, kk, sq, qh = _gate_terms(q, f_ref[rows, sl], lb)
                b, rs, re, tot = _decays(jnp.log(fg), consts)
                eq = jnp.exp(b - rs)
                ek = jnp.exp(re - b)
                qt = qh * eq
                kt = kk * ek
                e_in = jnp.exp(b)
                e_up = jnp.exp(tot - b)
                e_tot = jnp.exp(tot)
                q_in = (qh * e_in).astype(BF16)
                k_up = (kk * e_up).astype(BF16)
                vb = v.astype(BF16)
                st = st_ref[j, c]
                dst = dstate[j]
                dstb = dst.astype(BF16)
                yield

                sg = _sigmoid(og)
                r = lax.rsqrt(jnp.mean(o * o, axis=-1, keepdims=True) + RMS_EPS)
                ohat = o * r
                d_og = dcg * ohat * hgv * (sg * (1.0 + og * (1.0 - sg)))
                d_on = dcg * (og * sg)
                hgacc[:, sl] += jnp.sum((d_on * ohat).reshape(CHUNK // SUBLANE, SUBLANE, LANE), axis=0)
                d_oh = d_on * hgv
                do = r * (d_oh - ohat * jnp.mean(d_oh * ohat, axis=-1, keepdims=True))
                dob = do.astype(BF16)

                da = _dot_nt(dob, vb)
                yield
                a_off = jnp.zeros((CHUNK, CHUNK), F32)
                dqt = jnp.zeros((CHUNK, LANE), F32)
                dkt = jnp.zeros((CHUNK, LANE), F32)
                for blk in range(1, CHUNK // SUB):
                    n = SUB * blk
                    kib, scale = _scaled_keys(kt, rs, re, blk)
                    qib = _only_block(qt, blk)
                    dab = _only_block(da, blk)
                    a_off = a_off + _dot_nt(qib, kib)
                    dqt = dqt + _dot(dab, kib)
                    dkt = jnp.concatenate([dkt[:n] + _dot_tn(dab, qib)[:n] * scale, dkt[n:]], axis=0)
                    yield
                dqh = dqt * eq
                dk = dkt * ek
                dv = _dot_tn(a_off.astype(BF16), dob)

                dqh = dqh + _dot(dob, st.astype(BF16)) * e_in
                dk = dk + _dot(vb, dstb) * e_up
                dv = dv + _dot_nt(k_up, dstb)
                st_end = st * e_tot + _dot_tn(vb, k_up)
                carry_g = jnp.sum(st_end * dst, axis=0, keepdims=True)
                dstate[j] = dst * e_tot + _dot_tn(dob, q_in)
                yield

                e, rf = None, fg
                for d in range(SUB):
                    if d == 0:
                        a_d = jnp.sum(qh * kk, axis=-1, keepdims=True)
                        da_d = jnp.sum(do * v, axis=-1, keepdims=True)
                        dqh = dqh + da_d * kk
                        dk = dk + da_d * qh
                        dv = dv + a_d * do
                        continue
                    e = rf if e is None else e * rf
                    rf = _roll8(fg, d)
                    em = jnp.where(rowpos >= d, e, 0.0)
                    ks, vs = 1.0 - rf, _roll8(v, d)
                    a_d = jnp.sum(qh * ks * em, axis=-1, keepdims=True)
                    da_d = jnp.sum(do * vs, axis=-1, keepdims=True) * em
                    dqh = dqh + da_d * ks
                    dk = dk + _roll8(da_d * qh, -d)
                    dv = dv + _roll8(a_d * do, -d)
                yield

                dg = _dot3(upper, qh * dqh - kk * dk) + carry_g
                dfg = dg / fg - dk
                lbacc[:, sl] += jnp.sum((dfg * (1.0 - sf)).reshape(CHUNK // SUBLANE, SUBLANE, LANE), axis=0)
                stash[slot, 0, rows, sl] = (dqh * (sq * (1.0 + q * (1.0 - sq)))).astype(BF16)
                stash[slot, 1, rows, sl] = (dfg * (1.0 - lb) * sf * (1.0 - sf)).astype(BF16)
                stash[slot, 2, rows, sl] = dv.astype(BF16)
                stash[slot, 3, rows, sl] = d_og.astype(BF16)

            def chunk(cr, carry):
                c = nc - 1 - cr
                rows = pl.ds(pl.multiple_of(c * CHUNK, CHUNK), CHUNK)
                running = [head(j, c, rows) for j in range(hpb)]
                while running:
                    running = [g for g in running if next(g, StopIteration) is not StopIteration]
                return carry

            lax.fori_loop(0, nc, chunk, 0)

            @pl.when(i == nb - 1)
            def _():
                dlb_ref[...] = jnp.sum(lbacc[...], axis=0, keepdims=True)
                dhg_ref[...] = jnp.sum(hgacc[...], axis=0, keepdims=True)

        compute()
        out_copy(slot, nb - 1 - i, h).start(LOCAL_COPY_PRIORITY)

        @pl.when(step == n_steps - 1)
        def _():
            out_copy(slot, 0, 0).wait()
            if n_steps >= 2:
                out_copy(1 - slot, 0, 0).wait()

    rev = lambda i: nb - 1 - i
    sec = lambda s: pl.BlockSpec((None, tb, bw), lambda h, i: (s, rev(i), h))
    return _call(
        body, (p3, p3, p3, p3, lb_logits, hg, o_pre, states, dcat, dp3), name="hgrn_bwd", grid=(nh // hpb, nb),
        in_specs=[sec(2), sec(3), sec(4), sec(5),
                  pl.BlockSpec((2, bw), lambda h, i: (0, h)),
                  pl.BlockSpec((1, bw), lambda h, i: (0, h)),
                  pl.BlockSpec((tb, bw), lambda h, i: (rev(i), h)),
                  pl.BlockSpec((hpb, nc, LANE, LANE), lambda h, i: (h, rev(i), 0, 0)),
                  pl.BlockSpec((tb, bw), lambda h, i: (rev(i), nh // hpb + h)), HBM],
        out_specs=[HBM,
                   pl.BlockSpec((1, bw), lambda h, i: (0, h)),
                   pl.BlockSpec((1, bw), lambda h, i: (0, h))],
        out_shape=[jax.ShapeDtypeStruct((n_sec, t, w), BF16), jax.ShapeDtypeStruct((1, w), F32),
                   jax.ShapeDtypeStruct((1, w), F32)],
        scratch_shapes=[pltpu.VMEM((hpb, LANE, LANE), F32), pltpu.VMEM((2, 4, tb, bw), BF16),
                        pltpu.VMEM((SUBLANE, bw), F32), pltpu.VMEM((SUBLANE, bw), F32),
                        pltpu.SemaphoreType.DMA((2,))],
        aliases={9: 0}, rider=rider)


def _place():
    x, y, c = lax.axis_index("x"), lax.axis_index("y"), lax.axis_index("c")
    chips = [(1 - x, y), (x, 1 - y), (1 - x, 1 - y)]
    return x, y, c, chips


def _rows(buf, px, py, pc, part=None):
    half = buf.shape[1] // 2
    if part is None:
        return buf.at[2 * px + py, pl.ds(pc * half, half)]
    lo, hi, n = part
    piece = half // n
    return buf.at[2 * px + py, pl.ds(pc * half + lo * piece, (hi - lo) * piece)]


def _rcopy(src, dst, send, recv, idx, to):
    return pltpu.make_async_remote_copy(src_ref=src, dst_ref=dst, send_sem=send.at[idx], recv_sem=recv.at[idx],
                                        device_id=to, device_id_type=MESH)


def _same(bufs):
    return [jax.ShapeDtypeStruct(b.shape, b.dtype) for b in bufs]


def _ride_gather_ici(bufs, parts=None):
    n = len(bufs)
    parts = parts or [None] * n

    def start(rin, rout, send, recv):
        x, y, c, chips = _place()
        for k in range(n):
            mine = _rows(rout[k], x, y, c, parts[k])
            for j, chip in enumerate(chips):
                _rcopy(mine, mine, send, recv, 3 * k + j, (*chip, c)).start()

    def finish(rin, rout, send, recv):
        x, y, c, chips = _place()
        for k in range(n):
            for j, chip in enumerate(chips):
                theirs = _rows(rout[k], *chip, c, parts[k])
                _rcopy(theirs, theirs, send, recv, 3 * k + j, (x, y, c)).wait_recv()
        for k in range(n):
            mine = _rows(rout[k], x, y, c, parts[k])
            for j in range(3):
                _rcopy(mine, mine, send, recv, 3 * k + j, (x, y, c)).wait_send()

    return _Rider(bufs, _same(bufs), {k: k for k in range(n)}, 3 * n, start, finish)


class _SemView:
    def __init__(self, ref, base):
        self.ref, self.base = ref, base

    @property
    def at(self):
        return self

    def __getitem__(self, idx):
        return self.ref.at[idx + self.base]


def _ride_both(a, b):
    nai, nao = len(a.ins), len(a.outs)

    def start(rin, rout, send, recv):
        a.start(rin[:nai], rout[:nao], send, recv)
        b.start(rin[nai:], rout[nao:], _SemView(send, a.n_sems), _SemView(recv, a.n_sems))

    def finish(rin, rout, send, recv):
        a.finish(rin[:nai], rout[:nao], send, recv)
        b.finish(rin[nai:], rout[nao:], _SemView(send, a.n_sems), _SemView(recv, a.n_sems))

    aliases = dict(a.aliases)
    aliases.update({nai + ri: nao + ro for ri, ro in b.aliases.items()})
    return _Rider(a.ins + b.ins, a.outs + b.outs, aliases, a.n_sems + b.n_sems, start, finish)


def _ride_gather_d2d(bufs):
    n = len(bufs)

    def start(rin, rout, send, recv):
        x, y, c, chips = _place()
        for k in range(n):
            for j, chip in enumerate(chips):
                got = _rows(rout[k], *chip, c)
                _rcopy(got, got, send, recv, 3 * k + j, (x, y, 1 - c)).start()

    def finish(rin, rout, send, recv):
        x, y, c, chips = _place()
        for k in range(n):
            for j, chip in enumerate(chips):
                theirs = _rows(rout[k], *chip, 1 - c)
                _rcopy(theirs, theirs, send, recv, 3 * k + j, (x, y, c)).wait_recv()
        for k in range(n):
            for j, chip in enumerate(chips):
                got = _rows(rout[k], *chip, c)
                _rcopy(got, got, send, recv, 3 * k + j, (x, y, c)).wait_send()

    return _Rider(bufs, _same(bufs), {k: k for k in range(n)}, 3 * n, start, finish)


def _ride_swap(grads):
    n = len(grads)

    def copy(k, rin, rout, send, recv):
        x, y, c, _ = _place()
        half = rin[k].shape[1] // 2
        return _rcopy(rin[k].at[:, pl.ds((1 - c) * half, half)], rout[k], send, recv, k, (x, y, 1 - c))

    def start(rin, rout, send, recv):
        for k in range(n):
            copy(k, rin, rout, send, recv).start()

    def finish(rin, rout, send, recv):
        for k in range(n):
            copy(k, rin, rout, send, recv).wait()

    outs = [jax.ShapeDtypeStruct((g.shape[0], g.shape[1] // 2, g.shape[2]), g.dtype) for g in grads]
    return _Rider(grads, outs, {}, n, start, finish)


def _ride_send_partials(parts, pieces=None, into=None):
    n = len(parts)
    pieces = pieces or [None] * n

    def cut(ref, k):
        if pieces[k] is None:
            return ref
        lo, hi, m = pieces[k]
        q = ref.shape[0] // m
        return ref.at[pl.ds(lo * q, (hi - lo) * q)]

    def copies(rin, rout, send, recv):
        x, y, c, chips = _place()
        return [_rcopy(cut(rin[k].at[2 * px + py], k), cut(rout[k].at[j], k), send, recv, 3 * k + j, (px, py, c))
                for k in range(n) for j, (px, py) in enumerate(chips)]

    def start(rin, rout, send, recv):
        for cp in copies(rin, rout, send, recv):
            cp.start()

    def finish(rin, rout, send, recv):
        for cp in copies(rin, rout, send, recv):
            cp.wait()

    if into is None:
        outs = [jax.ShapeDtypeStruct((3,) + p.shape[1:], p.dtype) for p in parts]
        return _Rider(parts, outs, {}, 3 * n, start, finish)
    return _Rider(list(parts) + list(into), _same(into), {n + k: k for k in range(n)}, 3 * n, start, finish)


def _ride_join(bufs):
    n = len(bufs)

    def half_of(buf, pc):
        half = buf.shape[0] // 2
        return buf.at[pl.ds(pc * half, half)]

    def start(rin, rout, send, recv):
        x, y, c, _ = _place()
        for k in range(n):
            mine = half_of(rout[k], c)
            _rcopy(mine, mine, send, recv, k, (x, y, 1 - c)).start()

    def finish(rin, rout, send, recv):
        x, y, c, _ = _place()
        for k in range(n):
            mine, theirs = half_of(rout[k], c), half_of(rout[k], 1 - c)
            _rcopy(mine, mine, send, recv, k, (x, y, c)).wait_send()
            _rcopy(theirs, theirs, send, recv, k, (x, y, c)).wait_recv()

    return _Rider(bufs, _same(bufs), {k: k for k in range(n)}, n, start, finish)


def _run(name, rider):
    def body(*refs):
        nri, nro = len(rider.ins), len(rider.outs)
        rin, rout = refs[:nri], refs[nri:nri + nro]
        send, recv = refs[nri + nro:]
        rider.start(rin, rout, send, recv)
        rider.finish(rin, rout, send, recv)

    return pl.pallas_call(
        body, name=name, in_specs=[HBM] * len(rider.ins), out_specs=[HBM] * len(rider.outs), out_shape=rider.outs,
        scratch_shapes=[pltpu.SemaphoreType.DMA((rider.n_sems,)), pltpu.SemaphoreType.DMA((rider.n_sems,))],
        input_output_aliases=rider.aliases,
    )(*rider.ins)


def _add_halves(name, g, other, c_idx):
    s, r, cols = g.shape
    half = r // 2
    tr = _div_tile(half, 16, 512)
    nb = half // tr

    def body(c_ref, g_ref, o_ref, q_ref):
        del c_ref
        q_ref[...] = (g_ref[...] + o_ref[...]).astype(BF16)

    return pl.pallas_call(
        body, name=name,
        grid_spec=pltpu.PrefetchScalarGridSpec(
            num_scalar_prefetch=1, grid=(s, nb),
            in_specs=[pl.BlockSpec((None, tr, cols), lambda k, i, c: (k, c[0] * nb + i, 0)),
                      pl.BlockSpec((None, tr, cols), lambda k, i, c: (k, i, 0))],
            out_specs=pl.BlockSpec((None, tr, cols), lambda k, i, c: (k, i, 0))),
        out_shape=jax.ShapeDtypeStruct((s, half, cols), BF16),
        compiler_params=_params("parallel", "parallel"),
    )(c_idx, g, other)


def _sum_partials(name, part, arrived, place_idx):
    _, half, cols = part.shape
    tr = _div_tile(half, 16, 512)
    nb = half // tr

    def body(s_ref, p_ref, a_ref, o_ref):
        del s_ref
        o_ref[...] = ((p_ref[...].astype(F32) + a_ref[0].astype(F32)) + a_ref[1].astype(F32)) + a_ref[2].astype(F32)

    return pl.pallas_call(
        body, name=name,
        grid_spec=pltpu.PrefetchScalarGridSpec(
            num_scalar_prefetch=1, grid=(nb,),
            in_specs=[pl.BlockSpec((None, tr, cols), lambda i, s: (s[0], i, 0)),
                      pl.BlockSpec((3, tr, cols), lambda i, s: (0, i, 0))],
            out_specs=pl.BlockSpec((tr, cols), lambda i, s: (s[1] * nb + i, 0))),
        out_shape=jax.ShapeDtypeStruct((2 * half, cols), F32),
        compiler_params=_params("parallel"),
    )(place_idx, part, arrived)


def _pack_small(name, wide_rows, ffn_rows, w, dff, n_wide, n_ffn):
    n_in = len(wide_rows) + len(ffn_rows)

    def body(*refs):
        ins, outs = refs[:n_in], refs[n_in:]
        p1 = outs[0]
        p1[...] = jnp.zeros_like(p1)
        row = 0
        for ref, (_, r, m) in zip(ins, wide_rows):
            if m == 1 and r % SUBLANE == 0 and row % SUBLANE == 0:
                p1[row:row + r, :] = ref[...]
                row += r
                continue
            for rr in range(r):
                for mm in range(m):
                    p1[row:row + 1, :] = ref[rr:rr + 1, mm * w:(mm + 1) * w]
                    row += 1
        if ffn_rows:
            p2 = outs[1]
            p2[...] = jnp.zeros_like(p2)
            row = 0
            for ref, arr in zip(ins[len(wide_rows):], ffn_rows):
                r = arr.shape[0]
                p2[row:row + r, :] = ref[...]
                row += r

    shapes = [jax.ShapeDtypeStruct((n_wide, w), F32)] + ([jax.ShapeDtypeStruct((n_ffn, dff), F32)] if ffn_rows else [])
    return pl.pallas_call(
        body, name=name, in_specs=[VMEM_FULL] * n_in, out_specs=[VMEM_FULL] * len(shapes), out_shape=shapes,
        compiler_params=pltpu.CompilerParams(vmem_limit_bytes=VMEM_LIMIT),
    )(*[a for a, _, _ in wide_rows], *ffn_rows)


def _ride_exchange8(packs):
    n = len(packs)

    def copies(rin, rout, send, recv):
        x, y, c, _ = _place()
        me = 4 * x + 2 * y + c
        out = []
        for a in range(n):
            for mask in range(1, 8):
                peer = (x ^ (mask >> 2), y ^ ((mask >> 1) & 1), c ^ (mask & 1))
                out.append(_rcopy(rin[a], rout[a].at[me], send, recv, 8 * a + mask, peer))
        own = [pltpu.make_async_copy(rin[a], rout[a].at[me], send.at[8 * a]) for a in range(n)]
        return out, own

    def start(rin, rout, send, recv):
        remote, own = copies(rin, rout, send, recv)
        for cp in remote:
            cp.start()
        for cp in own:
            cp.start(LOCAL_COPY_PRIORITY)

    def finish(rin, rout, send, recv):
        remote, own = copies(rin, rout, send, recv)
        for cp in remote + own:
            cp.wait()

    outs = [jax.ShapeDtypeStruct((8,) + p.shape, p.dtype) for p in packs]
    return _Rider(packs, outs, {}, 8 * n, start, finish)


def _sum_small(name, slots):
    def body(*refs):
        n = len(refs) // 2
        for r_ref, s_ref in zip(refs[:n], refs[n:]):
            tot = r_ref[0]
            for d in range(1, 8):
                tot = tot + r_ref[d]
            s_ref[...] = tot

    return pl.pallas_call(
        body, name=name, in_specs=[VMEM_FULL] * len(slots), out_specs=[VMEM_FULL] * len(slots),
        out_shape=[jax.ShapeDtypeStruct(s.shape[1:], s.dtype) for s in slots],
        compiler_params=pltpu.CompilerParams(vmem_limit_bytes=VMEM_LIMIT),
    )(*slots)


def _adamw(w, g, m, v):
    m2 = ADAM_B1 * m + (1.0 - ADAM_B1) * g
    v2 = ADAM_B2 * v + (1.0 - ADAM_B2) * (g * g)
    m_hat = m2 / (1.0 - ADAM_B1 ** ADAM_STEP)
    v_hat = v2 / (1.0 - ADAM_B2 ** ADAM_STEP)
    delta = -ADAM_LR * (m_hat / (jnp.sqrt(v_hat) + ADAM_EPS) + ADAM_WD * w)
    return delta, m2, v2


def _adam_big(name, w, g, m, v):
    r, c = w.shape
    tr = 128 if r % 128 == 0 else r

    def body(w_ref, g_ref, m_ref, v_ref, go_ref, d_ref, m2_ref, v2_ref):
        g = g_ref[...]
        go_ref[...] = g
        d_ref[...], m2_ref[...], v2_ref[...] = _adamw(w_ref[...], g, m_ref[...], v_ref[...])

    blk = pl.BlockSpec((tr, c), lambda i: (i, 0))
    return _call(
        body, (w, g, m, v), name=name, grid=(r // tr,), in_specs=[blk] * 4, out_specs=[blk] * 4,
        out_shape=[jax.ShapeDtypeStruct((r, c), F32)] * 4)


def _adam_small(s1, s2, s3, cw_g, fw_g, lb_logits, triples, layout, w):
    n = len(triples)

    def body(*refs):
        s1_ref, s2_ref, s3_ref, cw_ref, fw_ref, lbl_ref = refs[:6]
        prm = refs[6:6 + 3 * n]
        outs = refs[6 + 3 * n:]
        for p, lay in enumerate(layout):
            w_ref, m_ref, v_ref = prm[3 * p:3 * p + 3]
            g_ref, d_ref, m2_ref, v2_ref = outs[4 * p:4 * p + 4]
            if lay[0] in ("wide", "late"):
                _, row, r, pieces = lay
                src = s1_ref if lay[0] == "wide" else s3_ref
                for rr in range(r):
                    for mm in range(pieces):
                        g_ref[rr:rr + 1, mm * w:(mm + 1) * w] = src[row:row + 1, :]
                        row += 1
            elif lay[0] == "ffn":
                _, row, r = lay
                g_ref[...] = s2_ref[row:row + r, :]
            elif lay[0] == "cw":
                g_ref[...] = cw_ref[0:g_ref.shape[0], :]
            elif lay[0] == "fw":
                g_ref[...] = fw_ref[0:g_ref.shape[0], :]
            else:
                s0 = _lower_bound(lbl_ref)
                d0 = s1_ref[lay[1]:lay[1] + 1, :] * s0 * (1.0 - s0)
                g_ref[0:1, :] = d0
                g_ref[1:2, :] = -d0
            d_ref[...], m2_ref[...], v2_ref[...] = _adamw(w_ref[...], g_ref[...], m_ref[...], v_ref[...])

    flat = [a for tr in triples for a in tr]
    shapes = []
    for tr in triples:
        shapes.extend([jax.ShapeDtypeStruct(tr[0].shape, F32)] * 4)
    return pl.pallas_call(
        body, name="adam_small", in_specs=[VMEM_FULL] * (6 + 3 * n), out_specs=[VMEM_FULL] * (4 * n),
        out_shape=shapes, compiler_params=pltpu.CompilerParams(vmem_limit_bytes=VMEM_LIMIT),
    )(s1, s2, s3, cw_g, fw_g, lb_logits, *flat)


def _row_tile(t):
    return 512 if t % 512 == 0 and t >= 2048 else 128


def kernel(x, emb_ln_g, emb_ln_b, w_in, conv_w, conv_b, conv_norm_g, conv_norm_b, lb_logits, hgrn_norm_g, w_out, ln1_g, ln1_b, w_ffn_up, ffn_conv_w, ffn_conv_b, w_ffn_down, ln2_g, ln2_b, loss_target, m_emb_ln_g, m_emb_ln_b, m_w_in, m_conv_w, m_conv_b, m_conv_norm_g, m_conv_norm_b, m_lb_logits, m_hgrn_norm_g, m_w_out, m_ln1_g, m_ln1_b, m_w_ffn_up, m_ffn_conv_w, m_ffn_conv_b, m_w_ffn_down, m_ln2_g, m_ln2_b, v_emb_ln_g, v_emb_ln_b, v_w_in, v_conv_w, v_conv_b, v_conv_norm_g, v_conv_norm_b, v_lb_logits, v_hgrn_norm_g, v_w_out, v_ln1_g, v_ln1_b, v_w_ffn_up, v_ffn_conv_w, v_ffn_conv_b, v_w_ffn_down, v_ln2_g, v_ln2_b):
    depth = w_in.shape[0]
    assert depth == 1 and x.shape[0] == 1
    alpha = (2.0 * depth) ** 0.25
    t, d = x.shape[1], x.shape[2]
    w = d // 2
    dff = ffn_conv_b.shape[1]
    kc = conv_w.shape[1]
    assert w % (2 * LANE) == 0 and dff % (4 * LANE) == 0 and t % 128 == 0
    tm = _row_tile(t)
    tm2 = tm // 2
    tmm = 1024 if t % 1024 == 0 and t >= 2048 else tm
    cb = 2 * LANE
    cbf = 4 * LANE
    tb = tm
    nh = w // LANE
    hpb = 4 if nh % 4 == 0 else 2

    xi = lax.axis_index("x")
    yi = lax.axis_index("y")
    ci = lax.axis_index("c")
    chip = 2 * xi + yi
    c_idx = jnp.reshape(ci, (1,)).astype(jnp.int32)
    chip_idx = jnp.reshape(chip, (1,)).astype(jnp.int32)
    place_idx = jnp.stack([chip, ci]).astype(jnp.int32)

    x2 = x[0]
    tgt = loss_target[0]
    g0, b0 = emb_ln_g.reshape(1, d), emb_ln_b.reshape(1, d)
    w_in2, w_out2, w_up2, w_dn2 = w_in[0], w_out[0], w_ffn_up[0], w_ffn_down[0]
    cw2, fw2 = conv_w[0], ffn_conv_w[0]

    b_in = _place_shard(w_in2, "place_w_in", chip_idx, BF16)
    b_out = _place_shard(w_out2, "place_w_out", chip_idx, BF16)
    b_up = _place_shard(w_up2, "place_w_up", chip_idx, BF16)
    b_dn = _place_shard(w_dn2, "place_w_down", chip_idx, BF16)
    b_cw = _place_shard(_pad_rows(cw2), "place_conv_w", chip_idx, F32)
    b_fw = _place_shard(_pad_rows(fw2), "place_ffn_conv_w", chip_idx, F32)
    h0b, (b_in,) = _ln0(x2, g0, b0, tm, rider=_ride_gather_ici([b_in], [(0, 1, 8)]))
    first = _run("gather_first_ici", _ride_gather_ici([b_in, b_cw, b_fw], [(1, 8, 8), None, None]))
    w_in3, cw_full3, fw_full3 = _run("gather_first_d2d", _ride_gather_d2d(first))
    cw_full = _unshard_cols(cw_full3)[:kc]
    fw_full = _unshard_cols(fw_full3)[:fw2.shape[0]]

    p3, (b_out, b_up) = _proj("in_proj", h0b, w_in3, 6, 2 * tmm if t % (2 * tmm) == 0 else tmm, w // 2,
                              rider=_ride_gather_ici([b_out, b_up], [None, (0, 1, 4)]))
    (cat, u1), (w_out3, b_up) = _conv_fwd(
        p3, cw_full, conv_b, conv_norm_g, conv_norm_b, tm2, cb,
        rider=_ride_both(_ride_gather_d2d([b_out]), _ride_gather_ici([b_up], [(1, 2, 4)])))
    w_out_full = w_out3.reshape(d, d)
    (cat, o_pre, states), got = _hgrn_fwd(p3, lb_logits, hgrn_norm_g, cat, tb, hpb,
                                          rider=_ride_gather_ici([b_up], [(2, 4, 4)]))
    (xhat1, h1b, rstd1), (w_up3,) = _mix_ln1(cat, w_out_full, x2, g0, b0, ln1_g, ln1_b, alpha, tm2,
                                             rider=_ride_gather_d2d(got))
    hh3, got = _proj("ffn_up", h1b, w_up3, 2, tmm, dff // 4, rider=_ride_gather_ici([b_dn]))
    act, (w_dn3,) = _ffn_act_fwd(hh3, fw_full, ffn_conv_b, tm, cbf, rider=_ride_gather_d2d(got))
    ks = dff // N_CHIPS
    ffn = _wgrad("ffn_down", act, w_dn3, (t, d), (t // tmm, 1, N_CHIPS),
                 pl.BlockSpec((tmm, ks), lambda i, j, k: (i, k)),
                 pl.BlockSpec((None, ks, d), lambda i, j, k: (k, 0, 0)),
                 pl.BlockSpec((tmm, d), lambda i, j, k: (i, 0)), dot=_dot)
    dz2, dz2b, dg2, db2, loss_row = _ln2_loss(ffn, xhat1, tgt, ln1_g, ln1_b, ln2_g, ln2_b, alpha, tm2)

    dact = _proj_t("ffn_down_t", dz2b, w_dn3.reshape(dff, d), tmm, ks)
    dhh3, dfw, dfb = _ffn_act_bwd(dact, hh3, fw_full, ffn_conv_b, tm, cbf)
    tt = 2 * tmm if t % (2 * tmm) == 0 else tmm
    d_w_dn = _wgrad("wgrad_down", act, dz2b, (N_CHIPS, ks, d), (N_CHIPS, 2, t // tt),
                    pl.BlockSpec((tt, ks), lambda s, j, k: (k, s)),
                    pl.BlockSpec((tt, d // 2), lambda s, j, k: (k, j)),
                    pl.BlockSpec((None, ks, d // 2), lambda s, j, k: (s, 0, j)))
    wu = 2 * dff // N_CHIPS
    tnu = wu // 2
    per_sec_u = dff // tnu
    pre1, (arr_dn,) = _wgrad(
        "up_t", dhh3, w_up3, (t, d), (t // tmm, 1, 2 * N_CHIPS),
        pl.BlockSpec((None, tmm, tnu), lambda i, j, k: (k // per_sec_u, i, k % per_sec_u)),
        pl.BlockSpec((None, d, tnu), lambda i, j, k: (k // 2, 0, k % 2)),
        pl.BlockSpec((tmm, d), lambda i, j, k: (i, 0)), dot=_dot_nt, rider=_ride_swap([d_w_dn]))
    dz1, dz1b, dg1, db1 = _ln1_bwd(pre1, dz2, xhat1, rstd1, ln1_g, alpha, tm2)
    part_dn = _add_halves("add_halves_w_down", d_w_dn, arr_dn, c_idx)
    d_w_up, (land_dn,) = _wgrad(
        "wgrad_up", h1b, dhh3, (N_CHIPS, d, wu), (N_CHIPS, 2, 2, t // tt),
        pl.BlockSpec((tt, d // 2), lambda s, r, j, k: (k, r)),
        pl.BlockSpec((None, tt, tnu), lambda s, r, j, k: ((2 * s + j) // per_sec_u, k, (2 * s + j) % per_sec_u)),
        pl.BlockSpec((None, d // 2, tnu), lambda s, r, j, k: (s, r, j)), rider=_ride_send_partials([part_dn]))
    dcat = _proj_t("out_proj_t", dz1b, w_out_full, tmm, d // 2)
    d_w_out = _wgrad("wgrad_out", cat, dz1b, (d, d), (2, 2, t // tt),
                     pl.BlockSpec((tt, d // 2), lambda r, j, k: (k, r)),
                     pl.BlockSpec((tt, d // 2), lambda r, j, k: (k, j)),
                     pl.BlockSpec((d // 2, d // 2), lambda r, j, k: (r, j))).reshape(N_CHIPS, d // N_CHIPS, d)
    du1, dcng, dcnb = _conv_norm_bwd(dcat, u1, conv_norm_g, conv_norm_b, tm)
    (dp3, dcw, dcb), (arr_up, arr_out) = _conv_bwd(du1, p3, cw_full, tm2, cb, rider=_ride_swap([d_w_up, d_w_out]))
    part_up = _add_halves("add_halves_w_up", d_w_up, arr_up, c_idx)
    part_out = _add_halves("add_halves_w_out", d_w_out, arr_out, c_idx)
    (dp3, dlb, dhg), (land_up, land_out) = _hgrn_bwd(
        p3, lb_logits, hgrn_norm_g, o_pre, states, dcat, dp3, tb, hpb,
        rider=_ride_send_partials([part_up, part_out], [(0, 3, 4), None]))
    kpad = dcw.shape[0]
    wide = [(dcw, kpad, 1), (dg1, 1, 2), (db1, 1, 2), (dg2, 1, 2), (db2, 1, 2),
            (dcb, 1, 1), (dcng, 1, 1), (dcnb, 1, 1), (dlb, 1, 1), (dhg, 1, 1)]
    n_wide = -(-sum(r * m for _, r, m in wide) // SUBLANE) * SUBLANE
    packs = _pack_small("pack_small", wide, [dfw, dfb], w, dff, n_wide, 2 * SUBLANE)
    d_w_in, (land_up, slots1, slots2) = _wgrad_in(
        h0b, dp3, N_CHIPS, tt,
        rider=_ride_both(_ride_send_partials([part_up], [(3, 4, 4)], into=[land_up]), _ride_exchange8(packs)))
    s1, s2 = _sum_small("sum_small", [slots1, slots2])
    (arr_in,) = _run("swap_w_in", _ride_swap([d_w_in]))
    part_in = _add_halves("add_halves_w_in", d_w_in, arr_in, c_idx)
    h_out, h_up, h_dn = [
        _sum_partials("sum_partials_" + nm, p, a, place_idx)
        for nm, p, a in (("w_out", part_out, land_out), ("w_up", part_up, land_up), ("w_down", part_dn, land_dn))]
    pre0, (land_in, g_w_out, g_w_up, g_w_dn) = _in_t(
        dp3, w_in3, tmm, rider=_ride_both(_ride_send_partials([part_in]), _ride_join([h_out, h_up, h_dn])))
    dx, dg0, db0 = _ln0_bwd(pre0, dz1, x2, g0, alpha, tm2)
    h_in = _sum_partials("sum_partials_w_in", part_in, land_in, place_idx)
    (g_w_in,) = _run("join_w_in", _ride_join([h_in]))

    late = _pack_small("pack_late", [(dg0, 1, 2), (db0, 1, 2)], [], w, dff, SUBLANE, 0)
    (s3,) = _sum_small("sum_late", _run("exchange_late", _ride_exchange8(late)))
    cw_g = lax.dynamic_slice_in_dim(s1[0:kpad], chip * (w // N_CHIPS), w // N_CHIPS, axis=1)
    fw_g = lax.dynamic_slice_in_dim(s2[0:SUBLANE], chip * (dff // N_CHIPS), dff // N_CHIPS, axis=1)

    small = [
        (g0, m_emb_ln_g.reshape(1, d), v_emb_ln_g.reshape(1, d)), (b0, m_emb_ln_b.reshape(1, d), v_emb_ln_b.reshape(1, d)),
        (cw2, m_conv_w[0], v_conv_w[0]), (conv_b, m_conv_b, v_conv_b),
        (conv_norm_g, m_conv_norm_g, v_conv_norm_g), (conv_norm_b, m_conv_norm_b, v_conv_norm_b),
        (lb_logits, m_lb_logits, v_lb_logits), (hgrn_norm_g, m_hgrn_norm_g, v_hgrn_norm_g),
        (ln1_g, m_ln1_g, v_ln1_g), (ln1_b, m_ln1_b, v_ln1_b),
        (fw2, m_ffn_conv_w[0], v_ffn_conv_w[0]), (ffn_conv_b, m_ffn_conv_b, v_ffn_conv_b),
        (ln2_g, m_ln2_g, v_ln2_g), (ln2_b, m_ln2_b, v_ln2_b),
    ]
    r0 = kpad
    layout = [("late", 0, 1, 2), ("late", 2, 1, 2), ("cw",), ("wide", r0 + 8, 1, 1), ("wide", r0 + 9, 1, 1),
              ("wide", r0 + 10, 1, 1), ("lb", r0 + 11), ("wide", r0 + 12, 1, 1), ("wide", r0, 1, 2),
              ("wide", r0 + 2, 1, 2), ("fw",), ("ffn", SUBLANE, 1), ("wide", r0 + 4, 1, 2), ("wide", r0 + 6, 1, 2)]
    so = _adam_small(s1, s2, s3, cw_g, fw_g, lb_logits, small, layout, w)
    sm = {nm: so[4 * i:4 * i + 4] for i, nm in enumerate(
        ["emb_ln_g", "emb_ln_b", "conv_w", "conv_b", "conv_norm_g", "conv_norm_b", "lb_logits", "hgrn_norm_g",
         "ln1_g", "ln1_b", "ffn_conv_w", "ffn_conv_b", "ln2_g", "ln2_b"])}
    bigs = {}
    for nm, wt, g, m, v in (("w_in", w_in2, g_w_in, m_w_in[0], v_w_in[0]), ("w_out", w_out2, g_w_out, m_w_out[0], v_w_out[0]),
                            ("w_ffn_up", w_up2, g_w_up, m_w_ffn_up[0], v_w_ffn_up[0]),
                            ("w_ffn_down", w_dn2, g_w_dn, m_w_ffn_down[0], v_w_ffn_down[0])):
        bigs[nm] = tuple(_adam_big("adam_" + nm, wt, g, m, v))

    loss = lax.psum(loss_row[0, 0], ("x", "y", "c"))

    order = ["emb_ln_g", "emb_ln_b", "w_in", "conv_w", "conv_b", "conv_norm_g", "conv_norm_b", "lb_logits",
             "hgrn_norm_g", "w_out", "ln1_g", "ln1_b", "w_ffn_up", "ffn_conv_w", "ffn_conv_b", "w_ffn_down",
             "ln2_g", "ln2_b"]
    shapes = dict(emb_ln_g=emb_ln_g.shape, emb_ln_b=emb_ln_b.shape, w_in=w_in.shape, conv_w=conv_w.shape,
                  w_out=w_out.shape, w_ffn_up=w_ffn_up.shape, ffn_conv_w=ffn_conv_w.shape, w_ffn_down=w_ffn_down.shape)
    outs = [loss, dx.reshape(x.shape)]
    for which in range(4):
        for nm in order:
            a = bigs[nm][which] if nm in bigs else sm[nm][which]
            outs.append(a.reshape(shapes[nm]) if nm in shapes else a)
    return tuple(outs)


def _pad_rows(a):
    k = a.shape[0]
    kp = -(-k // 16) * 16
    return jnp.pad(a, ((0, kp - k), (0, 0)))


def _unshard_cols(a3):
    s, k, c = a3.shape
    return jnp.transpose(a3, (1, 0, 2)).reshape(k, s * c)
```
